```python
import jax, jax.numpy as jnp
from jax import lax
import numpy as np

D_MODEL = 1024
BATCH = 8
SEQ = 8192
DEPTH = 4

N_META = 16
EPS = 1e-6
NEG_INF = -1e30
FOX_HEADS = 8
FOX_HEAD_DIM = 64
FOX_BLOCK = 128
FOX_W = FOX_HEADS * FOX_HEAD_DIM
GDN_HEADS = 8
GDN_HEAD_DIM = 128
GDN_CHUNK = 64
GDN_CONV = 4
GDN_W = GDN_HEADS * GDN_HEAD_DIM
N_BRANCH = 2
D_FF = 2816
FFN_CONV = 3
IN_SIZES = [FOX_W, FOX_W, FOX_W, FOX_HEADS,
            GDN_W, GDN_W, GDN_W, GDN_HEADS, GDN_HEADS,
            GDN_W,
            N_BRANCH * D_MODEL]
D_IN = int(sum(IN_SIZES))
IN_SPLIT = [int(c) for c in np.cumsum(IN_SIZES)[:-1]]

kernel_name = "hybrid_fox_gdn_convffn_trunk"


def rmsnorm(x, g):
    xf = x.astype(jnp.float32)
    y = xf * lax.rsqrt(jnp.mean(xf * xf, axis=-1, keepdims=True) + EPS)
    return (y * g.astype(jnp.float32)).astype(x.dtype)


def l2norm(x):
    return x * lax.rsqrt(jnp.sum(x * x, axis=-1, keepdims=True) + EPS)


def causal_dwconv(x, w):
    K, C = w.shape
    return lax.conv_general_dilated(x, w[:, None, :].astype(x.dtype), window_strides=(1,),
                                    padding=[(K - 1, 0)], dimension_numbers=('NWC', 'WIO', 'NWC'),
                                    feature_group_count=C)


def fox_attention(q, k, v, log_f):
    B_, L, H, Dh = q.shape
    pad = (-L) % FOX_BLOCK
    Lp = L + pad
    nb = Lp // FOX_BLOCK
    p4 = ((0, 0), (pad, 0), (0, 0), (0, 0))
    qp, kp, vp = jnp.pad(q, p4), jnp.pad(k, p4), jnp.pad(v, p4)
    F = jnp.cumsum(jnp.pad(log_f, ((0, 0), (pad, 0), (0, 0))), axis=1)
    Fk = F.transpose(0, 2, 1)
    kpos = jnp.arange(Lp)
    key_valid = kpos >= pad
    qb = qp.reshape(B_, nb, FOX_BLOCK, H, Dh).transpose(1, 0, 2, 3, 4)
    Fq = Fk.reshape(B_, H, nb, FOX_BLOCK).transpose(2, 0, 1, 3)
    scale = Dh ** -0.5

    def block(args):
        i, q_blk, F_blk = args
        qpos = i * FOX_BLOCK + jnp.arange(FOX_BLOCK)
        s = jnp.einsum('bqhd,bkhd->bhqk', q_blk, kp, preferred_element_type=jnp.float32) * scale
        s = s + F_blk[..., :, None] - Fk[:, :, None, :]
        mask = (kpos[None, :] <= qpos[:, None]) & key_valid[None, :]
        p = jax.nn.softmax(jnp.where(mask, s, NEG_INF), axis=-1)
        return jnp.einsum('bhqk,bkhd->bqhd', p.astype(vp.dtype), vp)

    out = lax.map(block, (jnp.arange(nb), qb, Fq))
    return out.transpose(1, 0, 2, 3, 4).reshape(B_, Lp, H, Dh)[:, pad:]


def gated_delta_rule(q, k, v, beta, g):
    B_, L, H, Dk = q.shape
    Dv = v.shape[-1]
    C = GDN_CHUNK
    pad = (-L) % C
    Lp = L + pad
    N = Lp // C
    p4 = ((0, 0), (pad, 0), (0, 0), (0, 0))
    p3 = ((0, 0), (pad, 0), (0, 0))

    def chunks4(t):
        return jnp.pad(t, p4).reshape(B_, N, C, H, t.shape[-1]).transpose(0, 3, 1, 2, 4)

    def chunks3(t):
        return jnp.pad(t, p3).reshape(B_, N, C, H).transpose(0, 3, 1, 2)

    qc, kc, vc = chunks4(q), chunks4(k), chunks4(v)
    bc, gc = chunks3(beta), chunks3(g)
    G = jnp.cumsum(gc, axis=-1)
    idx = jnp.arange(C)
    incl = idx[:, None] >= idx[None, :]
    strict = idx[:, None] > idx[None, :]
    diff = G[..., :, None] - G[..., None, :]
    decay_incl = jnp.exp(jnp.where(incl, diff, -jnp.inf))
    decay_strict = jnp.where(strict, decay_incl, 0.0)
    kb = kc * bc[..., None]
    M = jnp.einsum('bhncd,bhnsd->bhncs', kb, kc) * decay_strict
    A = M + jnp.eye(C, dtype=M.dtype)
    u_hat = lax.linalg.triangular_solve(A, vc * bc[..., None], left_side=True, lower=True, unit_diagonal=True)
    w = lax.linalg.triangular_solve(A, kb * jnp.exp(G)[..., None], left_side=True, lower=True, unit_diagonal=True)
    qg = qc * jnp.exp(G)[..., None]
    aqk = jnp.einsum('bhncd,bhnsd->bhncs', qc, kc) * decay_incl
    kd = kc * jnp.exp(G[..., -1:] - G)[..., None]
    gC = jnp.exp(G[..., -1])

    def to_front(t):
        return jnp.moveaxis(t, 2, 0)

    def step(S, xs):
        u_c, w_c, qg_c, a_c, kd_c, g_c = xs
        U = u_c - jnp.einsum('bhck,bhkv->bhcv', w_c, S)
        o = jnp.einsum('bhck,bhkv->bhcv', qg_c, S) + jnp.einsum('bhcs,bhsv->bhcv', a_c, U)
        S = S * g_c[..., None, None] + jnp.einsum('bhck,bhcv->bhkv', kd_c, U)
        return S, o

    S0 = jnp.zeros((B_, H, Dk, Dv), jnp.float32)
    _, o = lax.scan(step, S0, (to_front(u_hat), to_front(w), to_front(qg), to_front(aqk), to_front(kd), to_front(gC)))
    return o.transpose(1, 0, 3, 2, 4).reshape(B_, Lp, H, Dv)[:, pad:]


def _fwd_setup_inputs(seed: int = 0) -> dict:
    key = jax.random.key(seed)
    ks = jax.random.split(key, 20)
    f32 = jnp.float32

    def nrm(k, shape, scale):
        return jax.random.normal(k, shape, f32) * scale

    dt = jnp.exp(jax.random.uniform(ks[7], (DEPTH, GDN_HEADS), f32, np.log(1e-3), np.log(1e-1)))
    return {
        "x": nrm(ks[0], (BATCH, SEQ, D_MODEL), 1.0),
        "meta_tokens": nrm(ks[1], (N_META, D_MODEL), 1.0),
        "norm1_g": 1.0 + nrm(ks[2], (DEPTH, D_MODEL), 0.02),
        "w_in": nrm(ks[3], (DEPTH, D_MODEL, D_IN), D_MODEL ** -0.5),
        "fox_f_bias": 2.0 + nrm(ks[4], (DEPTH, FOX_HEADS), 0.5),
        "fox_q_norm_g": 1.0 + nrm(ks[5], (DEPTH, FOX_HEAD_DIM), 0.02),
        "fox_k_norm_g": 1.0 + nrm(ks[6], (DEPTH, FOX_HEAD_DIM), 0.02),
        "gdn_conv_w": nrm(ks[8], (DEPTH, GDN_CONV, 3 * GDN_W), GDN_CONV ** -0.5),
        "gdn_a_log": jnp.log(jax.random.uniform(ks[9], (DEPTH, GDN_HEADS), f32, 1.0, 16.0)),
        "gdn_dt_bias": jnp.log(jnp.expm1(dt)),
        "gdn_norm_g": 1.0 + nrm(ks[10], (DEPTH, GDN_HEAD_DIM), 0.02),
        "w_branch_a": nrm(ks[11], (DEPTH, FOX_W, D_MODEL), FOX_W ** -0.5),
        "w_branch_b": nrm(ks[12], (DEPTH, GDN_W, D_MODEL), GDN_W ** -0.5),
        "w_out": nrm(ks[13], (DEPTH, D_MODEL, D_MODEL), D_MODEL ** -0.5),
        "norm2_g": 1.0 + nrm(ks[14], (DEPTH, D_MODEL), 0.02),
        "w_up": nrm(ks[15], (DEPTH, D_MODEL, 2 * D_FF), D_MODEL ** -0.5),
        "ffn_conv_w": nrm(ks[16], (DEPTH, FFN_CONV, 2 * D_FF), FFN_CONV ** -0.5),
        "w_down": nrm(ks[17], (DEPTH, D_FF, D_MODEL), D_FF ** -0.5),
    }


def _fwd_reference(x, meta_tokens, norm1_g, w_in, fox_f_bias, fox_q_norm_g, fox_k_norm_g, gdn_conv_w,
              gdn_a_log, gdn_dt_bias, gdn_norm_g, w_branch_a, w_branch_b, w_out, norm2_g, w_up,
              ffn_conv_w, w_down):
    B_, S_, D = x.shape
    meta = jnp.broadcast_to(meta_tokens.astype(x.dtype)[None], (B_, N_META, D))
    h_res = jnp.concatenate([meta, x], axis=1)
    L = h_res.shape[1]
    f32 = jnp.float32
    for l in range(DEPTH):
        h = rmsnorm(h_res, norm1_g[l])
        proj = h @ w_in[l]
        (fq, fk, fv, f_logit, gq, gk, gv, b_logit, a_logit, gz, gate_logit) = jnp.split(proj, IN_SPLIT, axis=-1)

        fq = rmsnorm(fq.reshape(B_, L, FOX_HEADS, FOX_HEAD_DIM), fox_q_norm_g[l])
        fk = rmsnorm(fk.reshape(B_, L, FOX_HEADS, FOX_HEAD_DIM), fox_k_norm_g[l])
        fv = fv.reshape(B_, L, FOX_HEADS, FOX_HEAD_DIM)
        log_f = jax.nn.log_sigmoid(f_logit.astype(f32) + fox_f_bias[l].astype(f32))
        y_a = fox_attention(fq, fk, fv, log_f).reshape(B_, L, FOX_W) @ w_branch_a[l]

        qkv = jax.nn.silu(causal_dwconv(jnp.concatenate([gq, gk, gv], axis=-1), gdn_conv_w[l]))
        gq, gk, gv = jnp.split(qkv.astype(f32), 3, axis=-1)
        gq = l2norm(gq.reshape(B_, L, GDN_HEADS, GDN_HEAD_DIM)) * (GDN_HEAD_DIM ** -0.5)
        gk = l2norm(gk.reshape(B_, L, GDN_HEADS, GDN_HEAD_DIM))
        gv = gv.reshape(B_, L, GDN_HEADS, GDN_HEAD_DIM)
        beta = jax.nn.sigmoid(b_logit.astype(f32))
        g = -jnp.exp(gdn_a_log[l].astype(f32)) * jax.nn.softplus(a_logit.astype(f32) + gdn_dt_bias[l].astype(f32))
        o_b = gated_delta_rule(gq, gk, gv, beta, g)
        o_b = rmsnorm(o_b, gdn_norm_g[l]).astype(x.dtype) * jax.nn.silu(gz.reshape(B_, L, GDN_HEADS, GDN_HEAD_DIM))
        y_b = o_b.reshape(B_, L, GDN_W) @ w_branch_b[l]

        gates = jax.nn.sigmoid(gate_logit).reshape(B_, L, N_BRANCH, D)
        mixed = gates[:, :, 0] * y_a + gates[:, :, 1] * y_b
        h_res = h_res + mixed @ w_out[l]

        h = rmsnorm(h_res, norm2_g[l])
        up = causal_dwconv(h @ w_up[l], ffn_conv_w[l])
        u_gate, u_val = jnp.split(up, 2, axis=-1)
        h_res = h_res + (jax.nn.silu(u_gate) * u_val) @ w_down[l]
    return h_res[:, N_META:]


import jax as _jax
import jax.numpy as _jnp

TWIN_FORMAT = 'train_step'
FWD_PARAMS = ['x', 'meta_tokens', 'norm1_g', 'w_in', 'fox_f_bias', 'fox_q_norm_g', 'fox_k_norm_g', 'gdn_conv_w', 'gdn_a_log', 'gdn_dt_bias', 'gdn_norm_g', 'w_branch_a', 'w_branch_b', 'w_out', 'norm2_g', 'w_up', 'ffn_conv_w', 'w_down']
TWIN_WEIGHTS = ['meta_tokens', 'norm1_g', 'w_in', 'fox_f_bias', 'fox_q_norm_g', 'fox_k_norm_g', 'gdn_conv_w', 'gdn_a_log', 'gdn_dt_bias', 'gdn_norm_g', 'w_branch_a', 'w_branch_b', 'w_out', 'norm2_g', 'w_up', 'ffn_conv_w', 'w_down']
TWIN_DIFF_INPUT = 'x'
TWIN_INPUTS = ['x', 'meta_tokens', 'norm1_g', 'w_in', 'fox_f_bias', 'fox_q_norm_g', 'fox_k_norm_g', 'gdn_conv_w', 'gdn_a_log', 'gdn_dt_bias', 'gdn_norm_g', 'w_branch_a', 'w_branch_b', 'w_out', 'norm2_g', 'w_up', 'ffn_conv_w', 'w_down', 'loss_target', 'm_meta_tokens', 'm_norm1_g', 'm_w_in', 'm_fox_f_bias', 'm_fox_q_norm_g', 'm_fox_k_norm_g', 'm_gdn_conv_w', 'm_gdn_a_log', 'm_gdn_dt_bias', 'm_gdn_norm_g', 'm_w_branch_a', 'm_w_branch_b', 'm_w_out', 'm_norm2_g', 'm_w_up', 'm_ffn_conv_w', 'm_w_down', 'v_meta_tokens', 'v_norm1_g', 'v_w_in', 'v_fox_f_bias', 'v_fox_q_norm_g', 'v_fox_k_norm_g', 'v_gdn_conv_w', 'v_gdn_a_log', 'v_gdn_dt_bias', 'v_gdn_norm_g', 'v_w_branch_a', 'v_w_branch_b', 'v_w_out', 'v_norm2_g', 'v_w_up', 'v_ffn_conv_w', 'v_w_down']
TWIN_OUTPUTS = ['loss', 'grad_x', 'grad_meta_tokens', 'grad_norm1_g', 'grad_w_in', 'grad_fox_f_bias', 'grad_fox_q_norm_g', 'grad_fox_k_norm_g', 'grad_gdn_conv_w', 'grad_gdn_a_log', 'grad_gdn_dt_bias', 'grad_gdn_norm_g', 'grad_w_branch_a', 'grad_w_branch_b', 'grad_w_out', 'grad_norm2_g', 'grad_w_up', 'grad_ffn_conv_w', 'grad_w_down', 'delta_meta_tokens', 'delta_norm1_g', 'delta_w_in', 'delta_fox_f_bias', 'delta_fox_q_norm_g', 'delta_fox_k_norm_g', 'delta_gdn_conv_w', 'delta_gdn_a_log', 'delta_gdn_dt_bias', 'delta_gdn_norm_g', 'delta_w_branch_a', 'delta_w_branch_b', 'delta_w_out', 'delta_norm2_g', 'delta_w_up', 'delta_ffn_conv_w', 'delta_w_down', 'new_m_meta_tokens', 'new_m_norm1_g', 'new_m_w_in', 'new_m_fox_f_bias', 'new_m_fox_q_norm_g', 'new_m_fox_k_norm_g', 'new_m_gdn_conv_w', 'new_m_gdn_a_log', 'new_m_gdn_dt_bias', 'new_m_gdn_norm_g', 'new_m_w_branch_a', 'new_m_w_branch_b', 'new_m_w_out', 'new_m_norm2_g', 'new_m_w_up', 'new_m_ffn_conv_w', 'new_m_w_down', 'new_v_meta_tokens', 'new_v_norm1_g', 'new_v_w_in', 'new_v_fox_f_bias', 'new_v_fox_q_norm_g', 'new_v_fox_k_norm_g', 'new_v_gdn_conv_w', 'new_v_gdn_a_log', 'new_v_gdn_dt_bias', 'new_v_gdn_norm_g', 'new_v_w_branch_a', 'new_v_w_branch_b', 'new_v_w_out', 'new_v_norm2_g', 'new_v_w_up', 'new_v_ffn_conv_w', 'new_v_w_down']
TWIN_LEAF_KINDS = {'loss': 'loss', 'grad_x': 'grad_x', 'grad_meta_tokens': 'grad_w', 'grad_norm1_g': 'grad_w', 'grad_w_in': 'grad_w', 'grad_fox_f_bias': 'grad_w', 'grad_fox_q_norm_g': 'grad_w', 'grad_fox_k_norm_g': 'grad_w', 'grad_gdn_conv_w': 'grad_w', 'grad_gdn_a_log': 'grad_w', 'grad_gdn_dt_bias': 'grad_w', 'grad_gdn_norm_g': 'grad_w', 'grad_w_branch_a': 'grad_w', 'grad_w_branch_b': 'grad_w', 'grad_w_out': 'grad_w', 'grad_norm2_g': 'grad_w', 'grad_w_up': 'grad_w', 'grad_ffn_conv_w': 'grad_w', 'grad_w_down': 'grad_w', 'delta_meta_tokens': 'delta_w', 'delta_norm1_g': 'delta_w', 'delta_w_in': 'delta_w', 'delta_fox_f_bias': 'delta_w', 'delta_fox_q_norm_g': 'delta_w', 'delta_fox_k_norm_g': 'delta_w', 'delta_gdn_conv_w': 'delta_w', 'delta_gdn_a_log': 'delta_w', 'delta_gdn_dt_bias': 'delta_w', 'delta_gdn_norm_g': 'delta_w', 'delta_w_branch_a': 'delta_w', 'delta_w_branch_b': 'delta_w', 'delta_w_out': 'delta_w', 'delta_norm2_g': 'delta_w', 'delta_w_up': 'delta_w', 'delta_ffn_conv_w': 'delta_w', 'delta_w_down': 'delta_w', 'new_m_meta_tokens': 'new_m', 'new_m_norm1_g': 'new_m', 'new_m_w_in': 'new_m', 'new_m_fox_f_bias': 'new_m', 'new_m_fox_q_norm_g': 'new_m', 'new_m_fox_k_norm_g': 'new_m', 'new_m_gdn_conv_w': 'new_m', 'new_m_gdn_a_log': 'new_m', 'new_m_gdn_dt_bias': 'new_m', 'new_m_gdn_norm_g': 'new_m', 'new_m_w_branch_a': 'new_m', 'new_m_w_branch_b': 'new_m', 'new_m_w_out': 'new_m', 'new_m_norm2_g': 'new_m', 'new_m_w_up': 'new_m', 'new_m_ffn_conv_w': 'new_m', 'new_m_w_down': 'new_m', 'new_v_meta_tokens': 'new_v', 'new_v_norm1_g': 'new_v', 'new_v_w_in': 'new_v', 'new_v_fox_f_bias': 'new_v', 'new_v_fox_q_norm_g': 'new_v', 'new_v_fox_k_norm_g': 'new_v', 'new_v_gdn_conv_w': 'new_v', 'new_v_gdn_a_log': 'new_v', 'new_v_gdn_dt_bias': 'new_v', 'new_v_gdn_norm_g': 'new_v', 'new_v_w_branch_a': 'new_v', 'new_v_w_branch_b': 'new_v', 'new_v_w_out': 'new_v', 'new_v_norm2_g': 'new_v', 'new_v_w_up': 'new_v', 'new_v_ffn_conv_w': 'new_v', 'new_v_w_down': 'new_v'}


def _forward(args):
    return _fwd_reference(*[args[k] for k in FWD_PARAMS])


def _output_shape():
    out = _jax.eval_shape(lambda: _forward(_fwd_setup_inputs(0)))
    return out.shape, out.dtype

N_MICROBATCH = 1
ADAM_LR = 0.001
ADAM_B1 = 0.9
ADAM_B2 = 0.999
ADAM_EPS = 1e-08
ADAM_WD = 0.01
ADAM_STEP = 10
PER_EXAMPLE_BATCH_AXIS = {'x': 0, 'loss_target': 0}
SHARED_INPUTS = []
_WEIGHT_DTYPES = {'meta_tokens': _jnp.float32, 'norm1_g': _jnp.float32, 'w_in': _jnp.float32, 'fox_f_bias': _jnp.float32, 'fox_q_norm_g': _jnp.float32, 'fox_k_norm_g': _jnp.float32, 'gdn_conv_w': _jnp.float32, 'gdn_a_log': _jnp.float32, 'gdn_dt_bias': _jnp.float32, 'gdn_norm_g': _jnp.float32, 'w_branch_a': _jnp.float32, 'w_branch_b': _jnp.float32, 'w_out': _jnp.float32, 'norm2_g': _jnp.float32, 'w_up': _jnp.float32, 'ffn_conv_w': _jnp.float32, 'w_down': _jnp.float32}
MOMENT_SCALE = {'meta_tokens': 7.938389e-02, 'norm1_g': 1.177739e+01, 'w_in': 3.666062e-01, 'fox_f_bias': 1.312471e+02, 'fox_q_norm_g': 1.681998e+01, 'fox_k_norm_g': 1.681775e+01, 'gdn_conv_w': 5.291658e-01, 'gdn_a_log': 1.954922e+01, 'gdn_dt_bias': 1.861468e+01, 'gdn_norm_g': 5.368436e+01, 'w_branch_a': 4.708992e-01, 'w_branch_b': 1.267060e+00, 'w_out': 1.300751e+00, 'norm2_g': 5.057913e+01, 'w_up': 5.757124e-01, 'ffn_conv_w': 6.623852e+00, 'w_down': 7.407473e-01}


def _to_microbatches(a, axis):
    t = _jnp.moveaxis(a, axis, 0)
    t = t.reshape((N_MICROBATCH, t.shape[0] // N_MICROBATCH) + t.shape[1:])
    return _jnp.moveaxis(t, 1, axis + 1)


def setup_inputs(seed: int = 0) -> dict:
    inp = _fwd_setup_inputs(seed)
    key = _jax.random.fold_in(_jax.random.key(seed), 7919)
    shape, _ = _output_shape()
    out = dict(inp)
    out["loss_target"] = _jax.random.normal(_jax.random.fold_in(key, 0), shape, _jnp.float32)
    for i, name in enumerate(TWIN_WEIGHTS):
        w = inp[name].astype(_jnp.float32)
        if MOMENT_SCALE is None:
            s = _jnp.sqrt(_jnp.mean(_jnp.square(w)) + 1e-30)
        else:
            s = MOMENT_SCALE[name]
        km, kv = _jax.random.split(_jax.random.fold_in(key, i + 1))
        out[name] = w
        out["m_" + name] = s * _jax.random.normal(km, w.shape, _jnp.float32)
        out["v_" + name] = (s * s) * _jax.random.uniform(kv, w.shape, _jnp.float32, 0.5, 1.5)
    if N_MICROBATCH > 1:
        for name, axis in PER_EXAMPLE_BATCH_AXIS.items():
            out[name] = _to_microbatches(out[name], axis)
    return {'x': out['x'], 'meta_tokens': out['meta_tokens'], 'norm1_g': out['norm1_g'], 'w_in': out['w_in'], 'fox_f_bias': out['fox_f_bias'], 'fox_q_norm_g': out['fox_q_norm_g'], 'fox_k_norm_g': out['fox_k_norm_g'], 'gdn_conv_w': out['gdn_conv_w'], 'gdn_a_log': out['gdn_a_log'], 'gdn_dt_bias': out['gdn_dt_bias'], 'gdn_norm_g': out['gdn_norm_g'], 'w_branch_a': out['w_branch_a'], 'w_branch_b': out['w_branch_b'], 'w_out': out['w_out'], 'norm2_g': out['norm2_g'], 'w_up': out['w_up'], 'ffn_conv_w': out['ffn_conv_w'], 'w_down': out['w_down'], 'loss_target': out['loss_target'], 'm_meta_tokens': out['m_meta_tokens'], 'm_norm1_g': out['m_norm1_g'], 'm_w_in': out['m_w_in'], 'm_fox_f_bias': out['m_fox_f_bias'], 'm_fox_q_norm_g': out['m_fox_q_norm_g'], 'm_fox_k_norm_g': out['m_fox_k_norm_g'], 'm_gdn_conv_w': out['m_gdn_conv_w'], 'm_gdn_a_log': out['m_gdn_a_log'], 'm_gdn_dt_bias': out['m_gdn_dt_bias'], 'm_gdn_norm_g': out['m_gdn_norm_g'], 'm_w_branch_a': out['m_w_branch_a'], 'm_w_branch_b': out['m_w_branch_b'], 'm_w_out': out['m_w_out'], 'm_norm2_g': out['m_norm2_g'], 'm_w_up': out['m_w_up'], 'm_ffn_conv_w': out['m_ffn_conv_w'], 'm_w_down': out['m_w_down'], 'v_meta_tokens': out['v_meta_tokens'], 'v_norm1_g': out['v_norm1_g'], 'v_w_in': out['v_w_in'], 'v_fox_f_bias': out['v_fox_f_bias'], 'v_fox_q_norm_g': out['v_fox_q_norm_g'], 'v_fox_k_norm_g': out['v_fox_k_norm_g'], 'v_gdn_conv_w': out['v_gdn_conv_w'], 'v_gdn_a_log': out['v_gdn_a_log'], 'v_gdn_dt_bias': out['v_gdn_dt_bias'], 'v_gdn_norm_g': out['v_gdn_norm_g'], 'v_w_branch_a': out['v_w_branch_a'], 'v_w_branch_b': out['v_w_branch_b'], 'v_w_out': out['v_w_out'], 'v_norm2_g': out['v_norm2_g'], 'v_w_up': out['v_w_up'], 'v_ffn_conv_w': out['v_ffn_conv_w'], 'v_w_down': out['v_w_down']}


def _loss(weights, diff, rest, loss_target):
    with _jax.named_scope("forward"):
        args = {**rest, TWIN_DIFF_INPUT: diff, **{k: w.astype(_WEIGHT_DTYPES[k]) for k, w in weights.items()}}
        y = _forward(args)
    with _jax.named_scope("loss_head"):
        err = _jnp.square(y.astype(_jnp.float32) - loss_target)
        return 0.5 * _jnp.sum(_jnp.mean(err, axis=-1)) if err.ndim else 0.5 * err


def _adamw(w, g, m, v):
    m = ADAM_B1 * m + (1.0 - ADAM_B1) * g
    v = ADAM_B2 * v + (1.0 - ADAM_B2) * _jnp.square(g)
    m_hat = m / (1.0 - ADAM_B1 ** ADAM_STEP)
    v_hat = v / (1.0 - ADAM_B2 ** ADAM_STEP)
    delta = -ADAM_LR * (m_hat / (_jnp.sqrt(v_hat) + ADAM_EPS) + ADAM_WD * w)
    return delta, m, v


def reference(x, meta_tokens, norm1_g, w_in, fox_f_bias, fox_q_norm_g, fox_k_norm_g, gdn_conv_w, gdn_a_log, gdn_dt_bias, gdn_norm_g, w_branch_a, w_branch_b, w_out, norm2_g, w_up, ffn_conv_w, w_down, loss_target, m_meta_tokens, m_norm1_g, m_w_in, m_fox_f_bias, m_fox_q_norm_g, m_fox_k_norm_g, m_gdn_conv_w, m_gdn_a_log, m_gdn_dt_bias, m_gdn_norm_g, m_w_branch_a, m_w_branch_b, m_w_out, m_norm2_g, m_w_up, m_ffn_conv_w, m_w_down, v_meta_tokens, v_norm1_g, v_w_in, v_fox_f_bias, v_fox_q_norm_g, v_fox_k_norm_g, v_gdn_conv_w, v_gdn_a_log, v_gdn_dt_bias, v_gdn_norm_g, v_w_branch_a, v_w_branch_b, v_w_out, v_norm2_g, v_w_up, v_ffn_conv_w, v_w_down):
    given = dict(x=x, meta_tokens=meta_tokens, norm1_g=norm1_g, w_in=w_in, fox_f_bias=fox_f_bias, fox_q_norm_g=fox_q_norm_g, fox_k_norm_g=fox_k_norm_g, gdn_conv_w=gdn_conv_w, gdn_a_log=gdn_a_log, gdn_dt_bias=gdn_dt_bias, gdn_norm_g=gdn_norm_g, w_branch_a=w_branch_a, w_branch_b=w_branch_b, w_out=w_out, norm2_g=norm2_g, w_up=w_up, ffn_conv_w=ffn_conv_w, w_down=w_down, loss_target=loss_target, m_meta_tokens=m_meta_tokens, m_norm1_g=m_norm1_g, m_w_in=m_w_in, m_fox_f_bias=m_fox_f_bias, m_fox_q_norm_g=m_fox_q_norm_g, m_fox_k_norm_g=m_fox_k_norm_g, m_gdn_conv_w=m_gdn_conv_w, m_gdn_a_log=m_gdn_a_log, m_gdn_dt_bias=m_gdn_dt_bias, m_gdn_norm_g=m_gdn_norm_g, m_w_branch_a=m_w_branch_a, m_w_branch_b=m_w_branch_b, m_w_out=m_w_out, m_norm2_g=m_norm2_g, m_w_up=m_w_up, m_ffn_conv_w=m_ffn_conv_w, m_w_down=m_w_down, v_meta_tokens=v_meta_tokens, v_norm1_g=v_norm1_g, v_w_in=v_w_in, v_fox_f_bias=v_fox_f_bias, v_fox_q_norm_g=v_fox_q_norm_g, v_fox_k_norm_g=v_fox_k_norm_g, v_gdn_conv_w=v_gdn_conv_w, v_gdn_a_log=v_gdn_a_log, v_gdn_dt_bias=v_gdn_dt_bias, v_gdn_norm_g=v_gdn_norm_g, v_w_branch_a=v_w_branch_a, v_w_branch_b=v_w_branch_b, v_w_out=v_w_out, v_norm2_g=v_norm2_g, v_w_up=v_w_up, v_ffn_conv_w=v_ffn_conv_w, v_w_down=v_w_down)
    weights = {n: given[n] for n in TWIN_WEIGHTS}
    shared = {n: given[n] for n in SHARED_INPUTS}
    per_example = {n: given[n] for n in ['x']}
    grad_fn = _jax.value_and_grad(_loss, argnums=(0, 1))

    def one_microbatch(ex, loss_target):
        ex = dict(ex)
        diff = ex.pop(TWIN_DIFF_INPUT)
        return grad_fn(weights, diff, {**shared, **ex}, loss_target)

    if N_MICROBATCH == 1:
        loss, (grad_w, grad_x) = one_microbatch(per_example, given["loss_target"])
    else:
        def body(carry, xs):
            loss_sum, grad_sum = carry
            l_k, (gw_k, gx_k) = one_microbatch(xs[0], xs[1])
            with _jax.named_scope("update"):
                return (loss_sum + l_k, _jax.tree.map(_jnp.add, grad_sum, gw_k)), gx_k

        init = (_jnp.zeros((), _jnp.float32), _jax.tree.map(_jnp.zeros_like, weights))
        (loss, grad_w), grad_x = _jax.lax.scan(body, init, (per_example, given["loss_target"]))
    with _jax.named_scope("update"):
        delta_w, new_m, new_v = {}, {}, {}
        for n in TWIN_WEIGHTS:
            delta_w[n], new_m[n], new_v[n] = _adamw(weights[n], grad_w[n], given["m_" + n], given["v_" + n])
    return (loss, grad_x, *[grad_w[n] for n in TWIN_WEIGHTS], *[delta_w[n] for n in TWIN_WEIGHTS],
            *[new_m[n] for n in TWIN_WEIGHTS], *[new_v[n] for n in TWIN_WEIGHTS])
```

```python
import functools

import jax
import jax.numpy as jnp
from jax import lax
from jax.experimental import pallas as pl
from jax.experimental.pallas import tpu as pltpu

F32, BF16 = jnp.float32, jnp.bfloat16
HI = lax.Precision.HIGHEST
MESH = pl.DeviceIdType.MESH

D = 1024
N_META = 16
DEPTH = 4
EPS = 1e-6
NEG = -1e30
FOX_W, FOX_DH = 512, 64
GDN_W, GDN_DH, GDN_H = 1024, 128, 8
CHUNK = 64
D_FF = 2816
N_DEV = 8
ADAM_LR, ADAM_B1, ADAM_B2, ADAM_EPS, ADAM_WD, ADAM_STEP = 0.001, 0.9, 0.999, 1e-08, 0.01, 10

VMEM_LIMIT_BYTES = 48 * 1024 * 1024
ROW_TILE = 128
LANE = 128

C_GQ, C_GK, C_GV, C_GZ, C_GATE, C_FQ, C_FK, C_FV = 0, 1024, 2048, 3072, 4096, 6144, 6656, 7168
W_MAIN = 7680
O_FQ, O_FK, O_FV, O_FL, O_GQ, O_GK, O_GV, O_BL, O_AL, O_GZ, O_GATE, O_END = (
    0, 512, 1024, 1536, 1544, 2568, 3592, 4616, 4624, 4632, 5656, 7704)


def _pick(n, cands):
    for c in cands:
        if n % c == 0:
            return c
    return n


def _call(body, *, name, out_shape, in_specs, out_specs, grid=(), scratch=(), sem=None):
    kw = dict(vmem_limit_bytes=VMEM_LIMIT_BYTES)
    if sem is not None:
        kw["dimension_semantics"] = sem
    return pl.pallas_call(body, name=name, out_shape=out_shape, grid=grid, in_specs=in_specs,
                          out_specs=out_specs, scratch_shapes=list(scratch),
                          compiler_params=pltpu.CompilerParams(**kw))


_DIMS = {"nn": (((1,), (0,)), ((), ())), "nt": (((1,), (1,)), ((), ())), "tn": (((0,), (0,)), ((), ()))}


def _dot(a, b, mode, prec=None):
    return lax.dot_general(a, b, _DIMS[mode], precision=prec, preferred_element_type=F32)


def _mm_grads(f, mode, a, b, g):
    if mode == "nn":
        return f(g, b, "nt"), f(a, g, "tn")
    if mode == "nt":
        return f(g, b, "nn"), f(g, a, "tn")
    return f(b, g, "nt"), f(a, g, "nn")


@functools.partial(jax.custom_vjp, nondiff_argnums=(2,))
def _mmb(a, b, mode):
    return _dot(a.astype(BF16), b.astype(BF16), mode)


def _mmb_fwd(a, b, mode):
    return _mmb(a, b, mode), (a, b)


def _mmb_bwd(mode, res, g):
    return _mm_grads(_mmb, mode, res[0], res[1], g)


_mmb.defvjp(_mmb_fwd, _mmb_bwd)


@functools.partial(jax.custom_vjp, nondiff_argnums=(2,))
def _mmh(a, b, mode):
    return _dot(a, b, mode, HI)


def _mmh_fwd(a, b, mode):
    return _mmh(a, b, mode), (a, b)


def _mmh_bwd(mode, res, g):
    return _mm_grads(_mmh, mode, res[0], res[1], g)


_mmh.defvjp(_mmh_fwd, _mmh_bwd)


def _softplus(z):
    return jnp.maximum(z, 0.0) + jnp.log(1.0 + jnp.exp(-jnp.abs(z)))


def _log_sigmoid(z):
    return jnp.minimum(z, 0.0) - jnp.log(1.0 + jnp.exp(-jnp.abs(z)))


def _silu(z):
    return z * jax.nn.sigmoid(z)


def _iota(shape, dim):
    return lax.broadcasted_iota(jnp.int32, shape, dim)


def _inv_unit_lower_raw(n):
    c = n.shape[0]
    ri, ci = _iota((c, c), 0), _iota((c, c), 1)
    eye = (ri == ci).astype(F32)
    dmask = (ri // 16) == (ci // 16)
    dpart = jnp.where(dmask, n, 0.0)
    lpart = n - dpart
    x = -dpart
    p = eye + x
    for _ in range(3):
        x = _mmh(x, x, "nn")
        p = p + _mmh(p, x, "nn")
    m = -_mmh(p, lpart, "nn")
    q = eye + m
    steps = 1
    while (1 << steps) < c // 16:
        steps += 1
    for _ in range(steps - 1):
        m = _mmh(m, m, "nn")
        q = q + _mmh(q, m, "nn")
    if steps >= 1 and c // 16 > 1:
        pass
    return _mmh(q, p, "nn")


@jax.custom_vjp
def _inv_unit_lower(n):
    return _inv_unit_lower_raw(n)


def _inv_fwd(n):
    t = _inv_unit_lower_raw(n)
    return t, t


def _inv_bwd(t, g):
    c = t.shape[0]
    strict = _iota((c, c), 0) > _iota((c, c), 1)
    d = -_mmh(_mmh(t, g, "tn"), t, "nt")
    return (jnp.where(strict, d, 0.0),)


_inv_unit_lower.defvjp(_inv_fwd, _inv_bwd)


def _shift_down(x, halo, s):
    if s == 0:
        return x
    xs = pltpu.roll(x, s, 0)
    hs = pltpu.roll(halo, s, 0)
    top = jnp.where(_iota(hs.shape, 0) < s, hs, xs[0:8])
    return jnp.concatenate([top, xs[8:]], axis=0)


def _shift_up(x, halo, s):
    if s == 0:
        return x
    tm = x.shape[0]
    xs = pltpu.roll(x, tm - s, 0)
    hs = pltpu.roll(halo, 8 - s, 0)
    bot = jnp.where(_iota(hs.shape, 0) >= 8 - s, hs, xs[tm - 8:])
    return jnp.concatenate([xs[:tm - 8], bot], axis=0)


def _causal_conv(x, halo, w):
    kk = w.shape[0]
    y = x * w[kk - 1:kk, :]
    for k in range(kk - 1):
        y = y + _shift_down(x, halo, kk - 1 - k) * w[k:k + 1, :]
    return y


def _head_scale(x, width, fn):
    outs = []
    for h in range(x.shape[1] // width):
        seg = x[:, h * width:(h + 1) * width]
        outs.append(seg * fn(jnp.sum(seg * seg, axis=1, keepdims=True)))
    return jnp.concatenate(outs, axis=1)


def _matmul(a, b, mode, out_dtype, add=None, name="mm"):
    if mode == "nn":
        (m, k), n = a.shape, b.shape[1]
    elif mode == "nt":
        (m, k), n = a.shape, b.shape[0]
    else:
        (k, m), n = a.shape, b.shape[1]
    tm = _pick(m, (640, 512, 256, 128))
    tn = _pick(n, (1408, 1024, 768, 512, 256, 128))
    if mode == "tn":
        tk = _pick(k, (640, 512, 256, 128))
    else:
        tk = k if k <= 2816 else _pick(k, (768, 512, 256, 128))
    nk = k // tk
    a_spec = {"nn": pl.BlockSpec((tm, tk), lambda i, j, kk: (i, kk)),
              "nt": pl.BlockSpec((tm, tk), lambda i, j, kk: (i, kk)),
              "tn": pl.BlockSpec((tk, tm), lambda i, j, kk: (kk, i))}[mode]
    b_spec = {"nn": pl.BlockSpec((tk, tn), lambda i, j, kk: (kk, j)),
              "nt": pl.BlockSpec((tn, tk), lambda i, j, kk: (j, kk)),
              "tn": pl.BlockSpec((tk, tn), lambda i, j, kk: (kk, j))}[mode]
    o_spec = pl.BlockSpec((tm, tn), lambda i, j, kk: (i, j))
    has_add = add is not None

    def body(*refs):
        a_ref, b_ref = refs[0], refs[1]
        add_ref = refs[2] if has_add else None
        o_ref = refs[3] if has_add else refs[2]
        part = _dot(a_ref[...].astype(BF16), b_ref[...].astype(BF16), mode)
        if nk == 1:
            if has_add:
                part = part + add_ref[...].astype(F32)
            o_ref[...] = part.astype(out_dtype)
        else:
            acc = refs[-1]
            kk = pl.program_id(2)

            @pl.when(kk == 0)
            def _():
                acc[...] = part

            @pl.when(kk > 0)
            def _():
                acc[...] += part

            @pl.when(kk == nk - 1)
            def _():
                r = acc[...]
                if has_add:
                    r = r + add_ref[...].astype(F32)
                o_ref[...] = r.astype(out_dtype)

    ins = [a, b] + ([add] if has_add else [])
    specs = [a_spec, b_spec] + ([o_spec] if has_add else [])
    return _call(body, name=name, out_shape=jax.ShapeDtypeStruct((m, n), out_dtype), grid=(m // tm, n // tn, nk),
                 in_specs=specs, out_specs=o_spec,
                 scratch=[pltpu.VMEM((tm, tn), F32)] if nk > 1 else [],
                 sem=("parallel", "parallel", "arbitrary"))(*ins)


def _row_spec(width, colblock, tm):
    return pl.BlockSpec((tm, width), lambda i, cb=colblock: (i, cb))


def _full_spec(arr):
    nd = arr.ndim
    return pl.BlockSpec(arr.shape, lambda i, nd=nd: (0,) * nd)


def _rowwise(fn, rows, params, outs, name, tm=ROW_TILE):
    lp = rows[0][0].shape[0]
    nr, npar = len(rows), len(params)

    def body(*refs):
        row0 = pl.program_id(0) * tm
        vals = [r[...].astype(F32) for r in refs[:nr + npar]]
        res = fn(*vals, row0)
        for o_ref, r in zip(refs[nr + npar:], res):
            o_ref[...] = r.astype(o_ref.dtype)

    out = _call(body, name=name, grid=(lp // tm,),
                out_shape=[jax.ShapeDtypeStruct((lp, w), dt) for w, dt in outs],
                in_specs=[_row_spec(w, cb, tm) for _, w, cb in rows] + [_full_spec(p) for p in params],
                out_specs=[_row_spec(w, 0, tm) for w, _ in outs], sem=("parallel",))(
                    *[r[0] for r in rows], *params)
    return out


def _rowwise_bwd(fn, rows, params, cts, name, pad_rows, grad_dtypes, adds=None, tm=ROW_TILE):
    lp = rows[0][0].shape[0]
    nr, npar, nct = len(rows), len(params), len(cts)
    adds = adds or [None] * nr
    add_list = [a for a in adds if a is not None]
    nadd = len(add_list)

    def body(*refs):
        i = pl.program_id(0)
        row0 = i * tm
        vals = [r[...].astype(F32) for r in refs[:nr + npar]]
        ct_vals = tuple(r[...].astype(F32) for r in refs[nr + npar:nr + npar + nct])
        add_refs = list(refs[nr + npar + nct:nr + npar + nct + nadd])
        outs = refs[nr + npar + nct + nadd:]
        _, vjp = jax.vjp(lambda *args: tuple(fn(*args, row0)), *vals)
        grads = vjp(ct_vals)
        valid = (row0 + _iota((tm, 1), 0)) >= pad_rows
        for idx in range(nr):
            g = jnp.where(valid, grads[idx], 0.0)
            if adds[idx] is not None:
                g = g + add_refs.pop(0)[...].astype(F32)
            outs[idx][...] = g.astype(outs[idx].dtype)
        for idx in range(npar):
            o_ref = outs[nr + idx]

            @pl.when(i == 0)
            def _(o_ref=o_ref):
                o_ref[...] = jnp.zeros_like(o_ref)

            o_ref[...] += grads[nr + idx]

    out = _call(body, name=name, grid=(lp // tm,),
                out_shape=[jax.ShapeDtypeStruct((lp, w), dt) for (_, w, _), dt in zip(rows, grad_dtypes)]
                + [jax.ShapeDtypeStruct(p.shape, F32) for p in params],
                in_specs=[_row_spec(w, cb, tm) for _, w, cb in rows] + [_full_spec(p) for p in params]
                + [_row_spec(w, cb, tm) for _, w, cb in cts] + [_row_spec(w, cb, tm) for _, w, cb in add_list],
                out_specs=[_row_spec(w, 0, tm) for _, w, _ in rows] + [_full_spec(p) for p in params],
                sem=("arbitrary",))(*[r[0] for r in rows], *params, *[c[0] for c in cts], *[a[0] for a in add_list])
    return out[:nr], out[nr:]


def _rmsnorm_fn(x, g, row0):
    return (x * lax.rsqrt(jnp.mean(x * x, axis=1, keepdims=True) + EPS) * g,)


def _fox_prep_fn(pad_rows, fq, fk, small, qg, kg, fb, row0):
    ri, ci = _iota((FOX_W, FOX_W), 0), _iota((FOX_W, FOX_W), 1)
    bd = jnp.where((ri // FOX_DH) == (ci // FOX_DH), 1.0 / FOX_DH, 0.0)

    def hn(x, g):
        return x * lax.rsqrt(_mmh(x * x, bd, "nn") + EPS) * g

    tm = small.shape[0]
    keep = (_iota((tm, LANE), 1) < 8) & ((row0 + _iota((tm, LANE), 0)) >= pad_rows)
    logf = jnp.where(keep, _log_sigmoid(small + fb), 0.0)
    return hn(fq, qg) * (FOX_DH ** -0.5), hn(fk, kg), logf


def _gdn_act_fn(cq, ck, cv, small, alog, dtb):
    tm = small.shape[0]
    q = _head_scale(_silu(cq), GDN_DH, lambda s: lax.rsqrt(s + EPS) * (GDN_DH ** -0.5))
    k = _head_scale(_silu(ck), GDN_DH, lambda s: lax.rsqrt(s + EPS))
    v = _silu(cv)
    lane = _iota((tm, LANE), 1)
    beta = jnp.where((lane >= 8) & (lane < 16), jax.nn.sigmoid(small), 0.0)
    g = jnp.where((lane >= 16) & (lane < 24), -jnp.exp(alog) * _softplus(small + dtb), 0.0)
    ri, ci = _iota((tm, tm), 0), _iota((tm, tm), 1)
    tri = jnp.where(((ri // CHUNK) == (ci // CHUNK)) & (ci <= ri), 1.0, 0.0)
    return q, k, v, beta + _mmh(tri, g, "nn")


def _gdn_post_fn(o, gz, gn, row0):
    return (_head_scale(o, GDN_DH, lambda s: lax.rsqrt(s * (1.0 / GDN_DH) + EPS)) * gn * _silu(gz),)


def _merge_fn(g0, g1, ya, yb, row0):
    return (jax.nn.sigmoid(g0) * ya + jax.nn.sigmoid(g1) * yb,)


def _cumsum_rows(x, reverse, name):
    lp, w = x.shape
    tm = _pick(lp, (640, 512, 256, 128))
    nt = lp // tm

    def body(x_ref, o_ref, carry):
        i = pl.program_id(0)

        @pl.when(i == 0)
        def _():
            carry[...] = jnp.zeros_like(carry)

        ri, ci = _iota((tm, tm), 0), _iota((tm, tm), 1)
        tri = jnp.where((ci >= ri) if reverse else (ci <= ri), 1.0, 0.0)
        blk = x_ref[...]
        o_ref[...] = _dot(tri, blk, "nn", HI) + carry[0:1, :]
        carry[...] = carry[...] + jnp.sum(blk, axis=0, keepdims=True)

    idx = (lambda i: (nt - 1 - i, 0)) if reverse else (lambda i: (i, 0))
    return _call(body, name=name, grid=(nt,), out_shape=jax.ShapeDtypeStruct((lp, w), F32),
                 in_specs=[pl.BlockSpec((tm, w), idx)], out_specs=pl.BlockSpec((tm, w), idx),
                 scratch=[pltpu.VMEM((8, w), F32)], sem=("arbitrary",))(x)


def _fox_scores(q, k, fq, fk, hh, qpos0, kpos0, pad_rows):
    tq, tk = q.shape[0], k.shape[0]
    lane = _iota(q.shape, 1)
    sel = (lane < FOX_DH) if hh == 0 else (lane >= FOX_DH)
    s = _dot(jnp.where(sel, q, jnp.zeros_like(q)), k, "nt") + fq - fk
    qpos = qpos0 + _iota((tq, tk), 0)
    kpos = kpos0 + _iota((tq, tk), 1)
    mask = (kpos <= qpos) & (kpos >= pad_rows)
    return jnp.where(mask, s, NEG), mask, sel


def _fox_fwd(q, k, v, fcol, frow, pad_rows):
    lp = q.shape[0]
    t = _pick(lp, (640, 512, 256, 128))
    n = lp // t

    def body(q_ref, k_ref, v_ref, fq_ref, fk_ref, o_ref, lse_ref, acc, m_s, l_s):
        i, j = pl.program_id(1), pl.program_id(2)

        @pl.when(j == 0)
        def _():
            acc[...] = jnp.zeros_like(acc)
            m_s[...] = jnp.full_like(m_s, NEG)
            l_s[...] = jnp.zeros_like(l_s)

        @pl.when(j <= i)
        def _():
            for hh in range(2):
                s, mask, _ = _fox_scores(q_ref[...], k_ref[...], fq_ref[hh], fk_ref[hh], hh, i * t, j * t, pad_rows)
                m_prev = m_s[hh]
                m_new = jnp.maximum(m_prev, jnp.max(s, axis=1, keepdims=True))
                p = jnp.where(mask, jnp.exp(s - m_new), 0.0)
                alpha = jnp.exp(m_prev - m_new)
                l_s[hh] = alpha * l_s[hh] + jnp.sum(p, axis=1, keepdims=True)
                acc[hh] = alpha * acc[hh] + _dot(p.astype(BF16), v_ref[...], "nn")
                m_s[hh] = m_new

        @pl.when(j == i)
        def _():
            outs = []
            for hh in range(2):
                l = l_s[hh]
                ok = l > 0.0
                outs.append(acc[hh] * jnp.where(ok, 1.0 / jnp.where(ok, l, 1.0), 0.0))
                lse_ref[hh] = jnp.where(ok, m_s[hh] + jnp.log(jnp.where(ok, l, 1.0)), 0.0)
            lane = _iota((t, LANE), 1)
            o_ref[...] = jnp.where(lane < FOX_DH, outs[0], outs[1]).astype(o_ref.dtype)

    qspec = pl.BlockSpec((t, LANE), lambda p, i, j: (i, p))
    kspec = pl.BlockSpec((t, LANE), lambda p, i, j: (jnp.minimum(j, i), p))
    cspec = pl.BlockSpec((2, t, 1), lambda p, i, j: (p, i, 0))
    rspec = pl.BlockSpec((2, 1, t), lambda p, i, j: (p, 0, jnp.minimum(j, i)))
    return _call(body, name="fox_fwd", grid=(FOX_W // LANE, n, n),
                 out_shape=[jax.ShapeDtypeStruct((lp, FOX_W), BF16), jax.ShapeDtypeStruct((8, lp, 1), F32)],
                 in_specs=[qspec, kspec, kspec, cspec, rspec], out_specs=[qspec, cspec],
                 scratch=[pltpu.VMEM((2, t, LANE), F32), pltpu.VMEM((2, t, 1), F32), pltpu.VMEM((2, t, 1), F32)],
                 sem=("parallel", "parallel", "arbitrary"))(q, k, v, fcol, frow)


def _fox_bwd_q(q, k, v, fcol, frow, o, do, lse, pad_rows):
    lp = q.shape[0]
    t = _pick(lp, (640, 512, 256, 128))
    n = lp // t

    def body(q_ref, k_ref, v_ref, fq_ref, fk_ref, o_ref, do_ref, lse_ref, dq_ref, dfq_ref, delta_ref, acc, dfa):
        i, j = pl.program_id(1), pl.program_id(2)
        lane = _iota((t, LANE), 1)

        @pl.when(j == 0)
        def _():
            acc[...] = jnp.zeros_like(acc)
            dfa[...] = jnp.zeros_like(dfa)
            prod = do_ref[...].astype(F32) * o_ref[...].astype(F32)
            delta_ref[0] = jnp.sum(jnp.where(lane < FOX_DH, prod, 0.0), axis=1, keepdims=True)
            delta_ref[1] = jnp.sum(jnp.where(lane >= FOX_DH, prod, 0.0), axis=1, keepdims=True)

        @pl.when(j <= i)
        def _():
            for hh in range(2):
                s, mask, sel = _fox_scores(q_ref[...], k_ref[...], fq_ref[hh], fk_ref[hh], hh, i * t, j * t, pad_rows)
                p = jnp.where(mask, jnp.exp(s - lse_ref[hh]), 0.0)
                dop = jnp.where(sel, do_ref[...], jnp.zeros_like(do_ref[...]))
                ds = p * (_dot(dop, v_ref[...], "nt") - delta_ref[hh])
                acc[hh] += _dot(ds.astype(BF16), k_ref[...], "nn")
                dfa[hh] += jnp.sum(ds, axis=1, keepdims=True)

        @pl.when(j == i)
        def _():
            dq_ref[...] = jnp.where(lane < FOX_DH, acc[0], acc[1])
            dfq_ref[...] = dfa[...]

    qspec = pl.BlockSpec((t, LANE), lambda p, i, j: (i, p))
    kspec = pl.BlockSpec((t, LANE), lambda p, i, j: (jnp.minimum(j, i), p))
    cspec = pl.BlockSpec((2, t, 1), lambda p, i, j: (p, i, 0))
    rspec = pl.BlockSpec((2, 1, t), lambda p, i, j: (p, 0, jnp.minimum(j, i)))
    col = jax.ShapeDtypeStruct((8, lp, 1), F32)
    return _call(body, name="fox_bwd_q", grid=(FOX_W // LANE, n, n),
                 out_shape=[jax.ShapeDtypeStruct((lp, FOX_W), F32), col, col],
                 in_specs=[qspec, kspec, kspec, cspec, rspec, qspec, qspec, cspec], out_specs=[qspec, cspec, cspec],
                 scratch=[pltpu.VMEM((2, t, LANE), F32), pltpu.VMEM((2, t, 1), F32)],
                 sem=("parallel", "parallel", "arbitrary"))(q, k, v, fcol, frow, o, do, lse)


def _fox_bwd_kv(q, k, v, fcol, frow, do, lse, delta, pad_rows):
    lp = q.shape[0]
    t = _pick(lp, (640, 512, 256, 128))
    n = lp // t

    def body(q_ref, k_ref, v_ref, fq_ref, fk_ref, do_ref, lse_ref, delta_ref, dk_ref, dv_ref, dfk_ref, dka, dva, dfa):
        j, i = pl.program_id(1), pl.program_id(2)
        lane = _iota((t, LANE), 1)

        @pl.when(i == 0)
        def _():
            dka[...] = jnp.zeros_like(dka)
            dva[...] = jnp.zeros_like(dva)
            dfa[...] = jnp.zeros_like(dfa)

        @pl.when(i >= j)
        def _():
            for hh in range(2):
                s, mask, sel = _fox_scores(q_ref[...], k_ref[...], fq_ref[hh], fk_ref[hh], hh, i * t, j * t, pad_rows)
                p = jnp.where(mask, jnp.exp(s - lse_ref[hh]), 0.0)
                dop = jnp.where(sel, do_ref[...], jnp.zeros_like(do_ref[...]))
                ds = p * (_dot(dop, v_ref[...], "nt") - delta_ref[hh])
                dva[hh] += _dot(p.astype(BF16), do_ref[...], "tn")
                dka[hh] += _dot(ds.astype(BF16), q_ref[...], "tn")
                dfa[hh] -= jnp.sum(ds, axis=0, keepdims=True)

        @pl.when(i == n - 1)
        def _():
            dk_ref[...] = jnp.where(lane < FOX_DH, dka[0], dka[1])
            dv_ref[...] = jnp.where(lane < FOX_DH, dva[0], dva[1])
            dfk_ref[...] = dfa[...]

    qspec = pl.BlockSpec((t, LANE), lambda p, j, i: (jnp.maximum(i, j), p))
    kspec = pl.BlockSpec((t, LANE), lambda p, j, i: (j, p))
    cspec = pl.BlockSpec((2, t, 1), lambda p, j, i: (p, jnp.maximum(i, j), 0))
    rspec = pl.BlockSpec((2, 1, t), lambda p, j, i: (p, 0, j))
    wide = jax.ShapeDtypeStruct((lp, FOX_W), F32)
    return _call(body, name="fox_bwd_kv", grid=(FOX_W // LANE, n, n),
                 out_shape=[wide, wide, jax.ShapeDtypeStruct((8, 1, lp), F32)],
                 in_specs=[qspec, kspec, kspec, cspec, rspec, qspec, cspec, cspec], out_specs=[kspec, kspec, rspec],
                 scratch=[pltpu.VMEM((2, t, LANE), F32), pltpu.VMEM((2, t, LANE), F32), pltpu.VMEM((2, 1, t), F32)],
                 sem=("parallel", "parallel", "arbitrary"))(q, k, v, fcol, frow, do, lse, delta)


def _gdn_chunk(q, k, v, beta, gcol, grow, s, inv):
    c = q.shape[0]
    ri, ci = _iota((c, c), 0), _iota((c, c), 1)
    dec = jnp.exp(jnp.where(ri >= ci, gcol - grow, NEG))
    dec_strict = jnp.where(ri > ci, dec, 0.0)
    eg = jnp.exp(gcol)
    kb = k * beta
    t = inv(_mmb(kb, k, "nt") * dec_strict)
    u_hat = _mmh(t, v * beta, "nn")
    w = _mmh(t, kb * eg, "nn")
    u = u_hat - _mmb(w, s, "nn")
    o = _mmb(q * eg, s, "nn") + _mmb(_mmb(q, k, "nt") * dec, u, "nn")
    glast = jnp.sum(jnp.where(_iota((1, c), 1) == c - 1, grow, 0.0), axis=1, keepdims=True)
    s_new = s * jnp.exp(glast) + _mmb(k * jnp.exp(glast - gcol), u, "tn")
    return o, s_new


def _gdn_specs(lp):
    n = lp // CHUNK
    wide = pl.BlockSpec((CHUNK, GDN_DH), lambda h, c: (c, h))
    col = pl.BlockSpec((1, CHUNK, 1), lambda h, c: (h, c, 0))
    row = pl.BlockSpec((1, 1, 1, CHUNK), lambda h, c: (h, c, 0, 0))
    st = pl.BlockSpec((1, 1, GDN_DH, GDN_DH), lambda h, c: (h, c, 0, 0))
    return n, wide, col, row, st


def _gdn_fwd(q, k, v, bcol, gcol, grow):
    lp = q.shape[0]
    n, wide, col, row, st = _gdn_specs(lp)

    def body(q_ref, k_ref, v_ref, b_ref, gc_ref, gr_ref, o_ref, sp_ref, s_scr):
        @pl.when(pl.program_id(1) == 0)
        def _():
            s_scr[...] = jnp.zeros_like(s_scr)

        s = s_scr[...]
        sp_ref[0, 0] = s
        o, s_new = _gdn_chunk(q_ref[...], k_ref[...], v_ref[...], b_ref[0], gc_ref[0], gr_ref[0, 0], s,
                              _inv_unit_lower_raw)
        o_ref[...] = o
        s_scr[...] = s_new

    return _call(body, name="gdn_fwd", grid=(GDN_H, n),
                 out_shape=[jax.ShapeDtypeStruct((lp, GDN_W), F32),
                            jax.ShapeDtypeStruct((GDN_H, n, GDN_DH, GDN_DH), F32)],
                 in_specs=[wide, wide, wide, col, col, row], out_specs=[wide, st],
                 scratch=[pltpu.VMEM((GDN_DH, GDN_DH), F32)], sem=("parallel", "arbitrary"))(q, k, v, bcol, gcol, grow)


def _gdn_bwd(q, k, v, bcol, gcol, grow, sprev, do):
    lp = q.shape[0]
    n, _, _, _, _ = _gdn_specs(lp)
    wide = pl.BlockSpec((CHUNK, GDN_DH), lambda h, c: (n - 1 - c, h))
    col = pl.BlockSpec((1, CHUNK, 1), lambda h, c: (h, n - 1 - c, 0))
    row = pl.BlockSpec((1, 1, 1, CHUNK), lambda h, c: (h, n - 1 - c, 0, 0))
    st = pl.BlockSpec((1, 1, GDN_DH, GDN_DH), lambda h, c: (h, n - 1 - c, 0, 0))

    def body(q_ref, k_ref, v_ref, b_ref, gc_ref, gr_ref, sp_ref, do_ref,
             dq_ref, dk_ref, dv_ref, db_ref, dgc_ref, dgr_ref, ds_scr):
        @pl.when(pl.program_id(1) == 0)
        def _():
            ds_scr[...] = jnp.zeros_like(ds_scr)

        fn = functools.partial(_gdn_chunk, inv=_inv_unit_lower)
        _, vjp = jax.vjp(fn, q_ref[...], k_ref[...], v_ref[...], b_ref[0], gc_ref[0], gr_ref[0, 0], sp_ref[0, 0])
        dq, dk, dv, db, dgc, dgr, ds = vjp((do_ref[...], ds_scr[...]))
        dq_ref[...] = dq
        dk_ref[...] = dk
        dv_ref[...] = dv
        db_ref[0] = db
        dgc_ref[0] = dgc
        dgr_ref[0, 0] = dgr
        ds_scr[...] = ds

    wshape = jax.ShapeDtypeStruct((lp, GDN_W), F32)
    cshape = jax.ShapeDtypeStruct((GDN_H, lp, 1), F32)
    return _call(body, name="gdn_bwd", grid=(GDN_H, n),
                 out_shape=[wshape, wshape, wshape, cshape, cshape, jax.ShapeDtypeStruct((GDN_H, n, 1, CHUNK), F32)],
                 in_specs=[wide, wide, wide, col, col, row, st, wide], out_specs=[wide, wide, wide, col, col, row],
                 scratch=[pltpu.VMEM((GDN_DH, GDN_DH), F32)], sem=("parallel", "arbitrary"))(
                     q, k, v, bcol, gcol, grow, sprev, do)


def _halo_prev(width, colblock, tm):
    return pl.BlockSpec((8, width), lambda i, cb=colblock: (jnp.maximum(i * (tm // 8) - 1, 0), cb))


def _gdn_act(proj, small, conv_w, alog_row, dtb_row, tm=ROW_TILE):
    lp = proj.shape[0]

    def body(xq, xk, xv, hq, hk, hv, wq, wk, wv, sm, al, dt, q_ref, k_ref, v_ref, bg_ref):
        first = (pl.program_id(0) > 0).astype(F32)
        cs = [_causal_conv(x[...], h[...] * first, w[...]) for x, h, w in ((xq, hq, wq), (xk, hk, wk), (xv, hv, wv))]
        q, k, v, bg = _gdn_act_fn(cs[0], cs[1], cs[2], sm[...], al[...], dt[...])
        q_ref[...], k_ref[...], v_ref[...], bg_ref[...] = q, k, v, bg

    wide = jax.ShapeDtypeStruct((lp, GDN_W), F32)
    wspec = [pl.BlockSpec((4, GDN_W), lambda i, c=c: (0, c)) for c in range(3)]
    return _call(body, name="gdn_act", grid=(lp // tm,),
                 out_shape=[wide, wide, wide, jax.ShapeDtypeStruct((lp, LANE), F32)],
                 in_specs=[_row_spec(GDN_W, c, tm) for c in range(3)] + [_halo_prev(GDN_W, c, tm) for c in range(3)]
                 + wspec + [_row_spec(LANE, 0, tm), _full_spec(alog_row), _full_spec(dtb_row)],
                 out_specs=[_row_spec(GDN_W, 0, tm)] * 3 + [_row_spec(LANE, 0, tm)], sem=("parallel",))(
                     proj, proj, proj, proj, proj, proj, conv_w, conv_w, conv_w, small, alog_row, dtb_row)


def _gdn_act_bwd(proj, small, conv_w, alog_row, dtb_row, dq, dk, dv, dbg, tm=ROW_TILE):
    lp = proj.shape[0]

    def body(xq, xk, xv, hq, hk, hv, wq, wk, wv, sm, al, dt, dq_r, dk_r, dv_r, dbg_r,
             dc_ref, dsm_ref, dal_ref, ddt_ref, dw_ref):
        i = pl.program_id(0)
        first = (i > 0).astype(F32)
        xs = [(x[...], h[...] * first, w[...]) for x, h, w in ((xq, hq, wq), (xk, hk, wk), (xv, hv, wv))]
        cs = [_causal_conv(*t) for t in xs]
        _, vjp = jax.vjp(_gdn_act_fn, cs[0], cs[1], cs[2], sm[...], al[...], dt[...])
        dcq, dck, dcv, dsm, dal, ddt = vjp((dq_r[...], dk_r[...], dv_r[...], dbg_r[...]))
        dsm_ref[...] = dsm

        @pl.when(i == 0)
        def _():
            dal_ref[...] = jnp.zeros_like(dal_ref)
            ddt_ref[...] = jnp.zeros_like(ddt_ref)
            dw_ref[...] = jnp.zeros_like(dw_ref)

        dal_ref[...] += dal
        ddt_ref[...] += ddt
        for c, (dc, (x, h, w)) in enumerate(zip((dcq, dck, dcv), xs)):
            dc_ref[:, c * GDN_W:(c + 1) * GDN_W] = dc
            rows = [jnp.sum(_shift_down(x, h, 3 - kk) * dc, axis=0, keepdims=True) for kk in range(4)]
            dw_ref[:, c * GDN_W:(c + 1) * GDN_W] += jnp.concatenate(rows, axis=0)

    wspec = [pl.BlockSpec((4, GDN_W), lambda i, c=c: (0, c)) for c in range(3)]
    row128 = jax.ShapeDtypeStruct((1, LANE), F32)
    return _call(body, name="gdn_act_bwd", grid=(lp // tm,),
                 out_shape=[jax.ShapeDtypeStruct((lp, 3 * GDN_W), F32), jax.ShapeDtypeStruct((lp, LANE), F32),
                            row128, row128, jax.ShapeDtypeStruct((4, 3 * GDN_W), F32)],
                 in_specs=[_row_spec(GDN_W, c, tm) for c in range(3)] + [_halo_prev(GDN_W, c, tm) for c in range(3)]
                 + wspec + [_row_spec(LANE, 0, tm), _full_spec(alog_row), _full_spec(dtb_row)]
                 + [_row_spec(GDN_W, 0, tm)] * 3 + [_row_spec(LANE, 0, tm)],
                 out_specs=[_row_spec(3 * GDN_W, 0, tm), _row_spec(LANE, 0, tm),
                            _full_spec(alog_row), _full_spec(dtb_row), pl.BlockSpec((4, 3 * GDN_W), lambda i: (0, 0))],
                 sem=("arbitrary",))(proj, proj, proj, proj, proj, proj, conv_w, conv_w, conv_w, small,
                                     alog_row, dtb_row, dq, dk, dv, dbg)


def _ffn_act(up_pre, conv_w, tm=ROW_TILE):
    lp = up_pre.shape[0]

    def body(xg, xv, hg, hv, wg, wv, a_ref):
        first = (pl.program_id(0) > 0).astype(F32)
        ug = _causal_conv(xg[...], hg[...] * first, wg[...])
        uv = _causal_conv(xv[...], hv[...] * first, wv[...])
        a_ref[...] = (_silu(ug) * uv).astype(a_ref.dtype)

    wspec = [pl.BlockSpec((3, D_FF), lambda i, c=c: (0, c)) for c in range(2)]
    return _call(body, name="ffn_act", grid=(lp // tm,), out_shape=jax.ShapeDtypeStruct((lp, D_FF), BF16),
                 in_specs=[_row_spec(D_FF, c, tm) for c in range(2)] + [_halo_prev(D_FF, c, tm) for c in range(2)] + wspec,
                 out_specs=_row_spec(D_FF, 0, tm), sem=("parallel",))(up_pre, up_pre, up_pre, up_pre, conv_w, conv_w)


def _ffn_act_bwd(up_pre, conv_w, dact, tm=ROW_TILE):
    lp = up_pre.shape[0]

    def body(xg, xv, hg, hv, wg, wv, da, du_ref, dw_ref):
        i = pl.program_id(0)
        first = (i > 0).astype(F32)
        xs = [(x[...], h[...] * first, w[...]) for x, h, w in ((xg, hg, wg), (xv, hv, wv))]
        ug, uv = [_causal_conv(*t) for t in xs]
        _, vjp = jax.vjp(lambda a, b: _silu(a) * b, ug, uv)
        dus = vjp(da[...].astype(F32))

        @pl.when(i == 0)
        def _():
            dw_ref[...] = jnp.zeros_like(dw_ref)

        for c, (du, (x, h, w)) in enumerate(zip(dus, xs)):
            du_ref[:, c * D_FF:(c + 1) * D_FF] = du
            rows = [jnp.sum(_shift_down(x, h, 2 - kk) * du, axis=0, keepdims=True) for kk in range(3)]
            dw_ref[:, c * D_FF:(c + 1) * D_FF] += jnp.concatenate(rows, axis=0)

    wspec = [pl.BlockSpec((3, D_FF), lambda i, c=c: (0, c)) for c in range(2)]
    return _call(body, name="ffn_act_bwd", grid=(lp // tm,),
                 out_shape=[jax.ShapeDtypeStruct((lp, 2 * D_FF), F32), jax.ShapeDtypeStruct((3, 2 * D_FF), F32)],
                 in_specs=[_row_spec(D_FF, c, tm) for c in range(2)] + [_halo_prev(D_FF, c, tm) for c in range(2)]
                 + wspec + [_row_spec(D_FF, 0, tm)],
                 out_specs=[_row_spec(2 * D_FF, 0, tm), pl.BlockSpec((3, 2 * D_FF), lambda i: (0, 0))],
                 sem=("arbitrary",))(up_pre, up_pre, up_pre, up_pre, conv_w, conv_w, dact)


def _conv_bwd_x(dy, w, pad_rows, width, name, tm=ROW_TILE):
    lp, ctot = dy.shape
    nt = lp // tm
    kk = w.shape[0]

    def body(d_ref, h_ref, w_ref, o_ref):
        i = pl.program_id(0)
        last = (i < nt - 1).astype(F32)
        d, h, wv = d_ref[...], h_ref[...] * last, w_ref[...]
        y = d * wv[kk - 1:kk, :]
        for k in range(kk - 1):
            y = y + _shift_up(d, h, kk - 1 - k) * wv[k:k + 1, :]
        valid = (i * tm + _iota((tm, 1), 0)) >= pad_rows
        o_ref[...] = jnp.where(valid, y, 0.0).astype(o_ref.dtype)

    return _call(body, name=name, grid=(nt, ctot // width), out_shape=jax.ShapeDtypeStruct((lp, ctot), BF16),
                 in_specs=[pl.BlockSpec((tm, width), lambda i, c: (i, c)),
                           pl.BlockSpec((8, width), lambda i, c: (jnp.minimum((i + 1) * (tm // 8), lp // 8 - 1), c)),
                           pl.BlockSpec((kk, width), lambda i, c: (0, c))],
                 out_specs=pl.BlockSpec((tm, width), lambda i, c: (i, c)), sem=("parallel", "parallel"))(dy, dy, w)


def _loss_head(h_res, target, row_start, tm=ROW_TILE):
    lp, d = h_res.shape
    t0 = row_start // tm

    def body(h_ref, t_ref, dy_ref, loss_ref):
        i = pl.program_id(0)

        @pl.when(i == 0)
        def _():
            loss_ref[...] = jnp.zeros_like(loss_ref)

        live = (i >= t0).astype(F32)
        err = (h_ref[...] - t_ref[...]) * live
        dy_ref[...] = err * (1.0 / d)
        loss_ref[...] += 0.5 / d * jnp.sum(err * err)

    return _call(body, name="loss_head", grid=(lp // tm,),
                 out_shape=[jax.ShapeDtypeStruct((lp, d), F32), jax.ShapeDtypeStruct((8, LANE), F32)],
                 in_specs=[pl.BlockSpec((tm, d), lambda i: (i, 0)),
                           pl.BlockSpec((tm, d), lambda i: (jnp.maximum(i - t0, 0), 0))],
                 out_specs=[pl.BlockSpec((tm, d), lambda i: (i, 0)), pl.BlockSpec((8, LANE), lambda i: (0, 0))],
                 sem=("arbitrary",))(h_res, target)


def _sum_adamw(parts, w, m, v, name):
    r, c = w.shape
    tm = _pick(r, (512, 256, 128, 64, 32, 16, 8))
    bc1 = 1.0 - ADAM_B1 ** ADAM_STEP
    bc2 = 1.0 - ADAM_B2 ** ADAM_STEP

    def body(p_ref, w_ref, m_ref, v_ref, g_ref, d_ref, nm_ref, nv_ref):
        g = p_ref[0].astype(F32)
        for s in range(1, N_DEV):
            g = g + p_ref[s].astype(F32)
        nm = ADAM_B1 * m_ref[...] + (1.0 - ADAM_B1) * g
        nv = ADAM_B2 * v_ref[...] + (1.0 - ADAM_B2) * (g * g)
        g_ref[...] = g
        nm_ref[...] = nm
        nv_ref[...] = nv
        d_ref[...] = -ADAM_LR * ((nm / bc1) / (jnp.sqrt(nv / bc2) + ADAM_EPS) + ADAM_WD * w_ref[...])

    spec = pl.BlockSpec((tm, c), lambda i: (i, 0))
    shp = jax.ShapeDtypeStruct((r, c), F32)
    return _call(body, name=name, grid=(r // tm,), out_shape=[shp] * 4,
                 in_specs=[pl.BlockSpec((N_DEV, tm, c), lambda i: (0, i, 0)), spec, spec, spec],
                 out_specs=[spec] * 4, sem=("parallel",))(parts, w, m, v)


_ANY = pl.BlockSpec(memory_space=pl.ANY)


def _all_gather(block, name):
    def body(x_ref, out_ref, send_sems, recv_sems, local_sem):
        x, y, c = lax.axis_index("x"), lax.axis_index("y"), lax.axis_index("c")
        me, sibling = (x, y, c), (x, y, 1 - c)
        chips = [(1 - x, y), (x, 1 - y), (1 - x, 1 - y)]

        def slot(px, py, pc):
            return out_ref.at[4 * px + 2 * py + pc]

        def copy(k, blk, to, src=None):
            return pltpu.make_async_remote_copy(
                src_ref=slot(*blk) if src is None else src, dst_ref=slot(*blk),
                send_sem=send_sems.at[k], recv_sem=recv_sems.at[k], device_id=to, device_id_type=MESH)

        mine = pltpu.make_async_copy(x_ref, slot(*me), local_sem)
        mine.start()
        first = [copy(0, me, sibling, src=x_ref)]
        first += [copy(1 + j, me, (*chip, c), src=x_ref) for j, chip in enumerate(chips)]
        for cp in first:
            cp.start()
        passed = [copy(4 + j, (*chip, c), sibling) for j, chip in enumerate(chips)]
        for j, chip in enumerate(chips):
            copy(1 + j, (*chip, c), me).wait_recv()
            passed[j].start()
        copy(0, sibling, me).wait_recv()
        for j, chip in enumerate(chips):
            copy(4 + j, (*chip, 1 - c), me).wait_recv()
        for cp in first + passed:
            cp.wait_send()
        mine.wait()

    return pl.pallas_call(
        body, name=name, out_shape=jax.ShapeDtypeStruct((N_DEV,) + block.shape, block.dtype),
        in_specs=[_ANY], out_specs=_ANY,
        scratch_shapes=[pltpu.SemaphoreType.DMA((7,)), pltpu.SemaphoreType.DMA((7,)), pltpu.SemaphoreType.DMA],
    )(block)


def _all_to_all(src, name):
    def body(s_ref, o_ref, send_sems, recv_sems, local_sem):
        x, y, c = lax.axis_index("x"), lax.axis_index("y"), lax.axis_index("c")
        me = 4 * x + 2 * y + c
        mine = pltpu.make_async_copy(s_ref.at[me], o_ref.at[me], local_sem)
        mine.start()
        copies = []
        for k in range(1, N_DEV):
            px = 1 - x if k & 4 else x
            py = 1 - y if k & 2 else y
            pc = 1 - c if k & 1 else c
            peer = 4 * px + 2 * py + pc
            copies.append((pltpu.make_async_remote_copy(
                src_ref=s_ref.at[peer], dst_ref=o_ref.at[me], send_sem=send_sems.at[k - 1],
                recv_sem=recv_sems.at[k - 1], device_id=(px, py, pc), device_id_type=MESH), peer, k))
        for cp, _, _ in copies:
            cp.start()
        for cp, peer, k in copies:
            cp.wait_send()
            pltpu.make_async_remote_copy(
                src_ref=s_ref.at[peer], dst_ref=o_ref.at[peer], send_sem=send_sems.at[k - 1],
                recv_sem=recv_sems.at[k - 1], device_id=(x, y, c), device_id_type=MESH).wait_recv()
        mine.wait()

    return pl.pallas_call(
        body, name=name, out_shape=jax.ShapeDtypeStruct(src.shape, src.dtype), in_specs=[_ANY], out_specs=_ANY,
        scratch_shapes=[pltpu.SemaphoreType.DMA((7,)), pltpu.SemaphoreType.DMA((7,)), pltpu.SemaphoreType.DMA],
    )(src)


def _pack(blocks, width, dtype, row_mult):
    flat = jnp.concatenate([b.astype(dtype).reshape(-1) for b in blocks])
    per = width * row_mult
    total = -(-flat.shape[0] // per) * per
    return jnp.pad(flat, (0, total - flat.shape[0])).reshape(total // width, width)


def _pack_dest(fulls, axes, width, dtype, row_mult):
    rows = []
    for f, ax in zip(fulls, axes):
        f = f.astype(dtype)
        if ax is None:
            rows.append(jnp.broadcast_to(f.reshape(1, -1), (N_DEV, f.size)))
        else:
            shp = f.shape
            f = f.reshape(shp[:ax] + (N_DEV, shp[ax] // N_DEV) + shp[ax + 1:])
            rows.append(jnp.moveaxis(f, ax, 0).reshape(N_DEV, -1))
    flat = jnp.concatenate(rows, axis=1)
    per = width * row_mult
    total = -(-flat.shape[1] // per) * per
    return jnp.pad(flat, ((0, 0), (0, total - flat.shape[1]))).reshape(N_DEV, total // width, width)


def _unpack(packed, shapes):
    flat = packed.reshape(-1)
    out, off = [], 0
    for s in shapes:
        n = 1
        for d in s:
            n *= d
        out.append(flat[off:off + n].reshape(s))
        off += n
    return out


def _unpack_gathered(gathered, shapes, axes):
    flat = gathered.reshape(N_DEV, -1)
    out, off = [], 0
    for s, ax in zip(shapes, axes):
        n = 1
        for d in s:
            n *= d
        blk = jnp.moveaxis(flat[:, off:off + n].reshape((N_DEV,) + tuple(s)), 0, ax)
        out.append(blk.reshape(tuple(s[:ax]) + (N_DEV * s[ax],) + tuple(s[ax + 1:])))
        off += n
    return out


def _permute_w_in(w):
    main = jnp.concatenate([w[:, O_GQ:O_BL], w[:, O_GZ:O_END], w[:, O_FQ:O_FL]], axis=1)
    small = jnp.concatenate([w[:, O_FL:O_GQ], w[:, O_BL:O_GZ], jnp.zeros((w.shape[0], LANE - 24), w.dtype)], axis=1)
    return main, small


def _unpermute_w_in(main, small):
    return jnp.concatenate([main[:, C_FQ:W_MAIN], small[:, 0:8], main[:, C_GQ:C_GZ], small[:, 8:24],
                            main[:, C_GZ:C_FQ]], axis=1)


def _lanes(vec, start):
    return jnp.zeros((1, LANE), F32).at[0, start:start + vec.shape[0]].set(vec)


BIG = ("w_in", "w_branch_a", "w_branch_b", "w_out", "w_up", "w_down")
BIG_AXES = (2, 2, 1, 1, 2, 1)
SHARDED_SMALL = ("meta_tokens", "gdn_conv_w", "ffn_conv_w")
SHARDED_SMALL_AXES = (1, 2, 2)
REPL = ("norm1_g", "fox_f_bias", "fox_q_norm_g", "fox_k_norm_g", "gdn_a_log", "gdn_dt_bias", "gdn_norm_g", "norm2_g")
ORDER = ("meta_tokens", "norm1_g", "w_in", "fox_f_bias", "fox_q_norm_g", "fox_k_norm_g", "gdn_conv_w", "gdn_a_log",
         "gdn_dt_bias", "gdn_norm_g", "w_branch_a", "w_branch_b", "w_out", "norm2_g", "w_up", "ffn_conv_w", "w_down")


def _layer_fwd(h_res, wl, pad_rows):
    lp = h_res.shape[0]
    sv = {"res_in": h_res}
    (h1,) = _rowwise(_rmsnorm_fn, [(h_res, D, 0)], [wl["norm1_g"]], [(D, BF16)], "rmsnorm1")
    proj = _matmul(h1, wl["w_main"], "nn", F32, name="mm_in")
    small = _matmul(h1, wl["w_small"], "nn", F32, name="mm_in_small")
    sv.update(h1=h1, proj=proj, small=small)

    fox_fn = functools.partial(_fox_prep_fn, pad_rows)
    qh, kh, logf = _rowwise(fox_fn, [(proj, FOX_W, C_FQ // FOX_W), (proj, FOX_W, C_FK // FOX_W), (small, LANE, 0)],
                            [wl["qg"], wl["kg"], wl["fb"]], [(FOX_W, BF16), (FOX_W, BF16), (LANE, F32)], "fox_prep")
    (vh,) = _rowwise(lambda a, row0: (a,), [(proj, FOX_W, C_FV // FOX_W)], [], [(FOX_W, BF16)], "fox_v")
    fsum = _cumsum_rows(logf, False, "fox_cumsum")
    f8 = fsum[:, :8].T
    fcol, frow = f8.reshape(8, lp, 1), f8.reshape(8, 1, lp)
    o_a, lse = _fox_fwd(qh, kh, vh, fcol, frow, pad_rows)
    y_a = _matmul(o_a, wl["w_branch_a"], "nn", F32, name="mm_branch_a")
    sv.update(qh=qh, kh=kh, vh=vh, fcol=fcol, frow=frow, o_a=o_a, lse=lse)

    gq, gk, gv, bg = _gdn_act(proj, small, wl["gdn_conv_w"], wl["alog"], wl["dtb"])
    bcol = bg[:, 8:16].T.reshape(8, lp, 1)
    g8 = bg[:, 16:24].T
    gcol, grow = g8.reshape(8, lp, 1), g8.reshape(8, lp // CHUNK, 1, CHUNK)
    o_raw, sprev = _gdn_fwd(gq, gk, gv, bcol, gcol, grow)
    (o_b,) = _rowwise(_gdn_post_fn, [(o_raw, GDN_W, 0), (proj, GDN_W, C_GZ // GDN_W)], [wl["gn"]], [(GDN_W, BF16)],
                      "gdn_post")
    y_b = _matmul(o_b, wl["w_branch_b"], "nn", F32, name="mm_branch_b")
    sv.update(gq=gq, gk=gk, gv=gv, bcol=bcol, gcol=gcol, grow=grow, o_raw=o_raw, sprev=sprev, o_b=o_b)

    (mixed,) = _rowwise(_merge_fn, [(proj, D, C_GATE // D), (proj, D, C_GATE // D + 1), (y_a, D, 0), (y_b, D, 0)], [],
                        [(D, BF16)], "merge")
    res_mid = _matmul(mixed, wl["w_out"], "nn", F32, add=h_res, name="mm_out")
    sv.update(y_a=y_a, y_b=y_b, mixed=mixed, res_mid=res_mid)

    (h2,) = _rowwise(_rmsnorm_fn, [(res_mid, D, 0)], [wl["norm2_g"]], [(D, BF16)], "rmsnorm2")
    up_pre = _matmul(h2, wl["w_up"], "nn", F32, name="mm_up")
    act = _ffn_act(up_pre, wl["ffn_conv_w"])
    out = _matmul(act, wl["w_down"], "nn", F32, add=res_mid, name="mm_down")
    sv.update(h2=h2, up_pre=up_pre, act=act)
    return out, sv


def _layer_bwd(dres, wl, sv, pad_rows):
    lp = dres.shape[0]
    gw = {}
    gw["w_down"] = _matmul(sv["act"], dres, "tn", F32, name="mm_dw_down")
    dact = _matmul(dres, wl["w_down"], "nt", BF16, name="mm_dact")
    dup, gw["ffn_conv_w"] = _ffn_act_bwd(sv["up_pre"], wl["ffn_conv_w"], dact)
    dup_pre = _conv_bwd_x(dup, wl["ffn_conv_w"], pad_rows, D_FF, "ffn_conv_bwd")
    gw["w_up"] = _matmul(sv["h2"], dup_pre, "tn", F32, name="mm_dw_up")
    dh2 = _matmul(dup_pre, wl["w_up"], "nt", F32, name="mm_dh2")
    (dmid,), (gw["norm2_g"],) = _rowwise_bwd(_rmsnorm_fn, [(sv["res_mid"], D, 0)], [wl["norm2_g"]], [(dh2, D, 0)],
                                             "rmsnorm2_bwd", pad_rows, [F32], adds=[(dres, D, 0)])
    gw["w_out"] = _matmul(sv["mixed"], dmid, "tn", F32, name="mm_dw_out")
    dmixed = _matmul(dmid, wl["w_out"], "nt", F32, name="mm_dmixed")
    proj, small = sv["proj"], sv["small"]
    (dg0, dg1, dya, dyb), _ = _rowwise_bwd(
        _merge_fn, [(proj, D, C_GATE // D), (proj, D, C_GATE // D + 1), (sv["y_a"], D, 0), (sv["y_b"], D, 0)], [],
        [(dmixed, D, 0)], "merge_bwd", pad_rows, [BF16, BF16, BF16, BF16])
    gw["w_branch_a"] = _matmul(sv["o_a"], dya, "tn", F32, name="mm_dw_a")
    do_a = _matmul(dya, wl["w_branch_a"], "nt", BF16, name="mm_do_a")
    gw["w_branch_b"] = _matmul(sv["o_b"], dyb, "tn", F32, name="mm_dw_b")
    do_b = _matmul(dyb, wl["w_branch_b"], "nt", F32, name="mm_do_b")

    (do_raw, dgz), (gw["gn"],) = _rowwise_bwd(_gdn_post_fn, [(sv["o_raw"], GDN_W, 0), (proj, GDN_W, C_GZ // GDN_W)],
                                              [wl["gn"]], [(do_b, GDN_W, 0)], "gdn_post_bwd", pad_rows, [F32, BF16])
    dgq, dgk, dgv, dbcol, dgcol, dgrow = _gdn_bwd(sv["gq"], sv["gk"], sv["gv"], sv["bcol"], sv["gcol"], sv["grow"],
                                                   sv["sprev"], do_raw)
    db8 = dbcol.reshape(8, lp).T
    dg8 = (dgcol.reshape(8, lp) + dgrow.reshape(8, lp)).T
    dbg = jnp.concatenate([jnp.zeros((lp, 8), F32), db8, dg8, jnp.zeros((lp, LANE - 24), F32)], axis=1)
    dconv, dsmall_g, gw["alog"], gw["dtb"], gw["gdn_conv_w"] = _gdn_act_bwd(
        proj, small, wl["gdn_conv_w"], wl["alog"], wl["dtb"], dgq, dgk, dgv, dbg)
    dqkv = _conv_bwd_x(dconv, wl["gdn_conv_w"], pad_rows, GDN_W, "gdn_conv_bwd")

    dqh, dfq, delta = _fox_bwd_q(sv["qh"], sv["kh"], sv["vh"], sv["fcol"], sv["frow"], sv["o_a"], do_a, sv["lse"],
                                 pad_rows)
    dkh, dvh, dfk = _fox_bwd_kv(sv["qh"], sv["kh"], sv["vh"], sv["fcol"], sv["frow"], do_a, sv["lse"], delta, pad_rows)
    df8 = (dfq.reshape(8, lp) + dfk.reshape(8, lp)).T
    dlogf = _cumsum_rows(jnp.pad(df8, ((0, 0), (0, LANE - 8))), True, "fox_cumsum_bwd")
    fox_fn = functools.partial(_fox_prep_fn, pad_rows)
    (dfq_p, dfk_p, dsmall_f), (gw["qg"], gw["kg"], gw["fb"]) = _rowwise_bwd(
        fox_fn, [(proj, FOX_W, C_FQ // FOX_W), (proj, FOX_W, C_FK // FOX_W), (small, LANE, 0)],
        [wl["qg"], wl["kg"], wl["fb"]], [(dqh, FOX_W, 0), (dkh, FOX_W, 0), (dlogf, LANE, 0)],
        "fox_prep_bwd", pad_rows, [BF16, BF16, F32], adds=[None, None, (dsmall_g, LANE, 0)])

    dproj = jnp.concatenate([dqkv, dgz, dg0, dg1, dfq_p, dfk_p, dvh.astype(BF16)], axis=1)
    gw["w_main"] = _matmul(sv["h1"], dproj, "tn", F32, name="mm_dw_main")
    gw["w_small"] = _matmul(sv["h1"], dsmall_f, "tn", F32, name="mm_dw_small")
    dh1 = _matmul(dproj, wl["w_main"], "nt", F32, name="mm_dh1")
    dh1 = _matmul(dsmall_f, wl["w_small"], "nt", F32, add=dh1, name="mm_dh1_small")
    (din,), (gw["norm1_g"],) = _rowwise_bwd(_rmsnorm_fn, [(sv["res_in"], D, 0)], [wl["norm1_g"]], [(dh1, D, 0)],
                                            "rmsnorm1_bwd", pad_rows, [F32], adds=[(dmid, D, 0)])
    return din, gw


def kernel(x, meta_tokens, norm1_g, w_in, fox_f_bias, fox_q_norm_g, fox_k_norm_g, gdn_conv_w, gdn_a_log, gdn_dt_bias, gdn_norm_g, w_branch_a, w_branch_b, w_out, norm2_g, w_up, ffn_conv_w, w_down, loss_target, m_meta_tokens, m_norm1_g, m_w_in, m_fox_f_bias, m_fox_q_norm_g, m_fox_k_norm_g, m_gdn_conv_w, m_gdn_a_log, m_gdn_dt_bias, m_gdn_norm_g, m_w_branch_a, m_w_branch_b, m_w_out, m_norm2_g, m_w_up, m_ffn_conv_w, m_w_down, v_meta_tokens, v_norm1_g, v_w_in, v_fox_f_bias, v_fox_q_norm_g, v_fox_k_norm_g, v_gdn_conv_w, v_gdn_a_log, v_gdn_dt_bias, v_gdn_norm_g, v_w_branch_a, v_w_branch_b, v_w_out, v_norm2_g, v_w_up, v_ffn_conv_w, v_w_down):
    w = dict(meta_tokens=meta_tokens, norm1_g=norm1_g, w_in=w_in, fox_f_bias=fox_f_bias, fox_q_norm_g=fox_q_norm_g,
             fox_k_norm_g=fox_k_norm_g, gdn_conv_w=gdn_conv_w, gdn_a_log=gdn_a_log, gdn_dt_bias=gdn_dt_bias,
             gdn_norm_g=gdn_norm_g, w_branch_a=w_branch_a, w_branch_b=w_branch_b, w_out=w_out, norm2_g=norm2_g,
             w_up=w_up, ffn_conv_w=ffn_conv_w, w_down=w_down)
    mom = dict(meta_tokens=m_meta_tokens, norm1_g=m_norm1_g, w_in=m_w_in, fox_f_bias=m_fox_f_bias,
               fox_q_norm_g=m_fox_q_norm_g, fox_k_norm_g=m_fox_k_norm_g, gdn_conv_w=m_gdn_conv_w,
               gdn_a_log=m_gdn_a_log, gdn_dt_bias=m_gdn_dt_bias, gdn_norm_g=m_gdn_norm_g, w_branch_a=m_w_branch_a,
               w_branch_b=m_w_branch_b, w_out=m_w_out, norm2_g=m_norm2_g, w_up=m_w_up, ffn_conv_w=m_ffn_conv_w,
               w_down=m_w_down)
    var = dict(meta_tokens=v_meta_tokens, norm1_g=v_norm1_g, w_in=v_w_in, fox_f_bias=v_fox_f_bias,
               fox_q_norm_g=v_fox_q_norm_g, fox_k_norm_g=v_fox_k_norm_g, gdn_conv_w=v_gdn_conv_w,
               gdn_a_log=v_gdn_a_log, gdn_dt_bias=v_gdn_dt_bias, gdn_norm_g=v_gdn_norm_g, w_branch_a=v_w_branch_a,
               w_branch_b=v_w_branch_b, w_out=v_w_out, norm2_g=v_norm2_g, w_up=v_w_up, ffn_conv_w=v_ffn_conv_w,
               w_down=v_w_down)
    depth = norm1_g.shape[0]
    seq = x.shape[1]
    l_tok = N_META + seq
    lp = -(-l_tok // LANE) * LANE
    pad_rows = lp - l_tok
    row_start = pad_rows + N_META

    big_shapes = [w[n].shape for n in BIG]
    gathered = _all_gather(_pack([w[n] for n in BIG], 1024, BF16, 16), "gather_weights")
    full = dict(zip(BIG, _unpack_gathered(gathered, big_shapes, BIG_AXES)))
    small_shapes = [w[n].shape for n in SHARDED_SMALL]
    gathered_s = _all_gather(_pack([w[n] for n in SHARDED_SMALL], LANE, F32, 8), "gather_small")
    full.update(zip(SHARDED_SMALL, _unpack_gathered(gathered_s, small_shapes, SHARDED_SMALL_AXES)))

    layers = []
    for l in range(depth):
        w_main, w_small = _permute_w_in(full["w_in"][l])
        layers.append(dict(
            w_main=w_main, w_small=w_small, w_branch_a=full["w_branch_a"][l], w_branch_b=full["w_branch_b"][l],
            w_out=full["w_out"][l], w_up=full["w_up"][l], w_down=full["w_down"][l],
            gdn_conv_w=full["gdn_conv_w"][l], ffn_conv_w=full["ffn_conv_w"][l],
            norm1_g=norm1_g[l].reshape(1, D), norm2_g=norm2_g[l].reshape(1, D),
            qg=jnp.tile(fox_q_norm_g[l], 8).reshape(1, FOX_W), kg=jnp.tile(fox_k_norm_g[l], 8).reshape(1, FOX_W),
            fb=_lanes(fox_f_bias[l], 0), alog=_lanes(gdn_a_log[l], 16), dtb=_lanes(gdn_dt_bias[l], 16),
            gn=jnp.tile(gdn_norm_g[l], 8).reshape(1, GDN_W)))

    h_res = jnp.concatenate([jnp.zeros((pad_rows, D), F32), full["meta_tokens"], x[0]], axis=0)
    saved = []
    for l in range(depth):
        h_res, sv = _layer_fwd(h_res, layers[l], pad_rows)
        saved.append(sv)
    dres, loss_part = _loss_head(h_res, loss_target[0], row_start)
    loss = lax.psum(loss_part[0, 0], ("x", "y", "c"))

    gws = [None] * depth
    for l in reversed(range(depth)):
        dres, gws[l] = _layer_bwd(dres, layers[l], saved[l], pad_rows)
    grad_x = dres[row_start:].reshape(x.shape)

    def stack(fn):
        return jnp.stack([fn(g) for g in gws])

    part = dict(
        meta_tokens=dres[pad_rows:row_start],
        norm1_g=stack(lambda g: g["norm1_g"][0]), norm2_g=stack(lambda g: g["norm2_g"][0]),
        w_in=stack(lambda g: _unpermute_w_in(g["w_main"], g["w_small"])),
        fox_f_bias=stack(lambda g: g["fb"][0, 0:8]),
        fox_q_norm_g=stack(lambda g: g["qg"].reshape(8, FOX_DH).sum(0)),
        fox_k_norm_g=stack(lambda g: g["kg"].reshape(8, FOX_DH).sum(0)),
        gdn_conv_w=stack(lambda g: g["gdn_conv_w"]), gdn_a_log=stack(lambda g: g["alog"][0, 16:24]),
        gdn_dt_bias=stack(lambda g: g["dtb"][0, 16:24]),
        gdn_norm_g=stack(lambda g: g["gn"].reshape(8, GDN_DH).sum(0)),
        w_branch_a=stack(lambda g: g["w_branch_a"]), w_branch_b=stack(lambda g: g["w_branch_b"]),
        w_out=stack(lambda g: g["w_out"]), w_up=stack(lambda g: g["w_up"]),
        ffn_conv_w=stack(lambda g: g["ffn_conv_w"]), w_down=stack(lambda g: g["w_down"]))

    landed = _all_to_all(_pack_dest([part[n] for n in BIG], BIG_AXES, 1024, BF16, 16), "scatter_grads")
    small_names = SHARDED_SMALL + REPL
    small_axes = SHARDED_SMALL_AXES + (None,) * len(REPL)
    landed_s = _all_to_all(_pack_dest([part[n] for n in small_names], small_axes, LANE, F32, 8), "scatter_small")

    res = {}
    outs = _sum_adamw(landed, *[_pack([d[n] for n in BIG], 1024, F32, 16) for d in (w, mom, var)], "adamw_big")
    for o_idx, packed in enumerate(outs):
        for n, a in zip(BIG, _unpack(packed, big_shapes)):
            res.setdefault(n, [None] * 4)[o_idx] = a
    shapes_s = [w[n].shape for n in small_names]
    outs = _sum_adamw(landed_s, *[_pack([d[n] for n in small_names], LANE, F32, 8) for d in (w, mom, var)], "adamw_small")
    for o_idx, packed in enumerate(outs):
        for n, a in zip(small_names, _unpack(packed, shapes_s)):
            res.setdefault(n, [None] * 4)[o_idx] = a

    return (loss, grad_x, *[res[n][0] for n in ORDER], *[res[n][1] for n in ORDER],
            *[res[n][2] for n in ORDER], *[res[n][3] for n in ORDER])
```

```python
import functools

import jax
import jax.numpy as jnp
from jax import lax
from jax.experimental import pallas as pl
from jax.experimental.pallas import tpu as pltpu

F32, BF16 = jnp.float32, jnp.bfloat16
HI = lax.Precision.HIGHEST
MESH = pl.DeviceIdType.MESH

D = 1024
N_META = 16
DEPTH = 4
EPS = 1e-6
NEG = -1e30
FOX_W, FOX_DH = 512, 64
GDN_W, GDN_DH, GDN_H = 1024, 128, 8
CHUNK = 64
D_FF = 2816
N_DEV = 8
ADAM_LR, ADAM_B1, ADAM_B2, ADAM_EPS, ADAM_WD, ADAM_STEP = 0.001, 0.9, 0.999, 1e-08, 0.01, 10

VMEM_LIMIT_BYTES = 48 * 1024 * 1024
ROW_TILE = 128
LANE = 128

C_GQ, C_GK, C_GV, C_GZ, C_GATE, C_FQ, C_FK, C_FV = 0, 1024, 2048, 3072, 4096, 6144, 6656, 7168
W_MAIN = 7680
O_FQ, O_FK, O_FV, O_FL, O_GQ, O_GK, O_GV, O_BL, O_AL, O_GZ, O_GATE, O_END = (
    0, 512, 1024, 1536, 1544, 2568, 3592, 4616, 4624, 4632, 5656, 7704)


def _pick(n, cands):
    for c in cands:
        if n % c == 0:
            return c
    return n


def _call(body, *, name, out_shape, in_specs, out_specs, grid=(), scratch=(), sem=None):
    kw = dict(vmem_limit_bytes=VMEM_LIMIT_BYTES)
    if sem is not None:
        kw["dimension_semantics"] = sem
    return pl.pallas_call(body, name=name, out_shape=out_shape, grid=grid, in_specs=in_specs,
                          out_specs=out_specs, scratch_shapes=list(scratch),
                          compiler_params=pltpu.CompilerParams(**kw))


_DIMS = {"nn": (((1,), (0,)), ((), ())), "nt": (((1,), (1,)), ((), ())), "tn": (((0,), (0,)), ((), ()))}


_DIMS_BATCHED = {"nn": (((2,), (1,)), ((0,), (0,))), "nt": (((2,), (2,)), ((0,), (0,))),
                 "tn": (((1,), (1,)), ((0,), (0,)))}


def _dot(a, b, mode, prec=None):
    dims = _DIMS[mode] if a.ndim == 2 else _DIMS_BATCHED[mode]
    return lax.dot_general(a, b, dims, precision=prec, preferred_element_type=F32)


def _mm_grads(f, mode, a, b, g):
    if mode == "nn":
        return f(g, b, "nt"), f(a, g, "tn")
    if mode == "nt":
        return f(g, b, "nn"), f(g, a, "tn")
    return f(b, g, "nt"), f(a, g, "nn")


@functools.partial(jax.custom_vjp, nondiff_argnums=(2,))
def _mmb(a, b, mode):
    return _dot(a.astype(BF16), b.astype(BF16), mode)


def _mmb_fwd(a, b, mode):
    return _mmb(a, b, mode), (a, b)


def _mmb_bwd(mode, res, g):
    return _mm_grads(_mmb, mode, res[0], res[1], g)


_mmb.defvjp(_mmb_fwd, _mmb_bwd)


@functools.partial(jax.custom_vjp, nondiff_argnums=(2,))
def _mmh(a, b, mode):
    return _dot(a, b, mode, HI)


def _mmh_fwd(a, b, mode):
    return _mmh(a, b, mode), (a, b)


def _mmh_bwd(mode, res, g):
    return _mm_grads(_mmh, mode, res[0], res[1], g)


_mmh.defvjp(_mmh_fwd, _mmh_bwd)


def _softplus(z):
    return jnp.maximum(z, 0.0) + jnp.log(1.0 + jnp.exp(-jnp.abs(z)))


def _log_sigmoid(z):
    return jnp.minimum(z, 0.0) - jnp.log(1.0 + jnp.exp(-jnp.abs(z)))


def _silu(z):
    return z * jax.nn.sigmoid(z)


def _iota(shape, dim):
    return lax.broadcasted_iota(jnp.int32, shape, dim)


def _inv_unit_lower_raw(n):
    c = n.shape[-1]
    ri, ci = _iota((c, c), 0), _iota((c, c), 1)
    eye = (ri == ci).astype(F32)
    dmask = (ri // 16) == (ci // 16)
    dpart = jnp.where(dmask, n, 0.0)
    lpart = n - dpart
    x = -dpart
    p = eye + x
    for _ in range(3):
        x = _mmh(x, x, "nn")
        p = p + _mmh(p, x, "nn")
    m = -_mmh(p, lpart, "nn")
    q = eye + m
    steps = 1
    while (1 << steps) < c // 16:
        steps += 1
    for _ in range(steps - 1):
        m = _mmh(m, m, "nn")
        q = q + _mmh(q, m, "nn")
    return _mmh(q, p, "nn")


@jax.custom_vjp
def _inv_unit_lower(n):
    return _inv_unit_lower_raw(n)


def _inv_fwd(n):
    t = _inv_unit_lower_raw(n)
    return t, t


def _inv_bwd(t, g):
    c = t.shape[-1]
    strict = _iota((c, c), 0) > _iota((c, c), 1)
    d = -_mmh(_mmh(t, g, "tn"), t, "nt")
    return (jnp.where(strict, d, 0.0),)


_inv_unit_lower.defvjp(_inv_fwd, _inv_bwd)


def _shift_down(x, halo, s):
    if s == 0:
        return x
    xs = pltpu.roll(x, s, 0)
    hs = pltpu.roll(halo, s, 0)
    top = jnp.where(_iota(hs.shape, 0) < s, hs, xs[0:8])
    return jnp.concatenate([top, xs[8:]], axis=0)


def _shift_up(x, halo, s):
    if s == 0:
        return x
    tm = x.shape[0]
    xs = pltpu.roll(x, tm - s, 0)
    hs = pltpu.roll(halo, 8 - s, 0)
    bot = jnp.where(_iota(hs.shape, 0) >= 8 - s, hs, xs[tm - 8:])
    return jnp.concatenate([xs[:tm - 8], bot], axis=0)


def _causal_conv(x, halo, w):
    kk = w.shape[0]
    y = x * w[kk - 1:kk, :]
    for k in range(kk - 1):
        y = y + _shift_down(x, halo, kk - 1 - k) * w[k:k + 1, :]
    return y


def _head_scale(x, width, fn):
    outs = []
    for h in range(x.shape[1] // width):
        seg = x[:, h * width:(h + 1) * width]
        outs.append(seg * fn(jnp.sum(seg * seg, axis=1, keepdims=True)))
    return jnp.concatenate(outs, axis=1)


def _matmul(a, b, mode, out_dtype, add=None, name="mm"):
    if mode == "nn":
        (m, k), n = a.shape, b.shape[1]
    elif mode == "nt":
        (m, k), n = a.shape, b.shape[0]
    else:
        (k, m), n = a.shape, b.shape[1]
    tm = _pick(m, (640, 512, 256, 128))
    tn = _pick(n, (1408, 1024, 768, 512, 256, 128))
    if mode == "tn":
        tk = _pick(k, (640, 512, 256, 128))
    else:
        tk = k if k <= 2816 else _pick(k, (768, 512, 256, 128))
    nk = k // tk
    a_spec = {"nn": pl.BlockSpec((tm, tk), lambda i, j, kk: (i, kk)),
              "nt": pl.BlockSpec((tm, tk), lambda i, j, kk: (i, kk)),
              "tn": pl.BlockSpec((tk, tm), lambda i, j, kk: (kk, i))}[mode]
    b_spec = {"nn": pl.BlockSpec((tk, tn), lambda i, j, kk: (kk, j)),
              "nt": pl.BlockSpec((tn, tk), lambda i, j, kk: (j, kk)),
              "tn": pl.BlockSpec((tk, tn), lambda i, j, kk: (kk, j))}[mode]
    o_spec = pl.BlockSpec((tm, tn), lambda i, j, kk: (i, j))
    has_add = add is not None

    def body(*refs):
        a_ref, b_ref = refs[0], refs[1]
        add_ref = refs[2] if has_add else None
        o_ref = refs[3] if has_add else refs[2]
        part = _dot(a_ref[...].astype(BF16), b_ref[...].astype(BF16), mode)
        if nk == 1:
            if has_add:
                part = part + add_ref[...].astype(F32)
            o_ref[...] = part.astype(out_dtype)
        else:
            acc = refs[-1]
            kk = pl.program_id(2)

            @pl.when(kk == 0)
            def _():
                acc[...] = part

            @pl.when(kk > 0)
            def _():
                acc[...] += part

            @pl.when(kk == nk - 1)
            def _():
                r = acc[...]
                if has_add:
                    r = r + add_ref[...].astype(F32)
                o_ref[...] = r.astype(out_dtype)

    ins = [a, b] + ([add] if has_add else [])
    specs = [a_spec, b_spec] + ([o_spec] if has_add else [])
    return _call(body, name=name, out_shape=jax.ShapeDtypeStruct((m, n), out_dtype), grid=(m // tm, n // tn, nk),
                 in_specs=specs, out_specs=o_spec,
                 scratch=[pltpu.VMEM((tm, tn), F32)] if nk > 1 else [],
                 sem=("parallel", "parallel", "arbitrary"))(*ins)


def _row_spec(width, colblock, tm):
    return pl.BlockSpec((tm, width), lambda i, cb=colblock: (i, cb))


def _full_spec(arr):
    nd = arr.ndim
    return pl.BlockSpec(arr.shape, lambda i, nd=nd: (0,) * nd)


def _rowwise(fn, rows, params, outs, name, tm=ROW_TILE):
    lp = rows[0][0].shape[0]
    nr, npar = len(rows), len(params)

    def body(*refs):
        row0 = pl.program_id(0) * tm
        vals = [r[...].astype(F32) for r in refs[:nr + npar]]
        res = fn(*vals, row0)
        for o_ref, r in zip(refs[nr + npar:], res):
            o_ref[...] = r.astype(o_ref.dtype)

    out = _call(body, name=name, grid=(lp // tm,),
                out_shape=[jax.ShapeDtypeStruct((lp, w), dt) for w, dt in outs],
                in_specs=[_row_spec(w, cb, tm) for _, w, cb in rows] + [_full_spec(p) for p in params],
                out_specs=[_row_spec(w, 0, tm) for w, _ in outs], sem=("parallel",))(
                    *[r[0] for r in rows], *params)
    return out


def _rowwise_bwd(fn, rows, params, cts, name, pad_rows, grad_dtypes, adds=None, tm=ROW_TILE):
    lp = rows[0][0].shape[0]
    nr, npar, nct = len(rows), len(params), len(cts)
    adds = adds or [None] * nr
    add_list = [a for a in adds if a is not None]
    nadd = len(add_list)

    def body(*refs):
        i = pl.program_id(0)
        row0 = i * tm
        vals = [r[...].astype(F32) for r in refs[:nr + npar]]
        ct_vals = tuple(r[...].astype(F32) for r in refs[nr + npar:nr + npar + nct])
        add_refs = list(refs[nr + npar + nct:nr + npar + nct + nadd])
        outs = refs[nr + npar + nct + nadd:]
        _, vjp = jax.vjp(lambda *args: tuple(fn(*args, row0)), *vals)
        grads = vjp(ct_vals)
        valid = (row0 + _iota((tm, 1), 0)) >= pad_rows
        for idx in range(nr):
            g = jnp.where(valid, grads[idx], 0.0)
            if adds[idx] is not None:
                g = g + add_refs.pop(0)[...].astype(F32)
            outs[idx][...] = g.astype(outs[idx].dtype)
        for idx in range(npar):
            o_ref = outs[nr + idx]

            @pl.when(i == 0)
            def _(o_ref=o_ref):
                o_ref[...] = jnp.zeros_like(o_ref)

            o_ref[...] += grads[nr + idx]

    out = _call(body, name=name, grid=(lp // tm,),
                out_shape=[jax.ShapeDtypeStruct((lp, w), dt) for (_, w, _), dt in zip(rows, grad_dtypes)]
                + [jax.ShapeDtypeStruct(p.shape, F32) for p in params],
                in_specs=[_row_spec(w, cb, tm) for _, w, cb in rows] + [_full_spec(p) for p in params]
                + [_row_spec(w, cb, tm) for _, w, cb in cts] + [_row_spec(w, cb, tm) for _, w, cb in add_list],
                out_specs=[_row_spec(w, 0, tm) for _, w, _ in rows] + [_full_spec(p) for p in params],
                sem=("arbitrary",))(*[r[0] for r in rows], *params, *[c[0] for c in cts], *[a[0] for a in add_list])
    return out[:nr], out[nr:]


def _rmsnorm_fn(x, g, row0):
    return (x * lax.rsqrt(jnp.mean(x * x, axis=1, keepdims=True) + EPS) * g,)


def _fox_prep_fn(pad_rows, fq, fk, small, qg, kg, fb, row0):
    ri, ci = _iota((FOX_W, FOX_W), 0), _iota((FOX_W, FOX_W), 1)
    bd = jnp.where((ri // FOX_DH) == (ci // FOX_DH), 1.0 / FOX_DH, 0.0)

    def hn(x, g):
        return x * lax.rsqrt(_mmh(x * x, bd, "nn") + EPS) * g

    tm = small.shape[0]
    keep = (_iota((tm, LANE), 1) < 8) & ((row0 + _iota((tm, LANE), 0)) >= pad_rows)
    logf = jnp.where(keep, _log_sigmoid(small + fb), 0.0)
    return hn(fq, qg) * (FOX_DH ** -0.5), hn(fk, kg), logf


def _gdn_act_fn(cq, ck, cv, small, alog, dtb):
    tm = small.shape[0]
    q = _head_scale(_silu(cq), GDN_DH, lambda s: lax.rsqrt(s + EPS) * (GDN_DH ** -0.5))
    k = _head_scale(_silu(ck), GDN_DH, lambda s: lax.rsqrt(s + EPS))
    v = _silu(cv)
    lane = _iota((tm, LANE), 1)
    beta = jnp.where((lane >= 8) & (lane < 16), jax.nn.sigmoid(small), 0.0)
    g = jnp.where((lane >= 16) & (lane < 24), -jnp.exp(alog) * _softplus(small + dtb), 0.0)
    ri, ci = _iota((tm, tm), 0), _iota((tm, tm), 1)
    tri = jnp.where(((ri // CHUNK) == (ci // CHUNK)) & (ci <= ri), 1.0, 0.0)
    return q, k, v, beta + _mmh(tri, g, "nn")


def _gdn_post_fn(o, gz, gn, row0):
    return (_head_scale(o, GDN_DH, lambda s: lax.rsqrt(s * (1.0 / GDN_DH) + EPS)) * gn * _silu(gz),)


def _merge_fn(g0, g1, ya, yb, row0):
    return (jax.nn.sigmoid(g0) * ya + jax.nn.sigmoid(g1) * yb,)


def _cumsum_rows(x, reverse, name):
    lp, w = x.shape
    tm = _pick(lp, (640, 512, 256, 128))
    nt = lp // tm

    def body(x_ref, o_ref, carry):
        i = pl.program_id(0)

        @pl.when(i == 0)
        def _():
            carry[...] = jnp.zeros_like(carry)

        ri, ci = _iota((tm, tm), 0), _iota((tm, tm), 1)
        tri = jnp.where((ci >= ri) if reverse else (ci <= ri), 1.0, 0.0)
        blk = x_ref[...]
        o_ref[...] = _dot(tri, blk, "nn", HI) + carry[0:1, :]
        carry[...] = carry[...] + jnp.sum(blk, axis=0, keepdims=True)

    idx = (lambda i: (nt - 1 - i, 0)) if reverse else (lambda i: (i, 0))
    return _call(body, name=name, grid=(nt,), out_shape=jax.ShapeDtypeStruct((lp, w), F32),
                 in_specs=[pl.BlockSpec((tm, w), idx)], out_specs=pl.BlockSpec((tm, w), idx),
                 scratch=[pltpu.VMEM((8, w), F32)], sem=("arbitrary",))(x)


def _fox_scores(q, k, fq, fk, hh, qpos0, kpos0, pad_rows, masked):
    tq, tk = q.shape[0], k.shape[0]
    lane = _iota(q.shape, 1)
    sel = (lane < FOX_DH) if hh == 0 else (lane >= FOX_DH)
    s = _dot(jnp.where(sel, q, jnp.zeros_like(q)), k, "nt") + fq - fk
    if not masked:
        return s, None, sel
    qpos = qpos0 + _iota((tq, tk), 0)
    kpos = kpos0 + _iota((tq, tk), 1)
    mask = (kpos <= qpos) & (kpos >= pad_rows)
    return jnp.where(mask, s, NEG), mask, sel


def _probs(s, mask, shift):
    p = jnp.exp(s - shift)
    return p if mask is None else jnp.where(mask, p, 0.0)


def _both_variants(needs_mask, fn):
    @pl.when(needs_mask)
    def _():
        fn(True)

    @pl.when(jnp.logical_not(needs_mask))
    def _():
        fn(False)


def _fox_fwd(q, k, v, fcol, frow, pad_rows):
    lp = q.shape[0]
    t = _pick(lp, (640, 512, 256, 128))
    n = lp // t

    def body(q_ref, k_ref, v_ref, fq_ref, fk_ref, o_ref, lse_ref, acc, m_s, l_s):
        i, j = pl.program_id(1), pl.program_id(2)

        @pl.when(j == 0)
        def _():
            acc[...] = jnp.zeros_like(acc)
            m_s[...] = jnp.full_like(m_s, NEG)
            l_s[...] = jnp.zeros_like(l_s)

        def step(masked):
            for hh in range(2):
                s, mask, _ = _fox_scores(q_ref[...], k_ref[...], fq_ref[hh], fk_ref[hh], hh, i * t, j * t, pad_rows,
                                         masked)
                m_prev = m_s[hh]
                m_new = jnp.maximum(m_prev, jnp.max(s, axis=1, keepdims=True))
                p = _probs(s, mask, m_new)
                alpha = jnp.exp(m_prev - m_new)
                l_s[hh] = alpha * l_s[hh] + jnp.sum(p, axis=1, keepdims=True)
                acc[hh] = alpha * acc[hh] + _dot(p.astype(BF16), v_ref[...], "nn")
                m_s[hh] = m_new

        @pl.when(j <= i)
        def _():
            _both_variants((j == i) | (j == 0), step)

        @pl.when(j == i)
        def _():
            outs = []
            for hh in range(2):
                l = l_s[hh]
                ok = l > 0.0
                outs.append(acc[hh] * jnp.where(ok, 1.0 / jnp.where(ok, l, 1.0), 0.0))
                lse_ref[hh] = jnp.where(ok, m_s[hh] + jnp.log(jnp.where(ok, l, 1.0)), 0.0)
            lane = _iota((t, LANE), 1)
            o_ref[...] = jnp.where(lane < FOX_DH, outs[0], outs[1]).astype(o_ref.dtype)

    qspec = pl.BlockSpec((t, LANE), lambda p, i, j: (i, p))
    kspec = pl.BlockSpec((t, LANE), lambda p, i, j: (jnp.minimum(j, i), p))
    cspec = pl.BlockSpec((2, t, 1), lambda p, i, j: (p, i, 0))
    rspec = pl.BlockSpec((2, 1, t), lambda p, i, j: (p, 0, jnp.minimum(j, i)))
    return _call(body, name="fox_fwd", grid=(FOX_W // LANE, n, n),
                 out_shape=[jax.ShapeDtypeStruct((lp, FOX_W), BF16), jax.ShapeDtypeStruct((8, lp, 1), F32)],
                 in_specs=[qspec, kspec, kspec, cspec, rspec], out_specs=[qspec, cspec],
                 scratch=[pltpu.VMEM((2, t, LANE), F32), pltpu.VMEM((2, t, 1), F32), pltpu.VMEM((2, t, 1), F32)],
                 sem=("parallel", "parallel", "arbitrary"))(q, k, v, fcol, frow)


def _fox_bwd_q(q, k, v, fcol, frow, o, do, lse, pad_rows):
    lp = q.shape[0]
    t = _pick(lp, (640, 512, 256, 128))
    n = lp // t

    def body(q_ref, k_ref, v_ref, fq_ref, fk_ref, o_ref, do_ref, lse_ref, dq_ref, dfq_ref, delta_ref, acc, dfa):
        i, j = pl.program_id(1), pl.program_id(2)
        lane = _iota((t, LANE), 1)

        @pl.when(j == 0)
        def _():
            acc[...] = jnp.zeros_like(acc)
            dfa[...] = jnp.zeros_like(dfa)
            prod = do_ref[...].astype(F32) * o_ref[...].astype(F32)
            delta_ref[0] = jnp.sum(jnp.where(lane < FOX_DH, prod, 0.0), axis=1, keepdims=True)
            delta_ref[1] = jnp.sum(jnp.where(lane >= FOX_DH, prod, 0.0), axis=1, keepdims=True)

        def step(masked):
            for hh in range(2):
                s, mask, sel = _fox_scores(q_ref[...], k_ref[...], fq_ref[hh], fk_ref[hh], hh, i * t, j * t, pad_rows,
                                           masked)
                p = _probs(s, mask, lse_ref[hh])
                dop = jnp.where(sel, do_ref[...], jnp.zeros_like(do_ref[...]))
                ds = p * (_dot(dop, v_ref[...], "nt") - delta_ref[hh])
                acc[hh] += _dot(ds.astype(BF16), k_ref[...], "nn")
                dfa[hh] += jnp.sum(ds, axis=1, keepdims=True)

        @pl.when(j <= i)
        def _():
            _both_variants((j == i) | (j == 0), step)

        @pl.when(j == i)
        def _():
            dq_ref[...] = jnp.where(lane < FOX_DH, acc[0], acc[1])
            dfq_ref[...] = dfa[...]

    qspec = pl.BlockSpec((t, LANE), lambda p, i, j: (i, p))
    kspec = pl.BlockSpec((t, LANE), lambda p, i, j: (jnp.minimum(j, i), p))
    cspec = pl.BlockSpec((2, t, 1), lambda p, i, j: (p, i, 0))
    rspec = pl.BlockSpec((2, 1, t), lambda p, i, j: (p, 0, jnp.minimum(j, i)))
    col = jax.ShapeDtypeStruct((8, lp, 1), F32)
    return _call(body, name="fox_bwd_q", grid=(FOX_W // LANE, n, n),
                 out_shape=[jax.ShapeDtypeStruct((lp, FOX_W), F32), col, col],
                 in_specs=[qspec, kspec, kspec, cspec, rspec, qspec, qspec, cspec], out_specs=[qspec, cspec, cspec],
                 scratch=[pltpu.VMEM((2, t, LANE), F32), pltpu.VMEM((2, t, 1), F32)],
                 sem=("parallel", "parallel", "arbitrary"))(q, k, v, fcol, frow, o, do, lse)


def _fox_bwd_kv(q, k, v, fcol, frow, do, lse, delta, pad_rows):
    lp = q.shape[0]
    t = _pick(lp, (640, 512, 256, 128))
    n = lp // t

    def body(q_ref, k_ref, v_ref, fq_ref, fk_ref, do_ref, lse_ref, delta_ref, dk_ref, dv_ref, dfk_ref, dka, dva, dfa):
        j, i = pl.program_id(1), pl.program_id(2)
        lane = _iota((t, LANE), 1)

        @pl.when(i == 0)
        def _():
            dka[...] = jnp.zeros_like(dka)
            dva[...] = jnp.zeros_like(dva)
            dfa[...] = jnp.zeros_like(dfa)

        def step(masked):
            for hh in range(2):
                s, mask, sel = _fox_scores(q_ref[...], k_ref[...], fq_ref[hh], fk_ref[hh], hh, i * t, j * t, pad_rows,
                                           masked)
                p = _probs(s, mask, lse_ref[hh])
                dop = jnp.where(sel, do_ref[...], jnp.zeros_like(do_ref[...]))
                ds = p * (_dot(dop, v_ref[...], "nt") - delta_ref[hh])
                dva[hh] += _dot(p.astype(BF16), do_ref[...], "tn")
                dka[hh] += _dot(ds.astype(BF16), q_ref[...], "tn")
                dfa[hh] -= jnp.sum(ds, axis=0, keepdims=True)

        @pl.when(i >= j)
        def _():
            _both_variants((j == i) | (j == 0), step)

        @pl.when(i == n - 1)
        def _():
            dk_ref[...] = jnp.where(lane < FOX_DH, dka[0], dka[1])
            dv_ref[...] = jnp.where(lane < FOX_DH, dva[0], dva[1])
            dfk_ref[...] = dfa[...]

    qspec = pl.BlockSpec((t, LANE), lambda p, j, i: (jnp.maximum(i, j), p))
    kspec = pl.BlockSpec((t, LANE), lambda p, j, i: (j, p))
    cspec = pl.BlockSpec((2, t, 1), lambda p, j, i: (p, jnp.maximum(i, j), 0))
    rspec = pl.BlockSpec((2, 1, t), lambda p, j, i: (p, 0, j))
    wide = jax.ShapeDtypeStruct((lp, FOX_W), F32)
    return _call(body, name="fox_bwd_kv", grid=(FOX_W // LANE, n, n),
                 out_shape=[wide, wide, jax.ShapeDtypeStruct((8, 1, lp), F32)],
                 in_specs=[qspec, kspec, kspec, cspec, rspec, qspec, cspec, cspec], out_specs=[kspec, kspec, rspec],
                 scratch=[pltpu.VMEM((2, t, LANE), F32), pltpu.VMEM((2, t, LANE), F32), pltpu.VMEM((2, 1, t), F32)],
                 sem=("parallel", "parallel", "arbitrary"))(q, k, v, fcol, frow, do, lse, delta)


def _gdn_chunk(q, k, v, beta, gcol, grow, s, inv):
    c = q.shape[-2]
    ri, ci = _iota((c, c), 0), _iota((c, c), 1)
    dec = jnp.exp(jnp.where(ri >= ci, gcol - grow, NEG))
    dec_strict = jnp.where(ri > ci, dec, 0.0)
    eg = jnp.exp(gcol)
    kb = k * beta
    t = inv(_mmb(kb, k, "nt") * dec_strict)
    u_hat = _mmh(t, v * beta, "nn")
    w = _mmh(t, kb * eg, "nn")
    u = u_hat - _mmb(w, s, "nn")
    o = _mmb(q * eg, s, "nn") + _mmb(_mmb(q, k, "nt") * dec, u, "nn")
    glast = jnp.sum(jnp.where(_iota((1, c), 1) == c - 1, grow, 0.0), axis=-1, keepdims=True)
    s_new = s * jnp.exp(glast) + _mmb(k * jnp.exp(glast - gcol), u, "tn")
    return o, s_new


def _gdn_specs(lp, reverse):
    n = lp // CHUNK
    pos = (lambda c: n - 1 - c) if reverse else (lambda c: c)
    wide = pl.BlockSpec((CHUNK, GDN_W), lambda c: (pos(c), 0))
    col = pl.BlockSpec((GDN_H, CHUNK, 1), lambda c: (0, pos(c), 0))
    row = pl.BlockSpec((GDN_H, 1, 1, CHUNK), lambda c: (0, pos(c), 0, 0))
    st = pl.BlockSpec((GDN_H, 1, GDN_DH, GDN_DH), lambda c: (0, pos(c), 0, 0))
    return n, wide, col, row, st


def _heads(ref):
    return jnp.stack([ref[:, h * GDN_DH:(h + 1) * GDN_DH] for h in range(GDN_H)])


def _put_heads(ref, val):
    for h in range(GDN_H):
        ref[:, h * GDN_DH:(h + 1) * GDN_DH] = val[h]


def _gdn_fwd(q, k, v, bcol, gcol, grow):
    lp = q.shape[0]
    n, wide, col, row, st = _gdn_specs(lp, False)

    def body(q_ref, k_ref, v_ref, b_ref, gc_ref, gr_ref, o_ref, sp_ref, s_scr):
        @pl.when(pl.program_id(0) == 0)
        def _():
            s_scr[...] = jnp.zeros_like(s_scr)

        s = s_scr[...]
        sp_ref[:, 0] = s
        o, s_new = _gdn_chunk(_heads(q_ref), _heads(k_ref), _heads(v_ref), b_ref[...], gc_ref[...], gr_ref[:, 0], s,
                              _inv_unit_lower_raw)
        _put_heads(o_ref, o)
        s_scr[...] = s_new

    return _call(body, name="gdn_fwd", grid=(n,),
                 out_shape=[jax.ShapeDtypeStruct((lp, GDN_W), F32),
                            jax.ShapeDtypeStruct((GDN_H, n, GDN_DH, GDN_DH), F32)],
                 in_specs=[wide, wide, wide, col, col, row], out_specs=[wide, st],
                 scratch=[pltpu.VMEM((GDN_H, GDN_DH, GDN_DH), F32)], sem=("arbitrary",))(q, k, v, bcol, gcol, grow)


def _gdn_bwd(q, k, v, bcol, gcol, grow, sprev, do):
    lp = q.shape[0]
    n, wide, col, row, st = _gdn_specs(lp, True)

    def body(q_ref, k_ref, v_ref, b_ref, gc_ref, gr_ref, sp_ref, do_ref,
             dq_ref, dk_ref, dv_ref, db_ref, dgc_ref, dgr_ref, ds_scr):
        @pl.when(pl.program_id(0) == 0)
        def _():
            ds_scr[...] = jnp.zeros_like(ds_scr)

        fn = functools.partial(_gdn_chunk, inv=_inv_unit_lower)
        _, vjp = jax.vjp(fn, _heads(q_ref), _heads(k_ref), _heads(v_ref), b_ref[...], gc_ref[...], gr_ref[:, 0],
                         sp_ref[:, 0])
        dq, dk, dv, db, dgc, dgr, ds = vjp((_heads(do_ref), ds_scr[...]))
        _put_heads(dq_ref, dq)
        _put_heads(dk_ref, dk)
        _put_heads(dv_ref, dv)
        db_ref[...] = db
        dgc_ref[...] = dgc
        dgr_ref[:, 0] = dgr
        ds_scr[...] = ds

    wshape = jax.ShapeDtypeStruct((lp, GDN_W), F32)
    cshape = jax.ShapeDtypeStruct((GDN_H, lp, 1), F32)
    return _call(body, name="gdn_bwd", grid=(n,),
                 out_shape=[wshape, wshape, wshape, cshape, cshape, jax.ShapeDtypeStruct((GDN_H, n, 1, CHUNK), F32)],
                 in_specs=[wide, wide, wide, col, col, row, st, wide], out_specs=[wide, wide, wide, col, col, row],
                 scratch=[pltpu.VMEM((GDN_H, GDN_DH, GDN_DH), F32)], sem=("arbitrary",))(
                     q, k, v, bcol, gcol, grow, sprev, do)


def _halo_prev(width, colblock, tm):
    return pl.BlockSpec((8, width), lambda i, cb=colblock: (jnp.maximum(i * (tm // 8) - 1, 0), cb))


def _gdn_act(proj, small, conv_w, alog_row, dtb_row, tm=ROW_TILE):
    lp = proj.shape[0]

    def body(xq, xk, xv, hq, hk, hv, wq, wk, wv, sm, al, dt, q_ref, k_ref, v_ref, bg_ref):
        first = (pl.program_id(0) > 0).astype(F32)
        cs = [_causal_conv(x[...], h[...] * first, w[...]) for x, h, w in ((xq, hq, wq), (xk, hk, wk), (xv, hv, wv))]
        q, k, v, bg = _gdn_act_fn(cs[0], cs[1], cs[2], sm[...], al[...], dt[...])
        q_ref[...], k_ref[...], v_ref[...], bg_ref[...] = q, k, v, bg

    wide = jax.ShapeDtypeStruct((lp, GDN_W), F32)
    wspec = [pl.BlockSpec((4, GDN_W), lambda i, c=c: (0, c)) for c in range(3)]
    return _call(body, name="gdn_act", grid=(lp // tm,),
                 out_shape=[wide, wide, wide, jax.ShapeDtypeStruct((lp, LANE), F32)],
                 in_specs=[_row_spec(GDN_W, c, tm) for c in range(3)] + [_halo_prev(GDN_W, c, tm) for c in range(3)]
                 + wspec + [_row_spec(LANE, 0, tm), _full_spec(alog_row), _full_spec(dtb_row)],
                 out_specs=[_row_spec(GDN_W, 0, tm)] * 3 + [_row_spec(LANE, 0, tm)], sem=("parallel",))(
                     proj, proj, proj, proj, proj, proj, conv_w, conv_w, conv_w, small, alog_row, dtb_row)


def _gdn_act_bwd(proj, small, conv_w, alog_row, dtb_row, dq, dk, dv, dbg, tm=ROW_TILE):
    lp = proj.shape[0]

    def body(xq, xk, xv, hq, hk, hv, wq, wk, wv, sm, al, dt, dq_r, dk_r, dv_r, dbg_r,
             dc_ref, dsm_ref, dal_ref, ddt_ref, dw_ref):
        i = pl.program_id(0)
        first = (i > 0).astype(F32)
        xs = [(x[...], h[...] * first, w[...]) for x, h, w in ((xq, hq, wq), (xk, hk, wk), (xv, hv, wv))]
        cs = [_causal_conv(*t) for t in xs]
        _, vjp = jax.vjp(_gdn_act_fn, cs[0], cs[1], cs[2], sm[...], al[...], dt[...])
        dcq, dck, dcv, dsm, dal, ddt = vjp((dq_r[...], dk_r[...], dv_r[...], dbg_r[...]))
        dsm_ref[...] = dsm

        @pl.when(i == 0)
        def _():
            dal_ref[...] = jnp.zeros_like(dal_ref)
            ddt_ref[...] = jnp.zeros_like(ddt_ref)
            dw_ref[...] = jnp.zeros_like(dw_ref)

        dal_ref[...] += dal
        ddt_ref[...] += ddt
        for c, (dc, (x, h, w)) in enumerate(zip((dcq, dck, dcv), xs)):
            dc_ref[:, c * GDN_W:(c + 1) * GDN_W] = dc
            rows = [jnp.sum(_shift_down(x, h, 3 - kk) * dc, axis=0, keepdims=True) for kk in range(4)]
            dw_ref[:, c * GDN_W:(c + 1) * GDN_W] += jnp.concatenate(rows, axis=0)

    wspec = [pl.BlockSpec((4, GDN_W), lambda i, c=c: (0, c)) for c in range(3)]
    row128 = jax.ShapeDtypeStruct((1, LANE), F32)
    return _call(body, name="gdn_act_bwd", grid=(lp // tm,),
                 out_shape=[jax.ShapeDtypeStruct((lp, 3 * GDN_W), F32), jax.ShapeDtypeStruct((lp, LANE), F32),
                            row128, row128, jax.ShapeDtypeStruct((4, 3 * GDN_W), F32)],
                 in_specs=[_row_spec(GDN_W, c, tm) for c in range(3)] + [_halo_prev(GDN_W, c, tm) for c in range(3)]
                 + wspec + [_row_spec(LANE, 0, tm), _full_spec(alog_row), _full_spec(dtb_row)]
                 + [_row_spec(GDN_W, 0, tm)] * 3 + [_row_spec(LANE, 0, tm)],
                 out_specs=[_row_spec(3 * GDN_W, 0, tm), _row_spec(LANE, 0, tm),
                            _full_spec(alog_row), _full_spec(dtb_row), pl.BlockSpec((4, 3 * GDN_W), lambda i: (0, 0))],
                 sem=("arbitrary",))(proj, proj, proj, proj, proj, proj, conv_w, conv_w, conv_w, small,
                                     alog_row, dtb_row, dq, dk, dv, dbg)


def _ffn_act(up_pre, conv_w, tm=ROW_TILE):
    lp = up_pre.shape[0]

    def body(xg, xv, hg, hv, wg, wv, a_ref):
        first = (pl.program_id(0) > 0).astype(F32)
        ug = _causal_conv(xg[...], hg[...] * first, wg[...])
        uv = _causal_conv(xv[...], hv[...] * first, wv[...])
        a_ref[...] = (_silu(ug) * uv).astype(a_ref.dtype)

    wspec = [pl.BlockSpec((3, D_FF), lambda i, c=c: (0, c)) for c in range(2)]
    return _call(body, name="ffn_act", grid=(lp // tm,), out_shape=jax.ShapeDtypeStruct((lp, D_FF), BF16),
                 in_specs=[_row_spec(D_FF, c, tm) for c in range(2)] + [_halo_prev(D_FF, c, tm) for c in range(2)] + wspec,
                 out_specs=_row_spec(D_FF, 0, tm), sem=("parallel",))(up_pre, up_pre, up_pre, up_pre, conv_w, conv_w)


def _ffn_act_bwd(up_pre, conv_w, dact, tm=ROW_TILE):
    lp = up_pre.shape[0]

    def body(xg, xv, hg, hv, wg, wv, da, du_ref, dw_ref):
        i = pl.program_id(0)
        first = (i > 0).astype(F32)
        xs = [(x[...], h[...] * first, w[...]) for x, h, w in ((xg, hg, wg), (xv, hv, wv))]
        ug, uv = [_causal_conv(*t) for t in xs]
        _, vjp = jax.vjp(lambda a, b: _silu(a) * b, ug, uv)
        dus = vjp(da[...].astype(F32))

        @pl.when(i == 0)
        def _():
            dw_ref[...] = jnp.zeros_like(dw_ref)

        for c, (du, (x, h, w)) in enumerate(zip(dus, xs)):
            du_ref[:, c * D_FF:(c + 1) * D_FF] = du
            rows = [jnp.sum(_shift_down(x, h, 2 - kk) * du, axis=0, keepdims=True) for kk in range(3)]
            dw_ref[:, c * D_FF:(c + 1) * D_FF] += jnp.concatenate(rows, axis=0)

    wspec = [pl.BlockSpec((3, D_FF), lambda i, c=c: (0, c)) for c in range(2)]
    return _call(body, name="ffn_act_bwd", grid=(lp // tm,),
                 out_shape=[jax.ShapeDtypeStruct((lp, 2 * D_FF), F32), jax.ShapeDtypeStruct((3, 2 * D_FF), F32)],
                 in_specs=[_row_spec(D_FF, c, tm) for c in range(2)] + [_halo_prev(D_FF, c, tm) for c in range(2)]
                 + wspec + [_row_spec(D_FF, 0, tm)],
                 out_specs=[_row_spec(2 * D_FF, 0, tm), pl.BlockSpec((3, 2 * D_FF), lambda i: (0, 0))],
                 sem=("arbitrary",))(up_pre, up_pre, up_pre, up_pre, conv_w, conv_w, dact)


def _conv_bwd_x(dy, w, pad_rows, width, name, tm=ROW_TILE):
    lp, ctot = dy.shape
    nt = lp // tm
    kk = w.shape[0]

    def body(d_ref, h_ref, w_ref, o_ref):
        i = pl.program_id(0)
        last = (i < nt - 1).astype(F32)
        d, h, wv = d_ref[...], h_ref[...] * last, w_ref[...]
        y = d * wv[kk - 1:kk, :]
        for k in range(kk - 1):
            y = y + _shift_up(d, h, kk - 1 - k) * wv[k:k + 1, :]
        valid = (i * tm + _iota((tm, 1), 0)) >= pad_rows
        o_ref[...] = jnp.where(valid, y, 0.0).astype(o_ref.dtype)

    return _call(body, name=name, grid=(nt, ctot // width), out_shape=jax.ShapeDtypeStruct((lp, ctot), BF16),
                 in_specs=[pl.BlockSpec((tm, width), lambda i, c: (i, c)),
                           pl.BlockSpec((8, width), lambda i, c: (jnp.minimum((i + 1) * (tm // 8), lp // 8 - 1), c)),
                           pl.BlockSpec((kk, width), lambda i, c: (0, c))],
                 out_specs=pl.BlockSpec((tm, width), lambda i, c: (i, c)), sem=("parallel", "parallel"))(dy, dy, w)


def _loss_head(h_res, target, row_start, tm=ROW_TILE):
    lp, d = h_res.shape
    t0 = row_start // tm

    def body(h_ref, t_ref, dy_ref, loss_ref):
        i = pl.program_id(0)

        @pl.when(i == 0)
        def _():
            loss_ref[...] = jnp.zeros_like(loss_ref)

        live = (i >= t0).astype(F32)
        err = (h_ref[...] - t_ref[...]) * live
        dy_ref[...] = err * (1.0 / d)
        loss_ref[...] += 0.5 / d * jnp.sum(err * err)

    return _call(body, name="loss_head", grid=(lp // tm,),
                 out_shape=[jax.ShapeDtypeStruct((lp, d), F32), jax.ShapeDtypeStruct((8, LANE), F32)],
                 in_specs=[pl.BlockSpec((tm, d), lambda i: (i, 0)),
                           pl.BlockSpec((tm, d), lambda i: (jnp.maximum(i - t0, 0), 0))],
                 out_specs=[pl.BlockSpec((tm, d), lambda i: (i, 0)), pl.BlockSpec((8, LANE), lambda i: (0, 0))],
                 sem=("arbitrary",))(h_res, target)


def _sum_adamw(parts, w, m, v, name):
    a, r, c = w.shape
    tm = _pick(r, (256, 128, 64, 32, 16))
    bc1 = 1.0 - ADAM_B1 ** ADAM_STEP
    bc2 = 1.0 - ADAM_B2 ** ADAM_STEP

    def body(p_ref, w_ref, m_ref, v_ref, g_ref, d_ref, nm_ref, nv_ref):
        g = p_ref[0, 0].astype(F32)
        for s in range(1, N_DEV):
            g = g + p_ref[s, 0].astype(F32)
        nm = ADAM_B1 * m_ref[0] + (1.0 - ADAM_B1) * g
        nv = ADAM_B2 * v_ref[0] + (1.0 - ADAM_B2) * (g * g)
        g_ref[0] = g
        nm_ref[0] = nm
        nv_ref[0] = nv
        d_ref[0] = -ADAM_LR * ((nm / bc1) / (jnp.sqrt(nv / bc2) + ADAM_EPS) + ADAM_WD * w_ref[0])

    spec = pl.BlockSpec((1, tm, c), lambda l, i: (l, i, 0))
    shp = jax.ShapeDtypeStruct((a, r, c), F32)
    return _call(body, name=name, grid=(a, r // tm), out_shape=[shp] * 4,
                 in_specs=[pl.BlockSpec((N_DEV, 1, tm, c), lambda l, i: (0, l, i, 0)), spec, spec, spec],
                 out_specs=[spec] * 4, sem=("parallel", "parallel"))(parts, w, m, v)


_ANY = pl.BlockSpec(memory_space=pl.ANY)


def _all_gather(block, name):
    def body(x_ref, out_ref, send_sems, recv_sems, local_sem):
        x, y, c = lax.axis_index("x"), lax.axis_index("y"), lax.axis_index("c")
        me, sibling = (x, y, c), (x, y, 1 - c)
        chips = [(1 - x, y), (x, 1 - y), (1 - x, 1 - y)]

        def slot(px, py, pc):
            return out_ref.at[4 * px + 2 * py + pc]

        def copy(k, blk, to, src=None):
            return pltpu.make_async_remote_copy(
                src_ref=slot(*blk) if src is None else src, dst_ref=slot(*blk),
                send_sem=send_sems.at[k], recv_sem=recv_sems.at[k], device_id=to, device_id_type=MESH)

        mine = pltpu.make_async_copy(x_ref, slot(*me), local_sem)
        mine.start()
        first = [copy(0, me, sibling, src=x_ref)]
        first += [copy(1 + j, me, (*chip, c), src=x_ref) for j, chip in enumerate(chips)]
        for cp in first:
            cp.start()
        passed = [copy(4 + j, (*chip, c), sibling) for j, chip in enumerate(chips)]
        for j, chip in enumerate(chips):
            copy(1 + j, (*chip, c), me).wait_recv()
            passed[j].start()
        copy(0, sibling, me).wait_recv()
        for j, chip in enumerate(chips):
            copy(4 + j, (*chip, 1 - c), me).wait_recv()
        for cp in first + passed:
            cp.wait_send()
        mine.wait()

    return pl.pallas_call(
        body, name=name, out_shape=jax.ShapeDtypeStruct((N_DEV,) + block.shape, block.dtype),
        in_specs=[_ANY], out_specs=_ANY,
        scratch_shapes=[pltpu.SemaphoreType.DMA((7,)), pltpu.SemaphoreType.DMA((7,)), pltpu.SemaphoreType.DMA],
    )(block)


def _all_to_all(src, name):
    def body(s_ref, o_ref, send_sems, recv_sems, local_sem):
        x, y, c = lax.axis_index("x"), lax.axis_index("y"), lax.axis_index("c")
        me = 4 * x + 2 * y + c
        mine = pltpu.make_async_copy(s_ref.at[me], o_ref.at[me], local_sem)
        mine.start()
        copies = []
        for k in range(1, N_DEV):
            px = 1 - x if k & 4 else x
            py = 1 - y if k & 2 else y
            pc = 1 - c if k & 1 else c
            peer = 4 * px + 2 * py + pc
            copies.append((pltpu.make_async_remote_copy(
                src_ref=s_ref.at[peer], dst_ref=o_ref.at[me], send_sem=send_sems.at[k - 1],
                recv_sem=recv_sems.at[k - 1], device_id=(px, py, pc), device_id_type=MESH), peer, k))
        for cp, _, _ in copies:
            cp.start()
        for cp, peer, k in copies:
            cp.wait_send()
            pltpu.make_async_remote_copy(
                src_ref=s_ref.at[peer], dst_ref=o_ref.at[peer], send_sem=send_sems.at[k - 1],
                recv_sem=recv_sems.at[k - 1], device_id=(x, y, c), device_id_type=MESH).wait_recv()
        mine.wait()

    return pl.pallas_call(
        body, name=name, out_shape=jax.ShapeDtypeStruct(src.shape, src.dtype), in_specs=[_ANY], out_specs=_ANY,
        scratch_shapes=[pltpu.SemaphoreType.DMA((7,)), pltpu.SemaphoreType.DMA((7,)), pltpu.SemaphoreType.DMA],
    )(src)


def _pack(blocks, width, dtype, row_mult):
    flat = jnp.concatenate([b.astype(dtype).reshape(-1) for b in blocks])
    per = width * row_mult
    total = -(-flat.shape[0] // per) * per
    return jnp.pad(flat, (0, total - flat.shape[0])).reshape(total // width, width)


def _pack_dest(fulls, axes, width, dtype, row_mult):
    rows = []
    for f, ax in zip(fulls, axes):
        f = f.astype(dtype)
        if ax is None:
            rows.append(jnp.broadcast_to(f.reshape(1, -1), (N_DEV, f.size)))
        else:
            shp = f.shape
            f = f.reshape(shp[:ax] + (N_DEV, shp[ax] // N_DEV) + shp[ax + 1:])
            rows.append(jnp.moveaxis(f, ax, 0).reshape(N_DEV, -1))
    flat = jnp.concatenate(rows, axis=1)
    per = width * row_mult
    total = -(-flat.shape[1] // per) * per
    return jnp.pad(flat, ((0, 0), (0, total - flat.shape[1]))).reshape(N_DEV, total // width, width)


def _unpack(packed, shapes):
    flat = packed.reshape(-1)
    out, off = [], 0
    for s in shapes:
        n = 1
        for d in s:
            n *= d
        out.append(flat[off:off + n].reshape(s))
        off += n
    return out


def _unpack_gathered(gathered, shapes, axes):
    flat = gathered.reshape(N_DEV, -1)
    out, off = [], 0
    for s, ax in zip(shapes, axes):
        n = 1
        for d in s:
            n *= d
        blk = jnp.moveaxis(flat[:, off:off + n].reshape((N_DEV,) + tuple(s)), 0, ax)
        out.append(blk.reshape(tuple(s[:ax]) + (N_DEV * s[ax],) + tuple(s[ax + 1:])))
        off += n
    return out


def _shard_cols(blocks, a, b):
    shard = blocks[0].shape[1]
    out = []
    while a < b:
        d = a // shard
        hi = min(b, (d + 1) * shard)
        out.append(blocks[d][:, a - d * shard:hi - d * shard])
        a = hi
    return out


def _permute_w_in(blocks):
    main = jnp.concatenate(_shard_cols(blocks, O_GQ, O_BL) + _shard_cols(blocks, O_GZ, O_END)
                           + _shard_cols(blocks, O_FQ, O_FL), axis=1)
    pad = jnp.zeros((blocks[0].shape[0], LANE - 24), blocks[0].dtype)
    small = jnp.concatenate(_shard_cols(blocks, O_FL, O_GQ) + _shard_cols(blocks, O_BL, O_GZ) + [pad], axis=1)
    return main, small


_W_IN_SEGS = ((O_FQ, O_FL, True, C_FQ), (O_FL, O_GQ, False, 0), (O_GQ, O_BL, True, C_GQ), (O_BL, O_GZ, False, 8),
              (O_GZ, O_END, True, C_GZ))


def _unpermute_cols(main, small, a, b):
    out = []
    for s0, s1, is_main, t0 in _W_IN_SEGS:
        lo, hi = max(a, s0), min(b, s1)
        if lo < hi:
            out.append((main if is_main else small)[:, t0 + lo - s0:t0 + hi - s0])
    return jnp.concatenate(out, axis=1)


def _lanes(vec, start):
    return jnp.pad(vec.astype(F32), (start, LANE - start - vec.shape[0])).reshape(1, LANE)


BIG = ("w_in", "w_branch_a", "w_branch_b", "w_out", "w_up", "w_down")
BIG_AXES = (2, 2, 1, 1, 2, 1)
SHARDED_SMALL = ("meta_tokens", "gdn_conv_w", "ffn_conv_w")
SHARDED_SMALL_AXES = (1, 2, 2)
REPL = ("norm1_g", "fox_f_bias", "fox_q_norm_g", "fox_k_norm_g", "gdn_a_log", "gdn_dt_bias", "gdn_norm_g", "norm2_g")
ORDER = ("meta_tokens", "norm1_g", "w_in", "fox_f_bias", "fox_q_norm_g", "fox_k_norm_g", "gdn_conv_w", "gdn_a_log",
         "gdn_dt_bias", "gdn_norm_g", "w_branch_a", "w_branch_b", "w_out", "norm2_g", "w_up", "ffn_conv_w", "w_down")


def _layer_fwd(h_res, wl, pad_rows):
    lp = h_res.shape[0]
    sv = {"res_in": h_res}
    (h1,) = _rowwise(_rmsnorm_fn, [(h_res, D, 0)], [wl["norm1_g"]], [(D, BF16)], "rmsnorm1")
    proj = _matmul(h1, wl["w_main"], "nn", F32, name="mm_in")
    small = _matmul(h1, wl["w_small"], "nn", F32, name="mm_in_small")
    sv.update(h1=h1, proj=proj, small=small)

    fox_fn = functools.partial(_fox_prep_fn, pad_rows)
    qh, kh, logf = _rowwise(fox_fn, [(proj, FOX_W, C_FQ // FOX_W), (proj, FOX_W, C_FK // FOX_W), (small, LANE, 0)],
                            [wl["qg"], wl["kg"], wl["fb"]], [(FOX_W, BF16), (FOX_W, BF16), (LANE, F32)], "fox_prep")
    (vh,) = _rowwise(lambda a, row0: (a,), [(proj, FOX_W, C_FV // FOX_W)], [], [(FOX_W, BF16)], "fox_v")
    fsum = _cumsum_rows(logf, False, "fox_cumsum")
    f8 = fsum[:, :8].T
    fcol, frow = f8.reshape(8, lp, 1), f8.reshape(8, 1, lp)
    o_a, lse = _fox_fwd(qh, kh, vh, fcol, frow, pad_rows)
    y_a = _matmul(o_a, wl["w_branch_a"], "nn", F32, name="mm_branch_a")
    sv.update(qh=qh, kh=kh, vh=vh, fcol=fcol, frow=frow, o_a=o_a, lse=lse)

    gq, gk, gv, bg = _gdn_act(proj, small, wl["gdn_conv_w"], wl["alog"], wl["dtb"])
    bcol = bg[:, 8:16].T.reshape(8, lp, 1)
    g8 = bg[:, 16:24].T
    gcol, grow = g8.reshape(8, lp, 1), g8.reshape(8, lp // CHUNK, 1, CHUNK)
    o_raw, sprev = _gdn_fwd(gq, gk, gv, bcol, gcol, grow)
    (o_b,) = _rowwise(_gdn_post_fn, [(o_raw, GDN_W, 0), (proj, GDN_W, C_GZ // GDN_W)], [wl["gn"]], [(GDN_W, BF16)],
                      "gdn_post")
    y_b = _matmul(o_b, wl["w_branch_b"], "nn", F32, name="mm_branch_b")
    sv.update(gq=gq, gk=gk, gv=gv, bcol=bcol, gcol=gcol, grow=grow, o_raw=o_raw, sprev=sprev, o_b=o_b)

    (mixed,) = _rowwise(_merge_fn, [(proj, D, C_GATE // D), (proj, D, C_GATE // D + 1), (y_a, D, 0), (y_b, D, 0)], [],
                        [(D, BF16)], "merge")
    res_mid = _matmul(mixed, wl["w_out"], "nn", F32, add=h_res, name="mm_out")
    sv.update(y_a=y_a, y_b=y_b, mixed=mixed, res_mid=res_mid)

    (h2,) = _rowwise(_rmsnorm_fn, [(res_mid, D, 0)], [wl["norm2_g"]], [(D, BF16)], "rmsnorm2")
    up_pre = _matmul(h2, wl["w_up"], "nn", F32, name="mm_up")
    act = _ffn_act(up_pre, wl["ffn_conv_w"])
    out = _matmul(act, wl["w_down"], "nn", F32, add=res_mid, name="mm_down")
    sv.update(h2=h2, up_pre=up_pre, act=act)
    return out, sv


def _layer_bwd(dres, wl, sv, pad_rows):
    lp = dres.shape[0]
    gw = {}
    gw["w_down"] = _matmul(sv["act"], dres, "tn", F32, name="mm_dw_down")
    dact = _matmul(dres, wl["w_down"], "nt", BF16, name="mm_dact")
    dup, gw["ffn_conv_w"] = _ffn_act_bwd(sv["up_pre"], wl["ffn_conv_w"], dact)
    dup_pre = _conv_bwd_x(dup, wl["ffn_conv_w"], pad_rows, D_FF, "ffn_conv_bwd")
    gw["w_up"] = _matmul(sv["h2"], dup_pre, "tn", F32, name="mm_dw_up")
    dh2 = _matmul(dup_pre, wl["w_up"], "nt", F32, name="mm_dh2")
    (dmid,), (gw["norm2_g"],) = _rowwise_bwd(_rmsnorm_fn, [(sv["res_mid"], D, 0)], [wl["norm2_g"]], [(dh2, D, 0)],
                                             "rmsnorm2_bwd", pad_rows, [F32], adds=[(dres, D, 0)])
    gw["w_out"] = _matmul(sv["mixed"], dmid, "tn", F32, name="mm_dw_out")
    dmixed = _matmul(dmid, wl["w_out"], "nt", F32, name="mm_dmixed")
    proj, small = sv["proj"], sv["small"]
    (dg0, dg1, dya, dyb), _ = _rowwise_bwd(
        _merge_fn, [(proj, D, C_GATE // D), (proj, D, C_GATE // D + 1), (sv["y_a"], D, 0), (sv["y_b"], D, 0)], [],
        [(dmixed, D, 0)], "merge_bwd", pad_rows, [BF16, BF16, BF16, BF16])
    gw["w_branch_a"] = _matmul(sv["o_a"], dya, "tn", F32, name="mm_dw_a")
    do_a = _matmul(dya, wl["w_branch_a"], "nt", BF16, name="mm_do_a")
    gw["w_branch_b"] = _matmul(sv["o_b"], dyb, "tn", F32, name="mm_dw_b")
    do_b = _matmul(dyb, wl["w_branch_b"], "nt", F32, name="mm_do_b")

    (do_raw, dgz), (gw["gn"],) = _rowwise_bwd(_gdn_post_fn, [(sv["o_raw"], GDN_W, 0), (proj, GDN_W, C_GZ // GDN_W)],
                                              [wl["gn"]], [(do_b, GDN_W, 0)], "gdn_post_bwd", pad_rows, [F32, BF16])
    dgq, dgk, dgv, dbcol, dgcol, dgrow = _gdn_bwd(sv["gq"], sv["gk"], sv["gv"], sv["bcol"], sv["gcol"], sv["grow"],
                                                   sv["sprev"], do_raw)
    db8 = dbcol.reshape(8, lp).T
    dg8 = (dgcol.reshape(8, lp) + dgrow.reshape(8, lp)).T
    dbg = jnp.concatenate([jnp.zeros((lp, 8), F32), db8, dg8, jnp.zeros((lp, LANE - 24), F32)], axis=1)
    dconv, dsmall_g, gw["alog"], gw["dtb"], gw["gdn_conv_w"] = _gdn_act_bwd(
        proj, small, wl["gdn_conv_w"], wl["alog"], wl["dtb"], dgq, dgk, dgv, dbg)
    dqkv = _conv_bwd_x(dconv, wl["gdn_conv_w"], pad_rows, GDN_W, "gdn_conv_bwd")

    dqh, dfq, delta = _fox_bwd_q(sv["qh"], sv["kh"], sv["vh"], sv["fcol"], sv["frow"], sv["o_a"], do_a, sv["lse"],
                                 pad_rows)
    dkh, dvh, dfk = _fox_bwd_kv(sv["qh"], sv["kh"], sv["vh"], sv["fcol"], sv["frow"], do_a, sv["lse"], delta, pad_rows)
    df8 = (dfq.reshape(8, lp) + dfk.reshape(8, lp)).T
    dlogf = _cumsum_rows(jnp.pad(df8, ((0, 0), (0, LANE - 8))), True, "fox_cumsum_bwd")
    fox_fn = functools.partial(_fox_prep_fn, pad_rows)
    (dfq_p, dfk_p, dsmall_f), (gw["qg"], gw["kg"], gw["fb"]) = _rowwise_bwd(
        fox_fn, [(proj, FOX_W, C_FQ // FOX_W), (proj, FOX_W, C_FK // FOX_W), (small, LANE, 0)],
        [wl["qg"], wl["kg"], wl["fb"]], [(dqh, FOX_W, 0), (dkh, FOX_W, 0), (dlogf, LANE, 0)],
        "fox_prep_bwd", pad_rows, [BF16, BF16, F32], adds=[None, None, (dsmall_g, LANE, 0)])

    dproj = jnp.concatenate([dqkv, dgz, dg0, dg1, dfq_p, dfk_p, dvh.astype(BF16)], axis=1)
    gw["w_main"] = _matmul(sv["h1"], dproj, "tn", F32, name="mm_dw_main")
    gw["w_small"] = _matmul(sv["h1"], dsmall_f, "tn", F32, name="mm_dw_small")
    dh1 = _matmul(dproj, wl["w_main"], "nt", F32, name="mm_dh1")
    dh1 = _matmul(dsmall_f, wl["w_small"], "nt", F32, add=dh1, name="mm_dh1_small")
    (din,), (gw["norm1_g"],) = _rowwise_bwd(_rmsnorm_fn, [(sv["res_in"], D, 0)], [wl["norm1_g"]], [(dh1, D, 0)],
                                            "rmsnorm1_bwd", pad_rows, [F32], adds=[(dmid, D, 0)])
    return din, gw


def kernel(x, meta_tokens, norm1_g, w_in, fox_f_bias, fox_q_norm_g, fox_k_norm_g, gdn_conv_w, gdn_a_log, gdn_dt_bias, gdn_norm_g, w_branch_a, w_branch_b, w_out, norm2_g, w_up, ffn_conv_w, w_down, loss_target, m_meta_tokens, m_norm1_g, m_w_in, m_fox_f_bias, m_fox_q_norm_g, m_fox_k_norm_g, m_gdn_conv_w, m_gdn_a_log, m_gdn_dt_bias, m_gdn_norm_g, m_w_branch_a, m_w_branch_b, m_w_out, m_norm2_g, m_w_up, m_ffn_conv_w, m_w_down, v_meta_tokens, v_norm1_g, v_w_in, v_fox_f_bias, v_fox_q_norm_g, v_fox_k_norm_g, v_gdn_conv_w, v_gdn_a_log, v_gdn_dt_bias, v_gdn_norm_g, v_w_branch_a, v_w_branch_b, v_w_out, v_norm2_g, v_w_up, v_ffn_conv_w, v_w_down):
    w = dict(meta_tokens=meta_tokens, norm1_g=norm1_g, w_in=w_in, fox_f_bias=fox_f_bias, fox_q_norm_g=fox_q_norm_g,
             fox_k_norm_g=fox_k_norm_g, gdn_conv_w=gdn_conv_w, gdn_a_log=gdn_a_log, gdn_dt_bias=gdn_dt_bias,
             gdn_norm_g=gdn_norm_g, w_branch_a=w_branch_a, w_branch_b=w_branch_b, w_out=w_out, norm2_g=norm2_g,
             w_up=w_up, ffn_conv_w=ffn_conv_w, w_down=w_down)
    mom = dict(meta_tokens=m_meta_tokens, norm1_g=m_norm1_g, w_in=m_w_in, fox_f_bias=m_fox_f_bias,
               fox_q_norm_g=m_fox_q_norm_g, fox_k_norm_g=m_fox_k_norm_g, gdn_conv_w=m_gdn_conv_w,
               gdn_a_log=m_gdn_a_log, gdn_dt_bias=m_gdn_dt_bias, gdn_norm_g=m_gdn_norm_g, w_branch_a=m_w_branch_a,
               w_branch_b=m_w_branch_b, w_out=m_w_out, norm2_g=m_norm2_g, w_up=m_w_up, ffn_conv_w=m_ffn_conv_w,
               w_down=m_w_down)
    var = dict(meta_tokens=v_meta_tokens, norm1_g=v_norm1_g, w_in=v_w_in, fox_f_bias=v_fox_f_bias,
               fox_q_norm_g=v_fox_q_norm_g, fox_k_norm_g=v_fox_k_norm_g, gdn_conv_w=v_gdn_conv_w,
               gdn_a_log=v_gdn_a_log, gdn_dt_bias=v_gdn_dt_bias, gdn_norm_g=v_gdn_norm_g, w_branch_a=v_w_branch_a,
               w_branch_b=v_w_branch_b, w_out=v_w_out, norm2_g=v_norm2_g, w_up=v_w_up, ffn_conv_w=v_ffn_conv_w,
               w_down=v_w_down)
    depth = norm1_g.shape[0]
    seq = x.shape[1]
    l_tok = N_META + seq
    lp = -(-l_tok // LANE) * LANE
    pad_rows = lp - l_tok
    row_start = pad_rows + N_META

    got = {n: _all_gather(w[n].astype(BF16), "gather_" + n) for n in BIG}
    small_shapes = [w[n].shape for n in SHARDED_SMALL]
    gathered_s = _all_gather(_pack([w[n] for n in SHARDED_SMALL], LANE, F32, 8), "gather_small")
    full = dict(zip(SHARDED_SMALL, _unpack_gathered(gathered_s, small_shapes, SHARDED_SMALL_AXES)))

    def join(name, l):
        return jnp.concatenate([got[name][d, l] for d in range(N_DEV)], axis=BIG_AXES[BIG.index(name)] - 1)

    layers = []
    for l in range(depth):
        w_main, w_small = _permute_w_in([got["w_in"][d, l] for d in range(N_DEV)])
        layers.append(dict(
            w_main=w_main, w_small=w_small, w_branch_a=join("w_branch_a", l), w_branch_b=join("w_branch_b", l),
            w_out=join("w_out", l), w_up=join("w_up", l), w_down=join("w_down", l),
            gdn_conv_w=full["gdn_conv_w"][l], ffn_conv_w=full["ffn_conv_w"][l],
            norm1_g=norm1_g[l].reshape(1, D), norm2_g=norm2_g[l].reshape(1, D),
            qg=jnp.tile(fox_q_norm_g[l], 8).reshape(1, FOX_W), kg=jnp.tile(fox_k_norm_g[l], 8).reshape(1, FOX_W),
            fb=_lanes(fox_f_bias[l], 0), alog=_lanes(gdn_a_log[l], 16), dtb=_lanes(gdn_dt_bias[l], 16),
            gn=jnp.tile(gdn_norm_g[l], 8).reshape(1, GDN_W)))

    h_res = jnp.concatenate([jnp.zeros((pad_rows, D), F32), full["meta_tokens"], x[0]], axis=0)
    saved = []
    for l in range(depth):
        h_res, sv = _layer_fwd(h_res, layers[l], pad_rows)
        saved.append(sv)
    dres, loss_part = _loss_head(h_res, loss_target[0], row_start)
    loss = lax.psum(loss_part[0, 0], ("x", "y", "c"))

    gws = [None] * depth
    for l in reversed(range(depth)):
        dres, gws[l] = _layer_bwd(dres, layers[l], saved[l], pad_rows)
    grad_x = dres[row_start:].reshape(x.shape)

    def stack(fn):
        return jnp.stack([fn(g) for g in gws])

    part = dict(
        meta_tokens=dres[pad_rows:row_start],
        norm1_g=stack(lambda g: g["norm1_g"][0]), norm2_g=stack(lambda g: g["norm2_g"][0]),
        fox_f_bias=stack(lambda g: g["fb"][0, 0:8]),
        fox_q_norm_g=stack(lambda g: g["qg"].reshape(8, FOX_DH).sum(0)),
        fox_k_norm_g=stack(lambda g: g["kg"].reshape(8, FOX_DH).sum(0)),
        gdn_conv_w=stack(lambda g: g["gdn_conv_w"]), gdn_a_log=stack(lambda g: g["alog"][0, 16:24]),
        gdn_dt_bias=stack(lambda g: g["dtb"][0, 16:24]),
        gdn_norm_g=stack(lambda g: g["gn"].reshape(8, GDN_DH).sum(0)),
        ffn_conv_w=stack(lambda g: g["ffn_conv_w"]))

    def dest_blocks(name, d):
        if name == "w_in":
            s = w_in.shape[2]
            blks = [_unpermute_cols(g["w_main"], g["w_small"], s * d, s * (d + 1)) for g in gws]
        elif BIG_AXES[BIG.index(name)] == 2:
            s = w[name].shape[2]
            blks = [g[name][:, s * d:s * (d + 1)] for g in gws]
        else:
            s = w[name].shape[1]
            blks = [g[name][s * d:s * (d + 1), :] for g in gws]
        return jnp.stack([b.astype(BF16) for b in blks])

    res = {}
    for n in BIG:
        landed = _all_to_all(jnp.stack([dest_blocks(n, d) for d in range(N_DEV)]), "scatter_" + n)
        res[n] = _sum_adamw(landed, w[n], mom[n], var[n], "adamw_" + n)

    small_names = SHARDED_SMALL + REPL
    small_axes = SHARDED_SMALL_AXES + (None,) * len(REPL)
    landed_s = _all_to_all(_pack_dest([part[n] for n in small_names], small_axes, LANE, F32, 8), "scatter_small")
    shapes_s = [w[n].shape for n in small_names]
    outs = _sum_adamw(landed_s[:, None], *[_pack([d[n] for n in small_names], LANE, F32, 8)[None] for d in (w, mom, var)],
                      "adamw_small")
    for o_idx, packed in enumerate(outs):
        for n, a in zip(small_names, _unpack(packed[0], shapes_s)):
            res.setdefault(n, [None] * 4)[o_idx] = a

    return (loss, grad_x, *[res[n][0] for n in ORDER], *[res[n][1] for n in ORDER],
            *[res[n][2] for n in ORDER], *[res[n][3] for n in ORDER])
```

```python
import functools

import jax
import jax.numpy as jnp
from jax import lax
from jax.experimental import pallas as pl
from jax.experimental.pallas import tpu as pltpu

F32, BF16 = jnp.float32, jnp.bfloat16
MESH = pl.DeviceIdType.MESH

D = 1024
N_META = 16
DEPTH = 4
EPS = 1e-6
NEG = -1e30
FOX_W, FOX_DH = 512, 64
GDN_W, GDN_DH, GDN_H = 1024, 128, 8
CHUNK = 64
D_FF = 2816
N_DEV = 8
ADAM_LR, ADAM_B1, ADAM_B2, ADAM_EPS, ADAM_WD, ADAM_STEP = 0.001, 0.9, 0.999, 1e-08, 0.01, 10

VMEM_LIMIT_BYTES = 48 * 1024 * 1024
MATMUL_VMEM_BUDGET = 36 * 1024 * 1024
ROW_TILE = 128
LANE = 128

C_GQ, C_GK, C_GV, C_GZ, C_GATE, C_FQ, C_FK, C_FV = 0, 1024, 2048, 3072, 4096, 6144, 6656, 7168
W_MAIN = 7680
O_FQ, O_FK, O_FV, O_FL, O_GQ, O_GK, O_GV, O_BL, O_AL, O_GZ, O_GATE, O_END = (
    0, 512, 1024, 1536, 1544, 2568, 3592, 4616, 4624, 4632, 5656, 7704)


def _pick(n, cands):
    for c in cands:
        if n % c == 0:
            return c
    return n


def _call(body, *, name, out_shape, in_specs, out_specs, grid=(), scratch=(), sem=None):
    kw = dict(vmem_limit_bytes=VMEM_LIMIT_BYTES)
    if sem is not None:
        kw["dimension_semantics"] = sem
    return pl.pallas_call(body, name=name, out_shape=out_shape, grid=grid, in_specs=in_specs,
                          out_specs=out_specs, scratch_shapes=list(scratch),
                          compiler_params=pltpu.CompilerParams(**kw))


_DIMS = {"nn": (((1,), (0,)), ((), ())), "nt": (((1,), (1,)), ((), ())), "tn": (((0,), (0,)), ((), ()))}


_DIMS_BATCHED = {"nn": (((2,), (1,)), ((0,), (0,))), "nt": (((2,), (2,)), ((0,), (0,))),
                 "tn": (((1,), (1,)), ((0,), (0,)))}


def _dot(a, b, mode, prec=None):
    dims = _DIMS[mode] if a.ndim == 2 else _DIMS_BATCHED[mode]
    return lax.dot_general(a, b, dims, precision=prec, preferred_element_type=F32)


def _mm_grads(f, mode, a, b, g):
    if mode == "nn":
        return f(g, b, "nt"), f(a, g, "tn")
    if mode == "nt":
        return f(g, b, "nn"), f(g, a, "tn")
    return f(b, g, "nt"), f(a, g, "nn")


@functools.partial(jax.custom_vjp, nondiff_argnums=(2,))
def _mmb(a, b, mode):
    return _dot(a.astype(BF16), b.astype(BF16), mode)


def _mmb_fwd(a, b, mode):
    return _mmb(a, b, mode), (a, b)


def _mmb_bwd(mode, res, g):
    return _mm_grads(_mmb, mode, res[0], res[1], g)


_mmb.defvjp(_mmb_fwd, _mmb_bwd)


def _split(a):
    hi = a.astype(BF16)
    return hi, (a - hi.astype(F32)).astype(BF16)


@functools.partial(jax.custom_vjp, nondiff_argnums=(2,))
def _mmh(a, b, mode):
    ah, al = _split(a)
    bh, bl = _split(b)
    return _dot(ah, bh, mode) + (_dot(ah, bl, mode) + _dot(al, bh, mode))


def _mmh_fwd(a, b, mode):
    return _mmh(a, b, mode), (a, b)


def _mmh_bwd(mode, res, g):
    return _mm_grads(_mmh, mode, res[0], res[1], g)


_mmh.defvjp(_mmh_fwd, _mmh_bwd)


def _dot_sel(sel, x, mode):
    s = sel.astype(BF16)
    x1 = x.astype(BF16)
    x2, x3 = _split(x - x1.astype(F32))
    return _dot(s, x1, mode) + (_dot(s, x2, mode) + _dot(s, x3, mode))


@jax.custom_vjp
def _mms(sel, x):
    return _dot_sel(sel, x, "nn")


def _mms_fwd(sel, x):
    return _dot_sel(sel, x, "nn"), sel


def _mms_bwd(sel, g):
    return jnp.zeros_like(sel), _dot_sel(sel, g, "tn")


_mms.defvjp(_mms_fwd, _mms_bwd)


def _softplus(z):
    return jnp.maximum(z, 0.0) + jnp.log(1.0 + jnp.exp(-jnp.abs(z)))


def _log_sigmoid(z):
    return jnp.minimum(z, 0.0) - jnp.log(1.0 + jnp.exp(-jnp.abs(z)))


def _silu(z):
    return z * jax.nn.sigmoid(z)


def _iota(shape, dim):
    return lax.broadcasted_iota(jnp.int32, shape, dim)


def _inv_unit_lower_raw(n):
    c = n.shape[-1]
    ri, ci = _iota((c, c), 0), _iota((c, c), 1)
    eye = (ri == ci).astype(F32)
    dmask = (ri // 16) == (ci // 16)
    dpart = jnp.where(dmask, n, 0.0)
    lpart = n - dpart
    x = -dpart
    p = eye + x
    for _ in range(3):
        x = _mmh(x, x, "nn")
        p = p + _mmh(p, x, "nn")
    m = -_mmh(p, lpart, "nn")
    q = eye + m
    steps = 1
    while (1 << steps) < c // 16:
        steps += 1
    for _ in range(steps - 1):
        m = _mmh(m, m, "nn")
        q = q + _mmh(q, m, "nn")
    return _mmh(q, p, "nn")


@jax.custom_vjp
def _inv_unit_lower(n):
    return _inv_unit_lower_raw(n)


def _inv_fwd(n):
    t = _inv_unit_lower_raw(n)
    return t, t


def _inv_bwd(t, g):
    c = t.shape[-1]
    strict = _iota((c, c), 0) > _iota((c, c), 1)
    d = -_mmh(_mmh(t, g, "tn"), t, "nt")
    return (jnp.where(strict, d, 0.0),)


_inv_unit_lower.defvjp(_inv_fwd, _inv_bwd)


def _shift_down(x, halo, s):
    if s == 0:
        return x
    xs = pltpu.roll(x, s, 0)
    hs = pltpu.roll(halo, s, 0)
    top = jnp.where(_iota(hs.shape, 0) < s, hs, xs[0:8])
    return jnp.concatenate([top, xs[8:]], axis=0)


def _shift_up(x, halo, s):
    if s == 0:
        return x
    tm = x.shape[0]
    xs = pltpu.roll(x, tm - s, 0)
    hs = pltpu.roll(halo, 8 - s, 0)
    bot = jnp.where(_iota(hs.shape, 0) >= 8 - s, hs, xs[tm - 8:])
    return jnp.concatenate([xs[:tm - 8], bot], axis=0)


def _causal_conv(x, halo, w):
    kk = w.shape[0]
    y = x * w[kk - 1:kk, :]
    for k in range(kk - 1):
        y = y + _shift_down(x, halo, kk - 1 - k) * w[k:k + 1, :]
    return y


def _head_scale(x, width, fn):
    outs = []
    for h in range(x.shape[1] // width):
        seg = x[:, h * width:(h + 1) * width]
        outs.append(seg * fn(jnp.sum(seg * seg, axis=1, keepdims=True)))
    return jnp.concatenate(outs, axis=1)


def _matmul(a, b, mode, out_dtype, add=None, name="mm"):
    if mode == "nn":
        (m, k), n = a.shape, b.shape[1]
    elif mode == "nt":
        (m, k), n = a.shape, b.shape[0]
    else:
        (k, m), n = a.shape, b.shape[1]
    tm = _pick(m, (1408, 1024, 512, 256, 128) if mode == "tn" else (640, 512, 256, 128))
    tn = _pick(n, (1408, 1024, 768, 512, 256, 128))
    sa, sb = a.dtype.itemsize, b.dtype.itemsize
    fixed = tm * tn * 4 * (3 + (2 if add is not None else 0))
    tk = 128
    for cand in (k, 2816, 2560, 1664, 1536, 1280, 1024, 832, 768, 640, 512, 256, 128):
        if mode != "tn" and cand != k and cand % LANE:
            continue
        if k % cand == 0 and fixed + 2 * cand * (tm * sa + tn * sb) <= MATMUL_VMEM_BUDGET:
            tk = cand
            break
    nk = k // tk
    a_spec = {"nn": pl.BlockSpec((tm, tk), lambda i, j, kk: (i, kk)),
              "nt": pl.BlockSpec((tm, tk), lambda i, j, kk: (i, kk)),
              "tn": pl.BlockSpec((tk, tm), lambda i, j, kk: (kk, i))}[mode]
    b_spec = {"nn": pl.BlockSpec((tk, tn), lambda i, j, kk: (kk, j)),
              "nt": pl.BlockSpec((tn, tk), lambda i, j, kk: (j, kk)),
              "tn": pl.BlockSpec((tk, tn), lambda i, j, kk: (kk, j))}[mode]
    o_spec = pl.BlockSpec((tm, tn), lambda i, j, kk: (i, j))
    has_add = add is not None

    def body(*refs):
        a_ref, b_ref = refs[0], refs[1]
        add_ref = refs[2] if has_add else None
        o_ref = refs[3] if has_add else refs[2]
        part = _dot(a_ref[...].astype(BF16), b_ref[...].astype(BF16), mode)
        if nk == 1:
            if has_add:
                part = part + add_ref[...].astype(F32)
            o_ref[...] = part.astype(out_dtype)
        else:
            acc = refs[-1]
            kk = pl.program_id(2)

            @pl.when(kk == 0)
            def _():
                acc[...] = part

            @pl.when(kk > 0)
            def _():
                acc[...] += part

            @pl.when(kk == nk - 1)
            def _():
                r = acc[...]
                if has_add:
                    r = r + add_ref[...].astype(F32)
                o_ref[...] = r.astype(out_dtype)

    ins = [a, b] + ([add] if has_add else [])
    specs = [a_spec, b_spec] + ([o_spec] if has_add else [])
    return _call(body, name=name, out_shape=jax.ShapeDtypeStruct((m, n), out_dtype), grid=(m // tm, n // tn, nk),
                 in_specs=specs, out_specs=o_spec,
                 scratch=[pltpu.VMEM((tm, tn), F32)] if nk > 1 else [],
                 sem=("parallel", "parallel", "arbitrary"))(*ins)


def _row_spec(width, colblock, tm):
    return pl.BlockSpec((tm, width), lambda i, cb=colblock: (i, cb))


def _full_spec(arr):
    nd = arr.ndim
    return pl.BlockSpec(arr.shape, lambda i, nd=nd: (0,) * nd)


def _rowwise(fn, rows, params, outs, name, tm=ROW_TILE):
    lp = rows[0][0].shape[0]
    nr, npar = len(rows), len(params)

    def body(*refs):
        row0 = pl.program_id(0) * tm
        vals = [r[...].astype(F32) for r in refs[:nr + npar]]
        res = fn(*vals, row0)
        for o_ref, r in zip(refs[nr + npar:], res):
            o_ref[...] = r.astype(o_ref.dtype)

    out = _call(body, name=name, grid=(lp // tm,),
                out_shape=[jax.ShapeDtypeStruct((lp, w), dt) for w, dt in outs],
                in_specs=[_row_spec(w, cb, tm) for _, w, cb in rows] + [_full_spec(p) for p in params],
                out_specs=[_row_spec(w, 0, tm) for w, _ in outs], sem=("parallel",))(
                    *[r[0] for r in rows], *params)
    return out


def _rowwise_bwd(fn, rows, params, cts, name, pad_rows, grad_dtypes, adds=None, tm=ROW_TILE):
    lp = rows[0][0].shape[0]
    nr, npar, nct = len(rows), len(params), len(cts)
    adds = adds or [None] * nr
    add_list = [a for a in adds if a is not None]
    nadd = len(add_list)

    def body(*refs):
        i = pl.program_id(0)
        row0 = i * tm
        vals = [r[...].astype(F32) for r in refs[:nr + npar]]
        ct_vals = tuple(r[...].astype(F32) for r in refs[nr + npar:nr + npar + nct])
        add_refs = list(refs[nr + npar + nct:nr + npar + nct + nadd])
        outs = refs[nr + npar + nct + nadd:]
        _, vjp = jax.vjp(lambda *args: tuple(fn(*args, row0)), *vals)
        grads = vjp(ct_vals)
        valid = (row0 + _iota((tm, 1), 0)) >= pad_rows
        for idx in range(nr):
            g = jnp.where(valid, grads[idx], 0.0)
            if adds[idx] is not None:
                g = g + add_refs.pop(0)[...].astype(F32)
            outs[idx][...] = g.astype(outs[idx].dtype)
        for idx in range(npar):
            o_ref = outs[nr + idx]

            @pl.when(i == 0)
            def _(o_ref=o_ref):
                o_ref[...] = jnp.zeros_like(o_ref)

            o_ref[...] += grads[nr + idx]

    out = _call(body, name=name, grid=(lp // tm,),
                out_shape=[jax.ShapeDtypeStruct((lp, w), dt) for (_, w, _), dt in zip(rows, grad_dtypes)]
                + [jax.ShapeDtypeStruct(p.shape, F32) for p in params],
                in_specs=[_row_spec(w, cb, tm) for _, w, cb in rows] + [_full_spec(p) for p in params]
                + [_row_spec(w, cb, tm) for _, w, cb in cts] + [_row_spec(w, cb, tm) for _, w, cb in add_list],
                out_specs=[_row_spec(w, 0, tm) for _, w, _ in rows] + [_full_spec(p) for p in params],
                sem=("arbitrary",))(*[r[0] for r in rows], *params, *[c[0] for c in cts], *[a[0] for a in add_list])
    return out[:nr], out[nr:]


def _rmsnorm_fn(x, g, row0):
    return (x * lax.rsqrt(jnp.mean(x * x, axis=1, keepdims=True) + EPS) * g,)


def _fox_prep_fn(pad_rows, fq, fk, small, qg, kg, fb, row0):
    ri, ci = _iota((FOX_W, FOX_W), 0), _iota((FOX_W, FOX_W), 1)
    bd = jnp.where((ri // FOX_DH) == (ci // FOX_DH), 1.0 / FOX_DH, 0.0)

    def hn(x, g):
        return x * lax.rsqrt(_mmh(x * x, bd, "nn") + EPS) * g

    tm = small.shape[0]
    keep = (_iota((tm, LANE), 1) < 8) & ((row0 + _iota((tm, LANE), 0)) >= pad_rows)
    logf = jnp.where(keep, _log_sigmoid(small + fb), 0.0)
    return hn(fq, qg) * (FOX_DH ** -0.5), hn(fk, kg), logf


def _gdn_act_fn(cq, ck, cv, small, alog, dtb):
    tm = small.shape[0]
    q = _head_scale(_silu(cq), GDN_DH, lambda s: lax.rsqrt(s + EPS) * (GDN_DH ** -0.5))
    k = _head_scale(_silu(ck), GDN_DH, lambda s: lax.rsqrt(s + EPS))
    v = _silu(cv)
    lane = _iota((tm, LANE), 1)
    beta = jnp.where((lane >= 8) & (lane < 16), jax.nn.sigmoid(small), 0.0)
    g = jnp.where((lane >= 16) & (lane < 24), -jnp.exp(alog) * _softplus(small + dtb), 0.0)
    ri, ci = _iota((tm, tm), 0), _iota((tm, tm), 1)
    tri = jnp.where(((ri // CHUNK) == (ci // CHUNK)) & (ci <= ri), 1.0, 0.0)
    return q, k, v, beta + _mms(tri, g)


def _gdn_post_fn(o, gz, gn, row0):
    return (_head_scale(o, GDN_DH, lambda s: lax.rsqrt(s * (1.0 / GDN_DH) + EPS)) * gn * _silu(gz),)


def _merge_fn(g0, g1, ya, yb, row0):
    return (jax.nn.sigmoid(g0) * ya + jax.nn.sigmoid(g1) * yb,)


def _cumsum_rows(x, reverse, name):
    lp, w = x.shape
    tm = _pick(lp, (640, 512, 256, 128))
    nt = lp // tm

    def body(x_ref, o_ref, carry):
        i = pl.program_id(0)

        @pl.when(i == 0)
        def _():
            carry[...] = jnp.zeros_like(carry)

        ri, ci = _iota((tm, tm), 0), _iota((tm, tm), 1)
        tri = jnp.where((ci >= ri) if reverse else (ci <= ri), 1.0, 0.0)
        blk = x_ref[...]
        o_ref[...] = _dot_sel(tri, blk, "nn") + carry[0:1, :]
        carry[...] = carry[...] + jnp.sum(blk, axis=0, keepdims=True)

    idx = (lambda i: (nt - 1 - i, 0)) if reverse else (lambda i: (i, 0))
    return _call(body, name=name, grid=(nt,), out_shape=jax.ShapeDtypeStruct((lp, w), F32),
                 in_specs=[pl.BlockSpec((tm, w), idx)], out_specs=pl.BlockSpec((tm, w), idx),
                 scratch=[pltpu.VMEM((8, w), F32)], sem=("arbitrary",))(x)


def _fox_scores(q, k, fq, fk, hh, qpos0, kpos0, pad_rows, masked):
    tq, tk = q.shape[0], k.shape[0]
    lane = _iota(q.shape, 1)
    sel = (lane < FOX_DH) if hh == 0 else (lane >= FOX_DH)
    s = _dot(jnp.where(sel, q, jnp.zeros_like(q)), k, "nt") + fq - fk
    if not masked:
        return s, None, sel
    qpos = qpos0 + _iota((tq, tk), 0)
    kpos = kpos0 + _iota((tq, tk), 1)
    mask = (kpos <= qpos) & (kpos >= pad_rows)
    return jnp.where(mask, s, NEG), mask, sel


def _probs(s, mask, shift):
    p = jnp.exp(s - shift)
    return p if mask is None else jnp.where(mask, p, 0.0)


def _both_variants(needs_mask, fn):
    @pl.when(needs_mask)
    def _():
        fn(True)

    @pl.when(jnp.logical_not(needs_mask))
    def _():
        fn(False)


def _fox_fwd(q, k, v, fcol, frow, pad_rows, v_col):
    lp = q.shape[0]
    t = _pick(lp, (640, 512, 256, 128))
    n = lp // t

    def body(q_ref, k_ref, v_ref, fq_ref, fk_ref, o_ref, lse_ref, acc, m_s, l_s):
        i, j = pl.program_id(1), pl.program_id(2)

        @pl.when(j == 0)
        def _():
            acc[...] = jnp.zeros_like(acc)
            m_s[...] = jnp.full_like(m_s, NEG)
            l_s[...] = jnp.zeros_like(l_s)

        def step(masked):
            for hh in range(2):
                s, mask, _ = _fox_scores(q_ref[...], k_ref[...], fq_ref[hh], fk_ref[hh], hh, i * t, j * t, pad_rows,
                                         masked)
                m_prev = m_s[hh]
                m_new = jnp.maximum(m_prev, jnp.max(s, axis=1, keepdims=True))
                p = _probs(s, mask, m_new)
                alpha = jnp.exp(m_prev - m_new)
                l_s[hh] = alpha * l_s[hh] + jnp.sum(p, axis=1, keepdims=True)
                acc[hh] = alpha * acc[hh] + _dot(p.astype(BF16), v_ref[...].astype(BF16), "nn")
                m_s[hh] = m_new

        @pl.when(j <= i)
        def _():
            _both_variants((j == i) | (j == 0), step)

        @pl.when(j == i)
        def _():
            outs = []
            for hh in range(2):
                l = l_s[hh]
                ok = l > 0.0
                outs.append(acc[hh] * jnp.where(ok, 1.0 / jnp.where(ok, l, 1.0), 0.0))
                lse_ref[hh] = jnp.where(ok, m_s[hh] + jnp.log(jnp.where(ok, l, 1.0)), 0.0)
            lane = _iota((t, LANE), 1)
            o_ref[...] = jnp.where(lane < FOX_DH, outs[0], outs[1]).astype(o_ref.dtype)

    qspec = pl.BlockSpec((t, LANE), lambda p, i, j: (i, p))
    kspec = pl.BlockSpec((t, LANE), lambda p, i, j: (jnp.minimum(j, i), p))
    vspec = pl.BlockSpec((t, LANE), lambda p, i, j: (jnp.minimum(j, i), v_col + p))
    cspec = pl.BlockSpec((2, t, 1), lambda p, i, j: (p, i, 0))
    rspec = pl.BlockSpec((2, 1, t), lambda p, i, j: (p, 0, jnp.minimum(j, i)))
    return _call(body, name="fox_fwd", grid=(FOX_W // LANE, n, n),
                 out_shape=[jax.ShapeDtypeStruct((lp, FOX_W), BF16), jax.ShapeDtypeStruct((8, lp, 1), F32)],
                 in_specs=[qspec, kspec, vspec, cspec, rspec], out_specs=[qspec, cspec],
                 scratch=[pltpu.VMEM((2, t, LANE), F32), pltpu.VMEM((2, t, 1), F32), pltpu.VMEM((2, t, 1), F32)],
                 sem=("parallel", "parallel", "arbitrary"))(q, k, v, fcol, frow)


def _fox_bwd_q(q, k, v, fcol, frow, o, do, lse, pad_rows, v_col):
    lp = q.shape[0]
    t = _pick(lp, (640, 512, 256, 128))
    n = lp // t

    def body(q_ref, k_ref, v_ref, fq_ref, fk_ref, o_ref, do_ref, lse_ref, dq_ref, dfq_ref, delta_ref, acc, dfa):
        i, j = pl.program_id(1), pl.program_id(2)
        lane = _iota((t, LANE), 1)

        @pl.when(j == 0)
        def _():
            acc[...] = jnp.zeros_like(acc)
            dfa[...] = jnp.zeros_like(dfa)
            prod = do_ref[...].astype(F32) * o_ref[...].astype(F32)
            delta_ref[0] = jnp.sum(jnp.where(lane < FOX_DH, prod, 0.0), axis=1, keepdims=True)
            delta_ref[1] = jnp.sum(jnp.where(lane >= FOX_DH, prod, 0.0), axis=1, keepdims=True)

        def step(masked):
            for hh in range(2):
                s, mask, sel = _fox_scores(q_ref[...], k_ref[...], fq_ref[hh], fk_ref[hh], hh, i * t, j * t, pad_rows,
                                           masked)
                p = _probs(s, mask, lse_ref[hh])
                dop = jnp.where(sel, do_ref[...], jnp.zeros_like(do_ref[...]))
                ds = p * (_dot(dop, v_ref[...].astype(BF16), "nt") - delta_ref[hh])
                acc[hh] += _dot(ds.astype(BF16), k_ref[...], "nn")
                dfa[hh] += jnp.sum(ds, axis=1, keepdims=True)

        @pl.when(j <= i)
        def _():
            _both_variants((j == i) | (j == 0), step)

        @pl.when(j == i)
        def _():
            dq_ref[...] = jnp.where(lane < FOX_DH, acc[0], acc[1])
            dfq_ref[...] = dfa[...]

    qspec = pl.BlockSpec((t, LANE), lambda p, i, j: (i, p))
    kspec = pl.BlockSpec((t, LANE), lambda p, i, j: (jnp.minimum(j, i), p))
    vspec = pl.BlockSpec((t, LANE), lambda p, i, j: (jnp.minimum(j, i), v_col + p))
    cspec = pl.BlockSpec((2, t, 1), lambda p, i, j: (p, i, 0))
    rspec = pl.BlockSpec((2, 1, t), lambda p, i, j: (p, 0, jnp.minimum(j, i)))
    col = jax.ShapeDtypeStruct((8, lp, 1), F32)
    return _call(body, name="fox_bwd_q", grid=(FOX_W // LANE, n, n),
                 out_shape=[jax.ShapeDtypeStruct((lp, FOX_W), F32), col, col],
                 in_specs=[qspec, kspec, vspec, cspec, rspec, qspec, qspec, cspec], out_specs=[qspec, cspec, cspec],
                 scratch=[pltpu.VMEM((2, t, LANE), F32), pltpu.VMEM((2, t, 1), F32)],
                 sem=("parallel", "parallel", "arbitrary"))(q, k, v, fcol, frow, o, do, lse)


def _fox_bwd_kv(q, k, v, fcol, frow, do, lse, delta, pad_rows, v_col):
    lp = q.shape[0]
    t = _pick(lp, (640, 512, 256, 128))
    n = lp // t

    def body(q_ref, k_ref, v_ref, fq_ref, fk_ref, do_ref, lse_ref, delta_ref, dk_ref, dv_ref, dfk_ref, dka, dva, dfa):
        j, i = pl.program_id(1), pl.program_id(2)
        lane = _iota((t, LANE), 1)

        @pl.when(i == 0)
        def _():
            dka[...] = jnp.zeros_like(dka)
            dva[...] = jnp.zeros_like(dva)
            dfa[...] = jnp.zeros_like(dfa)

        def step(masked):
            for hh in range(2):
                s, mask, sel = _fox_scores(q_ref[...], k_ref[...], fq_ref[hh], fk_ref[hh], hh, i * t, j * t, pad_rows,
                                           masked)
                p = _probs(s, mask, lse_ref[hh])
                dop = jnp.where(sel, do_ref[...], jnp.zeros_like(do_ref[...]))
                ds = p * (_dot(dop, v_ref[...].astype(BF16), "nt") - delta_ref[hh])
                dva[hh] += _dot(p.astype(BF16), do_ref[...], "tn")
                dka[hh] += _dot(ds.astype(BF16), q_ref[...], "tn")
                dfa[hh] -= jnp.sum(ds, axis=0, keepdims=True)

        @pl.when(i >= j)
        def _():
            _both_variants((j == i) | (j == 0), step)

        @pl.when(i == n - 1)
        def _():
            dk_ref[...] = jnp.where(lane < FOX_DH, dka[0], dka[1])
            dv_ref[...] = jnp.where(lane < FOX_DH, dva[0], dva[1]).astype(dv_ref.dtype)
            dfk_ref[...] = dfa[...]

    qspec = pl.BlockSpec((t, LANE), lambda p, j, i: (jnp.maximum(i, j), p))
    kspec = pl.BlockSpec((t, LANE), lambda p, j, i: (j, p))
    vspec = pl.BlockSpec((t, LANE), lambda p, j, i: (j, v_col + p))
    cspec = pl.BlockSpec((2, t, 1), lambda p, j, i: (p, jnp.maximum(i, j), 0))
    rspec = pl.BlockSpec((2, 1, t), lambda p, j, i: (p, 0, j))
    return _call(body, name="fox_bwd_kv", grid=(FOX_W // LANE, n, n),
                 out_shape=[jax.ShapeDtypeStruct((lp, FOX_W), F32), jax.ShapeDtypeStruct((lp, FOX_W), BF16),
                            jax.ShapeDtypeStruct((8, 1, lp), F32)],
                 in_specs=[qspec, kspec, vspec, cspec, rspec, qspec, cspec, cspec], out_specs=[kspec, kspec, rspec],
                 scratch=[pltpu.VMEM((2, t, LANE), F32), pltpu.VMEM((2, t, LANE), F32), pltpu.VMEM((2, 1, t), F32)],
                 sem=("parallel", "parallel", "arbitrary"))(q, k, v, fcol, frow, do, lse, delta)


def _gdn_chunk(q, k, v, beta, gcol, grow, s, inv):
    c = q.shape[-2]
    ri, ci = _iota((c, c), 0), _iota((c, c), 1)
    dec = jnp.exp(jnp.where(ri >= ci, gcol - grow, NEG))
    dec_strict = jnp.where(ri > ci, dec, 0.0)
    eg = jnp.exp(gcol)
    kb = k * beta
    t = inv(_mmb(kb, k, "nt") * dec_strict)
    u_hat = _mmh(t, v * beta, "nn")
    w = _mmh(t, kb * eg, "nn")
    u = u_hat - _mmb(w, s, "nn")
    o = _mmb(q * eg, s, "nn") + _mmb(_mmb(q, k, "nt") * dec, u, "nn")
    glast = jnp.sum(jnp.where(_iota((1, c), 1) == c - 1, grow, 0.0), axis=-1, keepdims=True)
    s_new = s * jnp.exp(glast) + _mmb(k * jnp.exp(glast - gcol), u, "tn")
    return o, s_new


def _gdn_specs(lp, reverse):
    n = lp // CHUNK
    pos = (lambda c: n - 1 - c) if reverse else (lambda c: c)
    wide = pl.BlockSpec((CHUNK, GDN_W), lambda c: (pos(c), 0))
    col = pl.BlockSpec((GDN_H, CHUNK, 1), lambda c: (0, pos(c), 0))
    row = pl.BlockSpec((GDN_H, 1, 1, CHUNK), lambda c: (0, pos(c), 0, 0))
    st = pl.BlockSpec((GDN_H, 1, GDN_DH, GDN_DH), lambda c: (0, pos(c), 0, 0))
    return n, wide, col, row, st


def _heads(ref):
    return jnp.stack([ref[:, h * GDN_DH:(h + 1) * GDN_DH] for h in range(GDN_H)])


def _put_heads(ref, val):
    for h in range(GDN_H):
        ref[:, h * GDN_DH:(h + 1) * GDN_DH] = val[h]


def _gdn_fwd(q, k, v, bcol, gcol, grow):
    lp = q.shape[0]
    n, wide, col, row, st = _gdn_specs(lp, False)

    def body(q_ref, k_ref, v_ref, b_ref, gc_ref, gr_ref, o_ref, sp_ref, s_scr):
        @pl.when(pl.program_id(0) == 0)
        def _():
            s_scr[...] = jnp.zeros_like(s_scr)

        s = s_scr[...]
        sp_ref[:, 0] = s
        o, s_new = _gdn_chunk(_heads(q_ref), _heads(k_ref), _heads(v_ref), b_ref[...], gc_ref[...], gr_ref[:, 0], s,
                              _inv_unit_lower_raw)
        _put_heads(o_ref, o)
        s_scr[...] = s_new

    return _call(body, name="gdn_fwd", grid=(n,),
                 out_shape=[jax.ShapeDtypeStruct((lp, GDN_W), F32),
                            jax.ShapeDtypeStruct((GDN_H, n, GDN_DH, GDN_DH), F32)],
                 in_specs=[wide, wide, wide, col, col, row], out_specs=[wide, st],
                 scratch=[pltpu.VMEM((GDN_H, GDN_DH, GDN_DH), F32)], sem=("arbitrary",))(q, k, v, bcol, gcol, grow)


def _gdn_bwd(q, k, v, bcol, gcol, grow, sprev, do):
    lp = q.shape[0]
    n, wide, col, row, st = _gdn_specs(lp, True)

    def body(q_ref, k_ref, v_ref, b_ref, gc_ref, gr_ref, sp_ref, do_ref,
             dq_ref, dk_ref, dv_ref, db_ref, dgc_ref, dgr_ref, ds_scr):
        @pl.when(pl.program_id(0) == 0)
        def _():
            ds_scr[...] = jnp.zeros_like(ds_scr)

        fn = functools.partial(_gdn_chunk, inv=_inv_unit_lower)
        _, vjp = jax.vjp(fn, _heads(q_ref), _heads(k_ref), _heads(v_ref), b_ref[...], gc_ref[...], gr_ref[:, 0],
                         sp_ref[:, 0])
        dq, dk, dv, db, dgc, dgr, ds = vjp((_heads(do_ref), ds_scr[...]))
        _put_heads(dq_ref, dq)
        _put_heads(dk_ref, dk)
        _put_heads(dv_ref, dv)
        db_ref[...] = db
        dgc_ref[...] = dgc
        dgr_ref[:, 0] = dgr
        ds_scr[...] = ds

    wshape = jax.ShapeDtypeStruct((lp, GDN_W), F32)
    cshape = jax.ShapeDtypeStruct((GDN_H, lp, 1), F32)
    return _call(body, name="gdn_bwd", grid=(n,),
                 out_shape=[wshape, wshape, wshape, cshape, cshape, jax.ShapeDtypeStruct((GDN_H, n, 1, CHUNK), F32)],
                 in_specs=[wide, wide, wide, col, col, row, st, wide], out_specs=[wide, wide, wide, col, col, row],
                 scratch=[pltpu.VMEM((GDN_H, GDN_DH, GDN_DH), F32)], sem=("arbitrary",))(
                     q, k, v, bcol, gcol, grow, sprev, do)


def _halo_prev(width, colblock, tm):
    return pl.BlockSpec((8, width), lambda i, cb=colblock: (jnp.maximum(i * (tm // 8) - 1, 0), cb))


def _gdn_act(proj, small, conv_w, alog_row, dtb_row, tm=ROW_TILE):
    lp = proj.shape[0]

    def body(xq, xk, xv, hq, hk, hv, wq, wk, wv, sm, al, dt, q_ref, k_ref, v_ref, bg_ref):
        first = (pl.program_id(0) > 0).astype(F32)
        cs = [_causal_conv(x[...], h[...] * first, w[...]) for x, h, w in ((xq, hq, wq), (xk, hk, wk), (xv, hv, wv))]
        q, k, v, bg = _gdn_act_fn(cs[0], cs[1], cs[2], sm[...], al[...], dt[...])
        q_ref[...], k_ref[...], v_ref[...], bg_ref[...] = q, k, v, bg

    wide = jax.ShapeDtypeStruct((lp, GDN_W), F32)
    wspec = [pl.BlockSpec((4, GDN_W), lambda i, c=c: (0, c)) for c in range(3)]
    return _call(body, name="gdn_act", grid=(lp // tm,),
                 out_shape=[wide, wide, wide, jax.ShapeDtypeStruct((lp, LANE), F32)],
                 in_specs=[_row_spec(GDN_W, c, tm) for c in range(3)] + [_halo_prev(GDN_W, c, tm) for c in range(3)]
                 + wspec + [_row_spec(LANE, 0, tm), _full_spec(alog_row), _full_spec(dtb_row)],
                 out_specs=[_row_spec(GDN_W, 0, tm)] * 3 + [_row_spec(LANE, 0, tm)], sem=("parallel",))(
                     proj, proj, proj, proj, proj, proj, conv_w, conv_w, conv_w, small, alog_row, dtb_row)


def _gdn_act_bwd(proj, small, conv_w, alog_row, dtb_row, dq, dk, dv, dbg, tm=ROW_TILE):
    lp = proj.shape[0]

    def body(xq, xk, xv, hq, hk, hv, wq, wk, wv, sm, al, dt, dq_r, dk_r, dv_r, dbg_r,
             dc_ref, dsm_ref, dal_ref, ddt_ref, dw_ref):
        i = pl.program_id(0)
        first = (i > 0).astype(F32)
        xs = [(x[...], h[...] * first, w[...]) for x, h, w in ((xq, hq, wq), (xk, hk, wk), (xv, hv, wv))]
        cs = [_causal_conv(*t) for t in xs]
        _, vjp = jax.vjp(_gdn_act_fn, cs[0], cs[1], cs[2], sm[...], al[...], dt[...])
        dcq, dck, dcv, dsm, dal, ddt = vjp((dq_r[...], dk_r[...], dv_r[...], dbg_r[...]))
        dsm_ref[...] = dsm

        @pl.when(i == 0)
        def _():
            dal_ref[...] = jnp.zeros_like(dal_ref)
            ddt_ref[...] = jnp.zeros_like(ddt_ref)
            dw_ref[...] = jnp.zeros_like(dw_ref)

        dal_ref[...] += dal
        ddt_ref[...] += ddt
        for c, (dc, (x, h, w)) in enumerate(zip((dcq, dck, dcv), xs)):
            dc_ref[:, c * GDN_W:(c + 1) * GDN_W] = dc
            rows = [jnp.sum(_shift_down(x, h, 3 - kk) * dc, axis=0, keepdims=True) for kk in range(4)]
            dw_ref[:, c * GDN_W:(c + 1) * GDN_W] += jnp.concatenate(rows, axis=0)

    wspec = [pl.BlockSpec((4, GDN_W), lambda i, c=c: (0, c)) for c in range(3)]
    row128 = jax.ShapeDtypeStruct((1, LANE), F32)
    return _call(body, name="gdn_act_bwd", grid=(lp // tm,),
                 out_shape=[jax.ShapeDtypeStruct((lp, 3 * GDN_W), F32), jax.ShapeDtypeStruct((lp, LANE), F32),
                            row128, row128, jax.ShapeDtypeStruct((4, 3 * GDN_W), F32)],
                 in_specs=[_row_spec(GDN_W, c, tm) for c in range(3)] + [_halo_prev(GDN_W, c, tm) for c in range(3)]
                 + wspec + [_row_spec(LANE, 0, tm), _full_spec(alog_row), _full_spec(dtb_row)]
                 + [_row_spec(GDN_W, 0, tm)] * 3 + [_row_spec(LANE, 0, tm)],
                 out_specs=[_row_spec(3 * GDN_W, 0, tm), _row_spec(LANE, 0, tm),
                            _full_spec(alog_row), _full_spec(dtb_row), pl.BlockSpec((4, 3 * GDN_W), lambda i: (0, 0))],
                 sem=("arbitrary",))(proj, proj, proj, proj, proj, proj, conv_w, conv_w, conv_w, small,
                                     alog_row, dtb_row, dq, dk, dv, dbg)


def _ffn_act(up_pre, conv_w, tm=ROW_TILE):
    lp = up_pre.shape[0]

    def body(xg, xv, hg, hv, wg, wv, a_ref):
        first = (pl.program_id(0) > 0).astype(F32)
        ug = _causal_conv(xg[...], hg[...] * first, wg[...])
        uv = _causal_conv(xv[...], hv[...] * first, wv[...])
        a_ref[...] = (_silu(ug) * uv).astype(a_ref.dtype)

    wspec = [pl.BlockSpec((3, D_FF), lambda i, c=c: (0, c)) for c in range(2)]
    return _call(body, name="ffn_act", grid=(lp // tm,), out_shape=jax.ShapeDtypeStruct((lp, D_FF), BF16),
                 in_specs=[_row_spec(D_FF, c, tm) for c in range(2)] + [_halo_prev(D_FF, c, tm) for c in range(2)] + wspec,
                 out_specs=_row_spec(D_FF, 0, tm), sem=("parallel",))(up_pre, up_pre, up_pre, up_pre, conv_w, conv_w)


def _ffn_act_bwd(up_pre, conv_w, dact, tm=ROW_TILE):
    lp = up_pre.shape[0]

    def body(xg, xv, hg, hv, wg, wv, da, du_ref, dw_ref):
        i = pl.program_id(0)
        first = (i > 0).astype(F32)
        xs = [(x[...], h[...] * first, w[...]) for x, h, w in ((xg, hg, wg), (xv, hv, wv))]
        ug, uv = [_causal_conv(*t) for t in xs]
        _, vjp = jax.vjp(lambda a, b: _silu(a) * b, ug, uv)
        dus = vjp(da[...].astype(F32))

        @pl.when(i == 0)
        def _():
            dw_ref[...] = jnp.zeros_like(dw_ref)

        for c, (du, (x, h, w)) in enumerate(zip(dus, xs)):
            du_ref[:, c * D_FF:(c + 1) * D_FF] = du
            rows = [jnp.sum(_shift_down(x, h, 2 - kk) * du, axis=0, keepdims=True) for kk in range(3)]
            dw_ref[:, c * D_FF:(c + 1) * D_FF] += jnp.concatenate(rows, axis=0)

    wspec = [pl.BlockSpec((3, D_FF), lambda i, c=c: (0, c)) for c in range(2)]
    return _call(body, name="ffn_act_bwd", grid=(lp // tm,),
                 out_shape=[jax.ShapeDtypeStruct((lp, 2 * D_FF), F32), jax.ShapeDtypeStruct((3, 2 * D_FF), F32)],
                 in_specs=[_row_spec(D_FF, c, tm) for c in range(2)] + [_halo_prev(D_FF, c, tm) for c in range(2)]
                 + wspec + [_row_spec(D_FF, 0, tm)],
                 out_specs=[_row_spec(2 * D_FF, 0, tm), pl.BlockSpec((3, 2 * D_FF), lambda i: (0, 0))],
                 sem=("arbitrary",))(up_pre, up_pre, up_pre, up_pre, conv_w, conv_w, dact)


def _conv_bwd_x(dy, w, pad_rows, width, name, tm=ROW_TILE):
    lp, ctot = dy.shape
    nt = lp // tm
    kk = w.shape[0]

    def body(d_ref, h_ref, w_ref, o_ref):
        i = pl.program_id(0)
        last = (i < nt - 1).astype(F32)
        d, h, wv = d_ref[...], h_ref[...] * last, w_ref[...]
        y = d * wv[kk - 1:kk, :]
        for k in range(kk - 1):
            y = y + _shift_up(d, h, kk - 1 - k) * wv[k:k + 1, :]
        valid = (i * tm + _iota((tm, 1), 0)) >= pad_rows
        o_ref[...] = jnp.where(valid, y, 0.0).astype(o_ref.dtype)

    return _call(body, name=name, grid=(nt, ctot // width), out_shape=jax.ShapeDtypeStruct((lp, ctot), BF16),
                 in_specs=[pl.BlockSpec((tm, width), lambda i, c: (i, c)),
                           pl.BlockSpec((8, width), lambda i, c: (jnp.minimum((i + 1) * (tm // 8), lp // 8 - 1), c)),
                           pl.BlockSpec((kk, width), lambda i, c: (0, c))],
                 out_specs=pl.BlockSpec((tm, width), lambda i, c: (i, c)), sem=("parallel", "parallel"))(dy, dy, w)


def _loss_head(h_res, target, row_start, tm=ROW_TILE):
    lp, d = h_res.shape
    t0 = row_start // tm

    def body(h_ref, t_ref, dy_ref, loss_ref):
        i = pl.program_id(0)

        @pl.when(i == 0)
        def _():
            loss_ref[...] = jnp.zeros_like(loss_ref)

        live = (i >= t0).astype(F32)
        err = (h_ref[...] - t_ref[...]) * live
        dy_ref[...] = err * (1.0 / d)
        loss_ref[...] += 0.5 / d * jnp.sum(err * err)

    return _call(body, name="loss_head", grid=(lp // tm,),
                 out_shape=[jax.ShapeDtypeStruct((lp, d), F32), jax.ShapeDtypeStruct((8, LANE), F32)],
                 in_specs=[pl.BlockSpec((tm, d), lambda i: (i, 0)),
                           pl.BlockSpec((tm, d), lambda i: (jnp.maximum(i - t0, 0), 0))],
                 out_specs=[pl.BlockSpec((tm, d), lambda i: (i, 0)), pl.BlockSpec((8, LANE), lambda i: (0, 0))],
                 sem=("arbitrary",))(h_res, target)


def _sum_adamw(parts, w, m, v, name):
    a, r, c = w.shape
    tm = _pick(r, (256, 128, 64, 32, 16))
    bc1 = 1.0 - ADAM_B1 ** ADAM_STEP
    bc2 = 1.0 - ADAM_B2 ** ADAM_STEP

    def body(p_ref, w_ref, m_ref, v_ref, g_ref, d_ref, nm_ref, nv_ref):
        g = p_ref[0, 0].astype(F32)
        for s in range(1, N_DEV):
            g = g + p_ref[s, 0].astype(F32)
        nm = ADAM_B1 * m_ref[0] + (1.0 - ADAM_B1) * g
        nv = ADAM_B2 * v_ref[0] + (1.0 - ADAM_B2) * (g * g)
        g_ref[0] = g
        nm_ref[0] = nm
        nv_ref[0] = nv
        d_ref[0] = -ADAM_LR * ((nm / bc1) / (jnp.sqrt(nv / bc2) + ADAM_EPS) + ADAM_WD * w_ref[0])

    spec = pl.BlockSpec((1, tm, c), lambda l, i: (l, i, 0))
    shp = jax.ShapeDtypeStruct((a, r, c), F32)
    return _call(body, name=name, grid=(a, r // tm), out_shape=[shp] * 4,
                 in_specs=[pl.BlockSpec((N_DEV, 1, tm, c), lambda l, i: (0, l, i, 0)), spec, spec, spec],
                 out_specs=[spec] * 4, sem=("parallel", "parallel"))(parts, w, m, v)


_ANY = pl.BlockSpec(memory_space=pl.ANY)


def _all_gather(block, name):
    def body(x_ref, out_ref, send_sems, recv_sems, local_sem):
        x, y, c = lax.axis_index("x"), lax.axis_index("y"), lax.axis_index("c")
        me, sibling = (x, y, c), (x, y, 1 - c)
        chips = [(1 - x, y), (x, 1 - y), (1 - x, 1 - y)]

        def slot(px, py, pc):
            return out_ref.at[4 * px + 2 * py + pc]

        def copy(k, blk, to, src=None):
            return pltpu.make_async_remote_copy(
                src_ref=slot(*blk) if src is None else src, dst_ref=slot(*blk),
                send_sem=send_sems.at[k], recv_sem=recv_sems.at[k], device_id=to, device_id_type=MESH)

        mine = pltpu.make_async_copy(x_ref, slot(*me), local_sem)
        mine.start()
        first = [copy(0, me, sibling, src=x_ref)]
        first += [copy(1 + j, me, (*chip, c), src=x_ref) for j, chip in enumerate(chips)]
        for cp in first:
            cp.start()
        passed = [copy(4 + j, (*chip, c), sibling) for j, chip in enumerate(chips)]
        for j, chip in enumerate(chips):
            copy(1 + j, (*chip, c), me).wait_recv()
            passed[j].start()
        copy(0, sibling, me).wait_recv()
        for j, chip in enumerate(chips):
            copy(4 + j, (*chip, 1 - c), me).wait_recv()
        for cp in first + passed:
            cp.wait_send()
        mine.wait()

    return pl.pallas_call(
        body, name=name, out_shape=jax.ShapeDtypeStruct((N_DEV,) + block.shape, block.dtype),
        in_specs=[_ANY], out_specs=_ANY,
        scratch_shapes=[pltpu.SemaphoreType.DMA((7,)), pltpu.SemaphoreType.DMA((7,)), pltpu.SemaphoreType.DMA],
    )(block)


def _all_to_all(src, name):
    def body(s_ref, o_ref, send_sems, recv_sems, local_sem):
        x, y, c = lax.axis_index("x"), lax.axis_index("y"), lax.axis_index("c")
        me = 4 * x + 2 * y + c
        mine = pltpu.make_async_copy(s_ref.at[me], o_ref.at[me], local_sem)
        mine.start()
        copies = []
        for k in range(1, N_DEV):
            px = 1 - x if k & 4 else x
            py = 1 - y if k & 2 else y
            pc = 1 - c if k & 1 else c
            peer = 4 * px + 2 * py + pc
            copies.append((pltpu.make_async_remote_copy(
                src_ref=s_ref.at[peer], dst_ref=o_ref.at[me], send_sem=send_sems.at[k - 1],
                recv_sem=recv_sems.at[k - 1], device_id=(px, py, pc), device_id_type=MESH), peer, k))
        for cp, _, _ in copies:
            cp.start()
        for cp, peer, k in copies:
            cp.wait_send()
            pltpu.make_async_remote_copy(
                src_ref=s_ref.at[peer], dst_ref=o_ref.at[peer], send_sem=send_sems.at[k - 1],
                recv_sem=recv_sems.at[k - 1], device_id=(x, y, c), device_id_type=MESH).wait_recv()
        mine.wait()

    return pl.pallas_call(
        body, name=name, out_shape=jax.ShapeDtypeStruct(src.shape, src.dtype), in_specs=[_ANY], out_specs=_ANY,
        scratch_shapes=[pltpu.SemaphoreType.DMA((7,)), pltpu.SemaphoreType.DMA((7,)), pltpu.SemaphoreType.DMA],
    )(src)


def _pack(blocks, width, dtype, row_mult):
    flat = jnp.concatenate([b.astype(dtype).reshape(-1) for b in blocks])
    per = width * row_mult
    total = -(-flat.shape[0] // per) * per
    return jnp.pad(flat, (0, total - flat.shape[0])).reshape(total // width, width)


def _pack_dest(fulls, axes, width, dtype, row_mult):
    rows = []
    for f, ax in zip(fulls, axes):
        f = f.astype(dtype)
        if ax is None:
            rows.append(jnp.broadcast_to(f.reshape(1, -1), (N_DEV, f.size)))
        else:
            shp = f.shape
            f = f.reshape(shp[:ax] + (N_DEV, shp[ax] // N_DEV) + shp[ax + 1:])
            rows.append(jnp.moveaxis(f, ax, 0).reshape(N_DEV, -1))
    flat = jnp.concatenate(rows, axis=1)
    per = width * row_mult
    total = -(-flat.shape[1] // per) * per
    return jnp.pad(flat, ((0, 0), (0, total - flat.shape[1]))).reshape(N_DEV, total // width, width)


def _unpack(packed, shapes):
    flat = packed.reshape(-1)
    out, off = [], 0
    for s in shapes:
        n = 1
        for d in s:
            n *= d
        out.append(flat[off:off + n].reshape(s))
        off += n
    return out


def _unpack_gathered(gathered, shapes, axes):
    flat = gathered.reshape(N_DEV, -1)
    out, off = [], 0
    for s, ax in zip(shapes, axes):
        n = 1
        for d in s:
            n *= d
        blk = jnp.moveaxis(flat[:, off:off + n].reshape((N_DEV,) + tuple(s)), 0, ax)
        out.append(blk.reshape(tuple(s[:ax]) + (N_DEV * s[ax],) + tuple(s[ax + 1:])))
        off += n
    return out


def _shard_cols(blocks, a, b):
    shard = blocks[0].shape[1]
    out = []
    while a < b:
        d = a // shard
        hi = min(b, (d + 1) * shard)
        out.append(blocks[d][:, a - d * shard:hi - d * shard])
        a = hi
    return out


def _permute_w_in(blocks):
    main = jnp.concatenate(_shard_cols(blocks, O_GQ, O_BL) + _shard_cols(blocks, O_GZ, O_END)
                           + _shard_cols(blocks, O_FQ, O_FL), axis=1)
    pad = jnp.zeros((blocks[0].shape[0], LANE - 24), blocks[0].dtype)
    small = jnp.concatenate(_shard_cols(blocks, O_FL, O_GQ) + _shard_cols(blocks, O_BL, O_GZ) + [pad], axis=1)
    return main, small


_W_IN_SEGS = ((O_FQ, O_FL, True, C_FQ), (O_FL, O_GQ, False, 0), (O_GQ, O_BL, True, C_GQ), (O_BL, O_GZ, False, 8),
              (O_GZ, O_END, True, C_GZ))


def _unpermute_cols(main, small, a, b):
    out = []
    for s0, s1, is_main, t0 in _W_IN_SEGS:
        lo, hi = max(a, s0), min(b, s1)
        if lo < hi:
            out.append((main if is_main else small)[:, t0 + lo - s0:t0 + hi - s0])
    return jnp.concatenate(out, axis=1)


def _lanes(vec, start):
    return jnp.pad(vec.astype(F32), (start, LANE - start - vec.shape[0])).reshape(1, LANE)


BIG = ("w_in", "w_branch_a", "w_branch_b", "w_out", "w_up", "w_down")
BIG_AXES = (2, 2, 1, 1, 2, 1)
SHARDED_SMALL = ("meta_tokens", "gdn_conv_w", "ffn_conv_w")
SHARDED_SMALL_AXES = (1, 2, 2)
REPL = ("norm1_g", "fox_f_bias", "fox_q_norm_g", "fox_k_norm_g", "gdn_a_log", "gdn_dt_bias", "gdn_norm_g", "norm2_g")
ORDER = ("meta_tokens", "norm1_g", "w_in", "fox_f_bias", "fox_q_norm_g", "fox_k_norm_g", "gdn_conv_w", "gdn_a_log",
         "gdn_dt_bias", "gdn_norm_g", "w_branch_a", "w_branch_b", "w_out", "norm2_g", "w_up", "ffn_conv_w", "w_down")


def _layer_fwd(h_res, wl, pad_rows):
    lp = h_res.shape[0]
    sv = {"res_in": h_res}
    (h1,) = _rowwise(_rmsnorm_fn, [(h_res, D, 0)], [wl["norm1_g"]], [(D, BF16)], "rmsnorm1")
    proj = _matmul(h1, wl["w_main"], "nn", F32, name="mm_in")
    small = _matmul(h1, wl["w_small"], "nn", F32, name="mm_in_small")
    sv.update(h1=h1, proj=proj, small=small)

    fox_fn = functools.partial(_fox_prep_fn, pad_rows)
    qh, kh, logf = _rowwise(fox_fn, [(proj, FOX_W, C_FQ // FOX_W), (proj, FOX_W, C_FK // FOX_W), (small, LANE, 0)],
                            [wl["qg"], wl["kg"], wl["fb"]], [(FOX_W, BF16), (FOX_W, BF16), (LANE, F32)], "fox_prep")
    fsum = _cumsum_rows(logf, False, "fox_cumsum")
    f8 = fsum[:, :8].T
    fcol, frow = f8.reshape(8, lp, 1), f8.reshape(8, 1, lp)
    o_a, lse = _fox_fwd(qh, kh, proj, fcol, frow, pad_rows, C_FV // LANE)
    y_a = _matmul(o_a, wl["w_branch_a"], "nn", F32, name="mm_branch_a")
    sv.update(qh=qh, kh=kh, fcol=fcol, frow=frow, o_a=o_a, lse=lse)

    gq, gk, gv, bg = _gdn_act(proj, small, wl["gdn_conv_w"], wl["alog"], wl["dtb"])
    bcol = bg[:, 8:16].T.reshape(8, lp, 1)
    g8 = bg[:, 16:24].T
    gcol, grow = g8.reshape(8, lp, 1), g8.reshape(8, lp // CHUNK, 1, CHUNK)
    o_raw, sprev = _gdn_fwd(gq, gk, gv, bcol, gcol, grow)
    (o_b,) = _rowwise(_gdn_post_fn, [(o_raw, GDN_W, 0), (proj, GDN_W, C_GZ // GDN_W)], [wl["gn"]], [(GDN_W, BF16)],
                      "gdn_post")
    y_b = _matmul(o_b, wl["w_branch_b"], "nn", F32, name="mm_branch_b")
    sv.update(gq=gq, gk=gk, gv=gv, bcol=bcol, gcol=gcol, grow=grow, o_raw=o_raw, sprev=sprev, o_b=o_b)

    (mixed,) = _rowwise(_merge_fn, [(proj, D, C_GATE // D), (proj, D, C_GATE // D + 1), (y_a, D, 0), (y_b, D, 0)], [],
                        [(D, BF16)], "merge")
    res_mid = _matmul(mixed, wl["w_out"], "nn", F32, add=h_res, name="mm_out")
    sv.update(y_a=y_a, y_b=y_b, mixed=mixed, res_mid=res_mid)

    (h2,) = _rowwise(_rmsnorm_fn, [(res_mid, D, 0)], [wl["norm2_g"]], [(D, BF16)], "rmsnorm2")
    up_pre = _matmul(h2, wl["w_up"], "nn", F32, name="mm_up")
    act = _ffn_act(up_pre, wl["ffn_conv_w"])
    out = _matmul(act, wl["w_down"], "nn", F32, add=res_mid, name="mm_down")
    sv.update(h2=h2, up_pre=up_pre, act=act)
    return out, sv


def _layer_bwd(dres, wl, sv, pad_rows):
    lp = dres.shape[0]
    gw = {}
    gw["w_down"] = _matmul(sv["act"], dres, "tn", F32, name="mm_dw_down")
    dact = _matmul(dres, wl["w_down"], "nt", BF16, name="mm_dact")
    dup, gw["ffn_conv_w"] = _ffn_act_bwd(sv["up_pre"], wl["ffn_conv_w"], dact)
    dup_pre = _conv_bwd_x(dup, wl["ffn_conv_w"], pad_rows, D_FF, "ffn_conv_bwd")
    gw["w_up"] = _matmul(sv["h2"], dup_pre, "tn", F32, name="mm_dw_up")
    dh2 = _matmul(dup_pre, wl["w_up"], "nt", F32, name="mm_dh2")
    (dmid,), (gw["norm2_g"],) = _rowwise_bwd(_rmsnorm_fn, [(sv["res_mid"], D, 0)], [wl["norm2_g"]], [(dh2, D, 0)],
                                             "rmsnorm2_bwd", pad_rows, [F32], adds=[(dres, D, 0)])
    gw["w_out"] = _matmul(sv["mixed"], dmid, "tn", F32, name="mm_dw_out")
    dmixed = _matmul(dmid, wl["w_out"], "nt", F32, name="mm_dmixed")
    proj, small = sv["proj"], sv["small"]
    (dg0, dg1, dya, dyb), _ = _rowwise_bwd(
        _merge_fn, [(proj, D, C_GATE // D), (proj, D, C_GATE // D + 1), (sv["y_a"], D, 0), (sv["y_b"], D, 0)], [],
        [(dmixed, D, 0)], "merge_bwd", pad_rows, [BF16, BF16, BF16, BF16])
    gw["w_branch_a"] = _matmul(sv["o_a"], dya, "tn", F32, name="mm_dw_a")
    do_a = _matmul(dya, wl["w_branch_a"], "nt", BF16, name="mm_do_a")
    gw["w_branch_b"] = _matmul(sv["o_b"], dyb, "tn", F32, name="mm_dw_b")
    do_b = _matmul(dyb, wl["w_branch_b"], "nt", F32, name="mm_do_b")

    (do_raw, dgz), (gw["gn"],) = _rowwise_bwd(_gdn_post_fn, [(sv["o_raw"], GDN_W, 0), (proj, GDN_W, C_GZ // GDN_W)],
                                              [wl["gn"]], [(do_b, GDN_W, 0)], "gdn_post_bwd", pad_rows, [F32, BF16])
    dgq, dgk, dgv, dbcol, dgcol, dgrow = _gdn_bwd(sv["gq"], sv["gk"], sv["gv"], sv["bcol"], sv["gcol"], sv["grow"],
                                                   sv["sprev"], do_raw)
    db8 = dbcol.reshape(8, lp).T
    dg8 = (dgcol.reshape(8, lp) + dgrow.reshape(8, lp)).T
    dbg = jnp.concatenate([jnp.zeros((lp, 8), F32), db8, dg8, jnp.zeros((lp, LANE - 24), F32)], axis=1)
    dconv, dsmall_g, gw["alog"], gw["dtb"], gw["gdn_conv_w"] = _gdn_act_bwd(
        proj, small, wl["gdn_conv_w"], wl["alog"], wl["dtb"], dgq, dgk, dgv, dbg)
    dqkv = _conv_bwd_x(dconv, wl["gdn_conv_w"], pad_rows, GDN_W, "gdn_conv_bwd")

    dqh, dfq, delta = _fox_bwd_q(sv["qh"], sv["kh"], proj, sv["fcol"], sv["frow"], sv["o_a"], do_a, sv["lse"],
                                 pad_rows, C_FV // LANE)
    dkh, dvh, dfk = _fox_bwd_kv(sv["qh"], sv["kh"], proj, sv["fcol"], sv["frow"], do_a, sv["lse"], delta, pad_rows,
                                C_FV // LANE)
    df8 = (dfq.reshape(8, lp) + dfk.reshape(8, lp)).T
    dlogf = _cumsum_rows(jnp.pad(df8, ((0, 0), (0, LANE - 8))), True, "fox_cumsum_bwd")
    fox_fn = functools.partial(_fox_prep_fn, pad_rows)
    (dfq_p, dfk_p, dsmall_f), (gw["qg"], gw["kg"], gw["fb"]) = _rowwise_bwd(
        fox_fn, [(proj, FOX_W, C_FQ // FOX_W), (proj, FOX_W, C_FK // FOX_W), (small, LANE, 0)],
        [wl["qg"], wl["kg"], wl["fb"]], [(dqh, FOX_W, 0), (dkh, FOX_W, 0), (dlogf, LANE, 0)],
        "fox_prep_bwd", pad_rows, [BF16, BF16, F32], adds=[None, None, (dsmall_g, LANE, 0)])

    dproj = jnp.concatenate([dqkv, dgz, dg0, dg1, dfq_p, dfk_p, dvh], axis=1)
    gw["w_main"] = _matmul(sv["h1"], dproj, "tn", F32, name="mm_dw_main")
    gw["w_small"] = _matmul(sv["h1"], dsmall_f, "tn", F32, name="mm_dw_small")
    dh1 = _matmul(dproj, wl["w_main"], "nt", F32, name="mm_dh1")
    dh1 = _matmul(dsmall_f, wl["w_small"], "nt", F32, add=dh1, name="mm_dh1_small")
    (din,), (gw["norm1_g"],) = _rowwise_bwd(_rmsnorm_fn, [(sv["res_in"], D, 0)], [wl["norm1_g"]], [(dh1, D, 0)],
                                            "rmsnorm1_bwd", pad_rows, [F32], adds=[(dmid, D, 0)])
    return din, gw


def kernel(x, meta_tokens, norm1_g, w_in, fox_f_bias, fox_q_norm_g, fox_k_norm_g, gdn_conv_w, gdn_a_log, gdn_dt_bias, gdn_norm_g, w_branch_a, w_branch_b, w_out, norm2_g, w_up, ffn_conv_w, w_down, loss_target, m_meta_tokens, m_norm1_g, m_w_in, m_fox_f_bias, m_fox_q_norm_g, m_fox_k_norm_g, m_gdn_conv_w, m_gdn_a_log, m_gdn_dt_bias, m_gdn_norm_g, m_w_branch_a, m_w_branch_b, m_w_out, m_norm2_g, m_w_up, m_ffn_conv_w, m_w_down, v_meta_tokens, v_norm1_g, v_w_in, v_fox_f_bias, v_fox_q_norm_g, v_fox_k_norm_g, v_gdn_conv_w, v_gdn_a_log, v_gdn_dt_bias, v_gdn_norm_g, v_w_branch_a, v_w_branch_b, v_w_out, v_norm2_g, v_w_up, v_ffn_conv_w, v_w_down):
    w = dict(meta_tokens=meta_tokens, norm1_g=norm1_g, w_in=w_in, fox_f_bias=fox_f_bias, fox_q_norm_g=fox_q_norm_g,
             fox_k_norm_g=fox_k_norm_g, gdn_conv_w=gdn_conv_w, gdn_a_log=gdn_a_log, gdn_dt_bias=gdn_dt_bias,
             gdn_norm_g=gdn_norm_g, w_branch_a=w_branch_a, w_branch_b=w_branch_b, w_out=w_out, norm2_g=norm2_g,
             w_up=w_up, ffn_conv_w=ffn_conv_w, w_down=w_down)
    mom = dict(meta_tokens=m_meta_tokens, norm1_g=m_norm1_g, w_in=m_w_in, fox_f_bias=m_fox_f_bias,
               fox_q_norm_g=m_fox_q_norm_g, fox_k_norm_g=m_fox_k_norm_g, gdn_conv_w=m_gdn_conv_w,
               gdn_a_log=m_gdn_a_log, gdn_dt_bias=m_gdn_dt_bias, gdn_norm_g=m_gdn_norm_g, w_branch_a=m_w_branch_a,
               w_branch_b=m_w_branch_b, w_out=m_w_out, norm2_g=m_norm2_g, w_up=m_w_up, ffn_conv_w=m_ffn_conv_w,
               w_down=m_w_down)
    var = dict(meta_tokens=v_meta_tokens, norm1_g=v_norm1_g, w_in=v_w_in, fox_f_bias=v_fox_f_bias,
               fox_q_norm_g=v_fox_q_norm_g, fox_k_norm_g=v_fox_k_norm_g, gdn_conv_w=v_gdn_conv_w,
               gdn_a_log=v_gdn_a_log, gdn_dt_bias=v_gdn_dt_bias, gdn_norm_g=v_gdn_norm_g, w_branch_a=v_w_branch_a,
               w_branch_b=v_w_branch_b, w_out=v_w_out, norm2_g=v_norm2_g, w_up=v_w_up, ffn_conv_w=v_ffn_conv_w,
               w_down=v_w_down)
    depth = norm1_g.shape[0]
    seq = x.shape[1]
    l_tok = N_META + seq
    lp = -(-l_tok // LANE) * LANE
    pad_rows = lp - l_tok
    row_start = pad_rows + N_META

    got = {n: _all_gather(w[n].astype(BF16), "gather_" + n) for n in BIG}
    small_shapes = [w[n].shape for n in SHARDED_SMALL]
    gathered_s = _all_gather(_pack([w[n] for n in SHARDED_SMALL], LANE, F32, 8), "gather_small")
    full = dict(zip(SHARDED_SMALL, _unpack_gathered(gathered_s, small_shapes, SHARDED_SMALL_AXES)))

    def join(name, l):
        return jnp.concatenate([got[name][d, l] for d in range(N_DEV)], axis=BIG_AXES[BIG.index(name)] - 1)

    layers = []
    for l in range(depth):
        w_main, w_small = _permute_w_in([got["w_in"][d, l] for d in range(N_DEV)])
        layers.append(dict(
            w_main=w_main, w_small=w_small, w_branch_a=join("w_branch_a", l), w_branch_b=join("w_branch_b", l),
            w_out=join("w_out", l), w_up=join("w_up", l), w_down=join("w_down", l),
            gdn_conv_w=full["gdn_conv_w"][l], ffn_conv_w=full["ffn_conv_w"][l],
            norm1_g=norm1_g[l].reshape(1, D), norm2_g=norm2_g[l].reshape(1, D),
            qg=jnp.tile(fox_q_norm_g[l], 8).reshape(1, FOX_W), kg=jnp.tile(fox_k_norm_g[l], 8).reshape(1, FOX_W),
            fb=_lanes(fox_f_bias[l], 0), alog=_lanes(gdn_a_log[l], 16), dtb=_lanes(gdn_dt_bias[l], 16),
            gn=jnp.tile(gdn_norm_g[l], 8).reshape(1, GDN_W)))

    h_res = jnp.concatenate([jnp.zeros((pad_rows, D), F32), full["meta_tokens"], x[0]], axis=0)
    saved = []
    for l in range(depth):
        h_res, sv = _layer_fwd(h_res, layers[l], pad_rows)
        saved.append(sv)
    dres, loss_part = _loss_head(h_res, loss_target[0], row_start)
    loss = lax.psum(loss_part[0, 0], ("x", "y", "c"))

    gws = [None] * depth
    for l in reversed(range(depth)):
        dres, gws[l] = _layer_bwd(dres, layers[l], saved[l], pad_rows)
    grad_x = dres[row_start:].reshape(x.shape)

    def stack(fn):
        return jnp.stack([fn(g) for g in gws])

    part = dict(
        meta_tokens=dres[pad_rows:row_start],
        norm1_g=stack(lambda g: g["norm1_g"][0]), norm2_g=stack(lambda g: g["norm2_g"][0]),
        fox_f_bias=stack(lambda g: g["fb"][0, 0:8]),
        fox_q_norm_g=stack(lambda g: g["qg"].reshape(8, FOX_DH).sum(0)),
        fox_k_norm_g=stack(lambda g: g["kg"].reshape(8, FOX_DH).sum(0)),
        gdn_conv_w=stack(lambda g: g["gdn_conv_w"]), gdn_a_log=stack(lambda g: g["alog"][0, 16:24]),
        gdn_dt_bias=stack(lambda g: g["dtb"][0, 16:24]),
        gdn_norm_g=stack(lambda g: g["gn"].reshape(8, GDN_DH).sum(0)),
        ffn_conv_w=stack(lambda g: g["ffn_conv_w"]))

    def dest_blocks(name, d):
        if name == "w_in":
            s = w_in.shape[2]
            blks = [_unpermute_cols(g["w_main"], g["w_small"], s * d, s * (d + 1)) for g in gws]
        elif BIG_AXES[BIG.index(name)] == 2:
            s = w[name].shape[2]
            blks = [g[name][:, s * d:s * (d + 1)] for g in gws]
        else:
            s = w[name].shape[1]
            blks = [g[name][s * d:s * (d + 1), :] for g in gws]
        return jnp.stack([b.astype(BF16) for b in blks])

    res = {}
    for n in BIG:
        landed = _all_to_all(jnp.stack([dest_blocks(n, d) for d in range(N_DEV)]), "scatter_" + n)
        res[n] = _sum_adamw(landed, w[n], mom[n], var[n], "adamw_" + n)

    small_names = SHARDED_SMALL + REPL
    small_axes = SHARDED_SMALL_AXES + (None,) * len(REPL)
    landed_s = _all_to_all(_pack_dest([part[n] for n in small_names], small_axes, LANE, F32, 8), "scatter_small")
    shapes_s = [w[n].shape for n in small_names]
    outs = _sum_adamw(landed_s[:, None], *[_pack([d[n] for n in small_names], LANE, F32, 8)[None] for d in (w, mom, var)],
                      "adamw_small")
    for o_idx, packed in enumerate(outs):
        for n, a in zip(small_names, _unpack(packed[0], shapes_s)):
            res.setdefault(n, [None] * 4)[o_idx] = a

    return (loss, grad_x, *[res[n][0] for n in ORDER], *[res[n][1] for n in ORDER],
            *[res[n][2] for n in ORDER], *[res[n][3] for n in ORDER])
```

```python
import functools

import jax
import jax.numpy as jnp
from jax import lax
from jax.experimental import pallas as pl
from jax.experimental.pallas import tpu as pltpu

F32, BF16 = jnp.float32, jnp.bfloat16
MESH = pl.DeviceIdType.MESH

D = 1024
N_META = 16
DEPTH = 4
EPS = 1e-6
NEG = -1e30
FOX_W, FOX_DH = 512, 64
GDN_W, GDN_DH, GDN_H = 1024, 128, 8
CHUNK = 64
D_FF = 2816
N_DEV = 8
ADAM_LR, ADAM_B1, ADAM_B2, ADAM_EPS, ADAM_WD, ADAM_STEP = 0.001, 0.9, 0.999, 1e-08, 0.01, 10

VMEM_LIMIT_BYTES = 48 * 1024 * 1024
MATMUL_VMEM_BUDGET = 36 * 1024 * 1024
ROW_TILE = 128
LANE = 128

C_GQ, C_GK, C_GV, C_GZ, C_GATE, C_FQ, C_FK, C_FV = 0, 1024, 2048, 3072, 4096, 6144, 6656, 7168
W_MAIN = 7680
O_FQ, O_FK, O_FV, O_FL, O_GQ, O_GK, O_GV, O_BL, O_AL, O_GZ, O_GATE, O_END = (
    0, 512, 1024, 1536, 1544, 2568, 3592, 4616, 4624, 4632, 5656, 7704)


def _pick(n, cands):
    for c in cands:
        if n % c == 0:
            return c
    return n


def _call(body, *, name, out_shape, in_specs, out_specs, grid=(), scratch=(), sem=None):
    kw = dict(vmem_limit_bytes=VMEM_LIMIT_BYTES)
    if sem is not None:
        kw["dimension_semantics"] = sem
    return pl.pallas_call(body, name=name, out_shape=out_shape, grid=grid, in_specs=in_specs,
                          out_specs=out_specs, scratch_shapes=list(scratch),
                          compiler_params=pltpu.CompilerParams(**kw))


_DIMS = {"nn": (((1,), (0,)), ((), ())), "nt": (((1,), (1,)), ((), ())), "tn": (((0,), (0,)), ((), ()))}


_DIMS_BATCHED = {"nn": (((2,), (1,)), ((0,), (0,))), "nt": (((2,), (2,)), ((0,), (0,))),
                 "tn": (((1,), (1,)), ((0,), (0,)))}


def _dot(a, b, mode, prec=None):
    dims = _DIMS[mode] if a.ndim == 2 else _DIMS_BATCHED[mode]
    return lax.dot_general(a, b, dims, precision=prec, preferred_element_type=F32)


def _mm_grads(f, mode, a, b, g):
    if mode == "nn":
        return f(g, b, "nt"), f(a, g, "tn")
    if mode == "nt":
        return f(g, b, "nn"), f(g, a, "tn")
    return f(b, g, "nt"), f(a, g, "nn")


@functools.partial(jax.custom_vjp, nondiff_argnums=(2,))
def _mmb(a, b, mode):
    return _dot(a.astype(BF16), b.astype(BF16), mode)


def _mmb_fwd(a, b, mode):
    return _mmb(a, b, mode), (a, b)


def _mmb_bwd(mode, res, g):
    return _mm_grads(_mmb, mode, res[0], res[1], g)


_mmb.defvjp(_mmb_fwd, _mmb_bwd)


def _split(a):
    hi = a.astype(BF16)
    return hi, (a - hi.astype(F32)).astype(BF16)


@functools.partial(jax.custom_vjp, nondiff_argnums=(2,))
def _mmh(a, b, mode):
    ah, al = _split(a)
    bh, bl = _split(b)
    return _dot(ah, bh, mode) + (_dot(ah, bl, mode) + _dot(al, bh, mode))


def _mmh_fwd(a, b, mode):
    return _mmh(a, b, mode), (a, b)


def _mmh_bwd(mode, res, g):
    return _mm_grads(_mmh, mode, res[0], res[1], g)


_mmh.defvjp(_mmh_fwd, _mmh_bwd)


def _dot_sel(sel, x, mode):
    s = sel.astype(BF16)
    x1 = x.astype(BF16)
    x2, x3 = _split(x - x1.astype(F32))
    return _dot(s, x1, mode) + (_dot(s, x2, mode) + _dot(s, x3, mode))


@jax.custom_vjp
def _mms(sel, x):
    return _dot_sel(sel, x, "nn")


def _mms_fwd(sel, x):
    return _dot_sel(sel, x, "nn"), sel


def _mms_bwd(sel, g):
    return jnp.zeros_like(sel), _dot_sel(sel, g, "tn")


_mms.defvjp(_mms_fwd, _mms_bwd)


def _softplus(z):
    return jnp.maximum(z, 0.0) + jnp.log(1.0 + jnp.exp(-jnp.abs(z)))


def _log_sigmoid(z):
    return jnp.minimum(z, 0.0) - jnp.log(1.0 + jnp.exp(-jnp.abs(z)))


def _silu(z):
    return z * jax.nn.sigmoid(z)


def _iota(shape, dim):
    return lax.broadcasted_iota(jnp.int32, shape, dim)


def _inv_unit_lower_raw(n):
    c = n.shape[-1]
    ri, ci = _iota((c, c), 0), _iota((c, c), 1)
    eye = (ri == ci).astype(F32)
    dmask = (ri // 16) == (ci // 16)
    dpart = jnp.where(dmask, n, 0.0)
    lpart = n - dpart
    x = -dpart
    p = eye + x
    for _ in range(3):
        x = _mmh(x, x, "nn")
        p = p + _mmh(p, x, "nn")
    m = -_mmh(p, lpart, "nn")
    q = eye + m
    steps = 1
    while (1 << steps) < c // 16:
        steps += 1
    for _ in range(steps - 1):
        m = _mmh(m, m, "nn")
        q = q + _mmh(q, m, "nn")
    return _mmh(q, p, "nn")


@jax.custom_vjp
def _inv_unit_lower(n):
    return _inv_unit_lower_raw(n)


def _inv_fwd(n):
    t = _inv_unit_lower_raw(n)
    return t, t


def _inv_bwd(t, g):
    c = t.shape[-1]
    strict = _iota((c, c), 0) > _iota((c, c), 1)
    d = -_mmh(_mmh(t, g, "tn"), t, "nt")
    return (jnp.where(strict, d, 0.0),)


_inv_unit_lower.defvjp(_inv_fwd, _inv_bwd)


def _shift_down(x, halo, s):
    if s == 0:
        return x
    xs = pltpu.roll(x, s, 0)
    hs = pltpu.roll(halo, s, 0)
    top = jnp.where(_iota(hs.shape, 0) < s, hs, xs[0:8])
    return jnp.concatenate([top, xs[8:]], axis=0)


def _shift_up(x, halo, s):
    if s == 0:
        return x
    tm = x.shape[0]
    xs = pltpu.roll(x, tm - s, 0)
    hs = pltpu.roll(halo, 8 - s, 0)
    bot = jnp.where(_iota(hs.shape, 0) >= 8 - s, hs, xs[tm - 8:])
    return jnp.concatenate([xs[:tm - 8], bot], axis=0)


def _causal_conv(x, halo, w):
    kk = w.shape[0]
    y = x * w[kk - 1:kk, :]
    for k in range(kk - 1):
        y = y + _shift_down(x, halo, kk - 1 - k) * w[k:k + 1, :]
    return y


def _head_scale(x, width, fn):
    outs = []
    for h in range(x.shape[1] // width):
        seg = x[:, h * width:(h + 1) * width]
        outs.append(seg * fn(jnp.sum(seg * seg, axis=1, keepdims=True)))
    return jnp.concatenate(outs, axis=1)


def _matmul(a, b, mode, out_dtype, add=None, name="mm"):
    if mode == "nn":
        (m, k), n = a.shape, b.shape[1]
    elif mode == "nt":
        (m, k), n = a.shape, b.shape[0]
    else:
        (k, m), n = a.shape, b.shape[1]
    tm = _pick(m, (1408, 1024, 512, 256, 128) if mode == "tn" else (640, 512, 256, 128))
    tn = _pick(n, (1536, 1408, 1024, 768, 512, 256, 128))
    sa, sb = a.dtype.itemsize, b.dtype.itemsize
    fixed = tm * tn * 4 * (3 + (2 if add is not None else 0))
    tk = 128
    for cand in (k, 2816, 2560, 1664, 1536, 1280, 1024, 832, 768, 640, 512, 256, 128):
        if mode != "tn" and cand != k and cand % LANE:
            continue
        if k % cand == 0 and fixed + 2 * cand * (tm * sa + tn * sb) <= MATMUL_VMEM_BUDGET:
            tk = cand
            break
    nk = k // tk
    a_spec = {"nn": pl.BlockSpec((tm, tk), lambda i, j, kk: (i, kk)),
              "nt": pl.BlockSpec((tm, tk), lambda i, j, kk: (i, kk)),
              "tn": pl.BlockSpec((tk, tm), lambda i, j, kk: (kk, i))}[mode]
    b_spec = {"nn": pl.BlockSpec((tk, tn), lambda i, j, kk: (kk, j)),
              "nt": pl.BlockSpec((tn, tk), lambda i, j, kk: (j, kk)),
              "tn": pl.BlockSpec((tk, tn), lambda i, j, kk: (kk, j))}[mode]
    o_spec = pl.BlockSpec((tm, tn), lambda i, j, kk: (i, j))
    has_add = add is not None

    def body(*refs):
        a_ref, b_ref = refs[0], refs[1]
        add_ref = refs[2] if has_add else None
        o_ref = refs[3] if has_add else refs[2]
        part = _dot(a_ref[...].astype(BF16), b_ref[...].astype(BF16), mode)
        if nk == 1:
            if has_add:
                part = part + add_ref[...].astype(F32)
            o_ref[...] = part.astype(out_dtype)
        else:
            acc = refs[-1]
            kk = pl.program_id(2)

            @pl.when(kk == 0)
            def _():
                acc[...] = part

            @pl.when(kk > 0)
            def _():
                acc[...] += part

            @pl.when(kk == nk - 1)
            def _():
                r = acc[...]
                if has_add:
                    r = r + add_ref[...].astype(F32)
                o_ref[...] = r.astype(out_dtype)

    ins = [a, b] + ([add] if has_add else [])
    specs = [a_spec, b_spec] + ([o_spec] if has_add else [])
    return _call(body, name=name, out_shape=jax.ShapeDtypeStruct((m, n), out_dtype), grid=(m // tm, n // tn, nk),
                 in_specs=specs, out_specs=o_spec,
                 scratch=[pltpu.VMEM((tm, tn), F32)] if nk > 1 else [],
                 sem=("parallel", "parallel", "arbitrary"))(*ins)


def _row_spec(width, colblock, tm):
    return pl.BlockSpec((tm, width), lambda i, cb=colblock: (i, cb))


def _full_spec(arr):
    nd = arr.ndim
    return pl.BlockSpec(arr.shape, lambda i, nd=nd: (0,) * nd)


def _rowwise(fn, rows, params, outs, name, tm=ROW_TILE):
    lp = rows[0][0].shape[0]
    nr, npar = len(rows), len(params)

    def body(*refs):
        row0 = pl.program_id(0) * tm
        vals = [r[...].astype(F32) for r in refs[:nr + npar]]
        res = fn(*vals, row0)
        for o_ref, r in zip(refs[nr + npar:], res):
            o_ref[...] = r.astype(o_ref.dtype)

    out = _call(body, name=name, grid=(lp // tm,),
                out_shape=[jax.ShapeDtypeStruct((lp, w), dt) for w, dt in outs],
                in_specs=[_row_spec(w, cb, tm) for _, w, cb in rows] + [_full_spec(p) for p in params],
                out_specs=[_row_spec(w, 0, tm) for w, _ in outs], sem=("parallel",))(
                    *[r[0] for r in rows], *params)
    return out


def _rowwise_bwd(fn, rows, params, cts, name, pad_rows, grad_dtypes, adds=None, tm=ROW_TILE):
    lp = rows[0][0].shape[0]
    nr, npar, nct = len(rows), len(params), len(cts)
    adds = adds or [None] * nr
    add_list = [a for a in adds if a is not None]
    nadd = len(add_list)

    def body(*refs):
        i = pl.program_id(0)
        row0 = i * tm
        vals = [r[...].astype(F32) for r in refs[:nr + npar]]
        ct_vals = tuple(r[...].astype(F32) for r in refs[nr + npar:nr + npar + nct])
        add_refs = list(refs[nr + npar + nct:nr + npar + nct + nadd])
        outs = refs[nr + npar + nct + nadd:]
        _, vjp = jax.vjp(lambda *args: tuple(fn(*args, row0)), *vals)
        grads = vjp(ct_vals)
        valid = (row0 + _iota((tm, 1), 0)) >= pad_rows
        for idx in range(nr):
            g = jnp.where(valid, grads[idx], 0.0)
            if adds[idx] is not None:
                g = g + add_refs.pop(0)[...].astype(F32)
            outs[idx][...] = g.astype(outs[idx].dtype)
        for idx in range(npar):
            o_ref = outs[nr + idx]

            @pl.when(i == 0)
            def _(o_ref=o_ref):
                o_ref[...] = jnp.zeros_like(o_ref)

            o_ref[...] += grads[nr + idx]

    out = _call(body, name=name, grid=(lp // tm,),
                out_shape=[jax.ShapeDtypeStruct((lp, w), dt) for (_, w, _), dt in zip(rows, grad_dtypes)]
                + [jax.ShapeDtypeStruct(p.shape, F32) for p in params],
                in_specs=[_row_spec(w, cb, tm) for _, w, cb in rows] + [_full_spec(p) for p in params]
                + [_row_spec(w, cb, tm) for _, w, cb in cts] + [_row_spec(w, cb, tm) for _, w, cb in add_list],
                out_specs=[_row_spec(w, 0, tm) for _, w, _ in rows] + [_full_spec(p) for p in params],
                sem=("arbitrary",))(*[r[0] for r in rows], *params, *[c[0] for c in cts], *[a[0] for a in add_list])
    return out[:nr], out[nr:]


def _rmsnorm_fn(x, g, row0):
    return (x * lax.rsqrt(jnp.mean(x * x, axis=1, keepdims=True) + EPS) * g,)


def _fox_prep_fn(pad_rows, fq, fk, small, qg, kg, fb, row0):
    ri, ci = _iota((FOX_W, FOX_W), 0), _iota((FOX_W, FOX_W), 1)
    bd = jnp.where((ri // FOX_DH) == (ci // FOX_DH), 1.0 / FOX_DH, 0.0)

    def hn(x, g):
        return x * lax.rsqrt(_mmh(x * x, bd, "nn") + EPS) * g

    tm = small.shape[0]
    keep = (_iota((tm, LANE), 1) < 8) & ((row0 + _iota((tm, LANE), 0)) >= pad_rows)
    logf = jnp.where(keep, _log_sigmoid(small + fb), 0.0)
    return hn(fq, qg) * (FOX_DH ** -0.5), hn(fk, kg), logf


def _gdn_act_fn(cq, ck, cv, small, alog, dtb):
    tm = small.shape[0]
    q = _head_scale(_silu(cq), GDN_DH, lambda s: lax.rsqrt(s + EPS) * (GDN_DH ** -0.5))
    k = _head_scale(_silu(ck), GDN_DH, lambda s: lax.rsqrt(s + EPS))
    v = _silu(cv)
    lane = _iota((tm, LANE), 1)
    beta = jnp.where((lane >= 8) & (lane < 16), jax.nn.sigmoid(small), 0.0)
    g = jnp.where((lane >= 16) & (lane < 24), -jnp.exp(alog) * _softplus(small + dtb), 0.0)
    ri, ci = _iota((tm, tm), 0), _iota((tm, tm), 1)
    tri = jnp.where(((ri // CHUNK) == (ci // CHUNK)) & (ci <= ri), 1.0, 0.0)
    return q, k, v, beta + _mms(tri, g)


def _gdn_post_fn(o, gz, gn, row0):
    return (_head_scale(o, GDN_DH, lambda s: lax.rsqrt(s * (1.0 / GDN_DH) + EPS)) * gn * _silu(gz),)


def _merge_fn(g0, g1, ya, yb, row0):
    return (jax.nn.sigmoid(g0) * ya + jax.nn.sigmoid(g1) * yb,)


def _cumsum_rows(x, reverse, name):
    lp, w = x.shape
    tm = _pick(lp, (640, 512, 256, 128))
    nt = lp // tm

    def body(x_ref, o_ref, carry):
        i = pl.program_id(0)

        @pl.when(i == 0)
        def _():
            carry[...] = jnp.zeros_like(carry)

        ri, ci = _iota((tm, tm), 0), _iota((tm, tm), 1)
        tri = jnp.where((ci >= ri) if reverse else (ci <= ri), 1.0, 0.0)
        blk = x_ref[...]
        o_ref[...] = _dot_sel(tri, blk, "nn") + carry[0:1, :]
        carry[...] = carry[...] + jnp.sum(blk, axis=0, keepdims=True)

    idx = (lambda i: (nt - 1 - i, 0)) if reverse else (lambda i: (i, 0))
    return _call(body, name=name, grid=(nt,), out_shape=jax.ShapeDtypeStruct((lp, w), F32),
                 in_specs=[pl.BlockSpec((tm, w), idx)], out_specs=pl.BlockSpec((tm, w), idx),
                 scratch=[pltpu.VMEM((8, w), F32)], sem=("arbitrary",))(x)


def _fox_scores(q, k, fq, fk, hh, qpos0, kpos0, pad_rows, masked):
    tq, tk = q.shape[0], k.shape[0]
    lane = _iota(q.shape, 1)
    sel = (lane < FOX_DH) if hh == 0 else (lane >= FOX_DH)
    s = _dot(jnp.where(sel, q, jnp.zeros_like(q)), k, "nt") + fq - fk
    if not masked:
        return s, None, sel
    qpos = qpos0 + _iota((tq, tk), 0)
    kpos = kpos0 + _iota((tq, tk), 1)
    mask = (kpos <= qpos) & (kpos >= pad_rows)
    return jnp.where(mask, s, NEG), mask, sel


def _probs(s, mask, shift):
    p = jnp.exp(s - shift)
    return p if mask is None else jnp.where(mask, p, 0.0)


def _both_variants(needs_mask, fn):
    @pl.when(needs_mask)
    def _():
        fn(True)

    @pl.when(jnp.logical_not(needs_mask))
    def _():
        fn(False)


def _lane_col(blk, lane_idx):
    return jnp.sum(jnp.where(_iota(blk.shape, 1) == lane_idx, blk, 0.0), axis=1, keepdims=True)


def _to_lanes(cols, width=LANE):
    lane = _iota((cols[0].shape[0], width), 1)
    out = jnp.zeros((cols[0].shape[0], width), F32)
    for idx, c in enumerate(cols):
        out = jnp.where(lane == idx, c, out)
    return out


def _fox_fwd(q, k, v, fsum, frow, pad_rows, v_col):
    lp = q.shape[0]
    t = _pick(lp, (640, 512, 256, 128))
    n = lp // t

    def body(q_ref, k_ref, v_ref, f_ref, fk_ref, o_ref, lse_ref, acc, m_s, l_s, fq_s):
        pr, i, j = pl.program_id(0), pl.program_id(1), pl.program_id(2)

        @pl.when(j == 0)
        def _():
            acc[...] = jnp.zeros_like(acc)
            m_s[...] = jnp.full_like(m_s, NEG)
            l_s[...] = jnp.zeros_like(l_s)
            for hh in range(2):
                fq_s[hh] = _lane_col(f_ref[...], 2 * pr + hh)

        def step(masked):
            for hh in range(2):
                s, mask, _ = _fox_scores(q_ref[...], k_ref[...], fq_s[hh], fk_ref[hh], hh, i * t, j * t, pad_rows,
                                         masked)
                m_prev = m_s[hh]
                m_new = jnp.maximum(m_prev, jnp.max(s, axis=1, keepdims=True))
                p = _probs(s, mask, m_new)
                alpha = jnp.exp(m_prev - m_new)
                l_s[hh] = alpha * l_s[hh] + jnp.sum(p, axis=1, keepdims=True)
                acc[hh] = alpha * acc[hh] + _dot(p.astype(BF16), v_ref[...].astype(BF16), "nn")
                m_s[hh] = m_new

        @pl.when(j <= i)
        def _():
            _both_variants((j == i) | (j == 0), step)

        @pl.when(j == i)
        def _():
            outs, lses = [], []
            for hh in range(2):
                l = l_s[hh]
                ok = l > 0.0
                outs.append(acc[hh] * jnp.where(ok, 1.0 / jnp.where(ok, l, 1.0), 0.0))
                lses.append(jnp.where(ok, m_s[hh] + jnp.log(jnp.where(ok, l, 1.0)), 0.0))
            lane = _iota((t, LANE), 1)
            o_ref[...] = jnp.where(lane < FOX_DH, outs[0], outs[1]).astype(o_ref.dtype)
            lse_ref[...] = _to_lanes(lses)

    qspec = pl.BlockSpec((t, LANE), lambda p, i, j: (i, p))
    kspec = pl.BlockSpec((t, LANE), lambda p, i, j: (jnp.minimum(j, i), p))
    vspec = pl.BlockSpec((t, LANE), lambda p, i, j: (jnp.minimum(j, i), v_col + p))
    fspec = pl.BlockSpec((t, LANE), lambda p, i, j: (i, 0))
    rspec = pl.BlockSpec((2, 1, t), lambda p, i, j: (p, 0, jnp.minimum(j, i)))
    return _call(body, name="fox_fwd", grid=(FOX_W // LANE, n, n),
                 out_shape=[jax.ShapeDtypeStruct((lp, FOX_W), BF16), jax.ShapeDtypeStruct((lp, FOX_W), F32)],
                 in_specs=[qspec, kspec, vspec, fspec, rspec], out_specs=[qspec, qspec],
                 scratch=[pltpu.VMEM((2, t, LANE), F32), pltpu.VMEM((2, t, 1), F32), pltpu.VMEM((2, t, 1), F32),
                          pltpu.VMEM((2, t, 1), F32)],
                 sem=("parallel", "parallel", "arbitrary"))(q, k, v, fsum, frow)


def _fox_delta_fn(o, do, row0):
    ri, ci = _iota((FOX_W, LANE), 0), _iota((FOX_W, LANE), 1)
    sel = jnp.where((ri // FOX_DH) == ci, 1.0, 0.0).astype(BF16)
    x = o * do
    x1 = x.astype(BF16)
    x2, x3 = _split(x - x1.astype(F32))
    return (_dot(x1, sel, "nn") + (_dot(x2, sel, "nn") + _dot(x3, sel, "nn")),)


def _fox_bwd(q, k, v, fsum, frow, do, lse, delta, pad_rows, v_col):
    lp = q.shape[0]
    t = _pick(lp, (640, 512, 256, 128))
    n = lp // t

    def body(q_ref, k_ref, v_ref, f_ref, fk_ref, do_ref, lse_ref, dl_ref,
             dq_ref, dk_ref, dv_ref, dfq_ref, dfk_ref, dka, dva, dfa):
        pr, j, i = pl.program_id(0), pl.program_id(1), pl.program_id(2)
        lane = _iota((t, LANE), 1)

        @pl.when((j == 0) & (i == 0))
        def _():
            dq_ref[...] = jnp.zeros_like(dq_ref)
            dfq_ref[...] = jnp.zeros_like(dfq_ref)

        @pl.when(i == 0)
        def _():
            dka[...] = jnp.zeros_like(dka)
            dva[...] = jnp.zeros_like(dva)
            dfa[...] = jnp.zeros_like(dfa)

        def step(masked):
            rows = pl.ds(pl.multiple_of(i * t, t), t)
            dq_add = jnp.zeros((t, LANE), F32)
            rowsums = []
            for hh in range(2):
                fq = _lane_col(f_ref[...], 2 * pr + hh)
                s, mask, sel = _fox_scores(q_ref[...], k_ref[...], fq, fk_ref[hh], hh, i * t, j * t, pad_rows, masked)
                p = _probs(s, mask, _lane_col(lse_ref[...], hh))
                dop = jnp.where(sel, do_ref[...], jnp.zeros_like(do_ref[...]))
                ds = p * (_dot(dop, v_ref[...].astype(BF16), "nt") - _lane_col(dl_ref[...], 2 * pr + hh))
                dsb = ds.astype(BF16)
                dva[hh] += _dot(p.astype(BF16), do_ref[...], "tn")
                dka[hh] += _dot(dsb, q_ref[...], "tn")
                dfa[hh] -= jnp.sum(ds, axis=0, keepdims=True)
                dq_add = dq_add + _dot(dsb, jnp.where(sel, k_ref[...], jnp.zeros_like(k_ref[...])), "nn")
                rowsums.append(jnp.sum(ds, axis=1, keepdims=True))
            dq_ref[rows, :] += dq_add
            dfq_ref[rows, :] += _to_lanes(rowsums)

        @pl.when(i >= j)
        def _():
            _both_variants((j == i) | (j == 0), step)

        @pl.when(i == n - 1)
        def _():
            dk_ref[...] = jnp.where(lane < FOX_DH, dka[0], dka[1])
            dv_ref[...] = jnp.where(lane < FOX_DH, dva[0], dva[1]).astype(dv_ref.dtype)
            dfk_ref[...] = dfa[...]

    qspec = pl.BlockSpec((t, LANE), lambda p, j, i: (jnp.maximum(i, j), p))
    f_q = pl.BlockSpec((t, LANE), lambda p, j, i: (jnp.maximum(i, j), 0))
    kspec = pl.BlockSpec((t, LANE), lambda p, j, i: (j, p))
    vspec = pl.BlockSpec((t, LANE), lambda p, j, i: (j, v_col + p))
    rspec = pl.BlockSpec((2, 1, t), lambda p, j, i: (p, 0, j))
    whole = pl.BlockSpec((lp, LANE), lambda p, j, i: (0, p))
    wide = jax.ShapeDtypeStruct((lp, FOX_W), F32)
    return _call(body, name="fox_bwd", grid=(FOX_W // LANE, n, n),
                 out_shape=[wide, wide, jax.ShapeDtypeStruct((lp, FOX_W), BF16), wide,
                            jax.ShapeDtypeStruct((8, 1, lp), F32)],
                 in_specs=[qspec, kspec, vspec, f_q, rspec, qspec, qspec, f_q],
                 out_specs=[whole, kspec, kspec, whole, rspec],
                 scratch=[pltpu.VMEM((2, t, LANE), F32), pltpu.VMEM((2, t, LANE), F32), pltpu.VMEM((2, 1, t), F32)],
                 sem=("parallel", "arbitrary", "arbitrary"))(q, k, v, fsum, frow, do, lse, delta)


def _gdn_chunk(q, k, v, beta, gcol, grow, s, inv):
    c = q.shape[-2]
    ri, ci = _iota((c, c), 0), _iota((c, c), 1)
    dec = jnp.exp(jnp.where(ri >= ci, gcol - grow, NEG))
    dec_strict = jnp.where(ri > ci, dec, 0.0)
    eg = jnp.exp(gcol)
    kb = k * beta
    t = inv(_mmb(kb, k, "nt") * dec_strict)
    u_hat = _mmh(t, v * beta, "nn")
    w = _mmh(t, kb * eg, "nn")
    u = u_hat - _mmb(w, s, "nn")
    o = _mmb(q * eg, s, "nn") + _mmb(_mmb(q, k, "nt") * dec, u, "nn")
    glast = jnp.sum(jnp.where(_iota((1, c), 1) == c - 1, grow, 0.0), axis=-1, keepdims=True)
    s_new = s * jnp.exp(glast) + _mmb(k * jnp.exp(glast - gcol), u, "tn")
    return o, s_new


def _gdn_specs(lp, reverse):
    n = lp // CHUNK
    pos = (lambda c: n - 1 - c) if reverse else (lambda c: c)
    wide = pl.BlockSpec((CHUNK, GDN_W), lambda c: (pos(c), 0))
    lanes = pl.BlockSpec((CHUNK, LANE), lambda c: (pos(c), 0))
    row = pl.BlockSpec((GDN_H, 1, 1, CHUNK), lambda c: (0, pos(c), 0, 0))
    st = pl.BlockSpec((GDN_H, 1, GDN_DH, GDN_DH), lambda c: (0, pos(c), 0, 0))
    return n, wide, lanes, row, st


def _heads(ref):
    return jnp.stack([ref[:, h * GDN_DH:(h + 1) * GDN_DH] for h in range(GDN_H)])


def _put_heads(ref, val):
    for h in range(GDN_H):
        ref[:, h * GDN_DH:(h + 1) * GDN_DH] = val[h]


def _head_cols(blk, lane0):
    return jnp.stack([_lane_col(blk, lane0 + h) for h in range(GDN_H)])


def _gdn_fwd(q, k, v, bg, grow):
    lp = q.shape[0]
    n, wide, lanes, row, st = _gdn_specs(lp, False)

    def body(q_ref, k_ref, v_ref, bg_ref, gr_ref, o_ref, sp_ref, s_scr):
        @pl.when(pl.program_id(0) == 0)
        def _():
            s_scr[...] = jnp.zeros_like(s_scr)

        s = s_scr[...]
        sp_ref[:, 0] = s
        bg_blk = bg_ref[...]
        o, s_new = _gdn_chunk(_heads(q_ref), _heads(k_ref), _heads(v_ref), _head_cols(bg_blk, 8),
                              _head_cols(bg_blk, 16), gr_ref[:, 0], s, _inv_unit_lower_raw)
        _put_heads(o_ref, o)
        s_scr[...] = s_new

    return _call(body, name="gdn_fwd", grid=(n,),
                 out_shape=[jax.ShapeDtypeStruct((lp, GDN_W), F32),
                            jax.ShapeDtypeStruct((GDN_H, n, GDN_DH, GDN_DH), F32)],
                 in_specs=[wide, wide, wide, lanes, row], out_specs=[wide, st],
                 scratch=[pltpu.VMEM((GDN_H, GDN_DH, GDN_DH), F32)], sem=("arbitrary",))(q, k, v, bg, grow)


def _gdn_bwd(q, k, v, bg, grow, sprev, do):
    lp = q.shape[0]
    n, wide, lanes, row, st = _gdn_specs(lp, True)

    def body(q_ref, k_ref, v_ref, bg_ref, gr_ref, sp_ref, do_ref, dq_ref, dk_ref, dv_ref, dbg_ref, dgr_ref, ds_scr):
        @pl.when(pl.program_id(0) == 0)
        def _():
            ds_scr[...] = jnp.zeros_like(ds_scr)

        fn = functools.partial(_gdn_chunk, inv=_inv_unit_lower)
        bg_blk = bg_ref[...]
        _, vjp = jax.vjp(fn, _heads(q_ref), _heads(k_ref), _heads(v_ref), _head_cols(bg_blk, 8),
                         _head_cols(bg_blk, 16), gr_ref[:, 0], sp_ref[:, 0])
        dq, dk, dv, db, dgc, dgr, ds = vjp((_heads(do_ref), ds_scr[...]))
        _put_heads(dq_ref, dq)
        _put_heads(dk_ref, dk)
        _put_heads(dv_ref, dv)
        lane = _iota((CHUNK, LANE), 1)
        dbg = jnp.zeros((CHUNK, LANE), F32)
        for h in range(GDN_H):
            dbg = jnp.where(lane == 8 + h, db[h], jnp.where(lane == 16 + h, dgc[h], dbg))
        dbg_ref[...] = dbg
        dgr_ref[:, 0] = dgr
        ds_scr[...] = ds

    wshape = jax.ShapeDtypeStruct((lp, GDN_W), F32)
    return _call(body, name="gdn_bwd", grid=(n,),
                 out_shape=[wshape, wshape, wshape, jax.ShapeDtypeStruct((lp, LANE), F32),
                            jax.ShapeDtypeStruct((GDN_H, n, 1, CHUNK), F32)],
                 in_specs=[wide, wide, wide, lanes, row, st, wide], out_specs=[wide, wide, wide, lanes, row],
                 scratch=[pltpu.VMEM((GDN_H, GDN_DH, GDN_DH), F32)], sem=("arbitrary",))(
                     q, k, v, bg, grow, sprev, do)


def _halo_prev(width, colblock, tm):
    return pl.BlockSpec((8, width), lambda i, cb=colblock: (jnp.maximum(i * (tm // 8) - 1, 0), cb))


def _gdn_act(proj, small, conv_w, alog_row, dtb_row, tm=ROW_TILE):
    lp = proj.shape[0]

    def body(xq, xk, xv, hq, hk, hv, wq, wk, wv, sm, al, dt, q_ref, k_ref, v_ref, bg_ref):
        first = (pl.program_id(0) > 0).astype(F32)
        cs = [_causal_conv(x[...], h[...] * first, w[...]) for x, h, w in ((xq, hq, wq), (xk, hk, wk), (xv, hv, wv))]
        q, k, v, bg = _gdn_act_fn(cs[0], cs[1], cs[2], sm[...], al[...], dt[...])
        q_ref[...], k_ref[...], v_ref[...], bg_ref[...] = q, k, v, bg

    wide = jax.ShapeDtypeStruct((lp, GDN_W), F32)
    wspec = [pl.BlockSpec((4, GDN_W), lambda i, c=c: (0, c)) for c in range(3)]
    return _call(body, name="gdn_act", grid=(lp // tm,),
                 out_shape=[wide, wide, wide, jax.ShapeDtypeStruct((lp, LANE), F32)],
                 in_specs=[_row_spec(GDN_W, c, tm) for c in range(3)] + [_halo_prev(GDN_W, c, tm) for c in range(3)]
                 + wspec + [_row_spec(LANE, 0, tm), _full_spec(alog_row), _full_spec(dtb_row)],
                 out_specs=[_row_spec(GDN_W, 0, tm)] * 3 + [_row_spec(LANE, 0, tm)], sem=("parallel",))(
                     proj, proj, proj, proj, proj, proj, conv_w, conv_w, conv_w, small, alog_row, dtb_row)


def _gdn_act_bwd(proj, small, conv_w, alog_row, dtb_row, dq, dk, dv, dbg, tm=ROW_TILE):
    lp = proj.shape[0]

    def body(xq, xk, xv, hq, hk, hv, wq, wk, wv, sm, al, dt, dq_r, dk_r, dv_r, dbg_r,
             dc_ref, dsm_ref, dal_ref, ddt_ref, dw_ref):
        i = pl.program_id(0)
        first = (i > 0).astype(F32)
        xs = [(x[...], h[...] * first, w[...]) for x, h, w in ((xq, hq, wq), (xk, hk, wk), (xv, hv, wv))]
        cs = [_causal_conv(*t) for t in xs]
        _, vjp = jax.vjp(_gdn_act_fn, cs[0], cs[1], cs[2], sm[...], al[...], dt[...])
        dcq, dck, dcv, dsm, dal, ddt = vjp((dq_r[...], dk_r[...], dv_r[...], dbg_r[...]))
        dsm_ref[...] = dsm

        @pl.when(i == 0)
        def _():
            dal_ref[...] = jnp.zeros_like(dal_ref)
            ddt_ref[...] = jnp.zeros_like(ddt_ref)
            dw_ref[...] = jnp.zeros_like(dw_ref)

        dal_ref[...] += dal
        ddt_ref[...] += ddt
        for c, (dc, (x, h, w)) in enumerate(zip((dcq, dck, dcv), xs)):
            dc_ref[:, c * GDN_W:(c + 1) * GDN_W] = dc
            rows = [jnp.sum(_shift_down(x, h, 3 - kk) * dc, axis=0, keepdims=True) for kk in range(4)]
            dw_ref[:, c * GDN_W:(c + 1) * GDN_W] += jnp.concatenate(rows, axis=0)

    wspec = [pl.BlockSpec((4, GDN_W), lambda i, c=c: (0, c)) for c in range(3)]
    row128 = jax.ShapeDtypeStruct((1, LANE), F32)
    return _call(body, name="gdn_act_bwd", grid=(lp // tm,),
                 out_shape=[jax.ShapeDtypeStruct((lp, 3 * GDN_W), F32), jax.ShapeDtypeStruct((lp, LANE), F32),
                            row128, row128, jax.ShapeDtypeStruct((4, 3 * GDN_W), F32)],
                 in_specs=[_row_spec(GDN_W, c, tm) for c in range(3)] + [_halo_prev(GDN_W, c, tm) for c in range(3)]
                 + wspec + [_row_spec(LANE, 0, tm), _full_spec(alog_row), _full_spec(dtb_row)]
                 + [_row_spec(GDN_W, 0, tm)] * 3 + [_row_spec(LANE, 0, tm)],
                 out_specs=[_row_spec(3 * GDN_W, 0, tm), _row_spec(LANE, 0, tm),
                            _full_spec(alog_row), _full_spec(dtb_row), pl.BlockSpec((4, 3 * GDN_W), lambda i: (0, 0))],
                 sem=("arbitrary",))(proj, proj, proj, proj, proj, proj, conv_w, conv_w, conv_w, small,
                                     alog_row, dtb_row, dq, dk, dv, dbg)


def _ffn_act(up_pre, conv_w, tm=ROW_TILE):
    lp = up_pre.shape[0]

    def body(xg, xv, hg, hv, wg, wv, a_ref):
        first = (pl.program_id(0) > 0).astype(F32)
        ug = _causal_conv(xg[...], hg[...] * first, wg[...])
        uv = _causal_conv(xv[...], hv[...] * first, wv[...])
        a_ref[...] = (_silu(ug) * uv).astype(a_ref.dtype)

    wspec = [pl.BlockSpec((3, D_FF), lambda i, c=c: (0, c)) for c in range(2)]
    return _call(body, name="ffn_act", grid=(lp // tm,), out_shape=jax.ShapeDtypeStruct((lp, D_FF), BF16),
                 in_specs=[_row_spec(D_FF, c, tm) for c in range(2)] + [_halo_prev(D_FF, c, tm) for c in range(2)] + wspec,
                 out_specs=_row_spec(D_FF, 0, tm), sem=("parallel",))(up_pre, up_pre, up_pre, up_pre, conv_w, conv_w)


def _ffn_act_bwd(up_pre, conv_w, dact, tm=ROW_TILE):
    lp = up_pre.shape[0]

    def body(xg, xv, hg, hv, wg, wv, da, du_ref, dw_ref):
        i = pl.program_id(0)
        first = (i > 0).astype(F32)
        xs = [(x[...], h[...] * first, w[...]) for x, h, w in ((xg, hg, wg), (xv, hv, wv))]
        ug, uv = [_causal_conv(*t) for t in xs]
        _, vjp = jax.vjp(lambda a, b: _silu(a) * b, ug, uv)
        dus = vjp(da[...].astype(F32))

        @pl.when(i == 0)
        def _():
            dw_ref[...] = jnp.zeros_like(dw_ref)

        for c, (du, (x, h, w)) in enumerate(zip(dus, xs)):
            du_ref[:, c * D_FF:(c + 1) * D_FF] = du
            rows = [jnp.sum(_shift_down(x, h, 2 - kk) * du, axis=0, keepdims=True) for kk in range(3)]
            dw_ref[:, c * D_FF:(c + 1) * D_FF] += jnp.concatenate(rows, axis=0)

    wspec = [pl.BlockSpec((3, D_FF), lambda i, c=c: (0, c)) for c in range(2)]
    return _call(body, name="ffn_act_bwd", grid=(lp // tm,),
                 out_shape=[jax.ShapeDtypeStruct((lp, 2 * D_FF), F32), jax.ShapeDtypeStruct((3, 2 * D_FF), F32)],
                 in_specs=[_row_spec(D_FF, c, tm) for c in range(2)] + [_halo_prev(D_FF, c, tm) for c in range(2)]
                 + wspec + [_row_spec(D_FF, 0, tm)],
                 out_specs=[_row_spec(2 * D_FF, 0, tm), pl.BlockSpec((3, 2 * D_FF), lambda i: (0, 0))],
                 sem=("arbitrary",))(up_pre, up_pre, up_pre, up_pre, conv_w, conv_w, dact)


def _conv_bwd_x(dy, w, pad_rows, width, name, tm=ROW_TILE):
    lp, ctot = dy.shape
    nt = lp // tm
    kk = w.shape[0]

    def body(d_ref, h_ref, w_ref, o_ref):
        i = pl.program_id(0)
        last = (i < nt - 1).astype(F32)
        d, h, wv = d_ref[...], h_ref[...] * last, w_ref[...]
        y = d * wv[kk - 1:kk, :]
        for k in range(kk - 1):
            y = y + _shift_up(d, h, kk - 1 - k) * wv[k:k + 1, :]
        valid = (i * tm + _iota((tm, 1), 0)) >= pad_rows
        o_ref[...] = jnp.where(valid, y, 0.0).astype(o_ref.dtype)

    return _call(body, name=name, grid=(nt, ctot // width), out_shape=jax.ShapeDtypeStruct((lp, ctot), BF16),
                 in_specs=[pl.BlockSpec((tm, width), lambda i, c: (i, c)),
                           pl.BlockSpec((8, width), lambda i, c: (jnp.minimum((i + 1) * (tm // 8), lp // 8 - 1), c)),
                           pl.BlockSpec((kk, width), lambda i, c: (0, c))],
                 out_specs=pl.BlockSpec((tm, width), lambda i, c: (i, c)), sem=("parallel", "parallel"))(dy, dy, w)


def _loss_head(h_res, target, row_start, tm=ROW_TILE):
    lp, d = h_res.shape
    t0 = row_start // tm

    def body(h_ref, t_ref, dy_ref, loss_ref):
        i = pl.program_id(0)

        @pl.when(i == 0)
        def _():
            loss_ref[...] = jnp.zeros_like(loss_ref)

        live = (i >= t0).astype(F32)
        err = (h_ref[...] - t_ref[...]) * live
        dy_ref[...] = err * (1.0 / d)
        loss_ref[...] += 0.5 / d * jnp.sum(err * err)

    return _call(body, name="loss_head", grid=(lp // tm,),
                 out_shape=[jax.ShapeDtypeStruct((lp, d), F32), jax.ShapeDtypeStruct((8, LANE), F32)],
                 in_specs=[pl.BlockSpec((tm, d), lambda i: (i, 0)),
                           pl.BlockSpec((tm, d), lambda i: (jnp.maximum(i - t0, 0), 0))],
                 out_specs=[pl.BlockSpec((tm, d), lambda i: (i, 0)), pl.BlockSpec((8, LANE), lambda i: (0, 0))],
                 sem=("arbitrary",))(h_res, target)


def _sum_adamw(parts, w, m, v, name):
    a, r, c = w.shape
    tm = _pick(r, (256, 128, 64, 32, 16))
    bc1 = 1.0 - ADAM_B1 ** ADAM_STEP
    bc2 = 1.0 - ADAM_B2 ** ADAM_STEP

    def body(p_ref, w_ref, m_ref, v_ref, g_ref, d_ref, nm_ref, nv_ref):
        g = p_ref[0, 0].astype(F32)
        for s in range(1, N_DEV):
            g = g + p_ref[s, 0].astype(F32)
        nm = ADAM_B1 * m_ref[0] + (1.0 - ADAM_B1) * g
        nv = ADAM_B2 * v_ref[0] + (1.0 - ADAM_B2) * (g * g)
        g_ref[0] = g
        nm_ref[0] = nm
        nv_ref[0] = nv
        d_ref[0] = -ADAM_LR * ((nm / bc1) / (jnp.sqrt(nv / bc2) + ADAM_EPS) + ADAM_WD * w_ref[0])

    spec = pl.BlockSpec((1, tm, c), lambda l, i: (l, i, 0))
    shp = jax.ShapeDtypeStruct((a, r, c), F32)
    return _call(body, name=name, grid=(a, r // tm), out_shape=[shp] * 4,
                 in_specs=[pl.BlockSpec((N_DEV, 1, tm, c), lambda l, i: (0, l, i, 0)), spec, spec, spec],
                 out_specs=[spec] * 4, sem=("parallel", "parallel"))(parts, w, m, v)


_ANY = pl.BlockSpec(memory_space=pl.ANY)


def _all_gather(block, name):
    def body(x_ref, out_ref, send_sems, recv_sems, local_sem):
        x, y, c = lax.axis_index("x"), lax.axis_index("y"), lax.axis_index("c")
        me, sibling = (x, y, c), (x, y, 1 - c)
        chips = [(1 - x, y), (x, 1 - y), (1 - x, 1 - y)]

        def slot(px, py, pc):
            return out_ref.at[4 * px + 2 * py + pc]

        def copy(k, blk, to, src=None):
            return pltpu.make_async_remote_copy(
                src_ref=slot(*blk) if src is None else src, dst_ref=slot(*blk),
                send_sem=send_sems.at[k], recv_sem=recv_sems.at[k], device_id=to, device_id_type=MESH)

        mine = pltpu.make_async_copy(x_ref, slot(*me), local_sem)
        mine.start()
        first = [copy(0, me, sibling, src=x_ref)]
        first += [copy(1 + j, me, (*chip, c), src=x_ref) for j, chip in enumerate(chips)]
        for cp in first:
            cp.start()
        passed = [copy(4 + j, (*chip, c), sibling) for j, chip in enumerate(chips)]
        for j, chip in enumerate(chips):
            copy(1 + j, (*chip, c), me).wait_recv()
            passed[j].start()
        copy(0, sibling, me).wait_recv()
        for j, chip in enumerate(chips):
            copy(4 + j, (*chip, 1 - c), me).wait_recv()
        for cp in first + passed:
            cp.wait_send()
        mine.wait()

    return pl.pallas_call(
        body, name=name, out_shape=jax.ShapeDtypeStruct((N_DEV,) + block.shape, block.dtype),
        in_specs=[_ANY], out_specs=_ANY,
        scratch_shapes=[pltpu.SemaphoreType.DMA((7,)), pltpu.SemaphoreType.DMA((7,)), pltpu.SemaphoreType.DMA],
    )(block)


def _all_to_all(src, name):
    def body(s_ref, o_ref, send_sems, recv_sems, local_sem):
        x, y, c = lax.axis_index("x"), lax.axis_index("y"), lax.axis_index("c")
        me = 4 * x + 2 * y + c
        mine = pltpu.make_async_copy(s_ref.at[me], o_ref.at[me], local_sem)
        mine.start()
        copies = []
        for k in range(1, N_DEV):
            px = 1 - x if k & 4 else x
            py = 1 - y if k & 2 else y
            pc = 1 - c if k & 1 else c
            peer = 4 * px + 2 * py + pc
            copies.append((pltpu.make_async_remote_copy(
                src_ref=s_ref.at[peer], dst_ref=o_ref.at[me], send_sem=send_sems.at[k - 1],
                recv_sem=recv_sems.at[k - 1], device_id=(px, py, pc), device_id_type=MESH), peer, k))
        for cp, _, _ in copies:
            cp.start()
        for cp, peer, k in copies:
            cp.wait_send()
            pltpu.make_async_remote_copy(
                src_ref=s_ref.at[peer], dst_ref=o_ref.at[peer], send_sem=send_sems.at[k - 1],
                recv_sem=recv_sems.at[k - 1], device_id=(x, y, c), device_id_type=MESH).wait_recv()
        mine.wait()

    return pl.pallas_call(
        body, name=name, out_shape=jax.ShapeDtypeStruct(src.shape, src.dtype), in_specs=[_ANY], out_specs=_ANY,
        scratch_shapes=[pltpu.SemaphoreType.DMA((7,)), pltpu.SemaphoreType.DMA((7,)), pltpu.SemaphoreType.DMA],
    )(src)


def _pack(blocks, width, dtype, row_mult):
    flat = jnp.concatenate([b.astype(dtype).reshape(-1) for b in blocks])
    per = width * row_mult
    total = -(-flat.shape[0] // per) * per
    return jnp.pad(flat, (0, total - flat.shape[0])).reshape(total // width, width)


def _pack_dest(fulls, axes, width, dtype, row_mult):
    rows = []
    for f, ax in zip(fulls, axes):
        f = f.astype(dtype)
        if ax is None:
            rows.append(jnp.broadcast_to(f.reshape(1, -1), (N_DEV, f.size)))
        else:
            shp = f.shape
            f = f.reshape(shp[:ax] + (N_DEV, shp[ax] // N_DEV) + shp[ax + 1:])
            rows.append(jnp.moveaxis(f, ax, 0).reshape(N_DEV, -1))
    flat = jnp.concatenate(rows, axis=1)
    per = width * row_mult
    total = -(-flat.shape[1] // per) * per
    return jnp.pad(flat, ((0, 0), (0, total - flat.shape[1]))).reshape(N_DEV, total // width, width)


def _unpack(packed, shapes):
    flat = packed.reshape(-1)
    out, off = [], 0
    for s in shapes:
        n = 1
        for d in s:
            n *= d
        out.append(flat[off:off + n].reshape(s))
        off += n
    return out


def _unpack_gathered(gathered, shapes, axes):
    flat = gathered.reshape(N_DEV, -1)
    out, off = [], 0
    for s, ax in zip(shapes, axes):
        n = 1
        for d in s:
            n *= d
        blk = jnp.moveaxis(flat[:, off:off + n].reshape((N_DEV,) + tuple(s)), 0, ax)
        out.append(blk.reshape(tuple(s[:ax]) + (N_DEV * s[ax],) + tuple(s[ax + 1:])))
        off += n
    return out


def _shard_cols(blocks, a, b):
    shard = blocks[0].shape[1]
    out = []
    while a < b:
        d = a // shard
        hi = min(b, (d + 1) * shard)
        out.append(blocks[d][:, a - d * shard:hi - d * shard])
        a = hi
    return out


def _permute_w_in(blocks):
    main = jnp.concatenate(_shard_cols(blocks, O_GQ, O_BL) + _shard_cols(blocks, O_GZ, O_END)
                           + _shard_cols(blocks, O_FQ, O_FL), axis=1)
    pad = jnp.zeros((blocks[0].shape[0], LANE - 24), blocks[0].dtype)
    small = jnp.concatenate(_shard_cols(blocks, O_FL, O_GQ) + _shard_cols(blocks, O_BL, O_GZ) + [pad], axis=1)
    return main, small


_W_IN_SEGS = ((O_FQ, O_FL, True, C_FQ), (O_FL, O_GQ, False, 0), (O_GQ, O_BL, True, C_GQ), (O_BL, O_GZ, False, 8),
              (O_GZ, O_END, True, C_GZ))


def _unpermute_cols(main, small, a, b):
    out = []
    for s0, s1, is_main, t0 in _W_IN_SEGS:
        lo, hi = max(a, s0), min(b, s1)
        if lo < hi:
            out.append((main if is_main else small)[:, t0 + lo - s0:t0 + hi - s0])
    return jnp.concatenate(out, axis=1)


def _lanes(vec, start):
    return jnp.pad(vec.astype(F32), (start, LANE - start - vec.shape[0])).reshape(1, LANE)


BIG = ("w_in", "w_branch_a", "w_branch_b", "w_out", "w_up", "w_down")
BIG_AXES = (2, 2, 1, 1, 2, 1)
SHARDED_SMALL = ("meta_tokens", "gdn_conv_w", "ffn_conv_w")
SHARDED_SMALL_AXES = (1, 2, 2)
REPL = ("norm1_g", "fox_f_bias", "fox_q_norm_g", "fox_k_norm_g", "gdn_a_log", "gdn_dt_bias", "gdn_norm_g", "norm2_g")
ORDER = ("meta_tokens", "norm1_g", "w_in", "fox_f_bias", "fox_q_norm_g", "fox_k_norm_g", "gdn_conv_w", "gdn_a_log",
         "gdn_dt_bias", "gdn_norm_g", "w_branch_a", "w_branch_b", "w_out", "norm2_g", "w_up", "ffn_conv_w", "w_down")


def _layer_fwd(h_res, wl, pad_rows):
    lp = h_res.shape[0]
    sv = {"res_in": h_res}
    (h1,) = _rowwise(_rmsnorm_fn, [(h_res, D, 0)], [wl["norm1_g"]], [(D, BF16)], "rmsnorm1")
    proj = _matmul(h1, wl["w_main"], "nn", F32, name="mm_in")
    small = _matmul(h1, wl["w_small"], "nn", F32, name="mm_in_small")
    sv.update(h1=h1, proj=proj, small=small)

    fox_fn = functools.partial(_fox_prep_fn, pad_rows)
    qh, kh, logf = _rowwise(fox_fn, [(proj, FOX_W, C_FQ // FOX_W), (proj, FOX_W, C_FK // FOX_W), (small, LANE, 0)],
                            [wl["qg"], wl["kg"], wl["fb"]], [(FOX_W, BF16), (FOX_W, BF16), (LANE, F32)], "fox_prep")
    fsum = _cumsum_rows(logf, False, "fox_cumsum")
    frow = fsum[:, :8].T.reshape(8, 1, lp)
    o_a, lse = _fox_fwd(qh, kh, proj, fsum, frow, pad_rows, C_FV // LANE)
    y_a = _matmul(o_a, wl["w_branch_a"], "nn", F32, name="mm_branch_a")
    sv.update(qh=qh, kh=kh, fsum=fsum, frow=frow, o_a=o_a, lse=lse)

    gq, gk, gv, bg = _gdn_act(proj, small, wl["gdn_conv_w"], wl["alog"], wl["dtb"])
    grow = bg[:, 16:24].T.reshape(8, lp // CHUNK, 1, CHUNK)
    o_raw, sprev = _gdn_fwd(gq, gk, gv, bg, grow)
    (o_b,) = _rowwise(_gdn_post_fn, [(o_raw, GDN_W, 0), (proj, GDN_W, C_GZ // GDN_W)], [wl["gn"]], [(GDN_W, BF16)],
                      "gdn_post")
    y_b = _matmul(o_b, wl["w_branch_b"], "nn", F32, name="mm_branch_b")
    sv.update(gq=gq, gk=gk, gv=gv, bg=bg, grow=grow, o_raw=o_raw, sprev=sprev, o_b=o_b)

    (mixed,) = _rowwise(_merge_fn, [(proj, D, C_GATE // D), (proj, D, C_GATE // D + 1), (y_a, D, 0), (y_b, D, 0)], [],
                        [(D, BF16)], "merge")
    res_mid = _matmul(mixed, wl["w_out"], "nn", F32, add=h_res, name="mm_out")
    sv.update(y_a=y_a, y_b=y_b, mixed=mixed, res_mid=res_mid)

    (h2,) = _rowwise(_rmsnorm_fn, [(res_mid, D, 0)], [wl["norm2_g"]], [(D, BF16)], "rmsnorm2")
    up_pre = _matmul(h2, wl["w_up"], "nn", F32, name="mm_up")
    act = _ffn_act(up_pre, wl["ffn_conv_w"])
    out = _matmul(act, wl["w_down"], "nn", F32, add=res_mid, name="mm_down")
    sv.update(h2=h2, up_pre=up_pre, act=act)
    return out, sv


def _layer_bwd(dres, wl, sv, pad_rows):
    lp = dres.shape[0]
    gw = {}
    gw["w_down"] = _matmul(sv["act"], dres, "tn", F32, name="mm_dw_down")
    dact = _matmul(dres, wl["w_down"], "nt", BF16, name="mm_dact")
    dup, gw["ffn_conv_w"] = _ffn_act_bwd(sv["up_pre"], wl["ffn_conv_w"], dact)
    dup_pre = _conv_bwd_x(dup, wl["ffn_conv_w"], pad_rows, D_FF, "ffn_conv_bwd")
    gw["w_up"] = _matmul(sv["h2"], dup_pre, "tn", F32, name="mm_dw_up")
    dh2 = _matmul(dup_pre, wl["w_up"], "nt", F32, name="mm_dh2")
    (dmid,), (gw["norm2_g"],) = _rowwise_bwd(_rmsnorm_fn, [(sv["res_mid"], D, 0)], [wl["norm2_g"]], [(dh2, D, 0)],
                                             "rmsnorm2_bwd", pad_rows, [F32], adds=[(dres, D, 0)])
    gw["w_out"] = _matmul(sv["mixed"], dmid, "tn", F32, name="mm_dw_out")
    dmixed = _matmul(dmid, wl["w_out"], "nt", F32, name="mm_dmixed")
    proj, small = sv["proj"], sv["small"]
    (dg0, dg1, dya, dyb), _ = _rowwise_bwd(
        _merge_fn, [(proj, D, C_GATE // D), (proj, D, C_GATE // D + 1), (sv["y_a"], D, 0), (sv["y_b"], D, 0)], [],
        [(dmixed, D, 0)], "merge_bwd", pad_rows, [BF16, BF16, BF16, BF16])
    gw["w_branch_a"] = _matmul(sv["o_a"], dya, "tn", F32, name="mm_dw_a")
    do_a = _matmul(dya, wl["w_branch_a"], "nt", BF16, name="mm_do_a")
    gw["w_branch_b"] = _matmul(sv["o_b"], dyb, "tn", F32, name="mm_dw_b")
    do_b = _matmul(dyb, wl["w_branch_b"], "nt", F32, name="mm_do_b")

    (do_raw, dgz), (gw["gn"],) = _rowwise_bwd(_gdn_post_fn, [(sv["o_raw"], GDN_W, 0), (proj, GDN_W, C_GZ // GDN_W)],
                                              [wl["gn"]], [(do_b, GDN_W, 0)], "gdn_post_bwd", pad_rows, [F32, BF16])
    dgq, dgk, dgv, dbg, dgrow = _gdn_bwd(sv["gq"], sv["gk"], sv["gv"], sv["bg"], sv["grow"], sv["sprev"], do_raw)
    dbg = dbg + jnp.pad(dgrow.reshape(8, lp).T, ((0, 0), (16, LANE - 24)))
    dconv, dsmall_g, gw["alog"], gw["dtb"], gw["gdn_conv_w"] = _gdn_act_bwd(
        proj, small, wl["gdn_conv_w"], wl["alog"], wl["dtb"], dgq, dgk, dgv, dbg)
    dqkv = _conv_bwd_x(dconv, wl["gdn_conv_w"], pad_rows, GDN_W, "gdn_conv_bwd")

    (delta,) = _rowwise(_fox_delta_fn, [(sv["o_a"], FOX_W, 0), (do_a, FOX_W, 0)], [], [(LANE, F32)], "fox_delta")
    dqh, dkh, dvh, dfq, dfk = _fox_bwd(sv["qh"], sv["kh"], proj, sv["fsum"], sv["frow"], do_a, sv["lse"], delta,
                                       pad_rows, C_FV // LANE)
    df8 = dfq.reshape(lp, FOX_W // LANE, LANE)[:, :, :2].reshape(lp, 8) + dfk.reshape(8, lp).T
    dlogf = _cumsum_rows(jnp.pad(df8, ((0, 0), (0, LANE - 8))), True, "fox_cumsum_bwd")
    fox_fn = functools.partial(_fox_prep_fn, pad_rows)
    (dfq_p, dfk_p, dsmall_f), (gw["qg"], gw["kg"], gw["fb"]) = _rowwise_bwd(
        fox_fn, [(proj, FOX_W, C_FQ // FOX_W), (proj, FOX_W, C_FK // FOX_W), (small, LANE, 0)],
        [wl["qg"], wl["kg"], wl["fb"]], [(dqh, FOX_W, 0), (dkh, FOX_W, 0), (dlogf, LANE, 0)],
        "fox_prep_bwd", pad_rows, [BF16, BF16, F32], adds=[None, None, (dsmall_g, LANE, 0)])

    dproj = jnp.concatenate([dqkv, dgz, dg0, dg1, dfq_p, dfk_p, dvh], axis=1)
    gw["w_main"] = _matmul(sv["h1"], dproj, "tn", F32, name="mm_dw_main")
    gw["w_small"] = _matmul(sv["h1"], dsmall_f, "tn", F32, name="mm_dw_small")
    dh1 = _matmul(dproj, wl["w_main"], "nt", F32, name="mm_dh1")
    dh1 = _matmul(dsmall_f, wl["w_small"], "nt", F32, add=dh1, name="mm_dh1_small")
    (din,), (gw["norm1_g"],) = _rowwise_bwd(_rmsnorm_fn, [(sv["res_in"], D, 0)], [wl["norm1_g"]], [(dh1, D, 0)],
                                            "rmsnorm1_bwd", pad_rows, [F32], adds=[(dmid, D, 0)])
    return din, gw


def kernel(x, meta_tokens, norm1_g, w_in, fox_f_bias, fox_q_norm_g, fox_k_norm_g, gdn_conv_w, gdn_a_log, gdn_dt_bias, gdn_norm_g, w_branch_a, w_branch_b, w_out, norm2_g, w_up, ffn_conv_w, w_down, loss_target, m_meta_tokens, m_norm1_g, m_w_in, m_fox_f_bias, m_fox_q_norm_g, m_fox_k_norm_g, m_gdn_conv_w, m_gdn_a_log, m_gdn_dt_bias, m_gdn_norm_g, m_w_branch_a, m_w_branch_b, m_w_out, m_norm2_g, m_w_up, m_ffn_conv_w, m_w_down, v_meta_tokens, v_norm1_g, v_w_in, v_fox_f_bias, v_fox_q_norm_g, v_fox_k_norm_g, v_gdn_conv_w, v_gdn_a_log, v_gdn_dt_bias, v_gdn_norm_g, v_w_branch_a, v_w_branch_b, v_w_out, v_norm2_g, v_w_up, v_ffn_conv_w, v_w_down):
    w = dict(meta_tokens=meta_tokens, norm1_g=norm1_g, w_in=w_in, fox_f_bias=fox_f_bias, fox_q_norm_g=fox_q_norm_g,
             fox_k_norm_g=fox_k_norm_g, gdn_conv_w=gdn_conv_w, gdn_a_log=gdn_a_log, gdn_dt_bias=gdn_dt_bias,
             gdn_norm_g=gdn_norm_g, w_branch_a=w_branch_a, w_branch_b=w_branch_b, w_out=w_out, norm2_g=norm2_g,
             w_up=w_up, ffn_conv_w=ffn_conv_w, w_down=w_down)
    mom = dict(meta_tokens=m_meta_tokens, norm1_g=m_norm1_g, w_in=m_w_in, fox_f_bias=m_fox_f_bias,
               fox_q_norm_g=m_fox_q_norm_g, fox_k_norm_g=m_fox_k_norm_g, gdn_conv_w=m_gdn_conv_w,
               gdn_a_log=m_gdn_a_log, gdn_dt_bias=m_gdn_dt_bias, gdn_norm_g=m_gdn_norm_g, w_branch_a=m_w_branch_a,
               w_branch_b=m_w_branch_b, w_out=m_w_out, norm2_g=m_norm2_g, w_up=m_w_up, ffn_conv_w=m_ffn_conv_w,
               w_down=m_w_down)
    var = dict(meta_tokens=v_meta_tokens, norm1_g=v_norm1_g, w_in=v_w_in, fox_f_bias=v_fox_f_bias,
               fox_q_norm_g=v_fox_q_norm_g, fox_k_norm_g=v_fox_k_norm_g, gdn_conv_w=v_gdn_conv_w,
               gdn_a_log=v_gdn_a_log, gdn_dt_bias=v_gdn_dt_bias, gdn_norm_g=v_gdn_norm_g, w_branch_a=v_w_branch_a,
               w_branch_b=v_w_branch_b, w_out=v_w_out, norm2_g=v_norm2_g, w_up=v_w_up, ffn_conv_w=v_ffn_conv_w,
               w_down=v_w_down)
    depth = norm1_g.shape[0]
    seq = x.shape[1]
    l_tok = N_META + seq
    lp = -(-l_tok // LANE) * LANE
    pad_rows = lp - l_tok
    row_start = pad_rows + N_META

    got = {n: _all_gather(w[n].astype(BF16), "gather_" + n) for n in BIG}
    small_shapes = [w[n].shape for n in SHARDED_SMALL]
    gathered_s = _all_gather(_pack([w[n] for n in SHARDED_SMALL], LANE, F32, 8), "gather_small")
    full = dict(zip(SHARDED_SMALL, _unpack_gathered(gathered_s, small_shapes, SHARDED_SMALL_AXES)))

    def join(name, l):
        return jnp.concatenate([got[name][d, l] for d in range(N_DEV)], axis=BIG_AXES[BIG.index(name)] - 1)

    layers = []
    for l in range(depth):
        w_main, w_small = _permute_w_in([got["w_in"][d, l] for d in range(N_DEV)])
        layers.append(dict(
            w_main=w_main, w_small=w_small, w_branch_a=join("w_branch_a", l), w_branch_b=join("w_branch_b", l),
            w_out=join("w_out", l), w_up=join("w_up", l), w_down=join("w_down", l),
            gdn_conv_w=full["gdn_conv_w"][l], ffn_conv_w=full["ffn_conv_w"][l],
            norm1_g=norm1_g[l].reshape(1, D), norm2_g=norm2_g[l].reshape(1, D),
            qg=jnp.tile(fox_q_norm_g[l], 8).reshape(1, FOX_W), kg=jnp.tile(fox_k_norm_g[l], 8).reshape(1, FOX_W),
            fb=_lanes(fox_f_bias[l], 0), alog=_lanes(gdn_a_log[l], 16), dtb=_lanes(gdn_dt_bias[l], 16),
            gn=jnp.tile(gdn_norm_g[l], 8).reshape(1, GDN_W)))

    h_res = jnp.concatenate([jnp.zeros((pad_rows, D), F32), full["meta_tokens"], x[0]], axis=0)
    saved = []
    for l in range(depth):
        h_res, sv = _layer_fwd(h_res, layers[l], pad_rows)
        saved.append(sv)
    dres, loss_part = _loss_head(h_res, loss_target[0], row_start)
    loss = lax.psum(loss_part[0, 0], ("x", "y", "c"))

    gws = [None] * depth
    for l in reversed(range(depth)):
        dres, gws[l] = _layer_bwd(dres, layers[l], saved[l], pad_rows)
    grad_x = dres[row_start:].reshape(x.shape)

    def stack(fn):
        return jnp.stack([fn(g) for g in gws])

    part = dict(
        meta_tokens=dres[pad_rows:row_start],
        norm1_g=stack(lambda g: g["norm1_g"][0]), norm2_g=stack(lambda g: g["norm2_g"][0]),
        fox_f_bias=stack(lambda g: g["fb"][0, 0:8]),
        fox_q_norm_g=stack(lambda g: g["qg"].reshape(8, FOX_DH).sum(0)),
        fox_k_norm_g=stack(lambda g: g["kg"].reshape(8, FOX_DH).sum(0)),
        gdn_conv_w=stack(lambda g: g["gdn_conv_w"]), gdn_a_log=stack(lambda g: g["alog"][0, 16:24]),
        gdn_dt_bias=stack(lambda g: g["dtb"][0, 16:24]),
        gdn_norm_g=stack(lambda g: g["gn"].reshape(8, GDN_DH).sum(0)),
        ffn_conv_w=stack(lambda g: g["ffn_conv_w"]))

    def dest_blocks(name, d):
        if name == "w_in":
            s = w_in.shape[2]
            blks = [_unpermute_cols(g["w_main"], g["w_small"], s * d, s * (d + 1)) for g in gws]
        elif BIG_AXES[BIG.index(name)] == 2:
            s = w[name].shape[2]
            blks = [g[name][:, s * d:s * (d + 1)] for g in gws]
        else:
            s = w[name].shape[1]
            blks = [g[name][s * d:s * (d + 1), :] for g in gws]
        return jnp.stack([b.astype(BF16) for b in blks])

    res = {}
    for n in BIG:
        landed = _all_to_all(jnp.stack([dest_blocks(n, d) for d in range(N_DEV)]), "scatter_" + n)
        res[n] = _sum_adamw(landed, w[n], mom[n], var[n], "adamw_" + n)

    small_names = SHARDED_SMALL + REPL
    small_axes = SHARDED_SMALL_AXES + (None,) * len(REPL)
    landed_s = _all_to_all(_pack_dest([part[n] for n in small_names], small_axes, LANE, F32, 8), "scatter_small")
    shapes_s = [w[n].shape for n in small_names]
    outs = _sum_adamw(landed_s[:, None], *[_pack([d[n] for n in small_names], LANE, F32, 8)[None] for d in (w, mom, var)],
                      "adamw_small")
    for o_idx, packed in enumerate(outs):
        for n, a in zip(small_names, _unpack(packed[0], shapes_s)):
            res.setdefault(n, [None] * 4)[o_idx] = a

    return (loss, grad_x, *[res[n][0] for n in ORDER], *[res[n][1] for n in ORDER],
            *[res[n][2] for n in ORDER], *[res[n][3] for n in ORDER])
```

```python
import functools

import jax
import jax.numpy as jnp
from jax import lax
from jax.experimental import pallas as pl
from jax.experimental.pallas import tpu as pltpu

F32, BF16 = jnp.float32, jnp.bfloat16
MESH = pl.DeviceIdType.MESH

D = 1024
N_META = 16
DEPTH = 4
EPS = 1e-6
NEG = -1e30
FOX_W, FOX_DH = 512, 64
GDN_W, GDN_DH, GDN_H = 1024, 128, 8
CHUNK = 64
D_FF = 2816
N_DEV = 8
ADAM_LR, ADAM_B1, ADAM_B2, ADAM_EPS, ADAM_WD, ADAM_STEP = 0.001, 0.9, 0.999, 1e-08, 0.01, 10

VMEM_LIMIT_BYTES = 48 * 1024 * 1024
MATMUL_VMEM_BUDGET = 36 * 1024 * 1024
ROW_TILE = 128
ROW_TILES_WIDE = (640, 512, 256, 128)
ROW_TILES = (320, 256, 128)
LANE = 128

C_GQ, C_GK, C_GV, C_GZ, C_GATE, C_FQ, C_FK, C_FV = 0, 1024, 2048, 3072, 4096, 6144, 6656, 7168
W_MAIN = 7680
O_FQ, O_FK, O_FV, O_FL, O_GQ, O_GK, O_GV, O_BL, O_AL, O_GZ, O_GATE, O_END = (
    0, 512, 1024, 1536, 1544, 2568, 3592, 4616, 4624, 4632, 5656, 7704)


def _pick(n, cands):
    for c in cands:
        if n % c == 0:
            return c
    return n


def _call(body, *, name, out_shape, in_specs, out_specs, grid=(), scratch=(), sem=None):
    kw = dict(vmem_limit_bytes=VMEM_LIMIT_BYTES)
    if sem is not None:
        kw["dimension_semantics"] = sem
    return pl.pallas_call(body, name=name, out_shape=out_shape, grid=grid, in_specs=in_specs,
                          out_specs=out_specs, scratch_shapes=list(scratch),
                          compiler_params=pltpu.CompilerParams(**kw))


_DIMS = {"nn": (((1,), (0,)), ((), ())), "nt": (((1,), (1,)), ((), ())), "tn": (((0,), (0,)), ((), ()))}


_DIMS_BATCHED = {"nn": (((2,), (1,)), ((0,), (0,))), "nt": (((2,), (2,)), ((0,), (0,))),
                 "tn": (((1,), (1,)), ((0,), (0,)))}


def _dot(a, b, mode, prec=None):
    dims = _DIMS[mode] if a.ndim == 2 else _DIMS_BATCHED[mode]
    return lax.dot_general(a, b, dims, precision=prec, preferred_element_type=F32)


def _mm_grads(f, mode, a, b, g):
    if mode == "nn":
        return f(g, b, "nt"), f(a, g, "tn")
    if mode == "nt":
        return f(g, b, "nn"), f(g, a, "tn")
    return f(b, g, "nt"), f(a, g, "nn")


@functools.partial(jax.custom_vjp, nondiff_argnums=(2,))
def _mmb(a, b, mode):
    return _dot(a.astype(BF16), b.astype(BF16), mode)


def _mmb_fwd(a, b, mode):
    return _mmb(a, b, mode), (a, b)


def _mmb_bwd(mode, res, g):
    return _mm_grads(_mmb, mode, res[0], res[1], g)


_mmb.defvjp(_mmb_fwd, _mmb_bwd)


def _split(a):
    hi = a.astype(BF16)
    return hi, (a - hi.astype(F32)).astype(BF16)


@functools.partial(jax.custom_vjp, nondiff_argnums=(2,))
def _mmh(a, b, mode):
    ah, al = _split(a)
    bh, bl = _split(b)
    return _dot(ah, bh, mode) + (_dot(ah, bl, mode) + _dot(al, bh, mode))


def _mmh_fwd(a, b, mode):
    return _mmh(a, b, mode), (a, b)


def _mmh_bwd(mode, res, g):
    return _mm_grads(_mmh, mode, res[0], res[1], g)


_mmh.defvjp(_mmh_fwd, _mmh_bwd)


def _dot_sel(sel, x, mode):
    s = sel.astype(BF16)
    x1 = x.astype(BF16)
    x2, x3 = _split(x - x1.astype(F32))
    return _dot(s, x1, mode) + (_dot(s, x2, mode) + _dot(s, x3, mode))


@jax.custom_vjp
def _mms(sel, x):
    return _dot_sel(sel, x, "nn")


def _mms_fwd(sel, x):
    return _dot_sel(sel, x, "nn"), sel


def _mms_bwd(sel, g):
    return jnp.zeros_like(sel), _dot_sel(sel, g, "tn")


_mms.defvjp(_mms_fwd, _mms_bwd)


def _softplus(z):
    return jnp.maximum(z, 0.0) + jnp.log(1.0 + jnp.exp(-jnp.abs(z)))


def _log_sigmoid(z):
    return jnp.minimum(z, 0.0) - jnp.log(1.0 + jnp.exp(-jnp.abs(z)))


def _silu(z):
    return z * jax.nn.sigmoid(z)


def _iota(shape, dim):
    return lax.broadcasted_iota(jnp.int32, shape, dim)


def _inv_unit_lower_raw(n):
    c = n.shape[-1]
    ri, ci = _iota((c, c), 0), _iota((c, c), 1)
    eye = (ri == ci).astype(F32)
    dmask = (ri // 16) == (ci // 16)
    dpart = jnp.where(dmask, n, 0.0)
    lpart = n - dpart
    x = -dpart
    p = eye + x
    for _ in range(3):
        x = _mmh(x, x, "nn")
        p = p + _mmh(p, x, "nn")
    m = -_mmh(p, lpart, "nn")
    q = eye + m
    steps = 1
    while (1 << steps) < c // 16:
        steps += 1
    for _ in range(steps - 1):
        m = _mmh(m, m, "nn")
        q = q + _mmh(q, m, "nn")
    return _mmh(q, p, "nn")


@jax.custom_vjp
def _inv_given(n, t):
    return t


def _inv_given_fwd(n, t):
    return t, t


def _inv_given_bwd(t, g):
    c = t.shape[-1]
    strict = _iota((c, c), 0) > _iota((c, c), 1)
    d = -_mmh(_mmh(t, g, "tn"), t, "nt")
    return jnp.where(strict, d, 0.0), jnp.zeros_like(t)


_inv_given.defvjp(_inv_given_fwd, _inv_given_bwd)


def _shift_down(x, halo, s):
    if s == 0:
        return x
    xs = pltpu.roll(x, s, 0)
    hs = pltpu.roll(halo, s, 0)
    top = jnp.where(_iota(hs.shape, 0) < s, hs, xs[0:8])
    return jnp.concatenate([top, xs[8:]], axis=0)


def _shift_up(x, halo, s):
    if s == 0:
        return x
    tm = x.shape[0]
    xs = pltpu.roll(x, tm - s, 0)
    hs = pltpu.roll(halo, 8 - s, 0)
    bot = jnp.where(_iota(hs.shape, 0) >= 8 - s, hs, xs[tm - 8:])
    return jnp.concatenate([xs[:tm - 8], bot], axis=0)


def _causal_conv(x, halo, w):
    kk = w.shape[0]
    y = x * w[kk - 1:kk, :]
    for k in range(kk - 1):
        y = y + _shift_down(x, halo, kk - 1 - k) * w[k:k + 1, :]
    return y


def _head_scale(x, width, fn):
    outs = []
    for h in range(x.shape[1] // width):
        seg = x[:, h * width:(h + 1) * width]
        outs.append(seg * fn(jnp.sum(seg * seg, axis=1, keepdims=True)))
    return jnp.concatenate(outs, axis=1)


def _matmul(a, b, mode, out_dtype, add=None, name="mm"):
    if mode == "nn":
        (m, k), n = a.shape, b.shape[1]
    elif mode == "nt":
        (m, k), n = a.shape, b.shape[0]
    else:
        (k, m), n = a.shape, b.shape[1]
    tm = _pick(m, (1408, 1024, 512, 256, 128) if mode == "tn" else (640, 512, 256, 128))
    tn = _pick(n, (1536, 1408, 1024, 768, 512, 256, 128))
    sa, sb = a.dtype.itemsize, b.dtype.itemsize
    fixed = tm * tn * 4 * (3 + (2 if add is not None else 0))
    tk = 128
    for cand in (k, 2816, 2560, 1664, 1536, 1280, 1024, 832, 768, 640, 512, 256, 128):
        if mode != "tn" and cand != k and cand % LANE:
            continue
        if k % cand == 0 and fixed + 2 * cand * (tm * sa + tn * sb) <= MATMUL_VMEM_BUDGET:
            tk = cand
            break
    nk = k // tk
    a_spec = {"nn": pl.BlockSpec((tm, tk), lambda i, j, kk: (i, kk)),
              "nt": pl.BlockSpec((tm, tk), lambda i, j, kk: (i, kk)),
              "tn": pl.BlockSpec((tk, tm), lambda i, j, kk: (kk, i))}[mode]
    b_spec = {"nn": pl.BlockSpec((tk, tn), lambda i, j, kk: (kk, j)),
              "nt": pl.BlockSpec((tn, tk), lambda i, j, kk: (j, kk)),
              "tn": pl.BlockSpec((tk, tn), lambda i, j, kk: (kk, j))}[mode]
    o_spec = pl.BlockSpec((tm, tn), lambda i, j, kk: (i, j))
    has_add = add is not None

    def body(*refs):
        a_ref, b_ref = refs[0], refs[1]
        add_ref = refs[2] if has_add else None
        o_ref = refs[3] if has_add else refs[2]
        part = _dot(a_ref[...].astype(BF16), b_ref[...].astype(BF16), mode)
        if nk == 1:
            if has_add:
                part = part + add_ref[...].astype(F32)
            o_ref[...] = part.astype(out_dtype)
        else:
            acc = refs[-1]
            kk = pl.program_id(2)

            @pl.when(kk == 0)
            def _():
                acc[...] = part

            @pl.when(kk > 0)
            def _():
                acc[...] += part

            @pl.when(kk == nk - 1)
            def _():
                r = acc[...]
                if has_add:
                    r = r + add_ref[...].astype(F32)
                o_ref[...] = r.astype(out_dtype)

    ins = [a, b] + ([add] if has_add else [])
    specs = [a_spec, b_spec] + ([o_spec] if has_add else [])
    return _call(body, name=name, out_shape=jax.ShapeDtypeStruct((m, n), out_dtype), grid=(m // tm, n // tn, nk),
                 in_specs=specs, out_specs=o_spec,
                 scratch=[pltpu.VMEM((tm, tn), F32)] if nk > 1 else [],
                 sem=("parallel", "parallel", "arbitrary"))(*ins)


def _row_spec(width, colblock, tm):
    return pl.BlockSpec((tm, width), lambda i, cb=colblock: (i, cb))


def _full_spec(arr):
    nd = arr.ndim
    return pl.BlockSpec(arr.shape, lambda i, nd=nd: (0,) * nd)


def _rowwise(fn, rows, params, outs, name):
    lp = rows[0][0].shape[0]
    tm = _pick(lp, ROW_TILES_WIDE)
    nr, npar = len(rows), len(params)

    def body(*refs):
        row0 = pl.program_id(0) * tm
        vals = [r[...].astype(F32) for r in refs[:nr + npar]]
        res = fn(*vals, row0)
        for o_ref, r in zip(refs[nr + npar:], res):
            o_ref[...] = r.astype(o_ref.dtype)

    out = _call(body, name=name, grid=(lp // tm,),
                out_shape=[jax.ShapeDtypeStruct((lp, w), dt) for w, dt in outs],
                in_specs=[_row_spec(w, cb, tm) for _, w, cb in rows] + [_full_spec(p) for p in params],
                out_specs=[_row_spec(w, 0, tm) for w, _ in outs], sem=("parallel",))(
                    *[r[0] for r in rows], *params)
    return out


def _rowwise_bwd(fn, rows, params, cts, name, pad_rows, grad_dtypes, adds=None):
    lp = rows[0][0].shape[0]
    tm = _pick(lp, ROW_TILES)
    nr, npar, nct = len(rows), len(params), len(cts)
    adds = adds or [None] * nr
    add_list = [a for a in adds if a is not None]
    nadd = len(add_list)

    def body(*refs):
        i = pl.program_id(0)
        row0 = i * tm
        vals = [r[...].astype(F32) for r in refs[:nr + npar]]
        ct_vals = tuple(r[...].astype(F32) for r in refs[nr + npar:nr + npar + nct])
        add_refs = list(refs[nr + npar + nct:nr + npar + nct + nadd])
        outs = refs[nr + npar + nct + nadd:]
        _, vjp = jax.vjp(lambda *args: tuple(fn(*args, row0)), *vals)
        grads = vjp(ct_vals)
        valid = (row0 + _iota((tm, 1), 0)) >= pad_rows
        for idx in range(nr):
            g = jnp.where(valid, grads[idx], 0.0)
            if adds[idx] is not None:
                g = g + add_refs.pop(0)[...].astype(F32)
            outs[idx][...] = g.astype(outs[idx].dtype)
        for idx in range(npar):
            o_ref = outs[nr + idx]

            @pl.when(i == 0)
            def _(o_ref=o_ref):
                o_ref[...] = jnp.zeros_like(o_ref)

            o_ref[...] += grads[nr + idx]

    out = _call(body, name=name, grid=(lp // tm,),
                out_shape=[jax.ShapeDtypeStruct((lp, w), dt) for (_, w, _), dt in zip(rows, grad_dtypes)]
                + [jax.ShapeDtypeStruct(p.shape, F32) for p in params],
                in_specs=[_row_spec(w, cb, tm) for _, w, cb in rows] + [_full_spec(p) for p in params]
                + [_row_spec(w, cb, tm) for _, w, cb in cts] + [_row_spec(w, cb, tm) for _, w, cb in add_list],
                out_specs=[_row_spec(w, 0, tm) for _, w, _ in rows] + [_full_spec(p) for p in params],
                sem=("arbitrary",))(*[r[0] for r in rows], *params, *[c[0] for c in cts], *[a[0] for a in add_list])
    return out[:nr], out[nr:]


def _rmsnorm_fn(x, g, row0):
    return (x * lax.rsqrt(jnp.mean(x * x, axis=1, keepdims=True) + EPS) * g,)


def _fox_prep_fn(pad_rows, fq, fk, small, qg, kg, fb, row0):
    ri, ci = _iota((FOX_W, FOX_W), 0), _iota((FOX_W, FOX_W), 1)
    bd = jnp.where((ri // FOX_DH) == (ci // FOX_DH), 1.0 / FOX_DH, 0.0)

    def hn(x, g):
        return x * lax.rsqrt(_mmh(x * x, bd, "nn") + EPS) * g

    tm = small.shape[0]
    keep = (_iota((tm, LANE), 1) < 8) & ((row0 + _iota((tm, LANE), 0)) >= pad_rows)
    logf = jnp.where(keep, _log_sigmoid(small + fb), 0.0)
    return hn(fq, qg) * (FOX_DH ** -0.5), hn(fk, kg), logf


def _gdn_act_fn(cq, ck, cv, small, alog, dtb):
    tm = small.shape[0]
    q = _head_scale(_silu(cq), GDN_DH, lambda s: lax.rsqrt(s + EPS) * (GDN_DH ** -0.5))
    k = _head_scale(_silu(ck), GDN_DH, lambda s: lax.rsqrt(s + EPS))
    v = _silu(cv)
    lane = _iota((tm, LANE), 1)
    beta = jnp.where((lane >= 8) & (lane < 16), jax.nn.sigmoid(small), 0.0)
    g = jnp.where((lane >= 16) & (lane < 24), -jnp.exp(alog) * _softplus(small + dtb), 0.0)
    ri, ci = _iota((tm, tm), 0), _iota((tm, tm), 1)
    tri = jnp.where(((ri // CHUNK) == (ci // CHUNK)) & (ci <= ri), 1.0, 0.0)
    return q, k, v, beta + _mms(tri, g)


def _gdn_post_fn(o, gz, gn, row0):
    return (_head_scale(o, GDN_DH, lambda s: lax.rsqrt(s * (1.0 / GDN_DH) + EPS)) * gn * _silu(gz),)


def _merge_fn(g0, g1, ya, yb, row0):
    return (jax.nn.sigmoid(g0) * ya + jax.nn.sigmoid(g1) * yb,)


def _cumsum_rows(x, reverse, name):
    lp, w = x.shape
    tm = _pick(lp, (640, 512, 256, 128))
    nt = lp // tm

    def body(x_ref, o_ref, carry):
        i = pl.program_id(0)

        @pl.when(i == 0)
        def _():
            carry[...] = jnp.zeros_like(carry)

        ri, ci = _iota((tm, tm), 0), _iota((tm, tm), 1)
        tri = jnp.where((ci >= ri) if reverse else (ci <= ri), 1.0, 0.0)
        blk = x_ref[...]
        o_ref[...] = _dot_sel(tri, blk, "nn") + carry[0:1, :]
        carry[...] = carry[...] + jnp.sum(blk, axis=0, keepdims=True)

    idx = (lambda i: (nt - 1 - i, 0)) if reverse else (lambda i: (i, 0))
    return _call(body, name=name, grid=(nt,), out_shape=jax.ShapeDtypeStruct((lp, w), F32),
                 in_specs=[pl.BlockSpec((tm, w), idx)], out_specs=pl.BlockSpec((tm, w), idx),
                 scratch=[pltpu.VMEM((8, w), F32)], sem=("arbitrary",))(x)


def _fox_scores(q, k, fq, fk, hh, qpos0, kpos0, pad_rows, masked):
    tq, tk = q.shape[0], k.shape[0]
    lane = _iota(q.shape, 1)
    sel = (lane < FOX_DH) if hh == 0 else (lane >= FOX_DH)
    s = _dot(jnp.where(sel, q, jnp.zeros_like(q)), k, "nt") + fq - fk
    if not masked:
        return s, None, sel
    qpos = qpos0 + _iota((tq, tk), 0)
    kpos = kpos0 + _iota((tq, tk), 1)
    mask = (kpos <= qpos) & (kpos >= pad_rows)
    return jnp.where(mask, s, NEG), mask, sel


def _probs(s, mask, shift):
    p = jnp.exp(s - shift)
    return p if mask is None else jnp.where(mask, p, 0.0)


def _both_variants(needs_mask, fn):
    @pl.when(needs_mask)
    def _():
        fn(True)

    @pl.when(jnp.logical_not(needs_mask))
    def _():
        fn(False)


def _lane_col(blk, lane_idx):
    return jnp.sum(jnp.where(_iota(blk.shape, 1) == lane_idx, blk, 0.0), axis=1, keepdims=True)


def _to_lanes(cols, width=LANE):
    lane = _iota((cols[0].shape[0], width), 1)
    out = jnp.zeros((cols[0].shape[0], width), F32)
    for idx, c in enumerate(cols):
        out = jnp.where(lane == idx, c, out)
    return out


def _fox_fwd(q, k, v, fsum, frow, pad_rows, v_col):
    lp = q.shape[0]
    t = _pick(lp, (640, 512, 256, 128))
    n = lp // t

    def body(q_ref, k_ref, v_ref, f_ref, fk_ref, o_ref, lse_ref, acc, m_s, l_s, fq_s):
        pr, i, j = pl.program_id(0), pl.program_id(1), pl.program_id(2)

        @pl.when(j == 0)
        def _():
            acc[...] = jnp.zeros_like(acc)
            m_s[...] = jnp.full_like(m_s, NEG)
            l_s[...] = jnp.zeros_like(l_s)
            for hh in range(2):
                fq_s[hh] = _lane_col(f_ref[...], 2 * pr + hh)

        def step(masked):
            for hh in range(2):
                s, mask, _ = _fox_scores(q_ref[...], k_ref[...], fq_s[hh], fk_ref[hh], hh, i * t, j * t, pad_rows,
                                         masked)
                m_prev = m_s[hh]
                m_new = jnp.maximum(m_prev, jnp.max(s, axis=1, keepdims=True))
                p = _probs(s, mask, m_new)
                alpha = jnp.exp(m_prev - m_new)
                l_s[hh] = alpha * l_s[hh] + jnp.sum(p, axis=1, keepdims=True)
                acc[hh] = alpha * acc[hh] + _dot(p.astype(BF16), v_ref[...].astype(BF16), "nn")
                m_s[hh] = m_new

        @pl.when(j <= i)
        def _():
            _both_variants((j == i) | (j == 0), step)

        @pl.when(j == i)
        def _():
            outs, lses = [], []
            for hh in range(2):
                l = l_s[hh]
                ok = l > 0.0
                outs.append(acc[hh] * jnp.where(ok, 1.0 / jnp.where(ok, l, 1.0), 0.0))
                lses.append(jnp.where(ok, m_s[hh] + jnp.log(jnp.where(ok, l, 1.0)), 0.0))
            lane = _iota((t, LANE), 1)
            o_ref[...] = jnp.where(lane < FOX_DH, outs[0], outs[1]).astype(o_ref.dtype)
            lse_ref[...] = _to_lanes(lses)

    qspec = pl.BlockSpec((t, LANE), lambda p, i, j: (i, p))
    kspec = pl.BlockSpec((t, LANE), lambda p, i, j: (jnp.minimum(j, i), p))
    vspec = pl.BlockSpec((t, LANE), lambda p, i, j: (jnp.minimum(j, i), v_col + p))
    fspec = pl.BlockSpec((t, LANE), lambda p, i, j: (i, 0))
    rspec = pl.BlockSpec((2, 1, t), lambda p, i, j: (p, 0, jnp.minimum(j, i)))
    return _call(body, name="fox_fwd", grid=(FOX_W // LANE, n, n),
                 out_shape=[jax.ShapeDtypeStruct((lp, FOX_W), BF16), jax.ShapeDtypeStruct((lp, FOX_W), F32)],
                 in_specs=[qspec, kspec, vspec, fspec, rspec], out_specs=[qspec, qspec],
                 scratch=[pltpu.VMEM((2, t, LANE), F32), pltpu.VMEM((2, t, 1), F32), pltpu.VMEM((2, t, 1), F32),
                          pltpu.VMEM((2, t, 1), F32)],
                 sem=("parallel", "parallel", "arbitrary"))(q, k, v, fsum, frow)


def _fox_delta_fn(o, do, row0):
    ri, ci = _iota((FOX_W, LANE), 0), _iota((FOX_W, LANE), 1)
    sel = jnp.where((ri // FOX_DH) == ci, 1.0, 0.0).astype(BF16)
    x = o * do
    x1 = x.astype(BF16)
    x2, x3 = _split(x - x1.astype(F32))
    return (_dot(x1, sel, "nn") + (_dot(x2, sel, "nn") + _dot(x3, sel, "nn")),)


def _fox_bwd(q, k, v, fsum, frow, do, lse, delta, pad_rows, v_col):
    lp = q.shape[0]
    t = _pick(lp, (640, 512, 256, 128))
    n = lp // t

    def body(q_ref, k_ref, v_ref, f_ref, fk_ref, do_ref, lse_ref, dl_ref,
             dq_ref, dk_ref, dv_ref, dfq_ref, dfk_ref, dka, dva, dfa):
        pr, j, i = pl.program_id(0), pl.program_id(1), pl.program_id(2)
        lane = _iota((t, LANE), 1)

        @pl.when((j == 0) & (i == 0))
        def _():
            dq_ref[...] = jnp.zeros_like(dq_ref)
            dfq_ref[...] = jnp.zeros_like(dfq_ref)

        @pl.when(i == 0)
        def _():
            dka[...] = jnp.zeros_like(dka)
            dva[...] = jnp.zeros_like(dva)
            dfa[...] = jnp.zeros_like(dfa)

        def step(masked):
            rows = pl.ds(pl.multiple_of(i * t, t), t)
            dq_add = jnp.zeros((t, LANE), F32)
            rowsums = []
            for hh in range(2):
                fq = _lane_col(f_ref[...], 2 * pr + hh)
                s, mask, sel = _fox_scores(q_ref[...], k_ref[...], fq, fk_ref[hh], hh, i * t, j * t, pad_rows, masked)
                p = _probs(s, mask, _lane_col(lse_ref[...], hh))
                dop = jnp.where(sel, do_ref[...], jnp.zeros_like(do_ref[...]))
                ds = p * (_dot(dop, v_ref[...].astype(BF16), "nt") - _lane_col(dl_ref[...], 2 * pr + hh))
                dsb = ds.astype(BF16)
                dva[hh] += _dot(p.astype(BF16), do_ref[...], "tn")
                dka[hh] += _dot(dsb, q_ref[...], "tn")
                dfa[hh] -= jnp.sum(ds, axis=0, keepdims=True)
                dq_add = dq_add + _dot(dsb, jnp.where(sel, k_ref[...], jnp.zeros_like(k_ref[...])), "nn")
                rowsums.append(jnp.sum(ds, axis=1, keepdims=True))
            dq_ref[rows, :] += dq_add
            dfq_ref[rows, :] += _to_lanes(rowsums)

        @pl.when(i >= j)
        def _():
            _both_variants((j == i) | (j == 0), step)

        @pl.when(i == n - 1)
        def _():
            dk_ref[...] = jnp.where(lane < FOX_DH, dka[0], dka[1])
            dv_ref[...] = jnp.where(lane < FOX_DH, dva[0], dva[1]).astype(dv_ref.dtype)
            dfk_ref[...] = dfa[...]

    qspec = pl.BlockSpec((t, LANE), lambda p, j, i: (jnp.maximum(i, j), p))
    f_q = pl.BlockSpec((t, LANE), lambda p, j, i: (jnp.maximum(i, j), 0))
    kspec = pl.BlockSpec((t, LANE), lambda p, j, i: (j, p))
    vspec = pl.BlockSpec((t, LANE), lambda p, j, i: (j, v_col + p))
    rspec = pl.BlockSpec((2, 1, t), lambda p, j, i: (p, 0, j))
    whole = pl.BlockSpec((lp, LANE), lambda p, j, i: (0, p))
    wide = jax.ShapeDtypeStruct((lp, FOX_W), F32)
    return _call(body, name="fox_bwd", grid=(FOX_W // LANE, n, n),
                 out_shape=[wide, wide, jax.ShapeDtypeStruct((lp, FOX_W), BF16), wide,
                            jax.ShapeDtypeStruct((8, 1, lp), F32)],
                 in_specs=[qspec, kspec, vspec, f_q, rspec, qspec, qspec, f_q],
                 out_specs=[whole, kspec, kspec, whole, rspec],
                 scratch=[pltpu.VMEM((2, t, LANE), F32), pltpu.VMEM((2, t, LANE), F32), pltpu.VMEM((2, 1, t), F32)],
                 sem=("parallel", "arbitrary", "arbitrary"))(q, k, v, fsum, frow, do, lse, delta)


def _gdn_chunk(q, k, v, beta, gcol, grow, s, inv):
    c = q.shape[-2]
    ri, ci = _iota((c, c), 0), _iota((c, c), 1)
    dec = jnp.exp(jnp.where(ri >= ci, gcol - grow, NEG))
    dec_strict = jnp.where(ri > ci, dec, 0.0)
    eg = jnp.exp(gcol)
    kb = k * beta
    t = inv(_mmb(kb, k, "nt") * dec_strict)
    u_hat = _mmh(t, v * beta, "nn")
    w = _mmh(t, kb * eg, "nn")
    u = u_hat - _mmb(w, s, "nn")
    o = _mmb(q * eg, s, "nn") + _mmb(_mmb(q, k, "nt") * dec, u, "nn")
    glast = jnp.sum(jnp.where(_iota((1, c), 1) == c - 1, grow, 0.0), axis=-1, keepdims=True)
    s_new = s * jnp.exp(glast) + _mmb(k * jnp.exp(glast - gcol), u, "tn")
    return o, s_new


def _gdn_specs(lp, reverse):
    n = lp // CHUNK
    pos = (lambda c: n - 1 - c) if reverse else (lambda c: c)
    wide = pl.BlockSpec((CHUNK, GDN_W), lambda c: (pos(c), 0))
    lanes = pl.BlockSpec((CHUNK, LANE), lambda c: (pos(c), 0))
    row = pl.BlockSpec((GDN_H, 1, 1, CHUNK), lambda c: (0, pos(c), 0, 0))
    st = pl.BlockSpec((GDN_H, 1, GDN_DH, GDN_DH), lambda c: (0, pos(c), 0, 0))
    return n, wide, lanes, row, st


def _heads(ref):
    return jnp.stack([ref[:, h * GDN_DH:(h + 1) * GDN_DH] for h in range(GDN_H)])


def _put_heads(ref, val):
    for h in range(GDN_H):
        ref[:, h * GDN_DH:(h + 1) * GDN_DH] = val[h]


def _head_cols(blk, lane0):
    return jnp.stack([_lane_col(blk, lane0 + h) for h in range(GDN_H)])


def _gdn_fwd(q, k, v, bg, grow):
    lp = q.shape[0]
    n, wide, lanes, row, st = _gdn_specs(lp, False)

    def body(q_ref, k_ref, v_ref, bg_ref, gr_ref, o_ref, sp_ref, t_ref, s_scr):
        @pl.when(pl.program_id(0) == 0)
        def _():
            s_scr[...] = jnp.zeros_like(s_scr)

        def inv(m):
            t = _inv_unit_lower_raw(m)
            t_ref[:, 0] = t
            return t

        s = s_scr[...]
        sp_ref[:, 0] = s
        bg_blk = bg_ref[...]
        o, s_new = _gdn_chunk(_heads(q_ref), _heads(k_ref), _heads(v_ref), _head_cols(bg_blk, 8),
                              _head_cols(bg_blk, 16), gr_ref[:, 0], s, inv)
        _put_heads(o_ref, o)
        s_scr[...] = s_new

    tri = pl.BlockSpec((GDN_H, 1, CHUNK, CHUNK), lambda c: (0, c, 0, 0))
    return _call(body, name="gdn_fwd", grid=(n,),
                 out_shape=[jax.ShapeDtypeStruct((lp, GDN_W), F32),
                            jax.ShapeDtypeStruct((GDN_H, n, GDN_DH, GDN_DH), F32),
                            jax.ShapeDtypeStruct((GDN_H, n, CHUNK, CHUNK), F32)],
                 in_specs=[wide, wide, wide, lanes, row], out_specs=[wide, st, tri],
                 scratch=[pltpu.VMEM((GDN_H, GDN_DH, GDN_DH), F32)], sem=("arbitrary",))(q, k, v, bg, grow)


def _gdn_bwd(q, k, v, bg, grow, sprev, tinv, do):
    lp = q.shape[0]
    n, wide, lanes, row, st = _gdn_specs(lp, True)

    def body(q_ref, k_ref, v_ref, bg_ref, gr_ref, sp_ref, t_ref, do_ref,
             dq_ref, dk_ref, dv_ref, dbg_ref, dgr_ref, ds_scr):
        @pl.when(pl.program_id(0) == 0)
        def _():
            ds_scr[...] = jnp.zeros_like(ds_scr)

        t_saved = t_ref[:, 0]
        fn = functools.partial(_gdn_chunk, inv=lambda m: _inv_given(m, t_saved))
        bg_blk = bg_ref[...]
        _, vjp = jax.vjp(fn, _heads(q_ref), _heads(k_ref), _heads(v_ref), _head_cols(bg_blk, 8),
                         _head_cols(bg_blk, 16), gr_ref[:, 0], sp_ref[:, 0])
        dq, dk, dv, db, dgc, dgr, ds = vjp((_heads(do_ref), ds_scr[...]))
        _put_heads(dq_ref, dq)
        _put_heads(dk_ref, dk)
        _put_heads(dv_ref, dv)
        lane = _iota((CHUNK, LANE), 1)
        dbg = jnp.zeros((CHUNK, LANE), F32)
        for h in range(GDN_H):
            dbg = jnp.where(lane == 8 + h, db[h], jnp.where(lane == 16 + h, dgc[h], dbg))
        dbg_ref[...] = dbg
        dgr_ref[:, 0] = dgr
        ds_scr[...] = ds

    wshape = jax.ShapeDtypeStruct((lp, GDN_W), F32)
    tri = pl.BlockSpec((GDN_H, 1, CHUNK, CHUNK), lambda c: (0, n - 1 - c, 0, 0))
    return _call(body, name="gdn_bwd", grid=(n,),
                 out_shape=[wshape, wshape, wshape, jax.ShapeDtypeStruct((lp, LANE), F32),
                            jax.ShapeDtypeStruct((GDN_H, n, 1, CHUNK), F32)],
                 in_specs=[wide, wide, wide, lanes, row, st, tri, wide], out_specs=[wide, wide, wide, lanes, row],
                 scratch=[pltpu.VMEM((GDN_H, GDN_DH, GDN_DH), F32)], sem=("arbitrary",))(
                     q, k, v, bg, grow, sprev, tinv, do)


def _halo_prev(width, colblock, tm):
    return pl.BlockSpec((8, width), lambda i, cb=colblock: (jnp.maximum(i * (tm // 8) - 1, 0), cb))


def _gdn_act(proj, small, conv_w, alog_row, dtb_row):
    lp = proj.shape[0]
    tm = _pick(lp, ROW_TILES)

    def body(xq, xk, xv, hq, hk, hv, wq, wk, wv, sm, al, dt, q_ref, k_ref, v_ref, bg_ref):
        first = (pl.program_id(0) > 0).astype(F32)
        cs = [_causal_conv(x[...], h[...] * first, w[...]) for x, h, w in ((xq, hq, wq), (xk, hk, wk), (xv, hv, wv))]
        q, k, v, bg = _gdn_act_fn(cs[0], cs[1], cs[2], sm[...], al[...], dt[...])
        q_ref[...], k_ref[...], v_ref[...], bg_ref[...] = q, k, v, bg

    wide = jax.ShapeDtypeStruct((lp, GDN_W), F32)
    wspec = [pl.BlockSpec((4, GDN_W), lambda i, c=c: (0, c)) for c in range(3)]
    return _call(body, name="gdn_act", grid=(lp // tm,),
                 out_shape=[wide, wide, wide, jax.ShapeDtypeStruct((lp, LANE), F32)],
                 in_specs=[_row_spec(GDN_W, c, tm) for c in range(3)] + [_halo_prev(GDN_W, c, tm) for c in range(3)]
                 + wspec + [_row_spec(LANE, 0, tm), _full_spec(alog_row), _full_spec(dtb_row)],
                 out_specs=[_row_spec(GDN_W, 0, tm)] * 3 + [_row_spec(LANE, 0, tm)], sem=("parallel",))(
                     proj, proj, proj, proj, proj, proj, conv_w, conv_w, conv_w, small, alog_row, dtb_row)


def _gdn_act_bwd(proj, small, conv_w, alog_row, dtb_row, dq, dk, dv, dbg):
    lp = proj.shape[0]
    tm = ROW_TILE

    def body(xq, xk, xv, hq, hk, hv, wq, wk, wv, sm, al, dt, dq_r, dk_r, dv_r, dbg_r,
             dc_ref, dsm_ref, dal_ref, ddt_ref, dw_ref):
        i = pl.program_id(0)
        first = (i > 0).astype(F32)
        xs = [(x[...], h[...] * first, w[...]) for x, h, w in ((xq, hq, wq), (xk, hk, wk), (xv, hv, wv))]
        cs = [_causal_conv(*t) for t in xs]
        _, vjp = jax.vjp(_gdn_act_fn, cs[0], cs[1], cs[2], sm[...], al[...], dt[...])
        dcq, dck, dcv, dsm, dal, ddt = vjp((dq_r[...], dk_r[...], dv_r[...], dbg_r[...]))
        dsm_ref[...] = dsm

        @pl.when(i == 0)
        def _():
            dal_ref[...] = jnp.zeros_like(dal_ref)
            ddt_ref[...] = jnp.zeros_like(ddt_ref)
            dw_ref[...] = jnp.zeros_like(dw_ref)

        dal_ref[...] += dal
        ddt_ref[...] += ddt
        for c, (dc, (x, h, w)) in enumerate(zip((dcq, dck, dcv), xs)):
            dc_ref[:, c * GDN_W:(c + 1) * GDN_W] = dc
            rows = [jnp.sum(_shift_down(x, h, 3 - kk) * dc, axis=0, keepdims=True) for kk in range(4)]
            dw_ref[:, c * GDN_W:(c + 1) * GDN_W] += jnp.concatenate(rows, axis=0)

    wspec = [pl.BlockSpec((4, GDN_W), lambda i, c=c: (0, c)) for c in range(3)]
    row128 = jax.ShapeDtypeStruct((1, LANE), F32)
    return _call(body, name="gdn_act_bwd", grid=(lp // tm,),
                 out_shape=[jax.ShapeDtypeStruct((lp, 3 * GDN_W), F32), jax.ShapeDtypeStruct((lp, LANE), F32),
                            row128, row128, jax.ShapeDtypeStruct((4, 3 * GDN_W), F32)],
                 in_specs=[_row_spec(GDN_W, c, tm) for c in range(3)] + [_halo_prev(GDN_W, c, tm) for c in range(3)]
                 + wspec + [_row_spec(LANE, 0, tm), _full_spec(alog_row), _full_spec(dtb_row)]
                 + [_row_spec(GDN_W, 0, tm)] * 3 + [_row_spec(LANE, 0, tm)],
                 out_specs=[_row_spec(3 * GDN_W, 0, tm), _row_spec(LANE, 0, tm),
                            _full_spec(alog_row), _full_spec(dtb_row), pl.BlockSpec((4, 3 * GDN_W), lambda i: (0, 0))],
                 sem=("arbitrary",))(proj, proj, proj, proj, proj, proj, conv_w, conv_w, conv_w, small,
                                     alog_row, dtb_row, dq, dk, dv, dbg)


def _ffn_act(up_pre, conv_w):
    lp = up_pre.shape[0]
    tm = _pick(lp, ROW_TILES)

    def body(xg, xv, hg, hv, wg, wv, a_ref):
        first = (pl.program_id(0) > 0).astype(F32)
        ug = _causal_conv(xg[...], hg[...] * first, wg[...])
        uv = _causal_conv(xv[...], hv[...] * first, wv[...])
        a_ref[...] = (_silu(ug) * uv).astype(a_ref.dtype)

    wspec = [pl.BlockSpec((3, D_FF), lambda i, c=c: (0, c)) for c in range(2)]
    return _call(body, name="ffn_act", grid=(lp // tm,), out_shape=jax.ShapeDtypeStruct((lp, D_FF), BF16),
                 in_specs=[_row_spec(D_FF, c, tm) for c in range(2)] + [_halo_prev(D_FF, c, tm) for c in range(2)] + wspec,
                 out_specs=_row_spec(D_FF, 0, tm), sem=("parallel",))(up_pre, up_pre, up_pre, up_pre, conv_w, conv_w)


def _ffn_act_bwd(up_pre, conv_w, dact, tm=ROW_TILE):
    lp = up_pre.shape[0]

    def body(xg, xv, hg, hv, wg, wv, da, du_ref, dw_ref):
        i = pl.program_id(0)
        first = (i > 0).astype(F32)
        xs = [(x[...], h[...] * first, w[...]) for x, h, w in ((xg, hg, wg), (xv, hv, wv))]
        ug, uv = [_causal_conv(*t) for t in xs]
        _, vjp = jax.vjp(lambda a, b: _silu(a) * b, ug, uv)
        dus = vjp(da[...].astype(F32))

        @pl.when(i == 0)
        def _():
            dw_ref[...] = jnp.zeros_like(dw_ref)

        for c, (du, (x, h, w)) in enumerate(zip(dus, xs)):
            du_ref[:, c * D_FF:(c + 1) * D_FF] = du
            rows = [jnp.sum(_shift_down(x, h, 2 - kk) * du, axis=0, keepdims=True) for kk in range(3)]
            dw_ref[:, c * D_FF:(c + 1) * D_FF] += jnp.concatenate(rows, axis=0)

    wspec = [pl.BlockSpec((3, D_FF), lambda i, c=c: (0, c)) for c in range(2)]
    return _call(body, name="ffn_act_bwd", grid=(lp // tm,),
                 out_shape=[jax.ShapeDtypeStruct((lp, 2 * D_FF), F32), jax.ShapeDtypeStruct((3, 2 * D_FF), F32)],
                 in_specs=[_row_spec(D_FF, c, tm) for c in range(2)] + [_halo_prev(D_FF, c, tm) for c in range(2)]
                 + wspec + [_row_spec(D_FF, 0, tm)],
                 out_specs=[_row_spec(2 * D_FF, 0, tm), pl.BlockSpec((3, 2 * D_FF), lambda i: (0, 0))],
                 sem=("arbitrary",))(up_pre, up_pre, up_pre, up_pre, conv_w, conv_w, dact)


def _conv_bwd_x(dy, w, pad_rows, width, name):
    lp, ctot = dy.shape
    tm = _pick(lp, ROW_TILES)
    nt = lp // tm
    kk = w.shape[0]

    def body(d_ref, h_ref, w_ref, o_ref):
        i = pl.program_id(0)
        last = (i < nt - 1).astype(F32)
        d, h, wv = d_ref[...], h_ref[...] * last, w_ref[...]
        y = d * wv[kk - 1:kk, :]
        for k in range(kk - 1):
            y = y + _shift_up(d, h, kk - 1 - k) * wv[k:k + 1, :]
        valid = (i * tm + _iota((tm, 1), 0)) >= pad_rows
        o_ref[...] = jnp.where(valid, y, 0.0).astype(o_ref.dtype)

    return _call(body, name=name, grid=(nt, ctot // width), out_shape=jax.ShapeDtypeStruct((lp, ctot), BF16),
                 in_specs=[pl.BlockSpec((tm, width), lambda i, c: (i, c)),
                           pl.BlockSpec((8, width), lambda i, c: (jnp.minimum((i + 1) * (tm // 8), lp // 8 - 1), c)),
                           pl.BlockSpec((kk, width), lambda i, c: (0, c))],
                 out_specs=pl.BlockSpec((tm, width), lambda i, c: (i, c)), sem=("parallel", "parallel"))(dy, dy, w)


def _loss_head(h_res, target, row_start, tm=ROW_TILE):
    lp, d = h_res.shape
    t0 = row_start // tm

    def body(h_ref, t_ref, dy_ref, loss_ref):
        i = pl.program_id(0)

        @pl.when(i == 0)
        def _():
            loss_ref[...] = jnp.zeros_like(loss_ref)

        live = (i >= t0).astype(F32)
        err = (h_ref[...] - t_ref[...]) * live
        dy_ref[...] = err * (1.0 / d)
        loss_ref[...] += 0.5 / d * jnp.sum(err * err)

    return _call(body, name="loss_head", grid=(lp // tm,),
                 out_shape=[jax.ShapeDtypeStruct((lp, d), F32), jax.ShapeDtypeStruct((8, LANE), F32)],
                 in_specs=[pl.BlockSpec((tm, d), lambda i: (i, 0)),
                           pl.BlockSpec((tm, d), lambda i: (jnp.maximum(i - t0, 0), 0))],
                 out_specs=[pl.BlockSpec((tm, d), lambda i: (i, 0)), pl.BlockSpec((8, LANE), lambda i: (0, 0))],
                 sem=("arbitrary",))(h_res, target)


def _sum_adamw(parts, w, m, v, name):
    a, r, c = w.shape
    tm = _pick(r, (256, 128, 64, 32, 16))
    bc1 = 1.0 - ADAM_B1 ** ADAM_STEP
    bc2 = 1.0 - ADAM_B2 ** ADAM_STEP

    def body(p_ref, w_ref, m_ref, v_ref, g_ref, d_ref, nm_ref, nv_ref):
        g = p_ref[0, 0].astype(F32)
        for s in range(1, N_DEV):
            g = g + p_ref[s, 0].astype(F32)
        nm = ADAM_B1 * m_ref[0] + (1.0 - ADAM_B1) * g
        nv = ADAM_B2 * v_ref[0] + (1.0 - ADAM_B2) * (g * g)
        g_ref[0] = g
        nm_ref[0] = nm
        nv_ref[0] = nv
        d_ref[0] = -ADAM_LR * ((nm / bc1) / (jnp.sqrt(nv / bc2) + ADAM_EPS) + ADAM_WD * w_ref[0])

    spec = pl.BlockSpec((1, tm, c), lambda l, i: (l, i, 0))
    shp = jax.ShapeDtypeStruct((a, r, c), F32)
    return _call(body, name=name, grid=(a, r // tm), out_shape=[shp] * 4,
                 in_specs=[pl.BlockSpec((N_DEV, 1, tm, c), lambda l, i: (0, l, i, 0)), spec, spec, spec],
                 out_specs=[spec] * 4, sem=("parallel", "parallel"))(parts, w, m, v)


_ANY = pl.BlockSpec(memory_space=pl.ANY)


def _all_gather(block, name):
    def body(x_ref, out_ref, send_sems, recv_sems, local_sem):
        x, y, c = lax.axis_index("x"), lax.axis_index("y"), lax.axis_index("c")
        me, sibling = (x, y, c), (x, y, 1 - c)
        chips = [(1 - x, y), (x, 1 - y), (1 - x, 1 - y)]

        def slot(px, py, pc):
            return out_ref.at[4 * px + 2 * py + pc]

        def copy(k, blk, to, src=None):
            return pltpu.make_async_remote_copy(
                src_ref=slot(*blk) if src is None else src, dst_ref=slot(*blk),
                send_sem=send_sems.at[k], recv_sem=recv_sems.at[k], device_id=to, device_id_type=MESH)

        mine = pltpu.make_async_copy(x_ref, slot(*me), local_sem)
        mine.start()
        first = [copy(0, me, sibling, src=x_ref)]
        first += [copy(1 + j, me, (*chip, c), src=x_ref) for j, chip in enumerate(chips)]
        for cp in first:
            cp.start()
        passed = [copy(4 + j, (*chip, c), sibling) for j, chip in enumerate(chips)]
        for j, chip in enumerate(chips):
            copy(1 + j, (*chip, c), me).wait_recv()
            passed[j].start()
        copy(0, sibling, me).wait_recv()
        for j, chip in enumerate(chips):
            copy(4 + j, (*chip, 1 - c), me).wait_recv()
        for cp in first + passed:
            cp.wait_send()
        mine.wait()

    return pl.pallas_call(
        body, name=name, out_shape=jax.ShapeDtypeStruct((N_DEV,) + block.shape, block.dtype),
        in_specs=[_ANY], out_specs=_ANY,
        scratch_shapes=[pltpu.SemaphoreType.DMA((7,)), pltpu.SemaphoreType.DMA((7,)), pltpu.SemaphoreType.DMA],
    )(block)


def _all_to_all(src, name):
    def body(s_ref, o_ref, send_sems, recv_sems, local_sem):
        x, y, c = lax.axis_index("x"), lax.axis_index("y"), lax.axis_index("c")
        me = 4 * x + 2 * y + c
        mine = pltpu.make_async_copy(s_ref.at[me], o_ref.at[me], local_sem)
        mine.start()
        copies = []
        for k in range(1, N_DEV):
            px = 1 - x if k & 4 else x
            py = 1 - y if k & 2 else y
            pc = 1 - c if k & 1 else c
            peer = 4 * px + 2 * py + pc
            copies.append((pltpu.make_async_remote_copy(
                src_ref=s_ref.at[peer], dst_ref=o_ref.at[me], send_sem=send_sems.at[k - 1],
                recv_sem=recv_sems.at[k - 1], device_id=(px, py, pc), device_id_type=MESH), peer, k))
        for cp, _, _ in copies:
            cp.start()
        for cp, peer, k in copies:
            cp.wait_send()
            pltpu.make_async_remote_copy(
                src_ref=s_ref.at[peer], dst_ref=o_ref.at[peer], send_sem=send_sems.at[k - 1],
                recv_sem=recv_sems.at[k - 1], device_id=(x, y, c), device_id_type=MESH).wait_recv()
        mine.wait()

    return pl.pallas_call(
        body, name=name, out_shape=jax.ShapeDtypeStruct(src.shape, src.dtype), in_specs=[_ANY], out_specs=_ANY,
        scratch_shapes=[pltpu.SemaphoreType.DMA((7,)), pltpu.SemaphoreType.DMA((7,)), pltpu.SemaphoreType.DMA],
    )(src)


def _pack(blocks, width, dtype, row_mult):
    flat = jnp.concatenate([b.astype(dtype).reshape(-1) for b in blocks])
    per = width * row_mult
    total = -(-flat.shape[0] // per) * per
    return jnp.pad(flat, (0, total - flat.shape[0])).reshape(total // width, width)


def _pack_dest(fulls, axes, width, dtype, row_mult):
    rows = []
    for f, ax in zip(fulls, axes):
        f = f.astype(dtype)
        if ax is None:
            rows.append(jnp.broadcast_to(f.reshape(1, -1), (N_DEV, f.size)))
        else:
            shp = f.shape
            f = f.reshape(shp[:ax] + (N_DEV, shp[ax] // N_DEV) + shp[ax + 1:])
            rows.append(jnp.moveaxis(f, ax, 0).reshape(N_DEV, -1))
    flat = jnp.concatenate(rows, axis=1)
    per = width * row_mult
    total = -(-flat.shape[1] // per) * per
    return jnp.pad(flat, ((0, 0), (0, total - flat.shape[1]))).reshape(N_DEV, total // width, width)


def _unpack(packed, shapes):
    flat = packed.reshape(-1)
    out, off = [], 0
    for s in shapes:
        n = 1
        for d in s:
            n *= d
        out.append(flat[off:off + n].reshape(s))
        off += n
    return out


def _unpack_gathered(gathered, shapes, axes):
    flat = gathered.reshape(N_DEV, -1)
    out, off = [], 0
    for s, ax in zip(shapes, axes):
        n = 1
        for d in s:
            n *= d
        blk = jnp.moveaxis(flat[:, off:off + n].reshape((N_DEV,) + tuple(s)), 0, ax)
        out.append(blk.reshape(tuple(s[:ax]) + (N_DEV * s[ax],) + tuple(s[ax + 1:])))
        off += n
    return out


def _shard_cols(blocks, a, b):
    shard = blocks[0].shape[1]
    out = []
    while a < b:
        d = a // shard
        hi = min(b, (d + 1) * shard)
        out.append(blocks[d][:, a - d * shard:hi - d * shard])
        a = hi
    return out


def _permute_w_in(blocks):
    main = jnp.concatenate(_shard_cols(blocks, O_GQ, O_BL) + _shard_cols(blocks, O_GZ, O_END)
                           + _shard_cols(blocks, O_FQ, O_FL), axis=1)
    pad = jnp.zeros((blocks[0].shape[0], LANE - 24), blocks[0].dtype)
    small = jnp.concatenate(_shard_cols(blocks, O_FL, O_GQ) + _shard_cols(blocks, O_BL, O_GZ) + [pad], axis=1)
    return main, small


_W_IN_SEGS = ((O_FQ, O_FL, True, C_FQ), (O_FL, O_GQ, False, 0), (O_GQ, O_BL, True, C_GQ), (O_BL, O_GZ, False, 8),
              (O_GZ, O_END, True, C_GZ))


def _unpermute_cols(main, small, a, b):
    out = []
    for s0, s1, is_main, t0 in _W_IN_SEGS:
        lo, hi = max(a, s0), min(b, s1)
        if lo < hi:
            out.append((main if is_main else small)[:, t0 + lo - s0:t0 + hi - s0])
    return jnp.concatenate(out, axis=1)


def _lanes(vec, start):
    return jnp.pad(vec.astype(F32), (start, LANE - start - vec.shape[0])).reshape(1, LANE)


BIG = ("w_in", "w_branch_a", "w_branch_b", "w_out", "w_up", "w_down")
BIG_AXES = (2, 2, 1, 1, 2, 1)
SHARDED_SMALL = ("meta_tokens", "gdn_conv_w", "ffn_conv_w")
SHARDED_SMALL_AXES = (1, 2, 2)
REPL = ("norm1_g", "fox_f_bias", "fox_q_norm_g", "fox_k_norm_g", "gdn_a_log", "gdn_dt_bias", "gdn_norm_g", "norm2_g")
ORDER = ("meta_tokens", "norm1_g", "w_in", "fox_f_bias", "fox_q_norm_g", "fox_k_norm_g", "gdn_conv_w", "gdn_a_log",
         "gdn_dt_bias", "gdn_norm_g", "w_branch_a", "w_branch_b", "w_out", "norm2_g", "w_up", "ffn_conv_w", "w_down")


def _layer_fwd(h_res, wl, pad_rows):
    lp = h_res.shape[0]
    sv = {"res_in": h_res}
    (h1,) = _rowwise(_rmsnorm_fn, [(h_res, D, 0)], [wl["norm1_g"]], [(D, BF16)], "rmsnorm1")
    proj = _matmul(h1, wl["w_main"], "nn", F32, name="mm_in")
    small = _matmul(h1, wl["w_small"], "nn", F32, name="mm_in_small")
    sv.update(h1=h1, proj=proj, small=small)

    fox_fn = functools.partial(_fox_prep_fn, pad_rows)
    qh, kh, logf = _rowwise(fox_fn, [(proj, FOX_W, C_FQ // FOX_W), (proj, FOX_W, C_FK // FOX_W), (small, LANE, 0)],
                            [wl["qg"], wl["kg"], wl["fb"]], [(FOX_W, BF16), (FOX_W, BF16), (LANE, F32)], "fox_prep")
    fsum = _cumsum_rows(logf, False, "fox_cumsum")
    frow = fsum[:, :8].T.reshape(8, 1, lp)
    o_a, lse = _fox_fwd(qh, kh, proj, fsum, frow, pad_rows, C_FV // LANE)
    y_a = _matmul(o_a, wl["w_branch_a"], "nn", F32, name="mm_branch_a")
    sv.update(qh=qh, kh=kh, fsum=fsum, frow=frow, o_a=o_a, lse=lse)

    gq, gk, gv, bg = _gdn_act(proj, small, wl["gdn_conv_w"], wl["alog"], wl["dtb"])
    grow = bg[:, 16:24].T.reshape(8, lp // CHUNK, 1, CHUNK)
    o_raw, sprev, tinv = _gdn_fwd(gq, gk, gv, bg, grow)
    (o_b,) = _rowwise(_gdn_post_fn, [(o_raw, GDN_W, 0), (proj, GDN_W, C_GZ // GDN_W)], [wl["gn"]], [(GDN_W, BF16)],
                      "gdn_post")
    y_b = _matmul(o_b, wl["w_branch_b"], "nn", F32, name="mm_branch_b")
    sv.update(gq=gq, gk=gk, gv=gv, bg=bg, grow=grow, o_raw=o_raw, sprev=sprev, tinv=tinv, o_b=o_b)

    (mixed,) = _rowwise(_merge_fn, [(proj, D, C_GATE // D), (proj, D, C_GATE // D + 1), (y_a, D, 0), (y_b, D, 0)], [],
                        [(D, BF16)], "merge")
    res_mid = _matmul(mixed, wl["w_out"], "nn", F32, add=h_res, name="mm_out")
    sv.update(y_a=y_a, y_b=y_b, mixed=mixed, res_mid=res_mid)

    (h2,) = _rowwise(_rmsnorm_fn, [(res_mid, D, 0)], [wl["norm2_g"]], [(D, BF16)], "rmsnorm2")
    up_pre = _matmul(h2, wl["w_up"], "nn", F32, name="mm_up")
    act = _ffn_act(up_pre, wl["ffn_conv_w"])
    out = _matmul(act, wl["w_down"], "nn", F32, add=res_mid, name="mm_down")
    sv.update(h2=h2, up_pre=up_pre, act=act)
    return out, sv


def _layer_bwd(dres, wl, sv, pad_rows):
    lp = dres.shape[0]
    gw = {}
    gw["w_down"] = _matmul(sv["act"], dres, "tn", F32, name="mm_dw_down")
    dact = _matmul(dres, wl["w_down"], "nt", BF16, name="mm_dact")
    dup, gw["ffn_conv_w"] = _ffn_act_bwd(sv["up_pre"], wl["ffn_conv_w"], dact)
    dup_pre = _conv_bwd_x(dup, wl["ffn_conv_w"], pad_rows, D_FF, "ffn_conv_bwd")
    gw["w_up"] = _matmul(sv["h2"], dup_pre, "tn", F32, name="mm_dw_up")
    dh2 = _matmul(dup_pre, wl["w_up"], "nt", F32, name="mm_dh2")
    (dmid,), (gw["norm2_g"],) = _rowwise_bwd(_rmsnorm_fn, [(sv["res_mid"], D, 0)], [wl["norm2_g"]], [(dh2, D, 0)],
                                             "rmsnorm2_bwd", pad_rows, [F32], adds=[(dres, D, 0)])
    gw["w_out"] = _matmul(sv["mixed"], dmid, "tn", F32, name="mm_dw_out")
    dmixed = _matmul(dmid, wl["w_out"], "nt", F32, name="mm_dmixed")
    proj, small = sv["proj"], sv["small"]
    (dg0, dg1, dya, dyb), _ = _rowwise_bwd(
        _merge_fn, [(proj, D, C_GATE // D), (proj, D, C_GATE // D + 1), (sv["y_a"], D, 0), (sv["y_b"], D, 0)], [],
        [(dmixed, D, 0)], "merge_bwd", pad_rows, [BF16, BF16, BF16, BF16])
    gw["w_branch_a"] = _matmul(sv["o_a"], dya, "tn", F32, name="mm_dw_a")
    do_a = _matmul(dya, wl["w_branch_a"], "nt", BF16, name="mm_do_a")
    gw["w_branch_b"] = _matmul(sv["o_b"], dyb, "tn", F32, name="mm_dw_b")
    do_b = _matmul(dyb, wl["w_branch_b"], "nt", F32, name="mm_do_b")

    (do_raw, dgz), (gw["gn"],) = _rowwise_bwd(_gdn_post_fn, [(sv["o_raw"], GDN_W, 0), (proj, GDN_W, C_GZ // GDN_W)],
                                              [wl["gn"]], [(do_b, GDN_W, 0)], "gdn_post_bwd", pad_rows, [F32, BF16])
    dgq, dgk, dgv, dbg, dgrow = _gdn_bwd(sv["gq"], sv["gk"], sv["gv"], sv["bg"], sv["grow"], sv["sprev"],
                                         sv["tinv"], do_raw)
    dbg = dbg + jnp.pad(dgrow.reshape(8, lp).T, ((0, 0), (16, LANE - 24)))
    dconv, dsmall_g, gw["alog"], gw["dtb"], gw["gdn_conv_w"] = _gdn_act_bwd(
        proj, small, wl["gdn_conv_w"], wl["alog"], wl["dtb"], dgq, dgk, dgv, dbg)
    dqkv = _conv_bwd_x(dconv, wl["gdn_conv_w"], pad_rows, GDN_W, "gdn_conv_bwd")

    (delta,) = _rowwise(_fox_delta_fn, [(sv["o_a"], FOX_W, 0), (do_a, FOX_W, 0)], [], [(LANE, F32)], "fox_delta")
    dqh, dkh, dvh, dfq, dfk = _fox_bwd(sv["qh"], sv["kh"], proj, sv["fsum"], sv["frow"], do_a, sv["lse"], delta,
                                       pad_rows, C_FV // LANE)
    df8 = dfq.reshape(lp, FOX_W // LANE, LANE)[:, :, :2].reshape(lp, 8) + dfk.reshape(8, lp).T
    dlogf = _cumsum_rows(jnp.pad(df8, ((0, 0), (0, LANE - 8))), True, "fox_cumsum_bwd")
    fox_fn = functools.partial(_fox_prep_fn, pad_rows)
    (dfq_p, dfk_p, dsmall_f), (gw["qg"], gw["kg"], gw["fb"]) = _rowwise_bwd(
        fox_fn, [(proj, FOX_W, C_FQ // FOX_W), (proj, FOX_W, C_FK // FOX_W), (small, LANE, 0)],
        [wl["qg"], wl["kg"], wl["fb"]], [(dqh, FOX_W, 0), (dkh, FOX_W, 0), (dlogf, LANE, 0)],
        "fox_prep_bwd", pad_rows, [BF16, BF16, F32], adds=[None, None, (dsmall_g, LANE, 0)])

    dproj = jnp.concatenate([dqkv, dgz, dg0, dg1, dfq_p, dfk_p, dvh], axis=1)
    gw["w_main"] = _matmul(sv["h1"], dproj, "tn", F32, name="mm_dw_main")
    gw["w_small"] = _matmul(sv["h1"], dsmall_f, "tn", F32, name="mm_dw_small")
    dh1 = _matmul(dproj, wl["w_main"], "nt", F32, name="mm_dh1")
    dh1 = _matmul(dsmall_f, wl["w_small"], "nt", F32, add=dh1, name="mm_dh1_small")
    (din,), (gw["norm1_g"],) = _rowwise_bwd(_rmsnorm_fn, [(sv["res_in"], D, 0)], [wl["norm1_g"]], [(dh1, D, 0)],
                                            "rmsnorm1_bwd", pad_rows, [F32], adds=[(dmid, D, 0)])
    return din, gw


def kernel(x, meta_tokens, norm1_g, w_in, fox_f_bias, fox_q_norm_g, fox_k_norm_g, gdn_conv_w, gdn_a_log, gdn_dt_bias, gdn_norm_g, w_branch_a, w_branch_b, w_out, norm2_g, w_up, ffn_conv_w, w_down, loss_target, m_meta_tokens, m_norm1_g, m_w_in, m_fox_f_bias, m_fox_q_norm_g, m_fox_k_norm_g, m_gdn_conv_w, m_gdn_a_log, m_gdn_dt_bias, m_gdn_norm_g, m_w_branch_a, m_w_branch_b, m_w_out, m_norm2_g, m_w_up, m_ffn_conv_w, m_w_down, v_meta_tokens, v_norm1_g, v_w_in, v_fox_f_bias, v_fox_q_norm_g, v_fox_k_norm_g, v_gdn_conv_w, v_gdn_a_log, v_gdn_dt_bias, v_gdn_norm_g, v_w_branch_a, v_w_branch_b, v_w_out, v_norm2_g, v_w_up, v_ffn_conv_w, v_w_down):
    w = dict(meta_tokens=meta_tokens, norm1_g=norm1_g, w_in=w_in, fox_f_bias=fox_f_bias, fox_q_norm_g=fox_q_norm_g,
             fox_k_norm_g=fox_k_norm_g, gdn_conv_w=gdn_conv_w, gdn_a_log=gdn_a_log, gdn_dt_bias=gdn_dt_bias,
             gdn_norm_g=gdn_norm_g, w_branch_a=w_branch_a, w_branch_b=w_branch_b, w_out=w_out, norm2_g=norm2_g,
             w_up=w_up, ffn_conv_w=ffn_conv_w, w_down=w_down)
    mom = dict(meta_tokens=m_meta_tokens, norm1_g=m_norm1_g, w_in=m_w_in, fox_f_bias=m_fox_f_bias,
               fox_q_norm_g=m_fox_q_norm_g, fox_k_norm_g=m_fox_k_norm_g, gdn_conv_w=m_gdn_conv_w,
               gdn_a_log=m_gdn_a_log, gdn_dt_bias=m_gdn_dt_bias, gdn_norm_g=m_gdn_norm_g, w_branch_a=m_w_branch_a,
               w_branch_b=m_w_branch_b, w_out=m_w_out, norm2_g=m_norm2_g, w_up=m_w_up, ffn_conv_w=m_ffn_conv_w,
               w_down=m_w_down)
    var = dict(meta_tokens=v_meta_tokens, norm1_g=v_norm1_g, w_in=v_w_in, fox_f_bias=v_fox_f_bias,
               fox_q_norm_g=v_fox_q_norm_g, fox_k_norm_g=v_fox_k_norm_g, gdn_conv_w=v_gdn_conv_w,
               gdn_a_log=v_gdn_a_log, gdn_dt_bias=v_gdn_dt_bias, gdn_norm_g=v_gdn_norm_g, w_branch_a=v_w_branch_a,
               w_branch_b=v_w_branch_b, w_out=v_w_out, norm2_g=v_norm2_g, w_up=v_w_up, ffn_conv_w=v_ffn_conv_w,
               w_down=v_w_down)
    depth = norm1_g.shape[0]
    seq = x.shape[1]
    l_tok = N_META + seq
    lp = -(-l_tok // LANE) * LANE
    pad_rows = lp - l_tok
    row_start = pad_rows + N_META

    got = {n: _all_gather(w[n].astype(BF16), "gather_" + n) for n in BIG}
    small_shapes = [w[n].shape for n in SHARDED_SMALL]
    gathered_s = _all_gather(_pack([w[n] for n in SHARDED_SMALL], LANE, F32, 8), "gather_small")
    full = dict(zip(SHARDED_SMALL, _unpack_gathered(gathered_s, small_shapes, SHARDED_SMALL_AXES)))

    def join(name, l):
        return jnp.concatenate([got[name][d, l] for d in range(N_DEV)], axis=BIG_AXES[BIG.index(name)] - 1)

    layers = []
    for l in range(depth):
        w_main, w_small = _permute_w_in([got["w_in"][d, l] for d in range(N_DEV)])
        layers.append(dict(
            w_main=w_main, w_small=w_small, w_branch_a=join("w_branch_a", l), w_branch_b=join("w_branch_b", l),
            w_out=join("w_out", l), w_up=join("w_up", l), w_down=join("w_down", l),
            gdn_conv_w=full["gdn_conv_w"][l], ffn_conv_w=full["ffn_conv_w"][l],
            norm1_g=norm1_g[l].reshape(1, D), norm2_g=norm2_g[l].reshape(1, D),
            qg=jnp.tile(fox_q_norm_g[l], 8).reshape(1, FOX_W), kg=jnp.tile(fox_k_norm_g[l], 8).reshape(1, FOX_W),
            fb=_lanes(fox_f_bias[l], 0), alog=_lanes(gdn_a_log[l], 16), dtb=_lanes(gdn_dt_bias[l], 16),
            gn=jnp.tile(gdn_norm_g[l], 8).reshape(1, GDN_W)))

    h_res = jnp.concatenate([jnp.zeros((pad_rows, D), F32), full["meta_tokens"], x[0]], axis=0)
    saved = []
    for l in range(depth):
        h_res, sv = _layer_fwd(h_res, layers[l], pad_rows)
        saved.append(sv)
    dres, loss_part = _loss_head(h_res, loss_target[0], row_start)
    loss = lax.psum(loss_part[0, 0], ("x", "y", "c"))

    gws = [None] * depth
    for l in reversed(range(depth)):
        dres, gws[l] = _layer_bwd(dres, layers[l], saved[l], pad_rows)
    grad_x = dres[row_start:].reshape(x.shape)

    def stack(fn):
        return jnp.stack([fn(g) for g in gws])

    part = dict(
        meta_tokens=dres[pad_rows:row_start],
        norm1_g=stack(lambda g: g["norm1_g"][0]), norm2_g=stack(lambda g: g["norm2_g"][0]),
        fox_f_bias=stack(lambda g: g["fb"][0, 0:8]),
        fox_q_norm_g=stack(lambda g: g["qg"].reshape(8, FOX_DH).sum(0)),
        fox_k_norm_g=stack(lambda g: g["kg"].reshape(8, FOX_DH).sum(0)),
        gdn_conv_w=stack(lambda g: g["gdn_conv_w"]), gdn_a_log=stack(lambda g: g["alog"][0, 16:24]),
        gdn_dt_bias=stack(lambda g: g["dtb"][0, 16:24]),
        gdn_norm_g=stack(lambda g: g["gn"].reshape(8, GDN_DH).sum(0)),
        ffn_conv_w=stack(lambda g: g["ffn_conv_w"]))

    def dest_blocks(name, d):
        if name == "w_in":
            s = w_in.shape[2]
            blks = [_unpermute_cols(g["w_main"], g["w_small"], s * d, s * (d + 1)) for g in gws]
        elif BIG_AXES[BIG.index(name)] == 2:
            s = w[name].shape[2]
            blks = [g[name][:, s * d:s * (d + 1)] for g in gws]
        else:
            s = w[name].shape[1]
            blks = [g[name][s * d:s * (d + 1), :] for g in gws]
        return jnp.stack([b.astype(BF16) for b in blks])

    res = {}
    for n in BIG:
        landed = _all_to_all(jnp.stack([dest_blocks(n, d) for d in range(N_DEV)]), "scatter_" + n)
        res[n] = _sum_adamw(landed, w[n], mom[n], var[n], "adamw_" + n)

    small_names = SHARDED_SMALL + REPL
    small_axes = SHARDED_SMALL_AXES + (None,) * len(REPL)
    landed_s = _all_to_all(_pack_dest([part[n] for n in small_names], small_axes, LANE, F32, 8), "scatter_small")
    shapes_s = [w[n].shape for n in small_names]
    outs = _sum_adamw(landed_s[:, None], *[_pack([d[n] for n in small_names], LANE, F32, 8)[None] for d in (w, mom, var)],
                      "adamw_small")
    for o_idx, packed in enumerate(outs):
        for n, a in zip(small_names, _unpack(packed[0], shapes_s)):
            res.setdefault(n, [None] * 4)[o_idx] = a

    return (loss, grad_x, *[res[n][0] for n in ORDER], *[res[n][1] for n in ORDER],
            *[res[n][2] for n in ORDER], *[res[n][3] for n in ORDER])
```

```python
import functools

import jax
import jax.numpy as jnp
from jax import lax
from jax.experimental import pallas as pl
from jax.experimental.pallas import tpu as pltpu

F32, BF16 = jnp.float32, jnp.bfloat16
MESH = pl.DeviceIdType.MESH

D = 1024
N_META = 16
DEPTH = 4
EPS = 1e-6
NEG = -1e30
FOX_W, FOX_DH = 512, 64
GDN_W, GDN_DH, GDN_H = 1024, 128, 8
CHUNK = 64
D_FF = 2816
N_DEV = 8
ADAM_LR, ADAM_B1, ADAM_B2, ADAM_EPS, ADAM_WD, ADAM_STEP = 0.001, 0.9, 0.999, 1e-08, 0.01, 10

VMEM_LIMIT_BYTES = 48 * 1024 * 1024
MATMUL_VMEM_BUDGET = 36 * 1024 * 1024
ROW_TILE = 128
ROW_TILES_WIDE = (640, 512, 256, 128)
ROW_TILES = (320, 256, 128)
LANE = 128

C_GQ, C_GK, C_GV, C_GZ, C_GATE, C_FQ, C_FK, C_FV = 0, 1024, 2048, 3072, 4096, 6144, 6656, 7168
W_MAIN = 7680
O_FQ, O_FK, O_FV, O_FL, O_GQ, O_GK, O_GV, O_BL, O_AL, O_GZ, O_GATE, O_END = (
    0, 512, 1024, 1536, 1544, 2568, 3592, 4616, 4624, 4632, 5656, 7704)


def _pick(n, cands):
    for c in cands:
        if n % c == 0:
            return c
    return n


def _call(body, *, name, out_shape, in_specs, out_specs, grid=(), scratch=(), sem=None):
    kw = dict(vmem_limit_bytes=VMEM_LIMIT_BYTES)
    if sem is not None:
        kw["dimension_semantics"] = sem
    return pl.pallas_call(body, name=name, out_shape=out_shape, grid=grid, in_specs=in_specs,
                          out_specs=out_specs, scratch_shapes=list(scratch),
                          compiler_params=pltpu.CompilerParams(**kw))


_DIMS = {"nn": (((1,), (0,)), ((), ())), "nt": (((1,), (1,)), ((), ())), "tn": (((0,), (0,)), ((), ()))}


_DIMS_BATCHED = {"nn": (((2,), (1,)), ((0,), (0,))), "nt": (((2,), (2,)), ((0,), (0,))),
                 "tn": (((1,), (1,)), ((0,), (0,)))}


def _dot(a, b, mode, prec=None):
    dims = _DIMS[mode] if a.ndim == 2 else _DIMS_BATCHED[mode]
    return lax.dot_general(a, b, dims, precision=prec, preferred_element_type=F32)


def _mm_grads(f, mode, a, b, g):
    if mode == "nn":
        return f(g, b, "nt"), f(a, g, "tn")
    if mode == "nt":
        return f(g, b, "nn"), f(g, a, "tn")
    return f(b, g, "nt"), f(a, g, "nn")


@functools.partial(jax.custom_vjp, nondiff_argnums=(2,))
def _mmb(a, b, mode):
    return _dot(a.astype(BF16), b.astype(BF16), mode)


def _mmb_fwd(a, b, mode):
    return _mmb(a, b, mode), (a, b)


def _mmb_bwd(mode, res, g):
    return _mm_grads(_mmb, mode, res[0], res[1], g)


_mmb.defvjp(_mmb_fwd, _mmb_bwd)


def _split(a):
    hi = a.astype(BF16)
    return hi, (a - hi.astype(F32)).astype(BF16)


@functools.partial(jax.custom_vjp, nondiff_argnums=(2,))
def _mmh(a, b, mode):
    ah, al = _split(a)
    bh, bl = _split(b)
    return _dot(ah, bh, mode) + (_dot(ah, bl, mode) + _dot(al, bh, mode))


def _mmh_fwd(a, b, mode):
    return _mmh(a, b, mode), (a, b)


def _mmh_bwd(mode, res, g):
    return _mm_grads(_mmh, mode, res[0], res[1], g)


_mmh.defvjp(_mmh_fwd, _mmh_bwd)


def _dot_sel(sel, x, mode):
    s = sel.astype(BF16)
    x1 = x.astype(BF16)
    x2, x3 = _split(x - x1.astype(F32))
    return _dot(s, x1, mode) + (_dot(s, x2, mode) + _dot(s, x3, mode))


@jax.custom_vjp
def _mms(sel, x):
    return _dot_sel(sel, x, "nn")


def _mms_fwd(sel, x):
    return _dot_sel(sel, x, "nn"), sel


def _mms_bwd(sel, g):
    return jnp.zeros_like(sel), _dot_sel(sel, g, "tn")


_mms.defvjp(_mms_fwd, _mms_bwd)


def _softplus(z):
    return jnp.maximum(z, 0.0) + jnp.log(1.0 + jnp.exp(-jnp.abs(z)))


def _log_sigmoid(z):
    return jnp.minimum(z, 0.0) - jnp.log(1.0 + jnp.exp(-jnp.abs(z)))


def _silu(z):
    return z * jax.nn.sigmoid(z)


def _iota(shape, dim):
    return lax.broadcasted_iota(jnp.int32, shape, dim)


def _inv_unit_lower_raw(n):
    c = n.shape[-1]
    ri, ci = _iota((c, c), 0), _iota((c, c), 1)
    eye = (ri == ci).astype(F32)
    dmask = (ri // 16) == (ci // 16)
    dpart = jnp.where(dmask, n, 0.0)
    lpart = n - dpart
    x = -dpart
    p = eye + x
    for _ in range(3):
        x = _mmh(x, x, "nn")
        p = p + _mmh(p, x, "nn")
    m = -_mmh(p, lpart, "nn")
    q = eye + m
    steps = 1
    while (1 << steps) < c // 16:
        steps += 1
    for _ in range(steps - 1):
        m = _mmh(m, m, "nn")
        q = q + _mmh(q, m, "nn")
    return _mmh(q, p, "nn")


@jax.custom_vjp
def _inv_given(n, t):
    return t


def _inv_given_fwd(n, t):
    return t, t


def _inv_given_bwd(t, g):
    c = t.shape[-1]
    strict = _iota((c, c), 0) > _iota((c, c), 1)
    d = -_mmh(_mmh(t, g, "tn"), t, "nt")
    return jnp.where(strict, d, 0.0), jnp.zeros_like(t)


_inv_given.defvjp(_inv_given_fwd, _inv_given_bwd)


def _shift_down(x, halo, s):
    if s == 0:
        return x
    xs = pltpu.roll(x, s, 0)
    hs = pltpu.roll(halo, s, 0)
    top = jnp.where(_iota(hs.shape, 0) < s, hs, xs[0:8])
    return jnp.concatenate([top, xs[8:]], axis=0)


def _shift_up(x, halo, s):
    if s == 0:
        return x
    tm = x.shape[0]
    xs = pltpu.roll(x, tm - s, 0)
    hs = pltpu.roll(halo, 8 - s, 0)
    bot = jnp.where(_iota(hs.shape, 0) >= 8 - s, hs, xs[tm - 8:])
    return jnp.concatenate([xs[:tm - 8], bot], axis=0)


def _causal_conv(x, halo, w):
    kk = w.shape[0]
    y = x * w[kk - 1:kk, :]
    for k in range(kk - 1):
        y = y + _shift_down(x, halo, kk - 1 - k) * w[k:k + 1, :]
    return y


def _head_scale(x, width, fn):
    outs = []
    for h in range(x.shape[1] // width):
        seg = x[:, h * width:(h + 1) * width]
        outs.append(seg * fn(jnp.sum(seg * seg, axis=1, keepdims=True)))
    return jnp.concatenate(outs, axis=1)


def _matmul(a, b, mode, out_dtype, add=None, name="mm"):
    if mode == "nn":
        (m, k), n = a.shape, b.shape[1]
    elif mode == "nt":
        (m, k), n = a.shape, b.shape[0]
    else:
        (k, m), n = a.shape, b.shape[1]
    tm = _pick(m, (1408, 1024, 512, 256, 128) if mode == "tn" else (640, 512, 256, 128))
    tn = _pick(n, (1536, 1408, 1024, 768, 512, 256, 128))
    sa, sb = a.dtype.itemsize, b.dtype.itemsize
    fixed = tm * tn * 4 * (3 + (2 if add is not None else 0))
    tk = 128
    for cand in (k, 2816, 2560, 1664, 1536, 1280, 1024, 832, 768, 640, 512, 256, 128):
        if mode != "tn" and cand != k and cand % LANE:
            continue
        if k % cand == 0 and fixed + 2 * cand * (tm * sa + tn * sb) <= MATMUL_VMEM_BUDGET:
            tk = cand
            break
    nk = k // tk
    a_spec = {"nn": pl.BlockSpec((tm, tk), lambda i, j, kk: (i, kk)),
              "nt": pl.BlockSpec((tm, tk), lambda i, j, kk: (i, kk)),
              "tn": pl.BlockSpec((tk, tm), lambda i, j, kk: (kk, i))}[mode]
    b_spec = {"nn": pl.BlockSpec((tk, tn), lambda i, j, kk: (kk, j)),
              "nt": pl.BlockSpec((tn, tk), lambda i, j, kk: (j, kk)),
              "tn": pl.BlockSpec((tk, tn), lambda i, j, kk: (kk, j))}[mode]
    o_spec = pl.BlockSpec((tm, tn), lambda i, j, kk: (i, j))
    has_add = add is not None

    def body(*refs):
        a_ref, b_ref = refs[0], refs[1]
        add_ref = refs[2] if has_add else None
        o_ref = refs[3] if has_add else refs[2]
        part = _dot(a_ref[...].astype(BF16), b_ref[...].astype(BF16), mode)
        if nk == 1:
            if has_add:
                part = part + add_ref[...].astype(F32)
            o_ref[...] = part.astype(out_dtype)
        else:
            acc = refs[-1]
            kk = pl.program_id(2)

            @pl.when(kk == 0)
            def _():
                acc[...] = part

            @pl.when(kk > 0)
            def _():
                acc[...] += part

            @pl.when(kk == nk - 1)
            def _():
                r = acc[...]
                if has_add:
                    r = r + add_ref[...].astype(F32)
                o_ref[...] = r.astype(out_dtype)

    ins = [a, b] + ([add] if has_add else [])
    specs = [a_spec, b_spec] + ([o_spec] if has_add else [])
    return _call(body, name=name, out_shape=jax.ShapeDtypeStruct((m, n), out_dtype), grid=(m // tm, n // tn, nk),
                 in_specs=specs, out_specs=o_spec,
                 scratch=[pltpu.VMEM((tm, tn), F32)] if nk > 1 else [],
                 sem=("parallel", "parallel", "arbitrary"))(*ins)


def _row_spec(width, colblock, tm):
    return pl.BlockSpec((tm, width), lambda i, cb=colblock: (i, cb))


def _full_spec(arr):
    nd = arr.ndim
    return pl.BlockSpec(arr.shape, lambda i, nd=nd: (0,) * nd)


def _rowwise(fn, rows, params, outs, name):
    lp = rows[0][0].shape[0]
    tm = _pick(lp, ROW_TILES_WIDE)
    nr, npar = len(rows), len(params)

    def body(*refs):
        row0 = pl.program_id(0) * tm
        vals = [r[...].astype(F32) for r in refs[:nr + npar]]
        res = fn(*vals, row0)
        for o_ref, r in zip(refs[nr + npar:], res):
            o_ref[...] = r.astype(o_ref.dtype)

    out = _call(body, name=name, grid=(lp // tm,),
                out_shape=[jax.ShapeDtypeStruct((lp, w), dt) for w, dt in outs],
                in_specs=[_row_spec(w, cb, tm) for _, w, cb in rows] + [_full_spec(p) for p in params],
                out_specs=[_row_spec(w, 0, tm) for w, _ in outs], sem=("parallel",))(
                    *[r[0] for r in rows], *params)
    return out


def _rowwise_bwd(fn, rows, params, cts, name, pad_rows, grad_dtypes, adds=None):
    lp = rows[0][0].shape[0]
    tm = _pick(lp, ROW_TILES)
    nr, npar, nct = len(rows), len(params), len(cts)
    adds = adds or [None] * nr
    add_list = [a for a in adds if a is not None]
    nadd = len(add_list)

    def body(*refs):
        i = pl.program_id(0)
        row0 = i * tm
        vals = [r[...].astype(F32) for r in refs[:nr + npar]]
        ct_vals = tuple(r[...].astype(F32) for r in refs[nr + npar:nr + npar + nct])
        add_refs = list(refs[nr + npar + nct:nr + npar + nct + nadd])
        outs = refs[nr + npar + nct + nadd:]
        _, vjp = jax.vjp(lambda *args: tuple(fn(*args, row0)), *vals)
        grads = vjp(ct_vals)
        valid = (row0 + _iota((tm, 1), 0)) >= pad_rows
        for idx in range(nr):
            g = jnp.where(valid, grads[idx], 0.0)
            if adds[idx] is not None:
                g = g + add_refs.pop(0)[...].astype(F32)
            outs[idx][...] = g.astype(outs[idx].dtype)
        for idx in range(npar):
            o_ref = outs[nr + idx]

            @pl.when(i == 0)
            def _(o_ref=o_ref):
                o_ref[...] = jnp.zeros_like(o_ref)

            o_ref[...] += grads[nr + idx]

    out = _call(body, name=name, grid=(lp // tm,),
                out_shape=[jax.ShapeDtypeStruct((lp, w), dt) for (_, w, _), dt in zip(rows, grad_dtypes)]
                + [jax.ShapeDtypeStruct(p.shape, F32) for p in params],
                in_specs=[_row_spec(w, cb, tm) for _, w, cb in rows] + [_full_spec(p) for p in params]
                + [_row_spec(w, cb, tm) for _, w, cb in cts] + [_row_spec(w, cb, tm) for _, w, cb in add_list],
                out_specs=[_row_spec(w, 0, tm) for _, w, _ in rows] + [_full_spec(p) for p in params],
                sem=("arbitrary",))(*[r[0] for r in rows], *params, *[c[0] for c in cts], *[a[0] for a in add_list])
    return out[:nr], out[nr:]


def _rmsnorm_fn(x, g, row0):
    return (x * lax.rsqrt(jnp.mean(x * x, axis=1, keepdims=True) + EPS) * g,)


def _fox_prep_fn(pad_rows, fq, fk, small, qg, kg, fb, row0):
    ri, ci = _iota((FOX_W, FOX_W), 0), _iota((FOX_W, FOX_W), 1)
    bd = jnp.where((ri // FOX_DH) == (ci // FOX_DH), 1.0 / FOX_DH, 0.0)

    def hn(x, g):
        return x * lax.rsqrt(_mmh(x * x, bd, "nn") + EPS) * g

    tm = small.shape[0]
    keep = (_iota((tm, LANE), 1) < 8) & ((row0 + _iota((tm, LANE), 0)) >= pad_rows)
    logf = jnp.where(keep, _log_sigmoid(small + fb), 0.0)
    return hn(fq, qg) * (FOX_DH ** -0.5), hn(fk, kg), logf


def _gdn_act_fn(cq, ck, cv, small, alog, dtb):
    tm = small.shape[0]
    q = _head_scale(_silu(cq), GDN_DH, lambda s: lax.rsqrt(s + EPS) * (GDN_DH ** -0.5))
    k = _head_scale(_silu(ck), GDN_DH, lambda s: lax.rsqrt(s + EPS))
    v = _silu(cv)
    lane = _iota((tm, LANE), 1)
    beta = jnp.where((lane >= 8) & (lane < 16), jax.nn.sigmoid(small), 0.0)
    g = jnp.where((lane >= 16) & (lane < 24), -jnp.exp(alog) * _softplus(small + dtb), 0.0)
    ri, ci = _iota((tm, tm), 0), _iota((tm, tm), 1)
    tri = jnp.where(((ri // CHUNK) == (ci // CHUNK)) & (ci <= ri), 1.0, 0.0)
    return q, k, v, beta + _mms(tri, g)


def _gdn_post_fn(o, gz, gn, row0):
    return (_head_scale(o, GDN_DH, lambda s: lax.rsqrt(s * (1.0 / GDN_DH) + EPS)) * gn * _silu(gz),)


def _merge_fn(g0, g1, ya, yb, row0):
    return (jax.nn.sigmoid(g0) * ya + jax.nn.sigmoid(g1) * yb,)


def _cumsum_rows(x, reverse, name):
    lp, w = x.shape
    tm = _pick(lp, (640, 512, 256, 128))
    nt = lp // tm

    def body(x_ref, o_ref, carry):
        i = pl.program_id(0)

        @pl.when(i == 0)
        def _():
            carry[...] = jnp.zeros_like(carry)

        ri, ci = _iota((tm, tm), 0), _iota((tm, tm), 1)
        tri = jnp.where((ci >= ri) if reverse else (ci <= ri), 1.0, 0.0)
        blk = x_ref[...]
        o_ref[...] = _dot_sel(tri, blk, "nn") + carry[0:1, :]
        carry[...] = carry[...] + jnp.sum(blk, axis=0, keepdims=True)

    idx = (lambda i: (nt - 1 - i, 0)) if reverse else (lambda i: (i, 0))
    return _call(body, name=name, grid=(nt,), out_shape=jax.ShapeDtypeStruct((lp, w), F32),
                 in_specs=[pl.BlockSpec((tm, w), idx)], out_specs=pl.BlockSpec((tm, w), idx),
                 scratch=[pltpu.VMEM((8, w), F32)], sem=("arbitrary",))(x)


def _fox_scores(q, k, fq, fk, hh, qpos0, kpos0, pad_rows, masked):
    tq, tk = q.shape[0], k.shape[0]
    lane = _iota(q.shape, 1)
    sel = (lane < FOX_DH) if hh == 0 else (lane >= FOX_DH)
    s = _dot(jnp.where(sel, q, jnp.zeros_like(q)), k, "nt") + fq - fk
    if not masked:
        return s, None, sel
    qpos = qpos0 + _iota((tq, tk), 0)
    kpos = kpos0 + _iota((tq, tk), 1)
    mask = (kpos <= qpos) & (kpos >= pad_rows)
    return jnp.where(mask, s, NEG), mask, sel


def _probs(s, mask, shift):
    p = jnp.exp(s - shift)
    return p if mask is None else jnp.where(mask, p, 0.0)


def _both_variants(needs_mask, fn):
    @pl.when(needs_mask)
    def _():
        fn(True)

    @pl.when(jnp.logical_not(needs_mask))
    def _():
        fn(False)


def _lane_col(blk, lane_idx):
    return jnp.sum(jnp.where(_iota(blk.shape, 1) == lane_idx, blk, 0.0), axis=1, keepdims=True)


def _to_lanes(cols, width=LANE):
    lane = _iota((cols[0].shape[0], width), 1)
    out = jnp.zeros((cols[0].shape[0], width), F32)
    for idx, c in enumerate(cols):
        out = jnp.where(lane == idx, c, out)
    return out


def _fox_fwd(q, k, v, fsum, frow, pad_rows, v_col):
    lp = q.shape[0]
    t = _pick(lp, (640, 512, 256, 128))
    n = lp // t

    def body(q_ref, k_ref, v_ref, f_ref, fk_ref, o_ref, lse_ref, acc, m_s, l_s, fq_s):
        pr, i, j = pl.program_id(0), pl.program_id(1), pl.program_id(2)

        @pl.when(j == 0)
        def _():
            acc[...] = jnp.zeros_like(acc)
            m_s[...] = jnp.full_like(m_s, NEG)
            l_s[...] = jnp.zeros_like(l_s)
            for hh in range(2):
                fq_s[hh] = _lane_col(f_ref[...], 2 * pr + hh)

        def step(masked):
            for hh in range(2):
                s, mask, _ = _fox_scores(q_ref[...], k_ref[...], fq_s[hh], fk_ref[hh], hh, i * t, j * t, pad_rows,
                                         masked)
                m_prev = m_s[hh]
                m_new = jnp.maximum(m_prev, jnp.max(s, axis=1, keepdims=True))
                p = _probs(s, mask, m_new)
                alpha = jnp.exp(m_prev - m_new)
                l_s[hh] = alpha * l_s[hh] + jnp.sum(p, axis=1, keepdims=True)
                acc[hh] = alpha * acc[hh] + _dot(p.astype(BF16), v_ref[...].astype(BF16), "nn")
                m_s[hh] = m_new

        @pl.when(j <= i)
        def _():
            _both_variants((j == i) | (j == 0), step)

        @pl.when(j == i)
        def _():
            outs, lses = [], []
            for hh in range(2):
                l = l_s[hh]
                ok = l > 0.0
                outs.append(acc[hh] * jnp.where(ok, 1.0 / jnp.where(ok, l, 1.0), 0.0))
                lses.append(jnp.where(ok, m_s[hh] + jnp.log(jnp.where(ok, l, 1.0)), 0.0))
            lane = _iota((t, LANE), 1)
            o_ref[...] = jnp.where(lane < FOX_DH, outs[0], outs[1]).astype(o_ref.dtype)
            lse_ref[...] = _to_lanes(lses)

    qspec = pl.BlockSpec((t, LANE), lambda p, i, j: (i, p))
    kspec = pl.BlockSpec((t, LANE), lambda p, i, j: (jnp.minimum(j, i), p))
    vspec = pl.BlockSpec((t, LANE), lambda p, i, j: (jnp.minimum(j, i), v_col + p))
    fspec = pl.BlockSpec((t, LANE), lambda p, i, j: (i, 0))
    rspec = pl.BlockSpec((2, 1, t), lambda p, i, j: (p, 0, jnp.minimum(j, i)))
    return _call(body, name="fox_fwd", grid=(FOX_W // LANE, n, n),
                 out_shape=[jax.ShapeDtypeStruct((lp, FOX_W), BF16), jax.ShapeDtypeStruct((lp, FOX_W), F32)],
                 in_specs=[qspec, kspec, vspec, fspec, rspec], out_specs=[qspec, qspec],
                 scratch=[pltpu.VMEM((2, t, LANE), F32), pltpu.VMEM((2, t, 1), F32), pltpu.VMEM((2, t, 1), F32),
                          pltpu.VMEM((2, t, 1), F32)],
                 sem=("parallel", "parallel", "arbitrary"))(q, k, v, fsum, frow)


def _fox_delta_fn(o, do, row0):
    ri, ci = _iota((FOX_W, LANE), 0), _iota((FOX_W, LANE), 1)
    sel = jnp.where((ri // FOX_DH) == ci, 1.0, 0.0).astype(BF16)
    x = o * do
    x1 = x.astype(BF16)
    x2, x3 = _split(x - x1.astype(F32))
    return (_dot(x1, sel, "nn") + (_dot(x2, sel, "nn") + _dot(x3, sel, "nn")),)


def _fox_bwd(q, k, v, fsum, frow, do, lse, delta, pad_rows, v_col):
    lp = q.shape[0]
    t = _pick(lp, (640, 512, 256, 128))
    n = lp // t

    def body(q_ref, k_ref, v_ref, f_ref, fk_ref, do_ref, lse_ref, dl_ref,
             dq_ref, dk_ref, dv_ref, dfq_ref, dfk_ref, dka, dva, dfa):
        pr, j, i = pl.program_id(0), pl.program_id(1), pl.program_id(2)
        lane = _iota((t, LANE), 1)

        @pl.when((j == 0) & (i == 0))
        def _():
            dq_ref[...] = jnp.zeros_like(dq_ref)
            dfq_ref[...] = jnp.zeros_like(dfq_ref)

        @pl.when(i == 0)
        def _():
            dka[...] = jnp.zeros_like(dka)
            dva[...] = jnp.zeros_like(dva)
            dfa[...] = jnp.zeros_like(dfa)

        def step(masked):
            rows = pl.ds(pl.multiple_of(i * t, t), t)
            dq_add = jnp.zeros((t, LANE), F32)
            rowsums = []
            for hh in range(2):
                fq = _lane_col(f_ref[...], 2 * pr + hh)
                s, mask, sel = _fox_scores(q_ref[...], k_ref[...], fq, fk_ref[hh], hh, i * t, j * t, pad_rows, masked)
                p = _probs(s, mask, _lane_col(lse_ref[...], hh))
                dop = jnp.where(sel, do_ref[...], jnp.zeros_like(do_ref[...]))
                ds = p * (_dot(dop, v_ref[...].astype(BF16), "nt") - _lane_col(dl_ref[...], 2 * pr + hh))
                dsb = ds.astype(BF16)
                dva[hh] += _dot(p.astype(BF16), do_ref[...], "tn")
                dka[hh] += _dot(dsb, q_ref[...], "tn")
                dfa[hh] -= jnp.sum(ds, axis=0, keepdims=True)
                dq_add = dq_add + _dot(dsb, jnp.where(sel, k_ref[...], jnp.zeros_like(k_ref[...])), "nn")
                rowsums.append(jnp.sum(ds, axis=1, keepdims=True))
            dq_ref[rows, :] += dq_add
            dfq_ref[rows, :] += _to_lanes(rowsums)

        @pl.when(i >= j)
        def _():
            _both_variants((j == i) | (j == 0), step)

        @pl.when(i == n - 1)
        def _():
            dk_ref[...] = jnp.where(lane < FOX_DH, dka[0], dka[1])
            dv_ref[...] = jnp.where(lane < FOX_DH, dva[0], dva[1]).astype(dv_ref.dtype)
            dfk_ref[...] = dfa[...]

    qspec = pl.BlockSpec((t, LANE), lambda p, j, i: (jnp.maximum(i, j), p))
    f_q = pl.BlockSpec((t, LANE), lambda p, j, i: (jnp.maximum(i, j), 0))
    kspec = pl.BlockSpec((t, LANE), lambda p, j, i: (j, p))
    vspec = pl.BlockSpec((t, LANE), lambda p, j, i: (j, v_col + p))
    rspec = pl.BlockSpec((2, 1, t), lambda p, j, i: (p, 0, j))
    whole = pl.BlockSpec((lp, LANE), lambda p, j, i: (0, p))
    wide = jax.ShapeDtypeStruct((lp, FOX_W), F32)
    return _call(body, name="fox_bwd", grid=(FOX_W // LANE, n, n),
                 out_shape=[wide, wide, jax.ShapeDtypeStruct((lp, FOX_W), BF16), wide,
                            jax.ShapeDtypeStruct((8, 1, lp), F32)],
                 in_specs=[qspec, kspec, vspec, f_q, rspec, qspec, qspec, f_q],
                 out_specs=[whole, kspec, kspec, whole, rspec],
                 scratch=[pltpu.VMEM((2, t, LANE), F32), pltpu.VMEM((2, t, LANE), F32), pltpu.VMEM((2, 1, t), F32)],
                 sem=("parallel", "arbitrary", "arbitrary"))(q, k, v, fsum, frow, do, lse, delta)


def _gdn_chunk(q, k, v, beta, gcol, grow, s, inv):
    c = q.shape[-2]
    ri, ci = _iota((c, c), 0), _iota((c, c), 1)
    dec = jnp.exp(jnp.where(ri >= ci, gcol - grow, NEG))
    dec_strict = jnp.where(ri > ci, dec, 0.0)
    eg = jnp.exp(gcol)
    kb = k * beta
    t = inv(_mmb(kb, k, "nt") * dec_strict)
    u_hat = _mmh(t, v * beta, "nn")
    w = _mmh(t, kb * eg, "nn")
    u = u_hat - _mmb(w, s, "nn")
    o = _mmb(q * eg, s, "nn") + _mmb(_mmb(q, k, "nt") * dec, u, "nn")
    glast = jnp.sum(jnp.where(_iota((1, c), 1) == c - 1, grow, 0.0), axis=-1, keepdims=True)
    s_new = s * jnp.exp(glast) + _mmb(k * jnp.exp(glast - gcol), u, "tn")
    return o, s_new


def _gdn_specs(lp, reverse):
    n = lp // CHUNK
    pos = (lambda c: n - 1 - c) if reverse else (lambda c: c)
    wide = pl.BlockSpec((CHUNK, GDN_W), lambda c: (pos(c), 0))
    lanes = pl.BlockSpec((CHUNK, LANE), lambda c: (pos(c), 0))
    row = pl.BlockSpec((GDN_H, 1, 1, CHUNK), lambda c: (0, pos(c), 0, 0))
    st = pl.BlockSpec((GDN_H, 1, GDN_DH, GDN_DH), lambda c: (0, pos(c), 0, 0))
    return n, wide, lanes, row, st


def _heads(ref):
    return jnp.stack([ref[:, h * GDN_DH:(h + 1) * GDN_DH] for h in range(GDN_H)])


def _put_heads(ref, val):
    for h in range(GDN_H):
        ref[:, h * GDN_DH:(h + 1) * GDN_DH] = val[h]


def _head_cols(blk, lane0):
    return jnp.stack([_lane_col(blk, lane0 + h) for h in range(GDN_H)])


def _gdn_fwd(q, k, v, bg, grow):
    lp = q.shape[0]
    n, wide, lanes, row, st = _gdn_specs(lp, False)

    def body(q_ref, k_ref, v_ref, bg_ref, gr_ref, o_ref, sp_ref, t_ref, s_scr):
        @pl.when(pl.program_id(0) == 0)
        def _():
            s_scr[...] = jnp.zeros_like(s_scr)

        def inv(m):
            t = _inv_unit_lower_raw(m)
            t_ref[:, 0] = t
            return t

        s = s_scr[...]
        sp_ref[:, 0] = s
        bg_blk = bg_ref[...]
        o, s_new = _gdn_chunk(_heads(q_ref), _heads(k_ref), _heads(v_ref), _head_cols(bg_blk, 8),
                              _head_cols(bg_blk, 16), gr_ref[:, 0], s, inv)
        _put_heads(o_ref, o)
        s_scr[...] = s_new

    tri = pl.BlockSpec((GDN_H, 1, CHUNK, CHUNK), lambda c: (0, c, 0, 0))
    return _call(body, name="gdn_fwd", grid=(n,),
                 out_shape=[jax.ShapeDtypeStruct((lp, GDN_W), F32),
                            jax.ShapeDtypeStruct((GDN_H, n, GDN_DH, GDN_DH), F32),
                            jax.ShapeDtypeStruct((GDN_H, n, CHUNK, CHUNK), F32)],
                 in_specs=[wide, wide, wide, lanes, row], out_specs=[wide, st, tri],
                 scratch=[pltpu.VMEM((GDN_H, GDN_DH, GDN_DH), F32)], sem=("arbitrary",))(q, k, v, bg, grow)


def _gdn_bwd(q, k, v, bg, grow, sprev, tinv, do):
    lp = q.shape[0]
    n, wide, lanes, row, st = _gdn_specs(lp, True)

    def body(q_ref, k_ref, v_ref, bg_ref, gr_ref, sp_ref, t_ref, do_ref,
             dq_ref, dk_ref, dv_ref, dbg_ref, dgr_ref, ds_scr):
        @pl.when(pl.program_id(0) == 0)
        def _():
            ds_scr[...] = jnp.zeros_like(ds_scr)

        t_saved = t_ref[:, 0]
        fn = functools.partial(_gdn_chunk, inv=lambda m: _inv_given(m, t_saved))
        bg_blk = bg_ref[...]
        _, vjp = jax.vjp(fn, _heads(q_ref), _heads(k_ref), _heads(v_ref), _head_cols(bg_blk, 8),
                         _head_cols(bg_blk, 16), gr_ref[:, 0], sp_ref[:, 0])
        dq, dk, dv, db, dgc, dgr, ds = vjp((_heads(do_ref), ds_scr[...]))
        _put_heads(dq_ref, dq)
        _put_heads(dk_ref, dk)
        _put_heads(dv_ref, dv)
        lane = _iota((CHUNK, LANE), 1)
        dbg = jnp.zeros((CHUNK, LANE), F32)
        for h in range(GDN_H):
            dbg = jnp.where(lane == 8 + h, db[h], jnp.where(lane == 16 + h, dgc[h], dbg))
        dbg_ref[...] = dbg
        dgr_ref[:, 0] = dgr
        ds_scr[...] = ds

    wshape = jax.ShapeDtypeStruct((lp, GDN_W), F32)
    tri = pl.BlockSpec((GDN_H, 1, CHUNK, CHUNK), lambda c: (0, n - 1 - c, 0, 0))
    return _call(body, name="gdn_bwd", grid=(n,),
                 out_shape=[wshape, wshape, wshape, jax.ShapeDtypeStruct((lp, LANE), F32),
                            jax.ShapeDtypeStruct((GDN_H, n, 1, CHUNK), F32)],
                 in_specs=[wide, wide, wide, lanes, row, st, tri, wide], out_specs=[wide, wide, wide, lanes, row],
                 scratch=[pltpu.VMEM((GDN_H, GDN_DH, GDN_DH), F32)], sem=("arbitrary",))(
                     q, k, v, bg, grow, sprev, tinv, do)


def _halo_prev(width, colblock, tm):
    return pl.BlockSpec((8, width), lambda i, cb=colblock: (jnp.maximum(i * (tm // 8) - 1, 0), cb))


def _gdn_act(proj, small, conv_w, alog_row, dtb_row):
    lp = proj.shape[0]
    tm = _pick(lp, ROW_TILES)

    def body(xq, xk, xv, hq, hk, hv, wq, wk, wv, sm, al, dt, q_ref, k_ref, v_ref, bg_ref):
        first = (pl.program_id(0) > 0).astype(F32)
        cs = [_causal_conv(x[...], h[...] * first, w[...]) for x, h, w in ((xq, hq, wq), (xk, hk, wk), (xv, hv, wv))]
        q, k, v, bg = _gdn_act_fn(cs[0], cs[1], cs[2], sm[...], al[...], dt[...])
        q_ref[...], k_ref[...], v_ref[...], bg_ref[...] = q, k, v, bg

    wide = jax.ShapeDtypeStruct((lp, GDN_W), F32)
    wspec = [pl.BlockSpec((4, GDN_W), lambda i, c=c: (0, c)) for c in range(3)]
    return _call(body, name="gdn_act", grid=(lp // tm,),
                 out_shape=[wide, wide, wide, jax.ShapeDtypeStruct((lp, LANE), F32)],
                 in_specs=[_row_spec(GDN_W, c, tm) for c in range(3)] + [_halo_prev(GDN_W, c, tm) for c in range(3)]
                 + wspec + [_row_spec(LANE, 0, tm), _full_spec(alog_row), _full_spec(dtb_row)],
                 out_specs=[_row_spec(GDN_W, 0, tm)] * 3 + [_row_spec(LANE, 0, tm)], sem=("parallel",))(
                     proj, proj, proj, proj, proj, proj, conv_w, conv_w, conv_w, small, alog_row, dtb_row)


def _gdn_act_bwd(proj, small, conv_w, alog_row, dtb_row, dq, dk, dv, dbg):
    lp = proj.shape[0]
    tm = ROW_TILE

    def body(xq, xk, xv, hq, hk, hv, wq, wk, wv, sm, al, dt, dq_r, dk_r, dv_r, dbg_r,
             dc_ref, dsm_ref, dal_ref, ddt_ref, dw_ref):
        i = pl.program_id(0)
        first = (i > 0).astype(F32)
        xs = [(x[...], h[...] * first, w[...]) for x, h, w in ((xq, hq, wq), (xk, hk, wk), (xv, hv, wv))]
        cs = [_causal_conv(*t) for t in xs]
        _, vjp = jax.vjp(_gdn_act_fn, cs[0], cs[1], cs[2], sm[...], al[...], dt[...])
        dcq, dck, dcv, dsm, dal, ddt = vjp((dq_r[...], dk_r[...], dv_r[...], dbg_r[...]))
        dsm_ref[...] = dsm

        @pl.when(i == 0)
        def _():
            dal_ref[...] = jnp.zeros_like(dal_ref)
            ddt_ref[...] = jnp.zeros_like(ddt_ref)
            dw_ref[...] = jnp.zeros_like(dw_ref)

        dal_ref[...] += dal
        ddt_ref[...] += ddt
        for c, (dc, (x, h, w)) in enumerate(zip((dcq, dck, dcv), xs)):
            dc_ref[:, c * GDN_W:(c + 1) * GDN_W] = dc
            rows = [jnp.sum(_shift_down(x, h, 3 - kk) * dc, axis=0, keepdims=True) for kk in range(4)]
            dw_ref[:, c * GDN_W:(c + 1) * GDN_W] += jnp.concatenate(rows, axis=0)

    wspec = [pl.BlockSpec((4, GDN_W), lambda i, c=c: (0, c)) for c in range(3)]
    row128 = jax.ShapeDtypeStruct((1, LANE), F32)
    return _call(body, name="gdn_act_bwd", grid=(lp // tm,),
                 out_shape=[jax.ShapeDtypeStruct((lp, 3 * GDN_W), F32), jax.ShapeDtypeStruct((lp, LANE), F32),
                            row128, row128, jax.ShapeDtypeStruct((4, 3 * GDN_W), F32)],
                 in_specs=[_row_spec(GDN_W, c, tm) for c in range(3)] + [_halo_prev(GDN_W, c, tm) for c in range(3)]
                 + wspec + [_row_spec(LANE, 0, tm), _full_spec(alog_row), _full_spec(dtb_row)]
                 + [_row_spec(GDN_W, 0, tm)] * 3 + [_row_spec(LANE, 0, tm)],
                 out_specs=[_row_spec(3 * GDN_W, 0, tm), _row_spec(LANE, 0, tm),
                            _full_spec(alog_row), _full_spec(dtb_row), pl.BlockSpec((4, 3 * GDN_W), lambda i: (0, 0))],
                 sem=("arbitrary",))(proj, proj, proj, proj, proj, proj, conv_w, conv_w, conv_w, small,
                                     alog_row, dtb_row, dq, dk, dv, dbg)


def _ffn_act(up_pre, conv_w):
    lp = up_pre.shape[0]
    tm = _pick(lp, ROW_TILES)

    def body(xg, xv, hg, hv, wg, wv, a_ref):
        first = (pl.program_id(0) > 0).astype(F32)
        ug = _causal_conv(xg[...], hg[...] * first, wg[...])
        uv = _causal_conv(xv[...], hv[...] * first, wv[...])
        a_ref[...] = (_silu(ug) * uv).astype(a_ref.dtype)

    wspec = [pl.BlockSpec((3, D_FF), lambda i, c=c: (0, c)) for c in range(2)]
    return _call(body, name="ffn_act", grid=(lp // tm,), out_shape=jax.ShapeDtypeStruct((lp, D_FF), BF16),
                 in_specs=[_row_spec(D_FF, c, tm) for c in range(2)] + [_halo_prev(D_FF, c, tm) for c in range(2)] + wspec,
                 out_specs=_row_spec(D_FF, 0, tm), sem=("parallel",))(up_pre, up_pre, up_pre, up_pre, conv_w, conv_w)


def _ffn_act_bwd(up_pre, conv_w, dact, tm=ROW_TILE):
    lp = up_pre.shape[0]

    def body(xg, xv, hg, hv, wg, wv, da, du_ref, dw_ref):
        i = pl.program_id(0)
        first = (i > 0).astype(F32)
        xs = [(x[...], h[...] * first, w[...]) for x, h, w in ((xg, hg, wg), (xv, hv, wv))]
        ug, uv = [_causal_conv(*t) for t in xs]
        _, vjp = jax.vjp(lambda a, b: _silu(a) * b, ug, uv)
        dus = vjp(da[...].astype(F32))

        @pl.when(i == 0)
        def _():
            dw_ref[...] = jnp.zeros_like(dw_ref)

        for c, (du, (x, h, w)) in enumerate(zip(dus, xs)):
            du_ref[:, c * D_FF:(c + 1) * D_FF] = du
            rows = [jnp.sum(_shift_down(x, h, 2 - kk) * du, axis=0, keepdims=True) for kk in range(3)]
            dw_ref[:, c * D_FF:(c + 1) * D_FF] += jnp.concatenate(rows, axis=0)

    wspec = [pl.BlockSpec((3, D_FF), lambda i, c=c: (0, c)) for c in range(2)]
    return _call(body, name="ffn_act_bwd", grid=(lp // tm,),
                 out_shape=[jax.ShapeDtypeStruct((lp, 2 * D_FF), F32), jax.ShapeDtypeStruct((3, 2 * D_FF), F32)],
                 in_specs=[_row_spec(D_FF, c, tm) for c in range(2)] + [_halo_prev(D_FF, c, tm) for c in range(2)]
                 + wspec + [_row_spec(D_FF, 0, tm)],
                 out_specs=[_row_spec(2 * D_FF, 0, tm), pl.BlockSpec((3, 2 * D_FF), lambda i: (0, 0))],
                 sem=("arbitrary",))(up_pre, up_pre, up_pre, up_pre, conv_w, conv_w, dact)


def _conv_bwd_x(dy, w, pad_rows, width, name):
    lp, ctot = dy.shape
    tm = _pick(lp, ROW_TILES)
    nt = lp // tm
    kk = w.shape[0]

    def body(d_ref, h_ref, w_ref, o_ref):
        i = pl.program_id(0)
        last = (i < nt - 1).astype(F32)
        d, h, wv = d_ref[...], h_ref[...] * last, w_ref[...]
        y = d * wv[kk - 1:kk, :]
        for k in range(kk - 1):
            y = y + _shift_up(d, h, kk - 1 - k) * wv[k:k + 1, :]
        valid = (i * tm + _iota((tm, 1), 0)) >= pad_rows
        o_ref[...] = jnp.where(valid, y, 0.0).astype(o_ref.dtype)

    return _call(body, name=name, grid=(nt, ctot // width), out_shape=jax.ShapeDtypeStruct((lp, ctot), BF16),
                 in_specs=[pl.BlockSpec((tm, width), lambda i, c: (i, c)),
                           pl.BlockSpec((8, width), lambda i, c: (jnp.minimum((i + 1) * (tm // 8), lp // 8 - 1), c)),
                           pl.BlockSpec((kk, width), lambda i, c: (0, c))],
                 out_specs=pl.BlockSpec((tm, width), lambda i, c: (i, c)), sem=("parallel", "parallel"))(dy, dy, w)


def _loss_head(h_res, target, row_start, tm=ROW_TILE):
    lp, d = h_res.shape
    t0 = row_start // tm

    def body(h_ref, t_ref, dy_ref, loss_ref):
        i = pl.program_id(0)

        @pl.when(i == 0)
        def _():
            loss_ref[...] = jnp.zeros_like(loss_ref)

        live = (i >= t0).astype(F32)
        err = (h_ref[...] - t_ref[...]) * live
        dy_ref[...] = err * (1.0 / d)
        loss_ref[...] += 0.5 / d * jnp.sum(err * err)

    return _call(body, name="loss_head", grid=(lp // tm,),
                 out_shape=[jax.ShapeDtypeStruct((lp, d), F32), jax.ShapeDtypeStruct((8, LANE), F32)],
                 in_specs=[pl.BlockSpec((tm, d), lambda i: (i, 0)),
                           pl.BlockSpec((tm, d), lambda i: (jnp.maximum(i - t0, 0), 0))],
                 out_specs=[pl.BlockSpec((tm, d), lambda i: (i, 0)), pl.BlockSpec((8, LANE), lambda i: (0, 0))],
                 sem=("arbitrary",))(h_res, target)


def _sum_adamw(parts, w, m, v, name):
    a, r, c = w.shape
    tm = _pick(r, (256, 128, 64, 32, 16))
    bc1 = 1.0 - ADAM_B1 ** ADAM_STEP
    bc2 = 1.0 - ADAM_B2 ** ADAM_STEP

    def body(p_ref, w_ref, m_ref, v_ref, g_ref, d_ref, nm_ref, nv_ref):
        g = p_ref[0, 0].astype(F32)
        for s in range(1, N_DEV):
            g = g + p_ref[s, 0].astype(F32)
        nm = ADAM_B1 * m_ref[0] + (1.0 - ADAM_B1) * g
        nv = ADAM_B2 * v_ref[0] + (1.0 - ADAM_B2) * (g * g)
        g_ref[0] = g
        nm_ref[0] = nm
        nv_ref[0] = nv
        d_ref[0] = -ADAM_LR * ((nm / bc1) / (jnp.sqrt(nv / bc2) + ADAM_EPS) + ADAM_WD * w_ref[0])

    spec = pl.BlockSpec((1, tm, c), lambda l, i: (l, i, 0))
    shp = jax.ShapeDtypeStruct((a, r, c), F32)
    return _call(body, name=name, grid=(a, r // tm), out_shape=[shp] * 4,
                 in_specs=[pl.BlockSpec((N_DEV, 1, tm, c), lambda l, i: (0, l, i, 0)), spec, spec, spec],
                 out_specs=[spec] * 4, sem=("parallel", "parallel"))(parts, w, m, v)


_ANY = pl.BlockSpec(memory_space=pl.ANY)


def _all_gather(block, name):
    def body(x_ref, out_ref, send_sems, recv_sems, local_sem):
        x, y, c = lax.axis_index("x"), lax.axis_index("y"), lax.axis_index("c")
        me, sibling = (x, y, c), (x, y, 1 - c)
        chips = [(1 - x, y), (x, 1 - y), (1 - x, 1 - y)]

        def slot(px, py, pc):
            return out_ref.at[4 * px + 2 * py + pc]

        def copy(k, blk, to, src=None):
            return pltpu.make_async_remote_copy(
                src_ref=slot(*blk) if src is None else src, dst_ref=slot(*blk),
                send_sem=send_sems.at[k], recv_sem=recv_sems.at[k], device_id=to, device_id_type=MESH)

        mine = pltpu.make_async_copy(x_ref, slot(*me), local_sem)
        mine.start()
        first = [copy(0, me, sibling, src=x_ref)]
        first += [copy(1 + j, me, (*chip, c), src=x_ref) for j, chip in enumerate(chips)]
        for cp in first:
            cp.start()
        passed = [copy(4 + j, (*chip, c), sibling) for j, chip in enumerate(chips)]
        for j, chip in enumerate(chips):
            copy(1 + j, (*chip, c), me).wait_recv()
            passed[j].start()
        copy(0, sibling, me).wait_recv()
        for j, chip in enumerate(chips):
            copy(4 + j, (*chip, 1 - c), me).wait_recv()
        for cp in first + passed:
            cp.wait_send()
        mine.wait()

    return pl.pallas_call(
        body, name=name, out_shape=jax.ShapeDtypeStruct((N_DEV,) + block.shape, block.dtype),
        in_specs=[_ANY], out_specs=_ANY,
        scratch_shapes=[pltpu.SemaphoreType.DMA((7,)), pltpu.SemaphoreType.DMA((7,)), pltpu.SemaphoreType.DMA],
    )(block)


def _all_to_all(src, name):
    def body(s_ref, o_ref, send_sems, recv_sems, local_sem):
        x, y, c = lax.axis_index("x"), lax.axis_index("y"), lax.axis_index("c")
        me = 4 * x + 2 * y + c
        mine = pltpu.make_async_copy(s_ref.at[me], o_ref.at[me], local_sem)
        mine.start()
        copies = []
        for k in range(1, N_DEV):
            px = 1 - x if k & 4 else x
            py = 1 - y if k & 2 else y
            pc = 1 - c if k & 1 else c
            peer = 4 * px + 2 * py + pc
            copies.append((pltpu.make_async_remote_copy(
                src_ref=s_ref.at[peer], dst_ref=o_ref.at[me], send_sem=send_sems.at[k - 1],
                recv_sem=recv_sems.at[k - 1], device_id=(px, py, pc), device_id_type=MESH), peer, k))
        for cp, _, _ in copies:
            cp.start()
        for cp, peer, k in copies:
            cp.wait_send()
            pltpu.make_async_remote_copy(
                src_ref=s_ref.at[peer], dst_ref=o_ref.at[peer], send_sem=send_sems.at[k - 1],
                recv_sem=recv_sems.at[k - 1], device_id=(x, y, c), device_id_type=MESH).wait_recv()
        mine.wait()

    return pl.pallas_call(
        body, name=name, out_shape=jax.ShapeDtypeStruct(src.shape, src.dtype), in_specs=[_ANY], out_specs=_ANY,
        scratch_shapes=[pltpu.SemaphoreType.DMA((7,)), pltpu.SemaphoreType.DMA((7,)), pltpu.SemaphoreType.DMA],
    )(src)


_HBM = pl.BlockSpec(memory_space=pltpu.HBM)
_SEM = pl.BlockSpec(memory_space=pltpu.SEMAPHORE)
_EFFECT = pltpu.SideEffectType.DATAFLOW_SIDE_EFFECTING


def _peer_list(x, y, c):
    return [(1 - x if k & 4 else x, 1 - y if k & 2 else y, 1 - c if k & 1 else c) for k in range(1, N_DEV)]


def _exchange_copies(s_refs, l_refs, send_sems, recv_sems, landing_of_peer):
    x, y, c = lax.axis_index("x"), lax.axis_index("y"), lax.axis_index("c")
    me = 4 * x + 2 * y + c
    out = []
    for wi, (s_ref, l_ref) in enumerate(zip(s_refs, l_refs)):
        for k, (px, py, pc) in enumerate(_peer_list(x, y, c)):
            peer = 4 * px + 2 * py + pc
            idx = wi * (N_DEV - 1) + k
            out.append(pltpu.make_async_remote_copy(
                src_ref=s_ref.at[peer], dst_ref=l_ref.at[peer if landing_of_peer else me],
                send_sem=send_sems.at[idx], recv_sem=recv_sems.at[idx], device_id=(px, py, pc), device_id_type=MESH))
    return out


def _exchange_start(srcs, name):
    nw = len(srcs)
    ncp = nw * (N_DEV - 1)

    def body(*refs):
        for cp in _exchange_copies(refs[:nw], refs[nw:2 * nw], refs[2 * nw], refs[2 * nw + 1], False):
            cp.start()
        refs[-1][...] = jnp.zeros_like(refs[-1])

    hbm = [pltpu.HBM(s.shape, s.dtype) for s in srcs]
    outs = pl.pallas_call(
        body, name=name,
        out_shape=(pltpu.SemaphoreType.DMA((ncp,)), pltpu.SemaphoreType.DMA((ncp,)), *hbm, *hbm,
                   jax.ShapeDtypeStruct((8, LANE), F32)),
        in_specs=[_HBM] * (2 * nw), out_specs=(_SEM, _SEM, *[_HBM] * (2 * nw), pl.BlockSpec(memory_space=pltpu.VMEM)),
        input_output_aliases={i: 2 + i for i in range(2 * nw)},
        compiler_params=pltpu.CompilerParams(has_side_effects=_EFFECT),
    )(*[pltpu.with_memory_space_constraint(s, pltpu.HBM) for s in srcs],
      *[pltpu.with_memory_space_constraint(lax.empty(s.shape, s.dtype), pltpu.HBM) for s in srcs])
    return outs[0], outs[1], list(outs[2:2 + nw]), list(outs[2 + nw:2 + 2 * nw]), outs[-1]


def _exchange_wait(send_sems, recv_sems, srcs, lands, after, name):
    nw = len(srcs)

    def body(*refs):
        for cp in _exchange_copies(refs[:nw], refs[nw:2 * nw], refs[2 * nw], refs[2 * nw + 1], True):
            cp.wait_send()
            cp.wait_recv()

    hbm = [pltpu.HBM(s.shape, s.dtype) for s in srcs]
    outs = pl.pallas_call(
        body, name=name, out_shape=(*hbm, *hbm),
        in_specs=[_HBM] * (2 * nw) + [_SEM, _SEM, pl.BlockSpec(memory_space=pl.ANY)], out_specs=tuple([_HBM] * (2 * nw)),
        input_output_aliases={i: i for i in range(2 * nw)},
        compiler_params=pltpu.CompilerParams(has_side_effects=_EFFECT),
    )(*srcs, *lands, send_sems, recv_sems, after)
    return list(outs[:nw]), list(outs[nw:])


def _pack(blocks, width, dtype, row_mult):
    flat = jnp.concatenate([b.astype(dtype).reshape(-1) for b in blocks])
    per = width * row_mult
    total = -(-flat.shape[0] // per) * per
    return jnp.pad(flat, (0, total - flat.shape[0])).reshape(total // width, width)


def _pack_dest(fulls, axes, width, dtype, row_mult):
    rows = []
    for f, ax in zip(fulls, axes):
        f = f.astype(dtype)
        if ax is None:
            rows.append(jnp.broadcast_to(f.reshape(1, -1), (N_DEV, f.size)))
        else:
            shp = f.shape
            f = f.reshape(shp[:ax] + (N_DEV, shp[ax] // N_DEV) + shp[ax + 1:])
            rows.append(jnp.moveaxis(f, ax, 0).reshape(N_DEV, -1))
    flat = jnp.concatenate(rows, axis=1)
    per = width * row_mult
    total = -(-flat.shape[1] // per) * per
    return jnp.pad(flat, ((0, 0), (0, total - flat.shape[1]))).reshape(N_DEV, total // width, width)


def _unpack(packed, shapes):
    flat = packed.reshape(-1)
    out, off = [], 0
    for s in shapes:
        n = 1
        for d in s:
            n *= d
        out.append(flat[off:off + n].reshape(s))
        off += n
    return out


def _unpack_gathered(gathered, shapes, axes):
    flat = gathered.reshape(N_DEV, -1)
    out, off = [], 0
    for s, ax in zip(shapes, axes):
        n = 1
        for d in s:
            n *= d
        blk = jnp.moveaxis(flat[:, off:off + n].reshape((N_DEV,) + tuple(s)), 0, ax)
        out.append(blk.reshape(tuple(s[:ax]) + (N_DEV * s[ax],) + tuple(s[ax + 1:])))
        off += n
    return out


def _shard_cols(blocks, a, b):
    shard = blocks[0].shape[1]
    out = []
    while a < b:
        d = a // shard
        hi = min(b, (d + 1) * shard)
        out.append(blocks[d][:, a - d * shard:hi - d * shard])
        a = hi
    return out


def _permute_w_in(blocks):
    main = jnp.concatenate(_shard_cols(blocks, O_GQ, O_BL) + _shard_cols(blocks, O_GZ, O_END)
                           + _shard_cols(blocks, O_FQ, O_FL), axis=1)
    pad = jnp.zeros((blocks[0].shape[0], LANE - 24), blocks[0].dtype)
    small = jnp.concatenate(_shard_cols(blocks, O_FL, O_GQ) + _shard_cols(blocks, O_BL, O_GZ) + [pad], axis=1)
    return main, small


_W_IN_SEGS = ((O_FQ, O_FL, True, C_FQ), (O_FL, O_GQ, False, 0), (O_GQ, O_BL, True, C_GQ), (O_BL, O_GZ, False, 8),
              (O_GZ, O_END, True, C_GZ))


def _unpermute_cols(main, small, a, b):
    out = []
    for s0, s1, is_main, t0 in _W_IN_SEGS:
        lo, hi = max(a, s0), min(b, s1)
        if lo < hi:
            out.append((main if is_main else small)[:, t0 + lo - s0:t0 + hi - s0])
    return jnp.concatenate(out, axis=1)


def _lanes(vec, start):
    return jnp.pad(vec.astype(F32), (start, LANE - start - vec.shape[0])).reshape(1, LANE)


BIG = ("w_in", "w_branch_a", "w_branch_b", "w_out", "w_up", "w_down")
BIG_AXES = (2, 2, 1, 1, 2, 1)
SHARDED_SMALL = ("meta_tokens", "gdn_conv_w", "ffn_conv_w")
SHARDED_SMALL_AXES = (1, 2, 2)
REPL = ("norm1_g", "fox_f_bias", "fox_q_norm_g", "fox_k_norm_g", "gdn_a_log", "gdn_dt_bias", "gdn_norm_g", "norm2_g")
ORDER = ("meta_tokens", "norm1_g", "w_in", "fox_f_bias", "fox_q_norm_g", "fox_k_norm_g", "gdn_conv_w", "gdn_a_log",
         "gdn_dt_bias", "gdn_norm_g", "w_branch_a", "w_branch_b", "w_out", "norm2_g", "w_up", "ffn_conv_w", "w_down")


def _layer_fwd(h_res, wl, pad_rows):
    lp = h_res.shape[0]
    sv = {"res_in": h_res}
    (h1,) = _rowwise(_rmsnorm_fn, [(h_res, D, 0)], [wl["norm1_g"]], [(D, BF16)], "rmsnorm1")
    proj = _matmul(h1, wl["w_main"], "nn", F32, name="mm_in")
    small = _matmul(h1, wl["w_small"], "nn", F32, name="mm_in_small")
    sv.update(h1=h1, proj=proj, small=small)

    fox_fn = functools.partial(_fox_prep_fn, pad_rows)
    qh, kh, logf = _rowwise(fox_fn, [(proj, FOX_W, C_FQ // FOX_W), (proj, FOX_W, C_FK // FOX_W), (small, LANE, 0)],
                            [wl["qg"], wl["kg"], wl["fb"]], [(FOX_W, BF16), (FOX_W, BF16), (LANE, F32)], "fox_prep")
    fsum = _cumsum_rows(logf, False, "fox_cumsum")
    frow = fsum[:, :8].T.reshape(8, 1, lp)
    o_a, lse = _fox_fwd(qh, kh, proj, fsum, frow, pad_rows, C_FV // LANE)
    y_a = _matmul(o_a, wl["w_branch_a"], "nn", F32, name="mm_branch_a")
    sv.update(qh=qh, kh=kh, fsum=fsum, frow=frow, o_a=o_a, lse=lse)

    gq, gk, gv, bg = _gdn_act(proj, small, wl["gdn_conv_w"], wl["alog"], wl["dtb"])
    grow = bg[:, 16:24].T.reshape(8, lp // CHUNK, 1, CHUNK)
    o_raw, sprev, tinv = _gdn_fwd(gq, gk, gv, bg, grow)
    (o_b,) = _rowwise(_gdn_post_fn, [(o_raw, GDN_W, 0), (proj, GDN_W, C_GZ // GDN_W)], [wl["gn"]], [(GDN_W, BF16)],
                      "gdn_post")
    y_b = _matmul(o_b, wl["w_branch_b"], "nn", F32, name="mm_branch_b")
    sv.update(gq=gq, gk=gk, gv=gv, bg=bg, grow=grow, o_raw=o_raw, sprev=sprev, tinv=tinv, o_b=o_b)

    (mixed,) = _rowwise(_merge_fn, [(proj, D, C_GATE // D), (proj, D, C_GATE // D + 1), (y_a, D, 0), (y_b, D, 0)], [],
                        [(D, BF16)], "merge")
    res_mid = _matmul(mixed, wl["w_out"], "nn", F32, add=h_res, name="mm_out")
    sv.update(y_a=y_a, y_b=y_b, mixed=mixed, res_mid=res_mid)

    (h2,) = _rowwise(_rmsnorm_fn, [(res_mid, D, 0)], [wl["norm2_g"]], [(D, BF16)], "rmsnorm2")
    up_pre = _matmul(h2, wl["w_up"], "nn", F32, name="mm_up")
    act = _ffn_act(up_pre, wl["ffn_conv_w"])
    out = _matmul(act, wl["w_down"], "nn", F32, add=res_mid, name="mm_down")
    sv.update(h2=h2, up_pre=up_pre, act=act)
    return out, sv


def _layer_bwd(dres, wl, sv, pad_rows):
    lp = dres.shape[0]
    gw = {}
    gw["w_down"] = _matmul(sv["act"], dres, "tn", F32, name="mm_dw_down")
    dact = _matmul(dres, wl["w_down"], "nt", BF16, name="mm_dact")
    dup, gw["ffn_conv_w"] = _ffn_act_bwd(sv["up_pre"], wl["ffn_conv_w"], dact)
    dup_pre = _conv_bwd_x(dup, wl["ffn_conv_w"], pad_rows, D_FF, "ffn_conv_bwd")
    gw["w_up"] = _matmul(sv["h2"], dup_pre, "tn", F32, name="mm_dw_up")
    dh2 = _matmul(dup_pre, wl["w_up"], "nt", F32, name="mm_dh2")
    (dmid,), (gw["norm2_g"],) = _rowwise_bwd(_rmsnorm_fn, [(sv["res_mid"], D, 0)], [wl["norm2_g"]], [(dh2, D, 0)],
                                             "rmsnorm2_bwd", pad_rows, [F32], adds=[(dres, D, 0)])
    gw["w_out"] = _matmul(sv["mixed"], dmid, "tn", F32, name="mm_dw_out")
    dmixed = _matmul(dmid, wl["w_out"], "nt", F32, name="mm_dmixed")
    proj, small = sv["proj"], sv["small"]
    (dg0, dg1, dya, dyb), _ = _rowwise_bwd(
        _merge_fn, [(proj, D, C_GATE // D), (proj, D, C_GATE // D + 1), (sv["y_a"], D, 0), (sv["y_b"], D, 0)], [],
        [(dmixed, D, 0)], "merge_bwd", pad_rows, [BF16, BF16, BF16, BF16])
    gw["w_branch_a"] = _matmul(sv["o_a"], dya, "tn", F32, name="mm_dw_a")
    do_a = _matmul(dya, wl["w_branch_a"], "nt", BF16, name="mm_do_a")
    gw["w_branch_b"] = _matmul(sv["o_b"], dyb, "tn", F32, name="mm_dw_b")
    do_b = _matmul(dyb, wl["w_branch_b"], "nt", F32, name="mm_do_b")

    (do_raw, dgz), (gw["gn"],) = _rowwise_bwd(_gdn_post_fn, [(sv["o_raw"], GDN_W, 0), (proj, GDN_W, C_GZ // GDN_W)],
                                              [wl["gn"]], [(do_b, GDN_W, 0)], "gdn_post_bwd", pad_rows, [F32, BF16])
    dgq, dgk, dgv, dbg, dgrow = _gdn_bwd(sv["gq"], sv["gk"], sv["gv"], sv["bg"], sv["grow"], sv["sprev"],
                                         sv["tinv"], do_raw)
    dbg = dbg + jnp.pad(dgrow.reshape(8, lp).T, ((0, 0), (16, LANE - 24)))
    dconv, dsmall_g, gw["alog"], gw["dtb"], gw["gdn_conv_w"] = _gdn_act_bwd(
        proj, small, wl["gdn_conv_w"], wl["alog"], wl["dtb"], dgq, dgk, dgv, dbg)
    dqkv = _conv_bwd_x(dconv, wl["gdn_conv_w"], pad_rows, GDN_W, "gdn_conv_bwd")

    (delta,) = _rowwise(_fox_delta_fn, [(sv["o_a"], FOX_W, 0), (do_a, FOX_W, 0)], [], [(LANE, F32)], "fox_delta")
    dqh, dkh, dvh, dfq, dfk = _fox_bwd(sv["qh"], sv["kh"], proj, sv["fsum"], sv["frow"], do_a, sv["lse"], delta,
                                       pad_rows, C_FV // LANE)
    df8 = dfq.reshape(lp, FOX_W // LANE, LANE)[:, :, :2].reshape(lp, 8) + dfk.reshape(8, lp).T
    dlogf = _cumsum_rows(jnp.pad(df8, ((0, 0), (0, LANE - 8))), True, "fox_cumsum_bwd")
    fox_fn = functools.partial(_fox_prep_fn, pad_rows)
    (dfq_p, dfk_p, dsmall_f), (gw["qg"], gw["kg"], gw["fb"]) = _rowwise_bwd(
        fox_fn, [(proj, FOX_W, C_FQ // FOX_W), (proj, FOX_W, C_FK // FOX_W), (small, LANE, 0)],
        [wl["qg"], wl["kg"], wl["fb"]], [(dqh, FOX_W, 0), (dkh, FOX_W, 0), (dlogf, LANE, 0)],
        "fox_prep_bwd", pad_rows, [BF16, BF16, F32], adds=[None, None, (dsmall_g, LANE, 0)])

    dproj = jnp.concatenate([dqkv, dgz, dg0, dg1, dfq_p, dfk_p, dvh], axis=1)
    gw["w_main"] = _matmul(sv["h1"], dproj, "tn", F32, name="mm_dw_main")
    gw["w_small"] = _matmul(sv["h1"], dsmall_f, "tn", F32, name="mm_dw_small")
    dh1 = _matmul(dproj, wl["w_main"], "nt", F32, name="mm_dh1")
    dh1 = _matmul(dsmall_f, wl["w_small"], "nt", F32, add=dh1, name="mm_dh1_small")
    (din,), (gw["norm1_g"],) = _rowwise_bwd(_rmsnorm_fn, [(sv["res_in"], D, 0)], [wl["norm1_g"]], [(dh1, D, 0)],
                                            "rmsnorm1_bwd", pad_rows, [F32], adds=[(dmid, D, 0)])
    return din, gw


def kernel(x, meta_tokens, norm1_g, w_in, fox_f_bias, fox_q_norm_g, fox_k_norm_g, gdn_conv_w, gdn_a_log, gdn_dt_bias, gdn_norm_g, w_branch_a, w_branch_b, w_out, norm2_g, w_up, ffn_conv_w, w_down, loss_target, m_meta_tokens, m_norm1_g, m_w_in, m_fox_f_bias, m_fox_q_norm_g, m_fox_k_norm_g, m_gdn_conv_w, m_gdn_a_log, m_gdn_dt_bias, m_gdn_norm_g, m_w_branch_a, m_w_branch_b, m_w_out, m_norm2_g, m_w_up, m_ffn_conv_w, m_w_down, v_meta_tokens, v_norm1_g, v_w_in, v_fox_f_bias, v_fox_q_norm_g, v_fox_k_norm_g, v_gdn_conv_w, v_gdn_a_log, v_gdn_dt_bias, v_gdn_norm_g, v_w_branch_a, v_w_branch_b, v_w_out, v_norm2_g, v_w_up, v_ffn_conv_w, v_w_down):
    w = dict(meta_tokens=meta_tokens, norm1_g=norm1_g, w_in=w_in, fox_f_bias=fox_f_bias, fox_q_norm_g=fox_q_norm_g,
             fox_k_norm_g=fox_k_norm_g, gdn_conv_w=gdn_conv_w, gdn_a_log=gdn_a_log, gdn_dt_bias=gdn_dt_bias,
             gdn_norm_g=gdn_norm_g, w_branch_a=w_branch_a, w_branch_b=w_branch_b, w_out=w_out, norm2_g=norm2_g,
             w_up=w_up, ffn_conv_w=ffn_conv_w, w_down=w_down)
    mom = dict(meta_tokens=m_meta_tokens, norm1_g=m_norm1_g, w_in=m_w_in, fox_f_bias=m_fox_f_bias,
               fox_q_norm_g=m_fox_q_norm_g, fox_k_norm_g=m_fox_k_norm_g, gdn_conv_w=m_gdn_conv_w,
               gdn_a_log=m_gdn_a_log, gdn_dt_bias=m_gdn_dt_bias, gdn_norm_g=m_gdn_norm_g, w_branch_a=m_w_branch_a,
               w_branch_b=m_w_branch_b, w_out=m_w_out, norm2_g=m_norm2_g, w_up=m_w_up, ffn_conv_w=m_ffn_conv_w,
               w_down=m_w_down)
    var = dict(meta_tokens=v_meta_tokens, norm1_g=v_norm1_g, w_in=v_w_in, fox_f_bias=v_fox_f_bias,
               fox_q_norm_g=v_fox_q_norm_g, fox_k_norm_g=v_fox_k_norm_g, gdn_conv_w=v_gdn_conv_w,
               gdn_a_log=v_gdn_a_log, gdn_dt_bias=v_gdn_dt_bias, gdn_norm_g=v_gdn_norm_g, w_branch_a=v_w_branch_a,
               w_branch_b=v_w_branch_b, w_out=v_w_out, norm2_g=v_norm2_g, w_up=v_w_up, ffn_conv_w=v_ffn_conv_w,
               w_down=v_w_down)
    depth = norm1_g.shape[0]
    seq = x.shape[1]
    l_tok = N_META + seq
    lp = -(-l_tok // LANE) * LANE
    pad_rows = lp - l_tok
    row_start = pad_rows + N_META

    got = {n: _all_gather(w[n].astype(BF16), "gather_" + n) for n in BIG}
    small_shapes = [w[n].shape for n in SHARDED_SMALL]
    gathered_s = _all_gather(_pack([w[n] for n in SHARDED_SMALL], LANE, F32, 8), "gather_small")
    full = dict(zip(SHARDED_SMALL, _unpack_gathered(gathered_s, small_shapes, SHARDED_SMALL_AXES)))

    def join(name, l):
        return jnp.concatenate([got[name][d, l] for d in range(N_DEV)], axis=BIG_AXES[BIG.index(name)] - 1)

    layers = []
    for l in range(depth):
        w_main, w_small = _permute_w_in([got["w_in"][d, l] for d in range(N_DEV)])
        layers.append(dict(
            w_main=w_main, w_small=w_small, w_branch_a=join("w_branch_a", l), w_branch_b=join("w_branch_b", l),
            w_out=join("w_out", l), w_up=join("w_up", l), w_down=join("w_down", l),
            gdn_conv_w=full["gdn_conv_w"][l], ffn_conv_w=full["ffn_conv_w"][l],
            norm1_g=norm1_g[l].reshape(1, D), norm2_g=norm2_g[l].reshape(1, D),
            qg=jnp.tile(fox_q_norm_g[l], 8).reshape(1, FOX_W), kg=jnp.tile(fox_k_norm_g[l], 8).reshape(1, FOX_W),
            fb=_lanes(fox_f_bias[l], 0), alog=_lanes(gdn_a_log[l], 16), dtb=_lanes(gdn_dt_bias[l], 16),
            gn=jnp.tile(gdn_norm_g[l], 8).reshape(1, GDN_W)))

    h_res = jnp.concatenate([jnp.zeros((pad_rows, D), F32), full["meta_tokens"], x[0]], axis=0)
    saved = []
    for l in range(depth):
        h_res, sv = _layer_fwd(h_res, layers[l], pad_rows)
        saved.append(sv)
    dres, loss_part = _loss_head(h_res, loss_target[0], row_start)
    loss = lax.psum(loss_part[0, 0], ("x", "y", "c"))

    def dest_block(name, g, d):
        if name == "w_in":
            s = w_in.shape[2]
            return _unpermute_cols(g["w_main"], g["w_small"], s * d, s * (d + 1)).astype(BF16)
        if BIG_AXES[BIG.index(name)] == 2:
            s = w[name].shape[2]
            return g[name][:, s * d:s * (d + 1)].astype(BF16)
        s = w[name].shape[1]
        return g[name][s * d:s * (d + 1), :].astype(BF16)

    gws = [None] * depth
    started = [None] * depth
    for l in reversed(range(depth)):
        dres, gws[l] = _layer_bwd(dres, layers[l], saved[l], pad_rows)
        started[l] = _exchange_start([jnp.stack([dest_block(n, gws[l], d) for d in range(N_DEV)]) for n in BIG],
                                     "scatter_start_%d" % l)
        if l > 0:
            token = started[l][4][0, 0].astype(BF16)
            layers[l - 1] = dict(layers[l - 1], w_down=layers[l - 1]["w_down"] + token)
    me = 4 * lax.axis_index("x") + 2 * lax.axis_index("y") + lax.axis_index("c")
    landed = [None] * depth
    for l in reversed(range(depth)):
        send_sems, recv_sems, srcs, lands, _ = started[l]
        srcs, lands = _exchange_wait(send_sems, recv_sems, srcs, lands, dres, "scatter_wait_%d" % l)
        landed[l] = [lax.dynamic_update_index_in_dim(ld, lax.dynamic_index_in_dim(sr, me, 0, keepdims=False), me, 0)
                     for sr, ld in zip(srcs, lands)]
    grad_x = dres[row_start:].reshape(x.shape)

    def stack(fn):
        return jnp.stack([fn(g) for g in gws])

    part = dict(
        meta_tokens=dres[pad_rows:row_start],
        norm1_g=stack(lambda g: g["norm1_g"][0]), norm2_g=stack(lambda g: g["norm2_g"][0]),
        fox_f_bias=stack(lambda g: g["fb"][0, 0:8]),
        fox_q_norm_g=stack(lambda g: g["qg"].reshape(8, FOX_DH).sum(0)),
        fox_k_norm_g=stack(lambda g: g["kg"].reshape(8, FOX_DH).sum(0)),
        gdn_conv_w=stack(lambda g: g["gdn_conv_w"]), gdn_a_log=stack(lambda g: g["alog"][0, 16:24]),
        gdn_dt_bias=stack(lambda g: g["dtb"][0, 16:24]),
        gdn_norm_g=stack(lambda g: g["gn"].reshape(8, GDN_DH).sum(0)),
        ffn_conv_w=stack(lambda g: g["ffn_conv_w"]))

    res = {}
    for idx, n in enumerate(BIG):
        parts = jnp.stack([landed[l][idx] for l in range(depth)], axis=1)
        res[n] = _sum_adamw(parts, w[n], mom[n], var[n], "adamw_" + n)

    small_names = SHARDED_SMALL + REPL
    small_axes = SHARDED_SMALL_AXES + (None,) * len(REPL)
    landed_s = _all_to_all(_pack_dest([part[n] for n in small_names], small_axes, LANE, F32, 8), "scatter_small")
    shapes_s = [w[n].shape for n in small_names]
    outs = _sum_adamw(landed_s[:, None], *[_pack([d[n] for n in small_names], LANE, F32, 8)[None] for d in (w, mom, var)],
                      "adamw_small")
    for o_idx, packed in enumerate(outs):
        for n, a in zip(small_names, _unpack(packed[0], shapes_s)):
            res.setdefault(n, [None] * 4)[o_idx] = a

    return (loss, grad_x, *[res[n][0] for n in ORDER], *[res[n][1] for n in ORDER],
            *[res[n][2] for n in ORDER], *[res[n][3] for n in ORDER])
```

```python
import functools

import jax
import jax.numpy as jnp
from jax import lax
from jax.experimental import pallas as pl
from jax.experimental.pallas import tpu as pltpu

F32, BF16 = jnp.float32, jnp.bfloat16
MESH = pl.DeviceIdType.MESH

D = 1024
N_META = 16
DEPTH = 4
EPS = 1e-6
NEG = -1e30
FOX_W, FOX_DH = 512, 64
GDN_W, GDN_DH, GDN_H = 1024, 128, 8
CHUNK = 64
D_FF = 2816
N_DEV = 8
ADAM_LR, ADAM_B1, ADAM_B2, ADAM_EPS, ADAM_WD, ADAM_STEP = 0.001, 0.9, 0.999, 1e-08, 0.01, 10

VMEM_LIMIT_BYTES = 48 * 1024 * 1024
MATMUL_VMEM_BUDGET = 36 * 1024 * 1024
ROW_TILE = 128
ROW_TILES_WIDE = (640, 512, 256, 128)
ROW_TILES = (320, 256, 128)
LANE = 128

C_GQ, C_GK, C_GV, C_GZ, C_GATE, C_FQ, C_FK, C_FV = 0, 1024, 2048, 3072, 4096, 6144, 6656, 7168
W_MAIN = 7680
O_FQ, O_FK, O_FV, O_FL, O_GQ, O_GK, O_GV, O_BL, O_AL, O_GZ, O_GATE, O_END = (
    0, 512, 1024, 1536, 1544, 2568, 3592, 4616, 4624, 4632, 5656, 7704)


def _pick(n, cands):
    for c in cands:
        if n % c == 0:
            return c
    return n


def _call(body, *, name, out_shape, in_specs, out_specs, grid=(), scratch=(), sem=None):
    kw = dict(vmem_limit_bytes=VMEM_LIMIT_BYTES)
    if sem is not None:
        kw["dimension_semantics"] = sem
    return pl.pallas_call(body, name=name, out_shape=out_shape, grid=grid, in_specs=in_specs,
                          out_specs=out_specs, scratch_shapes=list(scratch),
                          compiler_params=pltpu.CompilerParams(**kw))


_DIMS = {"nn": (((1,), (0,)), ((), ())), "nt": (((1,), (1,)), ((), ())), "tn": (((0,), (0,)), ((), ()))}


_DIMS_BATCHED = {"nn": (((2,), (1,)), ((0,), (0,))), "nt": (((2,), (2,)), ((0,), (0,))),
                 "tn": (((1,), (1,)), ((0,), (0,)))}


def _dot(a, b, mode, prec=None):
    dims = _DIMS[mode] if a.ndim == 2 else _DIMS_BATCHED[mode]
    return lax.dot_general(a, b, dims, precision=prec, preferred_element_type=F32)


def _mm_grads(f, mode, a, b, g):
    if mode == "nn":
        return f(g, b, "nt"), f(a, g, "tn")
    if mode == "nt":
        return f(g, b, "nn"), f(g, a, "tn")
    return f(b, g, "nt"), f(a, g, "nn")


@functools.partial(jax.custom_vjp, nondiff_argnums=(2,))
def _mmb(a, b, mode):
    return _dot(a.astype(BF16), b.astype(BF16), mode)


def _mmb_fwd(a, b, mode):
    return _mmb(a, b, mode), (a, b)


def _mmb_bwd(mode, res, g):
    return _mm_grads(_mmb, mode, res[0], res[1], g)


_mmb.defvjp(_mmb_fwd, _mmb_bwd)


def _split(a):
    hi = a.astype(BF16)
    return hi, (a - hi.astype(F32)).astype(BF16)


@functools.partial(jax.custom_vjp, nondiff_argnums=(2,))
def _mmh(a, b, mode):
    ah, al = _split(a)
    bh, bl = _split(b)
    return _dot(ah, bh, mode) + (_dot(ah, bl, mode) + _dot(al, bh, mode))


def _mmh_fwd(a, b, mode):
    return _mmh(a, b, mode), (a, b)


def _mmh_bwd(mode, res, g):
    return _mm_grads(_mmh, mode, res[0], res[1], g)


_mmh.defvjp(_mmh_fwd, _mmh_bwd)


def _dot_sel(sel, x, mode):
    s = sel.astype(BF16)
    x1 = x.astype(BF16)
    x2, x3 = _split(x - x1.astype(F32))
    return _dot(s, x1, mode) + (_dot(s, x2, mode) + _dot(s, x3, mode))


@jax.custom_vjp
def _mms(sel, x):
    return _dot_sel(sel, x, "nn")


def _mms_fwd(sel, x):
    return _dot_sel(sel, x, "nn"), sel


def _mms_bwd(sel, g):
    return jnp.zeros_like(sel), _dot_sel(sel, g, "tn")


_mms.defvjp(_mms_fwd, _mms_bwd)


def _softplus(z):
    return jnp.maximum(z, 0.0) + jnp.log(1.0 + jnp.exp(-jnp.abs(z)))


def _log_sigmoid(z):
    return jnp.minimum(z, 0.0) - jnp.log(1.0 + jnp.exp(-jnp.abs(z)))


def _silu(z):
    return z * jax.nn.sigmoid(z)


def _iota(shape, dim):
    return lax.broadcasted_iota(jnp.int32, shape, dim)


def _inv_unit_lower_raw(n):
    c = n.shape[-1]
    ri, ci = _iota((c, c), 0), _iota((c, c), 1)
    eye = (ri == ci).astype(F32)
    dmask = (ri // 16) == (ci // 16)
    dpart = jnp.where(dmask, n, 0.0)
    lpart = n - dpart
    x = -dpart
    p = eye + x
    for _ in range(3):
        x = _mmh(x, x, "nn")
        p = p + _mmh(p, x, "nn")
    m = -_mmh(p, lpart, "nn")
    q = eye + m
    steps = 1
    while (1 << steps) < c // 16:
        steps += 1
    for _ in range(steps - 1):
        m = _mmh(m, m, "nn")
        q = q + _mmh(q, m, "nn")
    return _mmh(q, p, "nn")


@jax.custom_vjp
def _inv_given(n, t):
    return t


def _inv_given_fwd(n, t):
    return t, t


def _inv_given_bwd(t, g):
    c = t.shape[-1]
    strict = _iota((c, c), 0) > _iota((c, c), 1)
    d = -_mmh(_mmh(t, g, "tn"), t, "nt")
    return jnp.where(strict, d, 0.0), jnp.zeros_like(t)


_inv_given.defvjp(_inv_given_fwd, _inv_given_bwd)


def _shift_down(x, halo, s):
    if s == 0:
        return x
    xs = pltpu.roll(x, s, 0)
    hs = pltpu.roll(halo, s, 0)
    top = jnp.where(_iota(hs.shape, 0) < s, hs, xs[0:8])
    return jnp.concatenate([top, xs[8:]], axis=0)


def _shift_up(x, halo, s):
    if s == 0:
        return x
    tm = x.shape[0]
    xs = pltpu.roll(x, tm - s, 0)
    hs = pltpu.roll(halo, 8 - s, 0)
    bot = jnp.where(_iota(hs.shape, 0) >= 8 - s, hs, xs[tm - 8:])
    return jnp.concatenate([xs[:tm - 8], bot], axis=0)


def _causal_conv(x, halo, w):
    kk = w.shape[0]
    y = x * w[kk - 1:kk, :]
    for k in range(kk - 1):
        y = y + _shift_down(x, halo, kk - 1 - k) * w[k:k + 1, :]
    return y


def _head_scale(x, width, fn):
    outs = []
    for h in range(x.shape[1] // width):
        seg = x[:, h * width:(h + 1) * width]
        outs.append(seg * fn(jnp.sum(seg * seg, axis=1, keepdims=True)))
    return jnp.concatenate(outs, axis=1)


def _matmul(a, b, mode, out_dtype, add=None, name="mm"):
    if mode == "nn":
        (m, k), n = a.shape, b.shape[1]
    elif mode == "nt":
        (m, k), n = a.shape, b.shape[0]
    else:
        (k, m), n = a.shape, b.shape[1]
    tm = _pick(m, (1408, 1024, 512, 256, 128) if mode == "tn" else (640, 512, 256, 128))
    tn = _pick(n, (1536, 1408, 1024, 768, 512, 256, 128))
    sa, sb = a.dtype.itemsize, b.dtype.itemsize
    fixed = tm * tn * 4 * (3 + (2 if add is not None else 0))
    tk = 128
    for cand in (k, 2816, 2560, 1664, 1536, 1280, 1024, 832, 768, 640, 512, 256, 128):
        if mode != "tn" and cand != k and cand % LANE:
            continue
        if k % cand == 0 and fixed + 2 * cand * (tm * sa + tn * sb) <= MATMUL_VMEM_BUDGET:
            tk = cand
            break
    nk = k // tk
    a_spec = {"nn": pl.BlockSpec((tm, tk), lambda i, j, kk: (i, kk)),
              "nt": pl.BlockSpec((tm, tk), lambda i, j, kk: (i, kk)),
              "tn": pl.BlockSpec((tk, tm), lambda i, j, kk: (kk, i))}[mode]
    b_spec = {"nn": pl.BlockSpec((tk, tn), lambda i, j, kk: (kk, j)),
              "nt": pl.BlockSpec((tn, tk), lambda i, j, kk: (j, kk)),
              "tn": pl.BlockSpec((tk, tn), lambda i, j, kk: (kk, j))}[mode]
    o_spec = pl.BlockSpec((tm, tn), lambda i, j, kk: (i, j))
    has_add = add is not None

    def body(*refs):
        a_ref, b_ref = refs[0], refs[1]
        add_ref = refs[2] if has_add else None
        o_ref = refs[3] if has_add else refs[2]
        part = _dot(a_ref[...].astype(BF16), b_ref[...].astype(BF16), mode)
        if nk == 1:
            if has_add:
                part = part + add_ref[...].astype(F32)
            o_ref[...] = part.astype(out_dtype)
        else:
            acc = refs[-1]
            kk = pl.program_id(2)

            @pl.when(kk == 0)
            def _():
                acc[...] = part

            @pl.when(kk > 0)
            def _():
                acc[...] += part

            @pl.when(kk == nk - 1)
            def _():
                r = acc[...]
                if has_add:
                    r = r + add_ref[...].astype(F32)
                o_ref[...] = r.astype(out_dtype)

    ins = [a, b] + ([add] if has_add else [])
    specs = [a_spec, b_spec] + ([o_spec] if has_add else [])
    return _call(body, name=name, out_shape=jax.ShapeDtypeStruct((m, n), out_dtype), grid=(m // tm, n // tn, nk),
                 in_specs=specs, out_specs=o_spec,
                 scratch=[pltpu.VMEM((tm, tn), F32)] if nk > 1 else [],
                 sem=("parallel", "parallel", "arbitrary"))(*ins)


def _row_spec(width, colblock, tm):
    return pl.BlockSpec((tm, width), lambda i, cb=colblock: (i, cb))


def _full_spec(arr):
    nd = arr.ndim
    return pl.BlockSpec(arr.shape, lambda i, nd=nd: (0,) * nd)


def _rowwise(fn, rows, params, outs, name):
    lp = rows[0][0].shape[0]
    tm = _pick(lp, ROW_TILES_WIDE)
    nr, npar = len(rows), len(params)

    def body(*refs):
        row0 = pl.program_id(0) * tm
        vals = [r[...].astype(F32) for r in refs[:nr + npar]]
        res = fn(*vals, row0)
        for o_ref, r in zip(refs[nr + npar:], res):
            o_ref[...] = r.astype(o_ref.dtype)

    out = _call(body, name=name, grid=(lp // tm,),
                out_shape=[jax.ShapeDtypeStruct((lp, w), dt) for w, dt in outs],
                in_specs=[_row_spec(w, cb, tm) for _, w, cb in rows] + [_full_spec(p) for p in params],
                out_specs=[_row_spec(w, 0, tm) for w, _ in outs], sem=("parallel",))(
                    *[r[0] for r in rows], *params)
    return out


def _rowwise_bwd(fn, rows, params, cts, name, pad_rows, grad_dtypes, adds=None):
    lp = rows[0][0].shape[0]
    tm = _pick(lp, ROW_TILES)
    nr, npar, nct = len(rows), len(params), len(cts)
    adds = adds or [None] * nr
    add_list = [a for a in adds if a is not None]
    nadd = len(add_list)

    def body(*refs):
        i = pl.program_id(0)
        row0 = i * tm
        vals = [r[...].astype(F32) for r in refs[:nr + npar]]
        ct_vals = tuple(r[...].astype(F32) for r in refs[nr + npar:nr + npar + nct])
        add_refs = list(refs[nr + npar + nct:nr + npar + nct + nadd])
        outs = refs[nr + npar + nct + nadd:]
        _, vjp = jax.vjp(lambda *args: tuple(fn(*args, row0)), *vals)
        grads = vjp(ct_vals)
        valid = (row0 + _iota((tm, 1), 0)) >= pad_rows
        for idx in range(nr):
            g = jnp.where(valid, grads[idx], 0.0)
            if adds[idx] is not None:
                g = g + add_refs.pop(0)[...].astype(F32)
            outs[idx][...] = g.astype(outs[idx].dtype)
        for idx in range(npar):
            o_ref = outs[nr + idx]

            @pl.when(i == 0)
            def _(o_ref=o_ref):
                o_ref[...] = jnp.zeros_like(o_ref)

            o_ref[...] += grads[nr + idx]

    out = _call(body, name=name, grid=(lp // tm,),
                out_shape=[jax.ShapeDtypeStruct((lp, w), dt) for (_, w, _), dt in zip(rows, grad_dtypes)]
                + [jax.ShapeDtypeStruct(p.shape, F32) for p in params],
                in_specs=[_row_spec(w, cb, tm) for _, w, cb in rows] + [_full_spec(p) for p in params]
                + [_row_spec(w, cb, tm) for _, w, cb in cts] + [_row_spec(w, cb, tm) for _, w, cb in add_list],
                out_specs=[_row_spec(w, 0, tm) for _, w, _ in rows] + [_full_spec(p) for p in params],
                sem=("arbitrary",))(*[r[0] for r in rows], *params, *[c[0] for c in cts], *[a[0] for a in add_list])
    return out[:nr], out[nr:]


def _rmsnorm_fn(x, g, row0):
    return (x * lax.rsqrt(jnp.mean(x * x, axis=1, keepdims=True) + EPS) * g,)


def _fox_prep_fn(pad_rows, fq, fk, small, qg, kg, fb, row0):
    ri, ci = _iota((FOX_W, FOX_W), 0), _iota((FOX_W, FOX_W), 1)
    bd = jnp.where((ri // FOX_DH) == (ci // FOX_DH), 1.0 / FOX_DH, 0.0)

    def hn(x, g):
        return x * lax.rsqrt(_mmh(x * x, bd, "nn") + EPS) * g

    tm = small.shape[0]
    keep = (_iota((tm, LANE), 1) < 8) & ((row0 + _iota((tm, LANE), 0)) >= pad_rows)
    logf = jnp.where(keep, _log_sigmoid(small + fb), 0.0)
    return hn(fq, qg) * (FOX_DH ** -0.5), hn(fk, kg), logf


def _gdn_act_fn(cq, ck, cv, small, alog, dtb):
    tm = small.shape[0]
    q = _head_scale(_silu(cq), GDN_DH, lambda s: lax.rsqrt(s + EPS) * (GDN_DH ** -0.5))
    k = _head_scale(_silu(ck), GDN_DH, lambda s: lax.rsqrt(s + EPS))
    v = _silu(cv)
    lane = _iota((tm, LANE), 1)
    beta = jnp.where((lane >= 8) & (lane < 16), jax.nn.sigmoid(small), 0.0)
    g = jnp.where((lane >= 16) & (lane < 24), -jnp.exp(alog) * _softplus(small + dtb), 0.0)
    ri, ci = _iota((tm, tm), 0), _iota((tm, tm), 1)
    tri = jnp.where(((ri // CHUNK) == (ci // CHUNK)) & (ci <= ri), 1.0, 0.0)
    return q, k, v, beta + _mms(tri, g)


def _gdn_post_fn(o, gz, gn, row0):
    return (_head_scale(o, GDN_DH, lambda s: lax.rsqrt(s * (1.0 / GDN_DH) + EPS)) * gn * _silu(gz),)


def _merge_fn(g0, g1, ya, yb, row0):
    return (jax.nn.sigmoid(g0) * ya + jax.nn.sigmoid(g1) * yb,)


def _cumsum_rows(x, reverse, name):
    lp, w = x.shape
    tm = _pick(lp, (640, 512, 256, 128))
    nt = lp // tm

    def body(x_ref, o_ref, carry):
        i = pl.program_id(0)

        @pl.when(i == 0)
        def _():
            carry[...] = jnp.zeros_like(carry)

        ri, ci = _iota((tm, tm), 0), _iota((tm, tm), 1)
        tri = jnp.where((ci >= ri) if reverse else (ci <= ri), 1.0, 0.0)
        blk = x_ref[...]
        o_ref[...] = _dot_sel(tri, blk, "nn") + carry[0:1, :]
        carry[...] = carry[...] + jnp.sum(blk, axis=0, keepdims=True)

    idx = (lambda i: (nt - 1 - i, 0)) if reverse else (lambda i: (i, 0))
    return _call(body, name=name, grid=(nt,), out_shape=jax.ShapeDtypeStruct((lp, w), F32),
                 in_specs=[pl.BlockSpec((tm, w), idx)], out_specs=pl.BlockSpec((tm, w), idx),
                 scratch=[pltpu.VMEM((8, w), F32)], sem=("arbitrary",))(x)


def _fox_scores(q, k, fq, fk, hh, qpos0, kpos0, pad_rows, masked):
    tq, tk = q.shape[0], k.shape[0]
    lane = _iota(q.shape, 1)
    sel = (lane < FOX_DH) if hh == 0 else (lane >= FOX_DH)
    s = _dot(jnp.where(sel, q, jnp.zeros_like(q)), k, "nt") + fq - fk
    if not masked:
        return s, None, sel
    qpos = qpos0 + _iota((tq, tk), 0)
    kpos = kpos0 + _iota((tq, tk), 1)
    mask = (kpos <= qpos) & (kpos >= pad_rows)
    return jnp.where(mask, s, NEG), mask, sel


def _probs(s, mask, shift):
    p = jnp.exp(s - shift)
    return p if mask is None else jnp.where(mask, p, 0.0)


def _both_variants(needs_mask, fn):
    @pl.when(needs_mask)
    def _():
        fn(True)

    @pl.when(jnp.logical_not(needs_mask))
    def _():
        fn(False)


def _lane_col(blk, lane_idx):
    return jnp.sum(jnp.where(_iota(blk.shape, 1) == lane_idx, blk, 0.0), axis=1, keepdims=True)


def _to_lanes(cols, width=LANE):
    lane = _iota((cols[0].shape[0], width), 1)
    out = jnp.zeros((cols[0].shape[0], width), F32)
    for idx, c in enumerate(cols):
        out = jnp.where(lane == idx, c, out)
    return out


def _fox_fwd(q, k, v, fsum, frow, pad_rows, v_col):
    lp = q.shape[0]
    t = _pick(lp, (640, 512, 256, 128))
    n = lp // t

    def body(q_ref, k_ref, v_ref, f_ref, fk_ref, o_ref, lse_ref, acc, m_s, l_s, fq_s):
        pr, i, j = pl.program_id(0), pl.program_id(1), pl.program_id(2)

        @pl.when(j == 0)
        def _():
            acc[...] = jnp.zeros_like(acc)
            m_s[...] = jnp.full_like(m_s, NEG)
            l_s[...] = jnp.zeros_like(l_s)
            for hh in range(2):
                fq_s[hh] = _lane_col(f_ref[...], 2 * pr + hh)

        def step(masked):
            for hh in range(2):
                s, mask, _ = _fox_scores(q_ref[...], k_ref[...], fq_s[hh], fk_ref[hh], hh, i * t, j * t, pad_rows,
                                         masked)
                m_prev = m_s[hh]
                m_new = jnp.maximum(m_prev, jnp.max(s, axis=1, keepdims=True))
                p = _probs(s, mask, m_new)
                alpha = jnp.exp(m_prev - m_new)
                l_s[hh] = alpha * l_s[hh] + jnp.sum(p, axis=1, keepdims=True)
                acc[hh] = alpha * acc[hh] + _dot(p.astype(BF16), v_ref[...].astype(BF16), "nn")
                m_s[hh] = m_new

        @pl.when(j <= i)
        def _():
            _both_variants((j == i) | (j == 0), step)

        @pl.when(j == i)
        def _():
            outs, lses = [], []
            for hh in range(2):
                l = l_s[hh]
                ok = l > 0.0
                outs.append(acc[hh] * jnp.where(ok, 1.0 / jnp.where(ok, l, 1.0), 0.0))
                lses.append(jnp.where(ok, m_s[hh] + jnp.log(jnp.where(ok, l, 1.0)), 0.0))
            lane = _iota((t, LANE), 1)
            o_ref[...] = jnp.where(lane < FOX_DH, outs[0], outs[1]).astype(o_ref.dtype)
            lse_ref[...] = _to_lanes(lses)

    qspec = pl.BlockSpec((t, LANE), lambda p, i, j: (i, p))
    kspec = pl.BlockSpec((t, LANE), lambda p, i, j: (jnp.minimum(j, i), p))
    vspec = pl.BlockSpec((t, LANE), lambda p, i, j: (jnp.minimum(j, i), v_col + p))
    fspec = pl.BlockSpec((t, LANE), lambda p, i, j: (i, 0))
    rspec = pl.BlockSpec((2, 1, t), lambda p, i, j: (p, 0, jnp.minimum(j, i)))
    return _call(body, name="fox_fwd", grid=(FOX_W // LANE, n, n),
                 out_shape=[jax.ShapeDtypeStruct((lp, FOX_W), BF16), jax.ShapeDtypeStruct((lp, FOX_W), F32)],
                 in_specs=[qspec, kspec, vspec, fspec, rspec], out_specs=[qspec, qspec],
                 scratch=[pltpu.VMEM((2, t, LANE), F32), pltpu.VMEM((2, t, 1), F32), pltpu.VMEM((2, t, 1), F32),
                          pltpu.VMEM((2, t, 1), F32)],
                 sem=("parallel", "parallel", "arbitrary"))(q, k, v, fsum, frow)


def _fox_delta_fn(o, do, row0):
    ri, ci = _iota((FOX_W, LANE), 0), _iota((FOX_W, LANE), 1)
    sel = jnp.where((ri // FOX_DH) == ci, 1.0, 0.0).astype(BF16)
    x = o * do
    x1 = x.astype(BF16)
    x2, x3 = _split(x - x1.astype(F32))
    return (_dot(x1, sel, "nn") + (_dot(x2, sel, "nn") + _dot(x3, sel, "nn")),)


def _fox_bwd(q, k, v, fsum, frow, do, lse, delta, pad_rows, v_col):
    lp = q.shape[0]
    t = _pick(lp, (640, 512, 256, 128))
    n = lp // t

    def body(q_ref, k_ref, v_ref, f_ref, fk_ref, do_ref, lse_ref, dl_ref,
             dq_ref, dk_ref, dv_ref, dfq_ref, dfk_ref, dka, dva, dfa):
        pr, j, i = pl.program_id(0), pl.program_id(1), pl.program_id(2)
        lane = _iota((t, LANE), 1)

        @pl.when((j == 0) & (i == 0))
        def _():
            dq_ref[...] = jnp.zeros_like(dq_ref)
            dfq_ref[...] = jnp.zeros_like(dfq_ref)

        @pl.when(i == 0)
        def _():
            dka[...] = jnp.zeros_like(dka)
            dva[...] = jnp.zeros_like(dva)
            dfa[...] = jnp.zeros_like(dfa)

        def step(masked):
            rows = pl.ds(pl.multiple_of(i * t, t), t)
            dq_add = jnp.zeros((t, LANE), F32)
            rowsums = []
            for hh in range(2):
                fq = _lane_col(f_ref[...], 2 * pr + hh)
                s, mask, sel = _fox_scores(q_ref[...], k_ref[...], fq, fk_ref[hh], hh, i * t, j * t, pad_rows, masked)
                p = _probs(s, mask, _lane_col(lse_ref[...], hh))
                dop = jnp.where(sel, do_ref[...], jnp.zeros_like(do_ref[...]))
                ds = p * (_dot(dop, v_ref[...].astype(BF16), "nt") - _lane_col(dl_ref[...], 2 * pr + hh))
                dsb = ds.astype(BF16)
                dva[hh] += _dot(p.astype(BF16), do_ref[...], "tn")
                dka[hh] += _dot(dsb, q_ref[...], "tn")
                dfa[hh] -= jnp.sum(ds, axis=0, keepdims=True)
                dq_add = dq_add + _dot(dsb, jnp.where(sel, k_ref[...], jnp.zeros_like(k_ref[...])), "nn")
                rowsums.append(jnp.sum(ds, axis=1, keepdims=True))
            dq_ref[rows, :] += dq_add
            dfq_ref[rows, :] += _to_lanes(rowsums)

        @pl.when(i >= j)
        def _():
            _both_variants((j == i) | (j == 0), step)

        @pl.when(i == n - 1)
        def _():
            dk_ref[...] = jnp.where(lane < FOX_DH, dka[0], dka[1])
            dv_ref[...] = jnp.where(lane < FOX_DH, dva[0], dva[1]).astype(dv_ref.dtype)
            dfk_ref[...] = dfa[...]

    qspec = pl.BlockSpec((t, LANE), lambda p, j, i: (jnp.maximum(i, j), p))
    f_q = pl.BlockSpec((t, LANE), lambda p, j, i: (jnp.maximum(i, j), 0))
    kspec = pl.BlockSpec((t, LANE), lambda p, j, i: (j, p))
    vspec = pl.BlockSpec((t, LANE), lambda p, j, i: (j, v_col + p))
    rspec = pl.BlockSpec((2, 1, t), lambda p, j, i: (p, 0, j))
    whole = pl.BlockSpec((lp, LANE), lambda p, j, i: (0, p))
    wide = jax.ShapeDtypeStruct((lp, FOX_W), F32)
    return _call(body, name="fox_bwd", grid=(FOX_W // LANE, n, n),
                 out_shape=[wide, wide, jax.ShapeDtypeStruct((lp, FOX_W), BF16), wide,
                            jax.ShapeDtypeStruct((8, 1, lp), F32)],
                 in_specs=[qspec, kspec, vspec, f_q, rspec, qspec, qspec, f_q],
                 out_specs=[whole, kspec, kspec, whole, rspec],
                 scratch=[pltpu.VMEM((2, t, LANE), F32), pltpu.VMEM((2, t, LANE), F32), pltpu.VMEM((2, 1, t), F32)],
                 sem=("parallel", "arbitrary", "arbitrary"))(q, k, v, fsum, frow, do, lse, delta)


def _gdn_chunk(q, k, v, beta, gcol, grow, s, inv):
    c = q.shape[-2]
    ri, ci = _iota((c, c), 0), _iota((c, c), 1)
    dec = jnp.exp(jnp.where(ri >= ci, gcol - grow, NEG))
    dec_strict = jnp.where(ri > ci, dec, 0.0)
    eg = jnp.exp(gcol)
    kb = k * beta
    t = inv(_mmb(kb, k, "nt") * dec_strict)
    u_hat = _mmh(t, v * beta, "nn")
    w = _mmh(t, kb * eg, "nn")
    u = u_hat - _mmb(w, s, "nn")
    o = _mmb(q * eg, s, "nn") + _mmb(_mmb(q, k, "nt") * dec, u, "nn")
    glast = jnp.sum(jnp.where(_iota((1, c), 1) == c - 1, grow, 0.0), axis=-1, keepdims=True)
    s_new = s * jnp.exp(glast) + _mmb(k * jnp.exp(glast - gcol), u, "tn")
    return o, s_new


def _gdn_specs(lp, reverse):
    n = lp // CHUNK
    pos = (lambda c: n - 1 - c) if reverse else (lambda c: c)
    wide = pl.BlockSpec((CHUNK, GDN_W), lambda c: (pos(c), 0))
    lanes = pl.BlockSpec((CHUNK, LANE), lambda c: (pos(c), 0))
    row = pl.BlockSpec((GDN_H, 1, 1, CHUNK), lambda c: (0, pos(c), 0, 0))
    st = pl.BlockSpec((GDN_H, 1, GDN_DH, GDN_DH), lambda c: (0, pos(c), 0, 0))
    return n, wide, lanes, row, st


def _heads(ref):
    return jnp.stack([ref[:, h * GDN_DH:(h + 1) * GDN_DH] for h in range(GDN_H)])


def _put_heads(ref, val):
    for h in range(GDN_H):
        ref[:, h * GDN_DH:(h + 1) * GDN_DH] = val[h]


def _head_cols(blk, lane0):
    return jnp.stack([_lane_col(blk, lane0 + h) for h in range(GDN_H)])


def _gdn_fwd(q, k, v, bg, grow):
    lp = q.shape[0]
    n, wide, lanes, row, st = _gdn_specs(lp, False)

    def body(q_ref, k_ref, v_ref, bg_ref, gr_ref, o_ref, sp_ref, t_ref, s_scr):
        @pl.when(pl.program_id(0) == 0)
        def _():
            s_scr[...] = jnp.zeros_like(s_scr)

        def inv(m):
            t = _inv_unit_lower_raw(m)
            t_ref[:, 0] = t
            return t

        s = s_scr[...]
        sp_ref[:, 0] = s
        bg_blk = bg_ref[...]
        o, s_new = _gdn_chunk(_heads(q_ref), _heads(k_ref), _heads(v_ref), _head_cols(bg_blk, 8),
                              _head_cols(bg_blk, 16), gr_ref[:, 0], s, inv)
        _put_heads(o_ref, o)
        s_scr[...] = s_new

    tri = pl.BlockSpec((GDN_H, 1, CHUNK, CHUNK), lambda c: (0, c, 0, 0))
    return _call(body, name="gdn_fwd", grid=(n,),
                 out_shape=[jax.ShapeDtypeStruct((lp, GDN_W), F32),
                            jax.ShapeDtypeStruct((GDN_H, n, GDN_DH, GDN_DH), F32),
                            jax.ShapeDtypeStruct((GDN_H, n, CHUNK, CHUNK), F32)],
                 in_specs=[wide, wide, wide, lanes, row], out_specs=[wide, st, tri],
                 scratch=[pltpu.VMEM((GDN_H, GDN_DH, GDN_DH), F32)], sem=("arbitrary",))(q, k, v, bg, grow)


def _gdn_bwd(q, k, v, bg, grow, sprev, tinv, do):
    lp = q.shape[0]
    n, wide, lanes, row, st = _gdn_specs(lp, True)

    def body(q_ref, k_ref, v_ref, bg_ref, gr_ref, sp_ref, t_ref, do_ref,
             dq_ref, dk_ref, dv_ref, dbg_ref, dgr_ref, ds_scr):
        @pl.when(pl.program_id(0) == 0)
        def _():
            ds_scr[...] = jnp.zeros_like(ds_scr)

        t_saved = t_ref[:, 0]
        fn = functools.partial(_gdn_chunk, inv=lambda m: _inv_given(m, t_saved))
        bg_blk = bg_ref[...]
        _, vjp = jax.vjp(fn, _heads(q_ref), _heads(k_ref), _heads(v_ref), _head_cols(bg_blk, 8),
                         _head_cols(bg_blk, 16), gr_ref[:, 0], sp_ref[:, 0])
        dq, dk, dv, db, dgc, dgr, ds = vjp((_heads(do_ref), ds_scr[...]))
        _put_heads(dq_ref, dq)
        _put_heads(dk_ref, dk)
        _put_heads(dv_ref, dv)
        lane = _iota((CHUNK, LANE), 1)
        dbg = jnp.zeros((CHUNK, LANE), F32)
        for h in range(GDN_H):
            dbg = jnp.where(lane == 8 + h, db[h], jnp.where(lane == 16 + h, dgc[h], dbg))
        dbg_ref[...] = dbg
        dgr_ref[:, 0] = dgr
        ds_scr[...] = ds

    wshape = jax.ShapeDtypeStruct((lp, GDN_W), F32)
    tri = pl.BlockSpec((GDN_H, 1, CHUNK, CHUNK), lambda c: (0, n - 1 - c, 0, 0))
    return _call(body, name="gdn_bwd", grid=(n,),
                 out_shape=[wshape, wshape, wshape, jax.ShapeDtypeStruct((lp, LANE), F32),
                            jax.ShapeDtypeStruct((GDN_H, n, 1, CHUNK), F32)],
                 in_specs=[wide, wide, wide, lanes, row, st, tri, wide], out_specs=[wide, wide, wide, lanes, row],
                 scratch=[pltpu.VMEM((GDN_H, GDN_DH, GDN_DH), F32)], sem=("arbitrary",))(
                     q, k, v, bg, grow, sprev, tinv, do)


def _halo_prev(width, colblock, tm):
    return pl.BlockSpec((8, width), lambda i, cb=colblock: (jnp.maximum(i * (tm // 8) - 1, 0), cb))


def _gdn_act(proj, small, conv_w, alog_row, dtb_row):
    lp = proj.shape[0]
    tm = _pick(lp, ROW_TILES)

    def body(xq, xk, xv, hq, hk, hv, wq, wk, wv, sm, al, dt, q_ref, k_ref, v_ref, bg_ref):
        first = (pl.program_id(0) > 0).astype(F32)
        cs = [_causal_conv(x[...], h[...] * first, w[...]) for x, h, w in ((xq, hq, wq), (xk, hk, wk), (xv, hv, wv))]
        q, k, v, bg = _gdn_act_fn(cs[0], cs[1], cs[2], sm[...], al[...], dt[...])
        q_ref[...], k_ref[...], v_ref[...], bg_ref[...] = q, k, v, bg

    wide = jax.ShapeDtypeStruct((lp, GDN_W), F32)
    wspec = [pl.BlockSpec((4, GDN_W), lambda i, c=c: (0, c)) for c in range(3)]
    return _call(body, name="gdn_act", grid=(lp // tm,),
                 out_shape=[wide, wide, wide, jax.ShapeDtypeStruct((lp, LANE), F32)],
                 in_specs=[_row_spec(GDN_W, c, tm) for c in range(3)] + [_halo_prev(GDN_W, c, tm) for c in range(3)]
                 + wspec + [_row_spec(LANE, 0, tm), _full_spec(alog_row), _full_spec(dtb_row)],
                 out_specs=[_row_spec(GDN_W, 0, tm)] * 3 + [_row_spec(LANE, 0, tm)], sem=("parallel",))(
                     proj, proj, proj, proj, proj, proj, conv_w, conv_w, conv_w, small, alog_row, dtb_row)


def _gdn_act_bwd(proj, small, conv_w, alog_row, dtb_row, dq, dk, dv, dbg):
    lp = proj.shape[0]
    tm = ROW_TILE

    def body(xq, xk, xv, hq, hk, hv, wq, wk, wv, sm, al, dt, dq_r, dk_r, dv_r, dbg_r,
             dc_ref, dsm_ref, dal_ref, ddt_ref, dw_ref):
        i = pl.program_id(0)
        first = (i > 0).astype(F32)
        xs = [(x[...], h[...] * first, w[...]) for x, h, w in ((xq, hq, wq), (xk, hk, wk), (xv, hv, wv))]
        cs = [_causal_conv(*t) for t in xs]
        _, vjp = jax.vjp(_gdn_act_fn, cs[0], cs[1], cs[2], sm[...], al[...], dt[...])
        dcq, dck, dcv, dsm, dal, ddt = vjp((dq_r[...], dk_r[...], dv_r[...], dbg_r[...]))
        dsm_ref[...] = dsm

        @pl.when(i == 0)
        def _():
            dal_ref[...] = jnp.zeros_like(dal_ref)
            ddt_ref[...] = jnp.zeros_like(ddt_ref)
            dw_ref[...] = jnp.zeros_like(dw_ref)

        dal_ref[...] += dal
        ddt_ref[...] += ddt
        for c, (dc, (x, h, w)) in enumerate(zip((dcq, dck, dcv), xs)):
            dc_ref[:, c * GDN_W:(c + 1) * GDN_W] = dc
            rows = [jnp.sum(_shift_down(x, h, 3 - kk) * dc, axis=0, keepdims=True) for kk in range(4)]
            dw_ref[:, c * GDN_W:(c + 1) * GDN_W] += jnp.concatenate(rows, axis=0)

    wspec = [pl.BlockSpec((4, GDN_W), lambda i, c=c: (0, c)) for c in range(3)]
    row128 = jax.ShapeDtypeStruct((1, LANE), F32)
    return _call(body, name="gdn_act_bwd", grid=(lp // tm,),
                 out_shape=[jax.ShapeDtypeStruct((lp, 3 * GDN_W), F32), jax.ShapeDtypeStruct((lp, LANE), F32),
                            row128, row128, jax.ShapeDtypeStruct((4, 3 * GDN_W), F32)],
                 in_specs=[_row_spec(GDN_W, c, tm) for c in range(3)] + [_halo_prev(GDN_W, c, tm) for c in range(3)]
                 + wspec + [_row_spec(LANE, 0, tm), _full_spec(alog_row), _full_spec(dtb_row)]
                 + [_row_spec(GDN_W, 0, tm)] * 3 + [_row_spec(LANE, 0, tm)],
                 out_specs=[_row_spec(3 * GDN_W, 0, tm), _row_spec(LANE, 0, tm),
                            _full_spec(alog_row), _full_spec(dtb_row), pl.BlockSpec((4, 3 * GDN_W), lambda i: (0, 0))],
                 sem=("arbitrary",))(proj, proj, proj, proj, proj, proj, conv_w, conv_w, conv_w, small,
                                     alog_row, dtb_row, dq, dk, dv, dbg)


def _ffn_act(up_pre, conv_w):
    lp = up_pre.shape[0]
    tm = _pick(lp, ROW_TILES)

    def body(xg, xv, hg, hv, wg, wv, a_ref):
        first = (pl.program_id(0) > 0).astype(F32)
        ug = _causal_conv(xg[...], hg[...] * first, wg[...])
        uv = _causal_conv(xv[...], hv[...] * first, wv[...])
        a_ref[...] = (_silu(ug) * uv).astype(a_ref.dtype)

    wspec = [pl.BlockSpec((3, D_FF), lambda i, c=c: (0, c)) for c in range(2)]
    return _call(body, name="ffn_act", grid=(lp // tm,), out_shape=jax.ShapeDtypeStruct((lp, D_FF), BF16),
                 in_specs=[_row_spec(D_FF, c, tm) for c in range(2)] + [_halo_prev(D_FF, c, tm) for c in range(2)] + wspec,
                 out_specs=_row_spec(D_FF, 0, tm), sem=("parallel",))(up_pre, up_pre, up_pre, up_pre, conv_w, conv_w)


def _ffn_act_bwd(up_pre, conv_w, dact, tm=ROW_TILE):
    lp = up_pre.shape[0]

    def body(xg, xv, hg, hv, wg, wv, da, du_ref, dw_ref):
        i = pl.program_id(0)
        first = (i > 0).astype(F32)
        xs = [(x[...], h[...] * first, w[...]) for x, h, w in ((xg, hg, wg), (xv, hv, wv))]
        ug, uv = [_causal_conv(*t) for t in xs]
        _, vjp = jax.vjp(lambda a, b: _silu(a) * b, ug, uv)
        dus = vjp(da[...].astype(F32))

        @pl.when(i == 0)
        def _():
            dw_ref[...] = jnp.zeros_like(dw_ref)

        for c, (du, (x, h, w)) in enumerate(zip(dus, xs)):
            du_ref[:, c * D_FF:(c + 1) * D_FF] = du
            rows = [jnp.sum(_shift_down(x, h, 2 - kk) * du, axis=0, keepdims=True) for kk in range(3)]
            dw_ref[:, c * D_FF:(c + 1) * D_FF] += jnp.concatenate(rows, axis=0)

    wspec = [pl.BlockSpec((3, D_FF), lambda i, c=c: (0, c)) for c in range(2)]
    return _call(body, name="ffn_act_bwd", grid=(lp // tm,),
                 out_shape=[jax.ShapeDtypeStruct((lp, 2 * D_FF), F32), jax.ShapeDtypeStruct((3, 2 * D_FF), F32)],
                 in_specs=[_row_spec(D_FF, c, tm) for c in range(2)] + [_halo_prev(D_FF, c, tm) for c in range(2)]
                 + wspec + [_row_spec(D_FF, 0, tm)],
                 out_specs=[_row_spec(2 * D_FF, 0, tm), pl.BlockSpec((3, 2 * D_FF), lambda i: (0, 0))],
                 sem=("arbitrary",))(up_pre, up_pre, up_pre, up_pre, conv_w, conv_w, dact)


def _conv_bwd_x(dy, w, pad_rows, width, name):
    lp, ctot = dy.shape
    tm = _pick(lp, ROW_TILES)
    nt = lp // tm
    kk = w.shape[0]

    def body(d_ref, h_ref, w_ref, o_ref):
        i = pl.program_id(0)
        last = (i < nt - 1).astype(F32)
        d, h, wv = d_ref[...], h_ref[...] * last, w_ref[...]
        y = d * wv[kk - 1:kk, :]
        for k in range(kk - 1):
            y = y + _shift_up(d, h, kk - 1 - k) * wv[k:k + 1, :]
        valid = (i * tm + _iota((tm, 1), 0)) >= pad_rows
        o_ref[...] = jnp.where(valid, y, 0.0).astype(o_ref.dtype)

    return _call(body, name=name, grid=(nt, ctot // width), out_shape=jax.ShapeDtypeStruct((lp, ctot), BF16),
                 in_specs=[pl.BlockSpec((tm, width), lambda i, c: (i, c)),
                           pl.BlockSpec((8, width), lambda i, c: (jnp.minimum((i + 1) * (tm // 8), lp // 8 - 1), c)),
                           pl.BlockSpec((kk, width), lambda i, c: (0, c))],
                 out_specs=pl.BlockSpec((tm, width), lambda i, c: (i, c)), sem=("parallel", "parallel"))(dy, dy, w)


def _loss_head(h_res, target, row_start, tm=ROW_TILE):
    lp, d = h_res.shape
    t0 = row_start // tm

    def body(h_ref, t_ref, dy_ref, loss_ref):
        i = pl.program_id(0)

        @pl.when(i == 0)
        def _():
            loss_ref[...] = jnp.zeros_like(loss_ref)

        live = (i >= t0).astype(F32)
        err = (h_ref[...] - t_ref[...]) * live
        dy_ref[...] = err * (1.0 / d)
        loss_ref[...] += 0.5 / d * jnp.sum(err * err)

    return _call(body, name="loss_head", grid=(lp // tm,),
                 out_shape=[jax.ShapeDtypeStruct((lp, d), F32), jax.ShapeDtypeStruct((8, LANE), F32)],
                 in_specs=[pl.BlockSpec((tm, d), lambda i: (i, 0)),
                           pl.BlockSpec((tm, d), lambda i: (jnp.maximum(i - t0, 0), 0))],
                 out_specs=[pl.BlockSpec((tm, d), lambda i: (i, 0)), pl.BlockSpec((8, LANE), lambda i: (0, 0))],
                 sem=("arbitrary",))(h_res, target)


def _sum_adamw(parts, w, m, v, name):
    a, r, c = w.shape
    tm = _pick(r, (256, 128, 64, 32, 16))
    bc1 = 1.0 - ADAM_B1 ** ADAM_STEP
    bc2 = 1.0 - ADAM_B2 ** ADAM_STEP

    def body(p_ref, w_ref, m_ref, v_ref, g_ref, d_ref, nm_ref, nv_ref):
        g = p_ref[0, 0].astype(F32)
        for s in range(1, N_DEV):
            g = g + p_ref[s, 0].astype(F32)
        nm = ADAM_B1 * m_ref[0] + (1.0 - ADAM_B1) * g
        nv = ADAM_B2 * v_ref[0] + (1.0 - ADAM_B2) * (g * g)
        g_ref[0] = g
        nm_ref[0] = nm
        nv_ref[0] = nv
        d_ref[0] = -ADAM_LR * ((nm / bc1) / (jnp.sqrt(nv / bc2) + ADAM_EPS) + ADAM_WD * w_ref[0])

    spec = pl.BlockSpec((1, tm, c), lambda l, i: (l, i, 0))
    shp = jax.ShapeDtypeStruct((a, r, c), F32)
    return _call(body, name=name, grid=(a, r // tm), out_shape=[shp] * 4,
                 in_specs=[pl.BlockSpec((N_DEV, 1, tm, c), lambda l, i: (0, l, i, 0)), spec, spec, spec],
                 out_specs=[spec] * 4, sem=("parallel", "parallel"))(parts, w, m, v)


_ANY = pl.BlockSpec(memory_space=pl.ANY)


def _all_gather(block, name):
    def body(x_ref, out_ref, send_sems, recv_sems, local_sem):
        x, y, c = lax.axis_index("x"), lax.axis_index("y"), lax.axis_index("c")
        me, sibling = (x, y, c), (x, y, 1 - c)
        chips = [(1 - x, y), (x, 1 - y), (1 - x, 1 - y)]

        def slot(px, py, pc):
            return out_ref.at[4 * px + 2 * py + pc]

        def copy(k, blk, to, src=None):
            return pltpu.make_async_remote_copy(
                src_ref=slot(*blk) if src is None else src, dst_ref=slot(*blk),
                send_sem=send_sems.at[k], recv_sem=recv_sems.at[k], device_id=to, device_id_type=MESH)

        mine = pltpu.make_async_copy(x_ref, slot(*me), local_sem)
        mine.start()
        first = [copy(0, me, sibling, src=x_ref)]
        first += [copy(1 + j, me, (*chip, c), src=x_ref) for j, chip in enumerate(chips)]
        for cp in first:
            cp.start()
        passed = [copy(4 + j, (*chip, c), sibling) for j, chip in enumerate(chips)]
        for j, chip in enumerate(chips):
            copy(1 + j, (*chip, c), me).wait_recv()
            passed[j].start()
        copy(0, sibling, me).wait_recv()
        for j, chip in enumerate(chips):
            copy(4 + j, (*chip, 1 - c), me).wait_recv()
        for cp in first + passed:
            cp.wait_send()
        mine.wait()

    return pl.pallas_call(
        body, name=name, out_shape=jax.ShapeDtypeStruct((N_DEV,) + block.shape, block.dtype),
        in_specs=[_ANY], out_specs=_ANY,
        scratch_shapes=[pltpu.SemaphoreType.DMA((7,)), pltpu.SemaphoreType.DMA((7,)), pltpu.SemaphoreType.DMA],
    )(block)


def _all_to_all(src, name):
    def body(s_ref, o_ref, send_sems, recv_sems, local_sem):
        x, y, c = lax.axis_index("x"), lax.axis_index("y"), lax.axis_index("c")
        me = 4 * x + 2 * y + c
        mine = pltpu.make_async_copy(s_ref.at[me], o_ref.at[me], local_sem)
        mine.start()
        copies = []
        for k in range(1, N_DEV):
            px = 1 - x if k & 4 else x
            py = 1 - y if k & 2 else y
            pc = 1 - c if k & 1 else c
            peer = 4 * px + 2 * py + pc
            copies.append((pltpu.make_async_remote_copy(
                src_ref=s_ref.at[peer], dst_ref=o_ref.at[me], send_sem=send_sems.at[k - 1],
                recv_sem=recv_sems.at[k - 1], device_id=(px, py, pc), device_id_type=MESH), peer, k))
        for cp, _, _ in copies:
            cp.start()
        for cp, peer, k in copies:
            cp.wait_send()
            pltpu.make_async_remote_copy(
                src_ref=s_ref.at[peer], dst_ref=o_ref.at[peer], send_sem=send_sems.at[k - 1],
                recv_sem=recv_sems.at[k - 1], device_id=(x, y, c), device_id_type=MESH).wait_recv()
        mine.wait()

    return pl.pallas_call(
        body, name=name, out_shape=jax.ShapeDtypeStruct(src.shape, src.dtype), in_specs=[_ANY], out_specs=_ANY,
        scratch_shapes=[pltpu.SemaphoreType.DMA((7,)), pltpu.SemaphoreType.DMA((7,)), pltpu.SemaphoreType.DMA],
    )(src)


_HBM = pl.BlockSpec(memory_space=pltpu.HBM)
_SEM = pl.BlockSpec(memory_space=pltpu.SEMAPHORE)
_EFFECT = pltpu.SideEffectType.DATAFLOW_SIDE_EFFECTING


def _peer_list(x, y, c):
    return [(1 - x if k & 4 else x, 1 - y if k & 2 else y, 1 - c if k & 1 else c) for k in range(1, N_DEV)]


def _exchange_copies(s_refs, l_refs, send_sems, recv_sems, landing_of_peer, same_block):
    x, y, c = lax.axis_index("x"), lax.axis_index("y"), lax.axis_index("c")
    me = 4 * x + 2 * y + c
    out = []
    for wi, (s_ref, l_ref) in enumerate(zip(s_refs, l_refs)):
        for k, (px, py, pc) in enumerate(_peer_list(x, y, c)):
            peer = 4 * px + 2 * py + pc
            idx = wi * (N_DEV - 1) + k
            out.append(pltpu.make_async_remote_copy(
                src_ref=s_ref if same_block else s_ref.at[peer], dst_ref=l_ref.at[peer if landing_of_peer else me],
                send_sem=send_sems.at[idx], recv_sem=recv_sems.at[idx], device_id=(px, py, pc), device_id_type=MESH))
    return out


def _exchange_start(srcs, name, same_block=False):
    nw = len(srcs)
    ncp = nw * (N_DEV - 1)

    def body(*refs):
        for cp in _exchange_copies(refs[:nw], refs[nw:2 * nw], refs[2 * nw], refs[2 * nw + 1], False, same_block):
            cp.start()
        refs[-1][...] = jnp.zeros_like(refs[-1])

    land_shapes = [((N_DEV,) + s.shape) if same_block else s.shape for s in srcs]
    hbm = [pltpu.HBM(s.shape, s.dtype) for s in srcs]
    hbm_l = [pltpu.HBM(ls, s.dtype) for ls, s in zip(land_shapes, srcs)]
    outs = pl.pallas_call(
        body, name=name,
        out_shape=(pltpu.SemaphoreType.DMA((ncp,)), pltpu.SemaphoreType.DMA((ncp,)), *hbm, *hbm_l,
                   jax.ShapeDtypeStruct((8, LANE), F32)),
        in_specs=[_HBM] * (2 * nw), out_specs=(_SEM, _SEM, *[_HBM] * (2 * nw), pl.BlockSpec(memory_space=pltpu.VMEM)),
        input_output_aliases={i: 2 + i for i in range(2 * nw)},
        compiler_params=pltpu.CompilerParams(has_side_effects=_EFFECT),
    )(*[pltpu.with_memory_space_constraint(s, pltpu.HBM) for s in srcs],
      *[pltpu.with_memory_space_constraint(lax.empty(ls, s.dtype), pltpu.HBM) for ls, s in zip(land_shapes, srcs)])
    return outs[0], outs[1], list(outs[2:2 + nw]), list(outs[2 + nw:2 + 2 * nw]), outs[-1]


def _exchange_wait(send_sems, recv_sems, srcs, lands, after, name, same_block=False):
    nw = len(srcs)

    def body(*refs):
        for cp in _exchange_copies(refs[:nw], refs[nw:2 * nw], refs[2 * nw], refs[2 * nw + 1], True, same_block):
            cp.wait_send()
            cp.wait_recv()

    hbm = [pltpu.HBM(a.shape, a.dtype) for a in list(srcs) + list(lands)]
    outs = pl.pallas_call(
        body, name=name, out_shape=tuple(hbm),
        in_specs=[_HBM] * (2 * nw) + [_SEM, _SEM, pl.BlockSpec(memory_space=pl.ANY)], out_specs=tuple([_HBM] * (2 * nw)),
        input_output_aliases={i: i for i in range(2 * nw)},
        compiler_params=pltpu.CompilerParams(has_side_effects=_EFFECT),
    )(*srcs, *lands, send_sems, recv_sems, after)
    return list(outs[:nw]), list(outs[nw:])


def _pack(blocks, width, dtype, row_mult):
    flat = jnp.concatenate([b.astype(dtype).reshape(-1) for b in blocks])
    per = width * row_mult
    total = -(-flat.shape[0] // per) * per
    return jnp.pad(flat, (0, total - flat.shape[0])).reshape(total // width, width)


def _pack_dest(fulls, axes, width, dtype, row_mult):
    rows = []
    for f, ax in zip(fulls, axes):
        f = f.astype(dtype)
        if ax is None:
            rows.append(jnp.broadcast_to(f.reshape(1, -1), (N_DEV, f.size)))
        else:
            shp = f.shape
            f = f.reshape(shp[:ax] + (N_DEV, shp[ax] // N_DEV) + shp[ax + 1:])
            rows.append(jnp.moveaxis(f, ax, 0).reshape(N_DEV, -1))
    flat = jnp.concatenate(rows, axis=1)
    per = width * row_mult
    total = -(-flat.shape[1] // per) * per
    return jnp.pad(flat, ((0, 0), (0, total - flat.shape[1]))).reshape(N_DEV, total // width, width)


def _unpack(packed, shapes):
    flat = packed.reshape(-1)
    out, off = [], 0
    for s in shapes:
        n = 1
        for d in s:
            n *= d
        out.append(flat[off:off + n].reshape(s))
        off += n
    return out


def _unpack_gathered(gathered, shapes, axes):
    flat = gathered.reshape(N_DEV, -1)
    out, off = [], 0
    for s, ax in zip(shapes, axes):
        n = 1
        for d in s:
            n *= d
        blk = jnp.moveaxis(flat[:, off:off + n].reshape((N_DEV,) + tuple(s)), 0, ax)
        out.append(blk.reshape(tuple(s[:ax]) + (N_DEV * s[ax],) + tuple(s[ax + 1:])))
        off += n
    return out


def _shard_cols(blocks, a, b):
    shard = blocks[0].shape[1]
    out = []
    while a < b:
        d = a // shard
        hi = min(b, (d + 1) * shard)
        out.append(blocks[d][:, a - d * shard:hi - d * shard])
        a = hi
    return out


def _permute_w_in(blocks):
    main = jnp.concatenate(_shard_cols(blocks, O_GQ, O_BL) + _shard_cols(blocks, O_GZ, O_END)
                           + _shard_cols(blocks, O_FQ, O_FL), axis=1)
    pad = jnp.zeros((blocks[0].shape[0], LANE - 24), blocks[0].dtype)
    small = jnp.concatenate(_shard_cols(blocks, O_FL, O_GQ) + _shard_cols(blocks, O_BL, O_GZ) + [pad], axis=1)
    return main, small


_W_IN_SEGS = ((O_FQ, O_FL, True, C_FQ), (O_FL, O_GQ, False, 0), (O_GQ, O_BL, True, C_GQ), (O_BL, O_GZ, False, 8),
              (O_GZ, O_END, True, C_GZ))


def _unpermute_cols(main, small, a, b):
    out = []
    for s0, s1, is_main, t0 in _W_IN_SEGS:
        lo, hi = max(a, s0), min(b, s1)
        if lo < hi:
            out.append((main if is_main else small)[:, t0 + lo - s0:t0 + hi - s0])
    return jnp.concatenate(out, axis=1)


def _lanes(vec, start):
    return jnp.pad(vec.astype(F32), (start, LANE - start - vec.shape[0])).reshape(1, LANE)


BIG = ("w_in", "w_branch_a", "w_branch_b", "w_out", "w_up", "w_down")
BIG_AXES = (2, 2, 1, 1, 2, 1)
SHARDED_SMALL = ("meta_tokens", "gdn_conv_w", "ffn_conv_w")
SHARDED_SMALL_AXES = (1, 2, 2)
REPL = ("norm1_g", "fox_f_bias", "fox_q_norm_g", "fox_k_norm_g", "gdn_a_log", "gdn_dt_bias", "gdn_norm_g", "norm2_g")
ORDER = ("meta_tokens", "norm1_g", "w_in", "fox_f_bias", "fox_q_norm_g", "fox_k_norm_g", "gdn_conv_w", "gdn_a_log",
         "gdn_dt_bias", "gdn_norm_g", "w_branch_a", "w_branch_b", "w_out", "norm2_g", "w_up", "ffn_conv_w", "w_down")


def _layer_fwd(h_res, wl, pad_rows):
    lp = h_res.shape[0]
    sv = {"res_in": h_res}
    (h1,) = _rowwise(_rmsnorm_fn, [(h_res, D, 0)], [wl["norm1_g"]], [(D, BF16)], "rmsnorm1")
    proj = _matmul(h1, wl["w_main"], "nn", F32, name="mm_in")
    small = _matmul(h1, wl["w_small"], "nn", F32, name="mm_in_small")
    sv.update(h1=h1, proj=proj, small=small)

    fox_fn = functools.partial(_fox_prep_fn, pad_rows)
    qh, kh, logf = _rowwise(fox_fn, [(proj, FOX_W, C_FQ // FOX_W), (proj, FOX_W, C_FK // FOX_W), (small, LANE, 0)],
                            [wl["qg"], wl["kg"], wl["fb"]], [(FOX_W, BF16), (FOX_W, BF16), (LANE, F32)], "fox_prep")
    fsum = _cumsum_rows(logf, False, "fox_cumsum")
    frow = fsum[:, :8].T.reshape(8, 1, lp)
    o_a, lse = _fox_fwd(qh, kh, proj, fsum, frow, pad_rows, C_FV // LANE)
    y_a = _matmul(o_a, wl["w_branch_a"], "nn", F32, name="mm_branch_a")
    sv.update(qh=qh, kh=kh, fsum=fsum, frow=frow, o_a=o_a, lse=lse)

    gq, gk, gv, bg = _gdn_act(proj, small, wl["gdn_conv_w"], wl["alog"], wl["dtb"])
    grow = bg[:, 16:24].T.reshape(8, lp // CHUNK, 1, CHUNK)
    o_raw, sprev, tinv = _gdn_fwd(gq, gk, gv, bg, grow)
    (o_b,) = _rowwise(_gdn_post_fn, [(o_raw, GDN_W, 0), (proj, GDN_W, C_GZ // GDN_W)], [wl["gn"]], [(GDN_W, BF16)],
                      "gdn_post")
    y_b = _matmul(o_b, wl["w_branch_b"], "nn", F32, name="mm_branch_b")
    sv.update(gq=gq, gk=gk, gv=gv, bg=bg, grow=grow, o_raw=o_raw, sprev=sprev, tinv=tinv, o_b=o_b)

    (mixed,) = _rowwise(_merge_fn, [(proj, D, C_GATE // D), (proj, D, C_GATE // D + 1), (y_a, D, 0), (y_b, D, 0)], [],
                        [(D, BF16)], "merge")
    res_mid = _matmul(mixed, wl["w_out"], "nn", F32, add=h_res, name="mm_out")
    sv.update(y_a=y_a, y_b=y_b, mixed=mixed, res_mid=res_mid)

    (h2,) = _rowwise(_rmsnorm_fn, [(res_mid, D, 0)], [wl["norm2_g"]], [(D, BF16)], "rmsnorm2")
    up_pre = _matmul(h2, wl["w_up"], "nn", F32, name="mm_up")
    act = _ffn_act(up_pre, wl["ffn_conv_w"])
    out = _matmul(act, wl["w_down"], "nn", F32, add=res_mid, name="mm_down")
    sv.update(h2=h2, up_pre=up_pre, act=act)
    return out, sv


def _layer_bwd(dres, wl, sv, pad_rows):
    lp = dres.shape[0]
    gw = {}
    gw["w_down"] = _matmul(sv["act"], dres, "tn", F32, name="mm_dw_down")
    dact = _matmul(dres, wl["w_down"], "nt", BF16, name="mm_dact")
    dup, gw["ffn_conv_w"] = _ffn_act_bwd(sv["up_pre"], wl["ffn_conv_w"], dact)
    dup_pre = _conv_bwd_x(dup, wl["ffn_conv_w"], pad_rows, D_FF, "ffn_conv_bwd")
    gw["w_up"] = _matmul(sv["h2"], dup_pre, "tn", F32, name="mm_dw_up")
    dh2 = _matmul(dup_pre, wl["w_up"], "nt", F32, name="mm_dh2")
    (dmid,), (gw["norm2_g"],) = _rowwise_bwd(_rmsnorm_fn, [(sv["res_mid"], D, 0)], [wl["norm2_g"]], [(dh2, D, 0)],
                                             "rmsnorm2_bwd", pad_rows, [F32], adds=[(dres, D, 0)])
    gw["w_out"] = _matmul(sv["mixed"], dmid, "tn", F32, name="mm_dw_out")
    dmixed = _matmul(dmid, wl["w_out"], "nt", F32, name="mm_dmixed")
    proj, small = sv["proj"], sv["small"]
    (dg0, dg1, dya, dyb), _ = _rowwise_bwd(
        _merge_fn, [(proj, D, C_GATE // D), (proj, D, C_GATE // D + 1), (sv["y_a"], D, 0), (sv["y_b"], D, 0)], [],
        [(dmixed, D, 0)], "merge_bwd", pad_rows, [BF16, BF16, BF16, BF16])
    gw["w_branch_a"] = _matmul(sv["o_a"], dya, "tn", F32, name="mm_dw_a")
    do_a = _matmul(dya, wl["w_branch_a"], "nt", BF16, name="mm_do_a")
    gw["w_branch_b"] = _matmul(sv["o_b"], dyb, "tn", F32, name="mm_dw_b")
    do_b = _matmul(dyb, wl["w_branch_b"], "nt", F32, name="mm_do_b")

    (do_raw, dgz), (gw["gn"],) = _rowwise_bwd(_gdn_post_fn, [(sv["o_raw"], GDN_W, 0), (proj, GDN_W, C_GZ // GDN_W)],
                                              [wl["gn"]], [(do_b, GDN_W, 0)], "gdn_post_bwd", pad_rows, [F32, BF16])
    dgq, dgk, dgv, dbg, dgrow = _gdn_bwd(sv["gq"], sv["gk"], sv["gv"], sv["bg"], sv["grow"], sv["sprev"],
                                         sv["tinv"], do_raw)
    dbg = dbg + jnp.pad(dgrow.reshape(8, lp).T, ((0, 0), (16, LANE - 24)))
    dconv, dsmall_g, gw["alog"], gw["dtb"], gw["gdn_conv_w"] = _gdn_act_bwd(
        proj, small, wl["gdn_conv_w"], wl["alog"], wl["dtb"], dgq, dgk, dgv, dbg)
    dqkv = _conv_bwd_x(dconv, wl["gdn_conv_w"], pad_rows, GDN_W, "gdn_conv_bwd")

    (delta,) = _rowwise(_fox_delta_fn, [(sv["o_a"], FOX_W, 0), (do_a, FOX_W, 0)], [], [(LANE, F32)], "fox_delta")
    dqh, dkh, dvh, dfq, dfk = _fox_bwd(sv["qh"], sv["kh"], proj, sv["fsum"], sv["frow"], do_a, sv["lse"], delta,
                                       pad_rows, C_FV // LANE)
    df8 = dfq.reshape(lp, FOX_W // LANE, LANE)[:, :, :2].reshape(lp, 8) + dfk.reshape(8, lp).T
    dlogf = _cumsum_rows(jnp.pad(df8, ((0, 0), (0, LANE - 8))), True, "fox_cumsum_bwd")
    fox_fn = functools.partial(_fox_prep_fn, pad_rows)
    (dfq_p, dfk_p, dsmall_f), (gw["qg"], gw["kg"], gw["fb"]) = _rowwise_bwd(
        fox_fn, [(proj, FOX_W, C_FQ // FOX_W), (proj, FOX_W, C_FK // FOX_W), (small, LANE, 0)],
        [wl["qg"], wl["kg"], wl["fb"]], [(dqh, FOX_W, 0), (dkh, FOX_W, 0), (dlogf, LANE, 0)],
        "fox_prep_bwd", pad_rows, [BF16, BF16, F32], adds=[None, None, (dsmall_g, LANE, 0)])

    dproj = jnp.concatenate([dqkv, dgz, dg0, dg1, dfq_p, dfk_p, dvh], axis=1)
    gw["w_main"] = _matmul(sv["h1"], dproj, "tn", F32, name="mm_dw_main")
    gw["w_small"] = _matmul(sv["h1"], dsmall_f, "tn", F32, name="mm_dw_small")
    dh1 = _matmul(dproj, wl["w_main"], "nt", F32, name="mm_dh1")
    dh1 = _matmul(dsmall_f, wl["w_small"], "nt", F32, add=dh1, name="mm_dh1_small")
    (din,), (gw["norm1_g"],) = _rowwise_bwd(_rmsnorm_fn, [(sv["res_in"], D, 0)], [wl["norm1_g"]], [(dh1, D, 0)],
                                            "rmsnorm1_bwd", pad_rows, [F32], adds=[(dmid, D, 0)])
    return din, gw


def kernel(x, meta_tokens, norm1_g, w_in, fox_f_bias, fox_q_norm_g, fox_k_norm_g, gdn_conv_w, gdn_a_log, gdn_dt_bias, gdn_norm_g, w_branch_a, w_branch_b, w_out, norm2_g, w_up, ffn_conv_w, w_down, loss_target, m_meta_tokens, m_norm1_g, m_w_in, m_fox_f_bias, m_fox_q_norm_g, m_fox_k_norm_g, m_gdn_conv_w, m_gdn_a_log, m_gdn_dt_bias, m_gdn_norm_g, m_w_branch_a, m_w_branch_b, m_w_out, m_norm2_g, m_w_up, m_ffn_conv_w, m_w_down, v_meta_tokens, v_norm1_g, v_w_in, v_fox_f_bias, v_fox_q_norm_g, v_fox_k_norm_g, v_gdn_conv_w, v_gdn_a_log, v_gdn_dt_bias, v_gdn_norm_g, v_w_branch_a, v_w_branch_b, v_w_out, v_norm2_g, v_w_up, v_ffn_conv_w, v_w_down):
    w = dict(meta_tokens=meta_tokens, norm1_g=norm1_g, w_in=w_in, fox_f_bias=fox_f_bias, fox_q_norm_g=fox_q_norm_g,
             fox_k_norm_g=fox_k_norm_g, gdn_conv_w=gdn_conv_w, gdn_a_log=gdn_a_log, gdn_dt_bias=gdn_dt_bias,
             gdn_norm_g=gdn_norm_g, w_branch_a=w_branch_a, w_branch_b=w_branch_b, w_out=w_out, norm2_g=norm2_g,
             w_up=w_up, ffn_conv_w=ffn_conv_w, w_down=w_down)
    mom = dict(meta_tokens=m_meta_tokens, norm1_g=m_norm1_g, w_in=m_w_in, fox_f_bias=m_fox_f_bias,
               fox_q_norm_g=m_fox_q_norm_g, fox_k_norm_g=m_fox_k_norm_g, gdn_conv_w=m_gdn_conv_w,
               gdn_a_log=m_gdn_a_log, gdn_dt_bias=m_gdn_dt_bias, gdn_norm_g=m_gdn_norm_g, w_branch_a=m_w_branch_a,
               w_branch_b=m_w_branch_b, w_out=m_w_out, norm2_g=m_norm2_g, w_up=m_w_up, ffn_conv_w=m_ffn_conv_w,
               w_down=m_w_down)
    var = dict(meta_tokens=v_meta_tokens, norm1_g=v_norm1_g, w_in=v_w_in, fox_f_bias=v_fox_f_bias,
               fox_q_norm_g=v_fox_q_norm_g, fox_k_norm_g=v_fox_k_norm_g, gdn_conv_w=v_gdn_conv_w,
               gdn_a_log=v_gdn_a_log, gdn_dt_bias=v_gdn_dt_bias, gdn_norm_g=v_gdn_norm_g, w_branch_a=v_w_branch_a,
               w_branch_b=v_w_branch_b, w_out=v_w_out, norm2_g=v_norm2_g, w_up=v_w_up, ffn_conv_w=v_ffn_conv_w,
               w_down=v_w_down)
    depth = norm1_g.shape[0]
    seq = x.shape[1]
    l_tok = N_META + seq
    lp = -(-l_tok // LANE) * LANE
    pad_rows = lp - l_tok
    row_start = pad_rows + N_META

    me = 4 * lax.axis_index("x") + 2 * lax.axis_index("y") + lax.axis_index("c")
    got = [{n: _all_gather(w[n][0].astype(BF16), "gather_" + n) for n in BIG}]
    started_g = [_exchange_start([w[n][l].astype(BF16) for n in BIG], "gather_start_%d" % l, same_block=True)
                 for l in range(1, depth)]
    small_shapes = [w[n].shape for n in SHARDED_SMALL]
    gathered_s = _all_gather(_pack([w[n] for n in SHARDED_SMALL], LANE, F32, 8), "gather_small")
    full = dict(zip(SHARDED_SMALL, _unpack_gathered(gathered_s, small_shapes, SHARDED_SMALL_AXES)))

    def layer_weights(l, blocks):
        def join(name):
            return jnp.concatenate([blocks[name][d] for d in range(N_DEV)], axis=BIG_AXES[BIG.index(name)] - 1)

        w_main, w_small = _permute_w_in([blocks["w_in"][d] for d in range(N_DEV)])
        return dict(
            w_main=w_main, w_small=w_small, w_branch_a=join("w_branch_a"), w_branch_b=join("w_branch_b"),
            w_out=join("w_out"), w_up=join("w_up"), w_down=join("w_down"),
            gdn_conv_w=full["gdn_conv_w"][l], ffn_conv_w=full["ffn_conv_w"][l],
            norm1_g=norm1_g[l].reshape(1, D), norm2_g=norm2_g[l].reshape(1, D),
            qg=jnp.tile(fox_q_norm_g[l], 8).reshape(1, FOX_W), kg=jnp.tile(fox_k_norm_g[l], 8).reshape(1, FOX_W),
            fb=_lanes(fox_f_bias[l], 0), alog=_lanes(gdn_a_log[l], 16), dtb=_lanes(gdn_dt_bias[l], 16),
            gn=jnp.tile(gdn_norm_g[l], 8).reshape(1, GDN_W))

    def arrived(l, after):
        send_sems, recv_sems, srcs, lands, _ = started_g[l - 1]
        srcs, lands = _exchange_wait(send_sems, recv_sems, srcs, lands, after, "gather_wait_%d" % l, same_block=True)
        return {n: lax.dynamic_update_index_in_dim(ld, sr, me, 0) for n, sr, ld in zip(BIG, srcs, lands)}

    h_res = jnp.concatenate([jnp.zeros((pad_rows, D), F32), full["meta_tokens"], x[0]], axis=0)
    for st in started_g:
        h_res = h_res + st[4][0:1, 0:1]
    layers, saved = [], []
    for l in range(depth):
        layers.append(layer_weights(l, got[0] if l == 0 else arrived(l, h_res)))
        h_res, sv = _layer_fwd(h_res, layers[l], pad_rows)
        saved.append(sv)
    dres, loss_part = _loss_head(h_res, loss_target[0], row_start)
    loss = lax.psum(loss_part[0, 0], ("x", "y", "c"))

    def dest_block(name, g, d):
        if name == "w_in":
            s = w_in.shape[2]
            return _unpermute_cols(g["w_main"], g["w_small"], s * d, s * (d + 1)).astype(BF16)
        if BIG_AXES[BIG.index(name)] == 2:
            s = w[name].shape[2]
            return g[name][:, s * d:s * (d + 1)].astype(BF16)
        s = w[name].shape[1]
        return g[name][s * d:s * (d + 1), :].astype(BF16)

    gws = [None] * depth
    started = [None] * depth
    for l in reversed(range(depth)):
        dres, gws[l] = _layer_bwd(dres, layers[l], saved[l], pad_rows)
        started[l] = _exchange_start([jnp.stack([dest_block(n, gws[l], d) for d in range(N_DEV)]) for n in BIG],
                                     "scatter_start_%d" % l)
        if l > 0:
            token = started[l][4][0, 0].astype(BF16)
            layers[l - 1] = dict(layers[l - 1], w_down=layers[l - 1]["w_down"] + token)
    landed = [None] * depth
    for l in reversed(range(depth)):
        send_sems, recv_sems, srcs, lands, _ = started[l]
        srcs, lands = _exchange_wait(send_sems, recv_sems, srcs, lands, dres, "scatter_wait_%d" % l)
        landed[l] = [lax.dynamic_update_index_in_dim(ld, lax.dynamic_index_in_dim(sr, me, 0, keepdims=False), me, 0)
                     for sr, ld in zip(srcs, lands)]
    grad_x = dres[row_start:].reshape(x.shape)

    def stack(fn):
        return jnp.stack([fn(g) for g in gws])

    part = dict(
        meta_tokens=dres[pad_rows:row_start],
        norm1_g=stack(lambda g: g["norm1_g"][0]), norm2_g=stack(lambda g: g["norm2_g"][0]),
        fox_f_bias=stack(lambda g: g["fb"][0, 0:8]),
        fox_q_norm_g=stack(lambda g: g["qg"].reshape(8, FOX_DH).sum(0)),
        fox_k_norm_g=stack(lambda g: g["kg"].reshape(8, FOX_DH).sum(0)),
        gdn_conv_w=stack(lambda g: g["gdn_conv_w"]), gdn_a_log=stack(lambda g: g["alog"][0, 16:24]),
        gdn_dt_bias=stack(lambda g: g["dtb"][0, 16:24]),
        gdn_norm_g=stack(lambda g: g["gn"].reshape(8, GDN_DH).sum(0)),
        ffn_conv_w=stack(lambda g: g["ffn_conv_w"]))

    res = {}
    for idx, n in enumerate(BIG):
        parts = jnp.stack([landed[l][idx] for l in range(depth)], axis=1)
        res[n] = _sum_adamw(parts, w[n], mom[n], var[n], "adamw_" + n)

    small_names = SHARDED_SMALL + REPL
    small_axes = SHARDED_SMALL_AXES + (None,) * len(REPL)
    landed_s = _all_to_all(_pack_dest([part[n] for n in small_names], small_axes, LANE, F32, 8), "scatter_small")
    shapes_s = [w[n].shape for n in small_names]
    outs = _sum_adamw(landed_s[:, None], *[_pack([d[n] for n in small_names], LANE, F32, 8)[None] for d in (w, mom, var)],
                      "adamw_small")
    for o_idx, packed in enumerate(outs):
        for n, a in zip(small_names, _unpack(packed[0], shapes_s)):
            res.setdefault(n, [None] * 4)[o_idx] = a

    return (loss, grad_x, *[res[n][0] for n in ORDER], *[res[n][1] for n in ORDER],
            *[res[n][2] for n in ORDER], *[res[n][3] for n in ORDER])
```

```python
import functools

import jax
import jax.numpy as jnp
from jax import lax
from jax.experimental import pallas as pl
from jax.experimental.pallas import tpu as pltpu

F32, BF16 = jnp.float32, jnp.bfloat16
MESH = pl.DeviceIdType.MESH

D = 1024
N_META = 16
DEPTH = 4
EPS = 1e-6
LOG2E = 1.4426950408889634
LN2 = 0.6931471805599453
NEG = -1e30
FOX_W, FOX_DH = 512, 64
GDN_W, GDN_DH, GDN_H = 1024, 128, 8
CHUNK = 64
D_FF = 2816
N_DEV = 8
ADAM_LR, ADAM_B1, ADAM_B2, ADAM_EPS, ADAM_WD, ADAM_STEP = 0.001, 0.9, 0.999, 1e-08, 0.01, 10

VMEM_LIMIT_BYTES = 48 * 1024 * 1024
MATMUL_VMEM_BUDGET = 36 * 1024 * 1024
ROW_TILE = 128
ROW_TILES_WIDE = (640, 512, 256, 128)
ROW_TILES = (320, 256, 128)
LANE = 128

C_GQ, C_GK, C_GV, C_GZ, C_GATE, C_FQ, C_FK, C_FV = 0, 1024, 2048, 3072, 4096, 6144, 6656, 7168
W_MAIN = 7680
O_FQ, O_FK, O_FV, O_FL, O_GQ, O_GK, O_GV, O_BL, O_AL, O_GZ, O_GATE, O_END = (
    0, 512, 1024, 1536, 1544, 2568, 3592, 4616, 4624, 4632, 5656, 7704)


def _pick(n, cands):
    for c in cands:
        if n % c == 0:
            return c
    return n


def _call(body, *, name, out_shape, in_specs, out_specs, grid=(), scratch=(), sem=None):
    kw = dict(vmem_limit_bytes=VMEM_LIMIT_BYTES)
    if sem is not None:
        kw["dimension_semantics"] = sem
    return pl.pallas_call(body, name=name, out_shape=out_shape, grid=grid, in_specs=in_specs,
                          out_specs=out_specs, scratch_shapes=list(scratch),
                          compiler_params=pltpu.CompilerParams(**kw))


_DIMS = {"nn": (((1,), (0,)), ((), ())), "nt": (((1,), (1,)), ((), ())), "tn": (((0,), (0,)), ((), ()))}


_DIMS_BATCHED = {"nn": (((2,), (1,)), ((0,), (0,))), "nt": (((2,), (2,)), ((0,), (0,))),
                 "tn": (((1,), (1,)), ((0,), (0,)))}


def _dot(a, b, mode, prec=None):
    dims = _DIMS[mode] if a.ndim == 2 else _DIMS_BATCHED[mode]
    return lax.dot_general(a, b, dims, precision=prec, preferred_element_type=F32)


def _mm_grads(f, mode, a, b, g):
    if mode == "nn":
        return f(g, b, "nt"), f(a, g, "tn")
    if mode == "nt":
        return f(g, b, "nn"), f(g, a, "tn")
    return f(b, g, "nt"), f(a, g, "nn")


@functools.partial(jax.custom_vjp, nondiff_argnums=(2,))
def _mmb(a, b, mode):
    return _dot(a.astype(BF16), b.astype(BF16), mode)


def _mmb_fwd(a, b, mode):
    return _mmb(a, b, mode), (a, b)


def _mmb_bwd(mode, res, g):
    return _mm_grads(_mmb, mode, res[0], res[1], g)


_mmb.defvjp(_mmb_fwd, _mmb_bwd)


def _split(a):
    hi = a.astype(BF16)
    return hi, (a - hi.astype(F32)).astype(BF16)


@functools.partial(jax.custom_vjp, nondiff_argnums=(2,))
def _mmh(a, b, mode):
    ah, al = _split(a)
    bh, bl = _split(b)
    return _dot(ah, bh, mode) + (_dot(ah, bl, mode) + _dot(al, bh, mode))


def _mmh_fwd(a, b, mode):
    return _mmh(a, b, mode), (a, b)


def _mmh_bwd(mode, res, g):
    return _mm_grads(_mmh, mode, res[0], res[1], g)


_mmh.defvjp(_mmh_fwd, _mmh_bwd)


def _dot_sel(sel, x, mode):
    s = sel.astype(BF16)
    x1 = x.astype(BF16)
    x2, x3 = _split(x - x1.astype(F32))
    return _dot(s, x1, mode) + (_dot(s, x2, mode) + _dot(s, x3, mode))


@jax.custom_vjp
def _mms(sel, x):
    return _dot_sel(sel, x, "nn")


def _mms_fwd(sel, x):
    return _dot_sel(sel, x, "nn"), sel


def _mms_bwd(sel, g):
    return jnp.zeros_like(sel), _dot_sel(sel, g, "tn")


_mms.defvjp(_mms_fwd, _mms_bwd)


def _softplus(z):
    return jnp.maximum(z, 0.0) + jnp.log(1.0 + jnp.exp(-jnp.abs(z)))


def _log_sigmoid(z):
    return jnp.minimum(z, 0.0) - jnp.log(1.0 + jnp.exp(-jnp.abs(z)))


def _silu(z):
    return z * jax.nn.sigmoid(z)


def _iota(shape, dim):
    return lax.broadcasted_iota(jnp.int32, shape, dim)


def _inv_unit_lower_raw(n):
    c = n.shape[-1]
    ri, ci = _iota((c, c), 0), _iota((c, c), 1)
    eye = (ri == ci).astype(F32)
    dmask = (ri // 16) == (ci // 16)
    dpart = jnp.where(dmask, n, 0.0)
    lpart = n - dpart
    x = -dpart
    p = eye + x
    for _ in range(3):
        x = _mmh(x, x, "nn")
        p = p + _mmh(p, x, "nn")
    m = -_mmh(p, lpart, "nn")
    q = eye + m
    steps = 1
    while (1 << steps) < c // 16:
        steps += 1
    for _ in range(steps - 1):
        m = _mmh(m, m, "nn")
        q = q + _mmh(q, m, "nn")
    return _mmh(q, p, "nn")


@jax.custom_vjp
def _inv_given(n, t):
    return t


def _inv_given_fwd(n, t):
    return t, t


def _inv_given_bwd(t, g):
    c = t.shape[-1]
    strict = _iota((c, c), 0) > _iota((c, c), 1)
    d = -_mmh(_mmh(t, g, "tn"), t, "nt")
    return jnp.where(strict, d, 0.0), jnp.zeros_like(t)


_inv_given.defvjp(_inv_given_fwd, _inv_given_bwd)


def _shift_down(x, halo, s):
    if s == 0:
        return x
    xs = pltpu.roll(x, s, 0)
    hs = pltpu.roll(halo, s, 0)
    top = jnp.where(_iota(hs.shape, 0) < s, hs, xs[0:8])
    return jnp.concatenate([top, xs[8:]], axis=0)


def _shift_up(x, halo, s):
    if s == 0:
        return x
    tm = x.shape[0]
    xs = pltpu.roll(x, tm - s, 0)
    hs = pltpu.roll(halo, 8 - s, 0)
    bot = jnp.where(_iota(hs.shape, 0) >= 8 - s, hs, xs[tm - 8:])
    return jnp.concatenate([xs[:tm - 8], bot], axis=0)


def _causal_conv(x, halo, w):
    kk = w.shape[0]
    y = x * w[kk - 1:kk, :]
    for k in range(kk - 1):
        y = y + _shift_down(x, halo, kk - 1 - k) * w[k:k + 1, :]
    return y


def _head_scale(x, width, fn):
    outs = []
    for h in range(x.shape[1] // width):
        seg = x[:, h * width:(h + 1) * width]
        outs.append(seg * fn(jnp.sum(seg * seg, axis=1, keepdims=True)))
    return jnp.concatenate(outs, axis=1)


def _matmul(a, b, mode, out_dtype, add=None, name="mm"):
    if mode == "nn":
        (m, k), n = a.shape, b.shape[1]
    elif mode == "nt":
        (m, k), n = a.shape, b.shape[0]
    else:
        (k, m), n = a.shape, b.shape[1]
    tm = _pick(m, (1408, 1024, 512, 256, 128) if mode == "tn" else (640, 512, 256, 128))
    tn = _pick(n, (1536, 1408, 1024, 768, 512, 256, 128))
    sa, sb = a.dtype.itemsize, b.dtype.itemsize
    fixed = tm * tn * 4 * (3 + (2 if add is not None else 0))
    tk = 128
    for cand in (k, 2816, 2560, 1664, 1536, 1280, 1024, 832, 768, 640, 512, 256, 128):
        if mode != "tn" and cand != k and cand % LANE:
            continue
        if k % cand == 0 and fixed + 2 * cand * (tm * sa + tn * sb) <= MATMUL_VMEM_BUDGET:
            tk = cand
            break
    nk = k // tk
    a_spec = {"nn": pl.BlockSpec((tm, tk), lambda i, j, kk: (i, kk)),
              "nt": pl.BlockSpec((tm, tk), lambda i, j, kk: (i, kk)),
              "tn": pl.BlockSpec((tk, tm), lambda i, j, kk: (kk, i))}[mode]
    b_spec = {"nn": pl.BlockSpec((tk, tn), lambda i, j, kk: (kk, j)),
              "nt": pl.BlockSpec((tn, tk), lambda i, j, kk: (j, kk)),
              "tn": pl.BlockSpec((tk, tn), lambda i, j, kk: (kk, j))}[mode]
    o_spec = pl.BlockSpec((tm, tn), lambda i, j, kk: (i, j))
    has_add = add is not None

    def body(*refs):
        a_ref, b_ref = refs[0], refs[1]
        add_ref = refs[2] if has_add else None
        o_ref = refs[3] if has_add else refs[2]
        part = _dot(a_ref[...].astype(BF16), b_ref[...].astype(BF16), mode)
        if nk == 1:
            if has_add:
                part = part + add_ref[...].astype(F32)
            o_ref[...] = part.astype(out_dtype)
        else:
            acc = refs[-1]
            kk = pl.program_id(2)

            @pl.when(kk == 0)
            def _():
                acc[...] = part

            @pl.when(kk > 0)
            def _():
                acc[...] += part

            @pl.when(kk == nk - 1)
            def _():
                r = acc[...]
                if has_add:
                    r = r + add_ref[...].astype(F32)
                o_ref[...] = r.astype(out_dtype)

    ins = [a, b] + ([add] if has_add else [])
    specs = [a_spec, b_spec] + ([o_spec] if has_add else [])
    return _call(body, name=name, out_shape=jax.ShapeDtypeStruct((m, n), out_dtype), grid=(m // tm, n // tn, nk),
                 in_specs=specs, out_specs=o_spec,
                 scratch=[pltpu.VMEM((tm, tn), F32)] if nk > 1 else [],
                 sem=("parallel", "parallel", "arbitrary"))(*ins)


def _row_spec(width, colblock, tm):
    return pl.BlockSpec((tm, width), lambda i, cb=colblock: (i, cb))


def _full_spec(arr):
    nd = arr.ndim
    return pl.BlockSpec(arr.shape, lambda i, nd=nd: (0,) * nd)


def _rowwise(fn, rows, params, outs, name):
    lp = rows[0][0].shape[0]
    tm = _pick(lp, ROW_TILES_WIDE)
    nr, npar = len(rows), len(params)

    def body(*refs):
        row0 = pl.program_id(0) * tm
        vals = [r[...].astype(F32) for r in refs[:nr + npar]]
        res = fn(*vals, row0)
        for o_ref, r in zip(refs[nr + npar:], res):
            o_ref[...] = r.astype(o_ref.dtype)

    out = _call(body, name=name, grid=(lp // tm,),
                out_shape=[jax.ShapeDtypeStruct((lp, w), dt) for w, dt in outs],
                in_specs=[_row_spec(w, cb, tm) for _, w, cb in rows] + [_full_spec(p) for p in params],
                out_specs=[_row_spec(w, 0, tm) for w, _ in outs], sem=("parallel",))(
                    *[r[0] for r in rows], *params)
    return out


def _rowwise_bwd(fn, rows, params, cts, name, pad_rows, grad_dtypes, adds=None):
    lp = rows[0][0].shape[0]
    tm = _pick(lp, ROW_TILES)
    nr, npar, nct = len(rows), len(params), len(cts)
    adds = adds or [None] * nr
    add_list = [a for a in adds if a is not None]
    nadd = len(add_list)

    def body(*refs):
        i = pl.program_id(0)
        row0 = i * tm
        vals = [r[...].astype(F32) for r in refs[:nr + npar]]
        ct_vals = tuple(r[...].astype(F32) for r in refs[nr + npar:nr + npar + nct])
        add_refs = list(refs[nr + npar + nct:nr + npar + nct + nadd])
        outs = refs[nr + npar + nct + nadd:]
        _, vjp = jax.vjp(lambda *args: tuple(fn(*args, row0)), *vals)
        grads = vjp(ct_vals)
        valid = (row0 + _iota((tm, 1), 0)) >= pad_rows
        for idx in range(nr):
            g = jnp.where(valid, grads[idx], 0.0)
            if adds[idx] is not None:
                g = g + add_refs.pop(0)[...].astype(F32)
            outs[idx][...] = g.astype(outs[idx].dtype)
        for idx in range(npar):
            o_ref = outs[nr + idx]

            @pl.when(i == 0)
            def _(o_ref=o_ref):
                o_ref[...] = jnp.zeros_like(o_ref)

            o_ref[...] += grads[nr + idx]

    out = _call(body, name=name, grid=(lp // tm,),
                out_shape=[jax.ShapeDtypeStruct((lp, w), dt) for (_, w, _), dt in zip(rows, grad_dtypes)]
                + [jax.ShapeDtypeStruct(p.shape, F32) for p in params],
                in_specs=[_row_spec(w, cb, tm) for _, w, cb in rows] + [_full_spec(p) for p in params]
                + [_row_spec(w, cb, tm) for _, w, cb in cts] + [_row_spec(w, cb, tm) for _, w, cb in add_list],
                out_specs=[_row_spec(w, 0, tm) for _, w, _ in rows] + [_full_spec(p) for p in params],
                sem=("arbitrary",))(*[r[0] for r in rows], *params, *[c[0] for c in cts], *[a[0] for a in add_list])
    return out[:nr], out[nr:]


def _rmsnorm_fn(x, g, row0):
    return (x * lax.rsqrt(jnp.mean(x * x, axis=1, keepdims=True) + EPS) * g,)


def _fox_prep_fn(pad_rows, fq, fk, small, qg, kg, fb, row0):
    ri, ci = _iota((FOX_W, FOX_W), 0), _iota((FOX_W, FOX_W), 1)
    bd = jnp.where((ri // FOX_DH) == (ci // FOX_DH), 1.0 / FOX_DH, 0.0)

    def hn(x, g):
        return x * lax.rsqrt(_mmh(x * x, bd, "nn") + EPS) * g

    tm = small.shape[0]
    keep = (_iota((tm, LANE), 1) < 8) & ((row0 + _iota((tm, LANE), 0)) >= pad_rows)
    logf = jnp.where(keep, _log_sigmoid(small + fb) * LOG2E, 0.0)
    return hn(fq, qg) * (FOX_DH ** -0.5 * LOG2E), hn(fk, kg), logf


def _gdn_act_fn(cq, ck, cv, small, alog, dtb):
    tm = small.shape[0]
    q = _head_scale(_silu(cq), GDN_DH, lambda s: lax.rsqrt(s + EPS) * (GDN_DH ** -0.5))
    k = _head_scale(_silu(ck), GDN_DH, lambda s: lax.rsqrt(s + EPS))
    v = _silu(cv)
    lane = _iota((tm, LANE), 1)
    beta = jnp.where((lane >= 8) & (lane < 16), jax.nn.sigmoid(small), 0.0)
    g = jnp.where((lane >= 16) & (lane < 24), -jnp.exp(alog) * _softplus(small + dtb), 0.0)
    ri, ci = _iota((tm, tm), 0), _iota((tm, tm), 1)
    tri = jnp.where(((ri // CHUNK) == (ci // CHUNK)) & (ci <= ri), 1.0, 0.0)
    return q, k, v, beta + _mms(tri, g)


def _gdn_post_fn(o, gz, gn, row0):
    return (_head_scale(o, GDN_DH, lambda s: lax.rsqrt(s * (1.0 / GDN_DH) + EPS)) * gn * _silu(gz),)


def _merge_fn(g0, g1, ya, yb, row0):
    return (jax.nn.sigmoid(g0) * ya + jax.nn.sigmoid(g1) * yb,)


def _cumsum_rows(x, reverse, name):
    lp, w = x.shape
    tm = _pick(lp, (640, 512, 256, 128))
    nt = lp // tm

    def body(x_ref, o_ref, carry):
        i = pl.program_id(0)

        @pl.when(i == 0)
        def _():
            carry[...] = jnp.zeros_like(carry)

        ri, ci = _iota((tm, tm), 0), _iota((tm, tm), 1)
        tri = jnp.where((ci >= ri) if reverse else (ci <= ri), 1.0, 0.0)
        blk = x_ref[...]
        o_ref[...] = _dot_sel(tri, blk, "nn") + carry[0:1, :]
        carry[...] = carry[...] + jnp.sum(blk, axis=0, keepdims=True)

    idx = (lambda i: (nt - 1 - i, 0)) if reverse else (lambda i: (i, 0))
    return _call(body, name=name, grid=(nt,), out_shape=jax.ShapeDtypeStruct((lp, w), F32),
                 in_specs=[pl.BlockSpec((tm, w), idx)], out_specs=pl.BlockSpec((tm, w), idx),
                 scratch=[pltpu.VMEM((8, w), F32)], sem=("arbitrary",))(x)


def _fox_scores(q, k, fq, fk, hh, qpos0, kpos0, pad_rows, masked):
    tq, tk = q.shape[0], k.shape[0]
    lane = _iota(q.shape, 1)
    sel = (lane < FOX_DH) if hh == 0 else (lane >= FOX_DH)
    s = _dot(jnp.where(sel, q, jnp.zeros_like(q)), k, "nt") + fq - fk
    if not masked:
        return s, None, sel
    qpos = qpos0 + _iota((tq, tk), 0)
    kpos = kpos0 + _iota((tq, tk), 1)
    mask = (kpos <= qpos) & (kpos >= pad_rows)
    return jnp.where(mask, s, NEG), mask, sel


def _probs(s, mask, shift):
    p = jnp.exp2(s - shift)
    return p if mask is None else jnp.where(mask, p, 0.0)


def _both_variants(needs_mask, fn):
    @pl.when(needs_mask)
    def _():
        fn(True)

    @pl.when(jnp.logical_not(needs_mask))
    def _():
        fn(False)


def _lane_col(blk, lane_idx):
    return jnp.sum(jnp.where(_iota(blk.shape, 1) == lane_idx, blk, 0.0), axis=1, keepdims=True)


def _to_lanes(cols, width=LANE):
    lane = _iota((cols[0].shape[0], width), 1)
    out = jnp.zeros((cols[0].shape[0], width), F32)
    for idx, c in enumerate(cols):
        out = jnp.where(lane == idx, c, out)
    return out


def _fox_fwd(q, k, v, fsum, frow, pad_rows, v_col):
    lp = q.shape[0]
    t = _pick(lp, (640, 512, 256, 128))
    n = lp // t

    def body(q_ref, k_ref, v_ref, f_ref, fk_ref, o_ref, lse_ref, acc, m_s, l_s, fq_s):
        pr, i, j = pl.program_id(0), pl.program_id(1), pl.program_id(2)

        @pl.when(j == 0)
        def _():
            acc[...] = jnp.zeros_like(acc)
            m_s[...] = jnp.full_like(m_s, NEG)
            l_s[...] = jnp.zeros_like(l_s)
            for hh in range(2):
                fq_s[hh] = _lane_col(f_ref[...], 2 * pr + hh)

        def step(masked):
            for hh in range(2):
                s, mask, _ = _fox_scores(q_ref[...], k_ref[...], fq_s[hh], fk_ref[hh], hh, i * t, j * t, pad_rows,
                                         masked)
                m_prev = m_s[hh]
                m_new = jnp.maximum(m_prev, jnp.max(s, axis=1, keepdims=True))
                p = _probs(s, mask, m_new)
                alpha = jnp.exp2(m_prev - m_new)
                l_s[hh] = alpha * l_s[hh] + jnp.sum(p, axis=1, keepdims=True)
                acc[hh] = alpha * acc[hh] + _dot(p.astype(BF16), v_ref[...].astype(BF16), "nn")
                m_s[hh] = m_new

        @pl.when(j <= i)
        def _():
            _both_variants((j == i) | (j == 0), step)

        @pl.when(j == i)
        def _():
            outs, lses = [], []
            for hh in range(2):
                l = l_s[hh]
                ok = l > 0.0
                outs.append(acc[hh] * jnp.where(ok, 1.0 / jnp.where(ok, l, 1.0), 0.0))
                lses.append(jnp.where(ok, m_s[hh] + jnp.log2(jnp.where(ok, l, 1.0)), 0.0))
            lane = _iota((t, LANE), 1)
            o_ref[...] = jnp.where(lane < FOX_DH, outs[0], outs[1]).astype(o_ref.dtype)
            lse_ref[...] = _to_lanes(lses)

    qspec = pl.BlockSpec((t, LANE), lambda p, i, j: (i, p))
    kspec = pl.BlockSpec((t, LANE), lambda p, i, j: (jnp.minimum(j, i), p))
    vspec = pl.BlockSpec((t, LANE), lambda p, i, j: (jnp.minimum(j, i), v_col + p))
    fspec = pl.BlockSpec((t, LANE), lambda p, i, j: (i, 0))
    rspec = pl.BlockSpec((2, 1, t), lambda p, i, j: (p, 0, jnp.minimum(j, i)))
    return _call(body, name="fox_fwd", grid=(FOX_W // LANE, n, n),
                 out_shape=[jax.ShapeDtypeStruct((lp, FOX_W), BF16), jax.ShapeDtypeStruct((lp, FOX_W), F32)],
                 in_specs=[qspec, kspec, vspec, fspec, rspec], out_specs=[qspec, qspec],
                 scratch=[pltpu.VMEM((2, t, LANE), F32), pltpu.VMEM((2, t, 1), F32), pltpu.VMEM((2, t, 1), F32),
                          pltpu.VMEM((2, t, 1), F32)],
                 sem=("parallel", "parallel", "arbitrary"))(q, k, v, fsum, frow)


def _fox_delta_fn(o, do, row0):
    ri, ci = _iota((FOX_W, LANE), 0), _iota((FOX_W, LANE), 1)
    sel = jnp.where((ri // FOX_DH) == ci, 1.0, 0.0).astype(BF16)
    x = o * do
    x1 = x.astype(BF16)
    x2, x3 = _split(x - x1.astype(F32))
    return (_dot(x1, sel, "nn") + (_dot(x2, sel, "nn") + _dot(x3, sel, "nn")),)


def _fox_bwd(q, k, v, fsum, frow, do, lse, delta, pad_rows, v_col):
    lp = q.shape[0]
    t = _pick(lp, (640, 512, 256, 128))
    n = lp // t

    def body(q_ref, k_ref, v_ref, f_ref, fk_ref, do_ref, lse_ref, dl_ref,
             dq_ref, dk_ref, dv_ref, dfq_ref, dfk_ref, dka, dva, dfa):
        pr, j, i = pl.program_id(0), pl.program_id(1), pl.program_id(2)
        lane = _iota((t, LANE), 1)

        @pl.when((j == 0) & (i == 0))
        def _():
            dq_ref[...] = jnp.zeros_like(dq_ref)
            dfq_ref[...] = jnp.zeros_like(dfq_ref)

        @pl.when(i == 0)
        def _():
            dka[...] = jnp.zeros_like(dka)
            dva[...] = jnp.zeros_like(dva)
            dfa[...] = jnp.zeros_like(dfa)

        def step(masked):
            rows = pl.ds(pl.multiple_of(i * t, t), t)
            dq_add = jnp.zeros((t, LANE), F32)
            rowsums = []
            for hh in range(2):
                fq = _lane_col(f_ref[...], 2 * pr + hh)
                s, mask, sel = _fox_scores(q_ref[...], k_ref[...], fq, fk_ref[hh], hh, i * t, j * t, pad_rows, masked)
                p = _probs(s, mask, _lane_col(lse_ref[...], hh))
                dop = jnp.where(sel, do_ref[...] * LN2, jnp.zeros_like(do_ref[...]))
                ds = p * (_dot(dop, v_ref[...].astype(BF16), "nt") - _lane_col(dl_ref[...], 2 * pr + hh) * LN2)
                dsb = ds.astype(BF16)
                dva[hh] += _dot(p.astype(BF16), do_ref[...], "tn")
                dka[hh] += _dot(dsb, q_ref[...], "tn")
                dfa[hh] -= jnp.sum(ds, axis=0, keepdims=True)
                dq_add = dq_add + _dot(dsb, jnp.where(sel, k_ref[...], jnp.zeros_like(k_ref[...])), "nn")
                rowsums.append(jnp.sum(ds, axis=1, keepdims=True))
            dq_ref[rows, :] += dq_add
            dfq_ref[rows, :] += _to_lanes(rowsums)

        @pl.when(i >= j)
        def _():
            _both_variants((j == i) | (j == 0), step)

        @pl.when(i == n - 1)
        def _():
            dk_ref[...] = jnp.where(lane < FOX_DH, dka[0], dka[1])
            dv_ref[...] = jnp.where(lane < FOX_DH, dva[0], dva[1]).astype(dv_ref.dtype)
            dfk_ref[...] = dfa[...]

    qspec = pl.BlockSpec((t, LANE), lambda p, j, i: (jnp.maximum(i, j), p))
    f_q = pl.BlockSpec((t, LANE), lambda p, j, i: (jnp.maximum(i, j), 0))
    kspec = pl.BlockSpec((t, LANE), lambda p, j, i: (j, p))
    vspec = pl.BlockSpec((t, LANE), lambda p, j, i: (j, v_col + p))
    rspec = pl.BlockSpec((2, 1, t), lambda p, j, i: (p, 0, j))
    whole = pl.BlockSpec((lp, LANE), lambda p, j, i: (0, p))
    wide = jax.ShapeDtypeStruct((lp, FOX_W), F32)
    return _call(body, name="fox_bwd", grid=(FOX_W // LANE, n, n),
                 out_shape=[wide, wide, jax.ShapeDtypeStruct((lp, FOX_W), BF16), wide,
                            jax.ShapeDtypeStruct((8, 1, lp), F32)],
                 in_specs=[qspec, kspec, vspec, f_q, rspec, qspec, qspec, f_q],
                 out_specs=[whole, kspec, kspec, whole, rspec],
                 scratch=[pltpu.VMEM((2, t, LANE), F32), pltpu.VMEM((2, t, LANE), F32), pltpu.VMEM((2, 1, t), F32)],
                 sem=("parallel", "arbitrary", "arbitrary"))(q, k, v, fsum, frow, do, lse, delta)


def _gdn_chunk(q, k, v, beta, gcol, grow, s, inv):
    c = q.shape[-2]
    ri, ci = _iota((c, c), 0), _iota((c, c), 1)
    dec = jnp.exp(jnp.where(ri >= ci, gcol - grow, NEG))
    dec_strict = jnp.where(ri > ci, dec, 0.0)
    eg = jnp.exp(gcol)
    kb = k * beta
    t = inv(_mmb(kb, k, "nt") * dec_strict)
    u_hat = _mmh(t, v * beta, "nn")
    w = _mmh(t, kb * eg, "nn")
    u = u_hat - _mmb(w, s, "nn")
    o = _mmb(q * eg, s, "nn") + _mmb(_mmb(q, k, "nt") * dec, u, "nn")
    glast = jnp.sum(jnp.where(_iota((1, c), 1) == c - 1, grow, 0.0), axis=-1, keepdims=True)
    s_new = s * jnp.exp(glast) + _mmb(k * jnp.exp(glast - gcol), u, "tn")
    return o, s_new


def _gdn_specs(lp, reverse):
    n = lp // CHUNK
    pos = (lambda c: n - 1 - c) if reverse else (lambda c: c)
    wide = pl.BlockSpec((CHUNK, GDN_W), lambda c: (pos(c), 0))
    lanes = pl.BlockSpec((CHUNK, LANE), lambda c: (pos(c), 0))
    row = pl.BlockSpec((GDN_H, 1, 1, CHUNK), lambda c: (0, pos(c), 0, 0))
    st = pl.BlockSpec((GDN_H, 1, GDN_DH, GDN_DH), lambda c: (0, pos(c), 0, 0))
    return n, wide, lanes, row, st


def _heads(ref):
    return jnp.stack([ref[:, h * GDN_DH:(h + 1) * GDN_DH] for h in range(GDN_H)])


def _put_heads(ref, val):
    for h in range(GDN_H):
        ref[:, h * GDN_DH:(h + 1) * GDN_DH] = val[h]


def _head_cols(blk, lane0):
    return jnp.stack([_lane_col(blk, lane0 + h) for h in range(GDN_H)])


def _gdn_fwd(q, k, v, bg, grow):
    lp = q.shape[0]
    n, wide, lanes, row, st = _gdn_specs(lp, False)

    def body(q_ref, k_ref, v_ref, bg_ref, gr_ref, o_ref, sp_ref, t_ref, s_scr):
        @pl.when(pl.program_id(0) == 0)
        def _():
            s_scr[...] = jnp.zeros_like(s_scr)

        def inv(m):
            t = _inv_unit_lower_raw(m)
            t_ref[:, 0] = t
            return t

        s = s_scr[...]
        sp_ref[:, 0] = s
        bg_blk = bg_ref[...]
        o, s_new = _gdn_chunk(_heads(q_ref), _heads(k_ref), _heads(v_ref), _head_cols(bg_blk, 8),
                              _head_cols(bg_blk, 16), gr_ref[:, 0], s, inv)
        _put_heads(o_ref, o)
        s_scr[...] = s_new

    tri = pl.BlockSpec((GDN_H, 1, CHUNK, CHUNK), lambda c: (0, c, 0, 0))
    return _call(body, name="gdn_fwd", grid=(n,),
                 out_shape=[jax.ShapeDtypeStruct((lp, GDN_W), F32),
                            jax.ShapeDtypeStruct((GDN_H, n, GDN_DH, GDN_DH), F32),
                            jax.ShapeDtypeStruct((GDN_H, n, CHUNK, CHUNK), F32)],
                 in_specs=[wide, wide, wide, lanes, row], out_specs=[wide, st, tri],
                 scratch=[pltpu.VMEM((GDN_H, GDN_DH, GDN_DH), F32)], sem=("arbitrary",))(q, k, v, bg, grow)


def _gdn_bwd(q, k, v, bg, grow, sprev, tinv, do):
    lp = q.shape[0]
    n, wide, lanes, row, st = _gdn_specs(lp, True)

    def body(q_ref, k_ref, v_ref, bg_ref, gr_ref, sp_ref, t_ref, do_ref,
             dq_ref, dk_ref, dv_ref, dbg_ref, dgr_ref, ds_scr):
        @pl.when(pl.program_id(0) == 0)
        def _():
            ds_scr[...] = jnp.zeros_like(ds_scr)

        t_saved = t_ref[:, 0]
        fn = functools.partial(_gdn_chunk, inv=lambda m: _inv_given(m, t_saved))
        bg_blk = bg_ref[...]
        _, vjp = jax.vjp(fn, _heads(q_ref), _heads(k_ref), _heads(v_ref), _head_cols(bg_blk, 8),
                         _head_cols(bg_blk, 16), gr_ref[:, 0], sp_ref[:, 0])
        dq, dk, dv, db, dgc, dgr, ds = vjp((_heads(do_ref), ds_scr[...]))
        _put_heads(dq_ref, dq)
        _put_heads(dk_ref, dk)
        _put_heads(dv_ref, dv)
        lane = _iota((CHUNK, LANE), 1)
        dbg = jnp.zeros((CHUNK, LANE), F32)
        for h in range(GDN_H):
            dbg = jnp.where(lane == 8 + h, db[h], jnp.where(lane == 16 + h, dgc[h], dbg))
        dbg_ref[...] = dbg
        dgr_ref[:, 0] = dgr
        ds_scr[...] = ds

    wshape = jax.ShapeDtypeStruct((lp, GDN_W), F32)
    tri = pl.BlockSpec((GDN_H, 1, CHUNK, CHUNK), lambda c: (0, n - 1 - c, 0, 0))
    return _call(body, name="gdn_bwd", grid=(n,),
                 out_shape=[wshape, wshape, wshape, jax.ShapeDtypeStruct((lp, LANE), F32),
                            jax.ShapeDtypeStruct((GDN_H, n, 1, CHUNK), F32)],
                 in_specs=[wide, wide, wide, lanes, row, st, tri, wide], out_specs=[wide, wide, wide, lanes, row],
                 scratch=[pltpu.VMEM((GDN_H, GDN_DH, GDN_DH), F32)], sem=("arbitrary",))(
                     q, k, v, bg, grow, sprev, tinv, do)


def _halo_prev(width, colblock, tm):
    return pl.BlockSpec((8, width), lambda i, cb=colblock: (jnp.maximum(i * (tm // 8) - 1, 0), cb))


def _gdn_act(proj, small, conv_w, alog_row, dtb_row):
    lp = proj.shape[0]
    tm = _pick(lp, ROW_TILES)

    def body(xq, xk, xv, hq, hk, hv, wq, wk, wv, sm, al, dt, q_ref, k_ref, v_ref, bg_ref):
        first = (pl.program_id(0) > 0).astype(F32)
        cs = [_causal_conv(x[...], h[...] * first, w[...]) for x, h, w in ((xq, hq, wq), (xk, hk, wk), (xv, hv, wv))]
        q, k, v, bg = _gdn_act_fn(cs[0], cs[1], cs[2], sm[...], al[...], dt[...])
        q_ref[...], k_ref[...], v_ref[...], bg_ref[...] = q, k, v, bg

    wide = jax.ShapeDtypeStruct((lp, GDN_W), F32)
    wspec = [pl.BlockSpec((4, GDN_W), lambda i, c=c: (0, c)) for c in range(3)]
    return _call(body, name="gdn_act", grid=(lp // tm,),
                 out_shape=[wide, wide, wide, jax.ShapeDtypeStruct((lp, LANE), F32)],
                 in_specs=[_row_spec(GDN_W, c, tm) for c in range(3)] + [_halo_prev(GDN_W, c, tm) for c in range(3)]
                 + wspec + [_row_spec(LANE, 0, tm), _full_spec(alog_row), _full_spec(dtb_row)],
                 out_specs=[_row_spec(GDN_W, 0, tm)] * 3 + [_row_spec(LANE, 0, tm)], sem=("parallel",))(
                     proj, proj, proj, proj, proj, proj, conv_w, conv_w, conv_w, small, alog_row, dtb_row)


def _gdn_act_bwd(proj, small, conv_w, alog_row, dtb_row, dq, dk, dv, dbg):
    lp = proj.shape[0]
    tm = ROW_TILE

    def body(xq, xk, xv, hq, hk, hv, wq, wk, wv, sm, al, dt, dq_r, dk_r, dv_r, dbg_r,
             dc_ref, dsm_ref, dal_ref, ddt_ref, dw_ref):
        i = pl.program_id(0)
        first = (i > 0).astype(F32)
        xs = [(x[...], h[...] * first, w[...]) for x, h, w in ((xq, hq, wq), (xk, hk, wk), (xv, hv, wv))]
        cs = [_causal_conv(*t) for t in xs]
        _, vjp = jax.vjp(_gdn_act_fn, cs[0], cs[1], cs[2], sm[...], al[...], dt[...])
        dcq, dck, dcv, dsm, dal, ddt = vjp((dq_r[...], dk_r[...], dv_r[...], dbg_r[...]))
        dsm_ref[...] = dsm

        @pl.when(i == 0)
        def _():
            dal_ref[...] = jnp.zeros_like(dal_ref)
            ddt_ref[...] = jnp.zeros_like(ddt_ref)
            dw_ref[...] = jnp.zeros_like(dw_ref)

        dal_ref[...] += dal
        ddt_ref[...] += ddt
        for c, (dc, (x, h, w)) in enumerate(zip((dcq, dck, dcv), xs)):
            dc_ref[:, c * GDN_W:(c + 1) * GDN_W] = dc
            rows = [jnp.sum(_shift_down(x, h, 3 - kk) * dc, axis=0, keepdims=True) for kk in range(4)]
            dw_ref[:, c * GDN_W:(c + 1) * GDN_W] += jnp.concatenate(rows, axis=0)

    wspec = [pl.BlockSpec((4, GDN_W), lambda i, c=c: (0, c)) for c in range(3)]
    row128 = jax.ShapeDtypeStruct((1, LANE), F32)
    return _call(body, name="gdn_act_bwd", grid=(lp // tm,),
                 out_shape=[jax.ShapeDtypeStruct((lp, 3 * GDN_W), F32), jax.ShapeDtypeStruct((lp, LANE), F32),
                            row128, row128, jax.ShapeDtypeStruct((4, 3 * GDN_W), F32)],
                 in_specs=[_row_spec(GDN_W, c, tm) for c in range(3)] + [_halo_prev(GDN_W, c, tm) for c in range(3)]
                 + wspec + [_row_spec(LANE, 0, tm), _full_spec(alog_row), _full_spec(dtb_row)]
                 + [_row_spec(GDN_W, 0, tm)] * 3 + [_row_spec(LANE, 0, tm)],
                 out_specs=[_row_spec(3 * GDN_W, 0, tm), _row_spec(LANE, 0, tm),
                            _full_spec(alog_row), _full_spec(dtb_row), pl.BlockSpec((4, 3 * GDN_W), lambda i: (0, 0))],
                 sem=("arbitrary",))(proj, proj, proj, proj, proj, proj, conv_w, conv_w, conv_w, small,
                                     alog_row, dtb_row, dq, dk, dv, dbg)


def _ffn_act(up_pre, conv_w):
    lp = up_pre.shape[0]
    tm = _pick(lp, ROW_TILES)

    def body(xg, xv, hg, hv, wg, wv, a_ref):
        first = (pl.program_id(0) > 0).astype(F32)
        ug = _causal_conv(xg[...], hg[...] * first, wg[...])
        uv = _causal_conv(xv[...], hv[...] * first, wv[...])
        a_ref[...] = (_silu(ug) * uv).astype(a_ref.dtype)

    wspec = [pl.BlockSpec((3, D_FF), lambda i, c=c: (0, c)) for c in range(2)]
    return _call(body, name="ffn_act", grid=(lp // tm,), out_shape=jax.ShapeDtypeStruct((lp, D_FF), BF16),
                 in_specs=[_row_spec(D_FF, c, tm) for c in range(2)] + [_halo_prev(D_FF, c, tm) for c in range(2)] + wspec,
                 out_specs=_row_spec(D_FF, 0, tm), sem=("parallel",))(up_pre, up_pre, up_pre, up_pre, conv_w, conv_w)


def _ffn_act_bwd(up_pre, conv_w, dact, tm=ROW_TILE):
    lp = up_pre.shape[0]

    def body(xg, xv, hg, hv, wg, wv, da, du_ref, dw_ref):
        i = pl.program_id(0)
        first = (i > 0).astype(F32)
        xs = [(x[...], h[...] * first, w[...]) for x, h, w in ((xg, hg, wg), (xv, hv, wv))]
        ug, uv = [_causal_conv(*t) for t in xs]
        _, vjp = jax.vjp(lambda a, b: _silu(a) * b, ug, uv)
        dus = vjp(da[...].astype(F32))

        @pl.when(i == 0)
        def _():
            dw_ref[...] = jnp.zeros_like(dw_ref)

        for c, (du, (x, h, w)) in enumerate(zip(dus, xs)):
            du_ref[:, c * D_FF:(c + 1) * D_FF] = du
            rows = [jnp.sum(_shift_down(x, h, 2 - kk) * du, axis=0, keepdims=True) for kk in range(3)]
            dw_ref[:, c * D_FF:(c + 1) * D_FF] += jnp.concatenate(rows, axis=0)

    wspec = [pl.BlockSpec((3, D_FF), lambda i, c=c: (0, c)) for c in range(2)]
    return _call(body, name="ffn_act_bwd", grid=(lp // tm,),
                 out_shape=[jax.ShapeDtypeStruct((lp, 2 * D_FF), F32), jax.ShapeDtypeStruct((3, 2 * D_FF), F32)],
                 in_specs=[_row_spec(D_FF, c, tm) for c in range(2)] + [_halo_prev(D_FF, c, tm) for c in range(2)]
                 + wspec + [_row_spec(D_FF, 0, tm)],
                 out_specs=[_row_spec(2 * D_FF, 0, tm), pl.BlockSpec((3, 2 * D_FF), lambda i: (0, 0))],
                 sem=("arbitrary",))(up_pre, up_pre, up_pre, up_pre, conv_w, conv_w, dact)


def _conv_bwd_x(dy, w, pad_rows, width, name):
    lp, ctot = dy.shape
    tm = _pick(lp, ROW_TILES)
    nt = lp // tm
    kk = w.shape[0]

    def body(d_ref, h_ref, w_ref, o_ref):
        i = pl.program_id(0)
        last = (i < nt - 1).astype(F32)
        d, h, wv = d_ref[...], h_ref[...] * last, w_ref[...]
        y = d * wv[kk - 1:kk, :]
        for k in range(kk - 1):
            y = y + _shift_up(d, h, kk - 1 - k) * wv[k:k + 1, :]
        valid = (i * tm + _iota((tm, 1), 0)) >= pad_rows
        o_ref[...] = jnp.where(valid, y, 0.0).astype(o_ref.dtype)

    return _call(body, name=name, grid=(nt, ctot // width), out_shape=jax.ShapeDtypeStruct((lp, ctot), BF16),
                 in_specs=[pl.BlockSpec((tm, width), lambda i, c: (i, c)),
                           pl.BlockSpec((8, width), lambda i, c: (jnp.minimum((i + 1) * (tm // 8), lp // 8 - 1), c)),
                           pl.BlockSpec((kk, width), lambda i, c: (0, c))],
                 out_specs=pl.BlockSpec((tm, width), lambda i, c: (i, c)), sem=("parallel", "parallel"))(dy, dy, w)


def _loss_head(h_res, target, row_start, tm=ROW_TILE):
    lp, d = h_res.shape
    t0 = row_start // tm

    def body(h_ref, t_ref, dy_ref, loss_ref):
        i = pl.program_id(0)

        @pl.when(i == 0)
        def _():
            loss_ref[...] = jnp.zeros_like(loss_ref)

        live = (i >= t0).astype(F32)
        err = (h_ref[...] - t_ref[...]) * live
        dy_ref[...] = err * (1.0 / d)
        loss_ref[...] += 0.5 / d * jnp.sum(err * err)

    return _call(body, name="loss_head", grid=(lp // tm,),
                 out_shape=[jax.ShapeDtypeStruct((lp, d), F32), jax.ShapeDtypeStruct((8, LANE), F32)],
                 in_specs=[pl.BlockSpec((tm, d), lambda i: (i, 0)),
                           pl.BlockSpec((tm, d), lambda i: (jnp.maximum(i - t0, 0), 0))],
                 out_specs=[pl.BlockSpec((tm, d), lambda i: (i, 0)), pl.BlockSpec((8, LANE), lambda i: (0, 0))],
                 sem=("arbitrary",))(h_res, target)


def _sum_adamw(parts, w, m, v, name):
    a, r, c = w.shape
    tm = _pick(r, (256, 128, 64, 32, 16))
    bc1 = 1.0 - ADAM_B1 ** ADAM_STEP
    bc2 = 1.0 - ADAM_B2 ** ADAM_STEP

    def body(p_ref, w_ref, m_ref, v_ref, g_ref, d_ref, nm_ref, nv_ref):
        g = p_ref[0, 0].astype(F32)
        for s in range(1, N_DEV):
            g = g + p_ref[s, 0].astype(F32)
        nm = ADAM_B1 * m_ref[0] + (1.0 - ADAM_B1) * g
        nv = ADAM_B2 * v_ref[0] + (1.0 - ADAM_B2) * (g * g)
        g_ref[0] = g
        nm_ref[0] = nm
        nv_ref[0] = nv
        d_ref[0] = -ADAM_LR * ((nm / bc1) / (jnp.sqrt(nv / bc2) + ADAM_EPS) + ADAM_WD * w_ref[0])

    spec = pl.BlockSpec((1, tm, c), lambda l, i: (l, i, 0))
    shp = jax.ShapeDtypeStruct((a, r, c), F32)
    return _call(body, name=name, grid=(a, r // tm), out_shape=[shp] * 4,
                 in_specs=[pl.BlockSpec((N_DEV, 1, tm, c), lambda l, i: (0, l, i, 0)), spec, spec, spec],
                 out_specs=[spec] * 4, sem=("parallel", "parallel"))(parts, w, m, v)


_ANY = pl.BlockSpec(memory_space=pl.ANY)


def _all_gather(block, name):
    def body(x_ref, out_ref, send_sems, recv_sems, local_sem):
        x, y, c = lax.axis_index("x"), lax.axis_index("y"), lax.axis_index("c")
        me, sibling = (x, y, c), (x, y, 1 - c)
        chips = [(1 - x, y), (x, 1 - y), (1 - x, 1 - y)]

        def slot(px, py, pc):
            return out_ref.at[4 * px + 2 * py + pc]

        def copy(k, blk, to, src=None):
            return pltpu.make_async_remote_copy(
                src_ref=slot(*blk) if src is None else src, dst_ref=slot(*blk),
                send_sem=send_sems.at[k], recv_sem=recv_sems.at[k], device_id=to, device_id_type=MESH)

        mine = pltpu.make_async_copy(x_ref, slot(*me), local_sem)
        mine.start()
        first = [copy(0, me, sibling, src=x_ref)]
        first += [copy(1 + j, me, (*chip, c), src=x_ref) for j, chip in enumerate(chips)]
        for cp in first:
            cp.start()
        passed = [copy(4 + j, (*chip, c), sibling) for j, chip in enumerate(chips)]
        for j, chip in enumerate(chips):
            copy(1 + j, (*chip, c), me).wait_recv()
            passed[j].start()
        copy(0, sibling, me).wait_recv()
        for j, chip in enumerate(chips):
            copy(4 + j, (*chip, 1 - c), me).wait_recv()
        for cp in first + passed:
            cp.wait_send()
        mine.wait()

    return pl.pallas_call(
        body, name=name, out_shape=jax.ShapeDtypeStruct((N_DEV,) + block.shape, block.dtype),
        in_specs=[_ANY], out_specs=_ANY,
        scratch_shapes=[pltpu.SemaphoreType.DMA((7,)), pltpu.SemaphoreType.DMA((7,)), pltpu.SemaphoreType.DMA],
    )(block)


def _all_to_all(src, name):
    def body(s_ref, o_ref, send_sems, recv_sems, local_sem):
        x, y, c = lax.axis_index("x"), lax.axis_index("y"), lax.axis_index("c")
        me = 4 * x + 2 * y + c
        mine = pltpu.make_async_copy(s_ref.at[me], o_ref.at[me], local_sem)
        mine.start()
        copies = []
        for k in range(1, N_DEV):
            px = 1 - x if k & 4 else x
            py = 1 - y if k & 2 else y
            pc = 1 - c if k & 1 else c
            peer = 4 * px + 2 * py + pc
            copies.append((pltpu.make_async_remote_copy(
                src_ref=s_ref.at[peer], dst_ref=o_ref.at[me], send_sem=send_sems.at[k - 1],
                recv_sem=recv_sems.at[k - 1], device_id=(px, py, pc), device_id_type=MESH), peer, k))
        for cp, _, _ in copies:
            cp.start()
        for cp, peer, k in copies:
            cp.wait_send()
            pltpu.make_async_remote_copy(
                src_ref=s_ref.at[peer], dst_ref=o_ref.at[peer], send_sem=send_sems.at[k - 1],
                recv_sem=recv_sems.at[k - 1], device_id=(x, y, c), device_id_type=MESH).wait_recv()
        mine.wait()

    return pl.pallas_call(
        body, name=name, out_shape=jax.ShapeDtypeStruct(src.shape, src.dtype), in_specs=[_ANY], out_specs=_ANY,
        scratch_shapes=[pltpu.SemaphoreType.DMA((7,)), pltpu.SemaphoreType.DMA((7,)), pltpu.SemaphoreType.DMA],
    )(src)


_HBM = pl.BlockSpec(memory_space=pltpu.HBM)
_SEM = pl.BlockSpec(memory_space=pltpu.SEMAPHORE)
_EFFECT = pltpu.SideEffectType.DATAFLOW_SIDE_EFFECTING


def _peer_list(x, y, c):
    return [(1 - x if k & 4 else x, 1 - y if k & 2 else y, 1 - c if k & 1 else c) for k in range(1, N_DEV)]


def _exchange_copies(s_refs, l_refs, send_sems, recv_sems, landing_of_peer, same_block):
    x, y, c = lax.axis_index("x"), lax.axis_index("y"), lax.axis_index("c")
    me = 4 * x + 2 * y + c
    out = []
    for wi, (s_ref, l_ref) in enumerate(zip(s_refs, l_refs)):
        for k, (px, py, pc) in enumerate(_peer_list(x, y, c)):
            peer = 4 * px + 2 * py + pc
            idx = wi * (N_DEV - 1) + k
            out.append(pltpu.make_async_remote_copy(
                src_ref=s_ref if same_block else s_ref.at[peer], dst_ref=l_ref.at[peer if landing_of_peer else me],
                send_sem=send_sems.at[idx], recv_sem=recv_sems.at[idx], device_id=(px, py, pc), device_id_type=MESH))
    return out


def _exchange_start(srcs, name, same_block=False):
    nw = len(srcs)
    ncp = nw * (N_DEV - 1)

    def body(*refs):
        for cp in _exchange_copies(refs[:nw], refs[nw:2 * nw], refs[2 * nw], refs[2 * nw + 1], False, same_block):
            cp.start()
        refs[-1][...] = jnp.zeros_like(refs[-1])

    land_shapes = [((N_DEV,) + s.shape) if same_block else s.shape for s in srcs]
    hbm = [pltpu.HBM(s.shape, s.dtype) for s in srcs]
    hbm_l = [pltpu.HBM(ls, s.dtype) for ls, s in zip(land_shapes, srcs)]
    outs = pl.pallas_call(
        body, name=name,
        out_shape=(pltpu.SemaphoreType.DMA((ncp,)), pltpu.SemaphoreType.DMA((ncp,)), *hbm, *hbm_l,
                   jax.ShapeDtypeStruct((8, LANE), F32)),
        in_specs=[_HBM] * (2 * nw), out_specs=(_SEM, _SEM, *[_HBM] * (2 * nw), pl.BlockSpec(memory_space=pltpu.VMEM)),
        input_output_aliases={i: 2 + i for i in range(2 * nw)},
        compiler_params=pltpu.CompilerParams(has_side_effects=_EFFECT),
    )(*[pltpu.with_memory_space_constraint(s, pltpu.HBM) for s in srcs],
      *[pltpu.with_memory_space_constraint(lax.empty(ls, s.dtype), pltpu.HBM) for ls, s in zip(land_shapes, srcs)])
    return outs[0], outs[1], list(outs[2:2 + nw]), list(outs[2 + nw:2 + 2 * nw]), outs[-1]


def _exchange_wait(send_sems, recv_sems, srcs, lands, after, name, same_block=False):
    nw = len(srcs)

    def body(*refs):
        for cp in _exchange_copies(refs[:nw], refs[nw:2 * nw], refs[2 * nw], refs[2 * nw + 1], True, same_block):
            cp.wait_send()
            cp.wait_recv()

    hbm = [pltpu.HBM(a.shape, a.dtype) for a in list(srcs) + list(lands)]
    outs = pl.pallas_call(
        body, name=name, out_shape=tuple(hbm),
        in_specs=[_HBM] * (2 * nw) + [_SEM, _SEM, pl.BlockSpec(memory_space=pl.ANY)], out_specs=tuple([_HBM] * (2 * nw)),
        input_output_aliases={i: i for i in range(2 * nw)},
        compiler_params=pltpu.CompilerParams(has_side_effects=_EFFECT),
    )(*srcs, *lands, send_sems, recv_sems, after)
    return list(outs[:nw]), list(outs[nw:])


def _pack(blocks, width, dtype, row_mult):
    flat = jnp.concatenate([b.astype(dtype).reshape(-1) for b in blocks])
    per = width * row_mult
    total = -(-flat.shape[0] // per) * per
    return jnp.pad(flat, (0, total - flat.shape[0])).reshape(total // width, width)


def _pack_dest(fulls, axes, width, dtype, row_mult):
    rows = []
    for f, ax in zip(fulls, axes):
        f = f.astype(dtype)
        if ax is None:
            rows.append(jnp.broadcast_to(f.reshape(1, -1), (N_DEV, f.size)))
        else:
            shp = f.shape
            f = f.reshape(shp[:ax] + (N_DEV, shp[ax] // N_DEV) + shp[ax + 1:])
            rows.append(jnp.moveaxis(f, ax, 0).reshape(N_DEV, -1))
    flat = jnp.concatenate(rows, axis=1)
    per = width * row_mult
    total = -(-flat.shape[1] // per) * per
    return jnp.pad(flat, ((0, 0), (0, total - flat.shape[1]))).reshape(N_DEV, total // width, width)


def _unpack(packed, shapes):
    flat = packed.reshape(-1)
    out, off = [], 0
    for s in shapes:
        n = 1
        for d in s:
            n *= d
        out.append(flat[off:off + n].reshape(s))
        off += n
    return out


def _unpack_gathered(gathered, shapes, axes):
    flat = gathered.reshape(N_DEV, -1)
    out, off = [], 0
    for s, ax in zip(shapes, axes):
        n = 1
        for d in s:
            n *= d
        blk = jnp.moveaxis(flat[:, off:off + n].reshape((N_DEV,) + tuple(s)), 0, ax)
        out.append(blk.reshape(tuple(s[:ax]) + (N_DEV * s[ax],) + tuple(s[ax + 1:])))
        off += n
    return out


def _shard_cols(blocks, a, b):
    shard = blocks[0].shape[1]
    out = []
    while a < b:
        d = a // shard
        hi = min(b, (d + 1) * shard)
        out.append(blocks[d][:, a - d * shard:hi - d * shard])
        a = hi
    return out


def _permute_w_in(blocks):
    main = jnp.concatenate(_shard_cols(blocks, O_GQ, O_BL) + _shard_cols(blocks, O_GZ, O_END)
                           + _shard_cols(blocks, O_FQ, O_FL), axis=1)
    pad = jnp.zeros((blocks[0].shape[0], LANE - 24), blocks[0].dtype)
    small = jnp.concatenate(_shard_cols(blocks, O_FL, O_GQ) + _shard_cols(blocks, O_BL, O_GZ) + [pad], axis=1)
    return main, small


_W_IN_SEGS = ((O_FQ, O_FL, True, C_FQ), (O_FL, O_GQ, False, 0), (O_GQ, O_BL, True, C_GQ), (O_BL, O_GZ, False, 8),
              (O_GZ, O_END, True, C_GZ))


def _unpermute_cols(main, small, a, b):
    out = []
    for s0, s1, is_main, t0 in _W_IN_SEGS:
        lo, hi = max(a, s0), min(b, s1)
        if lo < hi:
            out.append((main if is_main else small)[:, t0 + lo - s0:t0 + hi - s0])
    return jnp.concatenate(out, axis=1)


def _lanes(vec, start):
    return jnp.pad(vec.astype(F32), (start, LANE - start - vec.shape[0])).reshape(1, LANE)


BIG = ("w_in", "w_branch_a", "w_branch_b", "w_out", "w_up", "w_down")
BIG_AXES = (2, 2, 1, 1, 2, 1)
SHARDED_SMALL = ("meta_tokens", "gdn_conv_w", "ffn_conv_w")
SHARDED_SMALL_AXES = (1, 2, 2)
REPL = ("norm1_g", "fox_f_bias", "fox_q_norm_g", "fox_k_norm_g", "gdn_a_log", "gdn_dt_bias", "gdn_norm_g", "norm2_g")
ORDER = ("meta_tokens", "norm1_g", "w_in", "fox_f_bias", "fox_q_norm_g", "fox_k_norm_g", "gdn_conv_w", "gdn_a_log",
         "gdn_dt_bias", "gdn_norm_g", "w_branch_a", "w_branch_b", "w_out", "norm2_g", "w_up", "ffn_conv_w", "w_down")


def _layer_fwd(h_res, wl, pad_rows):
    lp = h_res.shape[0]
    sv = {"res_in": h_res}
    (h1,) = _rowwise(_rmsnorm_fn, [(h_res, D, 0)], [wl["norm1_g"]], [(D, BF16)], "rmsnorm1")
    proj = _matmul(h1, wl["w_main"], "nn", F32, name="mm_in")
    small = _matmul(h1, wl["w_small"], "nn", F32, name="mm_in_small")
    sv.update(h1=h1, proj=proj, small=small)

    fox_fn = functools.partial(_fox_prep_fn, pad_rows)
    qh, kh, logf = _rowwise(fox_fn, [(proj, FOX_W, C_FQ // FOX_W), (proj, FOX_W, C_FK // FOX_W), (small, LANE, 0)],
                            [wl["qg"], wl["kg"], wl["fb"]], [(FOX_W, BF16), (FOX_W, BF16), (LANE, F32)], "fox_prep")
    fsum = _cumsum_rows(logf, False, "fox_cumsum")
    frow = fsum[:, :8].T.reshape(8, 1, lp)
    o_a, lse = _fox_fwd(qh, kh, proj, fsum, frow, pad_rows, C_FV // LANE)
    y_a = _matmul(o_a, wl["w_branch_a"], "nn", F32, name="mm_branch_a")
    sv.update(qh=qh, kh=kh, fsum=fsum, frow=frow, o_a=o_a, lse=lse)

    gq, gk, gv, bg = _gdn_act(proj, small, wl["gdn_conv_w"], wl["alog"], wl["dtb"])
    grow = bg[:, 16:24].T.reshape(8, lp // CHUNK, 1, CHUNK)
    o_raw, sprev, tinv = _gdn_fwd(gq, gk, gv, bg, grow)
    (o_b,) = _rowwise(_gdn_post_fn, [(o_raw, GDN_W, 0), (proj, GDN_W, C_GZ // GDN_W)], [wl["gn"]], [(GDN_W, BF16)],
                      "gdn_post")
    y_b = _matmul(o_b, wl["w_branch_b"], "nn", F32, name="mm_branch_b")
    sv.update(gq=gq, gk=gk, gv=gv, bg=bg, grow=grow, o_raw=o_raw, sprev=sprev, tinv=tinv, o_b=o_b)

    (mixed,) = _rowwise(_merge_fn, [(proj, D, C_GATE // D), (proj, D, C_GATE // D + 1), (y_a, D, 0), (y_b, D, 0)], [],
                        [(D, BF16)], "merge")
    res_mid = _matmul(mixed, wl["w_out"], "nn", F32, add=h_res, name="mm_out")
    sv.update(y_a=y_a, y_b=y_b, mixed=mixed, res_mid=res_mid)

    (h2,) = _rowwise(_rmsnorm_fn, [(res_mid, D, 0)], [wl["norm2_g"]], [(D, BF16)], "rmsnorm2")
    up_pre = _matmul(h2, wl["w_up"], "nn", F32, name="mm_up")
    act = _ffn_act(up_pre, wl["ffn_conv_w"])
    out = _matmul(act, wl["w_down"], "nn", F32, add=res_mid, name="mm_down")
    sv.update(h2=h2, up_pre=up_pre, act=act)
    return out, sv


def _layer_bwd(dres, wl, sv, pad_rows):
    lp = dres.shape[0]
    gw = {}
    gw["w_down"] = _matmul(sv["act"], dres, "tn", F32, name="mm_dw_down")
    dact = _matmul(dres, wl["w_down"], "nt", BF16, name="mm_dact")
    dup, gw["ffn_conv_w"] = _ffn_act_bwd(sv["up_pre"], wl["ffn_conv_w"], dact)
    dup_pre = _conv_bwd_x(dup, wl["ffn_conv_w"], pad_rows, D_FF, "ffn_conv_bwd")
    gw["w_up"] = _matmul(sv["h2"], dup_pre, "tn", F32, name="mm_dw_up")
    dh2 = _matmul(dup_pre, wl["w_up"], "nt", F32, name="mm_dh2")
    (dmid,), (gw["norm2_g"],) = _rowwise_bwd(_rmsnorm_fn, [(sv["res_mid"], D, 0)], [wl["norm2_g"]], [(dh2, D, 0)],
                                             "rmsnorm2_bwd", pad_rows, [F32], adds=[(dres, D, 0)])
    gw["w_out"] = _matmul(sv["mixed"], dmid, "tn", F32, name="mm_dw_out")
    dmixed = _matmul(dmid, wl["w_out"], "nt", F32, name="mm_dmixed")
    proj, small = sv["proj"], sv["small"]
    (dg0, dg1, dya, dyb), _ = _rowwise_bwd(
        _merge_fn, [(proj, D, C_GATE // D), (proj, D, C_GATE // D + 1), (sv["y_a"], D, 0), (sv["y_b"], D, 0)], [],
        [(dmixed, D, 0)], "merge_bwd", pad_rows, [BF16, BF16, BF16, BF16])
    gw["w_branch_a"] = _matmul(sv["o_a"], dya, "tn", F32, name="mm_dw_a")
    do_a = _matmul(dya, wl["w_branch_a"], "nt", BF16, name="mm_do_a")
    gw["w_branch_b"] = _matmul(sv["o_b"], dyb, "tn", F32, name="mm_dw_b")
    do_b = _matmul(dyb, wl["w_branch_b"], "nt", F32, name="mm_do_b")

    (do_raw, dgz), (gw["gn"],) = _rowwise_bwd(_gdn_post_fn, [(sv["o_raw"], GDN_W, 0), (proj, GDN_W, C_GZ // GDN_W)],
                                              [wl["gn"]], [(do_b, GDN_W, 0)], "gdn_post_bwd", pad_rows, [F32, BF16])
    dgq, dgk, dgv, dbg, dgrow = _gdn_bwd(sv["gq"], sv["gk"], sv["gv"], sv["bg"], sv["grow"], sv["sprev"],
                                         sv["tinv"], do_raw)
    dbg = dbg + jnp.pad(dgrow.reshape(8, lp).T, ((0, 0), (16, LANE - 24)))
    dconv, dsmall_g, gw["alog"], gw["dtb"], gw["gdn_conv_w"] = _gdn_act_bwd(
        proj, small, wl["gdn_conv_w"], wl["alog"], wl["dtb"], dgq, dgk, dgv, dbg)
    dqkv = _conv_bwd_x(dconv, wl["gdn_conv_w"], pad_rows, GDN_W, "gdn_conv_bwd")

    (delta,) = _rowwise(_fox_delta_fn, [(sv["o_a"], FOX_W, 0), (do_a, FOX_W, 0)], [], [(LANE, F32)], "fox_delta")
    dqh, dkh, dvh, dfq, dfk = _fox_bwd(sv["qh"], sv["kh"], proj, sv["fsum"], sv["frow"], do_a, sv["lse"], delta,
                                       pad_rows, C_FV // LANE)
    df8 = dfq.reshape(lp, FOX_W // LANE, LANE)[:, :, :2].reshape(lp, 8) + dfk.reshape(8, lp).T
    dlogf = _cumsum_rows(jnp.pad(df8, ((0, 0), (0, LANE - 8))), True, "fox_cumsum_bwd")
    fox_fn = functools.partial(_fox_prep_fn, pad_rows)
    (dfq_p, dfk_p, dsmall_f), (gw["qg"], gw["kg"], gw["fb"]) = _rowwise_bwd(
        fox_fn, [(proj, FOX_W, C_FQ // FOX_W), (proj, FOX_W, C_FK // FOX_W), (small, LANE, 0)],
        [wl["qg"], wl["kg"], wl["fb"]], [(dqh, FOX_W, 0), (dkh, FOX_W, 0), (dlogf, LANE, 0)],
        "fox_prep_bwd", pad_rows, [BF16, BF16, F32], adds=[None, None, (dsmall_g, LANE, 0)])

    dproj = jnp.concatenate([dqkv, dgz, dg0, dg1, dfq_p, dfk_p, dvh], axis=1)
    gw["w_main"] = _matmul(sv["h1"], dproj, "tn", F32, name="mm_dw_main")
    gw["w_small"] = _matmul(sv["h1"], dsmall_f, "tn", F32, name="mm_dw_small")
    dh1 = _matmul(dproj, wl["w_main"], "nt", F32, name="mm_dh1")
    dh1 = _matmul(dsmall_f, wl["w_small"], "nt", F32, add=dh1, name="mm_dh1_small")
    (din,), (gw["norm1_g"],) = _rowwise_bwd(_rmsnorm_fn, [(sv["res_in"], D, 0)], [wl["norm1_g"]], [(dh1, D, 0)],
                                            "rmsnorm1_bwd", pad_rows, [F32], adds=[(dmid, D, 0)])
    return din, gw


def kernel(x, meta_tokens, norm1_g, w_in, fox_f_bias, fox_q_norm_g, fox_k_norm_g, gdn_conv_w, gdn_a_log, gdn_dt_bias, gdn_norm_g, w_branch_a, w_branch_b, w_out, norm2_g, w_up, ffn_conv_w, w_down, loss_target, m_meta_tokens, m_norm1_g, m_w_in, m_fox_f_bias, m_fox_q_norm_g, m_fox_k_norm_g, m_gdn_conv_w, m_gdn_a_log, m_gdn_dt_bias, m_gdn_norm_g, m_w_branch_a, m_w_branch_b, m_w_out, m_norm2_g, m_w_up, m_ffn_conv_w, m_w_down, v_meta_tokens, v_norm1_g, v_w_in, v_fox_f_bias, v_fox_q_norm_g, v_fox_k_norm_g, v_gdn_conv_w, v_gdn_a_log, v_gdn_dt_bias, v_gdn_norm_g, v_w_branch_a, v_w_branch_b, v_w_out, v_norm2_g, v_w_up, v_ffn_conv_w, v_w_down):
    w = dict(meta_tokens=meta_tokens, norm1_g=norm1_g, w_in=w_in, fox_f_bias=fox_f_bias, fox_q_norm_g=fox_q_norm_g,
             fox_k_norm_g=fox_k_norm_g, gdn_conv_w=gdn_conv_w, gdn_a_log=gdn_a_log, gdn_dt_bias=gdn_dt_bias,
             gdn_norm_g=gdn_norm_g, w_branch_a=w_branch_a, w_branch_b=w_branch_b, w_out=w_out, norm2_g=norm2_g,
             w_up=w_up, ffn_conv_w=ffn_conv_w, w_down=w_down)
    mom = dict(meta_tokens=m_meta_tokens, norm1_g=m_norm1_g, w_in=m_w_in, fox_f_bias=m_fox_f_bias,
               fox_q_norm_g=m_fox_q_norm_g, fox_k_norm_g=m_fox_k_norm_g, gdn_conv_w=m_gdn_conv_w,
               gdn_a_log=m_gdn_a_log, gdn_dt_bias=m_gdn_dt_bias, gdn_norm_g=m_gdn_norm_g, w_branch_a=m_w_branch_a,
               w_branch_b=m_w_branch_b, w_out=m_w_out, norm2_g=m_norm2_g, w_up=m_w_up, ffn_conv_w=m_ffn_conv_w,
               w_down=m_w_down)
    var = dict(meta_tokens=v_meta_tokens, norm1_g=v_norm1_g, w_in=v_w_in, fox_f_bias=v_fox_f_bias,
               fox_q_norm_g=v_fox_q_norm_g, fox_k_norm_g=v_fox_k_norm_g, gdn_conv_w=v_gdn_conv_w,
               gdn_a_log=v_gdn_a_log, gdn_dt_bias=v_gdn_dt_bias, gdn_norm_g=v_gdn_norm_g, w_branch_a=v_w_branch_a,
               w_branch_b=v_w_branch_b, w_out=v_w_out, norm2_g=v_norm2_g, w_up=v_w_up, ffn_conv_w=v_ffn_conv_w,
               w_down=v_w_down)
    depth = norm1_g.shape[0]
    seq = x.shape[1]
    l_tok = N_META + seq
    lp = -(-l_tok // LANE) * LANE
    pad_rows = lp - l_tok
    row_start = pad_rows + N_META

    me = 4 * lax.axis_index("x") + 2 * lax.axis_index("y") + lax.axis_index("c")
    got = [{n: _all_gather(w[n][0].astype(BF16), "gather_" + n) for n in BIG}]
    later, got[0] = lax.optimization_barrier(([[w[n][l].astype(BF16) for n in BIG] for l in range(1, depth)], got[0]))
    started_g = [_exchange_start(later[l - 1], "gather_start_%d" % l, same_block=True) for l in range(1, depth)]
    small_shapes = [w[n].shape for n in SHARDED_SMALL]
    gathered_s = _all_gather(_pack([w[n] for n in SHARDED_SMALL], LANE, F32, 8), "gather_small")
    full = dict(zip(SHARDED_SMALL, _unpack_gathered(gathered_s, small_shapes, SHARDED_SMALL_AXES)))

    def layer_weights(l, blocks):
        def join(name):
            return jnp.concatenate([blocks[name][d] for d in range(N_DEV)], axis=BIG_AXES[BIG.index(name)] - 1)

        w_main, w_small = _permute_w_in([blocks["w_in"][d] for d in range(N_DEV)])
        return dict(
            w_main=w_main, w_small=w_small, w_branch_a=join("w_branch_a"), w_branch_b=join("w_branch_b"),
            w_out=join("w_out"), w_up=join("w_up"), w_down=join("w_down"),
            gdn_conv_w=full["gdn_conv_w"][l], ffn_conv_w=full["ffn_conv_w"][l],
            norm1_g=norm1_g[l].reshape(1, D), norm2_g=norm2_g[l].reshape(1, D),
            qg=jnp.tile(fox_q_norm_g[l], 8).reshape(1, FOX_W), kg=jnp.tile(fox_k_norm_g[l], 8).reshape(1, FOX_W),
            fb=_lanes(fox_f_bias[l], 0), alog=_lanes(gdn_a_log[l], 16), dtb=_lanes(gdn_dt_bias[l], 16),
            gn=jnp.tile(gdn_norm_g[l], 8).reshape(1, GDN_W))

    def arrived(l, after):
        send_sems, recv_sems, srcs, lands, _ = started_g[l - 1]
        srcs, lands = _exchange_wait(send_sems, recv_sems, srcs, lands, after, "gather_wait_%d" % l, same_block=True)
        return {n: lax.dynamic_update_index_in_dim(ld, sr, me, 0) for n, sr, ld in zip(BIG, srcs, lands)}

    h_res = jnp.concatenate([jnp.zeros((pad_rows, D), F32), full["meta_tokens"], x[0]], axis=0)
    for st in started_g:
        h_res = h_res + st[4][0:1, 0:1]
    layers, saved = [], []
    for l in range(depth):
        layers.append(layer_weights(l, got[0] if l == 0 else arrived(l, h_res)))
        h_res, sv = _layer_fwd(h_res, layers[l], pad_rows)
        saved.append(sv)
    dres, loss_part = _loss_head(h_res, loss_target[0], row_start)
    loss = lax.psum(loss_part[0, 0], ("x", "y", "c"))

    def dest_block(name, g, d):
        if name == "w_in":
            s = w_in.shape[2]
            return _unpermute_cols(g["w_main"], g["w_small"], s * d, s * (d + 1)).astype(BF16)
        if BIG_AXES[BIG.index(name)] == 2:
            s = w[name].shape[2]
            return g[name][:, s * d:s * (d + 1)].astype(BF16)
        s = w[name].shape[1]
        return g[name][s * d:s * (d + 1), :].astype(BF16)

    gws = [None] * depth
    started = [None] * depth
    for l in reversed(range(depth)):
        dres, gws[l] = _layer_bwd(dres, layers[l], saved[l], pad_rows)
        started[l] = _exchange_start([jnp.stack([dest_block(n, gws[l], d) for d in range(N_DEV)]) for n in BIG],
                                     "scatter_start_%d" % l)
        if l > 0:
            token = started[l][4][0, 0].astype(BF16)
            layers[l - 1] = dict(layers[l - 1], w_down=layers[l - 1]["w_down"] + token)
    landed = [None] * depth
    for l in reversed(range(depth)):
        send_sems, recv_sems, srcs, lands, _ = started[l]
        srcs, lands = _exchange_wait(send_sems, recv_sems, srcs, lands, dres, "scatter_wait_%d" % l)
        landed[l] = [lax.dynamic_update_index_in_dim(ld, lax.dynamic_index_in_dim(sr, me, 0, keepdims=False), me, 0)
                     for sr, ld in zip(srcs, lands)]
    grad_x = dres[row_start:].reshape(x.shape)

    def stack(fn):
        return jnp.stack([fn(g) for g in gws])

    part = dict(
        meta_tokens=dres[pad_rows:row_start],
        norm1_g=stack(lambda g: g["norm1_g"][0]), norm2_g=stack(lambda g: g["norm2_g"][0]),
        fox_f_bias=stack(lambda g: g["fb"][0, 0:8]),
        fox_q_norm_g=stack(lambda g: g["qg"].reshape(8, FOX_DH).sum(0)),
        fox_k_norm_g=stack(lambda g: g["kg"].reshape(8, FOX_DH).sum(0)),
        gdn_conv_w=stack(lambda g: g["gdn_conv_w"]), gdn_a_log=stack(lambda g: g["alog"][0, 16:24]),
        gdn_dt_bias=stack(lambda g: g["dtb"][0, 16:24]),
        gdn_norm_g=stack(lambda g: g["gn"].reshape(8, GDN_DH).sum(0)),
        ffn_conv_w=stack(lambda g: g["ffn_conv_w"]))

    res = {}
    for idx, n in enumerate(BIG):
        parts = jnp.stack([landed[l][idx] for l in range(depth)], axis=1)
        res[n] = _sum_adamw(parts, w[n], mom[n], var[n], "adamw_" + n)

    small_names = SHARDED_SMALL + REPL
    small_axes = SHARDED_SMALL_AXES + (None,) * len(REPL)
    landed_s = _all_to_all(_pack_dest([part[n] for n in small_names], small_axes, LANE, F32, 8), "scatter_small")
    shapes_s = [w[n].shape for n in small_names]
    outs = _sum_adamw(landed_s[:, None], *[_pack([d[n] for n in small_names], LANE, F32, 8)[None] for d in (w, mom, var)],
                      "adamw_small")
    for o_idx, packed in enumerate(outs):
        for n, a in zip(small_names, _unpack(packed[0], shapes_s)):
            res.setdefault(n, [None] * 4)[o_idx] = a

    return (loss, grad_x, *[res[n][0] for n in ORDER], *[res[n][1] for n in ORDER],
            *[res[n][2] for n in ORDER], *[res[n][3] for n in ORDER])
```

```python
import functools

import jax
import jax.numpy as jnp
from jax import lax
from jax.experimental import pallas as pl
from jax.experimental.pallas import tpu as pltpu

F32, BF16 = jnp.float32, jnp.bfloat16
MESH = pl.DeviceIdType.MESH

D = 1024
N_META = 16
DEPTH = 4
EPS = 1e-6
LOG2E = 1.4426950408889634
LN2 = 0.6931471805599453
NEG = -1e30
FOX_W, FOX_DH = 512, 64
GDN_W, GDN_DH, GDN_H = 1024, 128, 8
CHUNK = 64
D_FF = 2816
N_DEV = 8
ADAM_LR, ADAM_B1, ADAM_B2, ADAM_EPS, ADAM_WD, ADAM_STEP = 0.001, 0.9, 0.999, 1e-08, 0.01, 10

VMEM_LIMIT_BYTES = 48 * 1024 * 1024
MATMUL_VMEM_BUDGET = 36 * 1024 * 1024
ROW_TILE = 128
ROW_TILES_WIDE = (640, 512, 256, 128)
ROW_TILES = (320, 256, 128)
LANE = 128

C_GQ, C_GK, C_GV, C_GZ, C_GATE, C_FQ, C_FK, C_FV = 0, 1024, 2048, 3072, 4096, 6144, 6656, 7168
W_MAIN = 7680
O_FQ, O_FK, O_FV, O_FL, O_GQ, O_GK, O_GV, O_BL, O_AL, O_GZ, O_GATE, O_END = (
    0, 512, 1024, 1536, 1544, 2568, 3592, 4616, 4624, 4632, 5656, 7704)


def _pick(n, cands):
    for c in cands:
        if n % c == 0:
            return c
    return n


def _call(body, *, name, out_shape, in_specs, out_specs, grid=(), scratch=(), sem=None):
    kw = dict(vmem_limit_bytes=VMEM_LIMIT_BYTES)
    if sem is not None:
        kw["dimension_semantics"] = sem
    return pl.pallas_call(body, name=name, out_shape=out_shape, grid=grid, in_specs=in_specs,
                          out_specs=out_specs, scratch_shapes=list(scratch),
                          compiler_params=pltpu.CompilerParams(**kw))


_DIMS = {"nn": (((1,), (0,)), ((), ())), "nt": (((1,), (1,)), ((), ())), "tn": (((0,), (0,)), ((), ()))}


_DIMS_BATCHED = {"nn": (((2,), (1,)), ((0,), (0,))), "nt": (((2,), (2,)), ((0,), (0,))),
                 "tn": (((1,), (1,)), ((0,), (0,)))}


def _dot(a, b, mode, prec=None):
    dims = _DIMS[mode] if a.ndim == 2 else _DIMS_BATCHED[mode]
    return lax.dot_general(a, b, dims, precision=prec, preferred_element_type=F32)


def _mm_grads(f, mode, a, b, g):
    if mode == "nn":
        return f(g, b, "nt"), f(a, g, "tn")
    if mode == "nt":
        return f(g, b, "nn"), f(g, a, "tn")
    return f(b, g, "nt"), f(a, g, "nn")


@functools.partial(jax.custom_vjp, nondiff_argnums=(2,))
def _mmb(a, b, mode):
    return _dot(a.astype(BF16), b.astype(BF16), mode)


def _mmb_fwd(a, b, mode):
    return _mmb(a, b, mode), (a, b)


def _mmb_bwd(mode, res, g):
    return _mm_grads(_mmb, mode, res[0], res[1], g)


_mmb.defvjp(_mmb_fwd, _mmb_bwd)


def _split(a):
    hi = a.astype(BF16)
    return hi, (a - hi.astype(F32)).astype(BF16)


@functools.partial(jax.custom_vjp, nondiff_argnums=(2,))
def _mmh(a, b, mode):
    ah, al = _split(a)
    bh, bl = _split(b)
    return _dot(ah, bh, mode) + (_dot(ah, bl, mode) + _dot(al, bh, mode))


def _mmh_fwd(a, b, mode):
    return _mmh(a, b, mode), (a, b)


def _mmh_bwd(mode, res, g):
    return _mm_grads(_mmh, mode, res[0], res[1], g)


_mmh.defvjp(_mmh_fwd, _mmh_bwd)


def _dot_sel(sel, x, mode):
    s = sel.astype(BF16)
    x1 = x.astype(BF16)
    x2, x3 = _split(x - x1.astype(F32))
    return _dot(s, x1, mode) + (_dot(s, x2, mode) + _dot(s, x3, mode))


@jax.custom_vjp
def _mms(sel, x):
    return _dot_sel(sel, x, "nn")


def _mms_fwd(sel, x):
    return _dot_sel(sel, x, "nn"), sel


def _mms_bwd(sel, g):
    return jnp.zeros_like(sel), _dot_sel(sel, g, "tn")


_mms.defvjp(_mms_fwd, _mms_bwd)


def _softplus(z):
    return jnp.maximum(z, 0.0) + jnp.log(1.0 + jnp.exp(-jnp.abs(z)))


def _log_sigmoid(z):
    return jnp.minimum(z, 0.0) - jnp.log(1.0 + jnp.exp(-jnp.abs(z)))


def _silu(z):
    return z * jax.nn.sigmoid(z)


def _iota(shape, dim):
    return lax.broadcasted_iota(jnp.int32, shape, dim)


def _inv_unit_lower_raw(n):
    c = n.shape[-1]
    ri, ci = _iota((c, c), 0), _iota((c, c), 1)
    eye = (ri == ci).astype(F32)
    dmask = (ri // 16) == (ci // 16)
    dpart = jnp.where(dmask, n, 0.0)
    lpart = n - dpart
    x = -dpart
    p = eye + x
    for _ in range(3):
        x = _mmh(x, x, "nn")
        p = p + _mmh(p, x, "nn")
    m = -_mmh(p, lpart, "nn")
    q = eye + m
    steps = 1
    while (1 << steps) < c // 16:
        steps += 1
    for _ in range(steps - 1):
        m = _mmh(m, m, "nn")
        q = q + _mmh(q, m, "nn")
    return _mmh(q, p, "nn")


@jax.custom_vjp
def _inv_given(n, t):
    return t


def _inv_given_fwd(n, t):
    return t, t


def _inv_given_bwd(t, g):
    c = t.shape[-1]
    strict = _iota((c, c), 0) > _iota((c, c), 1)
    d = -_mmh(_mmh(t, g, "tn"), t, "nt")
    return jnp.where(strict, d, 0.0), jnp.zeros_like(t)


_inv_given.defvjp(_inv_given_fwd, _inv_given_bwd)


def _shift_down(x, halo, s):
    if s == 0:
        return x
    xs = pltpu.roll(x, s, 0)
    hs = pltpu.roll(halo, s, 0)
    top = jnp.where(_iota(hs.shape, 0) < s, hs, xs[0:8])
    return jnp.concatenate([top, xs[8:]], axis=0)


def _shift_up(x, halo, s):
    if s == 0:
        return x
    tm = x.shape[0]
    xs = pltpu.roll(x, tm - s, 0)
    hs = pltpu.roll(halo, 8 - s, 0)
    bot = jnp.where(_iota(hs.shape, 0) >= 8 - s, hs, xs[tm - 8:])
    return jnp.concatenate([xs[:tm - 8], bot], axis=0)


def _causal_conv(x, halo, w):
    kk = w.shape[0]
    y = x * w[kk - 1:kk, :]
    for k in range(kk - 1):
        y = y + _shift_down(x, halo, kk - 1 - k) * w[k:k + 1, :]
    return y


def _head_scale(x, width, fn):
    outs = []
    for h in range(x.shape[1] // width):
        seg = x[:, h * width:(h + 1) * width]
        outs.append(seg * fn(jnp.sum(seg * seg, axis=1, keepdims=True)))
    return jnp.concatenate(outs, axis=1)


def _matmul(a, b, mode, out_dtype, add=None, name="mm"):
    if mode == "nn":
        (m, k), n = a.shape, b.shape[1]
    elif mode == "nt":
        (m, k), n = a.shape, b.shape[0]
    else:
        (k, m), n = a.shape, b.shape[1]
    tm = _pick(m, (1408, 1024, 512, 256, 128) if mode == "tn" else (640, 512, 256, 128))
    tn = _pick(n, (1536, 1408, 1024, 768, 512, 256, 128))
    sa, sb = a.dtype.itemsize, b.dtype.itemsize
    fixed = tm * tn * 4 * (3 + (2 if add is not None else 0))
    tk = 128
    for cand in (k, 2816, 2560, 1664, 1536, 1280, 1024, 832, 768, 640, 512, 256, 128):
        if mode != "tn" and cand != k and cand % LANE:
            continue
        if k % cand == 0 and fixed + 2 * cand * (tm * sa + tn * sb) <= MATMUL_VMEM_BUDGET:
            tk = cand
            break
    nk = k // tk
    a_spec = {"nn": pl.BlockSpec((tm, tk), lambda i, j, kk: (i, kk)),
              "nt": pl.BlockSpec((tm, tk), lambda i, j, kk: (i, kk)),
              "tn": pl.BlockSpec((tk, tm), lambda i, j, kk: (kk, i))}[mode]
    b_spec = {"nn": pl.BlockSpec((tk, tn), lambda i, j, kk: (kk, j)),
              "nt": pl.BlockSpec((tn, tk), lambda i, j, kk: (j, kk)),
              "tn": pl.BlockSpec((tk, tn), lambda i, j, kk: (kk, j))}[mode]
    o_spec = pl.BlockSpec((tm, tn), lambda i, j, kk: (i, j))
    has_add = add is not None

    def body(*refs):
        a_ref, b_ref = refs[0], refs[1]
        add_ref = refs[2] if has_add else None
        o_ref = refs[3] if has_add else refs[2]
        part = _dot(a_ref[...].astype(BF16), b_ref[...].astype(BF16), mode)
        if nk == 1:
            if has_add:
                part = part + add_ref[...].astype(F32)
            o_ref[...] = part.astype(out_dtype)
        else:
            acc = refs[-1]
            kk = pl.program_id(2)

            @pl.when(kk == 0)
            def _():
                acc[...] = part

            @pl.when(kk > 0)
            def _():
                acc[...] += part

            @pl.when(kk == nk - 1)
            def _():
                r = acc[...]
                if has_add:
                    r = r + add_ref[...].astype(F32)
                o_ref[...] = r.astype(out_dtype)

    ins = [a, b] + ([add] if has_add else [])
    specs = [a_spec, b_spec] + ([o_spec] if has_add else [])
    return _call(body, name=name, out_shape=jax.ShapeDtypeStruct((m, n), out_dtype), grid=(m // tm, n // tn, nk),
                 in_specs=specs, out_specs=o_spec,
                 scratch=[pltpu.VMEM((tm, tn), F32)] if nk > 1 else [],
                 sem=("parallel", "parallel", "arbitrary"))(*ins)


def _row_spec(width, colblock, tm):
    return pl.BlockSpec((tm, width), lambda i, cb=colblock: (i, cb))


def _full_spec(arr):
    nd = arr.ndim
    return pl.BlockSpec(arr.shape, lambda i, nd=nd: (0,) * nd)


def _rowwise(fn, rows, params, outs, name):
    lp = rows[0][0].shape[0]
    tm = _pick(lp, ROW_TILES_WIDE)
    nr, npar = len(rows), len(params)

    def body(*refs):
        row0 = pl.program_id(0) * tm
        vals = [r[...].astype(F32) for r in refs[:nr + npar]]
        res = fn(*vals, row0)
        for o_ref, r in zip(refs[nr + npar:], res):
            o_ref[...] = r.astype(o_ref.dtype)

    out = _call(body, name=name, grid=(lp // tm,),
                out_shape=[jax.ShapeDtypeStruct((lp, w), dt) for w, dt in outs],
                in_specs=[_row_spec(w, cb, tm) for _, w, cb in rows] + [_full_spec(p) for p in params],
                out_specs=[_row_spec(w, 0, tm) for w, _ in outs], sem=("parallel",))(
                    *[r[0] for r in rows], *params)
    return out


def _rowwise_bwd(fn, rows, params, cts, name, pad_rows, grad_dtypes, adds=None):
    lp = rows[0][0].shape[0]
    tm = _pick(lp, ROW_TILES)
    nr, npar, nct = len(rows), len(params), len(cts)
    adds = adds or [None] * nr
    add_list = [a for a in adds if a is not None]
    nadd = len(add_list)

    def body(*refs):
        i = pl.program_id(0)
        row0 = i * tm
        vals = [r[...].astype(F32) for r in refs[:nr + npar]]
        ct_vals = tuple(r[...].astype(F32) for r in refs[nr + npar:nr + npar + nct])
        add_refs = list(refs[nr + npar + nct:nr + npar + nct + nadd])
        outs = refs[nr + npar + nct + nadd:]
        _, vjp = jax.vjp(lambda *args: tuple(fn(*args, row0)), *vals)
        grads = vjp(ct_vals)
        valid = (row0 + _iota((tm, 1), 0)) >= pad_rows
        for idx in range(nr):
            g = jnp.where(valid, grads[idx], 0.0)
            if adds[idx] is not None:
                g = g + add_refs.pop(0)[...].astype(F32)
            outs[idx][...] = g.astype(outs[idx].dtype)
        for idx in range(npar):
            o_ref = outs[nr + idx]

            @pl.when(i == 0)
            def _(o_ref=o_ref):
                o_ref[...] = jnp.zeros_like(o_ref)

            o_ref[...] += grads[nr + idx]

    out = _call(body, name=name, grid=(lp // tm,),
                out_shape=[jax.ShapeDtypeStruct((lp, w), dt) for (_, w, _), dt in zip(rows, grad_dtypes)]
                + [jax.ShapeDtypeStruct(p.shape, F32) for p in params],
                in_specs=[_row_spec(w, cb, tm) for _, w, cb in rows] + [_full_spec(p) for p in params]
                + [_row_spec(w, cb, tm) for _, w, cb in cts] + [_row_spec(w, cb, tm) for _, w, cb in add_list],
                out_specs=[_row_spec(w, 0, tm) for _, w, _ in rows] + [_full_spec(p) for p in params],
                sem=("arbitrary",))(*[r[0] for r in rows], *params, *[c[0] for c in cts], *[a[0] for a in add_list])
    return out[:nr], out[nr:]


def _rmsnorm_fn(x, g, row0):
    return (x * lax.rsqrt(jnp.mean(x * x, axis=1, keepdims=True) + EPS) * g,)


def _fox_prep_fn(pad_rows, fq, fk, small, qg, kg, fb, row0):
    ri, ci = _iota((FOX_W, FOX_W), 0), _iota((FOX_W, FOX_W), 1)
    bd = jnp.where((ri // FOX_DH) == (ci // FOX_DH), 1.0 / FOX_DH, 0.0)

    def hn(x, g):
        return x * lax.rsqrt(_mmh(x * x, bd, "nn") + EPS) * g

    tm = small.shape[0]
    keep = (_iota((tm, LANE), 1) < 8) & ((row0 + _iota((tm, LANE), 0)) >= pad_rows)
    logf = jnp.where(keep, _log_sigmoid(small + fb) * LOG2E, 0.0)
    return hn(fq, qg) * (FOX_DH ** -0.5 * LOG2E), hn(fk, kg), logf


def _gdn_act_fn(cq, ck, cv, small, alog, dtb):
    tm = small.shape[0]
    q = _head_scale(_silu(cq), GDN_DH, lambda s: lax.rsqrt(s + EPS) * (GDN_DH ** -0.5))
    k = _head_scale(_silu(ck), GDN_DH, lambda s: lax.rsqrt(s + EPS))
    v = _silu(cv)
    lane = _iota((tm, LANE), 1)
    beta = jnp.where((lane >= 8) & (lane < 16), jax.nn.sigmoid(small), 0.0)
    g = jnp.where((lane >= 16) & (lane < 24), -jnp.exp(alog) * _softplus(small + dtb), 0.0)
    ri, ci = _iota((tm, tm), 0), _iota((tm, tm), 1)
    tri = jnp.where(((ri // CHUNK) == (ci // CHUNK)) & (ci <= ri), 1.0, 0.0)
    return q, k, v, beta + _mms(tri, g)


def _gdn_post_fn(o, gz, gn, row0):
    return (_head_scale(o, GDN_DH, lambda s: lax.rsqrt(s * (1.0 / GDN_DH) + EPS)) * gn * _silu(gz),)


def _merge_fn(g0, g1, ya, yb, row0):
    return (jax.nn.sigmoid(g0) * ya + jax.nn.sigmoid(g1) * yb,)


def _cumsum_rows(x, reverse, name):
    lp, w = x.shape
    tm = _pick(lp, (640, 512, 256, 128))
    nt = lp // tm

    def body(x_ref, o_ref, carry):
        i = pl.program_id(0)

        @pl.when(i == 0)
        def _():
            carry[...] = jnp.zeros_like(carry)

        ri, ci = _iota((tm, tm), 0), _iota((tm, tm), 1)
        tri = jnp.where((ci >= ri) if reverse else (ci <= ri), 1.0, 0.0)
        blk = x_ref[...]
        o_ref[...] = _dot_sel(tri, blk, "nn") + carry[0:1, :]
        carry[...] = carry[...] + jnp.sum(blk, axis=0, keepdims=True)

    idx = (lambda i: (nt - 1 - i, 0)) if reverse else (lambda i: (i, 0))
    return _call(body, name=name, grid=(nt,), out_shape=jax.ShapeDtypeStruct((lp, w), F32),
                 in_specs=[pl.BlockSpec((tm, w), idx)], out_specs=pl.BlockSpec((tm, w), idx),
                 scratch=[pltpu.VMEM((8, w), F32)], sem=("arbitrary",))(x)


def _fox_scores(q, k, fq, fk, hh, qpos0, kpos0, pad_rows, masked):
    tq, tk = q.shape[0], k.shape[0]
    lane = _iota(q.shape, 1)
    sel = (lane < FOX_DH) if hh == 0 else (lane >= FOX_DH)
    s = _dot(jnp.where(sel, q, jnp.zeros_like(q)), k, "nt") + fq - fk
    if not masked:
        return s, None, sel
    qpos = qpos0 + _iota((tq, tk), 0)
    kpos = kpos0 + _iota((tq, tk), 1)
    mask = (kpos <= qpos) & (kpos >= pad_rows)
    return jnp.where(mask, s, NEG), mask, sel


def _probs(s, mask, shift):
    p = jnp.exp2(s - shift)
    return p if mask is None else jnp.where(mask, p, 0.0)


def _both_variants(needs_mask, fn):
    @pl.when(needs_mask)
    def _():
        fn(True)

    @pl.when(jnp.logical_not(needs_mask))
    def _():
        fn(False)


def _lane_col(blk, lane_idx):
    return jnp.sum(jnp.where(_iota(blk.shape, 1) == lane_idx, blk, 0.0), axis=1, keepdims=True)


def _to_lanes(cols, width=LANE):
    lane = _iota((cols[0].shape[0], width), 1)
    out = jnp.zeros((cols[0].shape[0], width), F32)
    for idx, c in enumerate(cols):
        out = jnp.where(lane == idx, c, out)
    return out


def _fox_fwd(q, k, v, fsum, frow, pad_rows, v_col):
    lp = q.shape[0]
    t = _pick(lp, (640, 512, 256, 128))
    n = lp // t

    def body(q_ref, k_ref, v_ref, f_ref, fk_ref, o_ref, lse_ref, acc, m_s, fq_s):
        pr, i, j = pl.program_id(0), pl.program_id(1), pl.program_id(2)

        @pl.when(j == 0)
        def _():
            acc[...] = jnp.zeros_like(acc)
            m_s[...] = jnp.full_like(m_s, NEG)
            for hh in range(2):
                fq_s[hh] = _lane_col(f_ref[...], 2 * pr + hh)

        def step(masked):
            vv = v_ref[...].astype(BF16)
            for hh in range(2):
                s, mask, sel = _fox_scores(q_ref[...], k_ref[...], fq_s[hh], fk_ref[hh], hh, i * t, j * t, pad_rows,
                                           masked)
                m_prev = m_s[hh]
                m_new = jnp.maximum(m_prev, jnp.max(s, axis=1, keepdims=True))
                p = _probs(s, mask, m_new).astype(BF16)
                pv = _dot(p, jnp.where(sel, vv, jnp.ones_like(vv)), "nn")
                acc[hh] = jnp.exp2(m_prev - m_new) * acc[hh] + pv
                m_s[hh] = m_new

        @pl.when(j <= i)
        def _():
            _both_variants((j == i) | (j == 0), step)

        @pl.when(j == i)
        def _():
            outs, lses = [], []
            for hh in range(2):
                l = _lane_col(acc[hh], FOX_DH if hh == 0 else 0)
                ok = l > 0.0
                outs.append(acc[hh] * jnp.where(ok, 1.0 / jnp.where(ok, l, 1.0), 0.0))
                lses.append(jnp.where(ok, m_s[hh] + jnp.log2(jnp.where(ok, l, 1.0)), 0.0))
            lane = _iota((t, LANE), 1)
            o_ref[...] = jnp.where(lane < FOX_DH, outs[0], outs[1]).astype(o_ref.dtype)
            lse_ref[...] = _to_lanes(lses)

    qspec = pl.BlockSpec((t, LANE), lambda p, i, j: (i, p))
    kspec = pl.BlockSpec((t, LANE), lambda p, i, j: (jnp.minimum(j, i), p))
    vspec = pl.BlockSpec((t, LANE), lambda p, i, j: (jnp.minimum(j, i), v_col + p))
    fspec = pl.BlockSpec((t, LANE), lambda p, i, j: (i, 0))
    rspec = pl.BlockSpec((2, 1, t), lambda p, i, j: (p, 0, jnp.minimum(j, i)))
    return _call(body, name="fox_fwd", grid=(FOX_W // LANE, n, n),
                 out_shape=[jax.ShapeDtypeStruct((lp, FOX_W), BF16), jax.ShapeDtypeStruct((lp, FOX_W), F32)],
                 in_specs=[qspec, kspec, vspec, fspec, rspec], out_specs=[qspec, qspec],
                 scratch=[pltpu.VMEM((2, t, LANE), F32), pltpu.VMEM((2, t, 1), F32), pltpu.VMEM((2, t, 1), F32)],
                 sem=("parallel", "parallel", "arbitrary"))(q, k, v, fsum, frow)


def _fox_delta_fn(o, do, row0):
    ri, ci = _iota((FOX_W, LANE), 0), _iota((FOX_W, LANE), 1)
    sel = jnp.where((ri // FOX_DH) == ci, 1.0, 0.0).astype(BF16)
    x = o * do
    x1 = x.astype(BF16)
    x2, x3 = _split(x - x1.astype(F32))
    return (_dot(x1, sel, "nn") + (_dot(x2, sel, "nn") + _dot(x3, sel, "nn")),)


def _fox_bwd(q, k, v, fsum, frow, do, lse, delta, pad_rows, v_col):
    lp = q.shape[0]
    t = _pick(lp, (640, 512, 256, 128))
    n = lp // t

    def body(q_ref, k_ref, v_ref, f_ref, fk_ref, do_ref, lse_ref, dl_ref,
             dq_ref, dk_ref, dv_ref, dfq_ref, dfk_ref, dka, dva, dfa):
        pr, j, i = pl.program_id(0), pl.program_id(1), pl.program_id(2)
        lane = _iota((t, LANE), 1)

        @pl.when((j == 0) & (i == 0))
        def _():
            dq_ref[...] = jnp.zeros_like(dq_ref)
            dfq_ref[...] = jnp.zeros_like(dfq_ref)

        @pl.when(i == 0)
        def _():
            dka[...] = jnp.zeros_like(dka)
            dva[...] = jnp.zeros_like(dva)
            dfa[...] = jnp.zeros_like(dfa)

        def step(masked):
            rows = pl.ds(pl.multiple_of(i * t, t), t)
            dq_add = jnp.zeros((t, LANE), F32)
            rowsums = []
            for hh in range(2):
                fq = _lane_col(f_ref[...], 2 * pr + hh)
                s, mask, sel = _fox_scores(q_ref[...], k_ref[...], fq, fk_ref[hh], hh, i * t, j * t, pad_rows, masked)
                p = _probs(s, mask, _lane_col(lse_ref[...], hh))
                dop = jnp.where(sel, do_ref[...], jnp.zeros_like(do_ref[...]))
                ds = p * (_dot(dop, v_ref[...].astype(BF16), "nt") - _lane_col(dl_ref[...], 2 * pr + hh))
                dsb = ds.astype(BF16)
                dva[hh] += _dot(p.astype(BF16), do_ref[...], "tn")
                dka[hh] += _dot(dsb, q_ref[...], "tn")
                dfa[hh] -= jnp.sum(ds, axis=0, keepdims=True)
                dq_add = dq_add + _dot(dsb, jnp.where(sel, k_ref[...], jnp.zeros_like(k_ref[...])), "nn")
                rowsums.append(jnp.sum(ds, axis=1, keepdims=True))
            dq_ref[rows, :] += dq_add
            dfq_ref[rows, :] += _to_lanes(rowsums)

        @pl.when(i >= j)
        def _():
            _both_variants((j == i) | (j == 0), step)

        @pl.when(i == n - 1)
        def _():
            dk_ref[...] = jnp.where(lane < FOX_DH, dka[0], dka[1])
            dv_ref[...] = (jnp.where(lane < FOX_DH, dva[0], dva[1]) * LOG2E).astype(dv_ref.dtype)
            dfk_ref[...] = dfa[...]

    qspec = pl.BlockSpec((t, LANE), lambda p, j, i: (jnp.maximum(i, j), p))
    f_q = pl.BlockSpec((t, LANE), lambda p, j, i: (jnp.maximum(i, j), 0))
    kspec = pl.BlockSpec((t, LANE), lambda p, j, i: (j, p))
    vspec = pl.BlockSpec((t, LANE), lambda p, j, i: (j, v_col + p))
    rspec = pl.BlockSpec((2, 1, t), lambda p, j, i: (p, 0, j))
    whole = pl.BlockSpec((lp, LANE), lambda p, j, i: (0, p))
    wide = jax.ShapeDtypeStruct((lp, FOX_W), F32)
    return _call(body, name="fox_bwd", grid=(FOX_W // LANE, n, n),
                 out_shape=[wide, wide, jax.ShapeDtypeStruct((lp, FOX_W), BF16), wide,
                            jax.ShapeDtypeStruct((8, 1, lp), F32)],
                 in_specs=[qspec, kspec, vspec, f_q, rspec, qspec, qspec, f_q],
                 out_specs=[whole, kspec, kspec, whole, rspec],
                 scratch=[pltpu.VMEM((2, t, LANE), F32), pltpu.VMEM((2, t, LANE), F32), pltpu.VMEM((2, 1, t), F32)],
                 sem=("parallel", "arbitrary", "arbitrary"))(q, k, v, fsum, frow, do, lse, delta)


def _gdn_chunk(q, k, v, beta, gcol, grow, s, inv):
    c = q.shape[-2]
    ri, ci = _iota((c, c), 0), _iota((c, c), 1)
    dec = jnp.exp(jnp.where(ri >= ci, gcol - grow, NEG))
    dec_strict = jnp.where(ri > ci, dec, 0.0)
    eg = jnp.exp(gcol)
    kb = k * beta
    t = inv(_mmb(kb, k, "nt") * dec_strict)
    u_hat = _mmh(t, v * beta, "nn")
    w = _mmh(t, kb * eg, "nn")
    u = u_hat - _mmb(w, s, "nn")
    o = _mmb(q * eg, s, "nn") + _mmb(_mmb(q, k, "nt") * dec, u, "nn")
    glast = jnp.sum(jnp.where(_iota((1, c), 1) == c - 1, grow, 0.0), axis=-1, keepdims=True)
    s_new = s * jnp.exp(glast) + _mmb(k * jnp.exp(glast - gcol), u, "tn")
    return o, s_new


def _gdn_specs(lp, reverse):
    n = lp // CHUNK
    pos = (lambda c: n - 1 - c) if reverse else (lambda c: c)
    wide = pl.BlockSpec((CHUNK, GDN_W), lambda c: (pos(c), 0))
    lanes = pl.BlockSpec((CHUNK, LANE), lambda c: (pos(c), 0))
    row = pl.BlockSpec((GDN_H, 1, 1, CHUNK), lambda c: (0, pos(c), 0, 0))
    st = pl.BlockSpec((GDN_H, 1, GDN_DH, GDN_DH), lambda c: (0, pos(c), 0, 0))
    return n, wide, lanes, row, st


def _heads(ref):
    return jnp.stack([ref[:, h * GDN_DH:(h + 1) * GDN_DH] for h in range(GDN_H)])


def _put_heads(ref, val):
    for h in range(GDN_H):
        ref[:, h * GDN_DH:(h + 1) * GDN_DH] = val[h]


def _head_cols(blk, lane0):
    return jnp.stack([_lane_col(blk, lane0 + h) for h in range(GDN_H)])


def _gdn_fwd(q, k, v, bg, grow):
    lp = q.shape[0]
    n, wide, lanes, row, st = _gdn_specs(lp, False)

    def body(q_ref, k_ref, v_ref, bg_ref, gr_ref, o_ref, sp_ref, t_ref, s_scr):
        @pl.when(pl.program_id(0) == 0)
        def _():
            s_scr[...] = jnp.zeros_like(s_scr)

        def inv(m):
            t = _inv_unit_lower_raw(m)
            t_ref[:, 0] = t
            return t

        s = s_scr[...]
        sp_ref[:, 0] = s
        bg_blk = bg_ref[...]
        o, s_new = _gdn_chunk(_heads(q_ref), _heads(k_ref), _heads(v_ref), _head_cols(bg_blk, 8),
                              _head_cols(bg_blk, 16), gr_ref[:, 0], s, inv)
        _put_heads(o_ref, o)
        s_scr[...] = s_new

    tri = pl.BlockSpec((GDN_H, 1, CHUNK, CHUNK), lambda c: (0, c, 0, 0))
    return _call(body, name="gdn_fwd", grid=(n,),
                 out_shape=[jax.ShapeDtypeStruct((lp, GDN_W), F32),
                            jax.ShapeDtypeStruct((GDN_H, n, GDN_DH, GDN_DH), F32),
                            jax.ShapeDtypeStruct((GDN_H, n, CHUNK, CHUNK), F32)],
                 in_specs=[wide, wide, wide, lanes, row], out_specs=[wide, st, tri],
                 scratch=[pltpu.VMEM((GDN_H, GDN_DH, GDN_DH), F32)], sem=("arbitrary",))(q, k, v, bg, grow)


def _gdn_bwd(q, k, v, bg, grow, sprev, tinv, do):
    lp = q.shape[0]
    n, wide, lanes, row, st = _gdn_specs(lp, True)

    def body(q_ref, k_ref, v_ref, bg_ref, gr_ref, sp_ref, t_ref, do_ref,
             dq_ref, dk_ref, dv_ref, dbg_ref, dgr_ref, ds_scr):
        @pl.when(pl.program_id(0) == 0)
        def _():
            ds_scr[...] = jnp.zeros_like(ds_scr)

        t_saved = t_ref[:, 0]
        fn = functools.partial(_gdn_chunk, inv=lambda m: _inv_given(m, t_saved))
        bg_blk = bg_ref[...]
        _, vjp = jax.vjp(fn, _heads(q_ref), _heads(k_ref), _heads(v_ref), _head_cols(bg_blk, 8),
                         _head_cols(bg_blk, 16), gr_ref[:, 0], sp_ref[:, 0])
        dq, dk, dv, db, dgc, dgr, ds = vjp((_heads(do_ref), ds_scr[...]))
        _put_heads(dq_ref, dq)
        _put_heads(dk_ref, dk)
        _put_heads(dv_ref, dv)
        lane = _iota((CHUNK, LANE), 1)
        dbg = jnp.zeros((CHUNK, LANE), F32)
        for h in range(GDN_H):
            dbg = jnp.where(lane == 8 + h, db[h], jnp.where(lane == 16 + h, dgc[h], dbg))
        dbg_ref[...] = dbg
        dgr_ref[:, 0] = dgr
        ds_scr[...] = ds

    wshape = jax.ShapeDtypeStruct((lp, GDN_W), F32)
    tri = pl.BlockSpec((GDN_H, 1, CHUNK, CHUNK), lambda c: (0, n - 1 - c, 0, 0))
    return _call(body, name="gdn_bwd", grid=(n,),
                 out_shape=[wshape, wshape, wshape, jax.ShapeDtypeStruct((lp, LANE), F32),
                            jax.ShapeDtypeStruct((GDN_H, n, 1, CHUNK), F32)],
                 in_specs=[wide, wide, wide, lanes, row, st, tri, wide], out_specs=[wide, wide, wide, lanes, row],
                 scratch=[pltpu.VMEM((GDN_H, GDN_DH, GDN_DH), F32)], sem=("arbitrary",))(
                     q, k, v, bg, grow, sprev, tinv, do)


def _halo_prev(width, colblock, tm):
    return pl.BlockSpec((8, width), lambda i, cb=colblock: (jnp.maximum(i * (tm // 8) - 1, 0), cb))


def _gdn_act(proj, small, conv_w, alog_row, dtb_row):
    lp = proj.shape[0]
    tm = _pick(lp, ROW_TILES)

    def body(xq, xk, xv, hq, hk, hv, wq, wk, wv, sm, al, dt, q_ref, k_ref, v_ref, bg_ref):
        first = (pl.program_id(0) > 0).astype(F32)
        cs = [_causal_conv(x[...], h[...] * first, w[...]) for x, h, w in ((xq, hq, wq), (xk, hk, wk), (xv, hv, wv))]
        q, k, v, bg = _gdn_act_fn(cs[0], cs[1], cs[2], sm[...], al[...], dt[...])
        q_ref[...], k_ref[...], v_ref[...], bg_ref[...] = q, k, v, bg

    wide = jax.ShapeDtypeStruct((lp, GDN_W), F32)
    wspec = [pl.BlockSpec((4, GDN_W), lambda i, c=c: (0, c)) for c in range(3)]
    return _call(body, name="gdn_act", grid=(lp // tm,),
                 out_shape=[wide, wide, wide, jax.ShapeDtypeStruct((lp, LANE), F32)],
                 in_specs=[_row_spec(GDN_W, c, tm) for c in range(3)] + [_halo_prev(GDN_W, c, tm) for c in range(3)]
                 + wspec + [_row_spec(LANE, 0, tm), _full_spec(alog_row), _full_spec(dtb_row)],
                 out_specs=[_row_spec(GDN_W, 0, tm)] * 3 + [_row_spec(LANE, 0, tm)], sem=("parallel",))(
                     proj, proj, proj, proj, proj, proj, conv_w, conv_w, conv_w, small, alog_row, dtb_row)


def _gdn_act_bwd(proj, small, conv_w, alog_row, dtb_row, dq, dk, dv, dbg):
    lp = proj.shape[0]
    tm = ROW_TILE

    def body(xq, xk, xv, hq, hk, hv, wq, wk, wv, sm, al, dt, dq_r, dk_r, dv_r, dbg_r,
             dc_ref, dsm_ref, dal_ref, ddt_ref, dw_ref):
        i = pl.program_id(0)
        first = (i > 0).astype(F32)
        xs = [(x[...], h[...] * first, w[...]) for x, h, w in ((xq, hq, wq), (xk, hk, wk), (xv, hv, wv))]
        cs = [_causal_conv(*t) for t in xs]
        _, vjp = jax.vjp(_gdn_act_fn, cs[0], cs[1], cs[2], sm[...], al[...], dt[...])
        dcq, dck, dcv, dsm, dal, ddt = vjp((dq_r[...], dk_r[...], dv_r[...], dbg_r[...]))
        dsm_ref[...] = dsm

        @pl.when(i == 0)
        def _():
            dal_ref[...] = jnp.zeros_like(dal_ref)
            ddt_ref[...] = jnp.zeros_like(ddt_ref)
            dw_ref[...] = jnp.zeros_like(dw_ref)

        dal_ref[...] += dal
        ddt_ref[...] += ddt
        for c, (dc, (x, h, w)) in enumerate(zip((dcq, dck, dcv), xs)):
            dc_ref[:, c * GDN_W:(c + 1) * GDN_W] = dc
            rows = [jnp.sum(_shift_down(x, h, 3 - kk) * dc, axis=0, keepdims=True) for kk in range(4)]
            dw_ref[:, c * GDN_W:(c + 1) * GDN_W] += jnp.concatenate(rows, axis=0)

    wspec = [pl.BlockSpec((4, GDN_W), lambda i, c=c: (0, c)) for c in range(3)]
    row128 = jax.ShapeDtypeStruct((1, LANE), F32)
    return _call(body, name="gdn_act_bwd", grid=(lp // tm,),
                 out_shape=[jax.ShapeDtypeStruct((lp, 3 * GDN_W), F32), jax.ShapeDtypeStruct((lp, LANE), F32),
                            row128, row128, jax.ShapeDtypeStruct((4, 3 * GDN_W), F32)],
                 in_specs=[_row_spec(GDN_W, c, tm) for c in range(3)] + [_halo_prev(GDN_W, c, tm) for c in range(3)]
                 + wspec + [_row_spec(LANE, 0, tm), _full_spec(alog_row), _full_spec(dtb_row)]
                 + [_row_spec(GDN_W, 0, tm)] * 3 + [_row_spec(LANE, 0, tm)],
                 out_specs=[_row_spec(3 * GDN_W, 0, tm), _row_spec(LANE, 0, tm),
                            _full_spec(alog_row), _full_spec(dtb_row), pl.BlockSpec((4, 3 * GDN_W), lambda i: (0, 0))],
                 sem=("arbitrary",))(proj, proj, proj, proj, proj, proj, conv_w, conv_w, conv_w, small,
                                     alog_row, dtb_row, dq, dk, dv, dbg)


def _ffn_act(up_pre, conv_w):
    lp = up_pre.shape[0]
    tm = _pick(lp, ROW_TILES)

    def body(xg, xv, hg, hv, wg, wv, a_ref):
        first = (pl.program_id(0) > 0).astype(F32)
        ug = _causal_conv(xg[...], hg[...] * first, wg[...])
        uv = _causal_conv(xv[...], hv[...] * first, wv[...])
        a_ref[...] = (_silu(ug) * uv).astype(a_ref.dtype)

    wspec = [pl.BlockSpec((3, D_FF), lambda i, c=c: (0, c)) for c in range(2)]
    return _call(body, name="ffn_act", grid=(lp // tm,), out_shape=jax.ShapeDtypeStruct((lp, D_FF), BF16),
                 in_specs=[_row_spec(D_FF, c, tm) for c in range(2)] + [_halo_prev(D_FF, c, tm) for c in range(2)] + wspec,
                 out_specs=_row_spec(D_FF, 0, tm), sem=("parallel",))(up_pre, up_pre, up_pre, up_pre, conv_w, conv_w)


def _ffn_act_bwd(up_pre, conv_w, dact, tm=ROW_TILE):
    lp = up_pre.shape[0]

    def body(xg, xv, hg, hv, wg, wv, da, du_ref, dw_ref):
        i = pl.program_id(0)
        first = (i > 0).astype(F32)
        xs = [(x[...], h[...] * first, w[...]) for x, h, w in ((xg, hg, wg), (xv, hv, wv))]
        ug, uv = [_causal_conv(*t) for t in xs]
        _, vjp = jax.vjp(lambda a, b: _silu(a) * b, ug, uv)
        dus = vjp(da[...].astype(F32))

        @pl.when(i == 0)
        def _():
            dw_ref[...] = jnp.zeros_like(dw_ref)

        for c, (du, (x, h, w)) in enumerate(zip(dus, xs)):
            du_ref[:, c * D_FF:(c + 1) * D_FF] = du
            rows = [jnp.sum(_shift_down(x, h, 2 - kk) * du, axis=0, keepdims=True) for kk in range(3)]
            dw_ref[:, c * D_FF:(c + 1) * D_FF] += jnp.concatenate(rows, axis=0)

    wspec = [pl.BlockSpec((3, D_FF), lambda i, c=c: (0, c)) for c in range(2)]
    return _call(body, name="ffn_act_bwd", grid=(lp // tm,),
                 out_shape=[jax.ShapeDtypeStruct((lp, 2 * D_FF), F32), jax.ShapeDtypeStruct((3, 2 * D_FF), F32)],
                 in_specs=[_row_spec(D_FF, c, tm) for c in range(2)] + [_halo_prev(D_FF, c, tm) for c in range(2)]
                 + wspec + [_row_spec(D_FF, 0, tm)],
                 out_specs=[_row_spec(2 * D_FF, 0, tm), pl.BlockSpec((3, 2 * D_FF), lambda i: (0, 0))],
                 sem=("arbitrary",))(up_pre, up_pre, up_pre, up_pre, conv_w, conv_w, dact)


def _conv_bwd_x(dy, w, pad_rows, width, name):
    lp, ctot = dy.shape
    tm = _pick(lp, ROW_TILES)
    nt = lp // tm
    kk = w.shape[0]

    def body(d_ref, h_ref, w_ref, o_ref):
        i = pl.program_id(0)
        last = (i < nt - 1).astype(F32)
        d, h, wv = d_ref[...], h_ref[...] * last, w_ref[...]
        y = d * wv[kk - 1:kk, :]
        for k in range(kk - 1):
            y = y + _shift_up(d, h, kk - 1 - k) * wv[k:k + 1, :]
        valid = (i * tm + _iota((tm, 1), 0)) >= pad_rows
        o_ref[...] = jnp.where(valid, y, 0.0).astype(o_ref.dtype)

    return _call(body, name=name, grid=(nt, ctot // width), out_shape=jax.ShapeDtypeStruct((lp, ctot), BF16),
                 in_specs=[pl.BlockSpec((tm, width), lambda i, c: (i, c)),
                           pl.BlockSpec((8, width), lambda i, c: (jnp.minimum((i + 1) * (tm // 8), lp // 8 - 1), c)),
                           pl.BlockSpec((kk, width), lambda i, c: (0, c))],
                 out_specs=pl.BlockSpec((tm, width), lambda i, c: (i, c)), sem=("parallel", "parallel"))(dy, dy, w)


def _loss_head(h_res, target, row_start, tm=ROW_TILE):
    lp, d = h_res.shape
    t0 = row_start // tm

    def body(h_ref, t_ref, dy_ref, loss_ref):
        i = pl.program_id(0)

        @pl.when(i == 0)
        def _():
            loss_ref[...] = jnp.zeros_like(loss_ref)

        live = (i >= t0).astype(F32)
        err = (h_ref[...] - t_ref[...]) * live
        dy_ref[...] = err * (1.0 / d)
        loss_ref[...] += 0.5 / d * jnp.sum(err * err)

    return _call(body, name="loss_head", grid=(lp // tm,),
                 out_shape=[jax.ShapeDtypeStruct((lp, d), F32), jax.ShapeDtypeStruct((8, LANE), F32)],
                 in_specs=[pl.BlockSpec((tm, d), lambda i: (i, 0)),
                           pl.BlockSpec((tm, d), lambda i: (jnp.maximum(i - t0, 0), 0))],
                 out_specs=[pl.BlockSpec((tm, d), lambda i: (i, 0)), pl.BlockSpec((8, LANE), lambda i: (0, 0))],
                 sem=("arbitrary",))(h_res, target)


def _sum_adamw(parts, w, m, v, name):
    a, r, c = w.shape
    tm = _pick(r, (256, 128, 64, 32, 16))
    bc1 = 1.0 - ADAM_B1 ** ADAM_STEP
    bc2 = 1.0 - ADAM_B2 ** ADAM_STEP

    def body(p_ref, w_ref, m_ref, v_ref, g_ref, d_ref, nm_ref, nv_ref):
        g = p_ref[0, 0].astype(F32)
        for s in range(1, N_DEV):
            g = g + p_ref[s, 0].astype(F32)
        nm = ADAM_B1 * m_ref[0] + (1.0 - ADAM_B1) * g
        nv = ADAM_B2 * v_ref[0] + (1.0 - ADAM_B2) * (g * g)
        g_ref[0] = g
        nm_ref[0] = nm
        nv_ref[0] = nv
        d_ref[0] = -ADAM_LR * ((nm / bc1) / (jnp.sqrt(nv / bc2) + ADAM_EPS) + ADAM_WD * w_ref[0])

    spec = pl.BlockSpec((1, tm, c), lambda l, i: (l, i, 0))
    shp = jax.ShapeDtypeStruct((a, r, c), F32)
    return _call(body, name=name, grid=(a, r // tm), out_shape=[shp] * 4,
                 in_specs=[pl.BlockSpec((N_DEV, 1, tm, c), lambda l, i: (0, l, i, 0)), spec, spec, spec],
                 out_specs=[spec] * 4, sem=("parallel", "parallel"))(parts, w, m, v)


_ANY = pl.BlockSpec(memory_space=pl.ANY)


def _all_gather(block, name):
    def body(x_ref, out_ref, send_sems, recv_sems, local_sem):
        x, y, c = lax.axis_index("x"), lax.axis_index("y"), lax.axis_index("c")
        me, sibling = (x, y, c), (x, y, 1 - c)
        chips = [(1 - x, y), (x, 1 - y), (1 - x, 1 - y)]

        def slot(px, py, pc):
            return out_ref.at[4 * px + 2 * py + pc]

        def copy(k, blk, to, src=None):
            return pltpu.make_async_remote_copy(
                src_ref=slot(*blk) if src is None else src, dst_ref=slot(*blk),
                send_sem=send_sems.at[k], recv_sem=recv_sems.at[k], device_id=to, device_id_type=MESH)

        mine = pltpu.make_async_copy(x_ref, slot(*me), local_sem)
        mine.start()
        first = [copy(0, me, sibling, src=x_ref)]
        first += [copy(1 + j, me, (*chip, c), src=x_ref) for j, chip in enumerate(chips)]
        for cp in first:
            cp.start()
        passed = [copy(4 + j, (*chip, c), sibling) for j, chip in enumerate(chips)]
        for j, chip in enumerate(chips):
            copy(1 + j, (*chip, c), me).wait_recv()
            passed[j].start()
        copy(0, sibling, me).wait_recv()
        for j, chip in enumerate(chips):
            copy(4 + j, (*chip, 1 - c), me).wait_recv()
        for cp in first + passed:
            cp.wait_send()
        mine.wait()

    return pl.pallas_call(
        body, name=name, out_shape=jax.ShapeDtypeStruct((N_DEV,) + block.shape, block.dtype),
        in_specs=[_ANY], out_specs=_ANY,
        scratch_shapes=[pltpu.SemaphoreType.DMA((7,)), pltpu.SemaphoreType.DMA((7,)), pltpu.SemaphoreType.DMA],
    )(block)


def _all_to_all(src, name):
    def body(s_ref, o_ref, send_sems, recv_sems, local_sem):
        x, y, c = lax.axis_index("x"), lax.axis_index("y"), lax.axis_index("c")
        me = 4 * x + 2 * y + c
        mine = pltpu.make_async_copy(s_ref.at[me], o_ref.at[me], local_sem)
        mine.start()
        copies = []
        for k in range(1, N_DEV):
            px = 1 - x if k & 4 else x
            py = 1 - y if k & 2 else y
            pc = 1 - c if k & 1 else c
            peer = 4 * px + 2 * py + pc
            copies.append((pltpu.make_async_remote_copy(
                src_ref=s_ref.at[peer], dst_ref=o_ref.at[me], send_sem=send_sems.at[k - 1],
                recv_sem=recv_sems.at[k - 1], device_id=(px, py, pc), device_id_type=MESH), peer, k))
        for cp, _, _ in copies:
            cp.start()
        for cp, peer, k in copies:
            cp.wait_send()
            pltpu.make_async_remote_copy(
                src_ref=s_ref.at[peer], dst_ref=o_ref.at[peer], send_sem=send_sems.at[k - 1],
                recv_sem=recv_sems.at[k - 1], device_id=(x, y, c), device_id_type=MESH).wait_recv()
        mine.wait()

    return pl.pallas_call(
        body, name=name, out_shape=jax.ShapeDtypeStruct(src.shape, src.dtype), in_specs=[_ANY], out_specs=_ANY,
        scratch_shapes=[pltpu.SemaphoreType.DMA((7,)), pltpu.SemaphoreType.DMA((7,)), pltpu.SemaphoreType.DMA],
    )(src)


_HBM = pl.BlockSpec(memory_space=pltpu.HBM)
_SEM = pl.BlockSpec(memory_space=pltpu.SEMAPHORE)
_EFFECT = pltpu.SideEffectType.DATAFLOW_SIDE_EFFECTING


def _peer_list(x, y, c):
    return [(1 - x if k & 4 else x, 1 - y if k & 2 else y, 1 - c if k & 1 else c) for k in range(1, N_DEV)]


def _exchange_copies(s_refs, l_refs, send_sems, recv_sems, landing_of_peer, same_block):
    x, y, c = lax.axis_index("x"), lax.axis_index("y"), lax.axis_index("c")
    me = 4 * x + 2 * y + c
    out = []
    for wi, (s_ref, l_ref) in enumerate(zip(s_refs, l_refs)):
        for k, (px, py, pc) in enumerate(_peer_list(x, y, c)):
            peer = 4 * px + 2 * py + pc
            idx = wi * (N_DEV - 1) + k
            out.append(pltpu.make_async_remote_copy(
                src_ref=s_ref if same_block else s_ref.at[peer], dst_ref=l_ref.at[peer if landing_of_peer else me],
                send_sem=send_sems.at[idx], recv_sem=recv_sems.at[idx], device_id=(px, py, pc), device_id_type=MESH))
    return out


def _exchange_start(srcs, name, same_block=False):
    nw = len(srcs)
    ncp = nw * (N_DEV - 1)

    def body(*refs):
        for cp in _exchange_copies(refs[:nw], refs[nw:2 * nw], refs[2 * nw], refs[2 * nw + 1], False, same_block):
            cp.start()
        refs[-1][...] = jnp.zeros_like(refs[-1])

    land_shapes = [((N_DEV,) + s.shape) if same_block else s.shape for s in srcs]
    hbm = [pltpu.HBM(s.shape, s.dtype) for s in srcs]
    hbm_l = [pltpu.HBM(ls, s.dtype) for ls, s in zip(land_shapes, srcs)]
    outs = pl.pallas_call(
        body, name=name,
        out_shape=(pltpu.SemaphoreType.DMA((ncp,)), pltpu.SemaphoreType.DMA((ncp,)), *hbm, *hbm_l,
                   jax.ShapeDtypeStruct((8, LANE), F32)),
        in_specs=[_HBM] * (2 * nw), out_specs=(_SEM, _SEM, *[_HBM] * (2 * nw), pl.BlockSpec(memory_space=pltpu.VMEM)),
        input_output_aliases={i: 2 + i for i in range(2 * nw)},
        compiler_params=pltpu.CompilerParams(has_side_effects=_EFFECT),
    )(*[pltpu.with_memory_space_constraint(s, pltpu.HBM) for s in srcs],
      *[pltpu.with_memory_space_constraint(lax.empty(ls, s.dtype), pltpu.HBM) for ls, s in zip(land_shapes, srcs)])
    return outs[0], outs[1], list(outs[2:2 + nw]), list(outs[2 + nw:2 + 2 * nw]), outs[-1]


def _exchange_wait(send_sems, recv_sems, srcs, lands, after, name, same_block=False):
    nw = len(srcs)

    def body(*refs):
        for cp in _exchange_copies(refs[:nw], refs[nw:2 * nw], refs[2 * nw], refs[2 * nw + 1], True, same_block):
            cp.wait_send()
            cp.wait_recv()

    hbm = [pltpu.HBM(a.shape, a.dtype) for a in list(srcs) + list(lands)]
    outs = pl.pallas_call(
        body, name=name, out_shape=tuple(hbm),
        in_specs=[_HBM] * (2 * nw) + [_SEM, _SEM, pl.BlockSpec(memory_space=pl.ANY)], out_specs=tuple([_HBM] * (2 * nw)),
        input_output_aliases={i: i for i in range(2 * nw)},
        compiler_params=pltpu.CompilerParams(has_side_effects=_EFFECT),
    )(*srcs, *lands, send_sems, recv_sems, after)
    return list(outs[:nw]), list(outs[nw:])


def _pack(blocks, width, dtype, row_mult):
    flat = jnp.concatenate([b.astype(dtype).reshape(-1) for b in blocks])
    per = width * row_mult
    total = -(-flat.shape[0] // per) * per
    return jnp.pad(flat, (0, total - flat.shape[0])).reshape(total // width, width)


def _pack_dest(fulls, axes, width, dtype, row_mult):
    rows = []
    for f, ax in zip(fulls, axes):
        f = f.astype(dtype)
        if ax is None:
            rows.append(jnp.broadcast_to(f.reshape(1, -1), (N_DEV, f.size)))
        else:
            shp = f.shape
            f = f.reshape(shp[:ax] + (N_DEV, shp[ax] // N_DEV) + shp[ax + 1:])
            rows.append(jnp.moveaxis(f, ax, 0).reshape(N_DEV, -1))
    flat = jnp.concatenate(rows, axis=1)
    per = width * row_mult
    total = -(-flat.shape[1] // per) * per
    return jnp.pad(flat, ((0, 0), (0, total - flat.shape[1]))).reshape(N_DEV, total // width, width)


def _unpack(packed, shapes):
    flat = packed.reshape(-1)
    out, off = [], 0
    for s in shapes:
        n = 1
        for d in s:
            n *= d
        out.append(flat[off:off + n].reshape(s))
        off += n
    return out


def _unpack_gathered(gathered, shapes, axes):
    flat = gathered.reshape(N_DEV, -1)
    out, off = [], 0
    for s, ax in zip(shapes, axes):
        n = 1
        for d in s:
            n *= d
        blk = jnp.moveaxis(flat[:, off:off + n].reshape((N_DEV,) + tuple(s)), 0, ax)
        out.append(blk.reshape(tuple(s[:ax]) + (N_DEV * s[ax],) + tuple(s[ax + 1:])))
        off += n
    return out


def _shard_cols(blocks, a, b):
    shard = blocks[0].shape[1]
    out = []
    while a < b:
        d = a // shard
        hi = min(b, (d + 1) * shard)
        out.append(blocks[d][:, a - d * shard:hi - d * shard])
        a = hi
    return out


def _permute_w_in(blocks):
    main = jnp.concatenate(_shard_cols(blocks, O_GQ, O_BL) + _shard_cols(blocks, O_GZ, O_END)
                           + _shard_cols(blocks, O_FQ, O_FL), axis=1)
    pad = jnp.zeros((blocks[0].shape[0], LANE - 24), blocks[0].dtype)
    small = jnp.concatenate(_shard_cols(blocks, O_FL, O_GQ) + _shard_cols(blocks, O_BL, O_GZ) + [pad], axis=1)
    return main, small


_W_IN_SEGS = ((O_FQ, O_FL, True, C_FQ), (O_FL, O_GQ, False, 0), (O_GQ, O_BL, True, C_GQ), (O_BL, O_GZ, False, 8),
              (O_GZ, O_END, True, C_GZ))


def _unpermute_cols(main, small, a, b):
    out = []
    for s0, s1, is_main, t0 in _W_IN_SEGS:
        lo, hi = max(a, s0), min(b, s1)
        if lo < hi:
            out.append((main if is_main else small)[:, t0 + lo - s0:t0 + hi - s0])
    return jnp.concatenate(out, axis=1)


def _lanes(vec, start):
    return jnp.pad(vec.astype(F32), (start, LANE - start - vec.shape[0])).reshape(1, LANE)


BIG = ("w_in", "w_branch_a", "w_branch_b", "w_out", "w_up", "w_down")
BIG_AXES = (2, 2, 1, 1, 2, 1)
SHARDED_SMALL = ("meta_tokens", "gdn_conv_w", "ffn_conv_w")
SHARDED_SMALL_AXES = (1, 2, 2)
REPL = ("norm1_g", "fox_f_bias", "fox_q_norm_g", "fox_k_norm_g", "gdn_a_log", "gdn_dt_bias", "gdn_norm_g", "norm2_g")
ORDER = ("meta_tokens", "norm1_g", "w_in", "fox_f_bias", "fox_q_norm_g", "fox_k_norm_g", "gdn_conv_w", "gdn_a_log",
         "gdn_dt_bias", "gdn_norm_g", "w_branch_a", "w_branch_b", "w_out", "norm2_g", "w_up", "ffn_conv_w", "w_down")


def _layer_fwd(h_res, wl, pad_rows):
    lp = h_res.shape[0]
    sv = {"res_in": h_res}
    (h1,) = _rowwise(_rmsnorm_fn, [(h_res, D, 0)], [wl["norm1_g"]], [(D, BF16)], "rmsnorm1")
    proj = _matmul(h1, wl["w_main"], "nn", F32, name="mm_in")
    small = _matmul(h1, wl["w_small"], "nn", F32, name="mm_in_small")
    sv.update(h1=h1, proj=proj, small=small)

    fox_fn = functools.partial(_fox_prep_fn, pad_rows)
    qh, kh, logf = _rowwise(fox_fn, [(proj, FOX_W, C_FQ // FOX_W), (proj, FOX_W, C_FK // FOX_W), (small, LANE, 0)],
                            [wl["qg"], wl["kg"], wl["fb"]], [(FOX_W, BF16), (FOX_W, BF16), (LANE, F32)], "fox_prep")
    fsum = _cumsum_rows(logf, False, "fox_cumsum")
    frow = fsum[:, :8].T.reshape(8, 1, lp)
    o_a, lse = _fox_fwd(qh, kh, proj, fsum, frow, pad_rows, C_FV // LANE)
    y_a = _matmul(o_a, wl["w_branch_a"], "nn", F32, name="mm_branch_a")
    sv.update(qh=qh, kh=kh, fsum=fsum, frow=frow, o_a=o_a, lse=lse)

    gq, gk, gv, bg = _gdn_act(proj, small, wl["gdn_conv_w"], wl["alog"], wl["dtb"])
    grow = bg[:, 16:24].T.reshape(8, lp // CHUNK, 1, CHUNK)
    o_raw, sprev, tinv = _gdn_fwd(gq, gk, gv, bg, grow)
    (o_b,) = _rowwise(_gdn_post_fn, [(o_raw, GDN_W, 0), (proj, GDN_W, C_GZ // GDN_W)], [wl["gn"]], [(GDN_W, BF16)],
                      "gdn_post")
    y_b = _matmul(o_b, wl["w_branch_b"], "nn", F32, name="mm_branch_b")
    sv.update(gq=gq, gk=gk, gv=gv, bg=bg, grow=grow, o_raw=o_raw, sprev=sprev, tinv=tinv, o_b=o_b)

    (mixed,) = _rowwise(_merge_fn, [(proj, D, C_GATE // D), (proj, D, C_GATE // D + 1), (y_a, D, 0), (y_b, D, 0)], [],
                        [(D, BF16)], "merge")
    res_mid = _matmul(mixed, wl["w_out"], "nn", F32, add=h_res, name="mm_out")
    sv.update(y_a=y_a, y_b=y_b, mixed=mixed, res_mid=res_mid)

    (h2,) = _rowwise(_rmsnorm_fn, [(res_mid, D, 0)], [wl["norm2_g"]], [(D, BF16)], "rmsnorm2")
    up_pre = _matmul(h2, wl["w_up"], "nn", F32, name="mm_up")
    act = _ffn_act(up_pre, wl["ffn_conv_w"])
    out = _matmul(act, wl["w_down"], "nn", F32, add=res_mid, name="mm_down")
    sv.update(h2=h2, up_pre=up_pre, act=act)
    return out, sv


def _layer_bwd(dres, wl, sv, pad_rows):
    lp = dres.shape[0]
    gw = {}
    gw["w_down"] = _matmul(sv["act"], dres, "tn", F32, name="mm_dw_down")
    dact = _matmul(dres, wl["w_down"], "nt", BF16, name="mm_dact")
    dup, gw["ffn_conv_w"] = _ffn_act_bwd(sv["up_pre"], wl["ffn_conv_w"], dact)
    dup_pre = _conv_bwd_x(dup, wl["ffn_conv_w"], pad_rows, D_FF, "ffn_conv_bwd")
    gw["w_up"] = _matmul(sv["h2"], dup_pre, "tn", F32, name="mm_dw_up")
    dh2 = _matmul(dup_pre, wl["w_up"], "nt", F32, name="mm_dh2")
    (dmid,), (gw["norm2_g"],) = _rowwise_bwd(_rmsnorm_fn, [(sv["res_mid"], D, 0)], [wl["norm2_g"]], [(dh2, D, 0)],
                                             "rmsnorm2_bwd", pad_rows, [F32], adds=[(dres, D, 0)])
    gw["w_out"] = _matmul(sv["mixed"], dmid, "tn", F32, name="mm_dw_out")
    dmixed = _matmul(dmid, wl["w_out"], "nt", F32, name="mm_dmixed")
    proj, small = sv["proj"], sv["small"]
    (dg0, dg1, dya, dyb), _ = _rowwise_bwd(
        _merge_fn, [(proj, D, C_GATE // D), (proj, D, C_GATE // D + 1), (sv["y_a"], D, 0), (sv["y_b"], D, 0)], [],
        [(dmixed, D, 0)], "merge_bwd", pad_rows, [BF16, BF16, BF16, BF16])
    gw["w_branch_a"] = _matmul(sv["o_a"], dya, "tn", F32, name="mm_dw_a")
    do_a = (_matmul(dya, wl["w_branch_a"], "nt", F32, name="mm_do_a") * LN2).astype(BF16)
    gw["w_branch_b"] = _matmul(sv["o_b"], dyb, "tn", F32, name="mm_dw_b")
    do_b = _matmul(dyb, wl["w_branch_b"], "nt", F32, name="mm_do_b")

    (do_raw, dgz), (gw["gn"],) = _rowwise_bwd(_gdn_post_fn, [(sv["o_raw"], GDN_W, 0), (proj, GDN_W, C_GZ // GDN_W)],
                                              [wl["gn"]], [(do_b, GDN_W, 0)], "gdn_post_bwd", pad_rows, [F32, BF16])
    dgq, dgk, dgv, dbg, dgrow = _gdn_bwd(sv["gq"], sv["gk"], sv["gv"], sv["bg"], sv["grow"], sv["sprev"],
                                         sv["tinv"], do_raw)
    dbg = dbg + jnp.pad(dgrow.reshape(8, lp).T, ((0, 0), (16, LANE - 24)))
    dconv, dsmall_g, gw["alog"], gw["dtb"], gw["gdn_conv_w"] = _gdn_act_bwd(
        proj, small, wl["gdn_conv_w"], wl["alog"], wl["dtb"], dgq, dgk, dgv, dbg)
    dqkv = _conv_bwd_x(dconv, wl["gdn_conv_w"], pad_rows, GDN_W, "gdn_conv_bwd")

    (delta,) = _rowwise(_fox_delta_fn, [(sv["o_a"], FOX_W, 0), (do_a, FOX_W, 0)], [], [(LANE, F32)], "fox_delta")
    dqh, dkh, dvh, dfq, dfk = _fox_bwd(sv["qh"], sv["kh"], proj, sv["fsum"], sv["frow"], do_a, sv["lse"], delta,
                                       pad_rows, C_FV // LANE)
    df8 = dfq.reshape(lp, FOX_W // LANE, LANE)[:, :, :2].reshape(lp, 8) + dfk.reshape(8, lp).T
    dlogf = _cumsum_rows(jnp.pad(df8, ((0, 0), (0, LANE - 8))), True, "fox_cumsum_bwd")
    fox_fn = functools.partial(_fox_prep_fn, pad_rows)
    (dfq_p, dfk_p, dsmall_f), (gw["qg"], gw["kg"], gw["fb"]) = _rowwise_bwd(
        fox_fn, [(proj, FOX_W, C_FQ // FOX_W), (proj, FOX_W, C_FK // FOX_W), (small, LANE, 0)],
        [wl["qg"], wl["kg"], wl["fb"]], [(dqh, FOX_W, 0), (dkh, FOX_W, 0), (dlogf, LANE, 0)],
        "fox_prep_bwd", pad_rows, [BF16, BF16, F32], adds=[None, None, (dsmall_g, LANE, 0)])

    dproj = jnp.concatenate([dqkv, dgz, dg0, dg1, dfq_p, dfk_p, dvh], axis=1)
    gw["w_main"] = _matmul(sv["h1"], dproj, "tn", F32, name="mm_dw_main")
    gw["w_small"] = _matmul(sv["h1"], dsmall_f, "tn", F32, name="mm_dw_small")
    dh1 = _matmul(dproj, wl["w_main"], "nt", F32, name="mm_dh1")
    dh1 = _matmul(dsmall_f, wl["w_small"], "nt", F32, add=dh1, name="mm_dh1_small")
    (din,), (gw["norm1_g"],) = _rowwise_bwd(_rmsnorm_fn, [(sv["res_in"], D, 0)], [wl["norm1_g"]], [(dh1, D, 0)],
                                            "rmsnorm1_bwd", pad_rows, [F32], adds=[(dmid, D, 0)])
    return din, gw


def kernel(x, meta_tokens, norm1_g, w_in, fox_f_bias, fox_q_norm_g, fox_k_norm_g, gdn_conv_w, gdn_a_log, gdn_dt_bias, gdn_norm_g, w_branch_a, w_branch_b, w_out, norm2_g, w_up, ffn_conv_w, w_down, loss_target, m_meta_tokens, m_norm1_g, m_w_in, m_fox_f_bias, m_fox_q_norm_g, m_fox_k_norm_g, m_gdn_conv_w, m_gdn_a_log, m_gdn_dt_bias, m_gdn_norm_g, m_w_branch_a, m_w_branch_b, m_w_out, m_norm2_g, m_w_up, m_ffn_conv_w, m_w_down, v_meta_tokens, v_norm1_g, v_w_in, v_fox_f_bias, v_fox_q_norm_g, v_fox_k_norm_g, v_gdn_conv_w, v_gdn_a_log, v_gdn_dt_bias, v_gdn_norm_g, v_w_branch_a, v_w_branch_b, v_w_out, v_norm2_g, v_w_up, v_ffn_conv_w, v_w_down):
    w = dict(meta_tokens=meta_tokens, norm1_g=norm1_g, w_in=w_in, fox_f_bias=fox_f_bias, fox_q_norm_g=fox_q_norm_g,
             fox_k_norm_g=fox_k_norm_g, gdn_conv_w=gdn_conv_w, gdn_a_log=gdn_a_log, gdn_dt_bias=gdn_dt_bias,
             gdn_norm_g=gdn_norm_g, w_branch_a=w_branch_a, w_branch_b=w_branch_b, w_out=w_out, norm2_g=norm2_g,
             w_up=w_up, ffn_conv_w=ffn_conv_w, w_down=w_down)
    mom = dict(meta_tokens=m_meta_tokens, norm1_g=m_norm1_g, w_in=m_w_in, fox_f_bias=m_fox_f_bias,
               fox_q_norm_g=m_fox_q_norm_g, fox_k_norm_g=m_fox_k_norm_g, gdn_conv_w=m_gdn_conv_w,
               gdn_a_log=m_gdn_a_log, gdn_dt_bias=m_gdn_dt_bias, gdn_norm_g=m_gdn_norm_g, w_branch_a=m_w_branch_a,
               w_branch_b=m_w_branch_b, w_out=m_w_out, norm2_g=m_norm2_g, w_up=m_w_up, ffn_conv_w=m_ffn_conv_w,
               w_down=m_w_down)
    var = dict(meta_tokens=v_meta_tokens, norm1_g=v_norm1_g, w_in=v_w_in, fox_f_bias=v_fox_f_bias,
               fox_q_norm_g=v_fox_q_norm_g, fox_k_norm_g=v_fox_k_norm_g, gdn_conv_w=v_gdn_conv_w,
               gdn_a_log=v_gdn_a_log, gdn_dt_bias=v_gdn_dt_bias, gdn_norm_g=v_gdn_norm_g, w_branch_a=v_w_branch_a,
               w_branch_b=v_w_branch_b, w_out=v_w_out, norm2_g=v_norm2_g, w_up=v_w_up, ffn_conv_w=v_ffn_conv_w,
               w_down=v_w_down)
    depth = norm1_g.shape[0]
    seq = x.shape[1]
    l_tok = N_META + seq
    lp = -(-l_tok // LANE) * LANE
    pad_rows = lp - l_tok
    row_start = pad_rows + N_META

    me = 4 * lax.axis_index("x") + 2 * lax.axis_index("y") + lax.axis_index("c")
    got = [{n: _all_gather(w[n][0].astype(BF16), "gather_" + n) for n in BIG}]
    later, got[0] = lax.optimization_barrier(([[w[n][l].astype(BF16) for n in BIG] for l in range(1, depth)], got[0]))
    started_g = [_exchange_start(later[l - 1], "gather_start_%d" % l, same_block=True) for l in range(1, depth)]
    small_shapes = [w[n].shape for n in SHARDED_SMALL]
    gathered_s = _all_gather(_pack([w[n] for n in SHARDED_SMALL], LANE, F32, 8), "gather_small")
    full = dict(zip(SHARDED_SMALL, _unpack_gathered(gathered_s, small_shapes, SHARDED_SMALL_AXES)))

    def layer_weights(l, blocks):
        def join(name):
            return jnp.concatenate([blocks[name][d] for d in range(N_DEV)], axis=BIG_AXES[BIG.index(name)] - 1)

        w_main, w_small = _permute_w_in([blocks["w_in"][d] for d in range(N_DEV)])
        return dict(
            w_main=w_main, w_small=w_small, w_branch_a=join("w_branch_a"), w_branch_b=join("w_branch_b"),
            w_out=join("w_out"), w_up=join("w_up"), w_down=join("w_down"),
            gdn_conv_w=full["gdn_conv_w"][l], ffn_conv_w=full["ffn_conv_w"][l],
            norm1_g=norm1_g[l].reshape(1, D), norm2_g=norm2_g[l].reshape(1, D),
            qg=jnp.tile(fox_q_norm_g[l], 8).reshape(1, FOX_W), kg=jnp.tile(fox_k_norm_g[l], 8).reshape(1, FOX_W),
            fb=_lanes(fox_f_bias[l], 0), alog=_lanes(gdn_a_log[l], 16), dtb=_lanes(gdn_dt_bias[l], 16),
            gn=jnp.tile(gdn_norm_g[l], 8).reshape(1, GDN_W))

    def arrived(l, after):
        send_sems, recv_sems, srcs, lands, _ = started_g[l - 1]
        srcs, lands = _exchange_wait(send_sems, recv_sems, srcs, lands, after, "gather_wait_%d" % l, same_block=True)
        return {n: lax.dynamic_update_index_in_dim(ld, sr, me, 0) for n, sr, ld in zip(BIG, srcs, lands)}

    h_res = jnp.concatenate([jnp.zeros((pad_rows, D), F32), full["meta_tokens"], x[0]], axis=0)
    for st in started_g:
        h_res = h_res + st[4][0:1, 0:1]
    layers, saved = [], []
    for l in range(depth):
        layers.append(layer_weights(l, got[0] if l == 0 else arrived(l, h_res)))
        h_res, sv = _layer_fwd(h_res, layers[l], pad_rows)
        saved.append(sv)
    dres, loss_part = _loss_head(h_res, loss_target[0], row_start)
    loss = lax.psum(loss_part[0, 0], ("x", "y", "c"))

    def dest_block(name, g, d):
        if name == "w_in":
            s = w_in.shape[2]
            return _unpermute_cols(g["w_main"], g["w_small"], s * d, s * (d + 1)).astype(BF16)
        if BIG_AXES[BIG.index(name)] == 2:
            s = w[name].shape[2]
            return g[name][:, s * d:s * (d + 1)].astype(BF16)
        s = w[name].shape[1]
        return g[name][s * d:s * (d + 1), :].astype(BF16)

    gws = [None] * depth
    started = [None] * depth
    for l in reversed(range(depth)):
        dres, gws[l] = _layer_bwd(dres, layers[l], saved[l], pad_rows)
        started[l] = _exchange_start([jnp.stack([dest_block(n, gws[l], d) for d in range(N_DEV)]) for n in BIG],
                                     "scatter_start_%d" % l)
        if l > 0:
            token = started[l][4][0, 0].astype(BF16)
            layers[l - 1] = dict(layers[l - 1], w_down=layers[l - 1]["w_down"] + token)
    landed = [None] * depth
    for l in reversed(range(depth)):
        send_sems, recv_sems, srcs, lands, _ = started[l]
        srcs, lands = _exchange_wait(send_sems, recv_sems, srcs, lands, dres, "scatter_wait_%d" % l)
        landed[l] = [lax.dynamic_update_index_in_dim(ld, lax.dynamic_index_in_dim(sr, me, 0, keepdims=False), me, 0)
                     for sr, ld in zip(srcs, lands)]
    grad_x = dres[row_start:].reshape(x.shape)

    def stack(fn):
        return jnp.stack([fn(g) for g in gws])

    part = dict(
        meta_tokens=dres[pad_rows:row_start],
        norm1_g=stack(lambda g: g["norm1_g"][0]), norm2_g=stack(lambda g: g["norm2_g"][0]),
        fox_f_bias=stack(lambda g: g["fb"][0, 0:8]),
        fox_q_norm_g=stack(lambda g: g["qg"].reshape(8, FOX_DH).sum(0)),
        fox_k_norm_g=stack(lambda g: g["kg"].reshape(8, FOX_DH).sum(0)),
        gdn_conv_w=stack(lambda g: g["gdn_conv_w"]), gdn_a_log=stack(lambda g: g["alog"][0, 16:24]),
        gdn_dt_bias=stack(lambda g: g["dtb"][0, 16:24]),
        gdn_norm_g=stack(lambda g: g["gn"].reshape(8, GDN_DH).sum(0)),
        ffn_conv_w=stack(lambda g: g["ffn_conv_w"]))

    res = {}
    for idx, n in enumerate(BIG):
        parts = jnp.stack([landed[l][idx] for l in range(depth)], axis=1)
        res[n] = _sum_adamw(parts, w[n], mom[n], var[n], "adamw_" + n)

    small_names = SHARDED_SMALL + REPL
    small_axes = SHARDED_SMALL_AXES + (None,) * len(REPL)
    landed_s = _all_to_all(_pack_dest([part[n] for n in small_names], small_axes, LANE, F32, 8), "scatter_small")
    shapes_s = [w[n].shape for n in small_names]
    outs = _sum_adamw(landed_s[:, None], *[_pack([d[n] for n in small_names], LANE, F32, 8)[None] for d in (w, mom, var)],
                      "adamw_small")
    for o_idx, packed in enumerate(outs):
        for n, a in zip(small_names, _unpack(packed[0], shapes_s)):
            res.setdefault(n, [None] * 4)[o_idx] = a

    return (loss, grad_x, *[res[n][0] for n in ORDER], *[res[n][1] for n in ORDER],
            *[res[n][2] for n in ORDER], *[res[n][3] for n in ORDER])
```

```python
import functools

import jax
import jax.numpy as jnp
from jax import lax
from jax.experimental import pallas as pl
from jax.experimental.pallas import tpu as pltpu

F32, BF16 = jnp.float32, jnp.bfloat16
MESH = pl.DeviceIdType.MESH

D = 1024
N_META = 16
DEPTH = 4
EPS = 1e-6
LOG2E = 1.4426950408889634
LN2 = 0.6931471805599453
NEG = -1e30
FOX_W, FOX_DH = 512, 64
GDN_W, GDN_DH, GDN_H = 1024, 128, 8
CHUNK = 64
D_FF = 2816
N_DEV = 8
ADAM_LR, ADAM_B1, ADAM_B2, ADAM_EPS, ADAM_WD, ADAM_STEP = 0.001, 0.9, 0.999, 1e-08, 0.01, 10

VMEM_LIMIT_BYTES = 48 * 1024 * 1024
MATMUL_VMEM_BUDGET = 36 * 1024 * 1024
ROW_TILE = 128
ROW_TILES_WIDE = (640, 512, 256, 128)
ROW_TILES = (320, 256, 128)
LANE = 128

C_GQ, C_GK, C_GV, C_GZ, C_GATE, C_FQ, C_FK, C_FV = 0, 1024, 2048, 3072, 4096, 6144, 6656, 7168
W_MAIN = 7680
O_FQ, O_FK, O_FV, O_FL, O_GQ, O_GK, O_GV, O_BL, O_AL, O_GZ, O_GATE, O_END = (
    0, 512, 1024, 1536, 1544, 2568, 3592, 4616, 4624, 4632, 5656, 7704)


def _pick(n, cands):
    for c in cands:
        if n % c == 0:
            return c
    return n


def _call(body, *, name, out_shape, in_specs, out_specs, grid=(), scratch=(), sem=None):
    kw = dict(vmem_limit_bytes=VMEM_LIMIT_BYTES)
    if sem is not None:
        kw["dimension_semantics"] = sem
    return pl.pallas_call(body, name=name, out_shape=out_shape, grid=grid, in_specs=in_specs,
                          out_specs=out_specs, scratch_shapes=list(scratch),
                          compiler_params=pltpu.CompilerParams(**kw))


_DIMS = {"nn": (((1,), (0,)), ((), ())), "nt": (((1,), (1,)), ((), ())), "tn": (((0,), (0,)), ((), ()))}


_DIMS_BATCHED = {"nn": (((2,), (1,)), ((0,), (0,))), "nt": (((2,), (2,)), ((0,), (0,))),
                 "tn": (((1,), (1,)), ((0,), (0,)))}


def _dot(a, b, mode, prec=None):
    dims = _DIMS[mode] if a.ndim == 2 else _DIMS_BATCHED[mode]
    return lax.dot_general(a, b, dims, precision=prec, preferred_element_type=F32)


def _mm_grads(f, mode, a, b, g):
    if mode == "nn":
        return f(g, b, "nt"), f(a, g, "tn")
    if mode == "nt":
        return f(g, b, "nn"), f(g, a, "tn")
    return f(b, g, "nt"), f(a, g, "nn")


@functools.partial(jax.custom_vjp, nondiff_argnums=(2,))
def _mmb(a, b, mode):
    return _dot(a.astype(BF16), b.astype(BF16), mode)


def _mmb_fwd(a, b, mode):
    return _mmb(a, b, mode), (a, b)


def _mmb_bwd(mode, res, g):
    return _mm_grads(_mmb, mode, res[0], res[1], g)


_mmb.defvjp(_mmb_fwd, _mmb_bwd)


def _split(a):
    hi = a.astype(BF16)
    return hi, (a - hi.astype(F32)).astype(BF16)


@functools.partial(jax.custom_vjp, nondiff_argnums=(2,))
def _mmh(a, b, mode):
    ah, al = _split(a)
    bh, bl = _split(b)
    return _dot(ah, bh, mode) + (_dot(ah, bl, mode) + _dot(al, bh, mode))


def _mmh_fwd(a, b, mode):
    return _mmh(a, b, mode), (a, b)


def _mmh_bwd(mode, res, g):
    return _mm_grads(_mmh, mode, res[0], res[1], g)


_mmh.defvjp(_mmh_fwd, _mmh_bwd)


def _dot_sel(sel, x, mode):
    s = sel.astype(BF16)
    x1 = x.astype(BF16)
    x2, x3 = _split(x - x1.astype(F32))
    return _dot(s, x1, mode) + (_dot(s, x2, mode) + _dot(s, x3, mode))


@jax.custom_vjp
def _mms(sel, x):
    return _dot_sel(sel, x, "nn")


def _mms_fwd(sel, x):
    return _dot_sel(sel, x, "nn"), sel


def _mms_bwd(sel, g):
    return jnp.zeros_like(sel), _dot_sel(sel, g, "tn")


_mms.defvjp(_mms_fwd, _mms_bwd)


def _softplus(z):
    return jnp.maximum(z, 0.0) + jnp.log(1.0 + jnp.exp(-jnp.abs(z)))


def _log_sigmoid(z):
    return jnp.minimum(z, 0.0) - jnp.log(1.0 + jnp.exp(-jnp.abs(z)))


def _silu(z):
    return z * jax.nn.sigmoid(z)


def _iota(shape, dim):
    return lax.broadcasted_iota(jnp.int32, shape, dim)


def _inv_unit_lower_raw(n):
    c = n.shape[-1]
    ri, ci = _iota((c, c), 0), _iota((c, c), 1)
    eye = (ri == ci).astype(F32)
    dmask = (ri // 16) == (ci // 16)
    dpart = jnp.where(dmask, n, 0.0)
    lpart = n - dpart
    x = -dpart
    p = eye + x
    for _ in range(3):
        x = _mmh(x, x, "nn")
        p = p + _mmh(p, x, "nn")
    m = -_mmh(p, lpart, "nn")
    q = eye + m
    steps = 1
    while (1 << steps) < c // 16:
        steps += 1
    for _ in range(steps - 1):
        m = _mmh(m, m, "nn")
        q = q + _mmh(q, m, "nn")
    return _mmh(q, p, "nn")


@jax.custom_vjp
def _inv_given(n, t):
    return t


def _inv_given_fwd(n, t):
    return t, t


def _inv_given_bwd(t, g):
    c = t.shape[-1]
    strict = _iota((c, c), 0) > _iota((c, c), 1)
    d = -_mmh(_mmh(t, g, "tn"), t, "nt")
    return jnp.where(strict, d, 0.0), jnp.zeros_like(t)


_inv_given.defvjp(_inv_given_fwd, _inv_given_bwd)


def _shift_down(x, halo, s):
    if s == 0:
        return x
    xs = pltpu.roll(x, s, 0)
    hs = pltpu.roll(halo, s, 0)
    top = jnp.where(_iota(hs.shape, 0) < s, hs, xs[0:8])
    return jnp.concatenate([top, xs[8:]], axis=0)


def _shift_up(x, halo, s):
    if s == 0:
        return x
    tm = x.shape[0]
    xs = pltpu.roll(x, tm - s, 0)
    hs = pltpu.roll(halo, 8 - s, 0)
    bot = jnp.where(_iota(hs.shape, 0) >= 8 - s, hs, xs[tm - 8:])
    return jnp.concatenate([xs[:tm - 8], bot], axis=0)


def _causal_conv(x, halo, w):
    kk = w.shape[0]
    y = x * w[kk - 1:kk, :]
    for k in range(kk - 1):
        y = y + _shift_down(x, halo, kk - 1 - k) * w[k:k + 1, :]
    return y


def _head_scale(x, width, fn):
    outs = []
    for h in range(x.shape[1] // width):
        seg = x[:, h * width:(h + 1) * width]
        outs.append(seg * fn(jnp.sum(seg * seg, axis=1, keepdims=True)))
    return jnp.concatenate(outs, axis=1)


def _matmul(a, b, mode, out_dtype, add=None, name="mm"):
    if mode == "nn":
        (m, k), n = a.shape, b.shape[1]
    elif mode == "nt":
        (m, k), n = a.shape, b.shape[0]
    else:
        (k, m), n = a.shape, b.shape[1]
    tm = _pick(m, (1408, 1024, 512, 256, 128) if mode == "tn" else (640, 512, 256, 128))
    tn = _pick(n, (1536, 1408, 1024, 768, 512, 256, 128))
    sa, sb = a.dtype.itemsize, b.dtype.itemsize
    fixed = tm * tn * 4 * (3 + (2 if add is not None else 0))
    tk = 128
    for cand in (k, 2816, 2560, 1664, 1536, 1280, 1024, 832, 768, 640, 512, 256, 128):
        if mode != "tn" and cand != k and cand % LANE:
            continue
        if k % cand == 0 and fixed + 2 * cand * (tm * sa + tn * sb) <= MATMUL_VMEM_BUDGET:
            tk = cand
            break
    nk = k // tk
    a_spec = {"nn": pl.BlockSpec((tm, tk), lambda i, j, kk: (i, kk)),
              "nt": pl.BlockSpec((tm, tk), lambda i, j, kk: (i, kk)),
              "tn": pl.BlockSpec((tk, tm), lambda i, j, kk: (kk, i))}[mode]
    b_spec = {"nn": pl.BlockSpec((tk, tn), lambda i, j, kk: (kk, j)),
              "nt": pl.BlockSpec((tn, tk), lambda i, j, kk: (j, kk)),
              "tn": pl.BlockSpec((tk, tn), lambda i, j, kk: (kk, j))}[mode]
    o_spec = pl.BlockSpec((tm, tn), lambda i, j, kk: (i, j))
    has_add = add is not None

    def body(*refs):
        a_ref, b_ref = refs[0], refs[1]
        add_ref = refs[2] if has_add else None
        o_ref = refs[3] if has_add else refs[2]
        part = _dot(a_ref[...].astype(BF16), b_ref[...].astype(BF16), mode)
        if nk == 1:
            if has_add:
                part = part + add_ref[...].astype(F32)
            o_ref[...] = part.astype(out_dtype)
        else:
            acc = refs[-1]
            kk = pl.program_id(2)

            @pl.when(kk == 0)
            def _():
                acc[...] = part

            @pl.when(kk > 0)
            def _():
                acc[...] += part

            @pl.when(kk == nk - 1)
            def _():
                r = acc[...]
                if has_add:
                    r = r + add_ref[...].astype(F32)
                o_ref[...] = r.astype(out_dtype)

    ins = [a, b] + ([add] if has_add else [])
    specs = [a_spec, b_spec] + ([o_spec] if has_add else [])
    return _call(body, name=name, out_shape=jax.ShapeDtypeStruct((m, n), out_dtype), grid=(m // tm, n // tn, nk),
                 in_specs=specs, out_specs=o_spec,
                 scratch=[pltpu.VMEM((tm, tn), F32)] if nk > 1 else [],
                 sem=("parallel", "parallel", "arbitrary"))(*ins)


def _row_spec(width, colblock, tm):
    return pl.BlockSpec((tm, width), lambda i, cb=colblock: (i, cb))


def _full_spec(arr):
    nd = arr.ndim
    return pl.BlockSpec(arr.shape, lambda i, nd=nd: (0,) * nd)


def _rowwise(fn, rows, params, outs, name):
    lp = rows[0][0].shape[0]
    tm = _pick(lp, ROW_TILES_WIDE)
    nr, npar = len(rows), len(params)

    def body(*refs):
        row0 = pl.program_id(0) * tm
        vals = [r[...].astype(F32) for r in refs[:nr + npar]]
        res = fn(*vals, row0)
        for o_ref, r in zip(refs[nr + npar:], res):
            o_ref[...] = r.astype(o_ref.dtype)

    out = _call(body, name=name, grid=(lp // tm,),
                out_shape=[jax.ShapeDtypeStruct((lp, w), dt) for w, dt in outs],
                in_specs=[_row_spec(w, cb, tm) for _, w, cb in rows] + [_full_spec(p) for p in params],
                out_specs=[_row_spec(w, 0, tm) for w, _ in outs], sem=("parallel",))(
                    *[r[0] for r in rows], *params)
    return out


def _rowwise_bwd(fn, rows, params, cts, name, pad_rows, grad_dtypes, adds=None):
    lp = rows[0][0].shape[0]
    tm = _pick(lp, ROW_TILES)
    nr, npar, nct = len(rows), len(params), len(cts)
    adds = adds or [None] * nr
    add_list = [a for a in adds if a is not None]
    nadd = len(add_list)

    def body(*refs):
        i = pl.program_id(0)
        row0 = i * tm
        vals = [r[...].astype(F32) for r in refs[:nr + npar]]
        ct_vals = tuple(r[...].astype(F32) for r in refs[nr + npar:nr + npar + nct])
        add_refs = list(refs[nr + npar + nct:nr + npar + nct + nadd])
        outs = refs[nr + npar + nct + nadd:]
        _, vjp = jax.vjp(lambda *args: tuple(fn(*args, row0)), *vals)
        grads = vjp(ct_vals)
        valid = (row0 + _iota((tm, 1), 0)) >= pad_rows
        for idx in range(nr):
            g = jnp.where(valid, grads[idx], 0.0)
            if adds[idx] is not None:
                g = g + add_refs.pop(0)[...].astype(F32)
            outs[idx][...] = g.astype(outs[idx].dtype)
        for idx in range(npar):
            o_ref = outs[nr + idx]

            @pl.when(i == 0)
            def _(o_ref=o_ref):
                o_ref[...] = jnp.zeros_like(o_ref)

            o_ref[...] += grads[nr + idx]

    out = _call(body, name=name, grid=(lp // tm,),
                out_shape=[jax.ShapeDtypeStruct((lp, w), dt) for (_, w, _), dt in zip(rows, grad_dtypes)]
                + [jax.ShapeDtypeStruct(p.shape, F32) for p in params],
                in_specs=[_row_spec(w, cb, tm) for _, w, cb in rows] + [_full_spec(p) for p in params]
                + [_row_spec(w, cb, tm) for _, w, cb in cts] + [_row_spec(w, cb, tm) for _, w, cb in add_list],
                out_specs=[_row_spec(w, 0, tm) for _, w, _ in rows] + [_full_spec(p) for p in params],
                sem=("arbitrary",))(*[r[0] for r in rows], *params, *[c[0] for c in cts], *[a[0] for a in add_list])
    return out[:nr], out[nr:]


def _rmsnorm_fn(x, g, row0):
    return (x * lax.rsqrt(jnp.mean(x * x, axis=1, keepdims=True) + EPS) * g,)


def _fox_prep_fn(pad_rows, fq, fk, small, qg, kg, fb, row0):
    ri, ci = _iota((FOX_W, FOX_W), 0), _iota((FOX_W, FOX_W), 1)
    bd = jnp.where((ri // FOX_DH) == (ci // FOX_DH), 1.0 / FOX_DH, 0.0)

    def hn(x, g):
        return x * lax.rsqrt(_mmh(x * x, bd, "nn") + EPS) * g

    tm = small.shape[0]
    keep = (_iota((tm, LANE), 1) < 8) & ((row0 + _iota((tm, LANE), 0)) >= pad_rows)
    logf = jnp.where(keep, _log_sigmoid(small + fb) * LOG2E, 0.0)
    return hn(fq, qg) * (FOX_DH ** -0.5 * LOG2E), hn(fk, kg), logf


def _gdn_act_fn(cq, ck, cv, small, alog, dtb):
    tm = small.shape[0]
    q = _head_scale(_silu(cq), GDN_DH, lambda s: lax.rsqrt(s + EPS) * (GDN_DH ** -0.5))
    k = _head_scale(_silu(ck), GDN_DH, lambda s: lax.rsqrt(s + EPS))
    v = _silu(cv)
    lane = _iota((tm, LANE), 1)
    beta = jnp.where((lane >= 8) & (lane < 16), jax.nn.sigmoid(small), 0.0)
    g = jnp.where((lane >= 16) & (lane < 24), -jnp.exp(alog) * _softplus(small + dtb), 0.0)
    ri, ci = _iota((tm, tm), 0), _iota((tm, tm), 1)
    tri = jnp.where(((ri // CHUNK) == (ci // CHUNK)) & (ci <= ri), 1.0, 0.0)
    return q, k, v, beta + _mms(tri, g)


def _gdn_post_fn(o, gz, gn, row0):
    return (_head_scale(o, GDN_DH, lambda s: lax.rsqrt(s * (1.0 / GDN_DH) + EPS)) * gn * _silu(gz),)


def _merge_fn(g0, g1, ya, yb, row0):
    return (jax.nn.sigmoid(g0) * ya + jax.nn.sigmoid(g1) * yb,)


def _cumsum_rows(x, reverse, name):
    lp, w = x.shape
    tm = _pick(lp, (640, 512, 256, 128))
    nt = lp // tm

    def body(x_ref, o_ref, carry):
        i = pl.program_id(0)

        @pl.when(i == 0)
        def _():
            carry[...] = jnp.zeros_like(carry)

        ri, ci = _iota((tm, tm), 0), _iota((tm, tm), 1)
        tri = jnp.where((ci >= ri) if reverse else (ci <= ri), 1.0, 0.0)
        blk = x_ref[...]
        o_ref[...] = _dot_sel(tri, blk, "nn") + carry[0:1, :]
        carry[...] = carry[...] + jnp.sum(blk, axis=0, keepdims=True)

    idx = (lambda i: (nt - 1 - i, 0)) if reverse else (lambda i: (i, 0))
    return _call(body, name=name, grid=(nt,), out_shape=jax.ShapeDtypeStruct((lp, w), F32),
                 in_specs=[pl.BlockSpec((tm, w), idx)], out_specs=pl.BlockSpec((tm, w), idx),
                 scratch=[pltpu.VMEM((8, w), F32)], sem=("arbitrary",))(x)


def _fox_scores(q, k, fq, fk, hh, qpos0, kpos0, pad_rows, masked):
    tq, tk = q.shape[0], k.shape[0]
    lane = _iota(q.shape, 1)
    sel = (lane < FOX_DH) if hh == 0 else (lane >= FOX_DH)
    s = _dot(jnp.where(sel, q, jnp.zeros_like(q)), k, "nt") + fq - fk
    if not masked:
        return s, None, sel
    qpos = qpos0 + _iota((tq, tk), 0)
    kpos = kpos0 + _iota((tq, tk), 1)
    mask = (kpos <= qpos) & (kpos >= pad_rows)
    return jnp.where(mask, s, NEG), mask, sel


def _probs(s, mask, shift):
    p = jnp.exp2(s - shift)
    return p if mask is None else jnp.where(mask, p, 0.0)


def _both_variants(needs_mask, fn):
    @pl.when(needs_mask)
    def _():
        fn(True)

    @pl.when(jnp.logical_not(needs_mask))
    def _():
        fn(False)


def _lane_col(blk, lane_idx):
    return jnp.sum(jnp.where(_iota(blk.shape, 1) == lane_idx, blk, 0.0), axis=1, keepdims=True)


def _to_lanes(cols, width=LANE):
    lane = _iota((cols[0].shape[0], width), 1)
    out = jnp.zeros((cols[0].shape[0], width), F32)
    for idx, c in enumerate(cols):
        out = jnp.where(lane == idx, c, out)
    return out


def _fox_fwd(q, k, v, fsum, frow, pad_rows, v_col):
    lp = q.shape[0]
    t = _pick(lp, (640, 512, 256, 128))
    n = lp // t

    def body(q_ref, k_ref, v_ref, f_ref, fk_ref, o_ref, lse_ref, acc, m_s, l_s, fq_s):
        pr, i, j = pl.program_id(0), pl.program_id(1), pl.program_id(2)

        @pl.when(j == 0)
        def _():
            acc[...] = jnp.zeros_like(acc)
            m_s[...] = jnp.full_like(m_s, NEG)
            l_s[...] = jnp.zeros_like(l_s)
            for hh in range(2):
                fq_s[hh] = _lane_col(f_ref[...], 2 * pr + hh)

        def step(masked):
            for hh in range(2):
                s, mask, _ = _fox_scores(q_ref[...], k_ref[...], fq_s[hh], fk_ref[hh], hh, i * t, j * t, pad_rows,
                                         masked)
                m_prev = m_s[hh]
                m_new = jnp.maximum(m_prev, jnp.max(s, axis=1, keepdims=True))
                p = _probs(s, mask, m_new)
                alpha = jnp.exp2(m_prev - m_new)
                l_s[hh] = alpha * l_s[hh] + jnp.sum(p, axis=1, keepdims=True)
                acc[hh] = alpha * acc[hh] + _dot(p.astype(BF16), v_ref[...].astype(BF16), "nn")
                m_s[hh] = m_new

        @pl.when(j <= i)
        def _():
            _both_variants((j == i) | (j == 0), step)

        @pl.when(j == i)
        def _():
            outs, lses = [], []
            for hh in range(2):
                l = l_s[hh]
                ok = l > 0.0
                outs.append(acc[hh] * jnp.where(ok, 1.0 / jnp.where(ok, l, 1.0), 0.0))
                lses.append(jnp.where(ok, m_s[hh] + jnp.log2(jnp.where(ok, l, 1.0)), 0.0))
            lane = _iota((t, LANE), 1)
            o_ref[...] = jnp.where(lane < FOX_DH, outs[0], outs[1]).astype(o_ref.dtype)
            lse_ref[...] = _to_lanes(lses)

    qspec = pl.BlockSpec((t, LANE), lambda p, i, j: (i, p))
    kspec = pl.BlockSpec((t, LANE), lambda p, i, j: (jnp.minimum(j, i), p))
    vspec = pl.BlockSpec((t, LANE), lambda p, i, j: (jnp.minimum(j, i), v_col + p))
    fspec = pl.BlockSpec((t, LANE), lambda p, i, j: (i, 0))
    rspec = pl.BlockSpec((2, 1, t), lambda p, i, j: (p, 0, jnp.minimum(j, i)))
    return _call(body, name="fox_fwd", grid=(FOX_W // LANE, n, n),
                 out_shape=[jax.ShapeDtypeStruct((lp, FOX_W), BF16), jax.ShapeDtypeStruct((lp, FOX_W), F32)],
                 in_specs=[qspec, kspec, vspec, fspec, rspec], out_specs=[qspec, qspec],
                 scratch=[pltpu.VMEM((2, t, LANE), F32), pltpu.VMEM((2, t, 1), F32), pltpu.VMEM((2, t, 1), F32),
                          pltpu.VMEM((2, t, 1), F32)],
                 sem=("parallel", "parallel", "arbitrary"))(q, k, v, fsum, frow)


def _fox_delta_fn(o, do, row0):
    ri, ci = _iota((FOX_W, LANE), 0), _iota((FOX_W, LANE), 1)
    sel = jnp.where((ri // FOX_DH) == ci, 1.0, 0.0).astype(BF16)
    x = o * do
    x1 = x.astype(BF16)
    x2, x3 = _split(x - x1.astype(F32))
    return (_dot(x1, sel, "nn") + (_dot(x2, sel, "nn") + _dot(x3, sel, "nn")),)


def _fox_bwd(q, k, v, fsum, frow, do, lse, delta, pad_rows, v_col):
    lp = q.shape[0]
    t = _pick(lp, (640, 512, 256, 128))
    n = lp // t

    def body(q_ref, k_ref, v_ref, f_ref, fk_ref, do_ref, lse_ref, dl_ref,
             dq_ref, dk_ref, dv_ref, dfq_ref, dfk_ref, dka, dva, dfa):
        pr, j, i = pl.program_id(0), pl.program_id(1), pl.program_id(2)
        lane = _iota((t, LANE), 1)

        @pl.when((j == 0) & (i == 0))
        def _():
            dq_ref[...] = jnp.zeros_like(dq_ref)
            dfq_ref[...] = jnp.zeros_like(dfq_ref)

        @pl.when(i == 0)
        def _():
            dka[...] = jnp.zeros_like(dka)
            dva[...] = jnp.zeros_like(dva)
            dfa[...] = jnp.zeros_like(dfa)

        def step(masked):
            rows = pl.ds(pl.multiple_of(i * t, t), t)
            dq_add = jnp.zeros((t, LANE), F32)
            rowsums = []
            for hh in range(2):
                fq = _lane_col(f_ref[...], 2 * pr + hh)
                s, mask, sel = _fox_scores(q_ref[...], k_ref[...], fq, fk_ref[hh], hh, i * t, j * t, pad_rows, masked)
                p = _probs(s, mask, _lane_col(lse_ref[...], hh))
                dop = jnp.where(sel, do_ref[...], jnp.zeros_like(do_ref[...]))
                ds = p * (_dot(dop, v_ref[...].astype(BF16), "nt") - _lane_col(dl_ref[...], 2 * pr + hh))
                dsb = ds.astype(BF16)
                dva[hh] += _dot(p.astype(BF16), do_ref[...], "tn")
                dka[hh] += _dot(dsb, q_ref[...], "tn")
                dfa[hh] -= jnp.sum(ds, axis=0, keepdims=True)
                dq_add = dq_add + _dot(dsb, jnp.where(sel, k_ref[...], jnp.zeros_like(k_ref[...])), "nn")
                rowsums.append(jnp.sum(ds, axis=1, keepdims=True))
            dq_ref[rows, :] += dq_add
            dfq_ref[rows, :] += _to_lanes(rowsums)

        @pl.when(i >= j)
        def _():
            _both_variants((j == i) | (j == 0), step)

        @pl.when(i == n - 1)
        def _():
            dk_ref[...] = jnp.where(lane < FOX_DH, dka[0], dka[1])
            dv_ref[...] = (jnp.where(lane < FOX_DH, dva[0], dva[1]) * LOG2E).astype(dv_ref.dtype)
            dfk_ref[...] = dfa[...]

    qspec = pl.BlockSpec((t, LANE), lambda p, j, i: (jnp.maximum(i, j), p))
    f_q = pl.BlockSpec((t, LANE), lambda p, j, i: (jnp.maximum(i, j), 0))
    kspec = pl.BlockSpec((t, LANE), lambda p, j, i: (j, p))
    vspec = pl.BlockSpec((t, LANE), lambda p, j, i: (j, v_col + p))
    rspec = pl.BlockSpec((2, 1, t), lambda p, j, i: (p, 0, j))
    whole = pl.BlockSpec((lp, LANE), lambda p, j, i: (0, p))
    wide = jax.ShapeDtypeStruct((lp, FOX_W), F32)
    return _call(body, name="fox_bwd", grid=(FOX_W // LANE, n, n),
                 out_shape=[wide, wide, jax.ShapeDtypeStruct((lp, FOX_W), BF16), wide,
                            jax.ShapeDtypeStruct((8, 1, lp), F32)],
                 in_specs=[qspec, kspec, vspec, f_q, rspec, qspec, qspec, f_q],
                 out_specs=[whole, kspec, kspec, whole, rspec],
                 scratch=[pltpu.VMEM((2, t, LANE), F32), pltpu.VMEM((2, t, LANE), F32), pltpu.VMEM((2, 1, t), F32)],
                 sem=("parallel", "arbitrary", "arbitrary"))(q, k, v, fsum, frow, do, lse, delta)


def _gdn_chunk(q, k, v, beta, gcol, grow, s, inv):
    c = q.shape[-2]
    ri, ci = _iota((c, c), 0), _iota((c, c), 1)
    dec = jnp.exp(jnp.where(ri >= ci, gcol - grow, NEG))
    dec_strict = jnp.where(ri > ci, dec, 0.0)
    eg = jnp.exp(gcol)
    kb = k * beta
    t = inv(_mmb(kb, k, "nt") * dec_strict)
    u_hat = _mmh(t, v * beta, "nn")
    w = _mmh(t, kb * eg, "nn")
    u = u_hat - _mmb(w, s, "nn")
    o = _mmb(q * eg, s, "nn") + _mmb(_mmb(q, k, "nt") * dec, u, "nn")
    glast = jnp.sum(jnp.where(_iota((1, c), 1) == c - 1, grow, 0.0), axis=-1, keepdims=True)
    s_new = s * jnp.exp(glast) + _mmb(k * jnp.exp(glast - gcol), u, "tn")
    return o, s_new


def _gdn_specs(lp, reverse):
    n = lp // CHUNK
    pos = (lambda c: n - 1 - c) if reverse else (lambda c: c)
    wide = pl.BlockSpec((CHUNK, GDN_W), lambda c: (pos(c), 0))
    lanes = pl.BlockSpec((CHUNK, LANE), lambda c: (pos(c), 0))
    row = pl.BlockSpec((GDN_H, 1, 1, CHUNK), lambda c: (0, pos(c), 0, 0))
    st = pl.BlockSpec((GDN_H, 1, GDN_DH, GDN_DH), lambda c: (0, pos(c), 0, 0))
    return n, wide, lanes, row, st


def _heads(ref):
    return jnp.stack([ref[:, h * GDN_DH:(h + 1) * GDN_DH] for h in range(GDN_H)])


def _put_heads(ref, val):
    for h in range(GDN_H):
        ref[:, h * GDN_DH:(h + 1) * GDN_DH] = val[h]


def _head_cols(blk, lane0):
    return jnp.stack([_lane_col(blk, lane0 + h) for h in range(GDN_H)])


def _gdn_fwd(q, k, v, bg, grow):
    lp = q.shape[0]
    n, wide, lanes, row, st = _gdn_specs(lp, False)

    def body(q_ref, k_ref, v_ref, bg_ref, gr_ref, o_ref, sp_ref, t_ref, s_scr):
        @pl.when(pl.program_id(0) == 0)
        def _():
            s_scr[...] = jnp.zeros_like(s_scr)

        def inv(m):
            t = _inv_unit_lower_raw(m)
            t_ref[:, 0] = t
            return t

        s = s_scr[...]
        sp_ref[:, 0] = s
        bg_blk = bg_ref[...]
        o, s_new = _gdn_chunk(_heads(q_ref), _heads(k_ref), _heads(v_ref), _head_cols(bg_blk, 8),
                              _head_cols(bg_blk, 16), gr_ref[:, 0], s, inv)
        _put_heads(o_ref, o)
        s_scr[...] = s_new

    tri = pl.BlockSpec((GDN_H, 1, CHUNK, CHUNK), lambda c: (0, c, 0, 0))
    return _call(body, name="gdn_fwd", grid=(n,),
                 out_shape=[jax.ShapeDtypeStruct((lp, GDN_W), F32),
                            jax.ShapeDtypeStruct((GDN_H, n, GDN_DH, GDN_DH), F32),
                            jax.ShapeDtypeStruct((GDN_H, n, CHUNK, CHUNK), F32)],
                 in_specs=[wide, wide, wide, lanes, row], out_specs=[wide, st, tri],
                 scratch=[pltpu.VMEM((GDN_H, GDN_DH, GDN_DH), F32)], sem=("arbitrary",))(q, k, v, bg, grow)


def _gdn_bwd(q, k, v, bg, grow, sprev, tinv, do):
    lp = q.shape[0]
    n, wide, lanes, row, st = _gdn_specs(lp, True)

    def body(q_ref, k_ref, v_ref, bg_ref, gr_ref, sp_ref, t_ref, do_ref,
             dq_ref, dk_ref, dv_ref, dbg_ref, dgr_ref, ds_scr):
        @pl.when(pl.program_id(0) == 0)
        def _():
            ds_scr[...] = jnp.zeros_like(ds_scr)

        t_saved = t_ref[:, 0]
        fn = functools.partial(_gdn_chunk, inv=lambda m: _inv_given(m, t_saved))
        bg_blk = bg_ref[...]
        _, vjp = jax.vjp(fn, _heads(q_ref), _heads(k_ref), _heads(v_ref), _head_cols(bg_blk, 8),
                         _head_cols(bg_blk, 16), gr_ref[:, 0], sp_ref[:, 0])
        dq, dk, dv, db, dgc, dgr, ds = vjp((_heads(do_ref), ds_scr[...]))
        _put_heads(dq_ref, dq)
        _put_heads(dk_ref, dk)
        _put_heads(dv_ref, dv)
        lane = _iota((CHUNK, LANE), 1)
        dbg = jnp.zeros((CHUNK, LANE), F32)
        for h in range(GDN_H):
            dbg = jnp.where(lane == 8 + h, db[h], jnp.where(lane == 16 + h, dgc[h], dbg))
        dbg_ref[...] = dbg
        dgr_ref[:, 0] = dgr
        ds_scr[...] = ds

    wshape = jax.ShapeDtypeStruct((lp, GDN_W), F32)
    tri = pl.BlockSpec((GDN_H, 1, CHUNK, CHUNK), lambda c: (0, n - 1 - c, 0, 0))
    return _call(body, name="gdn_bwd", grid=(n,),
                 out_shape=[wshape, wshape, wshape, jax.ShapeDtypeStruct((lp, LANE), F32),
                            jax.ShapeDtypeStruct((GDN_H, n, 1, CHUNK), F32)],
                 in_specs=[wide, wide, wide, lanes, row, st, tri, wide], out_specs=[wide, wide, wide, lanes, row],
                 scratch=[pltpu.VMEM((GDN_H, GDN_DH, GDN_DH), F32)], sem=("arbitrary",))(
                     q, k, v, bg, grow, sprev, tinv, do)


def _halo_prev(width, colblock, tm):
    return pl.BlockSpec((8, width), lambda i, cb=colblock: (jnp.maximum(i * (tm // 8) - 1, 0), cb))


def _gdn_act(proj, small, conv_w, alog_row, dtb_row):
    lp = proj.shape[0]
    tm = _pick(lp, ROW_TILES)

    def body(xq, xk, xv, hq, hk, hv, wq, wk, wv, sm, al, dt, q_ref, k_ref, v_ref, bg_ref):
        first = (pl.program_id(0) > 0).astype(F32)
        cs = [_causal_conv(x[...], h[...] * first, w[...]) for x, h, w in ((xq, hq, wq), (xk, hk, wk), (xv, hv, wv))]
        q, k, v, bg = _gdn_act_fn(cs[0], cs[1], cs[2], sm[...], al[...], dt[...])
        q_ref[...], k_ref[...], v_ref[...], bg_ref[...] = q, k, v, bg

    wide = jax.ShapeDtypeStruct((lp, GDN_W), F32)
    wspec = [pl.BlockSpec((4, GDN_W), lambda i, c=c: (0, c)) for c in range(3)]
    return _call(body, name="gdn_act", grid=(lp // tm,),
                 out_shape=[wide, wide, wide, jax.ShapeDtypeStruct((lp, LANE), F32)],
                 in_specs=[_row_spec(GDN_W, c, tm) for c in range(3)] + [_halo_prev(GDN_W, c, tm) for c in range(3)]
                 + wspec + [_row_spec(LANE, 0, tm), _full_spec(alog_row), _full_spec(dtb_row)],
                 out_specs=[_row_spec(GDN_W, 0, tm)] * 3 + [_row_spec(LANE, 0, tm)], sem=("parallel",))(
                     proj, proj, proj, proj, proj, proj, conv_w, conv_w, conv_w, small, alog_row, dtb_row)


def _gdn_act_bwd(proj, small, conv_w, alog_row, dtb_row, dq, dk, dv, dbg):
    lp = proj.shape[0]
    tm = ROW_TILE

    def body(xq, xk, xv, hq, hk, hv, wq, wk, wv, sm, al, dt, dq_r, dk_r, dv_r, dbg_r,
             dc_ref, dsm_ref, dal_ref, ddt_ref, dw_ref):
        i = pl.program_id(0)
        first = (i > 0).astype(F32)
        xs = [(x[...], h[...] * first, w[...]) for x, h, w in ((xq, hq, wq), (xk, hk, wk), (xv, hv, wv))]
        cs = [_causal_conv(*t) for t in xs]
        _, vjp = jax.vjp(_gdn_act_fn, cs[0], cs[1], cs[2], sm[...], al[...], dt[...])
        dcq, dck, dcv, dsm, dal, ddt = vjp((dq_r[...], dk_r[...], dv_r[...], dbg_r[...]))
        dsm_ref[...] = dsm

        @pl.when(i == 0)
        def _():
            dal_ref[...] = jnp.zeros_like(dal_ref)
            ddt_ref[...] = jnp.zeros_like(ddt_ref)
            dw_ref[...] = jnp.zeros_like(dw_ref)

        dal_ref[...] += dal
        ddt_ref[...] += ddt
        for c, (dc, (x, h, w)) in enumerate(zip((dcq, dck, dcv), xs)):
            dc_ref[:, c * GDN_W:(c + 1) * GDN_W] = dc
            rows = [jnp.sum(_shift_down(x, h, 3 - kk) * dc, axis=0, keepdims=True) for kk in range(4)]
            dw_ref[:, c * GDN_W:(c + 1) * GDN_W] += jnp.concatenate(rows, axis=0)

    wspec = [pl.BlockSpec((4, GDN_W), lambda i, c=c: (0, c)) for c in range(3)]
    row128 = jax.ShapeDtypeStruct((1, LANE), F32)
    return _call(body, name="gdn_act_bwd", grid=(lp // tm,),
                 out_shape=[jax.ShapeDtypeStruct((lp, 3 * GDN_W), F32), jax.ShapeDtypeStruct((lp, LANE), F32),
                            row128, row128, jax.ShapeDtypeStruct((4, 3 * GDN_W), F32)],
                 in_specs=[_row_spec(GDN_W, c, tm) for c in range(3)] + [_halo_prev(GDN_W, c, tm) for c in range(3)]
                 + wspec + [_row_spec(LANE, 0, tm), _full_spec(alog_row), _full_spec(dtb_row)]
                 + [_row_spec(GDN_W, 0, tm)] * 3 + [_row_spec(LANE, 0, tm)],
                 out_specs=[_row_spec(3 * GDN_W, 0, tm), _row_spec(LANE, 0, tm),
                            _full_spec(alog_row), _full_spec(dtb_row), pl.BlockSpec((4, 3 * GDN_W), lambda i: (0, 0))],
                 sem=("arbitrary",))(proj, proj, proj, proj, proj, proj, conv_w, conv_w, conv_w, small,
                                     alog_row, dtb_row, dq, dk, dv, dbg)


def _ffn_act(up_pre, conv_w):
    lp = up_pre.shape[0]
    tm = _pick(lp, ROW_TILES)

    def body(xg, xv, hg, hv, wg, wv, a_ref):
        first = (pl.program_id(0) > 0).astype(F32)
        ug = _causal_conv(xg[...], hg[...] * first, wg[...])
        uv = _causal_conv(xv[...], hv[...] * first, wv[...])
        a_ref[...] = (_silu(ug) * uv).astype(a_ref.dtype)

    wspec = [pl.BlockSpec((3, D_FF), lambda i, c=c: (0, c)) for c in range(2)]
    return _call(body, name="ffn_act", grid=(lp // tm,), out_shape=jax.ShapeDtypeStruct((lp, D_FF), BF16),
                 in_specs=[_row_spec(D_FF, c, tm) for c in range(2)] + [_halo_prev(D_FF, c, tm) for c in range(2)] + wspec,
                 out_specs=_row_spec(D_FF, 0, tm), sem=("parallel",))(up_pre, up_pre, up_pre, up_pre, conv_w, conv_w)


def _ffn_act_bwd(up_pre, conv_w, dact, tm=ROW_TILE):
    lp = up_pre.shape[0]

    def body(xg, xv, hg, hv, wg, wv, da, du_ref, dw_ref):
        i = pl.program_id(0)
        first = (i > 0).astype(F32)
        xs = [(x[...], h[...] * first, w[...]) for x, h, w in ((xg, hg, wg), (xv, hv, wv))]
        ug, uv = [_causal_conv(*t) for t in xs]
        _, vjp = jax.vjp(lambda a, b: _silu(a) * b, ug, uv)
        dus = vjp(da[...].astype(F32))

        @pl.when(i == 0)
        def _():
            dw_ref[...] = jnp.zeros_like(dw_ref)

        for c, (du, (x, h, w)) in enumerate(zip(dus, xs)):
            du_ref[:, c * D_FF:(c + 1) * D_FF] = du
            rows = [jnp.sum(_shift_down(x, h, 2 - kk) * du, axis=0, keepdims=True) for kk in range(3)]
            dw_ref[:, c * D_FF:(c + 1) * D_FF] += jnp.concatenate(rows, axis=0)

    wspec = [pl.BlockSpec((3, D_FF), lambda i, c=c: (0, c)) for c in range(2)]
    return _call(body, name="ffn_act_bwd", grid=(lp // tm,),
                 out_shape=[jax.ShapeDtypeStruct((lp, 2 * D_FF), F32), jax.ShapeDtypeStruct((3, 2 * D_FF), F32)],
                 in_specs=[_row_spec(D_FF, c, tm) for c in range(2)] + [_halo_prev(D_FF, c, tm) for c in range(2)]
                 + wspec + [_row_spec(D_FF, 0, tm)],
                 out_specs=[_row_spec(2 * D_FF, 0, tm), pl.BlockSpec((3, 2 * D_FF), lambda i: (0, 0))],
                 sem=("arbitrary",))(up_pre, up_pre, up_pre, up_pre, conv_w, conv_w, dact)


def _conv_bwd_x(dy, w, pad_rows, width, name):
    lp, ctot = dy.shape
    tm = _pick(lp, ROW_TILES)
    nt = lp // tm
    kk = w.shape[0]

    def body(d_ref, h_ref, w_ref, o_ref):
        i = pl.program_id(0)
        last = (i < nt - 1).astype(F32)
        d, h, wv = d_ref[...], h_ref[...] * last, w_ref[...]
        y = d * wv[kk - 1:kk, :]
        for k in range(kk - 1):
            y = y + _shift_up(d, h, kk - 1 - k) * wv[k:k + 1, :]
        valid = (i * tm + _iota((tm, 1), 0)) >= pad_rows
        o_ref[...] = jnp.where(valid, y, 0.0).astype(o_ref.dtype)

    return _call(body, name=name, grid=(nt, ctot // width), out_shape=jax.ShapeDtypeStruct((lp, ctot), BF16),
                 in_specs=[pl.BlockSpec((tm, width), lambda i, c: (i, c)),
                           pl.BlockSpec((8, width), lambda i, c: (jnp.minimum((i + 1) * (tm // 8), lp // 8 - 1), c)),
                           pl.BlockSpec((kk, width), lambda i, c: (0, c))],
                 out_specs=pl.BlockSpec((tm, width), lambda i, c: (i, c)), sem=("parallel", "parallel"))(dy, dy, w)


def _loss_head(h_res, target, row_start, tm=ROW_TILE):
    lp, d = h_res.shape
    t0 = row_start // tm

    def body(h_ref, t_ref, dy_ref, loss_ref):
        i = pl.program_id(0)

        @pl.when(i == 0)
        def _():
            loss_ref[...] = jnp.zeros_like(loss_ref)

        live = (i >= t0).astype(F32)
        err = (h_ref[...] - t_ref[...]) * live
        dy_ref[...] = err * (1.0 / d)
        loss_ref[...] += 0.5 / d * jnp.sum(err * err)

    return _call(body, name="loss_head", grid=(lp // tm,),
                 out_shape=[jax.ShapeDtypeStruct((lp, d), F32), jax.ShapeDtypeStruct((8, LANE), F32)],
                 in_specs=[pl.BlockSpec((tm, d), lambda i: (i, 0)),
                           pl.BlockSpec((tm, d), lambda i: (jnp.maximum(i - t0, 0), 0))],
                 out_specs=[pl.BlockSpec((tm, d), lambda i: (i, 0)), pl.BlockSpec((8, LANE), lambda i: (0, 0))],
                 sem=("arbitrary",))(h_res, target)


def _sum_adamw(parts, w, m, v, name):
    a, r, c = w.shape
    tm = _pick(r, (256, 128, 64, 32, 16))
    bc1 = 1.0 - ADAM_B1 ** ADAM_STEP
    bc2 = 1.0 - ADAM_B2 ** ADAM_STEP

    def body(p_ref, w_ref, m_ref, v_ref, g_ref, d_ref, nm_ref, nv_ref):
        g = p_ref[0, 0].astype(F32)
        for s in range(1, N_DEV):
            g = g + p_ref[s, 0].astype(F32)
        nm = ADAM_B1 * m_ref[0] + (1.0 - ADAM_B1) * g
        nv = ADAM_B2 * v_ref[0] + (1.0 - ADAM_B2) * (g * g)
        g_ref[0] = g
        nm_ref[0] = nm
        nv_ref[0] = nv
        d_ref[0] = -ADAM_LR * ((nm / bc1) / (jnp.sqrt(nv / bc2) + ADAM_EPS) + ADAM_WD * w_ref[0])

    spec = pl.BlockSpec((1, tm, c), lambda l, i: (l, i, 0))
    shp = jax.ShapeDtypeStruct((a, r, c), F32)
    return _call(body, name=name, grid=(a, r // tm), out_shape=[shp] * 4,
                 in_specs=[pl.BlockSpec((N_DEV, 1, tm, c), lambda l, i: (0, l, i, 0)), spec, spec, spec],
                 out_specs=[spec] * 4, sem=("parallel", "parallel"))(parts, w, m, v)


_ANY = pl.BlockSpec(memory_space=pl.ANY)


def _all_gather(block, name):
    def body(x_ref, out_ref, send_sems, recv_sems, local_sem):
        x, y, c = lax.axis_index("x"), lax.axis_index("y"), lax.axis_index("c")
        me, sibling = (x, y, c), (x, y, 1 - c)
        chips = [(1 - x, y), (x, 1 - y), (1 - x, 1 - y)]

        def slot(px, py, pc):
            return out_ref.at[4 * px + 2 * py + pc]

        def copy(k, blk, to, src=None):
            return pltpu.make_async_remote_copy(
                src_ref=slot(*blk) if src is None else src, dst_ref=slot(*blk),
                send_sem=send_sems.at[k], recv_sem=recv_sems.at[k], device_id=to, device_id_type=MESH)

        mine = pltpu.make_async_copy(x_ref, slot(*me), local_sem)
        mine.start()
        first = [copy(0, me, sibling, src=x_ref)]
        first += [copy(1 + j, me, (*chip, c), src=x_ref) for j, chip in enumerate(chips)]
        for cp in first:
            cp.start()
        passed = [copy(4 + j, (*chip, c), sibling) for j, chip in enumerate(chips)]
        for j, chip in enumerate(chips):
            copy(1 + j, (*chip, c), me).wait_recv()
            passed[j].start()
        copy(0, sibling, me).wait_recv()
        for j, chip in enumerate(chips):
            copy(4 + j, (*chip, 1 - c), me).wait_recv()
        for cp in first + passed:
            cp.wait_send()
        mine.wait()

    return pl.pallas_call(
        body, name=name, out_shape=jax.ShapeDtypeStruct((N_DEV,) + block.shape, block.dtype),
        in_specs=[_ANY], out_specs=_ANY,
        scratch_shapes=[pltpu.SemaphoreType.DMA((7,)), pltpu.SemaphoreType.DMA((7,)), pltpu.SemaphoreType.DMA],
    )(block)


def _all_to_all(src, name):
    def body(s_ref, o_ref, send_sems, recv_sems, local_sem):
        x, y, c = lax.axis_index("x"), lax.axis_index("y"), lax.axis_index("c")
        me = 4 * x + 2 * y + c
        mine = pltpu.make_async_copy(s_ref.at[me], o_ref.at[me], local_sem)
        mine.start()
        copies = []
        for k in range(1, N_DEV):
            px = 1 - x if k & 4 else x
            py = 1 - y if k & 2 else y
            pc = 1 - c if k & 1 else c
            peer = 4 * px + 2 * py + pc
            copies.append((pltpu.make_async_remote_copy(
                src_ref=s_ref.at[peer], dst_ref=o_ref.at[me], send_sem=send_sems.at[k - 1],
                recv_sem=recv_sems.at[k - 1], device_id=(px, py, pc), device_id_type=MESH), peer, k))
        for cp, _, _ in copies:
            cp.start()
        for cp, peer, k in copies:
            cp.wait_send()
            pltpu.make_async_remote_copy(
                src_ref=s_ref.at[peer], dst_ref=o_ref.at[peer], send_sem=send_sems.at[k - 1],
                recv_sem=recv_sems.at[k - 1], device_id=(x, y, c), device_id_type=MESH).wait_recv()
        mine.wait()

    return pl.pallas_call(
        body, name=name, out_shape=jax.ShapeDtypeStruct(src.shape, src.dtype), in_specs=[_ANY], out_specs=_ANY,
        scratch_shapes=[pltpu.SemaphoreType.DMA((7,)), pltpu.SemaphoreType.DMA((7,)), pltpu.SemaphoreType.DMA],
    )(src)


_HBM = pl.BlockSpec(memory_space=pltpu.HBM)
_SEM = pl.BlockSpec(memory_space=pltpu.SEMAPHORE)
_EFFECT = pltpu.SideEffectType.DATAFLOW_SIDE_EFFECTING


def _peer_list(x, y, c):
    return [(1 - x if k & 4 else x, 1 - y if k & 2 else y, 1 - c if k & 1 else c) for k in range(1, N_DEV)]


def _exchange_copies(s_refs, l_refs, send_sems, recv_sems, landing_of_peer, same_block):
    x, y, c = lax.axis_index("x"), lax.axis_index("y"), lax.axis_index("c")
    me = 4 * x + 2 * y + c
    out = []
    for wi, (s_ref, l_ref) in enumerate(zip(s_refs, l_refs)):
        for k, (px, py, pc) in enumerate(_peer_list(x, y, c)):
            peer = 4 * px + 2 * py + pc
            idx = wi * (N_DEV - 1) + k
            out.append(pltpu.make_async_remote_copy(
                src_ref=s_ref if same_block else s_ref.at[peer], dst_ref=l_ref.at[peer if landing_of_peer else me],
                send_sem=send_sems.at[idx], recv_sem=recv_sems.at[idx], device_id=(px, py, pc), device_id_type=MESH))
    return out


def _exchange_start(srcs, name, same_block=False):
    nw = len(srcs)
    ncp = nw * (N_DEV - 1)

    def body(*refs):
        for cp in _exchange_copies(refs[:nw], refs[nw:2 * nw], refs[2 * nw], refs[2 * nw + 1], False, same_block):
            cp.start()
        refs[-1][...] = jnp.zeros_like(refs[-1])

    land_shapes = [((N_DEV,) + s.shape) if same_block else s.shape for s in srcs]
    hbm = [pltpu.HBM(s.shape, s.dtype) for s in srcs]
    hbm_l = [pltpu.HBM(ls, s.dtype) for ls, s in zip(land_shapes, srcs)]
    outs = pl.pallas_call(
        body, name=name,
        out_shape=(pltpu.SemaphoreType.DMA((ncp,)), pltpu.SemaphoreType.DMA((ncp,)), *hbm, *hbm_l,
                   jax.ShapeDtypeStruct((8, LANE), F32)),
        in_specs=[_HBM] * (2 * nw), out_specs=(_SEM, _SEM, *[_HBM] * (2 * nw), pl.BlockSpec(memory_space=pltpu.VMEM)),
        input_output_aliases={i: 2 + i for i in range(2 * nw)},
        compiler_params=pltpu.CompilerParams(has_side_effects=_EFFECT),
    )(*[pltpu.with_memory_space_constraint(s, pltpu.HBM) for s in srcs],
      *[pltpu.with_memory_space_constraint(lax.empty(ls, s.dtype), pltpu.HBM) for ls, s in zip(land_shapes, srcs)])
    return outs[0], outs[1], list(outs[2:2 + nw]), list(outs[2 + nw:2 + 2 * nw]), outs[-1]


def _exchange_wait(send_sems, recv_sems, srcs, lands, after, name, same_block=False):
    nw = len(srcs)

    def body(*refs):
        for cp in _exchange_copies(refs[:nw], refs[nw:2 * nw], refs[2 * nw], refs[2 * nw + 1], True, same_block):
            cp.wait_send()
            cp.wait_recv()

    hbm = [pltpu.HBM(a.shape, a.dtype) for a in list(srcs) + list(lands)]
    outs = pl.pallas_call(
        body, name=name, out_shape=tuple(hbm),
        in_specs=[_HBM] * (2 * nw) + [_SEM, _SEM, pl.BlockSpec(memory_space=pl.ANY)], out_specs=tuple([_HBM] * (2 * nw)),
        input_output_aliases={i: i for i in range(2 * nw)},
        compiler_params=pltpu.CompilerParams(has_side_effects=_EFFECT),
    )(*srcs, *lands, send_sems, recv_sems, after)
    return list(outs[:nw]), list(outs[nw:])


def _pack(blocks, width, dtype, row_mult):
    flat = jnp.concatenate([b.astype(dtype).reshape(-1) for b in blocks])
    per = width * row_mult
    total = -(-flat.shape[0] // per) * per
    return jnp.pad(flat, (0, total - flat.shape[0])).reshape(total // width, width)


def _pack_dest(fulls, axes, width, dtype, row_mult):
    rows = []
    for f, ax in zip(fulls, axes):
        f = f.astype(dtype)
        if ax is None:
            rows.append(jnp.broadcast_to(f.reshape(1, -1), (N_DEV, f.size)))
        else:
            shp = f.shape
            f = f.reshape(shp[:ax] + (N_DEV, shp[ax] // N_DEV) + shp[ax + 1:])
            rows.append(jnp.moveaxis(f, ax, 0).reshape(N_DEV, -1))
    flat = jnp.concatenate(rows, axis=1)
    per = width * row_mult
    total = -(-flat.shape[1] // per) * per
    return jnp.pad(flat, ((0, 0), (0, total - flat.shape[1]))).reshape(N_DEV, total // width, width)


def _unpack(packed, shapes):
    flat = packed.reshape(-1)
    out, off = [], 0
    for s in shapes:
        n = 1
        for d in s:
            n *= d
        out.append(flat[off:off + n].reshape(s))
        off += n
    return out


def _unpack_gathered(gathered, shapes, axes):
    flat = gathered.reshape(N_DEV, -1)
    out, off = [], 0
    for s, ax in zip(shapes, axes):
        n = 1
        for d in s:
            n *= d
        blk = jnp.moveaxis(flat[:, off:off + n].reshape((N_DEV,) + tuple(s)), 0, ax)
        out.append(blk.reshape(tuple(s[:ax]) + (N_DEV * s[ax],) + tuple(s[ax + 1:])))
        off += n
    return out


def _shard_cols(blocks, a, b):
    shard = blocks[0].shape[1]
    out = []
    while a < b:
        d = a // shard
        hi = min(b, (d + 1) * shard)
        out.append(blocks[d][:, a - d * shard:hi - d * shard])
        a = hi
    return out


def _permute_w_in(blocks):
    main = jnp.concatenate(_shard_cols(blocks, O_GQ, O_BL) + _shard_cols(blocks, O_GZ, O_END)
                           + _shard_cols(blocks, O_FQ, O_FL), axis=1)
    pad = jnp.zeros((blocks[0].shape[0], LANE - 24), blocks[0].dtype)
    small = jnp.concatenate(_shard_cols(blocks, O_FL, O_GQ) + _shard_cols(blocks, O_BL, O_GZ) + [pad], axis=1)
    return main, small


_W_IN_SEGS = ((O_FQ, O_FL, True, C_FQ), (O_FL, O_GQ, False, 0), (O_GQ, O_BL, True, C_GQ), (O_BL, O_GZ, False, 8),
              (O_GZ, O_END, True, C_GZ))


def _unpermute_cols(main, small, a, b):
    out = []
    for s0, s1, is_main, t0 in _W_IN_SEGS:
        lo, hi = max(a, s0), min(b, s1)
        if lo < hi:
            out.append((main if is_main else small)[:, t0 + lo - s0:t0 + hi - s0])
    return jnp.concatenate(out, axis=1)


def _lanes(vec, start):
    return jnp.pad(vec.astype(F32), (start, LANE - start - vec.shape[0])).reshape(1, LANE)


BIG = ("w_in", "w_branch_a", "w_branch_b", "w_out", "w_up", "w_down")
BIG_AXES = (2, 2, 1, 1, 2, 1)
SHARDED_SMALL = ("meta_tokens", "gdn_conv_w", "ffn_conv_w")
SHARDED_SMALL_AXES = (1, 2, 2)
REPL = ("norm1_g", "fox_f_bias", "fox_q_norm_g", "fox_k_norm_g", "gdn_a_log", "gdn_dt_bias", "gdn_norm_g", "norm2_g")
ORDER = ("meta_tokens", "norm1_g", "w_in", "fox_f_bias", "fox_q_norm_g", "fox_k_norm_g", "gdn_conv_w", "gdn_a_log",
         "gdn_dt_bias", "gdn_norm_g", "w_branch_a", "w_branch_b", "w_out", "norm2_g", "w_up", "ffn_conv_w", "w_down")


def _layer_fwd(h_res, wl, pad_rows):
    lp = h_res.shape[0]
    sv = {"res_in": h_res}
    (h1,) = _rowwise(_rmsnorm_fn, [(h_res, D, 0)], [wl["norm1_g"]], [(D, BF16)], "rmsnorm1")
    proj = _matmul(h1, wl["w_main"], "nn", F32, name="mm_in")
    small = _matmul(h1, wl["w_small"], "nn", F32, name="mm_in_small")
    sv.update(h1=h1, proj=proj, small=small)

    fox_fn = functools.partial(_fox_prep_fn, pad_rows)
    qh, kh, logf = _rowwise(fox_fn, [(proj, FOX_W, C_FQ // FOX_W), (proj, FOX_W, C_FK // FOX_W), (small, LANE, 0)],
                            [wl["qg"], wl["kg"], wl["fb"]], [(FOX_W, BF16), (FOX_W, BF16), (LANE, F32)], "fox_prep")
    fsum = _cumsum_rows(logf, False, "fox_cumsum")
    frow = fsum[:, :8].T.reshape(8, 1, lp)
    o_a, lse = _fox_fwd(qh, kh, proj, fsum, frow, pad_rows, C_FV // LANE)
    y_a = _matmul(o_a, wl["w_branch_a"], "nn", F32, name="mm_branch_a")
    sv.update(qh=qh, kh=kh, fsum=fsum, frow=frow, o_a=o_a, lse=lse)

    gq, gk, gv, bg = _gdn_act(proj, small, wl["gdn_conv_w"], wl["alog"], wl["dtb"])
    grow = bg[:, 16:24].T.reshape(8, lp // CHUNK, 1, CHUNK)
    o_raw, sprev, tinv = _gdn_fwd(gq, gk, gv, bg, grow)
    (o_b,) = _rowwise(_gdn_post_fn, [(o_raw, GDN_W, 0), (proj, GDN_W, C_GZ // GDN_W)], [wl["gn"]], [(GDN_W, BF16)],
                      "gdn_post")
    y_b = _matmul(o_b, wl["w_branch_b"], "nn", F32, name="mm_branch_b")
    sv.update(gq=gq, gk=gk, gv=gv, bg=bg, grow=grow, o_raw=o_raw, sprev=sprev, tinv=tinv, o_b=o_b)

    (mixed,) = _rowwise(_merge_fn, [(proj, D, C_GATE // D), (proj, D, C_GATE // D + 1), (y_a, D, 0), (y_b, D, 0)], [],
                        [(D, BF16)], "merge")
    res_mid = _matmul(mixed, wl["w_out"], "nn", F32, add=h_res, name="mm_out")
    sv.update(y_a=y_a, y_b=y_b, mixed=mixed, res_mid=res_mid)

    (h2,) = _rowwise(_rmsnorm_fn, [(res_mid, D, 0)], [wl["norm2_g"]], [(D, BF16)], "rmsnorm2")
    up_pre = _matmul(h2, wl["w_up"], "nn", F32, name="mm_up")
    act = _ffn_act(up_pre, wl["ffn_conv_w"])
    out = _matmul(act, wl["w_down"], "nn", F32, add=res_mid, name="mm_down")
    sv.update(h2=h2, up_pre=up_pre, act=act)
    return out, sv


def _layer_bwd(dres, wl, sv, pad_rows, after_ffn=None):
    lp = dres.shape[0]
    gw = {}
    gw["w_down"] = _matmul(sv["act"], dres, "tn", F32, name="mm_dw_down")
    dact = _matmul(dres, wl["w_down"], "nt", BF16, name="mm_dact")
    dup, gw["ffn_conv_w"] = _ffn_act_bwd(sv["up_pre"], wl["ffn_conv_w"], dact)
    dup_pre = _conv_bwd_x(dup, wl["ffn_conv_w"], pad_rows, D_FF, "ffn_conv_bwd")
    gw["w_up"] = _matmul(sv["h2"], dup_pre, "tn", F32, name="mm_dw_up")
    dh2 = _matmul(dup_pre, wl["w_up"], "nt", F32, name="mm_dh2")
    (dmid,), (gw["norm2_g"],) = _rowwise_bwd(_rmsnorm_fn, [(sv["res_mid"], D, 0)], [wl["norm2_g"]], [(dh2, D, 0)],
                                             "rmsnorm2_bwd", pad_rows, [F32], adds=[(dres, D, 0)])
    if after_ffn is not None:
        wl = dict(wl, w_out=wl["w_out"] + after_ffn(gw))
    gw["w_out"] = _matmul(sv["mixed"], dmid, "tn", F32, name="mm_dw_out")
    dmixed = _matmul(dmid, wl["w_out"], "nt", F32, name="mm_dmixed")
    proj, small = sv["proj"], sv["small"]
    (dg0, dg1, dya, dyb), _ = _rowwise_bwd(
        _merge_fn, [(proj, D, C_GATE // D), (proj, D, C_GATE // D + 1), (sv["y_a"], D, 0), (sv["y_b"], D, 0)], [],
        [(dmixed, D, 0)], "merge_bwd", pad_rows, [BF16, BF16, BF16, BF16])
    gw["w_branch_a"] = _matmul(sv["o_a"], dya, "tn", F32, name="mm_dw_a")
    do_a = (_matmul(dya, wl["w_branch_a"], "nt", F32, name="mm_do_a") * LN2).astype(BF16)
    gw["w_branch_b"] = _matmul(sv["o_b"], dyb, "tn", F32, name="mm_dw_b")
    do_b = _matmul(dyb, wl["w_branch_b"], "nt", F32, name="mm_do_b")

    (do_raw, dgz), (gw["gn"],) = _rowwise_bwd(_gdn_post_fn, [(sv["o_raw"], GDN_W, 0), (proj, GDN_W, C_GZ // GDN_W)],
                                              [wl["gn"]], [(do_b, GDN_W, 0)], "gdn_post_bwd", pad_rows, [F32, BF16])
    dgq, dgk, dgv, dbg, dgrow = _gdn_bwd(sv["gq"], sv["gk"], sv["gv"], sv["bg"], sv["grow"], sv["sprev"],
                                         sv["tinv"], do_raw)
    dbg = dbg + jnp.pad(dgrow.reshape(8, lp).T, ((0, 0), (16, LANE - 24)))
    dconv, dsmall_g, gw["alog"], gw["dtb"], gw["gdn_conv_w"] = _gdn_act_bwd(
        proj, small, wl["gdn_conv_w"], wl["alog"], wl["dtb"], dgq, dgk, dgv, dbg)
    dqkv = _conv_bwd_x(dconv, wl["gdn_conv_w"], pad_rows, GDN_W, "gdn_conv_bwd")

    (delta,) = _rowwise(_fox_delta_fn, [(sv["o_a"], FOX_W, 0), (do_a, FOX_W, 0)], [], [(LANE, F32)], "fox_delta")
    dqh, dkh, dvh, dfq, dfk = _fox_bwd(sv["qh"], sv["kh"], proj, sv["fsum"], sv["frow"], do_a, sv["lse"], delta,
                                       pad_rows, C_FV // LANE)
    df8 = dfq.reshape(lp, FOX_W // LANE, LANE)[:, :, :2].reshape(lp, 8) + dfk.reshape(8, lp).T
    dlogf = _cumsum_rows(jnp.pad(df8, ((0, 0), (0, LANE - 8))), True, "fox_cumsum_bwd")
    fox_fn = functools.partial(_fox_prep_fn, pad_rows)
    (dfq_p, dfk_p, dsmall_f), (gw["qg"], gw["kg"], gw["fb"]) = _rowwise_bwd(
        fox_fn, [(proj, FOX_W, C_FQ // FOX_W), (proj, FOX_W, C_FK // FOX_W), (small, LANE, 0)],
        [wl["qg"], wl["kg"], wl["fb"]], [(dqh, FOX_W, 0), (dkh, FOX_W, 0), (dlogf, LANE, 0)],
        "fox_prep_bwd", pad_rows, [BF16, BF16, F32], adds=[None, None, (dsmall_g, LANE, 0)])

    dproj = jnp.concatenate([dqkv, dgz, dg0, dg1, dfq_p, dfk_p, dvh], axis=1)
    gw["w_main"] = _matmul(sv["h1"], dproj, "tn", F32, name="mm_dw_main")
    gw["w_small"] = _matmul(sv["h1"], dsmall_f, "tn", F32, name="mm_dw_small")
    dh1 = _matmul(dproj, wl["w_main"], "nt", F32, name="mm_dh1")
    dh1 = _matmul(dsmall_f, wl["w_small"], "nt", F32, add=dh1, name="mm_dh1_small")
    (din,), (gw["norm1_g"],) = _rowwise_bwd(_rmsnorm_fn, [(sv["res_in"], D, 0)], [wl["norm1_g"]], [(dh1, D, 0)],
                                            "rmsnorm1_bwd", pad_rows, [F32], adds=[(dmid, D, 0)])
    return din, gw


def kernel(x, meta_tokens, norm1_g, w_in, fox_f_bias, fox_q_norm_g, fox_k_norm_g, gdn_conv_w, gdn_a_log, gdn_dt_bias, gdn_norm_g, w_branch_a, w_branch_b, w_out, norm2_g, w_up, ffn_conv_w, w_down, loss_target, m_meta_tokens, m_norm1_g, m_w_in, m_fox_f_bias, m_fox_q_norm_g, m_fox_k_norm_g, m_gdn_conv_w, m_gdn_a_log, m_gdn_dt_bias, m_gdn_norm_g, m_w_branch_a, m_w_branch_b, m_w_out, m_norm2_g, m_w_up, m_ffn_conv_w, m_w_down, v_meta_tokens, v_norm1_g, v_w_in, v_fox_f_bias, v_fox_q_norm_g, v_fox_k_norm_g, v_gdn_conv_w, v_gdn_a_log, v_gdn_dt_bias, v_gdn_norm_g, v_w_branch_a, v_w_branch_b, v_w_out, v_norm2_g, v_w_up, v_ffn_conv_w, v_w_down):
    w = dict(meta_tokens=meta_tokens, norm1_g=norm1_g, w_in=w_in, fox_f_bias=fox_f_bias, fox_q_norm_g=fox_q_norm_g,
             fox_k_norm_g=fox_k_norm_g, gdn_conv_w=gdn_conv_w, gdn_a_log=gdn_a_log, gdn_dt_bias=gdn_dt_bias,
             gdn_norm_g=gdn_norm_g, w_branch_a=w_branch_a, w_branch_b=w_branch_b, w_out=w_out, norm2_g=norm2_g,
             w_up=w_up, ffn_conv_w=ffn_conv_w, w_down=w_down)
    mom = dict(meta_tokens=m_meta_tokens, norm1_g=m_norm1_g, w_in=m_w_in, fox_f_bias=m_fox_f_bias,
               fox_q_norm_g=m_fox_q_norm_g, fox_k_norm_g=m_fox_k_norm_g, gdn_conv_w=m_gdn_conv_w,
               gdn_a_log=m_gdn_a_log, gdn_dt_bias=m_gdn_dt_bias, gdn_norm_g=m_gdn_norm_g, w_branch_a=m_w_branch_a,
               w_branch_b=m_w_branch_b, w_out=m_w_out, norm2_g=m_norm2_g, w_up=m_w_up, ffn_conv_w=m_ffn_conv_w,
               w_down=m_w_down)
    var = dict(meta_tokens=v_meta_tokens, norm1_g=v_norm1_g, w_in=v_w_in, fox_f_bias=v_fox_f_bias,
               fox_q_norm_g=v_fox_q_norm_g, fox_k_norm_g=v_fox_k_norm_g, gdn_conv_w=v_gdn_conv_w,
               gdn_a_log=v_gdn_a_log, gdn_dt_bias=v_gdn_dt_bias, gdn_norm_g=v_gdn_norm_g, w_branch_a=v_w_branch_a,
               w_branch_b=v_w_branch_b, w_out=v_w_out, norm2_g=v_norm2_g, w_up=v_w_up, ffn_conv_w=v_ffn_conv_w,
               w_down=v_w_down)
    depth = norm1_g.shape[0]
    seq = x.shape[1]
    l_tok = N_META + seq
    lp = -(-l_tok // LANE) * LANE
    pad_rows = lp - l_tok
    row_start = pad_rows + N_META

    me = 4 * lax.axis_index("x") + 2 * lax.axis_index("y") + lax.axis_index("c")
    got = [{n: _all_gather(w[n][0].astype(BF16), "gather_" + n) for n in BIG}]
    later, got[0] = lax.optimization_barrier(([[w[n][l].astype(BF16) for n in BIG] for l in range(1, depth)], got[0]))
    started_g = [_exchange_start(later[l - 1], "gather_start_%d" % l, same_block=True) for l in range(1, depth)]
    small_shapes = [w[n].shape for n in SHARDED_SMALL]
    gathered_s = _all_gather(_pack([w[n] for n in SHARDED_SMALL], LANE, F32, 8), "gather_small")
    full = dict(zip(SHARDED_SMALL, _unpack_gathered(gathered_s, small_shapes, SHARDED_SMALL_AXES)))

    def layer_weights(l, blocks):
        def join(name):
            return jnp.concatenate([blocks[name][d] for d in range(N_DEV)], axis=BIG_AXES[BIG.index(name)] - 1)

        w_main, w_small = _permute_w_in([blocks["w_in"][d] for d in range(N_DEV)])
        return dict(
            w_main=w_main, w_small=w_small, w_branch_a=join("w_branch_a"), w_branch_b=join("w_branch_b"),
            w_out=join("w_out"), w_up=join("w_up"), w_down=join("w_down"),
            gdn_conv_w=full["gdn_conv_w"][l], ffn_conv_w=full["ffn_conv_w"][l],
            norm1_g=norm1_g[l].reshape(1, D), norm2_g=norm2_g[l].reshape(1, D),
            qg=jnp.tile(fox_q_norm_g[l], 8).reshape(1, FOX_W), kg=jnp.tile(fox_k_norm_g[l], 8).reshape(1, FOX_W),
            fb=_lanes(fox_f_bias[l], 0), alog=_lanes(gdn_a_log[l], 16), dtb=_lanes(gdn_dt_bias[l], 16),
            gn=jnp.tile(gdn_norm_g[l], 8).reshape(1, GDN_W))

    def arrived(l, after):
        send_sems, recv_sems, srcs, lands, _ = started_g[l - 1]
        srcs, lands = _exchange_wait(send_sems, recv_sems, srcs, lands, after, "gather_wait_%d" % l, same_block=True)
        return {n: lax.dynamic_update_index_in_dim(ld, sr, me, 0) for n, sr, ld in zip(BIG, srcs, lands)}

    h_res = jnp.concatenate([jnp.zeros((pad_rows, D), F32), full["meta_tokens"], x[0]], axis=0)
    for st in started_g:
        h_res = h_res + st[4][0:1, 0:1]
    layers, saved = [], []
    for l in range(depth):
        layers.append(layer_weights(l, got[0] if l == 0 else arrived(l, h_res)))
        h_res, sv = _layer_fwd(h_res, layers[l], pad_rows)
        saved.append(sv)
    dres, loss_part = _loss_head(h_res, loss_target[0], row_start)
    loss = lax.psum(loss_part[0, 0], ("x", "y", "c"))

    def dest_block(name, g, d):
        if name == "w_in":
            s = w_in.shape[2]
            return _unpermute_cols(g["w_main"], g["w_small"], s * d, s * (d + 1)).astype(BF16)
        if BIG_AXES[BIG.index(name)] == 2:
            s = w[name].shape[2]
            return g[name][:, s * d:s * (d + 1)].astype(BF16)
        s = w[name].shape[1]
        return g[name][s * d:s * (d + 1), :].astype(BF16)

    def start(names, g, name):
        return names, _exchange_start([jnp.stack([dest_block(n, g, d) for d in range(N_DEV)]) for n in names], name)

    ffn_names = ("w_up", "w_down")
    gws = [None] * depth
    started = []

    def early(g):
        started.append((0,) + start(ffn_names, g, "scatter_start_0_ffn"))
        return started[-1][2][4][0, 0].astype(BF16)

    for l in reversed(range(depth)):
        dres, gws[l] = _layer_bwd(dres, layers[l], saved[l], pad_rows, early if l == 0 else None)
        started.append((l,) + start(tuple(n for n in BIG if l > 0 or n not in ffn_names), gws[l],
                                    "scatter_start_%d" % l))
        if l > 0:
            token = started[-1][2][4][0, 0].astype(BF16)
            layers[l - 1] = dict(layers[l - 1], w_down=layers[l - 1]["w_down"] + token)
    landed = [dict() for _ in range(depth)]
    for idx, (l, names, (send_sems, recv_sems, srcs, lands, _)) in enumerate(started):
        srcs, lands = _exchange_wait(send_sems, recv_sems, srcs, lands, dres, "scatter_wait_%d" % idx)
        for n, sr, ld in zip(names, srcs, lands):
            own = lax.dynamic_index_in_dim(sr, me, 0, keepdims=False)
            landed[l][n] = lax.dynamic_update_index_in_dim(ld, own, me, 0)
    grad_x = dres[row_start:].reshape(x.shape)

    def stack(fn):
        return jnp.stack([fn(g) for g in gws])

    part = dict(
        meta_tokens=dres[pad_rows:row_start],
        norm1_g=stack(lambda g: g["norm1_g"][0]), norm2_g=stack(lambda g: g["norm2_g"][0]),
        fox_f_bias=stack(lambda g: g["fb"][0, 0:8]),
        fox_q_norm_g=stack(lambda g: g["qg"].reshape(8, FOX_DH).sum(0)),
        fox_k_norm_g=stack(lambda g: g["kg"].reshape(8, FOX_DH).sum(0)),
        gdn_conv_w=stack(lambda g: g["gdn_conv_w"]), gdn_a_log=stack(lambda g: g["alog"][0, 16:24]),
        gdn_dt_bias=stack(lambda g: g["dtb"][0, 16:24]),
        gdn_norm_g=stack(lambda g: g["gn"].reshape(8, GDN_DH).sum(0)),
        ffn_conv_w=stack(lambda g: g["ffn_conv_w"]))

    res = {}
    for n in BIG:
        parts = jnp.stack([landed[l][n] for l in range(depth)], axis=1)
        res[n] = _sum_adamw(parts, w[n], mom[n], var[n], "adamw_" + n)

    small_names = SHARDED_SMALL + REPL
    small_axes = SHARDED_SMALL_AXES + (None,) * len(REPL)
    landed_s = _all_to_all(_pack_dest([part[n] for n in small_names], small_axes, LANE, F32, 8), "scatter_small")
    shapes_s = [w[n].shape for n in small_names]
    outs = _sum_adamw(landed_s[:, None], *[_pack([d[n] for n in small_names], LANE, F32, 8)[None] for d in (w, mom, var)],
                      "adamw_small")
    for o_idx, packed in enumerate(outs):
        for n, a in zip(small_names, _unpack(packed[0], shapes_s)):
            res.setdefault(n, [None] * 4)[o_idx] = a

    return (loss, grad_x, *[res[n][0] for n in ORDER], *[res[n][1] for n in ORDER],
            *[res[n][2] for n in ORDER], *[res[n][3] for n in ORDER])
```

```python
import functools

import jax
import jax.numpy as jnp
from jax import lax
from jax.experimental import pallas as pl
from jax.experimental.pallas import tpu as pltpu

F32, BF16 = jnp.float32, jnp.bfloat16
MESH = pl.DeviceIdType.MESH

D = 1024
N_META = 16
DEPTH = 4
EPS = 1e-6
LOG2E = 1.4426950408889634
LN2 = 0.6931471805599453
NEG = -1e30
FOX_W, FOX_DH = 512, 64
GDN_W, GDN_DH, GDN_H = 1024, 128, 8
CHUNK = 64
D_FF = 2816
N_DEV = 8
ADAM_LR, ADAM_B1, ADAM_B2, ADAM_EPS, ADAM_WD, ADAM_STEP = 0.001, 0.9, 0.999, 1e-08, 0.01, 10

VMEM_LIMIT_BYTES = 48 * 1024 * 1024
MATMUL_VMEM_BUDGET = 36 * 1024 * 1024
ROW_TILE = 128
ROW_TILES_WIDE = (640, 512, 256, 128)
ROW_TILES = (320, 256, 128)
LANE = 128

C_GQ, C_GK, C_GV, C_GZ, C_GATE, C_FQ, C_FK, C_FV = 0, 1024, 2048, 3072, 4096, 6144, 6656, 7168
W_MAIN = 7680
O_FQ, O_FK, O_FV, O_FL, O_GQ, O_GK, O_GV, O_BL, O_AL, O_GZ, O_GATE, O_END = (
    0, 512, 1024, 1536, 1544, 2568, 3592, 4616, 4624, 4632, 5656, 7704)


def _pick(n, cands):
    for c in cands:
        if n % c == 0:
            return c
    return n


def _call(body, *, name, out_shape, in_specs, out_specs, grid=(), scratch=(), sem=None):
    kw = dict(vmem_limit_bytes=VMEM_LIMIT_BYTES)
    if sem is not None:
        kw["dimension_semantics"] = sem
    return pl.pallas_call(body, name=name, out_shape=out_shape, grid=grid, in_specs=in_specs,
                          out_specs=out_specs, scratch_shapes=list(scratch),
                          compiler_params=pltpu.CompilerParams(**kw))


_DIMS = {"nn": (((1,), (0,)), ((), ())), "nt": (((1,), (1,)), ((), ())), "tn": (((0,), (0,)), ((), ()))}


_DIMS_BATCHED = {"nn": (((2,), (1,)), ((0,), (0,))), "nt": (((2,), (2,)), ((0,), (0,))),
                 "tn": (((1,), (1,)), ((0,), (0,)))}


def _dot(a, b, mode, prec=None):
    dims = _DIMS[mode] if a.ndim == 2 else _DIMS_BATCHED[mode]
    return lax.dot_general(a, b, dims, precision=prec, preferred_element_type=F32)


def _mm_grads(f, mode, a, b, g):
    if mode == "nn":
        return f(g, b, "nt"), f(a, g, "tn")
    if mode == "nt":
        return f(g, b, "nn"), f(g, a, "tn")
    return f(b, g, "nt"), f(a, g, "nn")


@functools.partial(jax.custom_vjp, nondiff_argnums=(2,))
def _mmb(a, b, mode):
    return _dot(a.astype(BF16), b.astype(BF16), mode)


def _mmb_fwd(a, b, mode):
    return _mmb(a, b, mode), (a, b)


def _mmb_bwd(mode, res, g):
    return _mm_grads(_mmb, mode, res[0], res[1], g)


_mmb.defvjp(_mmb_fwd, _mmb_bwd)


def _split(a):
    hi = a.astype(BF16)
    return hi, (a - hi.astype(F32)).astype(BF16)


@functools.partial(jax.custom_vjp, nondiff_argnums=(2,))
def _mmh(a, b, mode):
    ah, al = _split(a)
    bh, bl = _split(b)
    return _dot(ah, bh, mode) + (_dot(ah, bl, mode) + _dot(al, bh, mode))


def _mmh_fwd(a, b, mode):
    return _mmh(a, b, mode), (a, b)


def _mmh_bwd(mode, res, g):
    return _mm_grads(_mmh, mode, res[0], res[1], g)


_mmh.defvjp(_mmh_fwd, _mmh_bwd)


def _dot_sel(sel, x, mode):
    s = sel.astype(BF16)
    x1 = x.astype(BF16)
    x2, x3 = _split(x - x1.astype(F32))
    return _dot(s, x1, mode) + (_dot(s, x2, mode) + _dot(s, x3, mode))


@jax.custom_vjp
def _mms(sel, x):
    return _dot_sel(sel, x, "nn")


def _mms_fwd(sel, x):
    return _dot_sel(sel, x, "nn"), sel


def _mms_bwd(sel, g):
    return jnp.zeros_like(sel), _dot_sel(sel, g, "tn")


_mms.defvjp(_mms_fwd, _mms_bwd)


def _softplus(z):
    return jnp.maximum(z, 0.0) + jnp.log(1.0 + jnp.exp(-jnp.abs(z)))


def _log_sigmoid(z):
    return jnp.minimum(z, 0.0) - jnp.log(1.0 + jnp.exp(-jnp.abs(z)))


def _silu(z):
    return z * jax.nn.sigmoid(z)


def _iota(shape, dim):
    return lax.broadcasted_iota(jnp.int32, shape, dim)


def _inv_unit_lower_raw(n):
    c = n.shape[-1]
    ri, ci = _iota((c, c), 0), _iota((c, c), 1)
    eye = (ri == ci).astype(F32)
    dmask = (ri // 16) == (ci // 16)
    dpart = jnp.where(dmask, n, 0.0)
    lpart = n - dpart
    x = -dpart
    p = eye + x
    for _ in range(3):
        x = _mmh(x, x, "nn")
        p = p + _mmh(p, x, "nn")
    m = -_mmh(p, lpart, "nn")
    q = eye + m
    steps = 1
    while (1 << steps) < c // 16:
        steps += 1
    for _ in range(steps - 1):
        m = _mmh(m, m, "nn")
        q = q + _mmh(q, m, "nn")
    return _mmh(q, p, "nn")


@jax.custom_vjp
def _inv_given(n, t):
    return t


def _inv_given_fwd(n, t):
    return t, t


def _inv_given_bwd(t, g):
    c = t.shape[-1]
    strict = _iota((c, c), 0) > _iota((c, c), 1)
    d = -_mmh(_mmh(t, g, "tn"), t, "nt")
    return jnp.where(strict, d, 0.0), jnp.zeros_like(t)


_inv_given.defvjp(_inv_given_fwd, _inv_given_bwd)


def _shift_down(x, halo, s):
    if s == 0:
        return x
    xs = pltpu.roll(x, s, 0)
    hs = pltpu.roll(halo, s, 0)
    top = jnp.where(_iota(hs.shape, 0) < s, hs, xs[0:8])
    return jnp.concatenate([top, xs[8:]], axis=0)


def _shift_up(x, halo, s):
    if s == 0:
        return x
    tm = x.shape[0]
    xs = pltpu.roll(x, tm - s, 0)
    hs = pltpu.roll(halo, 8 - s, 0)
    bot = jnp.where(_iota(hs.shape, 0) >= 8 - s, hs, xs[tm - 8:])
    return jnp.concatenate([xs[:tm - 8], bot], axis=0)


def _causal_conv(x, halo, w):
    kk = w.shape[0]
    y = x * w[kk - 1:kk, :]
    for k in range(kk - 1):
        y = y + _shift_down(x, halo, kk - 1 - k) * w[k:k + 1, :]
    return y


def _head_scale(x, width, fn):
    outs = []
    for h in range(x.shape[1] // width):
        seg = x[:, h * width:(h + 1) * width]
        outs.append(seg * fn(jnp.sum(seg * seg, axis=1, keepdims=True)))
    return jnp.concatenate(outs, axis=1)


def _matmul(a, b, mode, out_dtype, add=None, name="mm"):
    if mode == "nn":
        (m, k), n = a.shape, b.shape[1]
    elif mode == "nt":
        (m, k), n = a.shape, b.shape[0]
    else:
        (k, m), n = a.shape, b.shape[1]
    tm = _pick(m, (1408, 1024, 512, 256, 128) if mode == "tn" else (640, 512, 256, 128))
    tn = _pick(n, (1536, 1408, 1024, 768, 512, 256, 128))
    sa, sb = a.dtype.itemsize, b.dtype.itemsize
    fixed = tm * tn * 4 * (3 + (2 if add is not None else 0))
    tk = 128
    for cand in (k, 2816, 2560, 1664, 1536, 1280, 1024, 832, 768, 640, 512, 256, 128):
        if mode != "tn" and cand != k and cand % LANE:
            continue
        if k % cand == 0 and fixed + 2 * cand * (tm * sa + tn * sb) <= MATMUL_VMEM_BUDGET:
            tk = cand
            break
    nk = k // tk
    a_spec = {"nn": pl.BlockSpec((tm, tk), lambda i, j, kk: (i, kk)),
              "nt": pl.BlockSpec((tm, tk), lambda i, j, kk: (i, kk)),
              "tn": pl.BlockSpec((tk, tm), lambda i, j, kk: (kk, i))}[mode]
    b_spec = {"nn": pl.BlockSpec((tk, tn), lambda i, j, kk: (kk, j)),
              "nt": pl.BlockSpec((tn, tk), lambda i, j, kk: (j, kk)),
              "tn": pl.BlockSpec((tk, tn), lambda i, j, kk: (kk, j))}[mode]
    o_spec = pl.BlockSpec((tm, tn), lambda i, j, kk: (i, j))
    has_add = add is not None

    def body(*refs):
        a_ref, b_ref = refs[0], refs[1]
        add_ref = refs[2] if has_add else None
        o_ref = refs[3] if has_add else refs[2]
        part = _dot(a_ref[...].astype(BF16), b_ref[...].astype(BF16), mode)
        if nk == 1:
            if has_add:
                part = part + add_ref[...].astype(F32)
            o_ref[...] = part.astype(out_dtype)
        else:
            acc = refs[-1]
            kk = pl.program_id(2)

            @pl.when(kk == 0)
            def _():
                acc[...] = part

            @pl.when(kk > 0)
            def _():
                acc[...] += part

            @pl.when(kk == nk - 1)
            def _():
                r = acc[...]
                if has_add:
                    r = r + add_ref[...].astype(F32)
                o_ref[...] = r.astype(out_dtype)

    ins = [a, b] + ([add] if has_add else [])
    specs = [a_spec, b_spec] + ([o_spec] if has_add else [])
    return _call(body, name=name, out_shape=jax.ShapeDtypeStruct((m, n), out_dtype), grid=(m // tm, n // tn, nk),
                 in_specs=specs, out_specs=o_spec,
                 scratch=[pltpu.VMEM((tm, tn), F32)] if nk > 1 else [],
                 sem=("parallel", "parallel", "arbitrary"))(*ins)


def _row_spec(width, colblock, tm):
    return pl.BlockSpec((tm, width), lambda i, cb=colblock: (i, cb))


def _full_spec(arr):
    nd = arr.ndim
    return pl.BlockSpec(arr.shape, lambda i, nd=nd: (0,) * nd)


def _rowwise(fn, rows, params, outs, name):
    lp = rows[0][0].shape[0]
    tm = _pick(lp, ROW_TILES_WIDE)
    nr, npar = len(rows), len(params)

    def body(*refs):
        row0 = pl.program_id(0) * tm
        vals = [r[...].astype(F32) for r in refs[:nr + npar]]
        res = fn(*vals, row0)
        for o_ref, r in zip(refs[nr + npar:], res):
            o_ref[...] = r.astype(o_ref.dtype)

    out = _call(body, name=name, grid=(lp // tm,),
                out_shape=[jax.ShapeDtypeStruct((lp, w), dt) for w, dt in outs],
                in_specs=[_row_spec(w, cb, tm) for _, w, cb in rows] + [_full_spec(p) for p in params],
                out_specs=[_row_spec(w, 0, tm) for w, _ in outs], sem=("parallel",))(
                    *[r[0] for r in rows], *params)
    return out


def _rowwise_bwd(fn, rows, params, cts, name, pad_rows, grad_dtypes, adds=None):
    lp = rows[0][0].shape[0]
    tm = _pick(lp, ROW_TILES)
    nr, npar, nct = len(rows), len(params), len(cts)
    adds = adds or [None] * nr
    add_list = [a for a in adds if a is not None]
    nadd = len(add_list)

    def body(*refs):
        i = pl.program_id(0)
        row0 = i * tm
        vals = [r[...].astype(F32) for r in refs[:nr + npar]]
        ct_vals = tuple(r[...].astype(F32) for r in refs[nr + npar:nr + npar + nct])
        add_refs = list(refs[nr + npar + nct:nr + npar + nct + nadd])
        outs = refs[nr + npar + nct + nadd:]
        _, vjp = jax.vjp(lambda *args: tuple(fn(*args, row0)), *vals)
        grads = vjp(ct_vals)
        valid = (row0 + _iota((tm, 1), 0)) >= pad_rows
        for idx in range(nr):
            g = jnp.where(valid, grads[idx], 0.0)
            if adds[idx] is not None:
                g = g + add_refs.pop(0)[...].astype(F32)
            outs[idx][...] = g.astype(outs[idx].dtype)
        for idx in range(npar):
            o_ref = outs[nr + idx]

            @pl.when(i == 0)
            def _(o_ref=o_ref):
                o_ref[...] = jnp.zeros_like(o_ref)

            o_ref[...] += grads[nr + idx]

    out = _call(body, name=name, grid=(lp // tm,),
                out_shape=[jax.ShapeDtypeStruct((lp, w), dt) for (_, w, _), dt in zip(rows, grad_dtypes)]
                + [jax.ShapeDtypeStruct(p.shape, F32) for p in params],
                in_specs=[_row_spec(w, cb, tm) for _, w, cb in rows] + [_full_spec(p) for p in params]
                + [_row_spec(w, cb, tm) for _, w, cb in cts] + [_row_spec(w, cb, tm) for _, w, cb in add_list],
                out_specs=[_row_spec(w, 0, tm) for _, w, _ in rows] + [_full_spec(p) for p in params],
                sem=("arbitrary",))(*[r[0] for r in rows], *params, *[c[0] for c in cts], *[a[0] for a in add_list])
    return out[:nr], out[nr:]


def _rmsnorm_fn(x, g, row0):
    return (x * lax.rsqrt(jnp.mean(x * x, axis=1, keepdims=True) + EPS) * g,)


def _fox_prep_fn(pad_rows, fq, fk, small, qg, kg, fb, row0):
    ri, ci = _iota((FOX_W, FOX_W), 0), _iota((FOX_W, FOX_W), 1)
    bd = jnp.where((ri // FOX_DH) == (ci // FOX_DH), 1.0 / FOX_DH, 0.0)

    def hn(x, g):
        return x * lax.rsqrt(_mmh(x * x, bd, "nn") + EPS) * g

    tm = small.shape[0]
    keep = (_iota((tm, LANE), 1) < 8) & ((row0 + _iota((tm, LANE), 0)) >= pad_rows)
    logf = jnp.where(keep, _log_sigmoid(small + fb) * LOG2E, 0.0)
    return hn(fq, qg) * (FOX_DH ** -0.5 * LOG2E), hn(fk, kg), logf


def _gdn_act_fn(cq, ck, cv, small, alog, dtb):
    tm = small.shape[0]
    q = _head_scale(_silu(cq), GDN_DH, lambda s: lax.rsqrt(s + EPS) * (GDN_DH ** -0.5))
    k = _head_scale(_silu(ck), GDN_DH, lambda s: lax.rsqrt(s + EPS))
    v = _silu(cv)
    lane = _iota((tm, LANE), 1)
    beta = jnp.where((lane >= 8) & (lane < 16), jax.nn.sigmoid(small), 0.0)
    g = jnp.where((lane >= 16) & (lane < 24), -jnp.exp(alog) * _softplus(small + dtb), 0.0)
    ri, ci = _iota((tm, tm), 0), _iota((tm, tm), 1)
    tri = jnp.where(((ri // CHUNK) == (ci // CHUNK)) & (ci <= ri), 1.0, 0.0)
    return q, k, v, beta + _mms(tri, g)


def _gdn_post_fn(o, gz, gn, row0):
    return (_head_scale(o, GDN_DH, lambda s: lax.rsqrt(s * (1.0 / GDN_DH) + EPS)) * gn * _silu(gz),)


def _merge_fn(g0, g1, ya, yb, row0):
    return (jax.nn.sigmoid(g0) * ya + jax.nn.sigmoid(g1) * yb,)


def _cumsum_rows(x, reverse, name):
    lp, w = x.shape
    tm = _pick(lp, (640, 512, 256, 128))
    nt = lp // tm

    def body(x_ref, o_ref, carry):
        i = pl.program_id(0)

        @pl.when(i == 0)
        def _():
            carry[...] = jnp.zeros_like(carry)

        ri, ci = _iota((tm, tm), 0), _iota((tm, tm), 1)
        tri = jnp.where((ci >= ri) if reverse else (ci <= ri), 1.0, 0.0)
        blk = x_ref[...]
        o_ref[...] = _dot_sel(tri, blk, "nn") + carry[0:1, :]
        carry[...] = carry[...] + jnp.sum(blk, axis=0, keepdims=True)

    idx = (lambda i: (nt - 1 - i, 0)) if reverse else (lambda i: (i, 0))
    return _call(body, name=name, grid=(nt,), out_shape=jax.ShapeDtypeStruct((lp, w), F32),
                 in_specs=[pl.BlockSpec((tm, w), idx)], out_specs=pl.BlockSpec((tm, w), idx),
                 scratch=[pltpu.VMEM((8, w), F32)], sem=("arbitrary",))(x)


def _fox_scores(q, k, fq, fk, hh, qpos0, kpos0, pad_rows, masked):
    tq, tk = q.shape[0], k.shape[0]
    lane = _iota(q.shape, 1)
    sel = (lane < FOX_DH) if hh == 0 else (lane >= FOX_DH)
    s = _dot(jnp.where(sel, q, jnp.zeros_like(q)), k, "nt") + fq - fk
    if not masked:
        return s, None, sel
    qpos = qpos0 + _iota((tq, tk), 0)
    kpos = kpos0 + _iota((tq, tk), 1)
    mask = (kpos <= qpos) & (kpos >= pad_rows)
    return jnp.where(mask, s, NEG), mask, sel


def _probs(s, mask, shift):
    p = jnp.exp2(s - shift)
    return p if mask is None else jnp.where(mask, p, 0.0)


def _both_variants(needs_mask, fn):
    @pl.when(needs_mask)
    def _():
        fn(True)

    @pl.when(jnp.logical_not(needs_mask))
    def _():
        fn(False)


def _lane_col(blk, lane_idx):
    return jnp.sum(jnp.where(_iota(blk.shape, 1) == lane_idx, blk, 0.0), axis=1, keepdims=True)


def _to_lanes(cols, width=LANE):
    lane = _iota((cols[0].shape[0], width), 1)
    out = jnp.zeros((cols[0].shape[0], width), F32)
    for idx, c in enumerate(cols):
        out = jnp.where(lane == idx, c, out)
    return out


def _fox_fwd(q, k, v, fsum, frow, pad_rows, v_col):
    lp = q.shape[0]
    t = _pick(lp, (640, 512, 256, 128))
    n = lp // t

    def body(q_ref, k_ref, v_ref, f_ref, fk_ref, o_ref, lse_ref, acc, m_s, l_s, fq_s):
        pr, i, j = pl.program_id(0), pl.program_id(1), pl.program_id(2)

        @pl.when(j == 0)
        def _():
            acc[...] = jnp.zeros_like(acc)
            m_s[...] = jnp.full_like(m_s, NEG)
            l_s[...] = jnp.zeros_like(l_s)
            for hh in range(2):
                fq_s[hh] = _lane_col(f_ref[...], 2 * pr + hh)

        def step(masked):
            for hh in range(2):
                s, mask, _ = _fox_scores(q_ref[...], k_ref[...], fq_s[hh], fk_ref[hh], hh, i * t, j * t, pad_rows,
                                         masked)
                m_prev = m_s[hh]
                m_new = jnp.maximum(m_prev, jnp.max(s, axis=1, keepdims=True))
                p = _probs(s, mask, m_new)
                alpha = jnp.exp2(m_prev - m_new)
                l_s[hh] = alpha * l_s[hh] + jnp.sum(p, axis=1, keepdims=True)
                acc[hh] = alpha * acc[hh] + _dot(p.astype(BF16), v_ref[...].astype(BF16), "nn")
                m_s[hh] = m_new

        @pl.when(j <= i)
        def _():
            _both_variants((j == i) | (j == 0), step)

        @pl.when(j == i)
        def _():
            outs, lses = [], []
            for hh in range(2):
                l = l_s[hh]
                ok = l > 0.0
                outs.append(acc[hh] * jnp.where(ok, 1.0 / jnp.where(ok, l, 1.0), 0.0))
                lses.append(jnp.where(ok, m_s[hh] + jnp.log2(jnp.where(ok, l, 1.0)), 0.0))
            lane = _iota((t, LANE), 1)
            o_ref[...] = jnp.where(lane < FOX_DH, outs[0], outs[1]).astype(o_ref.dtype)
            lse_ref[...] = _to_lanes(lses)

    qspec = pl.BlockSpec((t, LANE), lambda p, i, j: (i, p))
    kspec = pl.BlockSpec((t, LANE), lambda p, i, j: (jnp.minimum(j, i), p))
    vspec = pl.BlockSpec((t, LANE), lambda p, i, j: (jnp.minimum(j, i), v_col + p))
    fspec = pl.BlockSpec((t, LANE), lambda p, i, j: (i, 0))
    rspec = pl.BlockSpec((2, 1, t), lambda p, i, j: (p, 0, jnp.minimum(j, i)))
    return _call(body, name="fox_fwd", grid=(FOX_W // LANE, n, n),
                 out_shape=[jax.ShapeDtypeStruct((lp, FOX_W), BF16), jax.ShapeDtypeStruct((lp, FOX_W), F32)],
                 in_specs=[qspec, kspec, vspec, fspec, rspec], out_specs=[qspec, qspec],
                 scratch=[pltpu.VMEM((2, t, LANE), F32), pltpu.VMEM((2, t, 1), F32), pltpu.VMEM((2, t, 1), F32),
                          pltpu.VMEM((2, t, 1), F32)],
                 sem=("parallel", "parallel", "arbitrary"))(q, k, v, fsum, frow)


def _fox_delta_fn(o, do, row0):
    ri, ci = _iota((FOX_W, LANE), 0), _iota((FOX_W, LANE), 1)
    sel = jnp.where((ri // FOX_DH) == ci, 1.0, 0.0).astype(BF16)
    x = o * do
    x1 = x.astype(BF16)
    x2, x3 = _split(x - x1.astype(F32))
    return (_dot(x1, sel, "nn") + (_dot(x2, sel, "nn") + _dot(x3, sel, "nn")),)


def _fox_bwd(q, k, v, fsum, frow, do, lse, delta, pad_rows, v_col):
    lp = q.shape[0]
    t = _pick(lp, (640, 512, 256, 128))
    n = lp // t

    def body(q_ref, k_ref, v_ref, f_ref, fk_ref, do_ref, lse_ref, dl_ref,
             dq_ref, dk_ref, dv_ref, dfq_ref, dfk_ref, dka, dva, dfa):
        pr, j, i = pl.program_id(0), pl.program_id(1), pl.program_id(2)
        lane = _iota((t, LANE), 1)

        @pl.when((j == 0) & (i == 0))
        def _():
            dq_ref[...] = jnp.zeros_like(dq_ref)
            dfq_ref[...] = jnp.zeros_like(dfq_ref)

        @pl.when(i == 0)
        def _():
            dka[...] = jnp.zeros_like(dka)
            dva[...] = jnp.zeros_like(dva)
            dfa[...] = jnp.zeros_like(dfa)

        def step(masked):
            rows = pl.ds(pl.multiple_of(i * t, t), t)
            dq_add = jnp.zeros((t, LANE), F32)
            rowsums = []
            for hh in range(2):
                fq = _lane_col(f_ref[...], 2 * pr + hh)
                s, mask, sel = _fox_scores(q_ref[...], k_ref[...], fq, fk_ref[hh], hh, i * t, j * t, pad_rows, masked)
                p = _probs(s, mask, _lane_col(lse_ref[...], hh))
                dop = jnp.where(sel, do_ref[...], jnp.zeros_like(do_ref[...]))
                ds = p * (_dot(dop, v_ref[...].astype(BF16), "nt") - _lane_col(dl_ref[...], 2 * pr + hh))
                dsb = ds.astype(BF16)
                dva[hh] += _dot(p.astype(BF16), do_ref[...], "tn")
                dka[hh] += _dot(dsb, q_ref[...], "tn")
                dfa[hh] -= jnp.sum(ds, axis=0, keepdims=True)
                dq_add = dq_add + _dot(dsb, jnp.where(sel, k_ref[...], jnp.zeros_like(k_ref[...])), "nn")
                rowsums.append(jnp.sum(ds, axis=1, keepdims=True))
            dq_ref[rows, :] += dq_add
            dfq_ref[rows, :] += _to_lanes(rowsums)

        @pl.when(i >= j)
        def _():
            _both_variants((j == i) | (j == 0), step)

        @pl.when(i == n - 1)
        def _():
            dk_ref[...] = jnp.where(lane < FOX_DH, dka[0], dka[1])
            dv_ref[...] = (jnp.where(lane < FOX_DH, dva[0], dva[1]) * LOG2E).astype(dv_ref.dtype)
            dfk_ref[...] = dfa[...]

    qspec = pl.BlockSpec((t, LANE), lambda p, j, i: (jnp.maximum(i, j), p))
    f_q = pl.BlockSpec((t, LANE), lambda p, j, i: (jnp.maximum(i, j), 0))
    kspec = pl.BlockSpec((t, LANE), lambda p, j, i: (j, p))
    vspec = pl.BlockSpec((t, LANE), lambda p, j, i: (j, v_col + p))
    rspec = pl.BlockSpec((2, 1, t), lambda p, j, i: (p, 0, j))
    whole = pl.BlockSpec((lp, LANE), lambda p, j, i: (0, p))
    wide = jax.ShapeDtypeStruct((lp, FOX_W), F32)
    return _call(body, name="fox_bwd", grid=(FOX_W // LANE, n, n),
                 out_shape=[wide, wide, jax.ShapeDtypeStruct((lp, FOX_W), BF16), wide,
                            jax.ShapeDtypeStruct((8, 1, lp), F32)],
                 in_specs=[qspec, kspec, vspec, f_q, rspec, qspec, qspec, f_q],
                 out_specs=[whole, kspec, kspec, whole, rspec],
                 scratch=[pltpu.VMEM((2, t, LANE), F32), pltpu.VMEM((2, t, LANE), F32), pltpu.VMEM((2, 1, t), F32)],
                 sem=("parallel", "arbitrary", "arbitrary"))(q, k, v, fsum, frow, do, lse, delta)


def _gdn_chunk(q, k, v, beta, gcol, grow, s, inv):
    c = q.shape[-2]
    ri, ci = _iota((c, c), 0), _iota((c, c), 1)
    dec = jnp.exp(jnp.where(ri >= ci, gcol - grow, NEG))
    dec_strict = jnp.where(ri > ci, dec, 0.0)
    eg = jnp.exp(gcol)
    kb = k * beta
    t = inv(_mmb(kb, k, "nt") * dec_strict)
    u_hat = _mmh(t, v * beta, "nn")
    w = _mmh(t, kb * eg, "nn")
    u = u_hat - _mmb(w, s, "nn")
    o = _mmb(q * eg, s, "nn") + _mmb(_mmb(q, k, "nt") * dec, u, "nn")
    glast = jnp.sum(jnp.where(_iota((1, c), 1) == c - 1, grow, 0.0), axis=-1, keepdims=True)
    s_new = s * jnp.exp(glast) + _mmb(k * jnp.exp(glast - gcol), u, "tn")
    return o, s_new


def _gdn_specs(lp, reverse):
    n = lp // CHUNK
    pos = (lambda c: n - 1 - c) if reverse else (lambda c: c)
    wide = pl.BlockSpec((CHUNK, GDN_W), lambda c: (pos(c), 0))
    lanes = pl.BlockSpec((CHUNK, LANE), lambda c: (pos(c), 0))
    row = pl.BlockSpec((GDN_H, 1, 1, CHUNK), lambda c: (0, pos(c), 0, 0))
    st = pl.BlockSpec((GDN_H, 1, GDN_DH, GDN_DH), lambda c: (0, pos(c), 0, 0))
    return n, wide, lanes, row, st


def _heads(ref):
    return jnp.stack([ref[:, h * GDN_DH:(h + 1) * GDN_DH] for h in range(GDN_H)])


def _put_heads(ref, val):
    for h in range(GDN_H):
        ref[:, h * GDN_DH:(h + 1) * GDN_DH] = val[h]


def _head_cols(blk, lane0):
    return jnp.stack([_lane_col(blk, lane0 + h) for h in range(GDN_H)])


def _gdn_fwd(q, k, v, bg, grow):
    lp = q.shape[0]
    n, wide, lanes, row, st = _gdn_specs(lp, False)

    def body(q_ref, k_ref, v_ref, bg_ref, gr_ref, o_ref, sp_ref, t_ref, s_scr):
        @pl.when(pl.program_id(0) == 0)
        def _():
            s_scr[...] = jnp.zeros_like(s_scr)

        def inv(m):
            t = _inv_unit_lower_raw(m)
            t_ref[:, 0] = t
            return t

        s = s_scr[...]
        sp_ref[:, 0] = s
        bg_blk = bg_ref[...]
        o, s_new = _gdn_chunk(_heads(q_ref), _heads(k_ref), _heads(v_ref), _head_cols(bg_blk, 8),
                              _head_cols(bg_blk, 16), gr_ref[:, 0], s, inv)
        _put_heads(o_ref, o)
        s_scr[...] = s_new

    tri = pl.BlockSpec((GDN_H, 1, CHUNK, CHUNK), lambda c: (0, c, 0, 0))
    return _call(body, name="gdn_fwd", grid=(n,),
                 out_shape=[jax.ShapeDtypeStruct((lp, GDN_W), F32),
                            jax.ShapeDtypeStruct((GDN_H, n, GDN_DH, GDN_DH), F32),
                            jax.ShapeDtypeStruct((GDN_H, n, CHUNK, CHUNK), F32)],
                 in_specs=[wide, wide, wide, lanes, row], out_specs=[wide, st, tri],
                 scratch=[pltpu.VMEM((GDN_H, GDN_DH, GDN_DH), F32)], sem=("arbitrary",))(q, k, v, bg, grow)


def _gdn_bwd(q, k, v, bg, grow, sprev, tinv, do):
    lp = q.shape[0]
    n, wide, lanes, row, st = _gdn_specs(lp, True)

    def body(q_ref, k_ref, v_ref, bg_ref, gr_ref, sp_ref, t_ref, do_ref,
             dq_ref, dk_ref, dv_ref, dbg_ref, dgr_ref, ds_scr):
        @pl.when(pl.program_id(0) == 0)
        def _():
            ds_scr[...] = jnp.zeros_like(ds_scr)

        t_saved = t_ref[:, 0]
        fn = functools.partial(_gdn_chunk, inv=lambda m: _inv_given(m, t_saved))
        bg_blk = bg_ref[...]
        _, vjp = jax.vjp(fn, _heads(q_ref), _heads(k_ref), _heads(v_ref), _head_cols(bg_blk, 8),
                         _head_cols(bg_blk, 16), gr_ref[:, 0], sp_ref[:, 0])
        dq, dk, dv, db, dgc, dgr, ds = vjp((_heads(do_ref), ds_scr[...]))
        _put_heads(dq_ref, dq)
        _put_heads(dk_ref, dk)
        _put_heads(dv_ref, dv)
        lane = _iota((CHUNK, LANE), 1)
        dbg = jnp.zeros((CHUNK, LANE), F32)
        for h in range(GDN_H):
            dbg = jnp.where(lane == 8 + h, db[h], jnp.where(lane == 16 + h, dgc[h], dbg))
        dbg_ref[...] = dbg
        dgr_ref[:, 0] = dgr
        ds_scr[...] = ds

    wshape = jax.ShapeDtypeStruct((lp, GDN_W), F32)
    tri = pl.BlockSpec((GDN_H, 1, CHUNK, CHUNK), lambda c: (0, n - 1 - c, 0, 0))
    return _call(body, name="gdn_bwd", grid=(n,),
                 out_shape=[wshape, wshape, wshape, jax.ShapeDtypeStruct((lp, LANE), F32),
                            jax.ShapeDtypeStruct((GDN_H, n, 1, CHUNK), F32)],
                 in_specs=[wide, wide, wide, lanes, row, st, tri, wide], out_specs=[wide, wide, wide, lanes, row],
                 scratch=[pltpu.VMEM((GDN_H, GDN_DH, GDN_DH), F32)], sem=("arbitrary",))(
                     q, k, v, bg, grow, sprev, tinv, do)


def _halo_prev(width, colblock, tm):
    return pl.BlockSpec((8, width), lambda i, cb=colblock: (jnp.maximum(i * (tm // 8) - 1, 0), cb))


def _gdn_act(proj, small, conv_w, alog_row, dtb_row):
    lp = proj.shape[0]
    tm = _pick(lp, ROW_TILES)

    def body(xq, xk, xv, hq, hk, hv, wq, wk, wv, sm, al, dt, q_ref, k_ref, v_ref, bg_ref):
        first = (pl.program_id(0) > 0).astype(F32)
        cs = [_causal_conv(x[...], h[...] * first, w[...]) for x, h, w in ((xq, hq, wq), (xk, hk, wk), (xv, hv, wv))]
        q, k, v, bg = _gdn_act_fn(cs[0], cs[1], cs[2], sm[...], al[...], dt[...])
        q_ref[...], k_ref[...], v_ref[...], bg_ref[...] = q, k, v, bg

    wide = jax.ShapeDtypeStruct((lp, GDN_W), F32)
    wspec = [pl.BlockSpec((4, GDN_W), lambda i, c=c: (0, c)) for c in range(3)]
    return _call(body, name="gdn_act", grid=(lp // tm,),
                 out_shape=[wide, wide, wide, jax.ShapeDtypeStruct((lp, LANE), F32)],
                 in_specs=[_row_spec(GDN_W, c, tm) for c in range(3)] + [_halo_prev(GDN_W, c, tm) for c in range(3)]
                 + wspec + [_row_spec(LANE, 0, tm), _full_spec(alog_row), _full_spec(dtb_row)],
                 out_specs=[_row_spec(GDN_W, 0, tm)] * 3 + [_row_spec(LANE, 0, tm)], sem=("parallel",))(
                     proj, proj, proj, proj, proj, proj, conv_w, conv_w, conv_w, small, alog_row, dtb_row)


def _gdn_act_bwd(proj, small, conv_w, alog_row, dtb_row, dq, dk, dv, dbg):
    lp = proj.shape[0]
    tm = ROW_TILE

    def body(xq, xk, xv, hq, hk, hv, wq, wk, wv, sm, al, dt, dq_r, dk_r, dv_r, dbg_r,
             dc_ref, dsm_ref, dal_ref, ddt_ref, dw_ref):
        i = pl.program_id(0)
        first = (i > 0).astype(F32)
        xs = [(x[...], h[...] * first, w[...]) for x, h, w in ((xq, hq, wq), (xk, hk, wk), (xv, hv, wv))]
        cs = [_causal_conv(*t) for t in xs]
        _, vjp = jax.vjp(_gdn_act_fn, cs[0], cs[1], cs[2], sm[...], al[...], dt[...])
        dcq, dck, dcv, dsm, dal, ddt = vjp((dq_r[...], dk_r[...], dv_r[...], dbg_r[...]))
        dsm_ref[...] = dsm

        @pl.when(i == 0)
        def _():
            dal_ref[...] = jnp.zeros_like(dal_ref)
            ddt_ref[...] = jnp.zeros_like(ddt_ref)
            dw_ref[...] = jnp.zeros_like(dw_ref)

        dal_ref[...] += dal
        ddt_ref[...] += ddt
        for c, (dc, (x, h, w)) in enumerate(zip((dcq, dck, dcv), xs)):
            dc_ref[:, c * GDN_W:(c + 1) * GDN_W] = dc
            rows = [jnp.sum(_shift_down(x, h, 3 - kk) * dc, axis=0, keepdims=True) for kk in range(4)]
            dw_ref[:, c * GDN_W:(c + 1) * GDN_W] += jnp.concatenate(rows, axis=0)

    wspec = [pl.BlockSpec((4, GDN_W), lambda i, c=c: (0, c)) for c in range(3)]
    row128 = jax.ShapeDtypeStruct((1, LANE), F32)
    return _call(body, name="gdn_act_bwd", grid=(lp // tm,),
                 out_shape=[jax.ShapeDtypeStruct((lp, 3 * GDN_W), F32), jax.ShapeDtypeStruct((lp, LANE), F32),
                            row128, row128, jax.ShapeDtypeStruct((4, 3 * GDN_W), F32)],
                 in_specs=[_row_spec(GDN_W, c, tm) for c in range(3)] + [_halo_prev(GDN_W, c, tm) for c in range(3)]
                 + wspec + [_row_spec(LANE, 0, tm), _full_spec(alog_row), _full_spec(dtb_row)]
                 + [_row_spec(GDN_W, 0, tm)] * 3 + [_row_spec(LANE, 0, tm)],
                 out_specs=[_row_spec(3 * GDN_W, 0, tm), _row_spec(LANE, 0, tm),
                            _full_spec(alog_row), _full_spec(dtb_row), pl.BlockSpec((4, 3 * GDN_W), lambda i: (0, 0))],
                 sem=("arbitrary",))(proj, proj, proj, proj, proj, proj, conv_w, conv_w, conv_w, small,
                                     alog_row, dtb_row, dq, dk, dv, dbg)


def _ffn_act(up_pre, conv_w):
    lp = up_pre.shape[0]
    tm = _pick(lp, ROW_TILES)

    def body(xg, xv, hg, hv, wg, wv, a_ref):
        first = (pl.program_id(0) > 0).astype(F32)
        ug = _causal_conv(xg[...], hg[...] * first, wg[...])
        uv = _causal_conv(xv[...], hv[...] * first, wv[...])
        a_ref[...] = (_silu(ug) * uv).astype(a_ref.dtype)

    wspec = [pl.BlockSpec((3, D_FF), lambda i, c=c: (0, c)) for c in range(2)]
    return _call(body, name="ffn_act", grid=(lp // tm,), out_shape=jax.ShapeDtypeStruct((lp, D_FF), BF16),
                 in_specs=[_row_spec(D_FF, c, tm) for c in range(2)] + [_halo_prev(D_FF, c, tm) for c in range(2)] + wspec,
                 out_specs=_row_spec(D_FF, 0, tm), sem=("parallel",))(up_pre, up_pre, up_pre, up_pre, conv_w, conv_w)


def _ffn_act_bwd(up_pre, conv_w, dact, tm=ROW_TILE):
    lp = up_pre.shape[0]

    def body(xg, xv, hg, hv, wg, wv, da, du_ref, dw_ref):
        i = pl.program_id(0)
        first = (i > 0).astype(F32)
        xs = [(x[...], h[...] * first, w[...]) for x, h, w in ((xg, hg, wg), (xv, hv, wv))]
        ug, uv = [_causal_conv(*t) for t in xs]
        _, vjp = jax.vjp(lambda a, b: _silu(a) * b, ug, uv)
        dus = vjp(da[...].astype(F32))

        @pl.when(i == 0)
        def _():
            dw_ref[...] = jnp.zeros_like(dw_ref)

        for c, (du, (x, h, w)) in enumerate(zip(dus, xs)):
            du_ref[:, c * D_FF:(c + 1) * D_FF] = du
            rows = [jnp.sum(_shift_down(x, h, 2 - kk) * du, axis=0, keepdims=True) for kk in range(3)]
            dw_ref[:, c * D_FF:(c + 1) * D_FF] += jnp.concatenate(rows, axis=0)

    wspec = [pl.BlockSpec((3, D_FF), lambda i, c=c: (0, c)) for c in range(2)]
    return _call(body, name="ffn_act_bwd", grid=(lp // tm,),
                 out_shape=[jax.ShapeDtypeStruct((lp, 2 * D_FF), F32), jax.ShapeDtypeStruct((3, 2 * D_FF), F32)],
                 in_specs=[_row_spec(D_FF, c, tm) for c in range(2)] + [_halo_prev(D_FF, c, tm) for c in range(2)]
                 + wspec + [_row_spec(D_FF, 0, tm)],
                 out_specs=[_row_spec(2 * D_FF, 0, tm), pl.BlockSpec((3, 2 * D_FF), lambda i: (0, 0))],
                 sem=("arbitrary",))(up_pre, up_pre, up_pre, up_pre, conv_w, conv_w, dact)


def _conv_bwd_x(dy, w, pad_rows, width, name):
    lp, ctot = dy.shape
    tm = _pick(lp, ROW_TILES)
    nt = lp // tm
    kk = w.shape[0]

    def body(d_ref, h_ref, w_ref, o_ref):
        i = pl.program_id(0)
        last = (i < nt - 1).astype(F32)
        d, h, wv = d_ref[...], h_ref[...] * last, w_ref[...]
        y = d * wv[kk - 1:kk, :]
        for k in range(kk - 1):
            y = y + _shift_up(d, h, kk - 1 - k) * wv[k:k + 1, :]
        valid = (i * tm + _iota((tm, 1), 0)) >= pad_rows
        o_ref[...] = jnp.where(valid, y, 0.0).astype(o_ref.dtype)

    return _call(body, name=name, grid=(nt, ctot // width), out_shape=jax.ShapeDtypeStruct((lp, ctot), BF16),
                 in_specs=[pl.BlockSpec((tm, width), lambda i, c: (i, c)),
                           pl.BlockSpec((8, width), lambda i, c: (jnp.minimum((i + 1) * (tm // 8), lp // 8 - 1), c)),
                           pl.BlockSpec((kk, width), lambda i, c: (0, c))],
                 out_specs=pl.BlockSpec((tm, width), lambda i, c: (i, c)), sem=("parallel", "parallel"))(dy, dy, w)


def _loss_head(h_res, target, row_start, tm=ROW_TILE):
    lp, d = h_res.shape
    t0 = row_start // tm

    def body(h_ref, t_ref, dy_ref, loss_ref):
        i = pl.program_id(0)

        @pl.when(i == 0)
        def _():
            loss_ref[...] = jnp.zeros_like(loss_ref)

        live = (i >= t0).astype(F32)
        err = (h_ref[...] - t_ref[...]) * live
        dy_ref[...] = err * (1.0 / d)
        loss_ref[...] += 0.5 / d * jnp.sum(err * err)

    return _call(body, name="loss_head", grid=(lp // tm,),
                 out_shape=[jax.ShapeDtypeStruct((lp, d), F32), jax.ShapeDtypeStruct((8, LANE), F32)],
                 in_specs=[pl.BlockSpec((tm, d), lambda i: (i, 0)),
                           pl.BlockSpec((tm, d), lambda i: (jnp.maximum(i - t0, 0), 0))],
                 out_specs=[pl.BlockSpec((tm, d), lambda i: (i, 0)), pl.BlockSpec((8, LANE), lambda i: (0, 0))],
                 sem=("arbitrary",))(h_res, target)


def _sum_adamw(parts, w, m, v, name):
    a, r, c = w.shape
    tm = _pick(r, (256, 128, 64, 32, 16))
    bc1 = 1.0 - ADAM_B1 ** ADAM_STEP
    bc2 = 1.0 - ADAM_B2 ** ADAM_STEP

    def body(p_ref, w_ref, m_ref, v_ref, g_ref, d_ref, nm_ref, nv_ref):
        g = p_ref[0, 0].astype(F32)
        for s in range(1, N_DEV):
            g = g + p_ref[s, 0].astype(F32)
        nm = ADAM_B1 * m_ref[0] + (1.0 - ADAM_B1) * g
        nv = ADAM_B2 * v_ref[0] + (1.0 - ADAM_B2) * (g * g)
        g_ref[0] = g
        nm_ref[0] = nm
        nv_ref[0] = nv
        d_ref[0] = -ADAM_LR * ((nm / bc1) / (jnp.sqrt(nv / bc2) + ADAM_EPS) + ADAM_WD * w_ref[0])

    spec = pl.BlockSpec((1, tm, c), lambda l, i: (l, i, 0))
    shp = jax.ShapeDtypeStruct((a, r, c), F32)
    return _call(body, name=name, grid=(a, r // tm), out_shape=[shp] * 4,
                 in_specs=[pl.BlockSpec((N_DEV, 1, tm, c), lambda l, i: (0, l, i, 0)), spec, spec, spec],
                 out_specs=[spec] * 4, sem=("parallel", "parallel"))(parts, w, m, v)


_ANY = pl.BlockSpec(memory_space=pl.ANY)


def _all_gather(block, name):
    def body(x_ref, out_ref, send_sems, recv_sems, local_sem):
        x, y, c = lax.axis_index("x"), lax.axis_index("y"), lax.axis_index("c")
        me, sibling = (x, y, c), (x, y, 1 - c)
        chips = [(1 - x, y), (x, 1 - y), (1 - x, 1 - y)]

        def slot(px, py, pc):
            return out_ref.at[4 * px + 2 * py + pc]

        def copy(k, blk, to, src=None):
            return pltpu.make_async_remote_copy(
                src_ref=slot(*blk) if src is None else src, dst_ref=slot(*blk),
                send_sem=send_sems.at[k], recv_sem=recv_sems.at[k], device_id=to, device_id_type=MESH)

        mine = pltpu.make_async_copy(x_ref, slot(*me), local_sem)
        mine.start()
        first = [copy(0, me, sibling, src=x_ref)]
        first += [copy(1 + j, me, (*chip, c), src=x_ref) for j, chip in enumerate(chips)]
        for cp in first:
            cp.start()
        passed = [copy(4 + j, (*chip, c), sibling) for j, chip in enumerate(chips)]
        for j, chip in enumerate(chips):
            copy(1 + j, (*chip, c), me).wait_recv()
            passed[j].start()
        copy(0, sibling, me).wait_recv()
        for j, chip in enumerate(chips):
            copy(4 + j, (*chip, 1 - c), me).wait_recv()
        for cp in first + passed:
            cp.wait_send()
        mine.wait()

    return pl.pallas_call(
        body, name=name, out_shape=jax.ShapeDtypeStruct((N_DEV,) + block.shape, block.dtype),
        in_specs=[_ANY], out_specs=_ANY,
        scratch_shapes=[pltpu.SemaphoreType.DMA((7,)), pltpu.SemaphoreType.DMA((7,)), pltpu.SemaphoreType.DMA],
    )(block)


def _all_to_all(src, name):
    def body(s_ref, o_ref, send_sems, recv_sems, local_sem):
        x, y, c = lax.axis_index("x"), lax.axis_index("y"), lax.axis_index("c")
        me = 4 * x + 2 * y + c
        mine = pltpu.make_async_copy(s_ref.at[me], o_ref.at[me], local_sem)
        mine.start()
        copies = []
        for k in range(1, N_DEV):
            px = 1 - x if k & 4 else x
            py = 1 - y if k & 2 else y
            pc = 1 - c if k & 1 else c
            peer = 4 * px + 2 * py + pc
            copies.append((pltpu.make_async_remote_copy(
                src_ref=s_ref.at[peer], dst_ref=o_ref.at[me], send_sem=send_sems.at[k - 1],
                recv_sem=recv_sems.at[k - 1], device_id=(px, py, pc), device_id_type=MESH), peer, k))
        for cp, _, _ in copies:
            cp.start()
        for cp, peer, k in copies:
            cp.wait_send()
            pltpu.make_async_remote_copy(
                src_ref=s_ref.at[peer], dst_ref=o_ref.at[peer], send_sem=send_sems.at[k - 1],
                recv_sem=recv_sems.at[k - 1], device_id=(x, y, c), device_id_type=MESH).wait_recv()
        mine.wait()

    return pl.pallas_call(
        body, name=name, out_shape=jax.ShapeDtypeStruct(src.shape, src.dtype), in_specs=[_ANY], out_specs=_ANY,
        scratch_shapes=[pltpu.SemaphoreType.DMA((7,)), pltpu.SemaphoreType.DMA((7,)), pltpu.SemaphoreType.DMA],
    )(src)


_HBM = pl.BlockSpec(memory_space=pltpu.HBM)
_SEM = pl.BlockSpec(memory_space=pltpu.SEMAPHORE)
_EFFECT = pltpu.SideEffectType.DATAFLOW_SIDE_EFFECTING


def _peer_list(x, y, c):
    return [(1 - x if k & 4 else x, 1 - y if k & 2 else y, 1 - c if k & 1 else c) for k in range(1, N_DEV)]


def _exchange_copies(s_refs, l_refs, send_sems, recv_sems, landing_of_peer, same_block):
    x, y, c = lax.axis_index("x"), lax.axis_index("y"), lax.axis_index("c")
    me = 4 * x + 2 * y + c
    out = []
    for wi, (s_ref, l_ref) in enumerate(zip(s_refs, l_refs)):
        for k, (px, py, pc) in enumerate(_peer_list(x, y, c)):
            peer = 4 * px + 2 * py + pc
            idx = wi * (N_DEV - 1) + k
            out.append(pltpu.make_async_remote_copy(
                src_ref=s_ref if same_block else s_ref.at[peer], dst_ref=l_ref.at[peer if landing_of_peer else me],
                send_sem=send_sems.at[idx], recv_sem=recv_sems.at[idx], device_id=(px, py, pc), device_id_type=MESH))
    return out


def _exchange_start(srcs, name, same_block=False):
    nw = len(srcs)
    ncp = nw * (N_DEV - 1)

    def body(*refs):
        for cp in _exchange_copies(refs[:nw], refs[nw:2 * nw], refs[2 * nw], refs[2 * nw + 1], False, same_block):
            cp.start()
        refs[-1][...] = jnp.zeros_like(refs[-1])

    land_shapes = [((N_DEV,) + s.shape) if same_block else s.shape for s in srcs]
    hbm = [pltpu.HBM(s.shape, s.dtype) for s in srcs]
    hbm_l = [pltpu.HBM(ls, s.dtype) for ls, s in zip(land_shapes, srcs)]
    outs = pl.pallas_call(
        body, name=name,
        out_shape=(pltpu.SemaphoreType.DMA((ncp,)), pltpu.SemaphoreType.DMA((ncp,)), *hbm, *hbm_l,
                   jax.ShapeDtypeStruct((8, LANE), F32)),
        in_specs=[_HBM] * (2 * nw), out_specs=(_SEM, _SEM, *[_HBM] * (2 * nw), pl.BlockSpec(memory_space=pltpu.VMEM)),
        input_output_aliases={i: 2 + i for i in range(2 * nw)},
        compiler_params=pltpu.CompilerParams(has_side_effects=_EFFECT),
    )(*[pltpu.with_memory_space_constraint(s, pltpu.HBM) for s in srcs],
      *[pltpu.with_memory_space_constraint(lax.empty(ls, s.dtype), pltpu.HBM) for ls, s in zip(land_shapes, srcs)])
    return outs[0], outs[1], list(outs[2:2 + nw]), list(outs[2 + nw:2 + 2 * nw]), outs[-1]


def _exchange_wait(send_sems, recv_sems, srcs, lands, after, name, same_block=False):
    nw = len(srcs)

    def body(*refs):
        for cp in _exchange_copies(refs[:nw], refs[nw:2 * nw], refs[2 * nw], refs[2 * nw + 1], True, same_block):
            cp.wait_send()
            cp.wait_recv()

    hbm = [pltpu.HBM(a.shape, a.dtype) for a in list(srcs) + list(lands)]
    outs = pl.pallas_call(
        body, name=name, out_shape=tuple(hbm),
        in_specs=[_HBM] * (2 * nw) + [_SEM, _SEM, pl.BlockSpec(memory_space=pl.ANY)], out_specs=tuple([_HBM] * (2 * nw)),
        input_output_aliases={i: i for i in range(2 * nw)},
        compiler_params=pltpu.CompilerParams(has_side_effects=_EFFECT),
    )(*srcs, *lands, send_sems, recv_sems, after)
    return list(outs[:nw]), list(outs[nw:])


def _pack(blocks, width, dtype, row_mult):
    flat = jnp.concatenate([b.astype(dtype).reshape(-1) for b in blocks])
    per = width * row_mult
    total = -(-flat.shape[0] // per) * per
    return jnp.pad(flat, (0, total - flat.shape[0])).reshape(total // width, width)


def _pack_dest(fulls, axes, width, dtype, row_mult):
    rows = []
    for f, ax in zip(fulls, axes):
        f = f.astype(dtype)
        if ax is None:
            rows.append(jnp.broadcast_to(f.reshape(1, -1), (N_DEV, f.size)))
        else:
            shp = f.shape
            f = f.reshape(shp[:ax] + (N_DEV, shp[ax] // N_DEV) + shp[ax + 1:])
            rows.append(jnp.moveaxis(f, ax, 0).reshape(N_DEV, -1))
    flat = jnp.concatenate(rows, axis=1)
    per = width * row_mult
    total = -(-flat.shape[1] // per) * per
    return jnp.pad(flat, ((0, 0), (0, total - flat.shape[1]))).reshape(N_DEV, total // width, width)


def _unpack(packed, shapes):
    flat = packed.reshape(-1)
    out, off = [], 0
    for s in shapes:
        n = 1
        for d in s:
            n *= d
        out.append(flat[off:off + n].reshape(s))
        off += n
    return out


def _unpack_gathered(gathered, shapes, axes):
    flat = gathered.reshape(N_DEV, -1)
    out, off = [], 0
    for s, ax in zip(shapes, axes):
        n = 1
        for d in s:
            n *= d
        blk = jnp.moveaxis(flat[:, off:off + n].reshape((N_DEV,) + tuple(s)), 0, ax)
        out.append(blk.reshape(tuple(s[:ax]) + (N_DEV * s[ax],) + tuple(s[ax + 1:])))
        off += n
    return out


def _shard_cols(blocks, a, b):
    shard = blocks[0].shape[1]
    out = []
    while a < b:
        d = a // shard
        hi = min(b, (d + 1) * shard)
        out.append(blocks[d][:, a - d * shard:hi - d * shard])
        a = hi
    return out


def _permute_w_in(blocks):
    main = jnp.concatenate(_shard_cols(blocks, O_GQ, O_BL) + _shard_cols(blocks, O_GZ, O_END)
                           + _shard_cols(blocks, O_FQ, O_FL), axis=1)
    pad = jnp.zeros((blocks[0].shape[0], LANE - 24), blocks[0].dtype)
    small = jnp.concatenate(_shard_cols(blocks, O_FL, O_GQ) + _shard_cols(blocks, O_BL, O_GZ) + [pad], axis=1)
    return main, small


_W_IN_SEGS = ((O_FQ, O_FL, True, C_FQ), (O_FL, O_GQ, False, 0), (O_GQ, O_BL, True, C_GQ), (O_BL, O_GZ, False, 8),
              (O_GZ, O_END, True, C_GZ))


def _unpermute_cols(main, small, a, b):
    out = []
    for s0, s1, is_main, t0 in _W_IN_SEGS:
        lo, hi = max(a, s0), min(b, s1)
        if lo < hi:
            out.append((main if is_main else small)[:, t0 + lo - s0:t0 + hi - s0])
    return jnp.concatenate(out, axis=1)


def _lanes(vec, start):
    return jnp.pad(vec.astype(F32), (start, LANE - start - vec.shape[0])).reshape(1, LANE)


BIG = ("w_in", "w_branch_a", "w_branch_b", "w_out", "w_up", "w_down")
BIG_AXES = (2, 2, 1, 1, 2, 1)
SHARDED_SMALL = ("meta_tokens", "gdn_conv_w", "ffn_conv_w")
SHARDED_SMALL_AXES = (1, 2, 2)
REPL = ("norm1_g", "fox_f_bias", "fox_q_norm_g", "fox_k_norm_g", "gdn_a_log", "gdn_dt_bias", "gdn_norm_g", "norm2_g")
ORDER = ("meta_tokens", "norm1_g", "w_in", "fox_f_bias", "fox_q_norm_g", "fox_k_norm_g", "gdn_conv_w", "gdn_a_log",
         "gdn_dt_bias", "gdn_norm_g", "w_branch_a", "w_branch_b", "w_out", "norm2_g", "w_up", "ffn_conv_w", "w_down")


def _layer_fwd(h_res, wl, pad_rows, late=None):
    lp = h_res.shape[0]
    sv = {"res_in": h_res}
    (h1,) = _rowwise(_rmsnorm_fn, [(h_res, D, 0)], [wl["norm1_g"]], [(D, BF16)], "rmsnorm1")
    proj = _matmul(h1, wl["w_main"], "nn", F32, name="mm_in")
    small = _matmul(h1, wl["w_small"], "nn", F32, name="mm_in_small")
    sv.update(h1=h1, proj=proj, small=small)
    if late is not None:
        wl = dict(wl, **late(proj))

    fox_fn = functools.partial(_fox_prep_fn, pad_rows)
    qh, kh, logf = _rowwise(fox_fn, [(proj, FOX_W, C_FQ // FOX_W), (proj, FOX_W, C_FK // FOX_W), (small, LANE, 0)],
                            [wl["qg"], wl["kg"], wl["fb"]], [(FOX_W, BF16), (FOX_W, BF16), (LANE, F32)], "fox_prep")
    fsum = _cumsum_rows(logf, False, "fox_cumsum")
    frow = fsum[:, :8].T.reshape(8, 1, lp)
    o_a, lse = _fox_fwd(qh, kh, proj, fsum, frow, pad_rows, C_FV // LANE)
    y_a = _matmul(o_a, wl["w_branch_a"], "nn", F32, name="mm_branch_a")
    sv.update(qh=qh, kh=kh, fsum=fsum, frow=frow, o_a=o_a, lse=lse)

    gq, gk, gv, bg = _gdn_act(proj, small, wl["gdn_conv_w"], wl["alog"], wl["dtb"])
    grow = bg[:, 16:24].T.reshape(8, lp // CHUNK, 1, CHUNK)
    o_raw, sprev, tinv = _gdn_fwd(gq, gk, gv, bg, grow)
    (o_b,) = _rowwise(_gdn_post_fn, [(o_raw, GDN_W, 0), (proj, GDN_W, C_GZ // GDN_W)], [wl["gn"]], [(GDN_W, BF16)],
                      "gdn_post")
    y_b = _matmul(o_b, wl["w_branch_b"], "nn", F32, name="mm_branch_b")
    sv.update(gq=gq, gk=gk, gv=gv, bg=bg, grow=grow, o_raw=o_raw, sprev=sprev, tinv=tinv, o_b=o_b)

    (mixed,) = _rowwise(_merge_fn, [(proj, D, C_GATE // D), (proj, D, C_GATE // D + 1), (y_a, D, 0), (y_b, D, 0)], [],
                        [(D, BF16)], "merge")
    res_mid = _matmul(mixed, wl["w_out"], "nn", F32, add=h_res, name="mm_out")
    sv.update(y_a=y_a, y_b=y_b, mixed=mixed, res_mid=res_mid)

    (h2,) = _rowwise(_rmsnorm_fn, [(res_mid, D, 0)], [wl["norm2_g"]], [(D, BF16)], "rmsnorm2")
    up_pre = _matmul(h2, wl["w_up"], "nn", F32, name="mm_up")
    act = _ffn_act(up_pre, wl["ffn_conv_w"])
    out = _matmul(act, wl["w_down"], "nn", F32, add=res_mid, name="mm_down")
    sv.update(h2=h2, up_pre=up_pre, act=act)
    return out, sv, wl


def _layer_bwd(dres, wl, sv, pad_rows, after_ffn=None):
    lp = dres.shape[0]
    gw = {}
    gw["w_down"] = _matmul(sv["act"], dres, "tn", F32, name="mm_dw_down")
    dact = _matmul(dres, wl["w_down"], "nt", BF16, name="mm_dact")
    dup, gw["ffn_conv_w"] = _ffn_act_bwd(sv["up_pre"], wl["ffn_conv_w"], dact)
    dup_pre = _conv_bwd_x(dup, wl["ffn_conv_w"], pad_rows, D_FF, "ffn_conv_bwd")
    gw["w_up"] = _matmul(sv["h2"], dup_pre, "tn", F32, name="mm_dw_up")
    dh2 = _matmul(dup_pre, wl["w_up"], "nt", F32, name="mm_dh2")
    (dmid,), (gw["norm2_g"],) = _rowwise_bwd(_rmsnorm_fn, [(sv["res_mid"], D, 0)], [wl["norm2_g"]], [(dh2, D, 0)],
                                             "rmsnorm2_bwd", pad_rows, [F32], adds=[(dres, D, 0)])
    if after_ffn is not None:
        wl = dict(wl, w_out=wl["w_out"] + after_ffn(gw))
    gw["w_out"] = _matmul(sv["mixed"], dmid, "tn", F32, name="mm_dw_out")
    dmixed = _matmul(dmid, wl["w_out"], "nt", F32, name="mm_dmixed")
    proj, small = sv["proj"], sv["small"]
    (dg0, dg1, dya, dyb), _ = _rowwise_bwd(
        _merge_fn, [(proj, D, C_GATE // D), (proj, D, C_GATE // D + 1), (sv["y_a"], D, 0), (sv["y_b"], D, 0)], [],
        [(dmixed, D, 0)], "merge_bwd", pad_rows, [BF16, BF16, BF16, BF16])
    gw["w_branch_a"] = _matmul(sv["o_a"], dya, "tn", F32, name="mm_dw_a")
    do_a = (_matmul(dya, wl["w_branch_a"], "nt", F32, name="mm_do_a") * LN2).astype(BF16)
    gw["w_branch_b"] = _matmul(sv["o_b"], dyb, "tn", F32, name="mm_dw_b")
    do_b = _matmul(dyb, wl["w_branch_b"], "nt", F32, name="mm_do_b")

    (do_raw, dgz), (gw["gn"],) = _rowwise_bwd(_gdn_post_fn, [(sv["o_raw"], GDN_W, 0), (proj, GDN_W, C_GZ // GDN_W)],
                                              [wl["gn"]], [(do_b, GDN_W, 0)], "gdn_post_bwd", pad_rows, [F32, BF16])
    dgq, dgk, dgv, dbg, dgrow = _gdn_bwd(sv["gq"], sv["gk"], sv["gv"], sv["bg"], sv["grow"], sv["sprev"],
                                         sv["tinv"], do_raw)
    dbg = dbg + jnp.pad(dgrow.reshape(8, lp).T, ((0, 0), (16, LANE - 24)))
    dconv, dsmall_g, gw["alog"], gw["dtb"], gw["gdn_conv_w"] = _gdn_act_bwd(
        proj, small, wl["gdn_conv_w"], wl["alog"], wl["dtb"], dgq, dgk, dgv, dbg)
    dqkv = _conv_bwd_x(dconv, wl["gdn_conv_w"], pad_rows, GDN_W, "gdn_conv_bwd")

    (delta,) = _rowwise(_fox_delta_fn, [(sv["o_a"], FOX_W, 0), (do_a, FOX_W, 0)], [], [(LANE, F32)], "fox_delta")
    dqh, dkh, dvh, dfq, dfk = _fox_bwd(sv["qh"], sv["kh"], proj, sv["fsum"], sv["frow"], do_a, sv["lse"], delta,
                                       pad_rows, C_FV // LANE)
    df8 = dfq.reshape(lp, FOX_W // LANE, LANE)[:, :, :2].reshape(lp, 8) + dfk.reshape(8, lp).T
    dlogf = _cumsum_rows(jnp.pad(df8, ((0, 0), (0, LANE - 8))), True, "fox_cumsum_bwd")
    fox_fn = functools.partial(_fox_prep_fn, pad_rows)
    (dfq_p, dfk_p, dsmall_f), (gw["qg"], gw["kg"], gw["fb"]) = _rowwise_bwd(
        fox_fn, [(proj, FOX_W, C_FQ // FOX_W), (proj, FOX_W, C_FK // FOX_W), (small, LANE, 0)],
        [wl["qg"], wl["kg"], wl["fb"]], [(dqh, FOX_W, 0), (dkh, FOX_W, 0), (dlogf, LANE, 0)],
        "fox_prep_bwd", pad_rows, [BF16, BF16, F32], adds=[None, None, (dsmall_g, LANE, 0)])

    dproj = jnp.concatenate([dqkv, dgz, dg0, dg1, dfq_p, dfk_p, dvh], axis=1)
    gw["w_main"] = _matmul(sv["h1"], dproj, "tn", F32, name="mm_dw_main")
    gw["w_small"] = _matmul(sv["h1"], dsmall_f, "tn", F32, name="mm_dw_small")
    dh1 = _matmul(dproj, wl["w_main"], "nt", F32, name="mm_dh1")
    dh1 = _matmul(dsmall_f, wl["w_small"], "nt", F32, add=dh1, name="mm_dh1_small")
    (din,), (gw["norm1_g"],) = _rowwise_bwd(_rmsnorm_fn, [(sv["res_in"], D, 0)], [wl["norm1_g"]], [(dh1, D, 0)],
                                            "rmsnorm1_bwd", pad_rows, [F32], adds=[(dmid, D, 0)])
    return din, gw


def kernel(x, meta_tokens, norm1_g, w_in, fox_f_bias, fox_q_norm_g, fox_k_norm_g, gdn_conv_w, gdn_a_log, gdn_dt_bias, gdn_norm_g, w_branch_a, w_branch_b, w_out, norm2_g, w_up, ffn_conv_w, w_down, loss_target, m_meta_tokens, m_norm1_g, m_w_in, m_fox_f_bias, m_fox_q_norm_g, m_fox_k_norm_g, m_gdn_conv_w, m_gdn_a_log, m_gdn_dt_bias, m_gdn_norm_g, m_w_branch_a, m_w_branch_b, m_w_out, m_norm2_g, m_w_up, m_ffn_conv_w, m_w_down, v_meta_tokens, v_norm1_g, v_w_in, v_fox_f_bias, v_fox_q_norm_g, v_fox_k_norm_g, v_gdn_conv_w, v_gdn_a_log, v_gdn_dt_bias, v_gdn_norm_g, v_w_branch_a, v_w_branch_b, v_w_out, v_norm2_g, v_w_up, v_ffn_conv_w, v_w_down):
    w = dict(meta_tokens=meta_tokens, norm1_g=norm1_g, w_in=w_in, fox_f_bias=fox_f_bias, fox_q_norm_g=fox_q_norm_g,
             fox_k_norm_g=fox_k_norm_g, gdn_conv_w=gdn_conv_w, gdn_a_log=gdn_a_log, gdn_dt_bias=gdn_dt_bias,
             gdn_norm_g=gdn_norm_g, w_branch_a=w_branch_a, w_branch_b=w_branch_b, w_out=w_out, norm2_g=norm2_g,
             w_up=w_up, ffn_conv_w=ffn_conv_w, w_down=w_down)
    mom = dict(meta_tokens=m_meta_tokens, norm1_g=m_norm1_g, w_in=m_w_in, fox_f_bias=m_fox_f_bias,
               fox_q_norm_g=m_fox_q_norm_g, fox_k_norm_g=m_fox_k_norm_g, gdn_conv_w=m_gdn_conv_w,
               gdn_a_log=m_gdn_a_log, gdn_dt_bias=m_gdn_dt_bias, gdn_norm_g=m_gdn_norm_g, w_branch_a=m_w_branch_a,
               w_branch_b=m_w_branch_b, w_out=m_w_out, norm2_g=m_norm2_g, w_up=m_w_up, ffn_conv_w=m_ffn_conv_w,
               w_down=m_w_down)
    var = dict(meta_tokens=v_meta_tokens, norm1_g=v_norm1_g, w_in=v_w_in, fox_f_bias=v_fox_f_bias,
               fox_q_norm_g=v_fox_q_norm_g, fox_k_norm_g=v_fox_k_norm_g, gdn_conv_w=v_gdn_conv_w,
               gdn_a_log=v_gdn_a_log, gdn_dt_bias=v_gdn_dt_bias, gdn_norm_g=v_gdn_norm_g, w_branch_a=v_w_branch_a,
               w_branch_b=v_w_branch_b, w_out=v_w_out, norm2_g=v_norm2_g, w_up=v_w_up, ffn_conv_w=v_ffn_conv_w,
               w_down=v_w_down)
    depth = norm1_g.shape[0]
    seq = x.shape[1]
    l_tok = N_META + seq
    lp = -(-l_tok // LANE) * LANE
    pad_rows = lp - l_tok
    row_start = pad_rows + N_META

    me = 4 * lax.axis_index("x") + 2 * lax.axis_index("y") + lax.axis_index("c")
    rest = tuple(n for n in BIG if n != "w_in")
    got_in = _all_gather(w["w_in"][0].astype(BF16), "gather_w_in")
    (rest0, later), got_in = lax.optimization_barrier(
        (([w[n][0].astype(BF16) for n in rest], [[w[n][l].astype(BF16) for n in BIG] for l in range(1, depth)]), got_in))
    started_0 = _exchange_start(rest0, "gather_start_0", same_block=True)
    later, _ = lax.optimization_barrier((later, started_0[4]))
    started_g = [_exchange_start(later[l - 1], "gather_start_%d" % l, same_block=True) for l in range(1, depth)]
    small_shapes = [w[n].shape for n in SHARDED_SMALL]
    gathered_s = _all_gather(_pack([w[n] for n in SHARDED_SMALL], LANE, F32, 8), "gather_small")
    full = dict(zip(SHARDED_SMALL, _unpack_gathered(gathered_s, small_shapes, SHARDED_SMALL_AXES)))

    def joined(blocks):
        out = {}
        for name, blk in blocks.items():
            if name == "w_in":
                out["w_main"], out["w_small"] = _permute_w_in([blk[d] for d in range(N_DEV)])
            else:
                out[name] = jnp.concatenate([blk[d] for d in range(N_DEV)], axis=BIG_AXES[BIG.index(name)] - 1)
        return out

    def layer_consts(l):
        return dict(
            gdn_conv_w=full["gdn_conv_w"][l], ffn_conv_w=full["ffn_conv_w"][l],
            norm1_g=norm1_g[l].reshape(1, D), norm2_g=norm2_g[l].reshape(1, D),
            qg=jnp.tile(fox_q_norm_g[l], 8).reshape(1, FOX_W), kg=jnp.tile(fox_k_norm_g[l], 8).reshape(1, FOX_W),
            fb=_lanes(fox_f_bias[l], 0), alog=_lanes(gdn_a_log[l], 16), dtb=_lanes(gdn_dt_bias[l], 16),
            gn=jnp.tile(gdn_norm_g[l], 8).reshape(1, GDN_W))

    def arrived(exchange, names, after, name):
        send_sems, recv_sems, srcs, lands, _ = exchange
        srcs, lands = _exchange_wait(send_sems, recv_sems, srcs, lands, after, name, same_block=True)
        return {n: lax.dynamic_update_index_in_dim(ld, sr, me, 0) for n, sr, ld in zip(names, srcs, lands)}

    h_res = jnp.concatenate([jnp.zeros((pad_rows, D), F32), full["meta_tokens"], x[0]], axis=0)
    for st in [started_0] + started_g:
        h_res = h_res + st[4][0:1, 0:1]
    layers, saved = [], []
    for l in range(depth):
        if l == 0:
            wl = dict(joined({"w_in": got_in}), **layer_consts(0))
            late = lambda proj: joined(arrived(started_0, rest, proj, "gather_wait_0"))
        else:
            wl = dict(joined(arrived(started_g[l - 1], BIG, h_res, "gather_wait_%d" % l)), **layer_consts(l))
            late = None
        h_res, sv, wl = _layer_fwd(h_res, wl, pad_rows, late)
        layers.append(wl)
        saved.append(sv)
    dres, loss_part = _loss_head(h_res, loss_target[0], row_start)
    loss = lax.psum(loss_part[0, 0], ("x", "y", "c"))

    def dest_block(name, g, d):
        if name == "w_in":
            s = w_in.shape[2]
            return _unpermute_cols(g["w_main"], g["w_small"], s * d, s * (d + 1)).astype(BF16)
        if BIG_AXES[BIG.index(name)] == 2:
            s = w[name].shape[2]
            return g[name][:, s * d:s * (d + 1)].astype(BF16)
        s = w[name].shape[1]
        return g[name][s * d:s * (d + 1), :].astype(BF16)

    def start(names, g, name):
        return names, _exchange_start([jnp.stack([dest_block(n, g, d) for d in range(N_DEV)]) for n in names], name)

    ffn_names = ("w_up", "w_down")
    gws = [None] * depth
    started = []

    def early(g):
        started.append((0,) + start(ffn_names, g, "scatter_start_0_ffn"))
        return started[-1][2][4][0, 0].astype(BF16)

    for l in reversed(range(depth)):
        dres, gws[l] = _layer_bwd(dres, layers[l], saved[l], pad_rows, early if l == 0 else None)
        started.append((l,) + start(tuple(n for n in BIG if l > 0 or n not in ffn_names), gws[l],
                                    "scatter_start_%d" % l))
        if l > 0:
            token = started[-1][2][4][0, 0].astype(BF16)
            layers[l - 1] = dict(layers[l - 1], w_down=layers[l - 1]["w_down"] + token)
    landed = [dict() for _ in range(depth)]
    for idx, (l, names, (send_sems, recv_sems, srcs, lands, _)) in enumerate(started):
        srcs, lands = _exchange_wait(send_sems, recv_sems, srcs, lands, dres, "scatter_wait_%d" % idx)
        for n, sr, ld in zip(names, srcs, lands):
            own = lax.dynamic_index_in_dim(sr, me, 0, keepdims=False)
            landed[l][n] = lax.dynamic_update_index_in_dim(ld, own, me, 0)
    grad_x = dres[row_start:].reshape(x.shape)

    def stack(fn):
        return jnp.stack([fn(g) for g in gws])

    part = dict(
        meta_tokens=dres[pad_rows:row_start],
        norm1_g=stack(lambda g: g["norm1_g"][0]), norm2_g=stack(lambda g: g["norm2_g"][0]),
        fox_f_bias=stack(lambda g: g["fb"][0, 0:8]),
        fox_q_norm_g=stack(lambda g: g["qg"].reshape(8, FOX_DH).sum(0)),
        fox_k_norm_g=stack(lambda g: g["kg"].reshape(8, FOX_DH).sum(0)),
        gdn_conv_w=stack(lambda g: g["gdn_conv_w"]), gdn_a_log=stack(lambda g: g["alog"][0, 16:24]),
        gdn_dt_bias=stack(lambda g: g["dtb"][0, 16:24]),
        gdn_norm_g=stack(lambda g: g["gn"].reshape(8, GDN_DH).sum(0)),
        ffn_conv_w=stack(lambda g: g["ffn_conv_w"]))

    res = {}
    for n in BIG:
        parts = jnp.stack([landed[l][n] for l in range(depth)], axis=1)
        res[n] = _sum_adamw(parts, w[n], mom[n], var[n], "adamw_" + n)

    small_names = SHARDED_SMALL + REPL
    small_axes = SHARDED_SMALL_AXES + (None,) * len(REPL)
    landed_s = _all_to_all(_pack_dest([part[n] for n in small_names], small_axes, LANE, F32, 8), "scatter_small")
    shapes_s = [w[n].shape for n in small_names]
    outs = _sum_adamw(landed_s[:, None], *[_pack([d[n] for n in small_names], LANE, F32, 8)[None] for d in (w, mom, var)],
                      "adamw_small")
    for o_idx, packed in enumerate(outs):
        for n, a in zip(small_names, _unpack(packed[0], shapes_s)):
            res.setdefault(n, [None] * 4)[o_idx] = a

    return (loss, grad_x, *[res[n][0] for n in ORDER], *[res[n][1] for n in ORDER],
            *[res[n][2] for n in ORDER], *[res[n][3] for n in ORDER])
```

```python
import functools

import jax
import jax.numpy as jnp
from jax import lax
from jax.experimental import pallas as pl
from jax.experimental.pallas import tpu as pltpu

F32, BF16 = jnp.float32, jnp.bfloat16
MESH = pl.DeviceIdType.MESH

D = 1024
N_META = 16
DEPTH = 4
EPS = 1e-6
LOG2E = 1.4426950408889634
LN2 = 0.6931471805599453
NEG = -1e30
FOX_W, FOX_DH = 512, 64
GDN_W, GDN_DH, GDN_H = 1024, 128, 8
CHUNK = 64
D_FF = 2816
N_DEV = 8
ADAM_LR, ADAM_B1, ADAM_B2, ADAM_EPS, ADAM_WD, ADAM_STEP = 0.001, 0.9, 0.999, 1e-08, 0.01, 10

VMEM_LIMIT_BYTES = 48 * 1024 * 1024
MATMUL_VMEM_BUDGET = 36 * 1024 * 1024
ROW_TILE = 128
ROW_TILES_WIDE = (640, 512, 256, 128)
ROW_TILES = (320, 256, 128)
LANE = 128

C_GQ, C_GK, C_GV, C_GZ, C_GATE, C_FQ, C_FK, C_FV = 0, 1024, 2048, 3072, 4096, 6144, 6656, 7168
W_MAIN = 7680
O_FQ, O_FK, O_FV, O_FL, O_GQ, O_GK, O_GV, O_BL, O_AL, O_GZ, O_GATE, O_END = (
    0, 512, 1024, 1536, 1544, 2568, 3592, 4616, 4624, 4632, 5656, 7704)


def _pick(n, cands):
    for c in cands:
        if n % c == 0:
            return c
    return n


def _call(body, *, name, out_shape, in_specs, out_specs, grid=(), scratch=(), sem=None):
    kw = dict(vmem_limit_bytes=VMEM_LIMIT_BYTES)
    if sem is not None:
        kw["dimension_semantics"] = sem
    return pl.pallas_call(body, name=name, out_shape=out_shape, grid=grid, in_specs=in_specs,
                          out_specs=out_specs, scratch_shapes=list(scratch),
                          compiler_params=pltpu.CompilerParams(**kw))


_DIMS = {"nn": (((1,), (0,)), ((), ())), "nt": (((1,), (1,)), ((), ())), "tn": (((0,), (0,)), ((), ()))}


_DIMS_BATCHED = {"nn": (((2,), (1,)), ((0,), (0,))), "nt": (((2,), (2,)), ((0,), (0,))),
                 "tn": (((1,), (1,)), ((0,), (0,)))}


def _dot(a, b, mode, prec=None):
    dims = _DIMS[mode] if a.ndim == 2 else _DIMS_BATCHED[mode]
    return lax.dot_general(a, b, dims, precision=prec, preferred_element_type=F32)


def _mm_grads(f, mode, a, b, g):
    if mode == "nn":
        return f(g, b, "nt"), f(a, g, "tn")
    if mode == "nt":
        return f(g, b, "nn"), f(g, a, "tn")
    return f(b, g, "nt"), f(a, g, "nn")


@functools.partial(jax.custom_vjp, nondiff_argnums=(2,))
def _mmb(a, b, mode):
    return _dot(a.astype(BF16), b.astype(BF16), mode)


def _mmb_fwd(a, b, mode):
    return _mmb(a, b, mode), (a, b)


def _mmb_bwd(mode, res, g):
    return _mm_grads(_mmb, mode, res[0], res[1], g)


_mmb.defvjp(_mmb_fwd, _mmb_bwd)


def _split(a):
    hi = a.astype(BF16)
    return hi, (a - hi.astype(F32)).astype(BF16)


@functools.partial(jax.custom_vjp, nondiff_argnums=(2,))
def _mmh(a, b, mode):
    ah, al = _split(a)
    bh, bl = _split(b)
    return _dot(ah, bh, mode) + (_dot(ah, bl, mode) + _dot(al, bh, mode))


def _mmh_fwd(a, b, mode):
    return _mmh(a, b, mode), (a, b)


def _mmh_bwd(mode, res, g):
    return _mm_grads(_mmh, mode, res[0], res[1], g)


_mmh.defvjp(_mmh_fwd, _mmh_bwd)


def _dot_sel(sel, x, mode):
    s = sel.astype(BF16)
    x1 = x.astype(BF16)
    x2, x3 = _split(x - x1.astype(F32))
    return _dot(s, x1, mode) + (_dot(s, x2, mode) + _dot(s, x3, mode))


@jax.custom_vjp
def _mms(sel, x):
    return _dot_sel(sel, x, "nn")


def _mms_fwd(sel, x):
    return _dot_sel(sel, x, "nn"), sel


def _mms_bwd(sel, g):
    return jnp.zeros_like(sel), _dot_sel(sel, g, "tn")


_mms.defvjp(_mms_fwd, _mms_bwd)


def _softplus(z):
    return jnp.maximum(z, 0.0) + jnp.log(1.0 + jnp.exp(-jnp.abs(z)))


def _log_sigmoid(z):
    return jnp.minimum(z, 0.0) - jnp.log(1.0 + jnp.exp(-jnp.abs(z)))


def _silu(z):
    return z * jax.nn.sigmoid(z)


def _iota(shape, dim):
    return lax.broadcasted_iota(jnp.int32, shape, dim)


def _inv_unit_lower_raw(n):
    c = n.shape[-1]
    ri, ci = _iota((c, c), 0), _iota((c, c), 1)
    eye = (ri == ci).astype(F32)
    dmask = (ri // 16) == (ci // 16)
    dpart = jnp.where(dmask, n, 0.0)
    lpart = n - dpart
    x = -dpart
    p = eye + x
    for _ in range(3):
        x = _mmh(x, x, "nn")
        p = p + _mmh(p, x, "nn")
    m = -_mmh(p, lpart, "nn")
    q = eye + m
    steps = 1
    while (1 << steps) < c // 16:
        steps += 1
    for _ in range(steps - 1):
        m = _mmh(m, m, "nn")
        q = q + _mmh(q, m, "nn")
    return _mmh(q, p, "nn")


@jax.custom_vjp
def _inv_given(n, t):
    return t


def _inv_given_fwd(n, t):
    return t, t


def _inv_given_bwd(t, g):
    c = t.shape[-1]
    strict = _iota((c, c), 0) > _iota((c, c), 1)
    d = -_mmh(_mmh(t, g, "tn"), t, "nt")
    return jnp.where(strict, d, 0.0), jnp.zeros_like(t)


_inv_given.defvjp(_inv_given_fwd, _inv_given_bwd)


def _shift_down(x, halo, s):
    if s == 0:
        return x
    xs = pltpu.roll(x, s, 0)
    hs = pltpu.roll(halo, s, 0)
    top = jnp.where(_iota(hs.shape, 0) < s, hs, xs[0:8])
    return jnp.concatenate([top, xs[8:]], axis=0)


def _shift_up(x, halo, s):
    if s == 0:
        return x
    tm = x.shape[0]
    xs = pltpu.roll(x, tm - s, 0)
    hs = pltpu.roll(halo, 8 - s, 0)
    bot = jnp.where(_iota(hs.shape, 0) >= 8 - s, hs, xs[tm - 8:])
    return jnp.concatenate([xs[:tm - 8], bot], axis=0)


def _causal_conv(x, halo, w):
    kk = w.shape[0]
    y = x * w[kk - 1:kk, :]
    for k in range(kk - 1):
        y = y + _shift_down(x, halo, kk - 1 - k) * w[k:k + 1, :]
    return y


def _head_scale(x, width, fn):
    outs = []
    for h in range(x.shape[1] // width):
        seg = x[:, h * width:(h + 1) * width]
        outs.append(seg * fn(jnp.sum(seg * seg, axis=1, keepdims=True)))
    return jnp.concatenate(outs, axis=1)


def _matmul(a, b, mode, out_dtype, add=None, name="mm"):
    if mode == "nn":
        (m, k), n = a.shape, b.shape[1]
    elif mode == "nt":
        (m, k), n = a.shape, b.shape[0]
    else:
        (k, m), n = a.shape, b.shape[1]
    tm = _pick(m, (1408, 1024, 512, 256, 128) if mode == "tn" else (640, 512, 256, 128))
    tn = _pick(n, (1536, 1408, 1024, 768, 512, 256, 128))
    sa, sb = a.dtype.itemsize, b.dtype.itemsize
    fixed = tm * tn * 4 * (3 + (2 if add is not None else 0))
    tk = 128
    for cand in (k, 2816, 2560, 1664, 1536, 1280, 1024, 832, 768, 640, 512, 256, 128):
        if mode != "tn" and cand != k and cand % LANE:
            continue
        if k % cand == 0 and fixed + 2 * cand * (tm * sa + tn * sb) <= MATMUL_VMEM_BUDGET:
            tk = cand
            break
    nk = k // tk
    a_spec = {"nn": pl.BlockSpec((tm, tk), lambda i, j, kk: (i, kk)),
              "nt": pl.BlockSpec((tm, tk), lambda i, j, kk: (i, kk)),
              "tn": pl.BlockSpec((tk, tm), lambda i, j, kk: (kk, i))}[mode]
    b_spec = {"nn": pl.BlockSpec((tk, tn), lambda i, j, kk: (kk, j)),
              "nt": pl.BlockSpec((tn, tk), lambda i, j, kk: (j, kk)),
              "tn": pl.BlockSpec((tk, tn), lambda i, j, kk: (kk, j))}[mode]
    o_spec = pl.BlockSpec((tm, tn), lambda i, j, kk: (i, j))
    has_add = add is not None

    def body(*refs):
        a_ref, b_ref = refs[0], refs[1]
        add_ref = refs[2] if has_add else None
        o_ref = refs[3] if has_add else refs[2]
        part = _dot(a_ref[...].astype(BF16), b_ref[...].astype(BF16), mode)
        if nk == 1:
            if has_add:
                part = part + add_ref[...].astype(F32)
            o_ref[...] = part.astype(out_dtype)
        else:
            acc = refs[-1]
            kk = pl.program_id(2)

            @pl.when(kk == 0)
            def _():
                acc[...] = part

            @pl.when(kk > 0)
            def _():
                acc[...] += part

            @pl.when(kk == nk - 1)
            def _():
                r = acc[...]
                if has_add:
                    r = r + add_ref[...].astype(F32)
                o_ref[...] = r.astype(out_dtype)

    ins = [a, b] + ([add] if has_add else [])
    specs = [a_spec, b_spec] + ([o_spec] if has_add else [])
    return _call(body, name=name, out_shape=jax.ShapeDtypeStruct((m, n), out_dtype), grid=(m // tm, n // tn, nk),
                 in_specs=specs, out_specs=o_spec,
                 scratch=[pltpu.VMEM((tm, tn), F32)] if nk > 1 else [],
                 sem=("parallel", "parallel", "arbitrary"))(*ins)


def _row_spec(width, colblock, tm):
    return pl.BlockSpec((tm, width), lambda i, cb=colblock: (i, cb))


def _full_spec(arr):
    nd = arr.ndim
    return pl.BlockSpec(arr.shape, lambda i, nd=nd: (0,) * nd)


def _rowwise(fn, rows, params, outs, name):
    lp = rows[0][0].shape[0]
    tm = _pick(lp, ROW_TILES_WIDE)
    nr, npar = len(rows), len(params)

    def body(*refs):
        row0 = pl.program_id(0) * tm
        vals = [r[...].astype(F32) for r in refs[:nr + npar]]
        res = fn(*vals, row0)
        for o_ref, r in zip(refs[nr + npar:], res):
            o_ref[...] = r.astype(o_ref.dtype)

    out = _call(body, name=name, grid=(lp // tm,),
                out_shape=[jax.ShapeDtypeStruct((lp, w), dt) for w, dt in outs],
                in_specs=[_row_spec(w, cb, tm) for _, w, cb in rows] + [_full_spec(p) for p in params],
                out_specs=[_row_spec(w, 0, tm) for w, _ in outs], sem=("parallel",))(
                    *[r[0] for r in rows], *params)
    return out


def _rowwise_bwd(fn, rows, params, cts, name, pad_rows, grad_dtypes, adds=None):
    lp = rows[0][0].shape[0]
    tm = _pick(lp, ROW_TILES)
    nr, npar, nct = len(rows), len(params), len(cts)
    adds = adds or [None] * nr
    add_list = [a for a in adds if a is not None]
    nadd = len(add_list)

    def body(*refs):
        i = pl.program_id(0)
        row0 = i * tm
        vals = [r[...].astype(F32) for r in refs[:nr + npar]]
        ct_vals = tuple(r[...].astype(F32) for r in refs[nr + npar:nr + npar + nct])
        add_refs = list(refs[nr + npar + nct:nr + npar + nct + nadd])
        outs = refs[nr + npar + nct + nadd:]
        _, vjp = jax.vjp(lambda *args: tuple(fn(*args, row0)), *vals)
        grads = vjp(ct_vals)
        valid = (row0 + _iota((tm, 1), 0)) >= pad_rows
        for idx in range(nr):
            g = jnp.where(valid, grads[idx], 0.0)
            if adds[idx] is not None:
                g = g + add_refs.pop(0)[...].astype(F32)
            outs[idx][...] = g.astype(outs[idx].dtype)
        for idx in range(npar):
            o_ref = outs[nr + idx]

            @pl.when(i == 0)
            def _(o_ref=o_ref):
                o_ref[...] = jnp.zeros_like(o_ref)

            o_ref[...] += grads[nr + idx]

    out = _call(body, name=name, grid=(lp // tm,),
                out_shape=[jax.ShapeDtypeStruct((lp, w), dt) for (_, w, _), dt in zip(rows, grad_dtypes)]
                + [jax.ShapeDtypeStruct(p.shape, F32) for p in params],
                in_specs=[_row_spec(w, cb, tm) for _, w, cb in rows] + [_full_spec(p) for p in params]
                + [_row_spec(w, cb, tm) for _, w, cb in cts] + [_row_spec(w, cb, tm) for _, w, cb in add_list],
                out_specs=[_row_spec(w, 0, tm) for _, w, _ in rows] + [_full_spec(p) for p in params],
                sem=("arbitrary",))(*[r[0] for r in rows], *params, *[c[0] for c in cts], *[a[0] for a in add_list])
    return out[:nr], out[nr:]


def _rmsnorm_fn(x, g, row0):
    return (x * lax.rsqrt(jnp.mean(x * x, axis=1, keepdims=True) + EPS) * g,)


def _fox_prep_fn(pad_rows, fq, fk, small, qg, kg, fb, row0):
    ri, ci = _iota((FOX_W, FOX_W), 0), _iota((FOX_W, FOX_W), 1)
    bd = jnp.where((ri // FOX_DH) == (ci // FOX_DH), 1.0 / FOX_DH, 0.0)

    def hn(x, g):
        return x * lax.rsqrt(_mmh(x * x, bd, "nn") + EPS) * g

    tm = small.shape[0]
    keep = (_iota((tm, LANE), 1) < 8) & ((row0 + _iota((tm, LANE), 0)) >= pad_rows)
    logf = jnp.where(keep, _log_sigmoid(small + fb) * LOG2E, 0.0)
    return hn(fq, qg) * (FOX_DH ** -0.5 * LOG2E), hn(fk, kg), logf


def _gdn_act_fn(cq, ck, cv, small, alog, dtb):
    tm = small.shape[0]
    q = _head_scale(_silu(cq), GDN_DH, lambda s: lax.rsqrt(s + EPS) * (GDN_DH ** -0.5))
    k = _head_scale(_silu(ck), GDN_DH, lambda s: lax.rsqrt(s + EPS))
    v = _silu(cv)
    lane = _iota((tm, LANE), 1)
    beta = jnp.where((lane >= 8) & (lane < 16), jax.nn.sigmoid(small), 0.0)
    g = jnp.where((lane >= 16) & (lane < 24), -jnp.exp(alog) * _softplus(small + dtb), 0.0)
    ri, ci = _iota((tm, tm), 0), _iota((tm, tm), 1)
    tri = jnp.where(((ri // CHUNK) == (ci // CHUNK)) & (ci <= ri), 1.0, 0.0)
    return q, k, v, beta + _mms(tri, g)


def _gdn_post_fn(o, gz, gn, row0):
    return (_head_scale(o, GDN_DH, lambda s: lax.rsqrt(s * (1.0 / GDN_DH) + EPS)) * gn * _silu(gz),)


def _merge_fn(g0, g1, ya, yb, row0):
    return (jax.nn.sigmoid(g0) * ya + jax.nn.sigmoid(g1) * yb,)


def _cumsum_rows(x, reverse, name):
    lp, w = x.shape
    tm = _pick(lp, (640, 512, 256, 128))
    nt = lp // tm

    def body(x_ref, o_ref, carry):
        i = pl.program_id(0)

        @pl.when(i == 0)
        def _():
            carry[...] = jnp.zeros_like(carry)

        ri, ci = _iota((tm, tm), 0), _iota((tm, tm), 1)
        tri = jnp.where((ci >= ri) if reverse else (ci <= ri), 1.0, 0.0)
        blk = x_ref[...]
        o_ref[...] = _dot_sel(tri, blk, "nn") + carry[0:1, :]
        carry[...] = carry[...] + jnp.sum(blk, axis=0, keepdims=True)

    idx = (lambda i: (nt - 1 - i, 0)) if reverse else (lambda i: (i, 0))
    return _call(body, name=name, grid=(nt,), out_shape=jax.ShapeDtypeStruct((lp, w), F32),
                 in_specs=[pl.BlockSpec((tm, w), idx)], out_specs=pl.BlockSpec((tm, w), idx),
                 scratch=[pltpu.VMEM((8, w), F32)], sem=("arbitrary",))(x)


def _fox_scores(q, k, fq, fk, hh, qpos0, kpos0, pad_rows, masked):
    tq, tk = q.shape[0], k.shape[0]
    lane = _iota(q.shape, 1)
    sel = (lane < FOX_DH) if hh == 0 else (lane >= FOX_DH)
    s = _dot(jnp.where(sel, q, jnp.zeros_like(q)), k, "nt") + fq - fk
    if not masked:
        return s, None, sel
    qpos = qpos0 + _iota((tq, tk), 0)
    kpos = kpos0 + _iota((tq, tk), 1)
    mask = (kpos <= qpos) & (kpos >= pad_rows)
    return jnp.where(mask, s, NEG), mask, sel


def _probs(s, mask, shift):
    p = jnp.exp2(s - shift)
    return p if mask is None else jnp.where(mask, p, 0.0)


def _both_variants(needs_mask, fn):
    @pl.when(needs_mask)
    def _():
        fn(True)

    @pl.when(jnp.logical_not(needs_mask))
    def _():
        fn(False)


def _lane_col(blk, lane_idx):
    return jnp.sum(jnp.where(_iota(blk.shape, 1) == lane_idx, blk, 0.0), axis=1, keepdims=True)


def _to_lanes(cols, width=LANE):
    lane = _iota((cols[0].shape[0], width), 1)
    out = jnp.zeros((cols[0].shape[0], width), F32)
    for idx, c in enumerate(cols):
        out = jnp.where(lane == idx, c, out)
    return out


def _fox_fwd(q, k, v, fsum, frow, pad_rows, v_col):
    lp = q.shape[0]
    t = _pick(lp, (640, 512, 256, 128))
    n = lp // t

    def body(q_ref, k_ref, v_ref, f_ref, fk_ref, o_ref, lse_ref, acc, m_s, l_s, fq_s):
        pr, i, j = pl.program_id(0), pl.program_id(1), pl.program_id(2)

        @pl.when(j == 0)
        def _():
            acc[...] = jnp.zeros_like(acc)
            m_s[...] = jnp.full_like(m_s, NEG)
            l_s[...] = jnp.zeros_like(l_s)
            for hh in range(2):
                fq_s[hh] = _lane_col(f_ref[...], 2 * pr + hh)

        def step(masked):
            for hh in range(2):
                s, mask, _ = _fox_scores(q_ref[...], k_ref[...], fq_s[hh], fk_ref[hh], hh, i * t, j * t, pad_rows,
                                         masked)
                m_prev = m_s[hh]
                m_new = jnp.maximum(m_prev, jnp.max(s, axis=1, keepdims=True))
                p = _probs(s, mask, m_new)
                alpha = jnp.exp2(m_prev - m_new)
                l_s[hh] = alpha * l_s[hh] + jnp.sum(p, axis=1, keepdims=True)
                acc[hh] = alpha * acc[hh] + _dot(p.astype(BF16), v_ref[...].astype(BF16), "nn")
                m_s[hh] = m_new

        @pl.when(j <= i)
        def _():
            _both_variants((j == i) | (j == 0), step)

        @pl.when(j == i)
        def _():
            outs, lses = [], []
            for hh in range(2):
                l = l_s[hh]
                ok = l > 0.0
                outs.append(acc[hh] * jnp.where(ok, 1.0 / jnp.where(ok, l, 1.0), 0.0))
                lses.append(jnp.where(ok, m_s[hh] + jnp.log2(jnp.where(ok, l, 1.0)), 0.0))
            lane = _iota((t, LANE), 1)
            o_ref[...] = jnp.where(lane < FOX_DH, outs[0], outs[1]).astype(o_ref.dtype)
            lse_ref[...] = _to_lanes(lses)

    qspec = pl.BlockSpec((t, LANE), lambda p, i, j: (i, p))
    kspec = pl.BlockSpec((t, LANE), lambda p, i, j: (jnp.minimum(j, i), p))
    vspec = pl.BlockSpec((t, LANE), lambda p, i, j: (jnp.minimum(j, i), v_col + p))
    fspec = pl.BlockSpec((t, LANE), lambda p, i, j: (i, 0))
    rspec = pl.BlockSpec((2, 1, t), lambda p, i, j: (p, 0, jnp.minimum(j, i)))
    return _call(body, name="fox_fwd", grid=(FOX_W // LANE, n, n),
                 out_shape=[jax.ShapeDtypeStruct((lp, FOX_W), BF16), jax.ShapeDtypeStruct((lp, FOX_W), F32)],
                 in_specs=[qspec, kspec, vspec, fspec, rspec], out_specs=[qspec, qspec],
                 scratch=[pltpu.VMEM((2, t, LANE), F32), pltpu.VMEM((2, t, 1), F32), pltpu.VMEM((2, t, 1), F32),
                          pltpu.VMEM((2, t, 1), F32)],
                 sem=("parallel", "parallel", "arbitrary"))(q, k, v, fsum, frow)


def _fox_delta_fn(o, do, row0):
    ri, ci = _iota((FOX_W, LANE), 0), _iota((FOX_W, LANE), 1)
    sel = jnp.where((ri // FOX_DH) == ci, 1.0, 0.0).astype(BF16)
    x = o * do
    x1 = x.astype(BF16)
    x2, x3 = _split(x - x1.astype(F32))
    return (_dot(x1, sel, "nn") + (_dot(x2, sel, "nn") + _dot(x3, sel, "nn")),)


def _fox_bwd(q, k, v, fsum, frow, do, lse, delta, pad_rows, v_col):
    lp = q.shape[0]
    t = _pick(lp, (640, 512, 256, 128))
    n = lp // t

    def body(q_ref, k_ref, v_ref, f_ref, fk_ref, do_ref, lse_ref, dl_ref,
             dq_ref, dk_ref, dv_ref, dfq_ref, dfk_ref, dka, dva, dfa):
        pr, j, i = pl.program_id(0), pl.program_id(1), pl.program_id(2)
        lane = _iota((t, LANE), 1)

        @pl.when((j == 0) & (i == 0))
        def _():
            dq_ref[...] = jnp.zeros_like(dq_ref)
            dfq_ref[...] = jnp.zeros_like(dfq_ref)

        @pl.when(i == 0)
        def _():
            dka[...] = jnp.zeros_like(dka)
            dva[...] = jnp.zeros_like(dva)
            dfa[...] = jnp.zeros_like(dfa)

        def step(masked):
            rows = pl.ds(pl.multiple_of(i * t, t), t)
            dq_add = jnp.zeros((t, LANE), F32)
            rowsums = []
            for hh in range(2):
                fq = _lane_col(f_ref[...], 2 * pr + hh)
                s, mask, sel = _fox_scores(q_ref[...], k_ref[...], fq, fk_ref[hh], hh, i * t, j * t, pad_rows, masked)
                p = _probs(s, mask, _lane_col(lse_ref[...], hh))
                dop = jnp.where(sel, do_ref[...], jnp.zeros_like(do_ref[...]))
                ds = p * (_dot(dop, v_ref[...].astype(BF16), "nt") - _lane_col(dl_ref[...], 2 * pr + hh))
                dsb = ds.astype(BF16)
                dva[hh] += _dot(p.astype(BF16), do_ref[...], "tn")
                dka[hh] += _dot(dsb, q_ref[...], "tn")
                dfa[hh] -= jnp.sum(ds, axis=0, keepdims=True)
                dq_add = dq_add + _dot(dsb, jnp.where(sel, k_ref[...], jnp.zeros_like(k_ref[...])), "nn")
                rowsums.append(jnp.sum(ds, axis=1, keepdims=True))
            dq_ref[rows, :] += dq_add
            dfq_ref[rows, :] += _to_lanes(rowsums)

        @pl.when(i >= j)
        def _():
            _both_variants((j == i) | (j == 0), step)

        @pl.when(i == n - 1)
        def _():
            dk_ref[...] = jnp.where(lane < FOX_DH, dka[0], dka[1])
            dv_ref[...] = (jnp.where(lane < FOX_DH, dva[0], dva[1]) * LOG2E).astype(dv_ref.dtype)
            dfk_ref[...] = dfa[...]

    qspec = pl.BlockSpec((t, LANE), lambda p, j, i: (jnp.maximum(i, j), p))
    f_q = pl.BlockSpec((t, LANE), lambda p, j, i: (jnp.maximum(i, j), 0))
    kspec = pl.BlockSpec((t, LANE), lambda p, j, i: (j, p))
    vspec = pl.BlockSpec((t, LANE), lambda p, j, i: (j, v_col + p))
    rspec = pl.BlockSpec((2, 1, t), lambda p, j, i: (p, 0, j))
    whole = pl.BlockSpec((lp, LANE), lambda p, j, i: (0, p))
    wide = jax.ShapeDtypeStruct((lp, FOX_W), F32)
    return _call(body, name="fox_bwd", grid=(FOX_W // LANE, n, n),
                 out_shape=[wide, wide, jax.ShapeDtypeStruct((lp, FOX_W), BF16), wide,
                            jax.ShapeDtypeStruct((8, 1, lp), F32)],
                 in_specs=[qspec, kspec, vspec, f_q, rspec, qspec, qspec, f_q],
                 out_specs=[whole, kspec, kspec, whole, rspec],
                 scratch=[pltpu.VMEM((2, t, LANE), F32), pltpu.VMEM((2, t, LANE), F32), pltpu.VMEM((2, 1, t), F32)],
                 sem=("parallel", "arbitrary", "arbitrary"))(q, k, v, fsum, frow, do, lse, delta)


def _gdn_chunk(q, k, v, beta, gcol, grow, s, inv):
    c = q.shape[-2]
    ri, ci = _iota((c, c), 0), _iota((c, c), 1)
    dec = jnp.exp(jnp.where(ri >= ci, gcol - grow, NEG))
    dec_strict = jnp.where(ri > ci, dec, 0.0)
    eg = jnp.exp(gcol)
    kb = k * beta
    t = inv(_mmb(kb, k, "nt") * dec_strict)
    u_hat = _mmh(t, v * beta, "nn")
    w = _mmh(t, kb * eg, "nn")
    u = u_hat - _mmb(w, s, "nn")
    o = _mmb(q * eg, s, "nn") + _mmb(_mmb(q, k, "nt") * dec, u, "nn")
    glast = jnp.sum(jnp.where(_iota((1, c), 1) == c - 1, grow, 0.0), axis=-1, keepdims=True)
    s_new = s * jnp.exp(glast) + _mmb(k * jnp.exp(glast - gcol), u, "tn")
    return o, s_new


def _gdn_specs(lp, reverse):
    n = lp // CHUNK
    pos = (lambda c: n - 1 - c) if reverse else (lambda c: c)
    wide = pl.BlockSpec((CHUNK, GDN_W), lambda c: (pos(c), 0))
    lanes = pl.BlockSpec((CHUNK, LANE), lambda c: (pos(c), 0))
    row = pl.BlockSpec((GDN_H, 1, 1, CHUNK), lambda c: (0, pos(c), 0, 0))
    st = pl.BlockSpec((GDN_H, 1, GDN_DH, GDN_DH), lambda c: (0, pos(c), 0, 0))
    return n, wide, lanes, row, st


def _heads(ref):
    return jnp.stack([ref[:, h * GDN_DH:(h + 1) * GDN_DH] for h in range(GDN_H)])


def _put_heads(ref, val):
    for h in range(GDN_H):
        ref[:, h * GDN_DH:(h + 1) * GDN_DH] = val[h]


def _head_cols(blk, lane0):
    return jnp.stack([_lane_col(blk, lane0 + h) for h in range(GDN_H)])


def _gdn_fwd(q, k, v, bg, grow):
    lp = q.shape[0]
    n, wide, lanes, row, st = _gdn_specs(lp, False)

    def body(q_ref, k_ref, v_ref, bg_ref, gr_ref, o_ref, sp_ref, t_ref, s_scr):
        @pl.when(pl.program_id(0) == 0)
        def _():
            s_scr[...] = jnp.zeros_like(s_scr)

        def inv(m):
            t = _inv_unit_lower_raw(m)
            t_ref[:, 0] = t
            return t

        s = s_scr[...]
        sp_ref[:, 0] = s
        bg_blk = bg_ref[...]
        o, s_new = _gdn_chunk(_heads(q_ref), _heads(k_ref), _heads(v_ref), _head_cols(bg_blk, 8),
                              _head_cols(bg_blk, 16), gr_ref[:, 0], s, inv)
        _put_heads(o_ref, o)
        s_scr[...] = s_new

    tri = pl.BlockSpec((GDN_H, 1, CHUNK, CHUNK), lambda c: (0, c, 0, 0))
    return _call(body, name="gdn_fwd", grid=(n,),
                 out_shape=[jax.ShapeDtypeStruct((lp, GDN_W), F32),
                            jax.ShapeDtypeStruct((GDN_H, n, GDN_DH, GDN_DH), F32),
                            jax.ShapeDtypeStruct((GDN_H, n, CHUNK, CHUNK), F32)],
                 in_specs=[wide, wide, wide, lanes, row], out_specs=[wide, st, tri],
                 scratch=[pltpu.VMEM((GDN_H, GDN_DH, GDN_DH), F32)], sem=("arbitrary",))(q, k, v, bg, grow)


def _gdn_bwd(q, k, v, bg, grow, sprev, tinv, do):
    lp = q.shape[0]
    n, wide, lanes, row, st = _gdn_specs(lp, True)

    def body(q_ref, k_ref, v_ref, bg_ref, gr_ref, sp_ref, t_ref, do_ref,
             dq_ref, dk_ref, dv_ref, dbg_ref, dgr_ref, ds_scr):
        @pl.when(pl.program_id(0) == 0)
        def _():
            ds_scr[...] = jnp.zeros_like(ds_scr)

        t_saved = t_ref[:, 0]
        fn = functools.partial(_gdn_chunk, inv=lambda m: _inv_given(m, t_saved))
        bg_blk = bg_ref[...]
        _, vjp = jax.vjp(fn, _heads(q_ref), _heads(k_ref), _heads(v_ref), _head_cols(bg_blk, 8),
                         _head_cols(bg_blk, 16), gr_ref[:, 0], sp_ref[:, 0])
        dq, dk, dv, db, dgc, dgr, ds = vjp((_heads(do_ref), ds_scr[...]))
        _put_heads(dq_ref, dq)
        _put_heads(dk_ref, dk)
        _put_heads(dv_ref, dv)
        lane = _iota((CHUNK, LANE), 1)
        dbg = jnp.zeros((CHUNK, LANE), F32)
        for h in range(GDN_H):
            dbg = jnp.where(lane == 8 + h, db[h], jnp.where(lane == 16 + h, dgc[h], dbg))
        dbg_ref[...] = dbg
        dgr_ref[:, 0] = dgr
        ds_scr[...] = ds

    wshape = jax.ShapeDtypeStruct((lp, GDN_W), F32)
    tri = pl.BlockSpec((GDN_H, 1, CHUNK, CHUNK), lambda c: (0, n - 1 - c, 0, 0))
    return _call(body, name="gdn_bwd", grid=(n,),
                 out_shape=[wshape, wshape, wshape, jax.ShapeDtypeStruct((lp, LANE), F32),
                            jax.ShapeDtypeStruct((GDN_H, n, 1, CHUNK), F32)],
                 in_specs=[wide, wide, wide, lanes, row, st, tri, wide], out_specs=[wide, wide, wide, lanes, row],
                 scratch=[pltpu.VMEM((GDN_H, GDN_DH, GDN_DH), F32)], sem=("arbitrary",))(
                     q, k, v, bg, grow, sprev, tinv, do)


def _halo_prev(width, colblock, tm):
    return pl.BlockSpec((8, width), lambda i, cb=colblock: (jnp.maximum(i * (tm // 8) - 1, 0), cb))


def _gdn_act(proj, small, conv_w, alog_row, dtb_row):
    lp = proj.shape[0]
    tm = _pick(lp, ROW_TILES)

    def body(xq, xk, xv, hq, hk, hv, wq, wk, wv, sm, al, dt, q_ref, k_ref, v_ref, bg_ref):
        first = (pl.program_id(0) > 0).astype(F32)
        cs = [_causal_conv(x[...], h[...] * first, w[...]) for x, h, w in ((xq, hq, wq), (xk, hk, wk), (xv, hv, wv))]
        q, k, v, bg = _gdn_act_fn(cs[0], cs[1], cs[2], sm[...], al[...], dt[...])
        q_ref[...], k_ref[...], v_ref[...], bg_ref[...] = q, k, v, bg

    wide = jax.ShapeDtypeStruct((lp, GDN_W), F32)
    wspec = [pl.BlockSpec((4, GDN_W), lambda i, c=c: (0, c)) for c in range(3)]
    return _call(body, name="gdn_act", grid=(lp // tm,),
                 out_shape=[wide, wide, wide, jax.ShapeDtypeStruct((lp, LANE), F32)],
                 in_specs=[_row_spec(GDN_W, c, tm) for c in range(3)] + [_halo_prev(GDN_W, c, tm) for c in range(3)]
                 + wspec + [_row_spec(LANE, 0, tm), _full_spec(alog_row), _full_spec(dtb_row)],
                 out_specs=[_row_spec(GDN_W, 0, tm)] * 3 + [_row_spec(LANE, 0, tm)], sem=("parallel",))(
                     proj, proj, proj, proj, proj, proj, conv_w, conv_w, conv_w, small, alog_row, dtb_row)


def _gdn_act_bwd(proj, small, conv_w, alog_row, dtb_row, dq, dk, dv, dbg):
    lp = proj.shape[0]
    tm = ROW_TILE

    def body(xq, xk, xv, hq, hk, hv, wq, wk, wv, sm, al, dt, dq_r, dk_r, dv_r, dbg_r,
             dc_ref, dsm_ref, dal_ref, ddt_ref, dw_ref):
        i = pl.program_id(0)
        first = (i > 0).astype(F32)
        xs = [(x[...], h[...] * first, w[...]) for x, h, w in ((xq, hq, wq), (xk, hk, wk), (xv, hv, wv))]
        cs = [_causal_conv(*t) for t in xs]
        _, vjp = jax.vjp(_gdn_act_fn, cs[0], cs[1], cs[2], sm[...], al[...], dt[...])
        dcq, dck, dcv, dsm, dal, ddt = vjp((dq_r[...], dk_r[...], dv_r[...], dbg_r[...]))
        dsm_ref[...] = dsm

        @pl.when(i == 0)
        def _():
            dal_ref[...] = jnp.zeros_like(dal_ref)
            ddt_ref[...] = jnp.zeros_like(ddt_ref)
            dw_ref[...] = jnp.zeros_like(dw_ref)

        dal_ref[...] += dal
        ddt_ref[...] += ddt
        for c, (dc, (x, h, w)) in enumerate(zip((dcq, dck, dcv), xs)):
            dc_ref[:, c * GDN_W:(c + 1) * GDN_W] = dc
            rows = [jnp.sum(_shift_down(x, h, 3 - kk) * dc, axis=0, keepdims=True) for kk in range(4)]
            dw_ref[:, c * GDN_W:(c + 1) * GDN_W] += jnp.concatenate(rows, axis=0)

    wspec = [pl.BlockSpec((4, GDN_W), lambda i, c=c: (0, c)) for c in range(3)]
    row128 = jax.ShapeDtypeStruct((1, LANE), F32)
    return _call(body, name="gdn_act_bwd", grid=(lp // tm,),
                 out_shape=[jax.ShapeDtypeStruct((lp, 3 * GDN_W), F32), jax.ShapeDtypeStruct((lp, LANE), F32),
                            row128, row128, jax.ShapeDtypeStruct((4, 3 * GDN_W), F32)],
                 in_specs=[_row_spec(GDN_W, c, tm) for c in range(3)] + [_halo_prev(GDN_W, c, tm) for c in range(3)]
                 + wspec + [_row_spec(LANE, 0, tm), _full_spec(alog_row), _full_spec(dtb_row)]
                 + [_row_spec(GDN_W, 0, tm)] * 3 + [_row_spec(LANE, 0, tm)],
                 out_specs=[_row_spec(3 * GDN_W, 0, tm), _row_spec(LANE, 0, tm),
                            _full_spec(alog_row), _full_spec(dtb_row), pl.BlockSpec((4, 3 * GDN_W), lambda i: (0, 0))],
                 sem=("arbitrary",))(proj, proj, proj, proj, proj, proj, conv_w, conv_w, conv_w, small,
                                     alog_row, dtb_row, dq, dk, dv, dbg)


def _ffn_act(up_pre, conv_w):
    lp = up_pre.shape[0]
    tm = _pick(lp, ROW_TILES)

    def body(xg, xv, hg, hv, wg, wv, a_ref):
        first = (pl.program_id(0) > 0).astype(F32)
        ug = _causal_conv(xg[...], hg[...] * first, wg[...])
        uv = _causal_conv(xv[...], hv[...] * first, wv[...])
        a_ref[...] = (_silu(ug) * uv).astype(a_ref.dtype)

    wspec = [pl.BlockSpec((3, D_FF), lambda i, c=c: (0, c)) for c in range(2)]
    return _call(body, name="ffn_act", grid=(lp // tm,), out_shape=jax.ShapeDtypeStruct((lp, D_FF), BF16),
                 in_specs=[_row_spec(D_FF, c, tm) for c in range(2)] + [_halo_prev(D_FF, c, tm) for c in range(2)] + wspec,
                 out_specs=_row_spec(D_FF, 0, tm), sem=("parallel",))(up_pre, up_pre, up_pre, up_pre, conv_w, conv_w)


def _ffn_act_bwd(up_pre, conv_w, dact, tm=ROW_TILE):
    lp = up_pre.shape[0]

    def body(xg, xv, hg, hv, wg, wv, da, du_ref, dw_ref):
        i = pl.program_id(0)
        first = (i > 0).astype(F32)
        xs = [(x[...], h[...] * first, w[...]) for x, h, w in ((xg, hg, wg), (xv, hv, wv))]
        ug, uv = [_causal_conv(*t) for t in xs]
        _, vjp = jax.vjp(lambda a, b: _silu(a) * b, ug, uv)
        dus = vjp(da[...].astype(F32))

        @pl.when(i == 0)
        def _():
            dw_ref[...] = jnp.zeros_like(dw_ref)

        for c, (du, (x, h, w)) in enumerate(zip(dus, xs)):
            du_ref[:, c * D_FF:(c + 1) * D_FF] = du
            rows = [jnp.sum(_shift_down(x, h, 2 - kk) * du, axis=0, keepdims=True) for kk in range(3)]
            dw_ref[:, c * D_FF:(c + 1) * D_FF] += jnp.concatenate(rows, axis=0)

    wspec = [pl.BlockSpec((3, D_FF), lambda i, c=c: (0, c)) for c in range(2)]
    return _call(body, name="ffn_act_bwd", grid=(lp // tm,),
                 out_shape=[jax.ShapeDtypeStruct((lp, 2 * D_FF), F32), jax.ShapeDtypeStruct((3, 2 * D_FF), F32)],
                 in_specs=[_row_spec(D_FF, c, tm) for c in range(2)] + [_halo_prev(D_FF, c, tm) for c in range(2)]
                 + wspec + [_row_spec(D_FF, 0, tm)],
                 out_specs=[_row_spec(2 * D_FF, 0, tm), pl.BlockSpec((3, 2 * D_FF), lambda i: (0, 0))],
                 sem=("arbitrary",))(up_pre, up_pre, up_pre, up_pre, conv_w, conv_w, dact)


def _conv_bwd_x(dy, w, pad_rows, width, name):
    lp, ctot = dy.shape
    tm = _pick(lp, ROW_TILES)
    nt = lp // tm
    kk = w.shape[0]

    def body(d_ref, h_ref, w_ref, o_ref):
        i = pl.program_id(0)
        last = (i < nt - 1).astype(F32)
        d, h, wv = d_ref[...], h_ref[...] * last, w_ref[...]
        y = d * wv[kk - 1:kk, :]
        for k in range(kk - 1):
            y = y + _shift_up(d, h, kk - 1 - k) * wv[k:k + 1, :]
        valid = (i * tm + _iota((tm, 1), 0)) >= pad_rows
        o_ref[...] = jnp.where(valid, y, 0.0).astype(o_ref.dtype)

    return _call(body, name=name, grid=(nt, ctot // width), out_shape=jax.ShapeDtypeStruct((lp, ctot), BF16),
                 in_specs=[pl.BlockSpec((tm, width), lambda i, c: (i, c)),
                           pl.BlockSpec((8, width), lambda i, c: (jnp.minimum((i + 1) * (tm // 8), lp // 8 - 1), c)),
                           pl.BlockSpec((kk, width), lambda i, c: (0, c))],
                 out_specs=pl.BlockSpec((tm, width), lambda i, c: (i, c)), sem=("parallel", "parallel"))(dy, dy, w)


def _loss_head(h_res, target, row_start, tm=ROW_TILE):
    lp, d = h_res.shape
    t0 = row_start // tm

    def body(h_ref, t_ref, dy_ref, loss_ref):
        i = pl.program_id(0)

        @pl.when(i == 0)
        def _():
            loss_ref[...] = jnp.zeros_like(loss_ref)

        live = (i >= t0).astype(F32)
        err = (h_ref[...] - t_ref[...]) * live
        dy_ref[...] = err * (1.0 / d)
        loss_ref[...] += 0.5 / d * jnp.sum(err * err)

    return _call(body, name="loss_head", grid=(lp // tm,),
                 out_shape=[jax.ShapeDtypeStruct((lp, d), F32), jax.ShapeDtypeStruct((8, LANE), F32)],
                 in_specs=[pl.BlockSpec((tm, d), lambda i: (i, 0)),
                           pl.BlockSpec((tm, d), lambda i: (jnp.maximum(i - t0, 0), 0))],
                 out_specs=[pl.BlockSpec((tm, d), lambda i: (i, 0)), pl.BlockSpec((8, LANE), lambda i: (0, 0))],
                 sem=("arbitrary",))(h_res, target)


def _sum_adamw(parts, w, m, v, name):
    a, r, c = w.shape
    tm = _pick(r, (256, 128, 64, 32, 16))
    bc1 = 1.0 - ADAM_B1 ** ADAM_STEP
    bc2 = 1.0 - ADAM_B2 ** ADAM_STEP

    def body(p_ref, w_ref, m_ref, v_ref, g_ref, d_ref, nm_ref, nv_ref):
        g = p_ref[0, 0].astype(F32)
        for s in range(1, N_DEV):
            g = g + p_ref[s, 0].astype(F32)
        nm = ADAM_B1 * m_ref[0] + (1.0 - ADAM_B1) * g
        nv = ADAM_B2 * v_ref[0] + (1.0 - ADAM_B2) * (g * g)
        g_ref[0] = g
        nm_ref[0] = nm
        nv_ref[0] = nv
        d_ref[0] = -ADAM_LR * ((nm / bc1) / (jnp.sqrt(nv / bc2) + ADAM_EPS) + ADAM_WD * w_ref[0])

    spec = pl.BlockSpec((1, tm, c), lambda l, i: (l, i, 0))
    shp = jax.ShapeDtypeStruct((a, r, c), F32)
    return _call(body, name=name, grid=(a, r // tm), out_shape=[shp] * 4,
                 in_specs=[pl.BlockSpec((N_DEV, 1, tm, c), lambda l, i: (0, l, i, 0)), spec, spec, spec],
                 out_specs=[spec] * 4, sem=("parallel", "parallel"))(parts, w, m, v)


_ANY = pl.BlockSpec(memory_space=pl.ANY)


def _all_gather(block, name):
    def body(x_ref, out_ref, send_sems, recv_sems, local_sem):
        x, y, c = lax.axis_index("x"), lax.axis_index("y"), lax.axis_index("c")
        me, sibling = (x, y, c), (x, y, 1 - c)
        chips = [(1 - x, y), (x, 1 - y), (1 - x, 1 - y)]

        def slot(px, py, pc):
            return out_ref.at[4 * px + 2 * py + pc]

        def copy(k, blk, to, src=None):
            return pltpu.make_async_remote_copy(
                src_ref=slot(*blk) if src is None else src, dst_ref=slot(*blk),
                send_sem=send_sems.at[k], recv_sem=recv_sems.at[k], device_id=to, device_id_type=MESH)

        mine = pltpu.make_async_copy(x_ref, slot(*me), local_sem)
        mine.start()
        first = [copy(0, me, sibling, src=x_ref)]
        first += [copy(1 + j, me, (*chip, c), src=x_ref) for j, chip in enumerate(chips)]
        for cp in first:
            cp.start()
        passed = [copy(4 + j, (*chip, c), sibling) for j, chip in enumerate(chips)]
        for j, chip in enumerate(chips):
            copy(1 + j, (*chip, c), me).wait_recv()
            passed[j].start()
        copy(0, sibling, me).wait_recv()
        for j, chip in enumerate(chips):
            copy(4 + j, (*chip, 1 - c), me).wait_recv()
        for cp in first + passed:
            cp.wait_send()
        mine.wait()

    return pl.pallas_call(
        body, name=name, out_shape=jax.ShapeDtypeStruct((N_DEV,) + block.shape, block.dtype),
        in_specs=[_ANY], out_specs=_ANY,
        scratch_shapes=[pltpu.SemaphoreType.DMA((7,)), pltpu.SemaphoreType.DMA((7,)), pltpu.SemaphoreType.DMA],
    )(block)


def _all_to_all(src, name):
    def body(s_ref, o_ref, send_sems, recv_sems, local_sem):
        x, y, c = lax.axis_index("x"), lax.axis_index("y"), lax.axis_index("c")
        me = 4 * x + 2 * y + c
        mine = pltpu.make_async_copy(s_ref.at[me], o_ref.at[me], local_sem)
        mine.start()
        copies = []
        for k in range(1, N_DEV):
            px = 1 - x if k & 4 else x
            py = 1 - y if k & 2 else y
            pc = 1 - c if k & 1 else c
            peer = 4 * px + 2 * py + pc
            copies.append((pltpu.make_async_remote_copy(
                src_ref=s_ref.at[peer], dst_ref=o_ref.at[me], send_sem=send_sems.at[k - 1],
                recv_sem=recv_sems.at[k - 1], device_id=(px, py, pc), device_id_type=MESH), peer, k))
        for cp, _, _ in copies:
            cp.start()
        for cp, peer, k in copies:
            cp.wait_send()
            pltpu.make_async_remote_copy(
                src_ref=s_ref.at[peer], dst_ref=o_ref.at[peer], send_sem=send_sems.at[k - 1],
                recv_sem=recv_sems.at[k - 1], device_id=(x, y, c), device_id_type=MESH).wait_recv()
        mine.wait()

    return pl.pallas_call(
        body, name=name, out_shape=jax.ShapeDtypeStruct(src.shape, src.dtype), in_specs=[_ANY], out_specs=_ANY,
        scratch_shapes=[pltpu.SemaphoreType.DMA((7,)), pltpu.SemaphoreType.DMA((7,)), pltpu.SemaphoreType.DMA],
    )(src)


_HBM = pl.BlockSpec(memory_space=pltpu.HBM)
_SEM = pl.BlockSpec(memory_space=pltpu.SEMAPHORE)
_EFFECT = pltpu.SideEffectType.DATAFLOW_SIDE_EFFECTING


def _peer_list(x, y, c):
    return [(1 - x if k & 4 else x, 1 - y if k & 2 else y, 1 - c if k & 1 else c) for k in range(1, N_DEV)]


def _exchange_copies(s_refs, l_refs, send_sems, recv_sems, landing_of_peer, same_block):
    x, y, c = lax.axis_index("x"), lax.axis_index("y"), lax.axis_index("c")
    me = 4 * x + 2 * y + c
    out = []
    for wi, (s_ref, l_ref) in enumerate(zip(s_refs, l_refs)):
        for k, (px, py, pc) in enumerate(_peer_list(x, y, c)):
            peer = 4 * px + 2 * py + pc
            idx = wi * (N_DEV - 1) + k
            out.append(pltpu.make_async_remote_copy(
                src_ref=s_ref if same_block else s_ref.at[peer], dst_ref=l_ref.at[peer if landing_of_peer else me],
                send_sem=send_sems.at[idx], recv_sem=recv_sems.at[idx], device_id=(px, py, pc), device_id_type=MESH))
    return out


def _exchange_start(srcs, name, same_block=False):
    nw = len(srcs)
    ncp = nw * (N_DEV - 1)

    def body(*refs):
        for cp in _exchange_copies(refs[:nw], refs[nw:2 * nw], refs[2 * nw], refs[2 * nw + 1], False, same_block):
            cp.start()
        refs[-1][...] = jnp.zeros_like(refs[-1])

    land_shapes = [((N_DEV,) + s.shape) if same_block else s.shape for s in srcs]
    hbm = [pltpu.HBM(s.shape, s.dtype) for s in srcs]
    hbm_l = [pltpu.HBM(ls, s.dtype) for ls, s in zip(land_shapes, srcs)]
    outs = pl.pallas_call(
        body, name=name,
        out_shape=(pltpu.SemaphoreType.DMA((ncp,)), pltpu.SemaphoreType.DMA((ncp,)), *hbm, *hbm_l,
                   jax.ShapeDtypeStruct((8, LANE), F32)),
        in_specs=[_HBM] * (2 * nw), out_specs=(_SEM, _SEM, *[_HBM] * (2 * nw), pl.BlockSpec(memory_space=pltpu.VMEM)),
        input_output_aliases={i: 2 + i for i in range(2 * nw)},
        compiler_params=pltpu.CompilerParams(has_side_effects=_EFFECT),
    )(*[pltpu.with_memory_space_constraint(s, pltpu.HBM) for s in srcs],
      *[pltpu.with_memory_space_constraint(lax.empty(ls, s.dtype), pltpu.HBM) for ls, s in zip(land_shapes, srcs)])
    return outs[0], outs[1], list(outs[2:2 + nw]), list(outs[2 + nw:2 + 2 * nw]), outs[-1]


def _exchange_wait(send_sems, recv_sems, srcs, lands, after, name, same_block=False):
    nw = len(srcs)

    def body(*refs):
        for cp in _exchange_copies(refs[:nw], refs[nw:2 * nw], refs[2 * nw], refs[2 * nw + 1], True, same_block):
            cp.wait_send()
            cp.wait_recv()

    hbm = [pltpu.HBM(a.shape, a.dtype) for a in list(srcs) + list(lands)]
    outs = pl.pallas_call(
        body, name=name, out_shape=tuple(hbm),
        in_specs=[_HBM] * (2 * nw) + [_SEM, _SEM, pl.BlockSpec(memory_space=pl.ANY)], out_specs=tuple([_HBM] * (2 * nw)),
        input_output_aliases={i: i for i in range(2 * nw)},
        compiler_params=pltpu.CompilerParams(has_side_effects=_EFFECT),
    )(*srcs, *lands, send_sems, recv_sems, after)
    return list(outs[:nw]), list(outs[nw:])


def _pack(blocks, width, dtype, row_mult):
    flat = jnp.concatenate([b.astype(dtype).reshape(-1) for b in blocks])
    per = width * row_mult
    total = -(-flat.shape[0] // per) * per
    return jnp.pad(flat, (0, total - flat.shape[0])).reshape(total // width, width)


def _pack_dest(fulls, axes, width, dtype, row_mult):
    rows = []
    for f, ax in zip(fulls, axes):
        f = f.astype(dtype)
        if ax is None:
            rows.append(jnp.broadcast_to(f.reshape(1, -1), (N_DEV, f.size)))
        else:
            shp = f.shape
            f = f.reshape(shp[:ax] + (N_DEV, shp[ax] // N_DEV) + shp[ax + 1:])
            rows.append(jnp.moveaxis(f, ax, 0).reshape(N_DEV, -1))
    flat = jnp.concatenate(rows, axis=1)
    per = width * row_mult
    total = -(-flat.shape[1] // per) * per
    return jnp.pad(flat, ((0, 0), (0, total - flat.shape[1]))).reshape(N_DEV, total // width, width)


def _unpack(packed, shapes):
    flat = packed.reshape(-1)
    out, off = [], 0
    for s in shapes:
        n = 1
        for d in s:
            n *= d
        out.append(flat[off:off + n].reshape(s))
        off += n
    return out


def _unpack_gathered(gathered, shapes, axes):
    flat = gathered.reshape(N_DEV, -1)
    out, off = [], 0
    for s, ax in zip(shapes, axes):
        n = 1
        for d in s:
            n *= d
        blk = jnp.moveaxis(flat[:, off:off + n].reshape((N_DEV,) + tuple(s)), 0, ax)
        out.append(blk.reshape(tuple(s[:ax]) + (N_DEV * s[ax],) + tuple(s[ax + 1:])))
        off += n
    return out


def _shard_cols(blocks, a, b):
    shard = blocks[0].shape[1]
    out = []
    while a < b:
        d = a // shard
        hi = min(b, (d + 1) * shard)
        out.append(blocks[d][:, a - d * shard:hi - d * shard])
        a = hi
    return out


def _permute_w_in(blocks):
    main = jnp.concatenate(_shard_cols(blocks, O_GQ, O_BL) + _shard_cols(blocks, O_GZ, O_END)
                           + _shard_cols(blocks, O_FQ, O_FL), axis=1)
    pad = jnp.zeros((blocks[0].shape[0], LANE - 24), blocks[0].dtype)
    small = jnp.concatenate(_shard_cols(blocks, O_FL, O_GQ) + _shard_cols(blocks, O_BL, O_GZ) + [pad], axis=1)
    return main, small


_W_IN_SEGS = ((O_FQ, O_FL, True, C_FQ), (O_FL, O_GQ, False, 0), (O_GQ, O_BL, True, C_GQ), (O_BL, O_GZ, False, 8),
              (O_GZ, O_END, True, C_GZ))


def _unpermute_cols(main, small, a, b):
    out = []
    for s0, s1, is_main, t0 in _W_IN_SEGS:
        lo, hi = max(a, s0), min(b, s1)
        if lo < hi:
            out.append((main if is_main else small)[:, t0 + lo - s0:t0 + hi - s0])
    return jnp.concatenate(out, axis=1)


def _lanes(vec, start):
    return jnp.pad(vec.astype(F32), (start, LANE - start - vec.shape[0])).reshape(1, LANE)


BIG = ("w_in", "w_branch_a", "w_branch_b", "w_out", "w_up", "w_down")
BIG_AXES = (2, 2, 1, 1, 2, 1)
SHARDED_SMALL = ("meta_tokens", "gdn_conv_w", "ffn_conv_w")
SHARDED_SMALL_AXES = (1, 2, 2)
REPL = ("norm1_g", "fox_f_bias", "fox_q_norm_g", "fox_k_norm_g", "gdn_a_log", "gdn_dt_bias", "gdn_norm_g", "norm2_g")
ORDER = ("meta_tokens", "norm1_g", "w_in", "fox_f_bias", "fox_q_norm_g", "fox_k_norm_g", "gdn_conv_w", "gdn_a_log",
         "gdn_dt_bias", "gdn_norm_g", "w_branch_a", "w_branch_b", "w_out", "norm2_g", "w_up", "ffn_conv_w", "w_down")


def _layer_fwd(h_res, wl, pad_rows, late=None):
    lp = h_res.shape[0]
    sv = {"res_in": h_res}
    (h1,) = _rowwise(_rmsnorm_fn, [(h_res, D, 0)], [wl["norm1_g"]], [(D, BF16)], "rmsnorm1")
    proj = _matmul(h1, wl["w_main"], "nn", F32, name="mm_in")
    small = _matmul(h1, wl["w_small"], "nn", F32, name="mm_in_small")
    sv.update(h1=h1, proj=proj, small=small)
    if late is not None:
        wl = dict(wl, **late(proj))

    fox_fn = functools.partial(_fox_prep_fn, pad_rows)
    qh, kh, logf = _rowwise(fox_fn, [(proj, FOX_W, C_FQ // FOX_W), (proj, FOX_W, C_FK // FOX_W), (small, LANE, 0)],
                            [wl["qg"], wl["kg"], wl["fb"]], [(FOX_W, BF16), (FOX_W, BF16), (LANE, F32)], "fox_prep")
    fsum = _cumsum_rows(logf, False, "fox_cumsum")
    frow = fsum[:, :8].T.reshape(8, 1, lp)
    o_a, lse = _fox_fwd(qh, kh, proj, fsum, frow, pad_rows, C_FV // LANE)
    y_a = _matmul(o_a, wl["w_branch_a"], "nn", F32, name="mm_branch_a")
    sv.update(qh=qh, kh=kh, fsum=fsum, frow=frow, o_a=o_a, lse=lse)

    gq, gk, gv, bg = _gdn_act(proj, small, wl["gdn_conv_w"], wl["alog"], wl["dtb"])
    grow = bg[:, 16:24].T.reshape(8, lp // CHUNK, 1, CHUNK)
    o_raw, sprev, tinv = _gdn_fwd(gq, gk, gv, bg, grow)
    (o_b,) = _rowwise(_gdn_post_fn, [(o_raw, GDN_W, 0), (proj, GDN_W, C_GZ // GDN_W)], [wl["gn"]], [(GDN_W, BF16)],
                      "gdn_post")
    y_b = _matmul(o_b, wl["w_branch_b"], "nn", F32, name="mm_branch_b")
    sv.update(gq=gq, gk=gk, gv=gv, bg=bg, grow=grow, o_raw=o_raw, sprev=sprev, tinv=tinv, o_b=o_b)

    (mixed,) = _rowwise(_merge_fn, [(proj, D, C_GATE // D), (proj, D, C_GATE // D + 1), (y_a, D, 0), (y_b, D, 0)], [],
                        [(D, BF16)], "merge")
    res_mid = _matmul(mixed, wl["w_out"], "nn", F32, add=h_res, name="mm_out")
    sv.update(y_a=y_a, y_b=y_b, mixed=mixed, res_mid=res_mid)

    (h2,) = _rowwise(_rmsnorm_fn, [(res_mid, D, 0)], [wl["norm2_g"]], [(D, BF16)], "rmsnorm2")
    up_pre = _matmul(h2, wl["w_up"], "nn", F32, name="mm_up")
    act = _ffn_act(up_pre, wl["ffn_conv_w"])
    out = _matmul(act, wl["w_down"], "nn", F32, add=res_mid, name="mm_down")
    sv.update(h2=h2, up_pre=up_pre, act=act)
    return out, sv, wl


def _layer_bwd(dres, wl, sv, pad_rows, after_ffn=None):
    lp = dres.shape[0]
    gw = {}
    gw["w_down"] = _matmul(sv["act"], dres, "tn", F32, name="mm_dw_down")
    dact = _matmul(dres, wl["w_down"], "nt", BF16, name="mm_dact")
    dup, gw["ffn_conv_w"] = _ffn_act_bwd(sv["up_pre"], wl["ffn_conv_w"], dact)
    dup_pre = _conv_bwd_x(dup, wl["ffn_conv_w"], pad_rows, D_FF, "ffn_conv_bwd")
    gw["w_up"] = _matmul(sv["h2"], dup_pre, "tn", F32, name="mm_dw_up")
    dh2 = _matmul(dup_pre, wl["w_up"], "nt", F32, name="mm_dh2")
    (dmid,), (gw["norm2_g"],) = _rowwise_bwd(_rmsnorm_fn, [(sv["res_mid"], D, 0)], [wl["norm2_g"]], [(dh2, D, 0)],
                                             "rmsnorm2_bwd", pad_rows, [F32], adds=[(dres, D, 0)])
    if after_ffn is not None:
        w_out, _ = lax.optimization_barrier((wl["w_out"], after_ffn(gw)))
        wl = dict(wl, w_out=w_out)
    gw["w_out"] = _matmul(sv["mixed"], dmid, "tn", F32, name="mm_dw_out")
    dmixed = _matmul(dmid, wl["w_out"], "nt", F32, name="mm_dmixed")
    proj, small = sv["proj"], sv["small"]
    (dg0, dg1, dya, dyb), _ = _rowwise_bwd(
        _merge_fn, [(proj, D, C_GATE // D), (proj, D, C_GATE // D + 1), (sv["y_a"], D, 0), (sv["y_b"], D, 0)], [],
        [(dmixed, D, 0)], "merge_bwd", pad_rows, [BF16, BF16, BF16, BF16])
    gw["w_branch_a"] = _matmul(sv["o_a"], dya, "tn", F32, name="mm_dw_a")
    do_a = (_matmul(dya, wl["w_branch_a"], "nt", F32, name="mm_do_a") * LN2).astype(BF16)
    gw["w_branch_b"] = _matmul(sv["o_b"], dyb, "tn", F32, name="mm_dw_b")
    do_b = _matmul(dyb, wl["w_branch_b"], "nt", F32, name="mm_do_b")

    (do_raw, dgz), (gw["gn"],) = _rowwise_bwd(_gdn_post_fn, [(sv["o_raw"], GDN_W, 0), (proj, GDN_W, C_GZ // GDN_W)],
                                              [wl["gn"]], [(do_b, GDN_W, 0)], "gdn_post_bwd", pad_rows, [F32, BF16])
    dgq, dgk, dgv, dbg, dgrow = _gdn_bwd(sv["gq"], sv["gk"], sv["gv"], sv["bg"], sv["grow"], sv["sprev"],
                                         sv["tinv"], do_raw)
    dbg = dbg + jnp.pad(dgrow.reshape(8, lp).T, ((0, 0), (16, LANE - 24)))
    dconv, dsmall_g, gw["alog"], gw["dtb"], gw["gdn_conv_w"] = _gdn_act_bwd(
        proj, small, wl["gdn_conv_w"], wl["alog"], wl["dtb"], dgq, dgk, dgv, dbg)
    dqkv = _conv_bwd_x(dconv, wl["gdn_conv_w"], pad_rows, GDN_W, "gdn_conv_bwd")

    (delta,) = _rowwise(_fox_delta_fn, [(sv["o_a"], FOX_W, 0), (do_a, FOX_W, 0)], [], [(LANE, F32)], "fox_delta")
    dqh, dkh, dvh, dfq, dfk = _fox_bwd(sv["qh"], sv["kh"], proj, sv["fsum"], sv["frow"], do_a, sv["lse"], delta,
                                       pad_rows, C_FV // LANE)
    df8 = dfq.reshape(lp, FOX_W // LANE, LANE)[:, :, :2].reshape(lp, 8) + dfk.reshape(8, lp).T
    dlogf = _cumsum_rows(jnp.pad(df8, ((0, 0), (0, LANE - 8))), True, "fox_cumsum_bwd")
    fox_fn = functools.partial(_fox_prep_fn, pad_rows)
    (dfq_p, dfk_p, dsmall_f), (gw["qg"], gw["kg"], gw["fb"]) = _rowwise_bwd(
        fox_fn, [(proj, FOX_W, C_FQ // FOX_W), (proj, FOX_W, C_FK // FOX_W), (small, LANE, 0)],
        [wl["qg"], wl["kg"], wl["fb"]], [(dqh, FOX_W, 0), (dkh, FOX_W, 0), (dlogf, LANE, 0)],
        "fox_prep_bwd", pad_rows, [BF16, BF16, F32], adds=[None, None, (dsmall_g, LANE, 0)])

    dproj = jnp.concatenate([dqkv, dgz, dg0, dg1, dfq_p, dfk_p, dvh], axis=1)
    gw["w_main"] = _matmul(sv["h1"], dproj, "tn", F32, name="mm_dw_main")
    gw["w_small"] = _matmul(sv["h1"], dsmall_f, "tn", F32, name="mm_dw_small")
    dh1 = _matmul(dproj, wl["w_main"], "nt", F32, name="mm_dh1")
    dh1 = _matmul(dsmall_f, wl["w_small"], "nt", F32, add=dh1, name="mm_dh1_small")
    (din,), (gw["norm1_g"],) = _rowwise_bwd(_rmsnorm_fn, [(sv["res_in"], D, 0)], [wl["norm1_g"]], [(dh1, D, 0)],
                                            "rmsnorm1_bwd", pad_rows, [F32], adds=[(dmid, D, 0)])
    return din, gw


def kernel(x, meta_tokens, norm1_g, w_in, fox_f_bias, fox_q_norm_g, fox_k_norm_g, gdn_conv_w, gdn_a_log, gdn_dt_bias, gdn_norm_g, w_branch_a, w_branch_b, w_out, norm2_g, w_up, ffn_conv_w, w_down, loss_target, m_meta_tokens, m_norm1_g, m_w_in, m_fox_f_bias, m_fox_q_norm_g, m_fox_k_norm_g, m_gdn_conv_w, m_gdn_a_log, m_gdn_dt_bias, m_gdn_norm_g, m_w_branch_a, m_w_branch_b, m_w_out, m_norm2_g, m_w_up, m_ffn_conv_w, m_w_down, v_meta_tokens, v_norm1_g, v_w_in, v_fox_f_bias, v_fox_q_norm_g, v_fox_k_norm_g, v_gdn_conv_w, v_gdn_a_log, v_gdn_dt_bias, v_gdn_norm_g, v_w_branch_a, v_w_branch_b, v_w_out, v_norm2_g, v_w_up, v_ffn_conv_w, v_w_down):
    w = dict(meta_tokens=meta_tokens, norm1_g=norm1_g, w_in=w_in, fox_f_bias=fox_f_bias, fox_q_norm_g=fox_q_norm_g,
             fox_k_norm_g=fox_k_norm_g, gdn_conv_w=gdn_conv_w, gdn_a_log=gdn_a_log, gdn_dt_bias=gdn_dt_bias,
             gdn_norm_g=gdn_norm_g, w_branch_a=w_branch_a, w_branch_b=w_branch_b, w_out=w_out, norm2_g=norm2_g,
             w_up=w_up, ffn_conv_w=ffn_conv_w, w_down=w_down)
    mom = dict(meta_tokens=m_meta_tokens, norm1_g=m_norm1_g, w_in=m_w_in, fox_f_bias=m_fox_f_bias,
               fox_q_norm_g=m_fox_q_norm_g, fox_k_norm_g=m_fox_k_norm_g, gdn_conv_w=m_gdn_conv_w,
               gdn_a_log=m_gdn_a_log, gdn_dt_bias=m_gdn_dt_bias, gdn_norm_g=m_gdn_norm_g, w_branch_a=m_w_branch_a,
               w_branch_b=m_w_branch_b, w_out=m_w_out, norm2_g=m_norm2_g, w_up=m_w_up, ffn_conv_w=m_ffn_conv_w,
               w_down=m_w_down)
    var = dict(meta_tokens=v_meta_tokens, norm1_g=v_norm1_g, w_in=v_w_in, fox_f_bias=v_fox_f_bias,
               fox_q_norm_g=v_fox_q_norm_g, fox_k_norm_g=v_fox_k_norm_g, gdn_conv_w=v_gdn_conv_w,
               gdn_a_log=v_gdn_a_log, gdn_dt_bias=v_gdn_dt_bias, gdn_norm_g=v_gdn_norm_g, w_branch_a=v_w_branch_a,
               w_branch_b=v_w_branch_b, w_out=v_w_out, norm2_g=v_norm2_g, w_up=v_w_up, ffn_conv_w=v_ffn_conv_w,
               w_down=v_w_down)
    depth = norm1_g.shape[0]
    seq = x.shape[1]
    l_tok = N_META + seq
    lp = -(-l_tok // LANE) * LANE
    pad_rows = lp - l_tok
    row_start = pad_rows + N_META

    me = 4 * lax.axis_index("x") + 2 * lax.axis_index("y") + lax.axis_index("c")
    rest = tuple(n for n in BIG if n != "w_in")
    got_in = _all_gather(w["w_in"][0].astype(BF16), "gather_w_in")
    (rest0, later), got_in = lax.optimization_barrier(
        (([w[n][0].astype(BF16) for n in rest], [[w[n][l].astype(BF16) for n in BIG] for l in range(1, depth)]), got_in))
    started_0 = _exchange_start(rest0, "gather_start_0", same_block=True)
    later, _ = lax.optimization_barrier((later, started_0[4]))
    started_g = [_exchange_start(later[l - 1], "gather_start_%d" % l, same_block=True) for l in range(1, depth)]
    small_shapes = [w[n].shape for n in SHARDED_SMALL]
    gathered_s = _all_gather(_pack([w[n] for n in SHARDED_SMALL], LANE, F32, 8), "gather_small")
    full = dict(zip(SHARDED_SMALL, _unpack_gathered(gathered_s, small_shapes, SHARDED_SMALL_AXES)))

    def joined(blocks):
        out = {}
        for name, blk in blocks.items():
            if name == "w_in":
                out["w_main"], out["w_small"] = _permute_w_in([blk[d] for d in range(N_DEV)])
            else:
                out[name] = jnp.concatenate([blk[d] for d in range(N_DEV)], axis=BIG_AXES[BIG.index(name)] - 1)
        return out

    def layer_consts(l):
        return dict(
            gdn_conv_w=full["gdn_conv_w"][l], ffn_conv_w=full["ffn_conv_w"][l],
            norm1_g=norm1_g[l].reshape(1, D), norm2_g=norm2_g[l].reshape(1, D),
            qg=jnp.tile(fox_q_norm_g[l], 8).reshape(1, FOX_W), kg=jnp.tile(fox_k_norm_g[l], 8).reshape(1, FOX_W),
            fb=_lanes(fox_f_bias[l], 0), alog=_lanes(gdn_a_log[l], 16), dtb=_lanes(gdn_dt_bias[l], 16),
            gn=jnp.tile(gdn_norm_g[l], 8).reshape(1, GDN_W))

    def arrived(exchange, names, after, name):
        send_sems, recv_sems, srcs, lands, _ = exchange
        srcs, lands = _exchange_wait(send_sems, recv_sems, srcs, lands, after, name, same_block=True)
        return {n: lax.dynamic_update_index_in_dim(ld, sr, me, 0) for n, sr, ld in zip(names, srcs, lands)}

    h_res = jnp.concatenate([jnp.zeros((pad_rows, D), F32), full["meta_tokens"], x[0]], axis=0)
    h_res, _ = lax.optimization_barrier((h_res, [st[4] for st in [started_0] + started_g]))
    layers, saved = [], []
    for l in range(depth):
        if l == 0:
            wl = dict(joined({"w_in": got_in}), **layer_consts(0))
            late = lambda proj: joined(arrived(started_0, rest, proj, "gather_wait_0"))
        else:
            wl = dict(joined(arrived(started_g[l - 1], BIG, h_res, "gather_wait_%d" % l)), **layer_consts(l))
            late = None
        h_res, sv, wl = _layer_fwd(h_res, wl, pad_rows, late)
        layers.append(wl)
        saved.append(sv)
    dres, loss_part = _loss_head(h_res, loss_target[0], row_start)
    loss = lax.psum(loss_part[0, 0], ("x", "y", "c"))

    def dest_block(name, g, d):
        if name == "w_in":
            s = w_in.shape[2]
            return _unpermute_cols(g["w_main"], g["w_small"], s * d, s * (d + 1)).astype(BF16)
        if BIG_AXES[BIG.index(name)] == 2:
            s = w[name].shape[2]
            return g[name][:, s * d:s * (d + 1)].astype(BF16)
        s = w[name].shape[1]
        return g[name][s * d:s * (d + 1), :].astype(BF16)

    def start(names, g, name):
        return names, _exchange_start([jnp.stack([dest_block(n, g, d) for d in range(N_DEV)]) for n in names], name)

    ffn_names = ("w_up", "w_down")
    gws = [None] * depth
    started = []

    def early(g):
        started.append((0,) + start(ffn_names, g, "scatter_start_0_ffn"))
        return started[-1][2][4]

    for l in reversed(range(depth)):
        dres, gws[l] = _layer_bwd(dres, layers[l], saved[l], pad_rows, early if l == 0 else None)
        started.append((l,) + start(tuple(n for n in BIG if l > 0 or n not in ffn_names), gws[l],
                                    "scatter_start_%d" % l))
        if l > 0:
            w_down_next, _ = lax.optimization_barrier((layers[l - 1]["w_down"], started[-1][2][4]))
            layers[l - 1] = dict(layers[l - 1], w_down=w_down_next)
    landed = [dict() for _ in range(depth)]
    for idx, (l, names, (send_sems, recv_sems, srcs, lands, _)) in enumerate(started):
        srcs, lands = _exchange_wait(send_sems, recv_sems, srcs, lands, dres, "scatter_wait_%d" % idx)
        for n, sr, ld in zip(names, srcs, lands):
            own = lax.dynamic_index_in_dim(sr, me, 0, keepdims=False)
            landed[l][n] = lax.dynamic_update_index_in_dim(ld, own, me, 0)
    grad_x = dres[row_start:].reshape(x.shape)

    def stack(fn):
        return jnp.stack([fn(g) for g in gws])

    part = dict(
        meta_tokens=dres[pad_rows:row_start],
        norm1_g=stack(lambda g: g["norm1_g"][0]), norm2_g=stack(lambda g: g["norm2_g"][0]),
        fox_f_bias=stack(lambda g: g["fb"][0, 0:8]),
        fox_q_norm_g=stack(lambda g: g["qg"].reshape(8, FOX_DH).sum(0)),
        fox_k_norm_g=stack(lambda g: g["kg"].reshape(8, FOX_DH).sum(0)),
        gdn_conv_w=stack(lambda g: g["gdn_conv_w"]), gdn_a_log=stack(lambda g: g["alog"][0, 16:24]),
        gdn_dt_bias=stack(lambda g: g["dtb"][0, 16:24]),
        gdn_norm_g=stack(lambda g: g["gn"].reshape(8, GDN_DH).sum(0)),
        ffn_conv_w=stack(lambda g: g["ffn_conv_w"]))

    res = {}
    for n in BIG:
        parts = jnp.stack([landed[l][n] for l in range(depth)], axis=1)
        res[n] = _sum_adamw(parts, w[n], mom[n], var[n], "adamw_" + n)

    small_names = SHARDED_SMALL + REPL
    small_axes = SHARDED_SMALL_AXES + (None,) * len(REPL)
    landed_s = _all_to_all(_pack_dest([part[n] for n in small_names], small_axes, LANE, F32, 8), "scatter_small")
    shapes_s = [w[n].shape for n in small_names]
    outs = _sum_adamw(landed_s[:, None], *[_pack([d[n] for n in small_names], LANE, F32, 8)[None] for d in (w, mom, var)],
                      "adamw_small")
    for o_idx, packed in enumerate(outs):
        for n, a in zip(small_names, _unpack(packed[0], shapes_s)):
            res.setdefault(n, [None] * 4)[o_idx] = a

    return (loss, grad_x, *[res[n][0] for n in ORDER], *[res[n][1] for n in ORDER],
            *[res[n][2] for n in ORDER], *[res[n][3] for n in ORDER])
```
